```python
import jax, jax.numpy as jnp
from jax import lax
import numpy as np

D_MODEL = 1024
BATCH = 8
SEQ = 2048
DEPTH = 1
DEC_BATCH = 128
DEC_SEQ = 1
PAST_LEN = 8192
PAGE_SIZE = 128

HEAD_DIM = 64
D_ATTN = D_MODEL // 2
N_HEADS = D_ATTN // HEAD_DIM
N_KV_HEADS = N_HEADS // 4
GROUP = N_HEADS // N_KV_HEADS
WINDOW = 128
ATTN_BLOCK = 128
ROT_DIM = HEAD_DIM // 4
ROPE_THETA = 500000.0
D_RNN = D_MODEL - D_ATTN
N_RNN_BLOCKS = 8
RNN_BLOCK_W = D_RNN // N_RNN_BLOCKS
CONV_WIDTH = 4
LRU_C = 8.0
D_MIX = D_ATTN + D_RNN
KV_W = N_KV_HEADS * HEAD_DIM
D_IN = D_ATTN + 2 * KV_W + 2 * D_RNN
N_EXPERTS = 32
TOP_K = 4
D_FF_EXPERT = D_MODEL
SWIGLU_LIMIT = 7.0
SWIGLU_ALPHA = 1.702
MOE_BLOCK = 128
EPS = 1e-6

kernel_name = "hymba_swa_sink_rglru_moe_step"


def rmsnorm(x, g):
    xf = x.astype(jnp.float32)
    y = xf * lax.rsqrt(jnp.mean(xf * xf, axis=-1, keepdims=True) + EPS)
    return (y * g.astype(jnp.float32)).astype(x.dtype)


def partial_rope(x, pos):
    inv = ROPE_THETA ** (-jnp.arange(0, ROT_DIM, 2, dtype=jnp.float32) / ROT_DIM)
    ang = pos.astype(jnp.float32)[:, None] * inv[None, :]
    cos = jnp.cos(ang)[None, :, None, :]
    sin = jnp.sin(ang)[None, :, None, :]
    xr = x[..., :ROT_DIM].astype(jnp.float32)
    x1, x2 = xr[..., :ROT_DIM // 2], xr[..., ROT_DIM // 2:]
    rot = jnp.concatenate([x1 * cos - x2 * sin, x2 * cos + x1 * sin], axis=-1)
    return jnp.concatenate([rot.astype(x.dtype), x[..., ROT_DIM:]], axis=-1)


def window_attention(q, k_ext, v_ext, sinks, pos0):
    B, S = q.shape[0], q.shape[1]
    qb = ATTN_BLOCK if S % ATTN_BLOCK == 0 else S
    nb = S // qb
    kspan = qb + WINDOW
    qr = q.reshape(B, nb, qb, N_KV_HEADS, GROUP, HEAD_DIM)
    kidx = jnp.arange(nb)[:, None] * qb + jnp.arange(kspan)[None, :]
    kb = k_ext[:, kidx]
    vb = v_ext[:, kidx]
    s = jnp.einsum('bnqkgd,bnskd->bnkgqs', qr, kb).astype(jnp.float32) * (HEAD_DIM ** -0.5)
    rel = jnp.arange(qb)[:, None] + WINDOW - jnp.arange(kspan)[None, :]
    band = (rel >= 0) & (rel <= WINDOW)
    valid = (pos0 - WINDOW + kidx) >= 0
    mask = band[None, :, :] & valid[:, None, :]
    s = jnp.where(mask[None, :, None, None, :, :], s, -jnp.inf)
    sink = jnp.broadcast_to(sinks.astype(jnp.float32).reshape(N_KV_HEADS, GROUP)[None, None, :, :, None, None],
                            s.shape[:-1] + (1,))
    p = jax.nn.softmax(jnp.concatenate([s, sink], axis=-1), axis=-1)[..., :-1]
    o = jnp.einsum('bnkgqs,bnskd->bnqkgd', p.astype(vb.dtype), vb)
    return o.reshape(B, S, N_HEADS * HEAD_DIM)


def causal_conv(ext, w, b):
    out = lax.conv_general_dilated(ext, w[:, None, :], window_strides=(1,), padding='VALID',
                                   dimension_numbers=('NWC', 'WIO', 'NWC'),
                                   feature_group_count=ext.shape[-1])
    return out + b


def _lin_combine(e1, e2):
    a1, b1 = e1
    a2, b2 = e2
    return a1 * a2, a2 * b1 + b2


def rglru(xc, h0, w_a, b_a, w_x, b_x, lam):
    B, S, C = xc.shape
    xb = xc.reshape(B, S, N_RNN_BLOCKS, RNN_BLOCK_W)
    r = jax.nn.sigmoid(jnp.einsum('bsni,nio->bsno', xb, w_a).reshape(B, S, C).astype(jnp.float32)
                       + b_a.reshape(C).astype(jnp.float32))
    i = jax.nn.sigmoid(jnp.einsum('bsni,nio->bsno', xb, w_x).reshape(B, S, C).astype(jnp.float32)
                       + b_x.reshape(C).astype(jnp.float32))
    log_a = -LRU_C * r * jax.nn.softplus(-lam.astype(jnp.float32))
    a = jnp.exp(log_a)
    u = jnp.sqrt(-jnp.expm1(2.0 * log_a)) * (i * xc.astype(jnp.float32))
    u = u.at[:, 0].add(a[:, 0] * h0.astype(jnp.float32))
    _, h = lax.associative_scan(_lin_combine, (a, u), axis=1)
    return h.astype(xc.dtype), h[:, -1].astype(xc.dtype)


def moe(x2d, w_router, b_router, w_gu, b_gu, w_dn, b_dn):
    T, D = x2d.shape
    logits = jnp.einsum('td,de->te', x2d, w_router).astype(jnp.float32) + b_router.astype(jnp.float32)
    top_vals, top_idx = lax.top_k(logits, TOP_K)
    gates = jax.nn.softmax(top_vals, axis=-1).astype(x2d.dtype)
    n_slots = T * TOP_K
    flat_e = top_idx.reshape(-1).astype(jnp.int32)
    flat_t = (jnp.arange(n_slots, dtype=jnp.int32) // TOP_K)
    flat_g = gates.reshape(-1)
    order = jnp.argsort(flat_e)
    sorted_e = flat_e[order]
    counts = jnp.bincount(flat_e, length=N_EXPERTS).astype(jnp.int32)
    padded = (counts + MOE_BLOCK - 1) // MOE_BLOCK * MOE_BLOCK
    pad_end = jnp.cumsum(padded)
    pad_start = pad_end - padded
    start = jnp.cumsum(counts) - counts
    dest = pad_start[sorted_e] + (jnp.arange(n_slots, dtype=jnp.int32) - start[sorted_e])
    n_blocks = -(-n_slots // MOE_BLOCK) + N_EXPERTS
    n_rows = n_blocks * MOE_BLOCK
    slot_tok = jnp.full((n_rows,), T, jnp.int32).at[dest].set(flat_t[order])
    slot_gate = jnp.zeros((n_rows,), x2d.dtype).at[dest].set(flat_g[order])
    block_e = jnp.minimum(jnp.searchsorted(pad_end, jnp.arange(n_blocks) * MOE_BLOCK, side='right'),
                          N_EXPERTS - 1).astype(jnp.int32)
    x_pad = jnp.concatenate([x2d, jnp.zeros((1, D), x2d.dtype)], axis=0)
    xs = x_pad[slot_tok].reshape(n_blocks, MOE_BLOCK, D)

    def expert_block(args):
        xb, e = args
        h = xb @ w_gu[e] + b_gu[e]
        g = jnp.minimum(h[:, 0::2], SWIGLU_LIMIT)
        up = jnp.clip(h[:, 1::2], -SWIGLU_LIMIT, SWIGLU_LIMIT)
        act = (up + 1.0) * (g * jax.nn.sigmoid(SWIGLU_ALPHA * g))
        return act @ w_dn[e] + b_dn[e]

    ys = lax.map(expert_block, (xs, block_e)).reshape(n_rows, D)
    out = jnp.zeros((T + 1, D), x2d.dtype).at[slot_tok].add(ys * slot_gate[:, None])
    return out[:T]


def block(x, k_buf, v_buf, conv_buf, h0, pos0,
          g_mix_norm, w_in, g_q_norm, g_k_norm, attn_sinks, conv_w, conv_b,
          w_lru_a, b_lru_a, w_lru_x, b_lru_x, lru_lambda, g_attn_out, g_rnn_out, w_out,
          g_ffn_norm, w_router, b_router, w_gate_up, b_gate_up, w_down, b_down):
    B, S, D = x.shape
    h = rmsnorm(x, g_mix_norm)
    proj = jnp.einsum('bsd,de->bse', h, w_in)
    q, k, v, xr, yr = jnp.split(proj, [D_ATTN, D_ATTN + KV_W, D_ATTN + 2 * KV_W,
                                       D_ATTN + 2 * KV_W + D_RNN], axis=-1)
    pos = pos0 + jnp.arange(S)
    q = partial_rope(rmsnorm(q.reshape(B, S, N_HEADS, HEAD_DIM), g_q_norm), pos)
    k = partial_rope(rmsnorm(k.reshape(B, S, N_KV_HEADS, HEAD_DIM), g_k_norm), pos)
    v = v.reshape(B, S, N_KV_HEADS, HEAD_DIM)
    k_ext = jnp.concatenate([k_buf, k], axis=1)
    v_ext = jnp.concatenate([v_buf, v], axis=1)
    attn = window_attention(q, k_ext, v_ext, attn_sinks, pos0)
    conv_ext = jnp.concatenate([conv_buf, xr], axis=1)
    xc = causal_conv(conv_ext, conv_w, conv_b)
    hseq, h_last = rglru(xc, h0, w_lru_a, b_lru_a, w_lru_x, b_lru_x, lru_lambda)
    rnn = jax.nn.gelu(yr) * hseq
    mixed = jnp.concatenate([rmsnorm(attn, g_attn_out), rmsnorm(rnn, g_rnn_out)], axis=-1)
    x = x + jnp.einsum('bse,ed->bsd', mixed, w_out)
    f = moe(rmsnorm(x, g_ffn_norm).reshape(B * S, D), w_router, b_router,
            w_gate_up, b_gate_up, w_down, b_down).reshape(B, S, D)
    x = x + f
    return (x, k_ext[:, -WINDOW:], v_ext[:, -WINDOW:],
            conv_ext[:, -(CONV_WIDTH - 1):], h_last)


def setup_inputs(seed: int = 0) -> dict:
    key = jax.random.key(seed)
    ks = jax.random.split(key, 32)
    f32 = jnp.float32
    nrm = lambda k, shape, scale: jax.random.normal(k, shape, f32) * scale
    a8 = jax.random.uniform(ks[17], (DEPTH, D_RNN), f32, minval=0.9, maxval=0.999)
    s = a8 ** (1.0 / LRU_C)
    lru_lambda = jnp.log(s) - jnp.log1p(-s)
    return {
        "x_prompt": nrm(ks[0], (BATCH, SEQ, D_MODEL), 1.0),
        "x_sample": nrm(ks[1], (DEC_BATCH, DEC_SEQ, D_MODEL), 1.0),
        "cache_k": nrm(ks[2], (DEPTH, DEC_BATCH, WINDOW, N_KV_HEADS, HEAD_DIM), 1.0),
        "cache_v": nrm(ks[3], (DEPTH, DEC_BATCH, WINDOW, N_KV_HEADS, HEAD_DIM), 1.0),
        "state_conv": nrm(ks[4], (DEPTH, DEC_BATCH, CONV_WIDTH - 1, D_RNN), 1.0),
        "state_h": nrm(ks[5], (DEPTH, DEC_BATCH, D_RNN), 0.5),
        "g_mix_norm": 1.0 + nrm(ks[6], (DEPTH, D_MODEL), 0.02),
        "w_in": nrm(ks[7], (DEPTH, D_MODEL, D_IN), D_MODEL ** -0.5),
        "g_q_norm": 1.0 + nrm(ks[8], (DEPTH, HEAD_DIM), 0.02),
        "g_k_norm": 1.0 + nrm(ks[9], (DEPTH, HEAD_DIM), 0.02),
        "attn_sinks": nrm(ks[10], (DEPTH, N_HEADS), 0.5),
        "conv_w": nrm(ks[11], (DEPTH, CONV_WIDTH, D_RNN), CONV_WIDTH ** -0.5),
        "conv_b": nrm(ks[12], (DEPTH, D_RNN), 0.01),
        "w_lru_a": nrm(ks[13], (DEPTH, N_RNN_BLOCKS, RNN_BLOCK_W, RNN_BLOCK_W), RNN_BLOCK_W ** -0.5),
        "b_lru_a": nrm(ks[14], (DEPTH, N_RNN_BLOCKS, RNN_BLOCK_W), 0.01),
        "w_lru_x": nrm(ks[15], (DEPTH, N_RNN_BLOCKS, RNN_BLOCK_W, RNN_BLOCK_W), RNN_BLOCK_W ** -0.5),
        "b_lru_x": nrm(ks[16], (DEPTH, N_RNN_BLOCKS, RNN_BLOCK_W), 0.01),
        "lru_lambda": lru_lambda,
        "g_attn_out": 1.0 + nrm(ks[18], (DEPTH, D_ATTN), 0.02),
        "g_rnn_out": 1.0 + nrm(ks[19], (DEPTH, D_RNN), 0.02),
        "w_out": nrm(ks[20], (DEPTH, D_MIX, D_MODEL), D_MIX ** -0.5),
        "g_ffn_norm": 1.0 + nrm(ks[21], (DEPTH, D_MODEL), 0.02),
        "w_router": nrm(ks[22], (DEPTH, D_MODEL, N_EXPERTS), D_MODEL ** -0.5),
        "b_router": nrm(ks[23], (DEPTH, N_EXPERTS), 0.01),
        "w_gate_up": nrm(ks[24], (DEPTH, N_EXPERTS, D_MODEL, 2 * D_FF_EXPERT), D_MODEL ** -0.5),
        "b_gate_up": nrm(ks[25], (DEPTH, N_EXPERTS, 2 * D_FF_EXPERT), 0.01),
        "w_down": nrm(ks[26], (DEPTH, N_EXPERTS, D_FF_EXPERT, D_MODEL), D_FF_EXPERT ** -0.5),
        "b_down": nrm(ks[27], (DEPTH, N_EXPERTS, D_MODEL), 0.01),
    }


def reference(x_prompt, x_sample, cache_k, cache_v, state_conv, state_h,
              g_mix_norm, w_in, g_q_norm, g_k_norm, attn_sinks, conv_w, conv_b,
              w_lru_a, b_lru_a, w_lru_x, b_lru_x, lru_lambda, g_attn_out, g_rnn_out, w_out,
              g_ffn_norm, w_router, b_router, w_gate_up, b_gate_up, w_down, b_down):
    B = x_prompt.shape[0]
    dt = x_prompt.dtype
    k0 = jnp.zeros((B, WINDOW, N_KV_HEADS, HEAD_DIM), dt)
    c0 = jnp.zeros((B, CONV_WIDTH - 1, D_RNN), dt)
    h0 = jnp.zeros((B, D_RNN), dt)
    yp, ys = x_prompt, x_sample
    kp_l, vp_l, cp_l, hp_l, ks_l, vs_l, cs_l, hs_l = [], [], [], [], [], [], [], []
    for l in range(DEPTH):
        lp = (g_mix_norm[l], w_in[l], g_q_norm[l], g_k_norm[l], attn_sinks[l], conv_w[l], conv_b[l],
              w_lru_a[l], b_lru_a[l], w_lru_x[l], b_lru_x[l], lru_lambda[l], g_attn_out[l], g_rnn_out[l],
              w_out[l], g_ffn_norm[l], w_router[l], b_router[l], w_gate_up[l], b_gate_up[l],
              w_down[l], b_down[l])
        yp, kp, vp, cp, hp = block(yp, k0, k0, c0, h0, 0, *lp)
        ys, kn, vn, cn, hn = block(ys, cache_k[l], cache_v[l], state_conv[l], state_h[l], PAST_LEN, *lp)
        kp_l.append(kp); vp_l.append(vp); cp_l.append(cp); hp_l.append(hp)
        ks_l.append(kn); vs_l.append(vn); cs_l.append(cn); hs_l.append(hn)
    return (yp, ys,
            jnp.stack(kp_l), jnp.stack(vp_l), jnp.stack(cp_l), jnp.stack(hp_l),
            jnp.stack(ks_l), jnp.stack(vs_l), jnp.stack(cs_l), jnp.stack(hs_l))
```

```python
import functools

import jax
import jax.numpy as jnp
from jax import lax
from jax.experimental import pallas as pl
from jax.experimental.pallas import tpu as pltpu

F32 = jnp.float32
BF16 = jnp.bfloat16
I32 = jnp.int32

D_MODEL = 1024
HEAD_DIM = 64
N_HEADS = 8
N_KV_HEADS = 2
GROUP = 4
WINDOW = 128
ATTN_BLOCK = 128
ROT_DIM = 16
ROPE_THETA = 500000.0
D_ATTN = 512
D_RNN = 512
KV_W = 128
D_IN = 1792
CONV_WIDTH = 4
LRU_C = 8.0
N_EXPERTS = 32
TOP_K = 4
D_FF = 1024
SWIGLU_LIMIT = 7.0
SWIGLU_ALPHA = 1.702
EPS = 1e-6
PAST_LEN = 8192

LANES = 128
SUBLANES = 8
NEG_BIG = -1e30
VMEM_LIMIT = 56 * 1024 * 1024

MOE_BLOCK_ROWS = 512
PERM_COLS = 256
DISPATCH_CHUNK = 256


def _cparams(n_axes):
    return pltpu.CompilerParams(dimension_semantics=("arbitrary",) * n_axes,
                                vmem_limit_bytes=VMEM_LIMIT)


def _rmsnorm(x, g):
    ms = jnp.mean(x * x, axis=-1, keepdims=True)
    return (x * lax.rsqrt(ms + EPS)) * g


def _in_proj_kernel(x_ref, g_ref, w_ref, gq_ref, gk_ref, c_ref, s1_ref, s2_ref,
                    q_ref, k_ref, v_ref, xr_ref, yr_ref):
    tm = x_ref.shape[0]
    h = _rmsnorm(x_ref[...], g_ref[...])
    proj = jnp.dot(h.astype(BF16), w_ref[...], preferred_element_type=F32)
    lo = lax.broadcasted_iota(I32, (tm, LANES), 1) < HEAD_DIM
    c = c_ref[...]
    s1 = s1_ref[...]
    s2 = s2_ref[...]

    def head_norm_rope(t, g):
        sq = t * t
        s_lo = jnp.sum(jnp.where(lo, sq, 0.0), axis=-1, keepdims=True)
        s_hi = jnp.sum(jnp.where(lo, 0.0, sq), axis=-1, keepdims=True)
        ms = jnp.where(lo, s_lo, s_hi) * (1.0 / HEAD_DIM)
        n = (t * lax.rsqrt(ms + EPS)) * g
        up = pltpu.roll(n, LANES - ROT_DIM // 2, 1)
        dn = pltpu.roll(n, ROT_DIM // 2, 1)
        return n * c + up * s1 + dn * s2

    gq = gq_ref[...]
    for j in range(D_ATTN // LANES):
        q_ref[:, j * LANES:(j + 1) * LANES] = head_norm_rope(proj[:, j * LANES:(j + 1) * LANES], gq)
    k_ref[...] = head_norm_rope(proj[:, D_ATTN:D_ATTN + KV_W], gk_ref[...])
    v_ref[...] = proj[:, D_ATTN + KV_W:D_ATTN + 2 * KV_W]
    o = D_ATTN + 2 * KV_W
    xr_ref[...] = proj[:, o:o + D_RNN]
    yr_ref[...] = proj[:, o + D_RNN:o + 2 * D_RNN]


def _in_proj(x2d, g, w_bf, gq2, gk2, ctab, s1tab, s2tab, tm):
    n = x2d.shape[0]
    ntab = ctab.shape[0] // tm
    row = lambda i: (i, 0)
    fix = lambda i: (0, 0)
    tab = lambda i: (i % ntab, 0)
    out_shapes = (jax.ShapeDtypeStruct((n, D_ATTN), F32), jax.ShapeDtypeStruct((n, KV_W), F32),
                  jax.ShapeDtypeStruct((n, KV_W), F32), jax.ShapeDtypeStruct((n, D_RNN), F32),
                  jax.ShapeDtypeStruct((n, D_RNN), F32))
    return pl.pallas_call(
        _in_proj_kernel,
        grid=(n // tm,),
        in_specs=[pl.BlockSpec((tm, D_MODEL), row), pl.BlockSpec((1, D_MODEL), fix),
                  pl.BlockSpec((D_MODEL, D_IN), fix), pl.BlockSpec((1, LANES), fix),
                  pl.BlockSpec((1, LANES), fix), pl.BlockSpec((tm, LANES), tab),
                  pl.BlockSpec((tm, LANES), tab), pl.BlockSpec((tm, LANES), tab)],
        out_specs=(pl.BlockSpec((tm, D_ATTN), row), pl.BlockSpec((tm, KV_W), row),
                   pl.BlockSpec((tm, KV_W), row), pl.BlockSpec((tm, D_RNN), row),
                   pl.BlockSpec((tm, D_RNN), row)),
        out_shape=out_shapes,
        compiler_params=_cparams(1),
        name="in_proj",
    )(x2d, g, w_bf, gq2, gk2, ctab, s1tab, s2tab)


def _rope_tables(pos):
    half = ROT_DIM // 2
    inv = ROPE_THETA ** (-jnp.arange(0, ROT_DIM, 2, dtype=F32) / ROT_DIM)
    ang = pos.astype(F32)[:, None] * inv[None, :]
    cos = jnp.cos(ang)
    sin = jnp.sin(ang)
    n = pos.shape[0]
    ones = jnp.ones((n, HEAD_DIM - ROT_DIM), F32)
    zeros = jnp.zeros((n, HEAD_DIM - ROT_DIM), F32)
    zh = jnp.zeros((n, half), F32)
    c = jnp.concatenate([cos, cos, ones], axis=1)
    s1 = jnp.concatenate([-sin, zh, zeros], axis=1)
    s2 = jnp.concatenate([zh, sin, zeros], axis=1)
    two = lambda t: jnp.concatenate([t, t], axis=1)
    return two(c), two(s1), two(s2)


def _attn_prompt_kernel(sink_ref, q_ref, kc_ref, kp_ref, vc_ref, vp_ref, g_ref, o_ref, acc_ref):
    j = pl.program_id(1)
    qb = ATTN_BLOCK
    q = q_ref[...]
    kc = kc_ref[...]
    kp = kp_ref[...]
    vc = vc_ref[...]
    vp = vp_ref[...]
    rows = GROUP * qb
    row = lax.broadcasted_iota(I32, (rows, 2 * qb), 0)
    col = lax.broadcasted_iota(I32, (rows, 2 * qb), 1)
    qi = row & (qb - 1)
    low = jnp.maximum(qi, jnp.where(j > 0, 0, qb))
    mask = ((col - low) | (qi + qb - col)) >= 0
    hrow = lax.broadcasted_iota(I32, (rows, 1), 0) >> (qb.bit_length() - 1)
    for kv in range(N_KV_HEADS):
        sl = slice(kv * HEAD_DIM, (kv + 1) * HEAD_DIM)
        kk = jnp.concatenate([kp[:, sl], kc[:, sl]], axis=0).astype(BF16)
        vv = jnp.concatenate([vp[:, sl], vc[:, sl]], axis=0).astype(BF16)
        qs = jnp.concatenate(
            [q[:, (kv * GROUP + g) * HEAD_DIM:(kv * GROUP + g + 1) * HEAD_DIM] for g in range(GROUP)],
            axis=0).astype(BF16)
        s = lax.dot_general(qs, kk, (((1,), (1,)), ((), ())), preferred_element_type=F32)
        s = s * (HEAD_DIM ** -0.5)
        sink = jnp.zeros((rows, 1), F32)
        for g in range(GROUP):
            sink = jnp.where(hrow == g, sink_ref[kv * GROUP + g], sink)
        s = jnp.where(mask, s, NEG_BIG)
        m = jnp.maximum(jnp.max(s, axis=-1, keepdims=True), sink)
        e = jnp.exp(s - m)
        denom = jnp.sum(e, axis=-1, keepdims=True) + jnp.exp(sink - m)
        p = e / denom
        o = jnp.dot(p.astype(BF16), vv, preferred_element_type=F32)
        for g in range(GROUP):
            hh = kv * GROUP + g
            acc_ref[:, hh * HEAD_DIM:(hh + 1) * HEAD_DIM] = o[g * qb:(g + 1) * qb, :]
    o_ref[...] = _rmsnorm(acc_ref[...], g_ref[...])


def _attn_prompt(q, k, v, sinks, g_attn, batch, seq):
    qb = ATTN_BLOCK
    nb = seq // qb
    cur = lambda b, j: (b * nb + j, 0)
    prev = lambda b, j: (b * nb + jnp.maximum(j - 1, 0), 0)
    fix = lambda b, j: (0, 0)
    return pl.pallas_call(
        _attn_prompt_kernel,
        grid=(batch, nb),
        in_specs=[pl.BlockSpec(memory_space=pltpu.SMEM),
                  pl.BlockSpec((qb, D_ATTN), cur),
                  pl.BlockSpec((qb, KV_W), cur), pl.BlockSpec((qb, KV_W), prev),
                  pl.BlockSpec((qb, KV_W), cur), pl.BlockSpec((qb, KV_W), prev),
                  pl.BlockSpec((1, D_ATTN), fix)],
        out_specs=pl.BlockSpec((qb, D_ATTN), cur),
        out_shape=jax.ShapeDtypeStruct((batch * seq, D_ATTN), F32),
        scratch_shapes=[pltpu.VMEM((qb, D_ATTN), F32)],
        compiler_params=_cparams(2),
        name="attn_prompt",
    )(sinks, q, k, k, v, v, g_attn)


def _attn_sample_kernel(sink_ref, q_ref, kn_ref, vn_ref, ck_ref, cv_ref, g_ref, o_ref, acc_ref):
    bb = q_ref.shape[0]
    q = q_ref[...]
    kn = kn_ref[...]
    vn = vn_ref[...]
    ck = ck_ref[...]
    cv = cv_ref[...]
    rows = GROUP * bb
    row = lax.broadcasted_iota(I32, (rows, bb * WINDOW), 0)
    col = lax.broadcasted_iota(I32, (rows, bb * WINDOW), 1)
    own = (col >> (WINDOW.bit_length() - 1)) == (row & (bb - 1))
    hrow = lax.broadcasted_iota(I32, (rows, 1), 0) >> (bb.bit_length() - 1)
    for kv in range(N_KV_HEADS):
        sl = slice(kv * HEAD_DIM, (kv + 1) * HEAD_DIM)
        qs = jnp.concatenate(
            [q[:, (kv * GROUP + g) * HEAD_DIM:(kv * GROUP + g + 1) * HEAD_DIM] for g in range(GROUP)],
            axis=0)
        knr = jnp.concatenate([kn[:, sl]] * GROUP, axis=0)
        vnr = jnp.concatenate([vn[:, sl]] * GROUP, axis=0)
        scale = HEAD_DIM ** -0.5
        s = lax.dot_general(qs.astype(BF16), ck[:, sl].astype(BF16), (((1,), (1,)), ((), ())),
                            preferred_element_type=F32) * scale
        s_new = jnp.sum(qs.astype(BF16).astype(F32) * knr.astype(BF16).astype(F32),
                        axis=-1, keepdims=True) * scale
        sink = jnp.zeros((rows, 1), F32)
        for g in range(GROUP):
            sink = jnp.where(hrow == g, sink_ref[kv * GROUP + g], sink)
        s = jnp.where(own, s, NEG_BIG)
        m = jnp.maximum(jnp.maximum(jnp.max(s, axis=-1, keepdims=True), s_new), sink)
        e = jnp.exp(s - m)
        e_new = jnp.exp(s_new - m)
        denom = jnp.sum(e, axis=-1, keepdims=True) + e_new + jnp.exp(sink - m)
        p = e / denom
        o = jnp.dot(p.astype(BF16), cv[:, sl].astype(BF16), preferred_element_type=F32)
        o = o + (e_new / denom) * vnr
        for g in range(GROUP):
            hh = kv * GROUP + g
            acc_ref[:, hh * HEAD_DIM:(hh + 1) * HEAD_DIM] = o[g * bb:(g + 1) * bb, :]
    o_ref[...] = _rmsnorm(acc_ref[...], g_ref[...])


def _attn_sample(q, kn, vn, ck2d, cv2d, sinks, g_attn, bb):
    n = q.shape[0]
    row = lambda i: (i, 0)
    fix = lambda i: (0, 0)
    return pl.pallas_call(
        _attn_sample_kernel,
        grid=(n // bb,),
        in_specs=[pl.BlockSpec(memory_space=pltpu.SMEM),
                  pl.BlockSpec((bb, D_ATTN), row), pl.BlockSpec((bb, KV_W), row),
                  pl.BlockSpec((bb, KV_W), row),
                  pl.BlockSpec((bb * WINDOW, KV_W), row), pl.BlockSpec((bb * WINDOW, KV_W), row),
                  pl.BlockSpec((1, D_ATTN), fix)],
        out_specs=pl.BlockSpec((bb, D_ATTN), row),
        out_shape=jax.ShapeDtypeStruct((n, D_ATTN), F32),
        scratch_shapes=[pltpu.VMEM((bb, D_ATTN), F32)],
        compiler_params=_cparams(1),
        name="attn_sample",
    )(sinks, q, kn, vn, ck2d, cv2d, g_attn)


def _softplus(z):
    return jnp.maximum(z, 0.0) + jnp.log1p(jnp.exp(-jnp.abs(z)))


def _lru_gates(xc, wa_ref, ba_ref, wx_ref, bx_ref, lam_ref):
    xb = xc.astype(BF16)
    r = jax.nn.sigmoid(jnp.dot(xb, wa_ref[...], preferred_element_type=F32) + ba_ref[...])
    i = jax.nn.sigmoid(jnp.dot(xb, wx_ref[...], preferred_element_type=F32) + bx_ref[...])
    log_a = (-LRU_C * r) * _softplus(-lam_ref[...])
    a = jnp.exp(log_a)
    u = jnp.sqrt(-jnp.tanh(log_a) * (a * a + 1.0)) * (i * xc)
    return a, u


def _rnn_prompt_kernel(xr_ref, yr_ref, cw_ref, cb_ref, wa_ref, ba_ref, wx_ref, bx_ref, lam_ref, g_ref,
                       o_ref, hl_ref, ext_ref, h_ref):
    c = pl.program_id(1)
    tc = xr_ref.shape[0]
    pad = SUBLANES

    @pl.when(c == 0)
    def _():
        ext_ref[0:pad, :] = jnp.zeros((pad, D_RNN), F32)
        h_ref[...] = jnp.zeros((1, D_RNN), F32)

    ext_ref[pad:pad + tc, :] = xr_ref[...]
    cw = cw_ref[...]
    xc = cb_ref[...] + ext_ref[pad:pad + tc, :] * cw[CONV_WIDTH - 1:CONV_WIDTH, :]
    for w in range(CONV_WIDTH - 1):
        sh = CONV_WIDTH - 1 - w
        xc = xc + ext_ref[pad - sh:pad - sh + tc, :] * cw[w:w + 1, :]
    ext_ref[0:pad, :] = ext_ref[tc:tc + pad, :]

    a, u = _lru_gates(xc, wa_ref, ba_ref, wx_ref, bx_ref, lam_ref)

    t = lax.broadcasted_iota(I32, (tc, D_RNN), 0)
    d = 1
    while d < tc:
        a_s = jnp.where(t >= d, pltpu.roll(a, d, 0), 1.0)
        u_s = jnp.where(t >= d, pltpu.roll(u, d, 0), 0.0)
        u = a * u_s + u
        a = a * a_s
        d *= 2
    h = a * h_ref[...] + u
    h_last = h[tc - 1:tc, :]
    h_ref[...] = h_last
    hl_ref[0] = h_last
    o_ref[...] = _rmsnorm(jax.nn.gelu(yr_ref[...]) * h, g_ref[...])


def _rnn_prompt(xr, yr, cw, cb, wa, ba, wx, bx, lam, g, batch, seq, tc):
    nc = seq // tc
    cur = lambda b, c: (b * nc + c, 0)
    fix = lambda b, c: (0, 0)
    vec = pl.BlockSpec((1, D_RNN), fix)
    return pl.pallas_call(
        _rnn_prompt_kernel,
        grid=(batch, nc),
        in_specs=[pl.BlockSpec((tc, D_RNN), cur), pl.BlockSpec((tc, D_RNN), cur),
                  pl.BlockSpec((CONV_WIDTH, D_RNN), fix), vec,
                  pl.BlockSpec((D_RNN, D_RNN), fix), vec,
                  pl.BlockSpec((D_RNN, D_RNN), fix), vec, vec, vec],
        out_specs=(pl.BlockSpec((tc, D_RNN), cur), pl.BlockSpec((1, 1, D_RNN), lambda b, c: (b, 0, 0))),
        out_shape=(jax.ShapeDtypeStruct((batch * seq, D_RNN), F32),
                   jax.ShapeDtypeStruct((batch, 1, D_RNN), F32)),
        scratch_shapes=[pltpu.VMEM((tc + SUBLANES, D_RNN), F32), pltpu.VMEM((1, D_RNN), F32)],
        compiler_params=_cparams(2),
        name="rnn_prompt",
    )(xr, yr, cw, cb, wa, ba, wx, bx, lam, g)


def _rnn_sample_kernel(xr_ref, yr_ref, hist_ref, h0_ref, cw_ref, cb_ref, wa_ref, ba_ref, wx_ref, bx_ref,
                       lam_ref, g_ref, o_ref, hl_ref):
    cw = cw_ref[...]
    xc = cb_ref[...] + xr_ref[...] * cw[CONV_WIDTH - 1:CONV_WIDTH, :]
    for w in range(CONV_WIDTH - 1):
        xc = xc + hist_ref[:, w * D_RNN:(w + 1) * D_RNN] * cw[w:w + 1, :]
    a, u = _lru_gates(xc, wa_ref, ba_ref, wx_ref, bx_ref, lam_ref)
    h = a * h0_ref[...] + u
    hl_ref[...] = h
    o_ref[...] = _rmsnorm(jax.nn.gelu(yr_ref[...]) * h, g_ref[...])


def _rnn_sample(xr, yr, hist, h0, cw, cb, wa, ba, wx, bx, lam, g):
    n = xr.shape[0]
    full = lambda a: pl.BlockSpec(a.shape, lambda: (0,) * a.ndim)
    args = (xr, yr, hist, h0, cw, cb, wa, ba, wx, bx, lam, g)
    return pl.pallas_call(
        _rnn_sample_kernel,
        in_specs=[full(a) for a in args],
        out_specs=(pl.BlockSpec((n, D_RNN), lambda: (0, 0)), pl.BlockSpec((n, D_RNN), lambda: (0, 0))),
        out_shape=(jax.ShapeDtypeStruct((n, D_RNN), F32), jax.ShapeDtypeStruct((n, D_RNN), F32)),
        compiler_params=pltpu.CompilerParams(vmem_limit_bytes=VMEM_LIMIT),
        name="rnn_sample",
    )(*args)


def _split3(x):
    hi = x.astype(BF16)
    r1 = x - hi.astype(F32)
    mid = r1.astype(BF16)
    lo = (r1 - mid.astype(F32)).astype(BF16)
    return hi, mid, lo


def _mix_route_kernel(x_ref, an_ref, rn_ref, woa_ref, wor_ref, g_ref, wrt_ref, br_ref, tri_ref, low_ref,
                      x2_ref, ts_ref, dest_ref, gate_ref, n8_ref, off_ref):
    tt = x_ref.shape[0]
    tile_rows = ts_ref.shape[0]
    x2 = x_ref[...] + jnp.dot(an_ref[...].astype(BF16), woa_ref[...], preferred_element_type=F32) \
        + jnp.dot(rn_ref[...].astype(BF16), wor_ref[...], preferred_element_type=F32)
    x2_ref[...] = x2
    hn = _rmsnorm(x2, g_ref[...])

    nt = (((1,), (1,)), ((), ()))
    h3 = _split3(hn)
    w3 = _split3(wrt_ref[...])
    logits = br_ref[...]
    for a, b in ((2, 0), (1, 1), (0, 2), (1, 0), (0, 1), (0, 0)):
        logits = logits + lax.dot_general(w3[a], h3[b], nt, preferred_element_type=F32)

    ie = lax.broadcasted_iota(I32, (N_EXPERTS, tt), 0).astype(F32)
    l = logits
    vals, sels = [], []
    for _ in range(TOP_K):
        m = jnp.max(l, axis=0, keepdims=True)
        idx = jnp.min(jnp.where(l == m, ie, float(N_EXPERTS)), axis=0, keepdims=True)
        sel = ie == idx
        vals.append(m)
        sels.append(sel)
        l = jnp.where(sel, NEG_BIG, l)
    es = [jnp.exp(v - vals[0]) for v in vals]
    den = es[0] + es[1] + es[2] + es[3]
    gate_ref[0] = jnp.concatenate([e / den for e in es], axis=0)

    oh = jnp.zeros((N_EXPERTS, tt), F32)
    for sel in sels:
        oh = oh + jnp.where(sel, 1.0, 0.0)
    before = jnp.dot(oh.astype(BF16), tri_ref[...], preferred_element_type=F32)
    cnt = jnp.sum(oh, axis=1, keepdims=True).astype(I32)
    n8 = ((cnt + (SUBLANES - 1)) >> 3) << 3
    n8b = jnp.broadcast_to(n8, (N_EXPERTS, LANES))
    off = jnp.dot(low_ref[...], n8b.astype(F32).astype(BF16), preferred_element_type=F32)
    n8_ref[0] = n8b
    off_ref[0] = off.astype(I32)
    base = off[:, 0:1] + before
    dests = [jnp.sum(jnp.where(sel, base, 0.0), axis=0, keepdims=True).astype(I32) for sel in sels]
    dest_ref[0] = jnp.concatenate(dests, axis=0)

    hb = hn.astype(BF16)
    ri = lax.broadcasted_iota(I32, (DISPATCH_CHUNK, tt), 0)
    for c in range(tile_rows // DISPATCH_CHUNK):
        p = jnp.zeros((DISPATCH_CHUNK, tt), F32)
        for d in dests:
            p = p + jnp.where(ri == d - c * DISPATCH_CHUNK, 1.0, 0.0)
        ts_ref[c * DISPATCH_CHUNK:(c + 1) * DISPATCH_CHUNK, :] = jnp.dot(
            p.astype(BF16), hb, preferred_element_type=F32)


def _tile_rows(tt):
    return -(-(TOP_K * tt + N_EXPERTS * (SUBLANES - 1)) // DISPATCH_CHUNK) * DISPATCH_CHUNK


def _mix_route(x2d, an, rn, woa, wor, g, wrt, br, low, tt):
    n = x2d.shape[0]
    nt = n // tt
    tile_rows = _tile_rows(tt)
    tri = jnp.triu(jnp.ones((tt, tt), BF16), k=1)
    row = lambda i: (i, 0)
    fix = lambda i: (0, 0)
    t3 = lambda i: (i, 0, 0)
    in_specs = [pl.BlockSpec((tt, D_MODEL), row), pl.BlockSpec((tt, D_ATTN), row),
                pl.BlockSpec((tt, D_RNN), row), pl.BlockSpec((D_ATTN, D_MODEL), fix),
                pl.BlockSpec((D_RNN, D_MODEL), fix), pl.BlockSpec((1, D_MODEL), fix),
                pl.BlockSpec((N_EXPERTS, D_MODEL), fix), pl.BlockSpec((N_EXPERTS, 1), fix),
                pl.BlockSpec((tt, tt), fix), pl.BlockSpec((N_EXPERTS, N_EXPERTS), fix)]
    out_shape = (jax.ShapeDtypeStruct((n, D_MODEL), F32),
                 jax.ShapeDtypeStruct((nt * tile_rows, D_MODEL), F32),
                 jax.ShapeDtypeStruct((nt, TOP_K, tt), I32),
                 jax.ShapeDtypeStruct((nt, TOP_K, tt), F32),
                 jax.ShapeDtypeStruct((nt, N_EXPERTS, LANES), I32),
                 jax.ShapeDtypeStruct((nt, N_EXPERTS, LANES), I32))
    out_specs = (pl.BlockSpec((tt, D_MODEL), row),
                 pl.BlockSpec((tile_rows, D_MODEL), row),
                 pl.BlockSpec((1, TOP_K, tt), t3), pl.BlockSpec((1, TOP_K, tt), t3),
                 pl.BlockSpec((1, N_EXPERTS, LANES), t3), pl.BlockSpec((1, N_EXPERTS, LANES), t3))
    return pl.pallas_call(
        _mix_route_kernel,
        grid=(nt,),
        in_specs=in_specs,
        out_specs=out_specs,
        out_shape=out_shape,
        compiler_params=_cparams(1),
        name="mix_route",
    )(x2d, an, rn, woa, wor, g, wrt, br, tri, low)


def _piece_dmas(src_hbm, dst_buf, sem, p_lo, p_hi, psrc_ref, pdst_ref, plen_ref, nbits, wait):
    def body(p, carry):
        l8 = plen_ref[p]
        s = psrc_ref[p]
        d = pdst_ref[p]
        for c in range(nbits):
            size = SUBLANES << c
            low = (l8 & ((1 << c) - 1)) * SUBLANES

            @pl.when(((l8 >> c) & 1) == 1)
            def _():
                cp = pltpu.make_async_copy(
                    src_hbm.at[pl.ds(pl.multiple_of(s + low, SUBLANES), size)],
                    dst_buf.at[pl.ds(pl.multiple_of(d + low, SUBLANES), size)], sem)
                if wait:
                    cp.wait()
                else:
                    cp.start()
        return carry

    lax.fori_loop(p_lo, p_hi, body, 0)


def _moe_gmm_kernel(be_ref, nact_ref, psa_ref, pea_ref, srca_ref, dsta_ref, lena_ref,
                    psb_ref, peb_ref, srcb_ref, dstb_ref, lenb_ref,
                    tsa_hbm, tsb_hbm, wgu_ref, wdn_ref, bg_ref, bu_ref, bd_ref, perm_ref,
                    ys_ref, lhs_ref, wg_ref, wu_ref, wd_ref, sem_ref, *, nbits_a, nbits_b):
    j = pl.program_id(0)
    nb = pl.num_programs(0)
    slot = j % 2

    def gather(blk, sl, wait):
        _piece_dmas(tsa_hbm, lhs_ref.at[sl], sem_ref.at[sl], psa_ref[blk], pea_ref[blk],
                    srca_ref, dsta_ref, lena_ref, nbits_a, wait)
        _piece_dmas(tsb_hbm, lhs_ref.at[sl], sem_ref.at[sl], psb_ref[blk], peb_ref[blk],
                    srcb_ref, dstb_ref, lenb_ref, nbits_b, wait)

    @pl.when(j == 0)
    def _():
        lhs_ref[...] = jnp.zeros(lhs_ref.shape, F32)
        gather(0, 0, False)

    @pl.when(j + 1 < nb)
    def _():
        gather(j + 1, 1 - slot, False)

    changed = jnp.logical_or(j == 0, be_ref[j] != be_ref[jnp.maximum(j - 1, 0)])

    @pl.when(changed)
    def _():
        perm = perm_ref[...]
        half = PERM_COLS // 2
        for c in range(2 * D_FF // PERM_COLS):
            wb = wgu_ref[0, :, c * PERM_COLS:(c + 1) * PERM_COLS].astype(BF16)
            wp = jnp.dot(wb, perm, preferred_element_type=F32).astype(BF16)
            wg_ref[:, c * half:(c + 1) * half] = wp[:, :half]
            wu_ref[:, c * half:(c + 1) * half] = wp[:, half:]
        wd_ref[...] = wdn_ref[0].astype(BF16)

    gather(j, slot, True)

    @pl.when(j < nact_ref[0])
    def _():
        x = lhs_ref[slot].astype(BF16)
        gate = jnp.dot(x, wg_ref[...], preferred_element_type=F32) + bg_ref[0]
        up = jnp.dot(x, wu_ref[...], preferred_element_type=F32) + bu_ref[0]
        gate = jnp.minimum(gate, SWIGLU_LIMIT)
        up = jnp.clip(up, -SWIGLU_LIMIT, SWIGLU_LIMIT)
        act = (up + 1.0) * (gate * jax.nn.sigmoid(SWIGLU_ALPHA * gate))
        ys_ref[...] = jnp.dot(act.astype(BF16), wd_ref[...], preferred_element_type=F32) + bd_ref[0]

    @pl.when(j >= nact_ref[0])
    def _():
        ys_ref[...] = jnp.zeros(ys_ref.shape, F32)


def _moe_gmm(blocks, tabs_a, tabs_b, ts_a, ts_b, w_gu, w_dn, bg, bu, bd, perm, nblocks, bm, nbits_a, nbits_b):
    we = lambda j, be, *_: (be[j], 0, 0)
    grid_spec = pltpu.PrefetchScalarGridSpec(
        num_scalar_prefetch=12,
        grid=(nblocks,),
        in_specs=[pl.BlockSpec(memory_space=pl.ANY), pl.BlockSpec(memory_space=pl.ANY),
                  pl.BlockSpec((1, D_MODEL, 2 * D_FF), we),
                  pl.BlockSpec((1, D_FF, D_MODEL), we),
                  pl.BlockSpec((1, 1, D_FF), we), pl.BlockSpec((1, 1, D_FF), we),
                  pl.BlockSpec((1, 1, D_MODEL), we),
                  pl.BlockSpec((PERM_COLS, PERM_COLS), lambda j, *_: (0, 0))],
        out_specs=pl.BlockSpec((bm, D_MODEL), lambda j, *_: (j, 0)),
        scratch_shapes=[pltpu.VMEM((2, bm, D_MODEL), F32),
                        pltpu.VMEM((D_MODEL, D_FF), BF16), pltpu.VMEM((D_MODEL, D_FF), BF16),
                        pltpu.VMEM((D_FF, D_MODEL), BF16),
                        pltpu.SemaphoreType.DMA((2,))],
    )
    return pl.pallas_call(
        functools.partial(_moe_gmm_kernel, nbits_a=nbits_a, nbits_b=nbits_b),
        grid_spec=grid_spec,
        out_shape=jax.ShapeDtypeStruct((nblocks * bm, D_MODEL), F32),
        compiler_params=_cparams(1),
        name="moe_gmm",
    )(*blocks, *tabs_a, *tabs_b, ts_a, ts_b, w_gu, w_dn, bg, bu, bd, perm)


def _combine_kernel(psrc_ref, pdst_ref, plen_ref, ys_hbm, x2_ref, dest_ref, gate_ref, o_ref,
                    buf_ref, sem_ref, *, pieces_per_tile, nbits):
    i = pl.program_id(0)
    n = pl.num_programs(0)
    slot = i % 2
    tt = x2_ref.shape[0]
    tile_rows = buf_ref.shape[1]

    def gather(tile, sl, wait):
        p0 = tile * pieces_per_tile
        _piece_dmas(ys_hbm, buf_ref.at[sl], sem_ref.at[sl], p0, p0 + pieces_per_tile,
                    psrc_ref, pdst_ref, plen_ref, nbits, wait)

    @pl.when(i == 0)
    def _():
        buf_ref[...] = jnp.zeros(buf_ref.shape, F32)
        gather(0, 0, False)

    @pl.when(i + 1 < n)
    def _():
        gather(i + 1, 1 - slot, False)

    gather(i, slot, True)

    dest = dest_ref[0]
    gate = gate_ref[0]
    acc = x2_ref[...]
    li = lax.broadcasted_iota(I32, (tt, DISPATCH_CHUNK), 1)
    for c in range(tile_rows // DISPATCH_CHUNK):
        gm = jnp.zeros((tt, DISPATCH_CHUNK), F32)
        for k in range(TOP_K):
            gm = gm + jnp.where(li == dest[:, k:k + 1] - c * DISPATCH_CHUNK, gate[:, k:k + 1], 0.0)
        rows = buf_ref[slot, c * DISPATCH_CHUNK:(c + 1) * DISPATCH_CHUNK, :].astype(BF16)
        acc = acc + jnp.dot(gm.astype(BF16), rows, preferred_element_type=F32)
    o_ref[...] = acc


def _combine(psrc, pdst, plen8, ys, x2, dest, gate, tt, nbits):
    n = x2.shape[0]
    nt = n // tt
    tile_rows = _tile_rows(tt)
    pieces_per_tile = 2 * N_EXPERTS
    grid_spec = pltpu.PrefetchScalarGridSpec(
        num_scalar_prefetch=3,
        grid=(nt,),
        in_specs=[pl.BlockSpec(memory_space=pl.ANY),
                  pl.BlockSpec((tt, D_MODEL), lambda i, *_: (i, 0)),
                  pl.BlockSpec((1, tt, TOP_K), lambda i, *_: (i, 0, 0)),
                  pl.BlockSpec((1, tt, TOP_K), lambda i, *_: (i, 0, 0))],
        out_specs=pl.BlockSpec((tt, D_MODEL), lambda i, *_: (i, 0)),
        scratch_shapes=[pltpu.VMEM((2, tile_rows, D_MODEL), F32), pltpu.SemaphoreType.DMA((2,))],
    )
    return pl.pallas_call(
        functools.partial(_combine_kernel, pieces_per_tile=pieces_per_tile, nbits=nbits),
        grid_spec=grid_spec,
        out_shape=jax.ShapeDtypeStruct((n, D_MODEL), F32),
        compiler_params=_cparams(1),
        name="combine",
    )(psrc, pdst, plen8, ys, x2, dest, gate)


def _piece_tables(n8_a, off_a, rows_a, n8_b, off_b, bm, nblocks):
    nta = n8_a.shape[0]
    n8 = jnp.concatenate([n8_a, n8_b], axis=0)
    seg_off = jnp.concatenate([off_a, off_b], axis=0)
    n_tiles = n8.shape[0]
    tile_base = jnp.concatenate([jnp.arange(nta, dtype=I32) * rows_a, jnp.zeros((n_tiles - nta,), I32)])[:, None]
    tot = jnp.sum(n8, axis=0)
    pos0 = jnp.cumsum(n8, axis=0) - n8
    nblk = (tot + bm - 1) // bm
    cs = jnp.cumsum(nblk)
    bs = cs - nblk
    kblk = pos0 // bm
    len0 = jnp.minimum(n8, (kblk + 1) * bm - pos0)
    len1 = n8 - len0
    b0 = bs[None, :] + kblk
    blk = jnp.stack([b0, jnp.where(len1 > 0, b0 + 1, b0)], axis=-1)
    plen8 = jnp.stack([len0, len1], axis=-1) // SUBLANES
    ts_row = jnp.stack([tile_base + seg_off, tile_base + seg_off + len0], axis=-1)
    in_blk = jnp.stack([pos0 - kblk * bm, jnp.zeros_like(pos0)], axis=-1)
    buf_row = jnp.stack([seg_off, seg_off + len0], axis=-1)
    ys_row = blk * bm + in_blk

    em = lambda a: jnp.transpose(a, (1, 0, 2)).reshape(-1).astype(I32)
    tm = lambda a: a.reshape(-1).astype(I32)
    jj = jnp.arange(nblocks, dtype=I32)

    def gmm_tabs(sl):
        blk_em = em(blk[sl])
        return (jnp.searchsorted(blk_em, jj, side="left").astype(I32),
                jnp.searchsorted(blk_em, jj, side="right").astype(I32),
                em(ts_row[sl]), em(in_blk[sl]), em(plen8[sl]))

    def comb_tabs(sl):
        return tm(ys_row[sl]), tm(buf_row[sl]), tm(plen8[sl])

    e_last = jnp.searchsorted(cs, jnp.maximum(cs[-1] - 1, 0), side="right")
    block_e = jnp.minimum(jnp.searchsorted(cs, jj, side="right"), e_last).astype(I32)
    n_active = cs[-1].reshape(1).astype(I32)
    a, b = slice(0, nta), slice(nta, n_tiles)
    return (block_e, n_active), gmm_tabs(a), gmm_tabs(b), comb_tabs(a), comb_tabs(b)


def _block_diag(w):
    nb, bi, bo = w.shape
    eye = jnp.eye(nb, dtype=w.dtype)
    return (eye[:, None, :, None] * w[:, :, None, :]).reshape(nb * bi, nb * bo)


def _step(x_prompt, x_sample, cache_k, cache_v, state_conv, state_h, g_mix_norm, w_in, g_q_norm, g_k_norm,
          attn_sinks, conv_w, conv_b, w_lru_a, b_lru_a, w_lru_x, b_lru_x, lru_lambda, g_attn_out, g_rnn_out,
          w_out, g_ffn_norm, w_router, b_router, w_gate_up, b_gate_up, w_down, b_down,
          *, tm, tt, tc, bm, past_len):
    B, S, D = x_prompt.shape
    NS = x_sample.shape[0]
    assert x_sample.shape[1] == 1 and D == D_MODEL
    assert (B * S) % tt == 0 and (B * S) % tm == 0 and S % tc == 0 and S % ATTN_BLOCK == 0 and S % tm == 0
    assert NS % SUBLANES == 0 and tt <= bm
    assert tt % SUBLANES == 0 and NS <= bm
    n_pt = (B * S) // tt
    total_rows = TOP_K * (B * S + NS) + (n_pt + 1) * N_EXPERTS * (SUBLANES - 1)
    nblocks = -(-total_rows // bm) + N_EXPERTS
    nbits_p = (tt // SUBLANES).bit_length()
    nbits_s = (NS // SUBLANES).bit_length()

    l = 0
    row = lambda v: v[l].reshape(1, -1)
    w_in_bf = w_in[l].astype(BF16)
    gq2 = jnp.tile(g_q_norm[l], 2).reshape(1, LANES)
    gk2 = jnp.tile(g_k_norm[l], 2).reshape(1, LANES)
    wa = _block_diag(w_lru_a[l]).astype(BF16)
    wx = _block_diag(w_lru_x[l]).astype(BF16)
    ba = b_lru_a[l].reshape(1, D_RNN)
    bx = b_lru_x[l].reshape(1, D_RNN)
    wo = w_out[l].astype(BF16)
    woa, wor = wo[:D_ATTN], wo[D_ATTN:]
    wrt = w_router[l].T
    br = b_router[l].reshape(N_EXPERTS, 1)
    low = jnp.tril(jnp.ones((N_EXPERTS, N_EXPERTS), BF16), k=-1)
    bgu = b_gate_up[l].reshape(N_EXPERTS, D_FF, 2)
    bg = bgu[:, :, 0].reshape(N_EXPERTS, 1, D_FF)
    bu = bgu[:, :, 1].reshape(N_EXPERTS, 1, D_FF)
    bd = b_down[l].reshape(N_EXPERTS, 1, D_MODEL)
    half = PERM_COLS // 2
    pr = jnp.arange(PERM_COLS)
    perm = (pr[None, :] == jnp.where(pr % 2 == 0, pr // 2, half + pr // 2)[:, None]).astype(BF16)
    sinks = attn_sinks[l]

    ctab, s1tab, s2tab = _rope_tables(jnp.arange(S))
    q, k, v, xr, yr = _in_proj(x_prompt.reshape(B * S, D), row(g_mix_norm), w_in_bf, gq2, gk2,
                               ctab, s1tab, s2tab, tm)
    an = _attn_prompt(q, k, v, sinks, row(g_attn_out), B, S)
    rn, h_last_p = _rnn_prompt(xr, yr, conv_w[l], row(conv_b), wa, ba, wx, bx, row(lru_lambda),
                               row(g_rnn_out), B, S, tc)
    x2_p, ts_p, dest_p, gate_p, n8_p, off_p = _mix_route(
        x_prompt.reshape(B * S, D), an, rn, woa, wor, row(g_ffn_norm), wrt, br, low, tt)

    cs_tab = _rope_tables(jnp.full((NS,), past_len, I32))
    q_s, k_s, v_s, xr_s, yr_s = _in_proj(x_sample.reshape(NS, D), row(g_mix_norm), w_in_bf, gq2, gk2,
                                         *cs_tab, NS)
    an_s = _attn_sample(q_s, k_s, v_s, cache_k[l].reshape(NS * WINDOW, KV_W),
                        cache_v[l].reshape(NS * WINDOW, KV_W), sinks, row(g_attn_out), SUBLANES)
    rn_s, h_last_s = _rnn_sample(xr_s, yr_s, state_conv[l].reshape(NS, (CONV_WIDTH - 1) * D_RNN), state_h[l],
                                 conv_w[l], row(conv_b), wa, ba, wx, bx, row(lru_lambda), row(g_rnn_out))
    x2_s, ts_s, dest_s, gate_s, n8_s, off_s = _mix_route(
        x_sample.reshape(NS, D), an_s, rn_s, woa, wor, row(g_ffn_norm), wrt, br, low, NS)

    blocks, gmm_p, gmm_s, comb_p, comb_s = _piece_tables(
        n8_p[:, :, 0], off_p[:, :, 0], _tile_rows(tt), n8_s[:, :, 0], off_s[:, :, 0], bm, nblocks)
    ys = _moe_gmm(blocks, gmm_p, gmm_s, ts_p, ts_s, w_gate_up[l], w_down[l], bg, bu, bd, perm,
                  nblocks, bm, nbits_p, nbits_s)
    tr = lambda a: jnp.transpose(a, (0, 2, 1))
    y_p = _combine(*comb_p, ys, x2_p, tr(dest_p), tr(gate_p), tt, nbits_p)
    y_s = _combine(*comb_s, ys, x2_s, tr(dest_s), tr(gate_s), NS, nbits_s)

    kp4 = k.reshape(B, S, N_KV_HEADS, HEAD_DIM)[:, S - WINDOW:]
    vp4 = v.reshape(B, S, N_KV_HEADS, HEAD_DIM)[:, S - WINDOW:]
    cp = xr.reshape(B, S, D_RNN)[:, S - (CONV_WIDTH - 1):]
    ks4 = jnp.concatenate([cache_k[l][:, 1:], k_s.reshape(NS, 1, N_KV_HEADS, HEAD_DIM)], axis=1)
    vs4 = jnp.concatenate([cache_v[l][:, 1:], v_s.reshape(NS, 1, N_KV_HEADS, HEAD_DIM)], axis=1)
    cs_new = jnp.concatenate([state_conv[l][:, 1:], xr_s.reshape(NS, 1, D_RNN)], axis=1)
    return (y_p.reshape(B, S, D), y_s.reshape(NS, 1, D),
            kp4[None], vp4[None], cp[None], h_last_p.reshape(1, B, D_RNN),
            ks4[None], vs4[None], cs_new[None], h_last_s[None])


def kernel(x_prompt, x_sample, cache_k, cache_v, state_conv, state_h, g_mix_norm, w_in, g_q_norm, g_k_norm, attn_sinks, conv_w, conv_b, w_lru_a, b_lru_a, w_lru_x, b_lru_x, lru_lambda, g_attn_out, g_rnn_out, w_out, g_ffn_norm, w_router, b_router, w_gate_up, b_gate_up, w_down, b_down):
    return _step(x_prompt, x_sample, cache_k, cache_v, state_conv, state_h, g_mix_norm, w_in, g_q_norm,
                 g_k_norm, attn_sinks, conv_w, conv_b, w_lru_a, b_lru_a, w_lru_x, b_lru_x, lru_lambda,
                 g_attn_out, g_rnn_out, w_out, g_ffn_norm, w_router, b_router, w_gate_up, b_gate_up,
                 w_down, b_down, tm=512, tt=512, tc=256, bm=MOE_BLOCK_ROWS, past_len=PAST_LEN)
```

```python
import functools

import jax
import jax.numpy as jnp
from jax import lax
from jax.experimental import pallas as pl
from jax.experimental.pallas import tpu as pltpu

F32 = jnp.float32
BF16 = jnp.bfloat16
I32 = jnp.int32

D_MODEL = 1024
HEAD_DIM = 64
N_HEADS = 8
N_KV_HEADS = 2
GROUP = 4
WINDOW = 128
ATTN_BLOCK = 128
ROT_DIM = 16
ROPE_THETA = 500000.0
D_ATTN = 512
D_RNN = 512
KV_W = 128
D_IN = 1792
CONV_WIDTH = 4
LRU_C = 8.0
N_EXPERTS = 32
TOP_K = 4
D_FF = 1024
SWIGLU_LIMIT = 7.0
SWIGLU_ALPHA = 1.702
EPS = 1e-6
PAST_LEN = 8192

LANES = 128
SUBLANES = 8
NEG_BIG = -1e30
VMEM_LIMIT = 56 * 1024 * 1024

MOE_BLOCK_ROWS = 512
PERM_COLS = 256
DISPATCH_CHUNK = 256


def _cparams(n_axes):
    return pltpu.CompilerParams(dimension_semantics=("arbitrary",) * n_axes,
                                vmem_limit_bytes=VMEM_LIMIT)


def _rmsnorm(x, g):
    ms = jnp.mean(x * x, axis=-1, keepdims=True)
    return (x * lax.rsqrt(ms + EPS)) * g


def _in_proj_kernel(x_ref, g_ref, w_ref, gq_ref, gk_ref, c_ref, s1_ref, s2_ref,
                    q_ref, k_ref, v_ref, xr_ref, yr_ref):
    tm = x_ref.shape[0]
    h = _rmsnorm(x_ref[...], g_ref[...])
    proj = jnp.dot(h.astype(BF16), w_ref[...], preferred_element_type=F32)
    lo = lax.broadcasted_iota(I32, (tm, LANES), 1) < HEAD_DIM
    c = c_ref[...]
    s1 = s1_ref[...]
    s2 = s2_ref[...]

    def head_norm_rope(t, g):
        sq = t * t
        s_lo = jnp.sum(jnp.where(lo, sq, 0.0), axis=-1, keepdims=True)
        s_hi = jnp.sum(jnp.where(lo, 0.0, sq), axis=-1, keepdims=True)
        ms = jnp.where(lo, s_lo, s_hi) * (1.0 / HEAD_DIM)
        n = (t * lax.rsqrt(ms + EPS)) * g
        up = pltpu.roll(n, LANES - ROT_DIM // 2, 1)
        dn = pltpu.roll(n, ROT_DIM // 2, 1)
        return n * c + up * s1 + dn * s2

    gq = gq_ref[...]
    for j in range(D_ATTN // LANES):
        q_ref[:, j * LANES:(j + 1) * LANES] = head_norm_rope(proj[:, j * LANES:(j + 1) * LANES], gq)
    k_ref[...] = head_norm_rope(proj[:, D_ATTN:D_ATTN + KV_W], gk_ref[...])
    v_ref[...] = proj[:, D_ATTN + KV_W:D_ATTN + 2 * KV_W]
    o = D_ATTN + 2 * KV_W
    xr_ref[...] = proj[:, o:o + D_RNN]
    yr_ref[...] = proj[:, o + D_RNN:o + 2 * D_RNN]


def _in_proj(x2d, g, w_bf, gq2, gk2, ctab, s1tab, s2tab, tm):
    n = x2d.shape[0]
    ntab = ctab.shape[0] // tm
    row = lambda i: (i, 0)
    fix = lambda i: (0, 0)
    tab = lambda i: (i % ntab, 0)
    out_shapes = (jax.ShapeDtypeStruct((n, D_ATTN), F32), jax.ShapeDtypeStruct((n, KV_W), F32),
                  jax.ShapeDtypeStruct((n, KV_W), F32), jax.ShapeDtypeStruct((n, D_RNN), F32),
                  jax.ShapeDtypeStruct((n, D_RNN), F32))
    return pl.pallas_call(
        _in_proj_kernel,
        grid=(n // tm,),
        in_specs=[pl.BlockSpec((tm, D_MODEL), row), pl.BlockSpec((1, D_MODEL), fix),
                  pl.BlockSpec((D_MODEL, D_IN), fix), pl.BlockSpec((1, LANES), fix),
                  pl.BlockSpec((1, LANES), fix), pl.BlockSpec((tm, LANES), tab),
                  pl.BlockSpec((tm, LANES), tab), pl.BlockSpec((tm, LANES), tab)],
        out_specs=(pl.BlockSpec((tm, D_ATTN), row), pl.BlockSpec((tm, KV_W), row),
                   pl.BlockSpec((tm, KV_W), row), pl.BlockSpec((tm, D_RNN), row),
                   pl.BlockSpec((tm, D_RNN), row)),
        out_shape=out_shapes,
        compiler_params=_cparams(1),
        name="in_proj",
    )(x2d, g, w_bf, gq2, gk2, ctab, s1tab, s2tab)


def _rope_tables(pos):
    half = ROT_DIM // 2
    inv = ROPE_THETA ** (-jnp.arange(0, ROT_DIM, 2, dtype=F32) / ROT_DIM)
    ang = pos.astype(F32)[:, None] * inv[None, :]
    cos = jnp.cos(ang)
    sin = jnp.sin(ang)
    n = pos.shape[0]
    ones = jnp.ones((n, HEAD_DIM - ROT_DIM), F32)
    zeros = jnp.zeros((n, HEAD_DIM - ROT_DIM), F32)
    zh = jnp.zeros((n, half), F32)
    c = jnp.concatenate([cos, cos, ones], axis=1)
    s1 = jnp.concatenate([-sin, zh, zeros], axis=1)
    s2 = jnp.concatenate([zh, sin, zeros], axis=1)
    two = lambda t: jnp.concatenate([t, t], axis=1)
    return two(c), two(s1), two(s2)


def _attn_prompt_kernel(sink_ref, q_ref, kc_ref, kp_ref, vc_ref, vp_ref, g_ref, o_ref, acc_ref):
    j = pl.program_id(1)
    qb = ATTN_BLOCK
    q = q_ref[...]
    kc = kc_ref[...]
    kp = kp_ref[...]
    vc = vc_ref[...]
    vp = vp_ref[...]
    rows = GROUP * qb
    row = lax.broadcasted_iota(I32, (rows, 2 * qb), 0)
    col = lax.broadcasted_iota(I32, (rows, 2 * qb), 1)
    qi = row & (qb - 1)
    low = jnp.maximum(qi, jnp.where(j > 0, 0, qb))
    mask = ((col - low) | (qi + qb - col)) >= 0
    hrow = lax.broadcasted_iota(I32, (rows, 1), 0) >> (qb.bit_length() - 1)
    for kv in range(N_KV_HEADS):
        sl = slice(kv * HEAD_DIM, (kv + 1) * HEAD_DIM)
        kk = jnp.concatenate([kp[:, sl], kc[:, sl]], axis=0).astype(BF16)
        vv = jnp.concatenate([vp[:, sl], vc[:, sl]], axis=0).astype(BF16)
        qs = jnp.concatenate(
            [q[:, (kv * GROUP + g) * HEAD_DIM:(kv * GROUP + g + 1) * HEAD_DIM] for g in range(GROUP)],
            axis=0).astype(BF16)
        s = lax.dot_general(qs, kk, (((1,), (1,)), ((), ())), preferred_element_type=F32)
        s = s * (HEAD_DIM ** -0.5)
        sink = jnp.zeros((rows, 1), F32)
        for g in range(GROUP):
            sink = jnp.where(hrow == g, sink_ref[kv * GROUP + g], sink)
        s = jnp.where(mask, s, NEG_BIG)
        m = jnp.maximum(jnp.max(s, axis=-1, keepdims=True), sink)
        e = jnp.exp(s - m)
        denom = jnp.sum(e, axis=-1, keepdims=True) + jnp.exp(sink - m)
        p = e / denom
        o = jnp.dot(p.astype(BF16), vv, preferred_element_type=F32)
        for g in range(GROUP):
            hh = kv * GROUP + g
            acc_ref[:, hh * HEAD_DIM:(hh + 1) * HEAD_DIM] = o[g * qb:(g + 1) * qb, :]
    o_ref[...] = _rmsnorm(acc_ref[...], g_ref[...])


def _attn_prompt(q, k, v, sinks, g_attn, batch, seq):
    qb = ATTN_BLOCK
    nb = seq // qb
    cur = lambda b, j: (b * nb + j, 0)
    prev = lambda b, j: (b * nb + jnp.maximum(j - 1, 0), 0)
    fix = lambda b, j: (0, 0)
    return pl.pallas_call(
        _attn_prompt_kernel,
        grid=(batch, nb),
        in_specs=[pl.BlockSpec(memory_space=pltpu.SMEM),
                  pl.BlockSpec((qb, D_ATTN), cur),
                  pl.BlockSpec((qb, KV_W), cur), pl.BlockSpec((qb, KV_W), prev),
                  pl.BlockSpec((qb, KV_W), cur), pl.BlockSpec((qb, KV_W), prev),
                  pl.BlockSpec((1, D_ATTN), fix)],
        out_specs=pl.BlockSpec((qb, D_ATTN), cur),
        out_shape=jax.ShapeDtypeStruct((batch * seq, D_ATTN), F32),
        scratch_shapes=[pltpu.VMEM((qb, D_ATTN), F32)],
        compiler_params=_cparams(2),
        name="attn_prompt",
    )(sinks, q, k, k, v, v, g_attn)


def _attn_sample_kernel(sink_ref, q_ref, kn_ref, vn_ref, ck_ref, cv_ref, g_ref, o_ref, acc_ref):
    bb = q_ref.shape[0]
    q = q_ref[...]
    kn = kn_ref[...]
    vn = vn_ref[...]
    ck = ck_ref[...]
    cv = cv_ref[...]
    rows = GROUP * bb
    row = lax.broadcasted_iota(I32, (rows, bb * WINDOW), 0)
    col = lax.broadcasted_iota(I32, (rows, bb * WINDOW), 1)
    own = (col >> (WINDOW.bit_length() - 1)) == (row & (bb - 1))
    hrow = lax.broadcasted_iota(I32, (rows, 1), 0) >> (bb.bit_length() - 1)
    for kv in range(N_KV_HEADS):
        sl = slice(kv * HEAD_DIM, (kv + 1) * HEAD_DIM)
        qs = jnp.concatenate(
            [q[:, (kv * GROUP + g) * HEAD_DIM:(kv * GROUP + g + 1) * HEAD_DIM] for g in range(GROUP)],
            axis=0)
        knr = jnp.concatenate([kn[:, sl]] * GROUP, axis=0)
        vnr = jnp.concatenate([vn[:, sl]] * GROUP, axis=0)
        scale = HEAD_DIM ** -0.5
        s = lax.dot_general(qs.astype(BF16), ck[:, sl].astype(BF16), (((1,), (1,)), ((), ())),
                            preferred_element_type=F32) * scale
        s_new = jnp.sum(qs.astype(BF16).astype(F32) * knr.astype(BF16).astype(F32),
                        axis=-1, keepdims=True) * scale
        sink = jnp.zeros((rows, 1), F32)
        for g in range(GROUP):
            sink = jnp.where(hrow == g, sink_ref[kv * GROUP + g], sink)
        s = jnp.where(own, s, NEG_BIG)
        m = jnp.maximum(jnp.maximum(jnp.max(s, axis=-1, keepdims=True), s_new), sink)
        e = jnp.exp(s - m)
        e_new = jnp.exp(s_new - m)
        denom = jnp.sum(e, axis=-1, keepdims=True) + e_new + jnp.exp(sink - m)
        p = e / denom
        o = jnp.dot(p.astype(BF16), cv[:, sl].astype(BF16), preferred_element_type=F32)
        o = o + (e_new / denom) * vnr
        for g in range(GROUP):
            hh = kv * GROUP + g
            acc_ref[:, hh * HEAD_DIM:(hh + 1) * HEAD_DIM] = o[g * bb:(g + 1) * bb, :]
    o_ref[...] = _rmsnorm(acc_ref[...], g_ref[...])


def _attn_sample(q, kn, vn, ck2d, cv2d, sinks, g_attn, bb):
    n = q.shape[0]
    row = lambda i: (i, 0)
    fix = lambda i: (0, 0)
    return pl.pallas_call(
        _attn_sample_kernel,
        grid=(n // bb,),
        in_specs=[pl.BlockSpec(memory_space=pltpu.SMEM),
                  pl.BlockSpec((bb, D_ATTN), row), pl.BlockSpec((bb, KV_W), row),
                  pl.BlockSpec((bb, KV_W), row),
                  pl.BlockSpec((bb * WINDOW, KV_W), row), pl.BlockSpec((bb * WINDOW, KV_W), row),
                  pl.BlockSpec((1, D_ATTN), fix)],
        out_specs=pl.BlockSpec((bb, D_ATTN), row),
        out_shape=jax.ShapeDtypeStruct((n, D_ATTN), F32),
        scratch_shapes=[pltpu.VMEM((bb, D_ATTN), F32)],
        compiler_params=_cparams(1),
        name="attn_sample",
    )(sinks, q, kn, vn, ck2d, cv2d, g_attn)


def _softplus(z):
    return jnp.maximum(z, 0.0) + jnp.log1p(jnp.exp(-jnp.abs(z)))


def _lru_gates(xc, wa_ref, ba_ref, wx_ref, bx_ref, lam_ref):
    xb = xc.astype(BF16)
    r = jax.nn.sigmoid(jnp.dot(xb, wa_ref[...], preferred_element_type=F32) + ba_ref[...])
    i = jax.nn.sigmoid(jnp.dot(xb, wx_ref[...], preferred_element_type=F32) + bx_ref[...])
    log_a = (-LRU_C * r) * _softplus(-lam_ref[...])
    a = jnp.exp(log_a)
    u = jnp.sqrt(-jnp.tanh(log_a) * (a * a + 1.0)) * (i * xc)
    return a, u


def _rnn_prompt_kernel(xr_ref, yr_ref, cw_ref, cb_ref, wa_ref, ba_ref, wx_ref, bx_ref, lam_ref, g_ref,
                       o_ref, hl_ref, ext_ref, h_ref):
    c = pl.program_id(1)
    tc = xr_ref.shape[0]
    pad = SUBLANES

    @pl.when(c == 0)
    def _():
        ext_ref[0:pad, :] = jnp.zeros((pad, D_RNN), F32)
        h_ref[...] = jnp.zeros((1, D_RNN), F32)

    ext_ref[pad:pad + tc, :] = xr_ref[...]
    cw = cw_ref[...]
    xc = cb_ref[...] + ext_ref[pad:pad + tc, :] * cw[CONV_WIDTH - 1:CONV_WIDTH, :]
    for w in range(CONV_WIDTH - 1):
        sh = CONV_WIDTH - 1 - w
        xc = xc + ext_ref[pad - sh:pad - sh + tc, :] * cw[w:w + 1, :]
    ext_ref[0:pad, :] = ext_ref[tc:tc + pad, :]

    a, u = _lru_gates(xc, wa_ref, ba_ref, wx_ref, bx_ref, lam_ref)

    t = lax.broadcasted_iota(I32, (tc, D_RNN), 0)
    d = 1
    while d < tc:
        a_s = jnp.where(t >= d, pltpu.roll(a, d, 0), 1.0)
        u_s = jnp.where(t >= d, pltpu.roll(u, d, 0), 0.0)
        u = a * u_s + u
        a = a * a_s
        d *= 2
    h = a * h_ref[...] + u
    h_last = h[tc - 1:tc, :]
    h_ref[...] = h_last
    hl_ref[0] = h_last
    o_ref[...] = _rmsnorm(jax.nn.gelu(yr_ref[...]) * h, g_ref[...])


def _rnn_prompt(xr, yr, cw, cb, wa, ba, wx, bx, lam, g, batch, seq, tc):
    nc = seq // tc
    cur = lambda b, c: (b * nc + c, 0)
    fix = lambda b, c: (0, 0)
    vec = pl.BlockSpec((1, D_RNN), fix)
    return pl.pallas_call(
        _rnn_prompt_kernel,
        grid=(batch, nc),
        in_specs=[pl.BlockSpec((tc, D_RNN), cur), pl.BlockSpec((tc, D_RNN), cur),
                  pl.BlockSpec((CONV_WIDTH, D_RNN), fix), vec,
                  pl.BlockSpec((D_RNN, D_RNN), fix), vec,
                  pl.BlockSpec((D_RNN, D_RNN), fix), vec, vec, vec],
        out_specs=(pl.BlockSpec((tc, D_RNN), cur), pl.BlockSpec((1, 1, D_RNN), lambda b, c: (b, 0, 0))),
        out_shape=(jax.ShapeDtypeStruct((batch * seq, D_RNN), F32),
                   jax.ShapeDtypeStruct((batch, 1, D_RNN), F32)),
        scratch_shapes=[pltpu.VMEM((tc + SUBLANES, D_RNN), F32), pltpu.VMEM((1, D_RNN), F32)],
        compiler_params=_cparams(2),
        name="rnn_prompt",
    )(xr, yr, cw, cb, wa, ba, wx, bx, lam, g)


def _rnn_sample_kernel(xr_ref, yr_ref, hist_ref, h0_ref, cw_ref, cb_ref, wa_ref, ba_ref, wx_ref, bx_ref,
                       lam_ref, g_ref, o_ref, hl_ref):
    cw = cw_ref[...]
    xc = cb_ref[...] + xr_ref[...] * cw[CONV_WIDTH - 1:CONV_WIDTH, :]
    for w in range(CONV_WIDTH - 1):
        xc = xc + hist_ref[:, w * D_RNN:(w + 1) * D_RNN] * cw[w:w + 1, :]
    a, u = _lru_gates(xc, wa_ref, ba_ref, wx_ref, bx_ref, lam_ref)
    h = a * h0_ref[...] + u
    hl_ref[...] = h
    o_ref[...] = _rmsnorm(jax.nn.gelu(yr_ref[...]) * h, g_ref[...])


def _rnn_sample(xr, yr, hist, h0, cw, cb, wa, ba, wx, bx, lam, g):
    n = xr.shape[0]
    full = lambda a: pl.BlockSpec(a.shape, lambda: (0,) * a.ndim)
    args = (xr, yr, hist, h0, cw, cb, wa, ba, wx, bx, lam, g)
    return pl.pallas_call(
        _rnn_sample_kernel,
        in_specs=[full(a) for a in args],
        out_specs=(pl.BlockSpec((n, D_RNN), lambda: (0, 0)), pl.BlockSpec((n, D_RNN), lambda: (0, 0))),
        out_shape=(jax.ShapeDtypeStruct((n, D_RNN), F32), jax.ShapeDtypeStruct((n, D_RNN), F32)),
        compiler_params=pltpu.CompilerParams(vmem_limit_bytes=VMEM_LIMIT),
        name="rnn_sample",
    )(*args)


def _split3(x):
    hi = x.astype(BF16)
    r1 = x - hi.astype(F32)
    mid = r1.astype(BF16)
    lo = (r1 - mid.astype(F32)).astype(BF16)
    return hi, mid, lo


def _mix_route_kernel(x_ref, an_ref, rn_ref, woa_ref, wor_ref, g_ref, wrt_ref, br_ref, tri_ref, low_ref,
                      x2_ref, ts_ref, dest_ref, gate_ref, n8_ref, off_ref):
    tt = x_ref.shape[0]
    tile_rows = ts_ref.shape[0]
    x2 = x_ref[...] + jnp.dot(an_ref[...].astype(BF16), woa_ref[...], preferred_element_type=F32) \
        + jnp.dot(rn_ref[...].astype(BF16), wor_ref[...], preferred_element_type=F32)
    x2_ref[...] = x2
    hn = _rmsnorm(x2, g_ref[...])

    nt = (((1,), (1,)), ((), ()))
    h3 = _split3(hn)
    w3 = _split3(wrt_ref[...])
    logits = br_ref[...]
    for a, b in ((2, 0), (1, 1), (0, 2), (1, 0), (0, 1), (0, 0)):
        logits = logits + lax.dot_general(w3[a], h3[b], nt, preferred_element_type=F32)

    ie = lax.broadcasted_iota(I32, (N_EXPERTS, tt), 0).astype(F32)
    l = logits
    vals, sels = [], []
    for _ in range(TOP_K):
        m = jnp.max(l, axis=0, keepdims=True)
        idx = jnp.min(jnp.where(l == m, ie, float(N_EXPERTS)), axis=0, keepdims=True)
        sel = ie == idx
        vals.append(m)
        sels.append(sel)
        l = jnp.where(sel, NEG_BIG, l)
    es = [jnp.exp(v - vals[0]) for v in vals]
    den = es[0] + es[1] + es[2] + es[3]
    gate_ref[0] = jnp.concatenate([e / den for e in es], axis=0)

    oh = jnp.zeros((N_EXPERTS, tt), F32)
    for sel in sels:
        oh = oh + jnp.where(sel, 1.0, 0.0)
    before = jnp.dot(oh.astype(BF16), tri_ref[...], preferred_element_type=F32)
    cnt = jnp.sum(oh, axis=1, keepdims=True).astype(I32)
    n8 = ((cnt + (SUBLANES - 1)) >> 3) << 3
    n8b = jnp.broadcast_to(n8, (N_EXPERTS, LANES))
    off = jnp.dot(low_ref[...], n8b.astype(F32).astype(BF16), preferred_element_type=F32)
    n8_ref[0] = n8b
    off_ref[0] = off.astype(I32)
    base = off[:, 0:1] + before
    dests = [jnp.sum(jnp.where(sel, base, 0.0), axis=0, keepdims=True).astype(I32) for sel in sels]
    dest_ref[0] = jnp.concatenate(dests, axis=0)

    hb = hn.astype(BF16)
    ri = lax.broadcasted_iota(I32, (DISPATCH_CHUNK, tt), 0)
    for c in range(tile_rows // DISPATCH_CHUNK):
        p = jnp.zeros((DISPATCH_CHUNK, tt), F32)
        for d in dests:
            p = p + jnp.where(ri == d - c * DISPATCH_CHUNK, 1.0, 0.0)
        ts_ref[c * DISPATCH_CHUNK:(c + 1) * DISPATCH_CHUNK, :] = jnp.dot(
            p.astype(BF16), hb, preferred_element_type=F32)


def _tile_rows(tt):
    return -(-(TOP_K * tt + N_EXPERTS * (SUBLANES - 1)) // DISPATCH_CHUNK) * DISPATCH_CHUNK


def _mix_route(x2d, an, rn, woa, wor, g, wrt, br, low, tt):
    n = x2d.shape[0]
    nt = n // tt
    tile_rows = _tile_rows(tt)
    tri = jnp.triu(jnp.ones((tt, tt), BF16), k=1)
    row = lambda i: (i, 0)
    fix = lambda i: (0, 0)
    t3 = lambda i: (i, 0, 0)
    in_specs = [pl.BlockSpec((tt, D_MODEL), row), pl.BlockSpec((tt, D_ATTN), row),
                pl.BlockSpec((tt, D_RNN), row), pl.BlockSpec((D_ATTN, D_MODEL), fix),
                pl.BlockSpec((D_RNN, D_MODEL), fix), pl.BlockSpec((1, D_MODEL), fix),
                pl.BlockSpec((N_EXPERTS, D_MODEL), fix), pl.BlockSpec((N_EXPERTS, 1), fix),
                pl.BlockSpec((tt, tt), fix), pl.BlockSpec((N_EXPERTS, N_EXPERTS), fix)]
    out_shape = (jax.ShapeDtypeStruct((n, D_MODEL), F32),
                 jax.ShapeDtypeStruct((nt * tile_rows, D_MODEL), F32),
                 jax.ShapeDtypeStruct((nt, TOP_K, tt), I32),
                 jax.ShapeDtypeStruct((nt, TOP_K, tt), F32),
                 jax.ShapeDtypeStruct((nt, N_EXPERTS, LANES), I32),
                 jax.ShapeDtypeStruct((nt, N_EXPERTS, LANES), I32))
    out_specs = (pl.BlockSpec((tt, D_MODEL), row),
                 pl.BlockSpec((tile_rows, D_MODEL), row),
                 pl.BlockSpec((1, TOP_K, tt), t3), pl.BlockSpec((1, TOP_K, tt), t3),
                 pl.BlockSpec((1, N_EXPERTS, LANES), t3), pl.BlockSpec((1, N_EXPERTS, LANES), t3))
    return pl.pallas_call(
        _mix_route_kernel,
        grid=(nt,),
        in_specs=in_specs,
        out_specs=out_specs,
        out_shape=out_shape,
        compiler_params=_cparams(1),
        name="mix_route",
    )(x2d, an, rn, woa, wor, g, wrt, br, tri, low)


def _start_pieces(src_hbm, dst_buf, sem, p_lo, p_hi, psrc_ref, pdst_ref, plen_ref, nbits):
    def body(p, carry):
        l8 = plen_ref[p]

        @pl.when(l8 != 0)
        def _():
            s = psrc_ref[p]
            d = pdst_ref[p]
            for c in range(nbits):
                size = SUBLANES << c
                low = (l8 & ((1 << c) - 1)) * SUBLANES

                @pl.when(((l8 >> c) & 1) == 1)
                def _():
                    pltpu.make_async_copy(
                        src_hbm.at[pl.ds(pl.multiple_of(s + low, SUBLANES), size)],
                        dst_buf.at[pl.ds(pl.multiple_of(d + low, SUBLANES), size)], sem).start()
        return carry

    lax.fori_loop(p_lo, p_hi, body, 0)


def _wait_rows(src_hbm, dst_buf, sem, rows8, nbits):
    for c in range(nbits):
        size = SUBLANES << c

        @pl.when(((rows8 >> c) & 1) == 1)
        def _():
            pltpu.make_async_copy(src_hbm.at[pl.ds(0, size)], dst_buf.at[pl.ds(0, size)], sem).wait()


def _moe_gmm_kernel(be_ref, nact_ref, rows_ref, psa_ref, pea_ref, srca_ref, dsta_ref, lena_ref,
                    psb_ref, peb_ref, srcb_ref, dstb_ref, lenb_ref,
                    tsa_hbm, tsb_hbm, wgu_ref, wdn_ref, bg_ref, bu_ref, bd_ref, perm_ref,
                    ys_ref, lhs_ref, wg_ref, wu_ref, wd_ref, sem_ref, *, nbits_a, nbits_b):
    j = pl.program_id(0)
    nb = pl.num_programs(0)
    slot = j % 2
    bm = lhs_ref.shape[1]

    def gather(blk, sl):
        _start_pieces(tsa_hbm, lhs_ref.at[sl], sem_ref.at[sl], psa_ref[blk], pea_ref[blk],
                      srca_ref, dsta_ref, lena_ref, nbits_a)
        _start_pieces(tsb_hbm, lhs_ref.at[sl], sem_ref.at[sl], psb_ref[blk], peb_ref[blk],
                      srcb_ref, dstb_ref, lenb_ref, nbits_b)

    @pl.when(j == 0)
    def _():
        lhs_ref[...] = jnp.zeros(lhs_ref.shape, F32)
        gather(0, 0)

    @pl.when(j + 1 < nb)
    def _():
        gather(j + 1, 1 - slot)

    changed = jnp.logical_or(j == 0, be_ref[j] != be_ref[jnp.maximum(j - 1, 0)])

    @pl.when(changed)
    def _():
        perm = perm_ref[...]
        half = PERM_COLS // 2
        for c in range(2 * D_FF // PERM_COLS):
            wb = wgu_ref[0, :, c * PERM_COLS:(c + 1) * PERM_COLS].astype(BF16)
            wp = jnp.dot(wb, perm, preferred_element_type=F32).astype(BF16)
            wg_ref[:, c * half:(c + 1) * half] = wp[:, :half]
            wu_ref[:, c * half:(c + 1) * half] = wp[:, half:]
        wd_ref[...] = wdn_ref[0].astype(BF16)

    _wait_rows(tsa_hbm, lhs_ref.at[slot], sem_ref.at[slot], rows_ref[j], (bm // SUBLANES).bit_length())

    @pl.when(j < nact_ref[0])
    def _():
        x = lhs_ref[slot].astype(BF16)
        gate = jnp.dot(x, wg_ref[...], preferred_element_type=F32) + bg_ref[0]
        up = jnp.dot(x, wu_ref[...], preferred_element_type=F32) + bu_ref[0]
        gate = jnp.minimum(gate, SWIGLU_LIMIT)
        up = jnp.clip(up, -SWIGLU_LIMIT, SWIGLU_LIMIT)
        act = (up + 1.0) * (gate * jax.nn.sigmoid(SWIGLU_ALPHA * gate))
        ys_ref[...] = jnp.dot(act.astype(BF16), wd_ref[...], preferred_element_type=F32) + bd_ref[0]

    @pl.when(j >= nact_ref[0])
    def _():
        ys_ref[...] = jnp.zeros(ys_ref.shape, F32)


def _moe_gmm(blocks, tabs_a, tabs_b, ts_a, ts_b, w_gu, w_dn, bg, bu, bd, perm, nblocks, bm, nbits_a, nbits_b):
    we = lambda j, be, *_: (be[j], 0, 0)
    grid_spec = pltpu.PrefetchScalarGridSpec(
        num_scalar_prefetch=13,
        grid=(nblocks,),
        in_specs=[pl.BlockSpec(memory_space=pl.ANY), pl.BlockSpec(memory_space=pl.ANY),
                  pl.BlockSpec((1, D_MODEL, 2 * D_FF), we),
                  pl.BlockSpec((1, D_FF, D_MODEL), we),
                  pl.BlockSpec((1, 1, D_FF), we), pl.BlockSpec((1, 1, D_FF), we),
                  pl.BlockSpec((1, 1, D_MODEL), we),
                  pl.BlockSpec((PERM_COLS, PERM_COLS), lambda j, *_: (0, 0))],
        out_specs=pl.BlockSpec((bm, D_MODEL), lambda j, *_: (j, 0)),
        scratch_shapes=[pltpu.VMEM((2, bm, D_MODEL), F32),
                        pltpu.VMEM((D_MODEL, D_FF), BF16), pltpu.VMEM((D_MODEL, D_FF), BF16),
                        pltpu.VMEM((D_FF, D_MODEL), BF16),
                        pltpu.SemaphoreType.DMA((2,))],
    )
    return pl.pallas_call(
        functools.partial(_moe_gmm_kernel, nbits_a=nbits_a, nbits_b=nbits_b),
        grid_spec=grid_spec,
        out_shape=jax.ShapeDtypeStruct((nblocks * bm, D_MODEL), F32),
        compiler_params=_cparams(1),
        name="moe_gmm",
    )(*blocks, *tabs_a, *tabs_b, ts_a, ts_b, w_gu, w_dn, bg, bu, bd, perm)


def _combine_kernel(psrc_ref, pdst_ref, plen_ref, rows_ref, ys_hbm, x2_ref, dest_ref, gate_ref, o_ref,
                    buf_ref, sem_ref, *, pieces_per_tile, nbits):
    i = pl.program_id(0)
    n = pl.num_programs(0)
    slot = i % 2
    tt = x2_ref.shape[0]
    tile_rows = buf_ref.shape[1]

    def gather(tile, sl):
        p0 = tile * pieces_per_tile
        _start_pieces(ys_hbm, buf_ref.at[sl], sem_ref.at[sl], p0, p0 + pieces_per_tile,
                      psrc_ref, pdst_ref, plen_ref, nbits)

    @pl.when(i == 0)
    def _():
        buf_ref[...] = jnp.zeros(buf_ref.shape, F32)
        gather(0, 0)

    @pl.when(i + 1 < n)
    def _():
        gather(i + 1, 1 - slot)

    _wait_rows(ys_hbm, buf_ref.at[slot], sem_ref.at[slot], rows_ref[i], (tile_rows // SUBLANES).bit_length())

    dest = dest_ref[0]
    gate = gate_ref[0]
    acc = x2_ref[...]
    li = lax.broadcasted_iota(I32, (tt, DISPATCH_CHUNK), 1)
    for c in range(tile_rows // DISPATCH_CHUNK):
        gm = jnp.zeros((tt, DISPATCH_CHUNK), F32)
        for k in range(TOP_K):
            gm = gm + jnp.where(li == dest[:, k:k + 1] - c * DISPATCH_CHUNK, gate[:, k:k + 1], 0.0)
        rows = buf_ref[slot, c * DISPATCH_CHUNK:(c + 1) * DISPATCH_CHUNK, :].astype(BF16)
        acc = acc + jnp.dot(gm.astype(BF16), rows, preferred_element_type=F32)
    o_ref[...] = acc


def _combine(psrc, pdst, plen8, rows8, ys, x2, dest, gate, tt, nbits):
    n = x2.shape[0]
    nt = n // tt
    tile_rows = _tile_rows(tt)
    pieces_per_tile = 2 * N_EXPERTS
    grid_spec = pltpu.PrefetchScalarGridSpec(
        num_scalar_prefetch=4,
        grid=(nt,),
        in_specs=[pl.BlockSpec(memory_space=pl.ANY),
                  pl.BlockSpec((tt, D_MODEL), lambda i, *_: (i, 0)),
                  pl.BlockSpec((1, tt, TOP_K), lambda i, *_: (i, 0, 0)),
                  pl.BlockSpec((1, tt, TOP_K), lambda i, *_: (i, 0, 0))],
        out_specs=pl.BlockSpec((tt, D_MODEL), lambda i, *_: (i, 0)),
        scratch_shapes=[pltpu.VMEM((2, tile_rows, D_MODEL), F32), pltpu.SemaphoreType.DMA((2,))],
    )
    return pl.pallas_call(
        functools.partial(_combine_kernel, pieces_per_tile=pieces_per_tile, nbits=nbits),
        grid_spec=grid_spec,
        out_shape=jax.ShapeDtypeStruct((n, D_MODEL), F32),
        compiler_params=_cparams(1),
        name="combine",
    )(psrc, pdst, plen8, rows8, ys, x2, dest, gate)


def _piece_tables(n8_a, off_a, rows_a, n8_b, off_b, bm, nblocks):
    nta = n8_a.shape[0]
    n8 = jnp.concatenate([n8_a, n8_b], axis=0)
    seg_off = jnp.concatenate([off_a, off_b], axis=0)
    n_tiles = n8.shape[0]
    tile_base = jnp.concatenate([jnp.arange(nta, dtype=I32) * rows_a, jnp.zeros((n_tiles - nta,), I32)])[:, None]
    tot = jnp.sum(n8, axis=0)
    pos0 = jnp.cumsum(n8, axis=0) - n8
    nblk = (tot + bm - 1) // bm
    cs = jnp.cumsum(nblk)
    bs = cs - nblk
    kblk = pos0 // bm
    len0 = jnp.minimum(n8, (kblk + 1) * bm - pos0)
    len1 = n8 - len0
    b0 = bs[None, :] + kblk
    blk = jnp.stack([b0, jnp.where(len1 > 0, b0 + 1, b0)], axis=-1)
    plen8 = jnp.stack([len0, len1], axis=-1) // SUBLANES
    ts_row = jnp.stack([tile_base + seg_off, tile_base + seg_off + len0], axis=-1)
    in_blk = jnp.stack([pos0 - kblk * bm, jnp.zeros_like(pos0)], axis=-1)
    buf_row = jnp.stack([seg_off, seg_off + len0], axis=-1)
    ys_row = blk * bm + in_blk

    em = lambda a: jnp.transpose(a, (1, 0, 2)).reshape(-1).astype(I32)
    tm = lambda a: a.reshape(-1).astype(I32)
    jj = jnp.arange(nblocks, dtype=I32)

    def gmm_tabs(sl):
        blk_em = em(blk[sl])
        first = jnp.sum(blk_em[None, :] < jj[:, None], axis=1).astype(I32)
        last = jnp.sum(blk_em[None, :] <= jj[:, None], axis=1).astype(I32)
        return first, last, em(ts_row[sl]), em(in_blk[sl]), em(plen8[sl])

    def comb_tabs(sl):
        return tm(ys_row[sl]), tm(buf_row[sl]), tm(plen8[sl]), (jnp.sum(n8[sl], axis=1) // SUBLANES).astype(I32)

    count_le = lambda v: jnp.sum(cs[None, :] <= v[:, None], axis=1)
    n_active = cs[-1]
    e_last = count_le(jnp.maximum(n_active - 1, 0).reshape(1))[0]
    block_e = jnp.minimum(count_le(jj), e_last).astype(I32)
    rows8 = jnp.clip(tot[block_e] - (jj - bs[block_e]) * bm, 0, bm) // SUBLANES
    rows8 = jnp.where(jj < n_active, rows8, 0).astype(I32)
    a, b = slice(0, nta), slice(nta, n_tiles)
    return ((block_e, n_active.reshape(1).astype(I32), rows8), gmm_tabs(a), gmm_tabs(b),
            comb_tabs(a), comb_tabs(b))


def _block_diag(w):
    nb, bi, bo = w.shape
    eye = jnp.eye(nb, dtype=w.dtype)
    return (eye[:, None, :, None] * w[:, :, None, :]).reshape(nb * bi, nb * bo)


def _step(x_prompt, x_sample, cache_k, cache_v, state_conv, state_h, g_mix_norm, w_in, g_q_norm, g_k_norm,
          attn_sinks, conv_w, conv_b, w_lru_a, b_lru_a, w_lru_x, b_lru_x, lru_lambda, g_attn_out, g_rnn_out,
          w_out, g_ffn_norm, w_router, b_router, w_gate_up, b_gate_up, w_down, b_down,
          *, tm, tt, tc, bm, past_len):
    B, S, D = x_prompt.shape
    NS = x_sample.shape[0]
    assert x_sample.shape[1] == 1 and D == D_MODEL
    assert (B * S) % tt == 0 and (B * S) % tm == 0 and S % tc == 0 and S % ATTN_BLOCK == 0 and S % tm == 0
    assert NS % SUBLANES == 0 and tt <= bm
    assert tt % SUBLANES == 0 and NS <= bm
    n_pt = (B * S) // tt
    total_rows = TOP_K * (B * S + NS) + (n_pt + 1) * N_EXPERTS * (SUBLANES - 1)
    nblocks = -(-total_rows // bm) + N_EXPERTS
    nbits_p = (tt // SUBLANES).bit_length()
    nbits_s = (NS // SUBLANES).bit_length()

    l = 0
    row = lambda v: v[l].reshape(1, -1)
    w_in_bf = w_in[l].astype(BF16)
    gq2 = jnp.tile(g_q_norm[l], 2).reshape(1, LANES)
    gk2 = jnp.tile(g_k_norm[l], 2).reshape(1, LANES)
    wa = _block_diag(w_lru_a[l]).astype(BF16)
    wx = _block_diag(w_lru_x[l]).astype(BF16)
    ba = b_lru_a[l].reshape(1, D_RNN)
    bx = b_lru_x[l].reshape(1, D_RNN)
    wo = w_out[l].astype(BF16)
    woa, wor = wo[:D_ATTN], wo[D_ATTN:]
    wrt = w_router[l].T
    br = b_router[l].reshape(N_EXPERTS, 1)
    low = jnp.tril(jnp.ones((N_EXPERTS, N_EXPERTS), BF16), k=-1)
    bgu = b_gate_up[l].reshape(N_EXPERTS, D_FF, 2)
    bg = bgu[:, :, 0].reshape(N_EXPERTS, 1, D_FF)
    bu = bgu[:, :, 1].reshape(N_EXPERTS, 1, D_FF)
    bd = b_down[l].reshape(N_EXPERTS, 1, D_MODEL)
    half = PERM_COLS // 2
    pr = jnp.arange(PERM_COLS)
    perm = (pr[None, :] == jnp.where(pr % 2 == 0, pr // 2, half + pr // 2)[:, None]).astype(BF16)
    sinks = attn_sinks[l]

    ctab, s1tab, s2tab = _rope_tables(jnp.arange(S))
    q, k, v, xr, yr = _in_proj(x_prompt.reshape(B * S, D), row(g_mix_norm), w_in_bf, gq2, gk2,
                               ctab, s1tab, s2tab, tm)
    an = _attn_prompt(q, k, v, sinks, row(g_attn_out), B, S)
    rn, h_last_p = _rnn_prompt(xr, yr, conv_w[l], row(conv_b), wa, ba, wx, bx, row(lru_lambda),
                               row(g_rnn_out), B, S, tc)
    x2_p, ts_p, dest_p, gate_p, n8_p, off_p = _mix_route(
        x_prompt.reshape(B * S, D), an, rn, woa, wor, row(g_ffn_norm), wrt, br, low, tt)

    cs_tab = _rope_tables(jnp.full((NS,), past_len, I32))
    q_s, k_s, v_s, xr_s, yr_s = _in_proj(x_sample.reshape(NS, D), row(g_mix_norm), w_in_bf, gq2, gk2,
                                         *cs_tab, NS)
    an_s = _attn_sample(q_s, k_s, v_s, cache_k[l].reshape(NS * WINDOW, KV_W),
                        cache_v[l].reshape(NS * WINDOW, KV_W), sinks, row(g_attn_out), SUBLANES)
    rn_s, h_last_s = _rnn_sample(xr_s, yr_s, state_conv[l].reshape(NS, (CONV_WIDTH - 1) * D_RNN), state_h[l],
                                 conv_w[l], row(conv_b), wa, ba, wx, bx, row(lru_lambda), row(g_rnn_out))
    x2_s, ts_s, dest_s, gate_s, n8_s, off_s = _mix_route(
        x_sample.reshape(NS, D), an_s, rn_s, woa, wor, row(g_ffn_norm), wrt, br, low, NS)

    blocks, gmm_p, gmm_s, comb_p, comb_s = _piece_tables(
        n8_p[:, :, 0], off_p[:, :, 0], _tile_rows(tt), n8_s[:, :, 0], off_s[:, :, 0], bm, nblocks)
    ys = _moe_gmm(blocks, gmm_p, gmm_s, ts_p, ts_s, w_gate_up[l], w_down[l], bg, bu, bd, perm,
                  nblocks, bm, nbits_p, nbits_s)
    tr = lambda a: jnp.transpose(a, (0, 2, 1))
    y_p = _combine(*comb_p, ys, x2_p, tr(dest_p), tr(gate_p), tt, nbits_p)
    y_s = _combine(*comb_s, ys, x2_s, tr(dest_s), tr(gate_s), NS, nbits_s)

    kp4 = k.reshape(B, S, N_KV_HEADS, HEAD_DIM)[:, S - WINDOW:]
    vp4 = v.reshape(B, S, N_KV_HEADS, HEAD_DIM)[:, S - WINDOW:]
    cp = xr.reshape(B, S, D_RNN)[:, S - (CONV_WIDTH - 1):]
    ks4 = jnp.concatenate([cache_k[l][:, 1:], k_s.reshape(NS, 1, N_KV_HEADS, HEAD_DIM)], axis=1)
    vs4 = jnp.concatenate([cache_v[l][:, 1:], v_s.reshape(NS, 1, N_KV_HEADS, HEAD_DIM)], axis=1)
    cs_new = jnp.concatenate([state_conv[l][:, 1:], xr_s.reshape(NS, 1, D_RNN)], axis=1)
    return (y_p.reshape(B, S, D), y_s.reshape(NS, 1, D),
            kp4[None], vp4[None], cp[None], h_last_p.reshape(1, B, D_RNN),
            ks4[None], vs4[None], cs_new[None], h_last_s[None])


def kernel(x_prompt, x_sample, cache_k, cache_v, state_conv, state_h, g_mix_norm, w_in, g_q_norm, g_k_norm, attn_sinks, conv_w, conv_b, w_lru_a, b_lru_a, w_lru_x, b_lru_x, lru_lambda, g_attn_out, g_rnn_out, w_out, g_ffn_norm, w_router, b_router, w_gate_up, b_gate_up, w_down, b_down):
    return _step(x_prompt, x_sample, cache_k, cache_v, state_conv, state_h, g_mix_norm, w_in, g_q_norm,
                 g_k_norm, attn_sinks, conv_w, conv_b, w_lru_a, b_lru_a, w_lru_x, b_lru_x, lru_lambda,
                 g_attn_out, g_rnn_out, w_out, g_ffn_norm, w_router, b_router, w_gate_up, b_gate_up,
                 w_down, b_down, tm=512, tt=512, tc=256, bm=MOE_BLOCK_ROWS, past_len=PAST_LEN)
```

```python
import functools

import jax
import jax.numpy as jnp
from jax import lax
from jax.experimental import pallas as pl
from jax.experimental.pallas import tpu as pltpu

F32 = jnp.float32
BF16 = jnp.bfloat16
I32 = jnp.int32

D_MODEL = 1024
HEAD_DIM = 64
N_HEADS = 8
N_KV_HEADS = 2
GROUP = 4
WINDOW = 128
ATTN_BLOCK = 128
ROT_DIM = 16
ROPE_THETA = 500000.0
D_ATTN = 512
D_RNN = 512
KV_W = 128
D_IN = 1792
CONV_WIDTH = 4
LRU_C = 8.0
N_EXPERTS = 32
TOP_K = 4
D_FF = 1024
SWIGLU_LIMIT = 7.0
SWIGLU_ALPHA = 1.702
EPS = 1e-6
PAST_LEN = 8192

LANES = 128
SUBLANES = 8
NEG_BIG = -1e30
VMEM_LIMIT = 56 * 1024 * 1024

MOE_BLOCK_ROWS = 512
PERM_COLS = 256
DISPATCH_CHUNK = 256


def _cparams(n_axes):
    return pltpu.CompilerParams(dimension_semantics=("arbitrary",) * n_axes,
                                vmem_limit_bytes=VMEM_LIMIT)


def _rmsnorm(x, g):
    ms = jnp.mean(x * x, axis=-1, keepdims=True)
    return (x * lax.rsqrt(ms + EPS)) * g


def _in_proj_kernel(x_ref, g_ref, w_ref, gq_ref, gk_ref, c_ref, s1_ref, s2_ref,
                    q_ref, k_ref, v_ref, xr_ref, yr_ref):
    tm = x_ref.shape[0]
    h = _rmsnorm(x_ref[...], g_ref[...])
    proj = jnp.dot(h.astype(BF16), w_ref[...], preferred_element_type=F32)
    lo = lax.broadcasted_iota(I32, (tm, LANES), 1) < HEAD_DIM
    c = c_ref[...]
    s1 = s1_ref[...]
    s2 = s2_ref[...]

    def head_norm_rope(t, g):
        sq = t * t
        s_lo = jnp.sum(jnp.where(lo, sq, 0.0), axis=-1, keepdims=True)
        s_hi = jnp.sum(jnp.where(lo, 0.0, sq), axis=-1, keepdims=True)
        ms = jnp.where(lo, s_lo, s_hi) * (1.0 / HEAD_DIM)
        n = (t * lax.rsqrt(ms + EPS)) * g
        up = pltpu.roll(n, LANES - ROT_DIM // 2, 1)
        dn = pltpu.roll(n, ROT_DIM // 2, 1)
        return n * c + up * s1 + dn * s2

    gq = gq_ref[...]
    for j in range(D_ATTN // LANES):
        q_ref[:, j * LANES:(j + 1) * LANES] = head_norm_rope(proj[:, j * LANES:(j + 1) * LANES], gq)
    k_ref[...] = head_norm_rope(proj[:, D_ATTN:D_ATTN + KV_W], gk_ref[...])
    v_ref[...] = proj[:, D_ATTN + KV_W:D_ATTN + 2 * KV_W]
    o = D_ATTN + 2 * KV_W
    xr_ref[...] = proj[:, o:o + D_RNN]
    yr_ref[...] = proj[:, o + D_RNN:o + 2 * D_RNN]


def _in_proj(x2d, g, w_bf, gq2, gk2, ctab, s1tab, s2tab, tm):
    n = x2d.shape[0]
    ntab = ctab.shape[0] // tm
    row = lambda i: (i, 0)
    fix = lambda i: (0, 0)
    tab = lambda i: (i % ntab, 0)
    out_shapes = (jax.ShapeDtypeStruct((n, D_ATTN), F32), jax.ShapeDtypeStruct((n, KV_W), F32),
                  jax.ShapeDtypeStruct((n, KV_W), F32), jax.ShapeDtypeStruct((n, D_RNN), F32),
                  jax.ShapeDtypeStruct((n, D_RNN), F32))
    return pl.pallas_call(
        _in_proj_kernel,
        grid=(n // tm,),
        in_specs=[pl.BlockSpec((tm, D_MODEL), row), pl.BlockSpec((1, D_MODEL), fix),
                  pl.BlockSpec((D_MODEL, D_IN), fix), pl.BlockSpec((1, LANES), fix),
                  pl.BlockSpec((1, LANES), fix), pl.BlockSpec((tm, LANES), tab),
                  pl.BlockSpec((tm, LANES), tab), pl.BlockSpec((tm, LANES), tab)],
        out_specs=(pl.BlockSpec((tm, D_ATTN), row), pl.BlockSpec((tm, KV_W), row),
                   pl.BlockSpec((tm, KV_W), row), pl.BlockSpec((tm, D_RNN), row),
                   pl.BlockSpec((tm, D_RNN), row)),
        out_shape=out_shapes,
        compiler_params=_cparams(1),
        name="in_proj",
    )(x2d, g, w_bf, gq2, gk2, ctab, s1tab, s2tab)


def _rope_tables(pos):
    half = ROT_DIM // 2
    inv = ROPE_THETA ** (-jnp.arange(0, ROT_DIM, 2, dtype=F32) / ROT_DIM)
    ang = pos.astype(F32)[:, None] * inv[None, :]
    cos = jnp.cos(ang)
    sin = jnp.sin(ang)
    n = pos.shape[0]
    ones = jnp.ones((n, HEAD_DIM - ROT_DIM), F32)
    zeros = jnp.zeros((n, HEAD_DIM - ROT_DIM), F32)
    zh = jnp.zeros((n, half), F32)
    c = jnp.concatenate([cos, cos, ones], axis=1)
    s1 = jnp.concatenate([-sin, zh, zeros], axis=1)
    s2 = jnp.concatenate([zh, sin, zeros], axis=1)
    two = lambda t: jnp.concatenate([t, t], axis=1)
    return two(c), two(s1), two(s2)


def _band_bias(qb):
    qi = (jnp.arange(GROUP * qb, dtype=I32) % qb)[:, None]
    c = jnp.arange(2 * qb, dtype=I32)[None, :]
    band = (c >= qi) & (c <= qi + qb)
    first = band & (c >= qb)
    return jnp.where(jnp.stack([first, band]), 0.0, NEG_BIG).astype(F32)


def _sink_column(sink_ref, kv, rows, per_head):
    hrow = lax.broadcasted_iota(I32, (rows, 1), 0) >> (per_head.bit_length() - 1)
    sink = jnp.zeros((rows, 1), F32)
    for g in range(GROUP):
        sink = jnp.where(hrow == g, sink_ref[kv * GROUP + g], sink)
    return sink


def _attn_prompt_kernel(sink_ref, q_ref, kc_ref, kp_ref, vc_ref, vp_ref, bias_ref, g_ref,
                        o_ref, kt_ref, vt_ref, acc_ref):
    j = pl.program_id(1)
    qb = ATTN_BLOCK
    q = q_ref[...]
    kc = kc_ref[...]
    kp = kp_ref[...]
    vc = vc_ref[...]
    vp = vp_ref[...]
    bias = bias_ref[0]
    rows = GROUP * qb
    for kv in range(N_KV_HEADS):
        sl = slice(kv * HEAD_DIM, (kv + 1) * HEAD_DIM)
        kk = jnp.concatenate([kp[:, sl], kc[:, sl]], axis=0).astype(BF16)
        vv = jnp.concatenate([vp[:, sl], vc[:, sl]], axis=0).astype(BF16)
        qs = jnp.concatenate(
            [q[:, (kv * GROUP + g) * HEAD_DIM:(kv * GROUP + g + 1) * HEAD_DIM] for g in range(GROUP)],
            axis=0)
        qs = (qs * (HEAD_DIM ** -0.5)).astype(BF16)
        s = lax.dot_general(qs, kk, (((1,), (1,)), ((), ())), preferred_element_type=F32) + bias
        sink = _sink_column(sink_ref, kv, rows, qb)
        m = jnp.maximum(jnp.max(s, axis=-1, keepdims=True), sink)
        e = jnp.exp(s - m)
        denom = jnp.sum(e, axis=-1, keepdims=True) + jnp.exp(sink - m)
        o = jnp.dot(e.astype(BF16), vv, preferred_element_type=F32) * (1.0 / denom)
        for g in range(GROUP):
            hh = kv * GROUP + g
            acc_ref[:, hh * HEAD_DIM:(hh + 1) * HEAD_DIM] = o[g * qb:(g + 1) * qb, :]
    o_ref[...] = _rmsnorm(acc_ref[...], g_ref[...])

    @pl.when(j == pl.num_programs(1) - 1)
    def _():
        kt_ref[0] = kc.T
        vt_ref[0] = vc.T


def _attn_prompt(q, k, v, sinks, g_attn, batch, seq):
    qb = ATTN_BLOCK
    nb = seq // qb
    cur = lambda b, j: (b * nb + j, 0)
    prev = lambda b, j: (b * nb + jnp.maximum(j - 1, 0), 0)
    fix = lambda b, j: (0, 0)
    per_b = lambda b, j: (b, 0, 0)
    return pl.pallas_call(
        _attn_prompt_kernel,
        grid=(batch, nb),
        in_specs=[pl.BlockSpec(memory_space=pltpu.SMEM),
                  pl.BlockSpec((qb, D_ATTN), cur),
                  pl.BlockSpec((qb, KV_W), cur), pl.BlockSpec((qb, KV_W), prev),
                  pl.BlockSpec((qb, KV_W), cur), pl.BlockSpec((qb, KV_W), prev),
                  pl.BlockSpec((1, GROUP * qb, 2 * qb), lambda b, j: (jnp.minimum(j, 1), 0, 0)),
                  pl.BlockSpec((1, D_ATTN), fix)],
        out_specs=(pl.BlockSpec((qb, D_ATTN), cur), pl.BlockSpec((1, KV_W, qb), per_b),
                   pl.BlockSpec((1, KV_W, qb), per_b)),
        out_shape=(jax.ShapeDtypeStruct((batch * seq, D_ATTN), F32),
                   jax.ShapeDtypeStruct((batch, KV_W, qb), F32),
                   jax.ShapeDtypeStruct((batch, KV_W, qb), F32)),
        scratch_shapes=[pltpu.VMEM((qb, D_ATTN), F32)],
        compiler_params=_cparams(2),
        name="attn_prompt",
    )(sinks, q, k, k, v, v, _band_bias(qb), g_attn)


def _attn_sample_kernel(sink_ref, q_ref, kn_ref, vn_ref, kt_ref, vt_ref, g_ref,
                        o_ref, nkt_ref, nvt_ref, acc_ref):
    bb = q_ref.shape[0]
    q = q_ref[...] * (HEAD_DIM ** -0.5)
    kn = kn_ref[...]
    vn = vn_ref[...]
    kt = kt_ref[...]
    vt = vt_ref[...]
    col = lax.broadcasted_iota(I32, (bb, bb * KV_W), 1)
    rowb = lax.broadcasted_iota(I32, (bb, bb * KV_W), 0)
    own_seq = (col >> (KV_W.bit_length() - 1)) == rowb
    half_hi = ((col >> (HEAD_DIM.bit_length() - 1)) & 1) == 1
    qbig = []
    for h in range(N_HEADS):
        kv = h // GROUP
        pair = q[:, (h // 2) * LANES:(h // 2 + 1) * LANES]
        if (h % 2) != kv:
            pair = pltpu.roll(pair, HEAD_DIM, 1)
        tiled = jnp.concatenate([pair] * bb, axis=1)
        keep = own_seq & (half_hi if kv == 1 else jnp.logical_not(half_hi))
        qbig.append(jnp.where(keep, tiled, 0.0))
    qbig = jnp.concatenate(qbig, axis=0)
    s = jnp.dot(qbig.astype(BF16), kt.astype(BF16), preferred_element_type=F32)
    qb16 = q.astype(BF16).astype(F32)
    kb16 = kn.astype(BF16).astype(F32)
    s_new, sink = [], []
    for h in range(N_HEADS):
        kv = h // GROUP
        s_new.append(jnp.sum(qb16[:, h * HEAD_DIM:(h + 1) * HEAD_DIM] * kb16[:, kv * HEAD_DIM:(kv + 1) * HEAD_DIM],
                             axis=-1, keepdims=True))
        sink.append(jnp.full((bb, 1), sink_ref[h], F32))
    s_new = jnp.concatenate(s_new, axis=0)
    sink = jnp.concatenate(sink, axis=0)
    m = jnp.maximum(jnp.maximum(jnp.max(s, axis=-1, keepdims=True), s_new), sink)
    e = jnp.exp(s - m)
    e_new = jnp.exp(s_new - m)
    inv = 1.0 / (jnp.sum(e, axis=-1, keepdims=True) + e_new + jnp.exp(sink - m))
    obig = lax.dot_general(e.astype(BF16), vt.astype(BF16), (((1,), (1,)), ((), ())),
                           preferred_element_type=F32)
    for h in range(N_HEADS):
        kv = h // GROUP
        blk = jnp.where(own_seq, obig[h * bb:(h + 1) * bb, :], 0.0)
        fold = blk[:, 0:KV_W]
        for t in range(1, bb):
            fold = fold + blk[:, t * KV_W:(t + 1) * KV_W]
        hs = slice(h * bb, (h + 1) * bb)
        ks = slice(kv * HEAD_DIM, (kv + 1) * HEAD_DIM)
        acc_ref[:, h * HEAD_DIM:(h + 1) * HEAD_DIM] = (fold[:, ks] + e_new[hs] * vn[:, ks]) * inv[hs]
    o_ref[...] = _rmsnorm(acc_ref[...], g_ref[...])

    last = lax.broadcasted_iota(I32, (KV_W, WINDOW), 1) == WINDOW - 1
    for b in range(bb):
        rs = slice(b * KV_W, (b + 1) * KV_W)
        kcol = jnp.broadcast_to(kn[b:b + 1, :], (KV_W, KV_W)).T
        vcol = jnp.broadcast_to(vn[b:b + 1, :], (KV_W, KV_W)).T
        nkt_ref[rs, :] = jnp.where(last, kcol, pltpu.roll(kt[rs, :], WINDOW - 1, 1))
        nvt_ref[rs, :] = jnp.where(last, vcol, pltpu.roll(vt[rs, :], WINDOW - 1, 1))


def _attn_sample(q, kn, vn, kt2d, vt2d, sinks, g_attn, bb):
    n = q.shape[0]
    row = lambda i: (i, 0)
    fix = lambda i: (0, 0)
    cache = pl.BlockSpec((bb * KV_W, WINDOW), row)
    return pl.pallas_call(
        _attn_sample_kernel,
        grid=(n // bb,),
        in_specs=[pl.BlockSpec(memory_space=pltpu.SMEM),
                  pl.BlockSpec((bb, D_ATTN), row), pl.BlockSpec((bb, KV_W), row),
                  pl.BlockSpec((bb, KV_W), row), cache, cache,
                  pl.BlockSpec((1, D_ATTN), fix)],
        out_specs=(pl.BlockSpec((bb, D_ATTN), row), cache, cache),
        out_shape=(jax.ShapeDtypeStruct((n, D_ATTN), F32),
                   jax.ShapeDtypeStruct(kt2d.shape, F32), jax.ShapeDtypeStruct(vt2d.shape, F32)),
        scratch_shapes=[pltpu.VMEM((bb, D_ATTN), F32)],
        compiler_params=_cparams(1),
        name="attn_sample",
    )(sinks, q, kn, vn, kt2d, vt2d, g_attn)


def _softplus(z):
    return jnp.maximum(z, 0.0) + jnp.log1p(jnp.exp(-jnp.abs(z)))


def _lru_gates(xc, wa_ref, ba_ref, wx_ref, bx_ref, lam_ref):
    xb = xc.astype(BF16)
    r = jax.nn.sigmoid(jnp.dot(xb, wa_ref[...], preferred_element_type=F32) + ba_ref[...])
    i = jax.nn.sigmoid(jnp.dot(xb, wx_ref[...], preferred_element_type=F32) + bx_ref[...])
    log_a = (-LRU_C * r) * _softplus(-lam_ref[...])
    a = jnp.exp(log_a)
    u = jnp.sqrt(-jnp.tanh(log_a) * (a * a + 1.0)) * (i * xc)
    return a, u


def _rnn_prompt_kernel(xr_ref, yr_ref, cw_ref, cb_ref, wa_ref, ba_ref, wx_ref, bx_ref, lam_ref, g_ref,
                       o_ref, hl_ref, ext_ref, h_ref):
    c = pl.program_id(1)
    tc = xr_ref.shape[0]
    pad = SUBLANES

    @pl.when(c == 0)
    def _():
        ext_ref[0:pad, :] = jnp.zeros((pad, D_RNN), F32)
        h_ref[...] = jnp.zeros((1, D_RNN), F32)

    ext_ref[pad:pad + tc, :] = xr_ref[...]
    cw = cw_ref[...]
    xc = cb_ref[...] + ext_ref[pad:pad + tc, :] * cw[CONV_WIDTH - 1:CONV_WIDTH, :]
    for w in range(CONV_WIDTH - 1):
        sh = CONV_WIDTH - 1 - w
        xc = xc + ext_ref[pad - sh:pad - sh + tc, :] * cw[w:w + 1, :]
    ext_ref[0:pad, :] = ext_ref[tc:tc + pad, :]

    a, u = _lru_gates(xc, wa_ref, ba_ref, wx_ref, bx_ref, lam_ref)

    ng = tc // SUBLANES
    a3 = a.reshape(ng, SUBLANES, D_RNN)
    u3 = u.reshape(ng, SUBLANES, D_RNN)
    t8 = lax.broadcasted_iota(I32, (ng, SUBLANES, D_RNN), 1)
    d = 1
    while d < SUBLANES:
        a_s = jnp.where(t8 >= d, pltpu.roll(a3, d, 1), 1.0)
        u_s = jnp.where(t8 >= d, pltpu.roll(u3, d, 1), 0.0)
        u3 = a3 * u_s + u3
        a3 = a3 * a_s
        d *= 2
    carry = h_ref[...]
    groups = []
    for g in range(ng):
        hg = a3[g] * carry + u3[g]
        groups.append(hg)
        carry = hg[SUBLANES - 1:SUBLANES, :]
    h = jnp.concatenate(groups, axis=0)
    h_ref[...] = carry
    hl_ref[0] = carry
    o_ref[...] = _rmsnorm(jax.nn.gelu(yr_ref[...]) * h, g_ref[...])


def _rnn_prompt(xr, yr, cw, cb, wa, ba, wx, bx, lam, g, batch, seq, tc):
    nc = seq // tc
    cur = lambda b, c: (b * nc + c, 0)
    fix = lambda b, c: (0, 0)
    vec = pl.BlockSpec((1, D_RNN), fix)
    return pl.pallas_call(
        _rnn_prompt_kernel,
        grid=(batch, nc),
        in_specs=[pl.BlockSpec((tc, D_RNN), cur), pl.BlockSpec((tc, D_RNN), cur),
                  pl.BlockSpec((CONV_WIDTH, D_RNN), fix), vec,
                  pl.BlockSpec((D_RNN, D_RNN), fix), vec,
                  pl.BlockSpec((D_RNN, D_RNN), fix), vec, vec, vec],
        out_specs=(pl.BlockSpec((tc, D_RNN), cur), pl.BlockSpec((1, 1, D_RNN), lambda b, c: (b, 0, 0))),
        out_shape=(jax.ShapeDtypeStruct((batch * seq, D_RNN), F32),
                   jax.ShapeDtypeStruct((batch, 1, D_RNN), F32)),
        scratch_shapes=[pltpu.VMEM((tc + SUBLANES, D_RNN), F32), pltpu.VMEM((1, D_RNN), F32)],
        compiler_params=_cparams(2),
        name="rnn_prompt",
    )(xr, yr, cw, cb, wa, ba, wx, bx, lam, g)


def _rnn_sample_kernel(xr_ref, yr_ref, hist_ref, h0_ref, cw_ref, cb_ref, wa_ref, ba_ref, wx_ref, bx_ref,
                       lam_ref, g_ref, o_ref, hl_ref, nh_ref):
    cw = cw_ref[...]
    xr = xr_ref[...]
    xc = cb_ref[...] + xr * cw[CONV_WIDTH - 1:CONV_WIDTH, :]
    for w in range(CONV_WIDTH - 1):
        xc = xc + hist_ref[w] * cw[w:w + 1, :]
    a, u = _lru_gates(xc, wa_ref, ba_ref, wx_ref, bx_ref, lam_ref)
    h = a * h0_ref[...] + u
    hl_ref[...] = h
    o_ref[...] = _rmsnorm(jax.nn.gelu(yr_ref[...]) * h, g_ref[...])
    for w in range(CONV_WIDTH - 2):
        nh_ref[w] = hist_ref[w + 1]
    nh_ref[CONV_WIDTH - 2] = xr


def _rnn_sample(xr, yr, hist, h0, cw, cb, wa, ba, wx, bx, lam, g):
    n = xr.shape[0]
    full = lambda a: pl.BlockSpec(a.shape, lambda: (0,) * a.ndim)
    args = (xr, yr, hist, h0, cw, cb, wa, ba, wx, bx, lam, g)
    return pl.pallas_call(
        _rnn_sample_kernel,
        in_specs=[full(a) for a in args],
        out_specs=(pl.BlockSpec((n, D_RNN), lambda: (0, 0)), pl.BlockSpec((n, D_RNN), lambda: (0, 0)),
                   pl.BlockSpec(hist.shape, lambda: (0, 0, 0))),
        out_shape=(jax.ShapeDtypeStruct((n, D_RNN), F32), jax.ShapeDtypeStruct((n, D_RNN), F32),
                   jax.ShapeDtypeStruct(hist.shape, F32)),
        compiler_params=pltpu.CompilerParams(vmem_limit_bytes=VMEM_LIMIT),
        name="rnn_sample",
    )(*args)


def _split3(x):
    hi = x.astype(BF16)
    r1 = x - hi.astype(F32)
    mid = r1.astype(BF16)
    lo = (r1 - mid.astype(F32)).astype(BF16)
    return hi, mid, lo


def _mix_route_kernel(x_ref, an_ref, rn_ref, woa_ref, wor_ref, g_ref, wrt_ref, br_ref, tri_ref, low_ref,
                      x2_ref, ts_ref, dest_ref, gate_ref, n8_ref, off_ref):
    tt = x_ref.shape[0]
    tile_rows = ts_ref.shape[0]
    x2 = x_ref[...] + jnp.dot(an_ref[...].astype(BF16), woa_ref[...], preferred_element_type=F32) \
        + jnp.dot(rn_ref[...].astype(BF16), wor_ref[...], preferred_element_type=F32)
    x2_ref[...] = x2
    hn = _rmsnorm(x2, g_ref[...])

    nt = (((1,), (1,)), ((), ()))
    h3 = _split3(hn)
    w3 = _split3(wrt_ref[...])
    logits = br_ref[...]
    for a, b in ((2, 0), (1, 1), (0, 2), (1, 0), (0, 1), (0, 0)):
        logits = logits + lax.dot_general(w3[a], h3[b], nt, preferred_element_type=F32)

    ie = lax.broadcasted_iota(I32, (N_EXPERTS, tt), 0).astype(F32)
    l = logits
    vals, sels = [], []
    for _ in range(TOP_K):
        m = jnp.max(l, axis=0, keepdims=True)
        idx = jnp.min(jnp.where(l == m, ie, float(N_EXPERTS)), axis=0, keepdims=True)
        sel = ie == idx
        vals.append(m)
        sels.append(sel)
        l = jnp.where(sel, NEG_BIG, l)
    es = [jnp.exp(v - vals[0]) for v in vals]
    den = es[0] + es[1] + es[2] + es[3]
    gate_ref[0] = jnp.concatenate([e / den for e in es], axis=0)

    oh = jnp.zeros((N_EXPERTS, tt), F32)
    for sel in sels:
        oh = oh + jnp.where(sel, 1.0, 0.0)
    before = jnp.dot(oh.astype(BF16), tri_ref[...], preferred_element_type=F32)
    cnt = jnp.sum(oh, axis=1, keepdims=True).astype(I32)
    n8 = ((cnt + (SUBLANES - 1)) >> 3) << 3
    n8b = jnp.broadcast_to(n8, (N_EXPERTS, LANES))
    off = jnp.dot(low_ref[...], n8b.astype(F32).astype(BF16), preferred_element_type=F32)
    n8_ref[0] = n8b
    off_ref[0] = off.astype(I32)
    base = off[:, 0:1] + before
    dests = [jnp.sum(jnp.where(sel, base, 0.0), axis=0, keepdims=True).astype(I32) for sel in sels]
    dest_ref[0] = jnp.concatenate(dests, axis=0)

    hb = hn.astype(BF16)
    ri = lax.broadcasted_iota(I32, (DISPATCH_CHUNK, tt), 0)
    for c in range(tile_rows // DISPATCH_CHUNK):
        p = jnp.zeros((DISPATCH_CHUNK, tt), F32)
        for d in dests:
            p = jnp.where(ri == d - c * DISPATCH_CHUNK, 1.0, p)
        ts_ref[c * DISPATCH_CHUNK:(c + 1) * DISPATCH_CHUNK, :] = jnp.dot(
            p.astype(BF16), hb, preferred_element_type=F32)


def _tile_rows(tt):
    return -(-(TOP_K * tt + N_EXPERTS * (SUBLANES - 1)) // DISPATCH_CHUNK) * DISPATCH_CHUNK


def _mix_route(x2d, an, rn, woa, wor, g, wrt, br, low, tt):
    n = x2d.shape[0]
    nt = n // tt
    tile_rows = _tile_rows(tt)
    tri = jnp.triu(jnp.ones((tt, tt), BF16), k=1)
    row = lambda i: (i, 0)
    fix = lambda i: (0, 0)
    t3 = lambda i: (i, 0, 0)
    in_specs = [pl.BlockSpec((tt, D_MODEL), row), pl.BlockSpec((tt, D_ATTN), row),
                pl.BlockSpec((tt, D_RNN), row), pl.BlockSpec((D_ATTN, D_MODEL), fix),
                pl.BlockSpec((D_RNN, D_MODEL), fix), pl.BlockSpec((1, D_MODEL), fix),
                pl.BlockSpec((N_EXPERTS, D_MODEL), fix), pl.BlockSpec((N_EXPERTS, 1), fix),
                pl.BlockSpec((tt, tt), fix), pl.BlockSpec((N_EXPERTS, N_EXPERTS), fix)]
    out_shape = (jax.ShapeDtypeStruct((n, D_MODEL), F32),
                 jax.ShapeDtypeStruct((nt * tile_rows, D_MODEL), F32),
                 jax.ShapeDtypeStruct((nt, TOP_K, tt), I32),
                 jax.ShapeDtypeStruct((nt, TOP_K, tt), F32),
                 jax.ShapeDtypeStruct((nt, N_EXPERTS, LANES), I32),
                 jax.ShapeDtypeStruct((nt, N_EXPERTS, LANES), I32))
    out_specs = (pl.BlockSpec((tt, D_MODEL), row),
                 pl.BlockSpec((tile_rows, D_MODEL), row),
                 pl.BlockSpec((1, TOP_K, tt), t3), pl.BlockSpec((1, TOP_K, tt), t3),
                 pl.BlockSpec((1, N_EXPERTS, LANES), t3), pl.BlockSpec((1, N_EXPERTS, LANES), t3))
    return pl.pallas_call(
        _mix_route_kernel,
        grid=(nt,),
        in_specs=in_specs,
        out_specs=out_specs,
        out_shape=out_shape,
        compiler_params=_cparams(1),
        name="mix_route",
    )(x2d, an, rn, woa, wor, g, wrt, br, tri, low)


def _start_pieces(src_hbm, dst_buf, sem, p_lo, p_hi, psrc_ref, pdst_ref, plen_ref, nbits):
    def body(p, carry):
        l8 = plen_ref[p]

        @pl.when(l8 != 0)
        def _():
            s = psrc_ref[p]
            d = pdst_ref[p]
            for c in range(nbits):
                size = SUBLANES << c
                low = (l8 & ((1 << c) - 1)) * SUBLANES

                @pl.when(((l8 >> c) & 1) == 1)
                def _():
                    pltpu.make_async_copy(
                        src_hbm.at[pl.ds(pl.multiple_of(s + low, SUBLANES), size)],
                        dst_buf.at[pl.ds(pl.multiple_of(d + low, SUBLANES), size)], sem).start()
        return carry

    lax.fori_loop(p_lo, p_hi, body, 0)


def _wait_rows(src_hbm, dst_buf, sem, rows8, nbits):
    for c in range(nbits):
        size = SUBLANES << c

        @pl.when(((rows8 >> c) & 1) == 1)
        def _():
            pltpu.make_async_copy(src_hbm.at[pl.ds(0, size)], dst_buf.at[pl.ds(0, size)], sem).wait()


def _moe_gmm_kernel(be_ref, nact_ref, rows_ref, psa_ref, pea_ref, srca_ref, dsta_ref, lena_ref,
                    psb_ref, peb_ref, srcb_ref, dstb_ref, lenb_ref,
                    tsa_hbm, tsb_hbm, wgu_ref, wdn_ref, bg_ref, bu_ref, bd_ref, perm_ref,
                    ys_ref, lhs_ref, wg_ref, wu_ref, wd_ref, sem_ref, *, nbits_a, nbits_b):
    j = pl.program_id(0)
    nb = pl.num_programs(0)
    slot = j % 2
    bm = lhs_ref.shape[1]

    def gather(blk, sl):
        _start_pieces(tsa_hbm, lhs_ref.at[sl], sem_ref.at[sl], psa_ref[blk], pea_ref[blk],
                      srca_ref, dsta_ref, lena_ref, nbits_a)
        _start_pieces(tsb_hbm, lhs_ref.at[sl], sem_ref.at[sl], psb_ref[blk], peb_ref[blk],
                      srcb_ref, dstb_ref, lenb_ref, nbits_b)

    @pl.when(j == 0)
    def _():
        lhs_ref[...] = jnp.zeros(lhs_ref.shape, F32)
        gather(0, 0)

    @pl.when(j + 1 < nb)
    def _():
        gather(j + 1, 1 - slot)

    changed = jnp.logical_or(j == 0, be_ref[j] != be_ref[jnp.maximum(j - 1, 0)])

    @pl.when(changed)
    def _():
        perm = perm_ref[...]
        half = PERM_COLS // 2
        for c in range(2 * D_FF // PERM_COLS):
            wb = wgu_ref[0, :, c * PERM_COLS:(c + 1) * PERM_COLS].astype(BF16)
            wp = jnp.dot(wb, perm, preferred_element_type=F32).astype(BF16)
            wg_ref[:, c * half:(c + 1) * half] = wp[:, :half]
            wu_ref[:, c * half:(c + 1) * half] = wp[:, half:]
        wd_ref[...] = wdn_ref[0].astype(BF16)

    _wait_rows(tsa_hbm, lhs_ref.at[slot], sem_ref.at[slot], rows_ref[j], (bm // SUBLANES).bit_length())

    @pl.when(j < nact_ref[0])
    def _():
        x = lhs_ref[slot].astype(BF16)
        gate = jnp.dot(x, wg_ref[...], preferred_element_type=F32) + bg_ref[0]
        up = jnp.dot(x, wu_ref[...], preferred_element_type=F32) + bu_ref[0]
        gate = jnp.minimum(gate, SWIGLU_LIMIT)
        up = jnp.clip(up, -SWIGLU_LIMIT, SWIGLU_LIMIT)
        act = (up + 1.0) * (gate * jax.nn.sigmoid(SWIGLU_ALPHA * gate))
        ys_ref[...] = jnp.dot(act.astype(BF16), wd_ref[...], preferred_element_type=F32) + bd_ref[0]

    @pl.when(j >= nact_ref[0])
    def _():
        ys_ref[...] = jnp.zeros(ys_ref.shape, F32)


def _moe_gmm(blocks, tabs_a, tabs_b, ts_a, ts_b, w_gu, w_dn, bg, bu, bd, perm, nblocks, bm, nbits_a, nbits_b):
    we = lambda j, be, *_: (be[j], 0, 0)
    grid_spec = pltpu.PrefetchScalarGridSpec(
        num_scalar_prefetch=13,
        grid=(nblocks,),
        in_specs=[pl.BlockSpec(memory_space=pl.ANY), pl.BlockSpec(memory_space=pl.ANY),
                  pl.BlockSpec((1, D_MODEL, 2 * D_FF), we),
                  pl.BlockSpec((1, D_FF, D_MODEL), we),
                  pl.BlockSpec((1, 1, D_FF), we), pl.BlockSpec((1, 1, D_FF), we),
                  pl.BlockSpec((1, 1, D_MODEL), we),
                  pl.BlockSpec((PERM_COLS, PERM_COLS), lambda j, *_: (0, 0))],
        out_specs=pl.BlockSpec((bm, D_MODEL), lambda j, *_: (j, 0)),
        scratch_shapes=[pltpu.VMEM((2, bm, D_MODEL), F32),
                        pltpu.VMEM((D_MODEL, D_FF), BF16), pltpu.VMEM((D_MODEL, D_FF), BF16),
                        pltpu.VMEM((D_FF, D_MODEL), BF16),
                        pltpu.SemaphoreType.DMA((2,))],
    )
    return pl.pallas_call(
        functools.partial(_moe_gmm_kernel, nbits_a=nbits_a, nbits_b=nbits_b),
        grid_spec=grid_spec,
        out_shape=jax.ShapeDtypeStruct((nblocks * bm, D_MODEL), F32),
        compiler_params=_cparams(1),
        name="moe_gmm",
    )(*blocks, *tabs_a, *tabs_b, ts_a, ts_b, w_gu, w_dn, bg, bu, bd, perm)


def _combine_kernel(psrc_ref, pdst_ref, plen_ref, rows_ref, ys_hbm, x2_ref, dest_ref, gate_ref, o_ref,
                    buf_ref, db_ref, gb_ref, sem_ref, *, pieces_per_tile, nbits):
    i = pl.program_id(0)
    n = pl.num_programs(0)
    slot = i % 2
    tt = x2_ref.shape[0]
    tile_rows = buf_ref.shape[1]

    def gather(tile, sl):
        p0 = tile * pieces_per_tile
        _start_pieces(ys_hbm, buf_ref.at[sl], sem_ref.at[sl], p0, p0 + pieces_per_tile,
                      psrc_ref, pdst_ref, plen_ref, nbits)

    @pl.when(i == 0)
    def _():
        buf_ref[...] = jnp.zeros(buf_ref.shape, F32)
        gather(0, 0)

    @pl.when(i + 1 < n)
    def _():
        gather(i + 1, 1 - slot)

    _wait_rows(ys_hbm, buf_ref.at[slot], sem_ref.at[slot], rows_ref[i], (tile_rows // SUBLANES).bit_length())

    dest = dest_ref[0]
    gate = gate_ref[0]
    for k in range(TOP_K):
        db_ref[k] = jnp.broadcast_to(dest[:, k:k + 1], (tt, DISPATCH_CHUNK))
        gb_ref[k] = jnp.broadcast_to(gate[:, k:k + 1], (tt, DISPATCH_CHUNK))
    acc = x2_ref[...]
    li = lax.broadcasted_iota(I32, (tt, DISPATCH_CHUNK), 1)
    for c in range(tile_rows // DISPATCH_CHUNK):
        lic = li + c * DISPATCH_CHUNK
        gm = jnp.zeros((tt, DISPATCH_CHUNK), F32)
        for k in range(TOP_K):
            gm = jnp.where(lic == db_ref[k], gb_ref[k], gm)
        rows = buf_ref[slot, c * DISPATCH_CHUNK:(c + 1) * DISPATCH_CHUNK, :].astype(BF16)
        acc = acc + jnp.dot(gm.astype(BF16), rows, preferred_element_type=F32)
    o_ref[...] = acc


def _combine(psrc, pdst, plen8, rows8, ys, x2, dest, gate, tt, nbits):
    n = x2.shape[0]
    nt = n // tt
    tile_rows = _tile_rows(tt)
    pieces_per_tile = 2 * N_EXPERTS
    grid_spec = pltpu.PrefetchScalarGridSpec(
        num_scalar_prefetch=4,
        grid=(nt,),
        in_specs=[pl.BlockSpec(memory_space=pl.ANY),
                  pl.BlockSpec((tt, D_MODEL), lambda i, *_: (i, 0)),
                  pl.BlockSpec((1, tt, TOP_K), lambda i, *_: (i, 0, 0)),
                  pl.BlockSpec((1, tt, TOP_K), lambda i, *_: (i, 0, 0))],
        out_specs=pl.BlockSpec((tt, D_MODEL), lambda i, *_: (i, 0)),
        scratch_shapes=[pltpu.VMEM((2, tile_rows, D_MODEL), F32),
                        pltpu.VMEM((TOP_K, tt, DISPATCH_CHUNK), I32), pltpu.VMEM((TOP_K, tt, DISPATCH_CHUNK), F32),
                        pltpu.SemaphoreType.DMA((2,))],
    )
    return pl.pallas_call(
        functools.partial(_combine_kernel, pieces_per_tile=pieces_per_tile, nbits=nbits),
        grid_spec=grid_spec,
        out_shape=jax.ShapeDtypeStruct((n, D_MODEL), F32),
        compiler_params=_cparams(1),
        name="combine",
    )(psrc, pdst, plen8, rows8, ys, x2, dest, gate)


def _piece_tables(n8_a, off_a, rows_a, n8_b, off_b, bm, nblocks):
    nta = n8_a.shape[0]
    n8 = jnp.concatenate([n8_a, n8_b], axis=0)
    seg_off = jnp.concatenate([off_a, off_b], axis=0)
    n_tiles = n8.shape[0]
    tile_base = jnp.concatenate([jnp.arange(nta, dtype=I32) * rows_a, jnp.zeros((n_tiles - nta,), I32)])[:, None]
    tot = jnp.sum(n8, axis=0)
    pos0 = jnp.cumsum(n8, axis=0) - n8
    nblk = (tot + bm - 1) // bm
    cs = jnp.cumsum(nblk)
    bs = cs - nblk
    kblk = pos0 // bm
    len0 = jnp.minimum(n8, (kblk + 1) * bm - pos0)
    len1 = n8 - len0
    b0 = bs[None, :] + kblk
    blk = jnp.stack([b0, jnp.where(len1 > 0, b0 + 1, b0)], axis=-1)
    plen8 = jnp.stack([len0, len1], axis=-1) // SUBLANES
    ts_row = jnp.stack([tile_base + seg_off, tile_base + seg_off + len0], axis=-1)
    in_blk = jnp.stack([pos0 - kblk * bm, jnp.zeros_like(pos0)], axis=-1)
    buf_row = jnp.stack([seg_off, seg_off + len0], axis=-1)
    ys_row = blk * bm + in_blk

    em = lambda a: jnp.transpose(a, (1, 0, 2)).reshape(-1).astype(I32)
    tm = lambda a: a.reshape(-1).astype(I32)
    jj = jnp.arange(nblocks, dtype=I32)

    def gmm_tabs(sl):
        blk_em = em(blk[sl])
        first = jnp.sum(blk_em[None, :] < jj[:, None], axis=1).astype(I32)
        last = jnp.sum(blk_em[None, :] <= jj[:, None], axis=1).astype(I32)
        return first, last, em(ts_row[sl]), em(in_blk[sl]), em(plen8[sl])

    def comb_tabs(sl):
        return tm(ys_row[sl]), tm(buf_row[sl]), tm(plen8[sl]), (jnp.sum(n8[sl], axis=1) // SUBLANES).astype(I32)

    count_le = lambda v: jnp.sum(cs[None, :] <= v[:, None], axis=1)
    n_active = cs[-1]
    e_last = count_le(jnp.maximum(n_active - 1, 0).reshape(1))[0]
    block_e = jnp.minimum(count_le(jj), e_last).astype(I32)
    rows8 = jnp.clip(tot[block_e] - (jj - bs[block_e]) * bm, 0, bm) // SUBLANES
    rows8 = jnp.where(jj < n_active, rows8, 0).astype(I32)
    a, b = slice(0, nta), slice(nta, n_tiles)
    return ((block_e, n_active.reshape(1).astype(I32), rows8), gmm_tabs(a), gmm_tabs(b),
            comb_tabs(a), comb_tabs(b))


def _block_diag(w):
    nb, bi, bo = w.shape
    eye = jnp.eye(nb, dtype=w.dtype)
    return (eye[:, None, :, None] * w[:, :, None, :]).reshape(nb * bi, nb * bo)


def _step(x_prompt, x_sample, cache_k, cache_v, state_conv, state_h, g_mix_norm, w_in, g_q_norm, g_k_norm,
          attn_sinks, conv_w, conv_b, w_lru_a, b_lru_a, w_lru_x, b_lru_x, lru_lambda, g_attn_out, g_rnn_out,
          w_out, g_ffn_norm, w_router, b_router, w_gate_up, b_gate_up, w_down, b_down,
          *, tm, tt, tc, bm, past_len):
    B, S, D = x_prompt.shape
    NS = x_sample.shape[0]
    assert x_sample.shape[1] == 1 and D == D_MODEL
    assert (B * S) % tt == 0 and (B * S) % tm == 0 and S % tc == 0 and S % ATTN_BLOCK == 0 and S % tm == 0
    assert NS % SUBLANES == 0 and tt <= bm
    assert tt % SUBLANES == 0 and NS <= bm
    n_pt = (B * S) // tt
    total_rows = TOP_K * (B * S + NS) + (n_pt + 1) * N_EXPERTS * (SUBLANES - 1)
    nblocks = -(-total_rows // bm) + N_EXPERTS
    nbits_p = (tt // SUBLANES).bit_length()
    nbits_s = (NS // SUBLANES).bit_length()

    l = 0
    row = lambda v: v[l].reshape(1, -1)
    w_in_bf = w_in[l].astype(BF16)
    gq2 = jnp.tile(g_q_norm[l], 2).reshape(1, LANES)
    gk2 = jnp.tile(g_k_norm[l], 2).reshape(1, LANES)
    wa = _block_diag(w_lru_a[l]).astype(BF16)
    wx = _block_diag(w_lru_x[l]).astype(BF16)
    ba = b_lru_a[l].reshape(1, D_RNN)
    bx = b_lru_x[l].reshape(1, D_RNN)
    wo = w_out[l].astype(BF16)
    woa, wor = wo[:D_ATTN], wo[D_ATTN:]
    wrt = w_router[l].T
    br = b_router[l].reshape(N_EXPERTS, 1)
    low = jnp.tril(jnp.ones((N_EXPERTS, N_EXPERTS), BF16), k=-1)
    bgu = b_gate_up[l].reshape(N_EXPERTS, D_FF, 2)
    bg = bgu[:, :, 0].reshape(N_EXPERTS, 1, D_FF)
    bu = bgu[:, :, 1].reshape(N_EXPERTS, 1, D_FF)
    bd = b_down[l].reshape(N_EXPERTS, 1, D_MODEL)
    half = PERM_COLS // 2
    pr = jnp.arange(PERM_COLS)
    perm = (pr[None, :] == jnp.where(pr % 2 == 0, pr // 2, half + pr // 2)[:, None]).astype(BF16)
    sinks = attn_sinks[l]

    ctab, s1tab, s2tab = _rope_tables(jnp.arange(S))
    q, k, v, xr, yr = _in_proj(x_prompt.reshape(B * S, D), row(g_mix_norm), w_in_bf, gq2, gk2,
                               ctab, s1tab, s2tab, tm)
    an, kt_p, vt_p = _attn_prompt(q, k, v, sinks, row(g_attn_out), B, S)
    rn, h_last_p = _rnn_prompt(xr, yr, conv_w[l], row(conv_b), wa, ba, wx, bx, row(lru_lambda),
                               row(g_rnn_out), B, S, tc)
    x2_p, ts_p, dest_p, gate_p, n8_p, off_p = _mix_route(
        x_prompt.reshape(B * S, D), an, rn, woa, wor, row(g_ffn_norm), wrt, br, low, tt)

    cs_tab = _rope_tables(jnp.full((NS,), past_len, I32))
    q_s, k_s, v_s, xr_s, yr_s = _in_proj(x_sample.reshape(NS, D), row(g_mix_norm), w_in_bf, gq2, gk2,
                                         *cs_tab, NS)
    to_rows = lambda c: jnp.transpose(c, (0, 2, 3, 1)).reshape(NS * KV_W, WINDOW)
    from_rows = lambda c, n: jnp.transpose(c.reshape(n, N_KV_HEADS, HEAD_DIM, WINDOW), (0, 3, 1, 2))[None]
    an_s, kt_s, vt_s = _attn_sample(q_s, k_s, v_s, to_rows(cache_k[l]), to_rows(cache_v[l]), sinks,
                                    row(g_attn_out), SUBLANES)
    rn_s, h_last_s, hist_s = _rnn_sample(xr_s, yr_s, jnp.transpose(state_conv[l], (1, 0, 2)), state_h[l],
                                         conv_w[l], row(conv_b), wa, ba, wx, bx, row(lru_lambda),
                                         row(g_rnn_out))
    x2_s, ts_s, dest_s, gate_s, n8_s, off_s = _mix_route(
        x_sample.reshape(NS, D), an_s, rn_s, woa, wor, row(g_ffn_norm), wrt, br, low, NS)

    blocks, gmm_p, gmm_s, comb_p, comb_s = _piece_tables(
        n8_p[:, :, 0], off_p[:, :, 0], _tile_rows(tt), n8_s[:, :, 0], off_s[:, :, 0], bm, nblocks)
    ys = _moe_gmm(blocks, gmm_p, gmm_s, ts_p, ts_s, w_gate_up[l], w_down[l], bg, bu, bd, perm,
                  nblocks, bm, nbits_p, nbits_s)
    tr = lambda a: jnp.transpose(a, (0, 2, 1))
    y_p = _combine(*comb_p, ys, x2_p, tr(dest_p), tr(gate_p), tt, nbits_p)
    y_s = _combine(*comb_s, ys, x2_s, tr(dest_s), tr(gate_s), NS, nbits_s)

    cp = xr.reshape(B, S, D_RNN)[:, S - (CONV_WIDTH - 1):]
    return (y_p.reshape(B, S, D), y_s.reshape(NS, 1, D),
            from_rows(kt_p, B), from_rows(vt_p, B), cp[None], h_last_p.reshape(1, B, D_RNN),
            from_rows(kt_s, NS), from_rows(vt_s, NS), jnp.transpose(hist_s, (1, 0, 2))[None], h_last_s[None])


def kernel(x_prompt, x_sample, cache_k, cache_v, state_conv, state_h, g_mix_norm, w_in, g_q_norm, g_k_norm, attn_sinks, conv_w, conv_b, w_lru_a, b_lru_a, w_lru_x, b_lru_x, lru_lambda, g_attn_out, g_rnn_out, w_out, g_ffn_norm, w_router, b_router, w_gate_up, b_gate_up, w_down, b_down):
    return _step(x_prompt, x_sample, cache_k, cache_v, state_conv, state_h, g_mix_norm, w_in, g_q_norm,
                 g_k_norm, attn_sinks, conv_w, conv_b, w_lru_a, b_lru_a, w_lru_x, b_lru_x, lru_lambda,
                 g_attn_out, g_rnn_out, w_out, g_ffn_norm, w_router, b_router, w_gate_up, b_gate_up,
                 w_down, b_down, tm=512, tt=512, tc=256, bm=MOE_BLOCK_ROWS, past_len=PAST_LEN)
```

```python
import functools

import jax
import jax.numpy as jnp
from jax import lax
from jax.experimental import pallas as pl
from jax.experimental.pallas import tpu as pltpu

F32 = jnp.float32
BF16 = jnp.bfloat16
I32 = jnp.int32

D_MODEL = 1024
HEAD_DIM = 64
N_HEADS = 8
N_KV_HEADS = 2
GROUP = 4
WINDOW = 128
ATTN_BLOCK = 128
ROT_DIM = 16
ROPE_THETA = 500000.0
D_ATTN = 512
D_RNN = 512
KV_W = 128
D_IN = 1792
CONV_WIDTH = 4
LRU_C = 8.0
N_EXPERTS = 32
TOP_K = 4
D_FF = 1024
SWIGLU_LIMIT = 7.0
SWIGLU_ALPHA = 1.702
EPS = 1e-6
PAST_LEN = 8192

LANES = 128
SUBLANES = 8
NEG_BIG = -1e30
VMEM_LIMIT = 56 * 1024 * 1024

MOE_BLOCK_ROWS = 512
PERM_COLS = 256
DISPATCH_CHUNK = 256


def _cparams(n_axes):
    return pltpu.CompilerParams(dimension_semantics=("arbitrary",) * n_axes,
                                vmem_limit_bytes=VMEM_LIMIT)


def _rmsnorm(x, g):
    ms = jnp.mean(x * x, axis=-1, keepdims=True)
    return (x * lax.rsqrt(ms + EPS)) * g


def _in_proj_kernel(x_ref, g_ref, w_ref, gq_ref, gk_ref, c_ref, s1_ref, s2_ref,
                    q_ref, k_ref, v_ref, xr_ref, yr_ref):
    tm = x_ref.shape[0]
    h = _rmsnorm(x_ref[...], g_ref[...])
    proj = jnp.dot(h.astype(BF16), w_ref[...], preferred_element_type=F32)
    lo = lax.broadcasted_iota(I32, (tm, LANES), 1) < HEAD_DIM
    c = c_ref[...]
    s1 = s1_ref[...]
    s2 = s2_ref[...]

    def head_norm_rope(t, g):
        sq = t * t
        s_lo = jnp.sum(jnp.where(lo, sq, 0.0), axis=-1, keepdims=True)
        s_hi = jnp.sum(jnp.where(lo, 0.0, sq), axis=-1, keepdims=True)
        ms = jnp.where(lo, s_lo, s_hi) * (1.0 / HEAD_DIM)
        n = (t * lax.rsqrt(ms + EPS)) * g
        up = pltpu.roll(n, LANES - ROT_DIM // 2, 1)
        dn = pltpu.roll(n, ROT_DIM // 2, 1)
        return n * c + up * s1 + dn * s2

    gq = gq_ref[...]
    for j in range(D_ATTN // LANES):
        q_ref[:, j * LANES:(j + 1) * LANES] = head_norm_rope(proj[:, j * LANES:(j + 1) * LANES], gq)
    k_ref[...] = head_norm_rope(proj[:, D_ATTN:D_ATTN + KV_W], gk_ref[...])
    v_ref[...] = proj[:, D_ATTN + KV_W:D_ATTN + 2 * KV_W]
    o = D_ATTN + 2 * KV_W
    xr_ref[...] = proj[:, o:o + D_RNN]
    yr_ref[...] = proj[:, o + D_RNN:o + 2 * D_RNN]


def _in_proj(x2d, g, w_bf, gq2, gk2, ctab, s1tab, s2tab, tm):
    n = x2d.shape[0]
    ntab = ctab.shape[0] // tm
    row = lambda i: (i, 0)
    fix = lambda i: (0, 0)
    tab = lambda i: (i % ntab, 0)
    out_shapes = (jax.ShapeDtypeStruct((n, D_ATTN), F32), jax.ShapeDtypeStruct((n, KV_W), F32),
                  jax.ShapeDtypeStruct((n, KV_W), F32), jax.ShapeDtypeStruct((n, D_RNN), F32),
                  jax.ShapeDtypeStruct((n, D_RNN), F32))
    return pl.pallas_call(
        _in_proj_kernel,
        grid=(n // tm,),
        in_specs=[pl.BlockSpec((tm, D_MODEL), row), pl.BlockSpec((1, D_MODEL), fix),
                  pl.BlockSpec((D_MODEL, D_IN), fix), pl.BlockSpec((1, LANES), fix),
                  pl.BlockSpec((1, LANES), fix), pl.BlockSpec((tm, LANES), tab),
                  pl.BlockSpec((tm, LANES), tab), pl.BlockSpec((tm, LANES), tab)],
        out_specs=(pl.BlockSpec((tm, D_ATTN), row), pl.BlockSpec((tm, KV_W), row),
                   pl.BlockSpec((tm, KV_W), row), pl.BlockSpec((tm, D_RNN), row),
                   pl.BlockSpec((tm, D_RNN), row)),
        out_shape=out_shapes,
        compiler_params=_cparams(1),
        name="in_proj",
    )(x2d, g, w_bf, gq2, gk2, ctab, s1tab, s2tab)


def _rope_tables(pos):
    half = ROT_DIM // 2
    inv = ROPE_THETA ** (-jnp.arange(0, ROT_DIM, 2, dtype=F32) / ROT_DIM)
    ang = pos.astype(F32)[:, None] * inv[None, :]
    cos = jnp.cos(ang)
    sin = jnp.sin(ang)
    n = pos.shape[0]
    ones = jnp.ones((n, HEAD_DIM - ROT_DIM), F32)
    zeros = jnp.zeros((n, HEAD_DIM - ROT_DIM), F32)
    zh = jnp.zeros((n, half), F32)
    c = jnp.concatenate([cos, cos, ones], axis=1)
    s1 = jnp.concatenate([-sin, zh, zeros], axis=1)
    s2 = jnp.concatenate([zh, sin, zeros], axis=1)
    two = lambda t: jnp.concatenate([t, t], axis=1)
    return two(c), two(s1), two(s2)


def _band_bias(qb):
    qi = jnp.arange(qb, dtype=I32)[:, None]
    c = jnp.arange(2 * qb, dtype=I32)[None, :]
    band = (c >= qi) & (c <= qi + qb)
    first = band & (c >= qb)
    return jnp.where(jnp.stack([first, band]), 0.0, NEG_BIG).astype(F32)


def _attn_prompt_kernel(sink_ref, q_ref, kc_ref, kp_ref, vc_ref, vp_ref, bias_ref, g_ref,
                        o_ref, kt_ref, vt_ref, acc_ref):
    j = pl.program_id(1)
    q = q_ref[...] * (HEAD_DIM ** -0.5)
    kc = kc_ref[...]
    kp = kp_ref[...]
    vc = vc_ref[...]
    vp = vp_ref[...]
    bias = bias_ref[0]
    for kv in range(N_KV_HEADS):
        sl = slice(kv * HEAD_DIM, (kv + 1) * HEAD_DIM)
        kk = jnp.concatenate([kp[:, sl], kc[:, sl]], axis=0).astype(BF16)
        vv = jnp.concatenate([vp[:, sl], vc[:, sl]], axis=0).astype(BF16)
        for g in range(GROUP):
            hs = slice((kv * GROUP + g) * HEAD_DIM, (kv * GROUP + g + 1) * HEAD_DIM)
            s = lax.dot_general(q[:, hs].astype(BF16), kk, (((1,), (1,)), ((), ())),
                                preferred_element_type=F32) + bias
            sink = sink_ref[kv * GROUP + g]
            m = jnp.maximum(jnp.max(s, axis=-1, keepdims=True), sink)
            e = jnp.exp(s - m)
            denom = jnp.sum(e, axis=-1, keepdims=True) + jnp.exp(sink - m)
            acc_ref[:, hs] = jnp.dot(e.astype(BF16), vv, preferred_element_type=F32) * (1.0 / denom)
    o_ref[...] = _rmsnorm(acc_ref[...], g_ref[...])

    @pl.when(j == pl.num_programs(1) - 1)
    def _():
        kt_ref[0] = kc.T
        vt_ref[0] = vc.T


def _attn_prompt(q, k, v, sinks, g_attn, batch, seq):
    qb = ATTN_BLOCK
    nb = seq // qb
    cur = lambda b, j: (b * nb + j, 0)
    prev = lambda b, j: (b * nb + jnp.maximum(j - 1, 0), 0)
    fix = lambda b, j: (0, 0)
    per_b = lambda b, j: (b, 0, 0)
    return pl.pallas_call(
        _attn_prompt_kernel,
        grid=(batch, nb),
        in_specs=[pl.BlockSpec(memory_space=pltpu.SMEM),
                  pl.BlockSpec((qb, D_ATTN), cur),
                  pl.BlockSpec((qb, KV_W), cur), pl.BlockSpec((qb, KV_W), prev),
                  pl.BlockSpec((qb, KV_W), cur), pl.BlockSpec((qb, KV_W), prev),
                  pl.BlockSpec((1, qb, 2 * qb), lambda b, j: (jnp.minimum(j, 1), 0, 0)),
                  pl.BlockSpec((1, D_ATTN), fix)],
        out_specs=(pl.BlockSpec((qb, D_ATTN), cur), pl.BlockSpec((1, KV_W, qb), per_b),
                   pl.BlockSpec((1, KV_W, qb), per_b)),
        out_shape=(jax.ShapeDtypeStruct((batch * seq, D_ATTN), F32),
                   jax.ShapeDtypeStruct((batch, KV_W, qb), F32),
                   jax.ShapeDtypeStruct((batch, KV_W, qb), F32)),
        scratch_shapes=[pltpu.VMEM((qb, D_ATTN), F32)],
        compiler_params=_cparams(2),
        name="attn_prompt",
    )(sinks, q, k, k, v, v, _band_bias(qb), g_attn)


def _attn_sample_kernel(sink_ref, q_ref, kn_ref, vn_ref, kt_ref, vt_ref, g_ref,
                        o_ref, nkt_ref, nvt_ref, acc_ref):
    bb = q_ref.shape[0]
    q = q_ref[...] * (HEAD_DIM ** -0.5)
    kn = kn_ref[...]
    vn = vn_ref[...]
    kt = kt_ref[...]
    vt = vt_ref[...]
    col = lax.broadcasted_iota(I32, (bb, bb * KV_W), 1)
    rowb = lax.broadcasted_iota(I32, (bb, bb * KV_W), 0)
    own_seq = (col >> (KV_W.bit_length() - 1)) == rowb
    half_hi = ((col >> (HEAD_DIM.bit_length() - 1)) & 1) == 1
    qbig = []
    for h in range(N_HEADS):
        kv = h // GROUP
        pair = q[:, (h // 2) * LANES:(h // 2 + 1) * LANES]
        if (h % 2) != kv:
            pair = pltpu.roll(pair, HEAD_DIM, 1)
        tiled = jnp.concatenate([pair] * bb, axis=1)
        keep = own_seq & (half_hi if kv == 1 else jnp.logical_not(half_hi))
        qbig.append(jnp.where(keep, tiled, 0.0))
    qbig = jnp.concatenate(qbig, axis=0)
    s = jnp.dot(qbig.astype(BF16), kt.astype(BF16), preferred_element_type=F32)
    qb16 = q.astype(BF16).astype(F32)
    kb16 = kn.astype(BF16).astype(F32)
    s_new, sink = [], []
    for h in range(N_HEADS):
        kv = h // GROUP
        s_new.append(jnp.sum(qb16[:, h * HEAD_DIM:(h + 1) * HEAD_DIM] * kb16[:, kv * HEAD_DIM:(kv + 1) * HEAD_DIM],
                             axis=-1, keepdims=True))
        sink.append(jnp.full((bb, 1), sink_ref[h], F32))
    s_new = jnp.concatenate(s_new, axis=0)
    sink = jnp.concatenate(sink, axis=0)
    m = jnp.maximum(jnp.maximum(jnp.max(s, axis=-1, keepdims=True), s_new), sink)
    e = jnp.exp(s - m)
    e_new = jnp.exp(s_new - m)
    inv = 1.0 / (jnp.sum(e, axis=-1, keepdims=True) + e_new + jnp.exp(sink - m))
    obig = lax.dot_general(e.astype(BF16), vt.astype(BF16), (((1,), (1,)), ((), ())),
                           preferred_element_type=F32)
    for h in range(N_HEADS):
        kv = h // GROUP
        blk = jnp.where(own_seq, obig[h * bb:(h + 1) * bb, :], 0.0)
        fold = blk[:, 0:KV_W]
        for t in range(1, bb):
            fold = fold + blk[:, t * KV_W:(t + 1) * KV_W]
        hs = slice(h * bb, (h + 1) * bb)
        ks = slice(kv * HEAD_DIM, (kv + 1) * HEAD_DIM)
        acc_ref[:, h * HEAD_DIM:(h + 1) * HEAD_DIM] = (fold[:, ks] + e_new[hs] * vn[:, ks]) * inv[hs]
    o_ref[...] = _rmsnorm(acc_ref[...], g_ref[...])

    last = lax.broadcasted_iota(I32, (KV_W, WINDOW), 1) == WINDOW - 1
    for b in range(bb):
        rs = slice(b * KV_W, (b + 1) * KV_W)
        kcol = jnp.broadcast_to(kn[b:b + 1, :], (KV_W, KV_W)).T
        vcol = jnp.broadcast_to(vn[b:b + 1, :], (KV_W, KV_W)).T
        nkt_ref[rs, :] = jnp.where(last, kcol, pltpu.roll(kt[rs, :], WINDOW - 1, 1))
        nvt_ref[rs, :] = jnp.where(last, vcol, pltpu.roll(vt[rs, :], WINDOW - 1, 1))


def _attn_sample(q, kn, vn, kt2d, vt2d, sinks, g_attn, bb):
    n = q.shape[0]
    row = lambda i: (i, 0)
    fix = lambda i: (0, 0)
    cache = pl.BlockSpec((bb * KV_W, WINDOW), row)
    return pl.pallas_call(
        _attn_sample_kernel,
        grid=(n // bb,),
        in_specs=[pl.BlockSpec(memory_space=pltpu.SMEM),
                  pl.BlockSpec((bb, D_ATTN), row), pl.BlockSpec((bb, KV_W), row),
                  pl.BlockSpec((bb, KV_W), row), cache, cache,
                  pl.BlockSpec((1, D_ATTN), fix)],
        out_specs=(pl.BlockSpec((bb, D_ATTN), row), cache, cache),
        out_shape=(jax.ShapeDtypeStruct((n, D_ATTN), F32),
                   jax.ShapeDtypeStruct(kt2d.shape, F32), jax.ShapeDtypeStruct(vt2d.shape, F32)),
        scratch_shapes=[pltpu.VMEM((bb, D_ATTN), F32)],
        compiler_params=_cparams(1),
        name="attn_sample",
    )(sinks, q, kn, vn, kt2d, vt2d, g_attn)


def _softplus(z):
    return jnp.maximum(z, 0.0) + jnp.log1p(jnp.exp(-jnp.abs(z)))


def _lru_gates(xc, wa_ref, ba_ref, wx_ref, bx_ref, lam_ref):
    xb = xc.astype(BF16)
    r = jax.nn.sigmoid(jnp.dot(xb, wa_ref[...], preferred_element_type=F32) + ba_ref[...])
    i = jax.nn.sigmoid(jnp.dot(xb, wx_ref[...], preferred_element_type=F32) + bx_ref[...])
    log_a = (-LRU_C * r) * _softplus(-lam_ref[...])
    a = jnp.exp(log_a)
    u = jnp.sqrt(-jnp.tanh(log_a) * (a * a + 1.0)) * (i * xc)
    return a, u


def _rnn_prompt_kernel(xr_ref, yr_ref, cw_ref, cb_ref, wa_ref, ba_ref, wx_ref, bx_ref, lam_ref, g_ref,
                       o_ref, hl_ref, ext_ref, h_ref):
    c = pl.program_id(1)
    tc = xr_ref.shape[0]
    pad = SUBLANES

    @pl.when(c == 0)
    def _():
        ext_ref[0:pad, :] = jnp.zeros((pad, D_RNN), F32)
        h_ref[...] = jnp.zeros((1, D_RNN), F32)

    ext_ref[pad:pad + tc, :] = xr_ref[...]
    cw = cw_ref[...]
    xc = cb_ref[...] + ext_ref[pad:pad + tc, :] * cw[CONV_WIDTH - 1:CONV_WIDTH, :]
    for w in range(CONV_WIDTH - 1):
        sh = CONV_WIDTH - 1 - w
        xc = xc + ext_ref[pad - sh:pad - sh + tc, :] * cw[w:w + 1, :]
    ext_ref[0:pad, :] = ext_ref[tc:tc + pad, :]

    a, u = _lru_gates(xc, wa_ref, ba_ref, wx_ref, bx_ref, lam_ref)

    ng = tc // SUBLANES
    a3 = a.reshape(ng, SUBLANES, D_RNN)
    u3 = u.reshape(ng, SUBLANES, D_RNN)
    t8 = lax.broadcasted_iota(I32, (ng, SUBLANES, D_RNN), 1)
    d = 1
    while d < SUBLANES:
        a_s = jnp.where(t8 >= d, pltpu.roll(a3, d, 1), 1.0)
        u_s = jnp.where(t8 >= d, pltpu.roll(u3, d, 1), 0.0)
        u3 = a3 * u_s + u3
        a3 = a3 * a_s
        d *= 2
    carry = h_ref[...]
    groups = []
    for g in range(ng):
        hg = a3[g] * carry + u3[g]
        groups.append(hg)
        carry = hg[SUBLANES - 1:SUBLANES, :]
    h = jnp.concatenate(groups, axis=0)
    h_ref[...] = carry
    hl_ref[0] = carry
    o_ref[...] = _rmsnorm(jax.nn.gelu(yr_ref[...]) * h, g_ref[...])


def _rnn_prompt(xr, yr, cw, cb, wa, ba, wx, bx, lam, g, batch, seq, tc):
    nc = seq // tc
    cur = lambda b, c: (b * nc + c, 0)
    fix = lambda b, c: (0, 0)
    vec = pl.BlockSpec((1, D_RNN), fix)
    return pl.pallas_call(
        _rnn_prompt_kernel,
        grid=(batch, nc),
        in_specs=[pl.BlockSpec((tc, D_RNN), cur), pl.BlockSpec((tc, D_RNN), cur),
                  pl.BlockSpec((CONV_WIDTH, D_RNN), fix), vec,
                  pl.BlockSpec((D_RNN, D_RNN), fix), vec,
                  pl.BlockSpec((D_RNN, D_RNN), fix), vec, vec, vec],
        out_specs=(pl.BlockSpec((tc, D_RNN), cur), pl.BlockSpec((1, 1, D_RNN), lambda b, c: (b, 0, 0))),
        out_shape=(jax.ShapeDtypeStruct((batch * seq, D_RNN), F32),
                   jax.ShapeDtypeStruct((batch, 1, D_RNN), F32)),
        scratch_shapes=[pltpu.VMEM((tc + SUBLANES, D_RNN), F32), pltpu.VMEM((1, D_RNN), F32)],
        compiler_params=_cparams(2),
        name="rnn_prompt",
    )(xr, yr, cw, cb, wa, ba, wx, bx, lam, g)


def _rnn_sample_kernel(xr_ref, yr_ref, hist_ref, h0_ref, cw_ref, cb_ref, wa_ref, ba_ref, wx_ref, bx_ref,
                       lam_ref, g_ref, o_ref, hl_ref, nh_ref):
    cw = cw_ref[...]
    xr = xr_ref[...]
    xc = cb_ref[...] + xr * cw[CONV_WIDTH - 1:CONV_WIDTH, :]
    for w in range(CONV_WIDTH - 1):
        xc = xc + hist_ref[w] * cw[w:w + 1, :]
    a, u = _lru_gates(xc, wa_ref, ba_ref, wx_ref, bx_ref, lam_ref)
    h = a * h0_ref[...] + u
    hl_ref[...] = h
    o_ref[...] = _rmsnorm(jax.nn.gelu(yr_ref[...]) * h, g_ref[...])
    for w in range(CONV_WIDTH - 2):
        nh_ref[w] = hist_ref[w + 1]
    nh_ref[CONV_WIDTH - 2] = xr


def _rnn_sample(xr, yr, hist, h0, cw, cb, wa, ba, wx, bx, lam, g):
    n = xr.shape[0]
    full = lambda a: pl.BlockSpec(a.shape, lambda: (0,) * a.ndim)
    args = (xr, yr, hist, h0, cw, cb, wa, ba, wx, bx, lam, g)
    return pl.pallas_call(
        _rnn_sample_kernel,
        in_specs=[full(a) for a in args],
        out_specs=(pl.BlockSpec((n, D_RNN), lambda: (0, 0)), pl.BlockSpec((n, D_RNN), lambda: (0, 0)),
                   pl.BlockSpec(hist.shape, lambda: (0, 0, 0))),
        out_shape=(jax.ShapeDtypeStruct((n, D_RNN), F32), jax.ShapeDtypeStruct((n, D_RNN), F32),
                   jax.ShapeDtypeStruct(hist.shape, F32)),
        compiler_params=pltpu.CompilerParams(vmem_limit_bytes=VMEM_LIMIT),
        name="rnn_sample",
    )(*args)


def _mix_route_kernel(x_ref, an_ref, rn_ref, woa_ref, wor_ref, g_ref, wr2_ref, br_ref, tri_ref, low_ref,
                      x2_ref, ts_ref, dest_ref, gate_ref, n8_ref, off_ref):
    tt = x_ref.shape[0]
    tile_rows = ts_ref.shape[0]
    x2 = x_ref[...] + jnp.dot(an_ref[...].astype(BF16), woa_ref[...], preferred_element_type=F32) \
        + jnp.dot(rn_ref[...].astype(BF16), wor_ref[...], preferred_element_type=F32)
    x2_ref[...] = x2
    hn = _rmsnorm(x2, g_ref[...])

    nt = (((1,), (1,)), ((), ()))
    hb = hn.astype(BF16)
    hmid = (hn - hb.astype(F32)).astype(BF16)
    wr2 = wr2_ref[...]
    both = lax.dot_general(wr2, hb, nt, preferred_element_type=F32)
    logits = (lax.dot_general(wr2[:N_EXPERTS], hmid, nt, preferred_element_type=F32)
              + both[N_EXPERTS:]) + both[:N_EXPERTS] + br_ref[...]

    ie = lax.broadcasted_iota(I32, (N_EXPERTS, tt), 0).astype(F32)
    l = logits
    vals, sels = [], []
    for _ in range(TOP_K):
        m = jnp.max(l, axis=0, keepdims=True)
        idx = jnp.min(jnp.where(l == m, ie, float(N_EXPERTS)), axis=0, keepdims=True)
        sel = ie == idx
        vals.append(m)
        sels.append(sel)
        l = jnp.where(sel, NEG_BIG, l)
    es = [jnp.exp(v - vals[0]) for v in vals]
    den = es[0] + es[1] + es[2] + es[3]
    gate_ref[0] = jnp.concatenate([e / den for e in es], axis=0)

    oh = jnp.zeros((N_EXPERTS, tt), F32)
    for sel in sels:
        oh = oh + jnp.where(sel, 1.0, 0.0)
    before = jnp.dot(oh.astype(BF16), tri_ref[...], preferred_element_type=F32)
    cnt = jnp.sum(oh, axis=1, keepdims=True).astype(I32)
    n8 = ((cnt + (SUBLANES - 1)) >> 3) << 3
    n8b = jnp.broadcast_to(n8, (N_EXPERTS, LANES))
    off = jnp.dot(low_ref[...], n8b.astype(F32).astype(BF16), preferred_element_type=F32)
    n8_ref[0] = n8b
    off_ref[0] = off.astype(I32)
    base = off[:, 0:1] + before
    dests = [jnp.sum(jnp.where(sel, base, 0.0), axis=0, keepdims=True).astype(I32) for sel in sels]
    dest_ref[0] = jnp.concatenate(dests, axis=0)

    ri = lax.broadcasted_iota(I32, (DISPATCH_CHUNK, tt), 0)
    for c in range(tile_rows // DISPATCH_CHUNK):
        p = jnp.zeros((DISPATCH_CHUNK, tt), F32)
        for d in dests:
            p = jnp.where(ri == d - c * DISPATCH_CHUNK, 1.0, p)
        ts_ref[c * DISPATCH_CHUNK:(c + 1) * DISPATCH_CHUNK, :] = jnp.dot(
            p.astype(BF16), hb, preferred_element_type=F32)


def _tile_rows(tt):
    return -(-(TOP_K * tt + N_EXPERTS * (SUBLANES - 1)) // DISPATCH_CHUNK) * DISPATCH_CHUNK


def _mix_route(x2d, an, rn, woa, wor, g, wrt, br, low, tt):
    n = x2d.shape[0]
    nt = n // tt
    tile_rows = _tile_rows(tt)
    tri = jnp.triu(jnp.ones((tt, tt), BF16), k=1)
    row = lambda i: (i, 0)
    fix = lambda i: (0, 0)
    t3 = lambda i: (i, 0, 0)
    in_specs = [pl.BlockSpec((tt, D_MODEL), row), pl.BlockSpec((tt, D_ATTN), row),
                pl.BlockSpec((tt, D_RNN), row), pl.BlockSpec((D_ATTN, D_MODEL), fix),
                pl.BlockSpec((D_RNN, D_MODEL), fix), pl.BlockSpec((1, D_MODEL), fix),
                pl.BlockSpec((2 * N_EXPERTS, D_MODEL), fix), pl.BlockSpec((N_EXPERTS, 1), fix),
                pl.BlockSpec((tt, tt), fix), pl.BlockSpec((N_EXPERTS, N_EXPERTS), fix)]
    out_shape = (jax.ShapeDtypeStruct((n, D_MODEL), F32),
                 jax.ShapeDtypeStruct((nt * tile_rows, D_MODEL), F32),
                 jax.ShapeDtypeStruct((nt, TOP_K, tt), I32),
                 jax.ShapeDtypeStruct((nt, TOP_K, tt), F32),
                 jax.ShapeDtypeStruct((nt, N_EXPERTS, LANES), I32),
                 jax.ShapeDtypeStruct((nt, N_EXPERTS, LANES), I32))
    out_specs = (pl.BlockSpec((tt, D_MODEL), row),
                 pl.BlockSpec((tile_rows, D_MODEL), row),
                 pl.BlockSpec((1, TOP_K, tt), t3), pl.BlockSpec((1, TOP_K, tt), t3),
                 pl.BlockSpec((1, N_EXPERTS, LANES), t3), pl.BlockSpec((1, N_EXPERTS, LANES), t3))
    return pl.pallas_call(
        _mix_route_kernel,
        grid=(nt,),
        in_specs=in_specs,
        out_specs=out_specs,
        out_shape=out_shape,
        compiler_params=_cparams(1),
        name="mix_route",
    )(x2d, an, rn, woa, wor, g, wrt, br, tri, low)


def _start_piece(src_hbm, dst_buf, sem, p, psrc_ref, pdst_ref, plen_ref, nbits):
    l8 = plen_ref[p]

    @pl.when(l8 != 0)
    def _():
        s = psrc_ref[p]
        d = pdst_ref[p]
        for c in range(nbits):
            size = SUBLANES << c
            low = (l8 & ((1 << c) - 1)) * SUBLANES

            @pl.when(((l8 >> c) & 1) == 1)
            def _():
                pltpu.make_async_copy(
                    src_hbm.at[pl.ds(pl.multiple_of(s + low, SUBLANES), size)],
                    dst_buf.at[pl.ds(pl.multiple_of(d + low, SUBLANES), size)], sem).start()


def _start_pieces(src_hbm, dst_buf, sem, p_lo, p_hi, psrc_ref, pdst_ref, plen_ref, nbits):
    def body(p, carry):
        _start_piece(src_hbm, dst_buf, sem, p, psrc_ref, pdst_ref, plen_ref, nbits)
        return carry

    lax.fori_loop(p_lo, p_hi, body, 0)


def _start_block_pieces(src_hbm, dst_buf, sem, blk, p0, pblk_ref, psrc_ref, pdst_ref, plen_ref, nbits):
    n = pblk_ref.shape[0]

    def cond(p):
        return jnp.logical_and(p < n, pblk_ref[jnp.minimum(p, n - 1)] == blk)

    def body(p):
        _start_piece(src_hbm, dst_buf, sem, p, psrc_ref, pdst_ref, plen_ref, nbits)
        return p + 1

    return lax.while_loop(cond, body, p0)


def _wait_rows(src_hbm, dst_buf, sem, rows8, nbits):
    for c in range(nbits):
        size = SUBLANES << c

        @pl.when(((rows8 >> c) & 1) == 1)
        def _():
            pltpu.make_async_copy(src_hbm.at[pl.ds(0, size)], dst_buf.at[pl.ds(0, size)], sem).wait()


def _moe_gmm_kernel(be_ref, rows_ref, blka_ref, srca_ref, dsta_ref, lena_ref,
                    blkb_ref, srcb_ref, dstb_ref, lenb_ref,
                    tsa_hbm, tsb_hbm, wgu_ref, wdn_ref, bg_ref, bu_ref, bd_ref, perm_ref,
                    ys_ref, lhs_ref, wg_ref, wu_ref, wd_ref, ptr_ref, sem_ref, *, nbits_a, nbits_b):
    j = pl.program_id(0)
    nb = pl.num_programs(0)
    slot = j % 2
    bm = lhs_ref.shape[1]

    def gather(blk, sl):
        ptr_ref[0] = _start_block_pieces(tsa_hbm, lhs_ref.at[sl], sem_ref.at[sl], blk, ptr_ref[0],
                                         blka_ref, srca_ref, dsta_ref, lena_ref, nbits_a)
        ptr_ref[1] = _start_block_pieces(tsb_hbm, lhs_ref.at[sl], sem_ref.at[sl], blk, ptr_ref[1],
                                         blkb_ref, srcb_ref, dstb_ref, lenb_ref, nbits_b)

    @pl.when(j == 0)
    def _():
        lhs_ref[...] = jnp.zeros(lhs_ref.shape, F32)
        ptr_ref[0] = 0
        ptr_ref[1] = 0
        gather(0, 0)

    @pl.when(j + 1 < nb)
    def _():
        gather(j + 1, 1 - slot)

    changed = jnp.logical_or(j == 0, be_ref[j] != be_ref[jnp.maximum(j - 1, 0)])

    @pl.when(changed)
    def _():
        perm = perm_ref[...]
        half = PERM_COLS // 2
        for c in range(2 * D_FF // PERM_COLS):
            wb = wgu_ref[0, :, c * PERM_COLS:(c + 1) * PERM_COLS].astype(BF16)
            wp = jnp.dot(wb, perm, preferred_element_type=F32).astype(BF16)
            wg_ref[:, c * half:(c + 1) * half] = wp[:, :half]
            wu_ref[:, c * half:(c + 1) * half] = wp[:, half:]
        wd_ref[...] = wdn_ref[0].astype(BF16)

    rows8 = rows_ref[j]
    _wait_rows(tsa_hbm, lhs_ref.at[slot], sem_ref.at[slot], rows8, (bm // SUBLANES).bit_length())

    hm = bm // 2
    for hf in range(2):
        rs = slice(hf * hm, (hf + 1) * hm)
        used = rows8 * SUBLANES > hf * hm

        @pl.when(used)
        def _():
            x = lhs_ref[slot, rs, :].astype(BF16)
            gate = jnp.dot(x, wg_ref[...], preferred_element_type=F32) + bg_ref[0]
            up = jnp.dot(x, wu_ref[...], preferred_element_type=F32) + bu_ref[0]
            gate = jnp.minimum(gate, SWIGLU_LIMIT)
            up = jnp.clip(up, -SWIGLU_LIMIT, SWIGLU_LIMIT)
            act = (up + 1.0) * (gate * jax.nn.sigmoid(SWIGLU_ALPHA * gate))
            ys_ref[rs, :] = jnp.dot(act.astype(BF16), wd_ref[...], preferred_element_type=F32) + bd_ref[0]

        @pl.when(jnp.logical_not(used))
        def _():
            ys_ref[rs, :] = jnp.zeros((hm, D_MODEL), F32)


def _moe_gmm(blocks, tabs_a, tabs_b, ts_a, ts_b, w_gu, w_dn, bg, bu, bd, perm, nblocks, bm, nbits_a, nbits_b):
    we = lambda j, be, *_: (be[j], 0, 0)
    grid_spec = pltpu.PrefetchScalarGridSpec(
        num_scalar_prefetch=10,
        grid=(nblocks,),
        in_specs=[pl.BlockSpec(memory_space=pl.ANY), pl.BlockSpec(memory_space=pl.ANY),
                  pl.BlockSpec((1, D_MODEL, 2 * D_FF), we),
                  pl.BlockSpec((1, D_FF, D_MODEL), we),
                  pl.BlockSpec((1, 1, D_FF), we), pl.BlockSpec((1, 1, D_FF), we),
                  pl.BlockSpec((1, 1, D_MODEL), we),
                  pl.BlockSpec((PERM_COLS, PERM_COLS), lambda j, *_: (0, 0))],
        out_specs=pl.BlockSpec((bm, D_MODEL), lambda j, *_: (j, 0)),
        scratch_shapes=[pltpu.VMEM((2, bm, D_MODEL), F32),
                        pltpu.VMEM((D_MODEL, D_FF), BF16), pltpu.VMEM((D_MODEL, D_FF), BF16),
                        pltpu.VMEM((D_FF, D_MODEL), BF16),
                        pltpu.SMEM((2,), I32),
                        pltpu.SemaphoreType.DMA((2,))],
    )
    return pl.pallas_call(
        functools.partial(_moe_gmm_kernel, nbits_a=nbits_a, nbits_b=nbits_b),
        grid_spec=grid_spec,
        out_shape=jax.ShapeDtypeStruct((nblocks * bm, D_MODEL), F32),
        compiler_params=_cparams(1),
        name="moe_gmm",
    )(*blocks, *tabs_a, *tabs_b, ts_a, ts_b, w_gu, w_dn, bg, bu, bd, perm)


def _combine_kernel(psrc_ref, pdst_ref, plen_ref, rows_ref, ys_hbm, x2_ref, dest_ref, gate_ref, o_ref,
                    buf_ref, db_ref, gb_ref, sem_ref, *, pieces_per_tile, nbits):
    i = pl.program_id(0)
    n = pl.num_programs(0)
    slot = i % 2
    tt = x2_ref.shape[0]
    tile_rows = buf_ref.shape[1]

    def gather(tile, sl):
        p0 = tile * pieces_per_tile
        _start_pieces(ys_hbm, buf_ref.at[sl], sem_ref.at[sl], p0, p0 + pieces_per_tile,
                      psrc_ref, pdst_ref, plen_ref, nbits)

    @pl.when(i == 0)
    def _():
        buf_ref[...] = jnp.zeros(buf_ref.shape, F32)
        gather(0, 0)

    @pl.when(i + 1 < n)
    def _():
        gather(i + 1, 1 - slot)

    _wait_rows(ys_hbm, buf_ref.at[slot], sem_ref.at[slot], rows_ref[i], (tile_rows // SUBLANES).bit_length())

    dest = dest_ref[0]
    gate = gate_ref[0]
    for k in range(TOP_K):
        db_ref[k] = jnp.broadcast_to(dest[:, k:k + 1], (tt, DISPATCH_CHUNK))
        gb_ref[k] = jnp.broadcast_to(gate[:, k:k + 1], (tt, DISPATCH_CHUNK))
    li = lax.broadcasted_iota(I32, (tt, DISPATCH_CHUNK), 1)
    gms = []
    for c in range(tile_rows // DISPATCH_CHUNK):
        lic = li + c * DISPATCH_CHUNK
        gm = jnp.zeros((tt, DISPATCH_CHUNK), F32)
        for k in range(TOP_K):
            gm = jnp.where(lic == db_ref[k], gb_ref[k], gm)
        gms.append(gm.astype(BF16))
    o_ref[...] = x2_ref[...] + jnp.dot(jnp.concatenate(gms, axis=1), buf_ref[slot].astype(BF16),
                                       preferred_element_type=F32)


def _combine(psrc, pdst, plen8, rows8, ys, x2, dest, gate, tt, nbits):
    n = x2.shape[0]
    nt = n // tt
    tile_rows = _tile_rows(tt)
    pieces_per_tile = 2 * N_EXPERTS
    grid_spec = pltpu.PrefetchScalarGridSpec(
        num_scalar_prefetch=4,
        grid=(nt,),
        in_specs=[pl.BlockSpec(memory_space=pl.ANY),
                  pl.BlockSpec((tt, D_MODEL), lambda i, *_: (i, 0)),
                  pl.BlockSpec((1, tt, TOP_K), lambda i, *_: (i, 0, 0)),
                  pl.BlockSpec((1, tt, TOP_K), lambda i, *_: (i, 0, 0))],
        out_specs=pl.BlockSpec((tt, D_MODEL), lambda i, *_: (i, 0)),
        scratch_shapes=[pltpu.VMEM((2, tile_rows, D_MODEL), F32),
                        pltpu.VMEM((TOP_K, tt, DISPATCH_CHUNK), I32), pltpu.VMEM((TOP_K, tt, DISPATCH_CHUNK), F32),
                        pltpu.SemaphoreType.DMA((2,))],
    )
    return pl.pallas_call(
        functools.partial(_combine_kernel, pieces_per_tile=pieces_per_tile, nbits=nbits),
        grid_spec=grid_spec,
        out_shape=jax.ShapeDtypeStruct((n, D_MODEL), F32),
        compiler_params=_cparams(1),
        name="combine",
    )(psrc, pdst, plen8, rows8, ys, x2, dest, gate)


def _piece_tables(n8_a, off_a, rows_a, n8_b, off_b, bm, nblocks):
    nta = n8_a.shape[0]
    n8 = jnp.concatenate([n8_a, n8_b], axis=0)
    seg_off = jnp.concatenate([off_a, off_b], axis=0)
    n_tiles = n8.shape[0]
    tile_base = jnp.concatenate([jnp.arange(nta, dtype=I32) * rows_a, jnp.zeros((n_tiles - nta,), I32)])[:, None]
    tot = jnp.sum(n8, axis=0)
    pos0 = jnp.cumsum(n8, axis=0) - n8
    nblk = (tot + bm - 1) // bm
    cs = jnp.cumsum(nblk)
    bs = cs - nblk
    kblk = pos0 // bm
    len0 = jnp.minimum(n8, (kblk + 1) * bm - pos0)
    len1 = n8 - len0
    b0 = bs[None, :] + kblk
    blk = jnp.stack([b0, jnp.where(len1 > 0, b0 + 1, b0)], axis=-1)
    plen8 = jnp.stack([len0, len1], axis=-1) // SUBLANES
    ts_row = jnp.stack([tile_base + seg_off, tile_base + seg_off + len0], axis=-1)
    in_blk = jnp.stack([pos0 - kblk * bm, jnp.zeros_like(pos0)], axis=-1)
    buf_row = jnp.stack([seg_off, seg_off + len0], axis=-1)
    ys_row = blk * bm + in_blk

    em = lambda a: jnp.transpose(a, (1, 0, 2)).reshape(-1).astype(I32)
    tm = lambda a: a.reshape(-1).astype(I32)
    jj = jnp.arange(nblocks, dtype=I32)

    def gmm_tabs(sl):
        return em(blk[sl]), em(ts_row[sl]), em(in_blk[sl]), em(plen8[sl])

    def comb_tabs(sl):
        return tm(ys_row[sl]), tm(buf_row[sl]), tm(plen8[sl]), (jnp.sum(n8[sl], axis=1) // SUBLANES).astype(I32)

    count_le = lambda v: jnp.sum(cs[None, :] <= v[:, None], axis=1)
    n_active = cs[-1]
    e_last = count_le(jnp.maximum(n_active - 1, 0).reshape(1))[0]
    block_e = jnp.minimum(count_le(jj), e_last).astype(I32)
    rows8 = jnp.clip(tot[block_e] - (jj - bs[block_e]) * bm, 0, bm) // SUBLANES
    rows8 = jnp.where(jj < n_active, rows8, 0).astype(I32)
    a, b = slice(0, nta), slice(nta, n_tiles)
    return (block_e, rows8), gmm_tabs(a), gmm_tabs(b), comb_tabs(a), comb_tabs(b)


def _block_diag(w):
    nb, bi, bo = w.shape
    eye = jnp.eye(nb, dtype=w.dtype)
    return (eye[:, None, :, None] * w[:, :, None, :]).reshape(nb * bi, nb * bo)


def _step(x_prompt, x_sample, cache_k, cache_v, state_conv, state_h, g_mix_norm, w_in, g_q_norm, g_k_norm,
          attn_sinks, conv_w, conv_b, w_lru_a, b_lru_a, w_lru_x, b_lru_x, lru_lambda, g_attn_out, g_rnn_out,
          w_out, g_ffn_norm, w_router, b_router, w_gate_up, b_gate_up, w_down, b_down,
          *, tm, tt, tc, bm, past_len):
    B, S, D = x_prompt.shape
    NS = x_sample.shape[0]
    assert x_sample.shape[1] == 1 and D == D_MODEL
    assert (B * S) % tt == 0 and (B * S) % tm == 0 and S % tc == 0 and S % ATTN_BLOCK == 0 and S % tm == 0
    assert NS % SUBLANES == 0 and tt <= bm
    assert tt % SUBLANES == 0 and NS <= bm
    n_pt = (B * S) // tt
    total_rows = TOP_K * (B * S + NS) + (n_pt + 1) * N_EXPERTS * (SUBLANES - 1)
    nblocks = -(-total_rows // bm) + N_EXPERTS
    nbits_p = (tt // SUBLANES).bit_length()
    nbits_s = (NS // SUBLANES).bit_length()

    l = 0
    row = lambda v: v[l].reshape(1, -1)
    w_in_bf = w_in[l].astype(BF16)
    gq2 = jnp.tile(g_q_norm[l], 2).reshape(1, LANES)
    gk2 = jnp.tile(g_k_norm[l], 2).reshape(1, LANES)
    wa = _block_diag(w_lru_a[l]).astype(BF16)
    wx = _block_diag(w_lru_x[l]).astype(BF16)
    ba = b_lru_a[l].reshape(1, D_RNN)
    bx = b_lru_x[l].reshape(1, D_RNN)
    wo = w_out[l].astype(BF16)
    woa, wor = wo[:D_ATTN], wo[D_ATTN:]
    wr = w_router[l].T
    wr_hi = wr.astype(BF16)
    wrt = jnp.concatenate([wr_hi, (wr - wr_hi.astype(F32)).astype(BF16)], axis=0)
    br = b_router[l].reshape(N_EXPERTS, 1)
    low = jnp.tril(jnp.ones((N_EXPERTS, N_EXPERTS), BF16), k=-1)
    bgu = b_gate_up[l].reshape(N_EXPERTS, D_FF, 2)
    bg = bgu[:, :, 0].reshape(N_EXPERTS, 1, D_FF)
    bu = bgu[:, :, 1].reshape(N_EXPERTS, 1, D_FF)
    bd = b_down[l].reshape(N_EXPERTS, 1, D_MODEL)
    half = PERM_COLS // 2
    pr = jnp.arange(PERM_COLS)
    perm = (pr[None, :] == jnp.where(pr % 2 == 0, pr // 2, half + pr // 2)[:, None]).astype(BF16)
    sinks = attn_sinks[l]

    ctab, s1tab, s2tab = _rope_tables(jnp.arange(S))
    q, k, v, xr, yr = _in_proj(x_prompt.reshape(B * S, D), row(g_mix_norm), w_in_bf, gq2, gk2,
                               ctab, s1tab, s2tab, tm)
    an, kt_p, vt_p = _attn_prompt(q, k, v, sinks, row(g_attn_out), B, S)
    rn, h_last_p = _rnn_prompt(xr, yr, conv_w[l], row(conv_b), wa, ba, wx, bx, row(lru_lambda),
                               row(g_rnn_out), B, S, tc)
    x2_p, ts_p, dest_p, gate_p, n8_p, off_p = _mix_route(
        x_prompt.reshape(B * S, D), an, rn, woa, wor, row(g_ffn_norm), wrt, br, low, tt)

    cs_tab = _rope_tables(jnp.full((NS,), past_len, I32))
    q_s, k_s, v_s, xr_s, yr_s = _in_proj(x_sample.reshape(NS, D), row(g_mix_norm), w_in_bf, gq2, gk2,
                                         *cs_tab, NS)
    to_rows = lambda c: jnp.transpose(c, (0, 2, 3, 1)).reshape(NS * KV_W, WINDOW)
    from_rows = lambda c, n: jnp.transpose(c.reshape(n, N_KV_HEADS, HEAD_DIM, WINDOW), (0, 3, 1, 2))[None]
    an_s, kt_s, vt_s = _attn_sample(q_s, k_s, v_s, to_rows(cache_k[l]), to_rows(cache_v[l]), sinks,
                                    row(g_attn_out), SUBLANES)
    rn_s, h_last_s, hist_s = _rnn_sample(xr_s, yr_s, jnp.transpose(state_conv[l], (1, 0, 2)), state_h[l],
                                         conv_w[l], row(conv_b), wa, ba, wx, bx, row(lru_lambda),
                                         row(g_rnn_out))
    x2_s, ts_s, dest_s, gate_s, n8_s, off_s = _mix_route(
        x_sample.reshape(NS, D), an_s, rn_s, woa, wor, row(g_ffn_norm), wrt, br, low, NS)

    blocks, gmm_p, gmm_s, comb_p, comb_s = _piece_tables(
        n8_p[:, :, 0], off_p[:, :, 0], _tile_rows(tt), n8_s[:, :, 0], off_s[:, :, 0], bm, nblocks)
    ys = _moe_gmm(blocks, gmm_p, gmm_s, ts_p, ts_s, w_gate_up[l], w_down[l], bg, bu, bd, perm,
                  nblocks, bm, nbits_p, nbits_s)
    tr = lambda a: jnp.transpose(a, (0, 2, 1))
    y_p = _combine(*comb_p, ys, x2_p, tr(dest_p), tr(gate_p), tt, nbits_p)
    y_s = _combine(*comb_s, ys, x2_s, tr(dest_s), tr(gate_s), NS, nbits_s)

    cp = xr.reshape(B, S, D_RNN)[:, S - (CONV_WIDTH - 1):]
    return (y_p.reshape(B, S, D), y_s.reshape(NS, 1, D),
            from_rows(kt_p, B), from_rows(vt_p, B), cp[None], h_last_p.reshape(1, B, D_RNN),
            from_rows(kt_s, NS), from_rows(vt_s, NS), jnp.transpose(hist_s, (1, 0, 2))[None], h_last_s[None])


def kernel(x_prompt, x_sample, cache_k, cache_v, state_conv, state_h, g_mix_norm, w_in, g_q_norm, g_k_norm, attn_sinks, conv_w, conv_b, w_lru_a, b_lru_a, w_lru_x, b_lru_x, lru_lambda, g_attn_out, g_rnn_out, w_out, g_ffn_norm, w_router, b_router, w_gate_up, b_gate_up, w_down, b_down):
    return _step(x_prompt, x_sample, cache_k, cache_v, state_conv, state_h, g_mix_norm, w_in, g_q_norm,
                 g_k_norm, attn_sinks, conv_w, conv_b, w_lru_a, b_lru_a, w_lru_x, b_lru_x, lru_lambda,
                 g_attn_out, g_rnn_out, w_out, g_ffn_norm, w_router, b_router, w_gate_up, b_gate_up,
                 w_down, b_down, tm=512, tt=512, tc=256, bm=MOE_BLOCK_ROWS, past_len=PAST_LEN)
```

```python
import functools

import jax
import jax.numpy as jnp
from jax import lax
from jax.experimental import pallas as pl
from jax.experimental.pallas import tpu as pltpu

F32 = jnp.float32
BF16 = jnp.bfloat16
I32 = jnp.int32

D_MODEL = 1024
HEAD_DIM = 64
N_HEADS = 8
N_KV_HEADS = 2
GROUP = 4
WINDOW = 128
ATTN_BLOCK = 128
ROT_DIM = 16
ROPE_THETA = 500000.0
D_ATTN = 512
D_RNN = 512
KV_W = 128
D_IN = 1792
CONV_WIDTH = 4
LRU_C = 8.0
N_EXPERTS = 32
TOP_K = 4
D_FF = 1024
SWIGLU_LIMIT = 7.0
SWIGLU_ALPHA = 1.702
EPS = 1e-6
PAST_LEN = 8192

LANES = 128
SUBLANES = 8
NEG_BIG = -1e30
VMEM_LIMIT = 56 * 1024 * 1024

MOE_BLOCK_ROWS = 512
PERM_COLS = 256
DISPATCH_CHUNK = 256


def _cparams(n_axes):
    return pltpu.CompilerParams(dimension_semantics=("arbitrary",) * n_axes,
                                vmem_limit_bytes=VMEM_LIMIT)


def _rmsnorm(x, g):
    ms = jnp.mean(x * x, axis=-1, keepdims=True)
    return (x * lax.rsqrt(ms + EPS)) * g


def _in_proj_kernel(x_ref, g_ref, w_ref, gq_ref, gk_ref, c_ref, s1_ref, s2_ref,
                    q_ref, k_ref, v_ref, xr_ref, yr_ref):
    tm = x_ref.shape[0]
    h = _rmsnorm(x_ref[...], g_ref[...])
    proj = jnp.dot(h.astype(BF16), w_ref[...], preferred_element_type=F32)
    lo = lax.broadcasted_iota(I32, (tm, LANES), 1) < HEAD_DIM
    c = c_ref[...]
    s1 = s1_ref[...]
    s2 = s2_ref[...]

    def head_norm_rope(t, g):
        sq = t * t
        s_lo = jnp.sum(jnp.where(lo, sq, 0.0), axis=-1, keepdims=True)
        s_hi = jnp.sum(jnp.where(lo, 0.0, sq), axis=-1, keepdims=True)
        ms = jnp.where(lo, s_lo, s_hi) * (1.0 / HEAD_DIM)
        n = (t * lax.rsqrt(ms + EPS)) * g
        up = pltpu.roll(n, LANES - ROT_DIM // 2, 1)
        dn = pltpu.roll(n, ROT_DIM // 2, 1)
        return n * c + up * s1 + dn * s2

    gq = gq_ref[...]
    for j in range(D_ATTN // LANES):
        q_ref[:, j * LANES:(j + 1) * LANES] = head_norm_rope(proj[:, j * LANES:(j + 1) * LANES], gq)
    k_ref[...] = head_norm_rope(proj[:, D_ATTN:D_ATTN + KV_W], gk_ref[...])
    v_ref[...] = proj[:, D_ATTN + KV_W:D_ATTN + 2 * KV_W]
    o = D_ATTN + 2 * KV_W
    xr_ref[...] = proj[:, o:o + D_RNN]
    yr_ref[...] = proj[:, o + D_RNN:o + 2 * D_RNN]


def _in_proj(x2d, g, w_bf, gq2, gk2, ctab, s1tab, s2tab, tm):
    n = x2d.shape[0]
    ntab = ctab.shape[0] // tm
    row = lambda i: (i, 0)
    fix = lambda i: (0, 0)
    tab = lambda i: (i % ntab, 0)
    out_shapes = (jax.ShapeDtypeStruct((n, D_ATTN), F32), jax.ShapeDtypeStruct((n, KV_W), F32),
                  jax.ShapeDtypeStruct((n, KV_W), F32), jax.ShapeDtypeStruct((n, D_RNN), F32),
                  jax.ShapeDtypeStruct((n, D_RNN), F32))
    return pl.pallas_call(
        _in_proj_kernel,
        grid=(n // tm,),
        in_specs=[pl.BlockSpec((tm, D_MODEL), row), pl.BlockSpec((1, D_MODEL), fix),
                  pl.BlockSpec((D_MODEL, D_IN), fix), pl.BlockSpec((1, LANES), fix),
                  pl.BlockSpec((1, LANES), fix), pl.BlockSpec((tm, LANES), tab),
                  pl.BlockSpec((tm, LANES), tab), pl.BlockSpec((tm, LANES), tab)],
        out_specs=(pl.BlockSpec((tm, D_ATTN), row), pl.BlockSpec((tm, KV_W), row),
                   pl.BlockSpec((tm, KV_W), row), pl.BlockSpec((tm, D_RNN), row),
                   pl.BlockSpec((tm, D_RNN), row)),
        out_shape=out_shapes,
        compiler_params=_cparams(1),
        name="in_proj",
    )(x2d, g, w_bf, gq2, gk2, ctab, s1tab, s2tab)


def _rope_tables(pos):
    half = ROT_DIM // 2
    inv = ROPE_THETA ** (-jnp.arange(0, ROT_DIM, 2, dtype=F32) / ROT_DIM)
    ang = pos.astype(F32)[:, None] * inv[None, :]
    cos = jnp.cos(ang)
    sin = jnp.sin(ang)
    n = pos.shape[0]
    ones = jnp.ones((n, HEAD_DIM - ROT_DIM), F32)
    zeros = jnp.zeros((n, HEAD_DIM - ROT_DIM), F32)
    zh = jnp.zeros((n, half), F32)
    c = jnp.concatenate([cos, cos, ones], axis=1)
    s1 = jnp.concatenate([-sin, zh, zeros], axis=1)
    s2 = jnp.concatenate([zh, sin, zeros], axis=1)
    two = lambda t: jnp.concatenate([t, t], axis=1)
    return two(c), two(s1), two(s2)


def _band_bias(qb):
    qi = jnp.arange(qb, dtype=I32)[:, None]
    c = jnp.arange(2 * qb, dtype=I32)[None, :]
    band = (c >= qi) & (c <= qi + qb)
    first = band & (c >= qb)
    one = jnp.where(jnp.stack([first, band]), 0.0, NEG_BIG).astype(F32)
    return jnp.concatenate([one, one], axis=2)


def _attn_prompt_kernel(sink_ref, q_ref, kc_ref, kp_ref, vc_ref, vp_ref, bias_ref, g_ref,
                        o_ref, kt_ref, vt_ref, s_ref, e_ref):
    j = pl.program_id(1)
    qb = ATTN_BLOCK
    kc = kc_ref[...]
    vc = vc_ref[...]
    k2 = jnp.concatenate([kp_ref[...], kc], axis=0)
    v2 = jnp.concatenate([vp_ref[...], vc], axis=0)
    k2r = pltpu.roll(k2, HEAD_DIM, 1)
    v2r = pltpu.roll(v2, HEAD_DIM, 1)
    lo_k = lax.broadcasted_iota(I32, (2 * qb, LANES), 1) < HEAD_DIM
    lo_q = lax.broadcasted_iota(I32, (qb, LANES), 1) < HEAD_DIM
    bias = bias_ref[0]
    nt = (((1,), (1,)), ((), ()))
    n_pairs = N_HEADS // 2
    kbd, vbd = [], []
    for kv in range(N_KV_HEADS):
        ka, kb = (k2, k2r) if kv == 0 else (k2r, k2)
        va, vb = (v2, v2r) if kv == 0 else (v2r, v2)
        kbd.append(jnp.concatenate([jnp.where(lo_k, ka, 0.0), jnp.where(lo_k, 0.0, kb)], axis=0).astype(BF16))
        vbd.append(jnp.concatenate([jnp.where(lo_k, va, 0.0), jnp.where(lo_k, 0.0, vb)], axis=0).astype(BF16))
    for pp in range(n_pairs):
        qp = (q_ref[:, pp * LANES:(pp + 1) * LANES] * (HEAD_DIM ** -0.5)).astype(BF16)
        s_ref[pp] = lax.dot_general(qp, kbd[pp // (GROUP // 2)], nt, preferred_element_type=F32) + bias
    invs = []
    for pp in range(n_pairs):
        inv = []
        for t in range(2):
            cols = slice(t * 2 * qb, (t + 1) * 2 * qb)
            st = s_ref[pp, :, cols]
            sink = sink_ref[2 * pp + t]
            m = jnp.maximum(jnp.max(st, axis=-1, keepdims=True), sink)
            e = jnp.exp(st - m)
            e_ref[pp, :, cols] = e.astype(BF16)
            inv.append(1.0 / (jnp.sum(e, axis=-1, keepdims=True) + jnp.exp(sink - m)))
        invs.append(jnp.where(lo_q, inv[0], inv[1]))
    outs = [jnp.dot(e_ref[pp], vbd[pp // (GROUP // 2)], preferred_element_type=F32) * invs[pp]
            for pp in range(n_pairs)]
    o_ref[...] = _rmsnorm(jnp.concatenate(outs, axis=1), g_ref[...])

    @pl.when(j == pl.num_programs(1) - 1)
    def _():
        kt_ref[0] = kc.T
        vt_ref[0] = vc.T


def _attn_prompt(q, k, v, sinks, g_attn, batch, seq):
    qb = ATTN_BLOCK
    nb = seq // qb
    cur = lambda b, j: (b * nb + j, 0)
    prev = lambda b, j: (b * nb + jnp.maximum(j - 1, 0), 0)
    fix = lambda b, j: (0, 0)
    per_b = lambda b, j: (b, 0, 0)
    return pl.pallas_call(
        _attn_prompt_kernel,
        grid=(batch, nb),
        in_specs=[pl.BlockSpec(memory_space=pltpu.SMEM),
                  pl.BlockSpec((qb, D_ATTN), cur),
                  pl.BlockSpec((qb, KV_W), cur), pl.BlockSpec((qb, KV_W), prev),
                  pl.BlockSpec((qb, KV_W), cur), pl.BlockSpec((qb, KV_W), prev),
                  pl.BlockSpec((1, qb, 4 * qb), lambda b, j: (jnp.minimum(j, 1), 0, 0)),
                  pl.BlockSpec((1, D_ATTN), fix)],
        out_specs=(pl.BlockSpec((qb, D_ATTN), cur), pl.BlockSpec((1, KV_W, qb), per_b),
                   pl.BlockSpec((1, KV_W, qb), per_b)),
        out_shape=(jax.ShapeDtypeStruct((batch * seq, D_ATTN), F32),
                   jax.ShapeDtypeStruct((batch, KV_W, qb), F32),
                   jax.ShapeDtypeStruct((batch, KV_W, qb), F32)),
        scratch_shapes=[pltpu.VMEM((N_HEADS // 2, qb, 4 * qb), F32),
                        pltpu.VMEM((N_HEADS // 2, qb, 4 * qb), BF16)],
        compiler_params=_cparams(2),
        name="attn_prompt",
    )(sinks, q, k, k, v, v, _band_bias(qb), g_attn)


def _attn_sample_kernel(sink_ref, q_ref, kn_ref, vn_ref, kt_ref, vt_ref, g_ref,
                        o_ref, nkt_ref, nvt_ref, acc_ref):
    bb = q_ref.shape[0]
    q = q_ref[...] * (HEAD_DIM ** -0.5)
    kn = kn_ref[...]
    vn = vn_ref[...]
    kt = kt_ref[...]
    vt = vt_ref[...]
    col = lax.broadcasted_iota(I32, (bb, bb * KV_W), 1)
    rowb = lax.broadcasted_iota(I32, (bb, bb * KV_W), 0)
    own_seq = (col >> (KV_W.bit_length() - 1)) == rowb
    half_hi = ((col >> (HEAD_DIM.bit_length() - 1)) & 1) == 1
    qbig = []
    for h in range(N_HEADS):
        kv = h // GROUP
        pair = q[:, (h // 2) * LANES:(h // 2 + 1) * LANES]
        if (h % 2) != kv:
            pair = pltpu.roll(pair, HEAD_DIM, 1)
        tiled = jnp.concatenate([pair] * bb, axis=1)
        keep = own_seq & (half_hi if kv == 1 else jnp.logical_not(half_hi))
        qbig.append(jnp.where(keep, tiled, 0.0))
    qbig = jnp.concatenate(qbig, axis=0)
    s = jnp.dot(qbig.astype(BF16), kt.astype(BF16), preferred_element_type=F32)
    qb16 = q.astype(BF16).astype(F32)
    kb16 = kn.astype(BF16).astype(F32)
    s_new, sink = [], []
    for h in range(N_HEADS):
        kv = h // GROUP
        s_new.append(jnp.sum(qb16[:, h * HEAD_DIM:(h + 1) * HEAD_DIM] * kb16[:, kv * HEAD_DIM:(kv + 1) * HEAD_DIM],
                             axis=-1, keepdims=True))
        sink.append(jnp.full((bb, 1), sink_ref[h], F32))
    s_new = jnp.concatenate(s_new, axis=0)
    sink = jnp.concatenate(sink, axis=0)
    m = jnp.maximum(jnp.maximum(jnp.max(s, axis=-1, keepdims=True), s_new), sink)
    e = jnp.exp(s - m)
    e_new = jnp.exp(s_new - m)
    inv = 1.0 / (jnp.sum(e, axis=-1, keepdims=True) + e_new + jnp.exp(sink - m))
    obig = lax.dot_general(e.astype(BF16), vt.astype(BF16), (((1,), (1,)), ((), ())),
                           preferred_element_type=F32)
    for h in range(N_HEADS):
        kv = h // GROUP
        blk = jnp.where(own_seq, obig[h * bb:(h + 1) * bb, :], 0.0)
        fold = blk[:, 0:KV_W]
        for t in range(1, bb):
            fold = fold + blk[:, t * KV_W:(t + 1) * KV_W]
        hs = slice(h * bb, (h + 1) * bb)
        ks = slice(kv * HEAD_DIM, (kv + 1) * HEAD_DIM)
        acc_ref[:, h * HEAD_DIM:(h + 1) * HEAD_DIM] = (fold[:, ks] + e_new[hs] * vn[:, ks]) * inv[hs]
    o_ref[...] = _rmsnorm(acc_ref[...], g_ref[...])

    last = lax.broadcasted_iota(I32, (KV_W, WINDOW), 1) == WINDOW - 1
    for b in range(bb):
        rs = slice(b * KV_W, (b + 1) * KV_W)
        kcol = jnp.broadcast_to(kn[b:b + 1, :], (KV_W, KV_W)).T
        vcol = jnp.broadcast_to(vn[b:b + 1, :], (KV_W, KV_W)).T
        nkt_ref[rs, :] = jnp.where(last, kcol, pltpu.roll(kt[rs, :], WINDOW - 1, 1))
        nvt_ref[rs, :] = jnp.where(last, vcol, pltpu.roll(vt[rs, :], WINDOW - 1, 1))


def _attn_sample(q, kn, vn, kt2d, vt2d, sinks, g_attn, bb):
    n = q.shape[0]
    row = lambda i: (i, 0)
    fix = lambda i: (0, 0)
    cache = pl.BlockSpec((bb * KV_W, WINDOW), row)
    return pl.pallas_call(
        _attn_sample_kernel,
        grid=(n // bb,),
        in_specs=[pl.BlockSpec(memory_space=pltpu.SMEM),
                  pl.BlockSpec((bb, D_ATTN), row), pl.BlockSpec((bb, KV_W), row),
                  pl.BlockSpec((bb, KV_W), row), cache, cache,
                  pl.BlockSpec((1, D_ATTN), fix)],
        out_specs=(pl.BlockSpec((bb, D_ATTN), row), cache, cache),
        out_shape=(jax.ShapeDtypeStruct((n, D_ATTN), F32),
                   jax.ShapeDtypeStruct(kt2d.shape, F32), jax.ShapeDtypeStruct(vt2d.shape, F32)),
        scratch_shapes=[pltpu.VMEM((bb, D_ATTN), F32)],
        compiler_params=_cparams(1),
        name="attn_sample",
    )(sinks, q, kn, vn, kt2d, vt2d, g_attn)


def _softplus(z):
    return jnp.maximum(z, 0.0) + jnp.log1p(jnp.exp(-jnp.abs(z)))


def _lru_gates(xc, wa_ref, ba_ref, wx_ref, bx_ref, lam_ref):
    xb = xc.astype(BF16)
    r = jax.nn.sigmoid(jnp.dot(xb, wa_ref[...], preferred_element_type=F32) + ba_ref[...])
    i = jax.nn.sigmoid(jnp.dot(xb, wx_ref[...], preferred_element_type=F32) + bx_ref[...])
    log_a = (-LRU_C * r) * _softplus(-lam_ref[...])
    a = jnp.exp(log_a)
    u = jnp.sqrt(-jnp.tanh(log_a) * (a * a + 1.0)) * (i * xc)
    return a, u


def _rnn_prompt_kernel(xr_ref, yr_ref, cw_ref, cb_ref, wa_ref, ba_ref, wx_ref, bx_ref, lam_ref, g_ref,
                       o_ref, hl_ref, ext_ref, h_ref):
    c = pl.program_id(1)
    tc = xr_ref.shape[0]
    pad = SUBLANES

    @pl.when(c == 0)
    def _():
        ext_ref[0:pad, :] = jnp.zeros((pad, D_RNN), F32)
        h_ref[...] = jnp.zeros((1, D_RNN), F32)

    ext_ref[pad:pad + tc, :] = xr_ref[...]
    cw = cw_ref[...]
    xc = cb_ref[...] + ext_ref[pad:pad + tc, :] * cw[CONV_WIDTH - 1:CONV_WIDTH, :]
    for w in range(CONV_WIDTH - 1):
        sh = CONV_WIDTH - 1 - w
        xc = xc + ext_ref[pad - sh:pad - sh + tc, :] * cw[w:w + 1, :]
    ext_ref[0:pad, :] = ext_ref[tc:tc + pad, :]

    a, u = _lru_gates(xc, wa_ref, ba_ref, wx_ref, bx_ref, lam_ref)

    ng = tc // SUBLANES
    a3 = a.reshape(ng, SUBLANES, D_RNN)
    u3 = u.reshape(ng, SUBLANES, D_RNN)
    t8 = lax.broadcasted_iota(I32, (ng, SUBLANES, D_RNN), 1)
    d = 1
    while d < SUBLANES:
        a_s = jnp.where(t8 >= d, pltpu.roll(a3, d, 1), 1.0)
        u_s = jnp.where(t8 >= d, pltpu.roll(u3, d, 1), 0.0)
        u3 = a3 * u_s + u3
        a3 = a3 * a_s
        d *= 2
    carry = h_ref[...]
    groups = []
    for g in range(ng):
        hg = a3[g] * carry + u3[g]
        groups.append(hg)
        carry = hg[SUBLANES - 1:SUBLANES, :]
    h = jnp.concatenate(groups, axis=0)
    h_ref[...] = carry
    hl_ref[0] = carry
    o_ref[...] = _rmsnorm(jax.nn.gelu(yr_ref[...]) * h, g_ref[...])


def _rnn_prompt(xr, yr, cw, cb, wa, ba, wx, bx, lam, g, batch, seq, tc):
    nc = seq // tc
    cur = lambda b, c: (b * nc + c, 0)
    fix = lambda b, c: (0, 0)
    vec = pl.BlockSpec((1, D_RNN), fix)
    return pl.pallas_call(
        _rnn_prompt_kernel,
        grid=(batch, nc),
        in_specs=[pl.BlockSpec((tc, D_RNN), cur), pl.BlockSpec((tc, D_RNN), cur),
                  pl.BlockSpec((CONV_WIDTH, D_RNN), fix), vec,
                  pl.BlockSpec((D_RNN, D_RNN), fix), vec,
                  pl.BlockSpec((D_RNN, D_RNN), fix), vec, vec, vec],
        out_specs=(pl.BlockSpec((tc, D_RNN), cur), pl.BlockSpec((1, 1, D_RNN), lambda b, c: (b, 0, 0))),
        out_shape=(jax.ShapeDtypeStruct((batch * seq, D_RNN), F32),
                   jax.ShapeDtypeStruct((batch, 1, D_RNN), F32)),
        scratch_shapes=[pltpu.VMEM((tc + SUBLANES, D_RNN), F32), pltpu.VMEM((1, D_RNN), F32)],
        compiler_params=_cparams(2),
        name="rnn_prompt",
    )(xr, yr, cw, cb, wa, ba, wx, bx, lam, g)


def _rnn_sample_kernel(xr_ref, yr_ref, hist_ref, h0_ref, cw_ref, cb_ref, wa_ref, ba_ref, wx_ref, bx_ref,
                       lam_ref, g_ref, o_ref, hl_ref, nh_ref):
    cw = cw_ref[...]
    xr = xr_ref[...]
    xc = cb_ref[...] + xr * cw[CONV_WIDTH - 1:CONV_WIDTH, :]
    for w in range(CONV_WIDTH - 1):
        xc = xc + hist_ref[w] * cw[w:w + 1, :]
    a, u = _lru_gates(xc, wa_ref, ba_ref, wx_ref, bx_ref, lam_ref)
    h = a * h0_ref[...] + u
    hl_ref[...] = h
    o_ref[...] = _rmsnorm(jax.nn.gelu(yr_ref[...]) * h, g_ref[...])
    for w in range(CONV_WIDTH - 2):
        nh_ref[w] = hist_ref[w + 1]
    nh_ref[CONV_WIDTH - 2] = xr


def _rnn_sample(xr, yr, hist, h0, cw, cb, wa, ba, wx, bx, lam, g):
    n = xr.shape[0]
    full = lambda a: pl.BlockSpec(a.shape, lambda: (0,) * a.ndim)
    args = (xr, yr, hist, h0, cw, cb, wa, ba, wx, bx, lam, g)
    return pl.pallas_call(
        _rnn_sample_kernel,
        in_specs=[full(a) for a in args],
        out_specs=(pl.BlockSpec((n, D_RNN), lambda: (0, 0)), pl.BlockSpec((n, D_RNN), lambda: (0, 0)),
                   pl.BlockSpec(hist.shape, lambda: (0, 0, 0))),
        out_shape=(jax.ShapeDtypeStruct((n, D_RNN), F32), jax.ShapeDtypeStruct((n, D_RNN), F32),
                   jax.ShapeDtypeStruct(hist.shape, F32)),
        compiler_params=pltpu.CompilerParams(vmem_limit_bytes=VMEM_LIMIT),
        name="rnn_sample",
    )(*args)


def _mix_route_kernel(x_ref, an_ref, rn_ref, woa_ref, wor_ref, g_ref, wr2_ref, br_ref, tri_ref, low_ref,
                      x2_ref, ts_ref, dest_ref, gate_ref, n8_ref, off_ref):
    tt = x_ref.shape[0]
    tile_rows = ts_ref.shape[0]
    x2 = x_ref[...] + jnp.dot(an_ref[...].astype(BF16), woa_ref[...], preferred_element_type=F32) \
        + jnp.dot(rn_ref[...].astype(BF16), wor_ref[...], preferred_element_type=F32)
    x2_ref[...] = x2
    hn = _rmsnorm(x2, g_ref[...])

    nt = (((1,), (1,)), ((), ()))
    hb = hn.astype(BF16)
    hmid = (hn - hb.astype(F32)).astype(BF16)
    wr2 = wr2_ref[...]
    both = lax.dot_general(wr2, hb, nt, preferred_element_type=F32)
    logits = (lax.dot_general(wr2[:N_EXPERTS], hmid, nt, preferred_element_type=F32)
              + both[N_EXPERTS:]) + both[:N_EXPERTS] + br_ref[...]

    ie = lax.broadcasted_iota(I32, (N_EXPERTS, tt), 0).astype(F32)
    l = logits
    vals, sels = [], []
    for _ in range(TOP_K):
        m = jnp.max(l, axis=0, keepdims=True)
        idx = jnp.min(jnp.where(l == m, ie, float(N_EXPERTS)), axis=0, keepdims=True)
        sel = ie == idx
        vals.append(m)
        sels.append(sel)
        l = jnp.where(sel, NEG_BIG, l)
    es = [jnp.exp(v - vals[0]) for v in vals]
    den = es[0] + es[1] + es[2] + es[3]
    gate_ref[0] = jnp.concatenate([e / den for e in es], axis=0)

    oh = jnp.zeros((N_EXPERTS, tt), F32)
    for sel in sels:
        oh = oh + jnp.where(sel, 1.0, 0.0)
    before = jnp.dot(oh.astype(BF16), tri_ref[...], preferred_element_type=F32)
    cnt = jnp.sum(oh, axis=1, keepdims=True).astype(I32)
    n8 = ((cnt + (SUBLANES - 1)) >> 3) << 3
    n8b = jnp.broadcast_to(n8, (N_EXPERTS, LANES))
    off = jnp.dot(low_ref[...], n8b.astype(F32).astype(BF16), preferred_element_type=F32)
    n8_ref[0] = n8b
    off_ref[0] = off.astype(I32)
    base = off[:, 0:1] + before
    dests = [jnp.sum(jnp.where(sel, base, 0.0), axis=0, keepdims=True).astype(I32) for sel in sels]
    dest_ref[0] = jnp.concatenate(dests, axis=0)

    ri = lax.broadcasted_iota(I32, (DISPATCH_CHUNK, tt), 0)
    for c in range(tile_rows // DISPATCH_CHUNK):
        p = jnp.zeros((DISPATCH_CHUNK, tt), F32)
        for d in dests:
            p = jnp.where(ri == d - c * DISPATCH_CHUNK, 1.0, p)
        ts_ref[c * DISPATCH_CHUNK:(c + 1) * DISPATCH_CHUNK, :] = jnp.dot(
            p.astype(BF16), hb, preferred_element_type=F32)


def _tile_rows(tt):
    return -(-(TOP_K * tt + N_EXPERTS * (SUBLANES - 1)) // DISPATCH_CHUNK) * DISPATCH_CHUNK


def _mix_route(x2d, an, rn, woa, wor, g, wrt, br, low, tt):
    n = x2d.shape[0]
    nt = n // tt
    tile_rows = _tile_rows(tt)
    tri = jnp.triu(jnp.ones((tt, tt), BF16), k=1)
    row = lambda i: (i, 0)
    fix = lambda i: (0, 0)
    t3 = lambda i: (i, 0, 0)
    in_specs = [pl.BlockSpec((tt, D_MODEL), row), pl.BlockSpec((tt, D_ATTN), row),
                pl.BlockSpec((tt, D_RNN), row), pl.BlockSpec((D_ATTN, D_MODEL), fix),
                pl.BlockSpec((D_RNN, D_MODEL), fix), pl.BlockSpec((1, D_MODEL), fix),
                pl.BlockSpec((2 * N_EXPERTS, D_MODEL), fix), pl.BlockSpec((N_EXPERTS, 1), fix),
                pl.BlockSpec((tt, tt), fix), pl.BlockSpec((N_EXPERTS, N_EXPERTS), fix)]
    out_shape = (jax.ShapeDtypeStruct((n, D_MODEL), F32),
                 jax.ShapeDtypeStruct((nt * tile_rows, D_MODEL), F32),
                 jax.ShapeDtypeStruct((nt, TOP_K, tt), I32),
                 jax.ShapeDtypeStruct((nt, TOP_K, tt), F32),
                 jax.ShapeDtypeStruct((nt, N_EXPERTS, LANES), I32),
                 jax.ShapeDtypeStruct((nt, N_EXPERTS, LANES), I32))
    out_specs = (pl.BlockSpec((tt, D_MODEL), row),
                 pl.BlockSpec((tile_rows, D_MODEL), row),
                 pl.BlockSpec((1, TOP_K, tt), t3), pl.BlockSpec((1, TOP_K, tt), t3),
                 pl.BlockSpec((1, N_EXPERTS, LANES), t3), pl.BlockSpec((1, N_EXPERTS, LANES), t3))
    return pl.pallas_call(
        _mix_route_kernel,
        grid=(nt,),
        in_specs=in_specs,
        out_specs=out_specs,
        out_shape=out_shape,
        compiler_params=_cparams(1),
        name="mix_route",
    )(x2d, an, rn, woa, wor, g, wrt, br, tri, low)


def _start_piece(src_hbm, dst_buf, sem, p, psrc_ref, pdst_ref, plen_ref, nbits):
    l8 = plen_ref[p]

    @pl.when(l8 != 0)
    def _():
        s = psrc_ref[p]
        d = pdst_ref[p]
        for c in range(nbits):
            size = SUBLANES << c
            low = (l8 & ((1 << c) - 1)) * SUBLANES

            @pl.when(((l8 >> c) & 1) == 1)
            def _():
                pltpu.make_async_copy(
                    src_hbm.at[pl.ds(pl.multiple_of(s + low, SUBLANES), size)],
                    dst_buf.at[pl.ds(pl.multiple_of(d + low, SUBLANES), size)], sem).start()


def _start_pieces(src_hbm, dst_buf, sem, p_lo, p_hi, psrc_ref, pdst_ref, plen_ref, nbits):
    def body(p, carry):
        _start_piece(src_hbm, dst_buf, sem, p, psrc_ref, pdst_ref, plen_ref, nbits)
        return carry

    lax.fori_loop(p_lo, p_hi, body, 0)


def _wait_rows(src_hbm, dst_buf, sem, rows8, nbits):
    for c in range(nbits):
        size = SUBLANES << c

        @pl.when(((rows8 >> c) & 1) == 1)
        def _():
            pltpu.make_async_copy(src_hbm.at[pl.ds(0, size)], dst_buf.at[pl.ds(0, size)], sem).wait()


def _moe_gmm_kernel(be_ref, rows_ref, wslot_ref, nxt_ref, psa_ref, pea_ref, srca_ref, dsta_ref, lena_ref,
                    psb_ref, peb_ref, srcb_ref, dstb_ref, lenb_ref,
                    tsa_hbm, tsb_hbm, wgu_hbm, wdn_hbm, bg_ref, bu_ref, bd_ref, perm_ref,
                    ys_ref, lhs_ref, wgu_buf, wdn_buf, wg_ref, wu_ref, wd_ref, sem_ref, wsem_ref,
                    *, nbits_a, nbits_b):
    j = pl.program_id(0)
    nb = pl.num_programs(0)
    slot = j % 2
    bm = lhs_ref.shape[1]

    def gather(blk, sl):
        _start_pieces(tsa_hbm, lhs_ref.at[sl], sem_ref.at[sl], psa_ref[blk], pea_ref[blk],
                      srca_ref, dsta_ref, lena_ref, nbits_a)
        _start_pieces(tsb_hbm, lhs_ref.at[sl], sem_ref.at[sl], psb_ref[blk], peb_ref[blk],
                      srcb_ref, dstb_ref, lenb_ref, nbits_b)

    def weight_copies(e, ws):
        return (pltpu.make_async_copy(wgu_hbm.at[e], wgu_buf.at[ws], wsem_ref.at[ws]),
                pltpu.make_async_copy(wdn_hbm.at[e], wdn_buf.at[ws], wsem_ref.at[ws]))

    @pl.when(j == 0)
    def _():
        lhs_ref[...] = jnp.zeros(lhs_ref.shape, F32)
        gather(0, 0)
        for cp in weight_copies(be_ref[0], wslot_ref[0]):
            cp.start()

    @pl.when(j + 1 < nb)
    def _():
        gather(j + 1, 1 - slot)

    @pl.when(jnp.logical_or(j == 0, be_ref[j] != be_ref[jnp.maximum(j - 1, 0)]))
    def _():
        ws = wslot_ref[j]
        for cp in weight_copies(be_ref[j], ws):
            cp.wait()
        nxt = nxt_ref[j]

        @pl.when(nxt >= 0)
        def _():
            for cp in weight_copies(nxt, 1 - ws):
                cp.start()

        perm = perm_ref[...]
        half = PERM_COLS // 2
        for c in range(2 * D_FF // PERM_COLS):
            wb = wgu_buf[ws, :, c * PERM_COLS:(c + 1) * PERM_COLS].astype(BF16)
            wp = jnp.dot(wb, perm, preferred_element_type=F32).astype(BF16)
            wg_ref[:, c * half:(c + 1) * half] = wp[:, :half]
            wu_ref[:, c * half:(c + 1) * half] = wp[:, half:]
        wd_ref[...] = wdn_buf[ws].astype(BF16)

    rows8 = rows_ref[j]
    _wait_rows(tsa_hbm, lhs_ref.at[slot], sem_ref.at[slot], rows8, (bm // SUBLANES).bit_length())

    @pl.when(rows8 > 0)
    def _():
        x = lhs_ref[slot].astype(BF16)
        gate = jnp.dot(x, wg_ref[...], preferred_element_type=F32) + bg_ref[0]
        up = jnp.dot(x, wu_ref[...], preferred_element_type=F32) + bu_ref[0]
        gate = jnp.minimum(gate, SWIGLU_LIMIT)
        up = jnp.clip(up, -SWIGLU_LIMIT, SWIGLU_LIMIT)
        act = (up + 1.0) * (gate * jax.nn.sigmoid(SWIGLU_ALPHA * gate))
        ys_ref[...] = jnp.dot(act.astype(BF16), wd_ref[...], preferred_element_type=F32) + bd_ref[0]

    @pl.when(rows8 == 0)
    def _():
        ys_ref[...] = jnp.zeros(ys_ref.shape, F32)


def _moe_gmm(blocks, tabs_a, tabs_b, ts_a, ts_b, w_gu, w_dn, bg, bu, bd, perm, nblocks, bm, nbits_a, nbits_b):
    we = lambda j, be, *_: (be[j], 0, 0)
    grid_spec = pltpu.PrefetchScalarGridSpec(
        num_scalar_prefetch=14,
        grid=(nblocks,),
        in_specs=[pl.BlockSpec(memory_space=pl.ANY), pl.BlockSpec(memory_space=pl.ANY),
                  pl.BlockSpec(memory_space=pl.ANY), pl.BlockSpec(memory_space=pl.ANY),
                  pl.BlockSpec((1, 1, D_FF), we), pl.BlockSpec((1, 1, D_FF), we),
                  pl.BlockSpec((1, 1, D_MODEL), we),
                  pl.BlockSpec((PERM_COLS, PERM_COLS), lambda j, *_: (0, 0))],
        out_specs=pl.BlockSpec((bm, D_MODEL), lambda j, *_: (j, 0)),
        scratch_shapes=[pltpu.VMEM((2, bm, D_MODEL), F32),
                        pltpu.VMEM((2, D_MODEL, 2 * D_FF), F32), pltpu.VMEM((2, D_FF, D_MODEL), F32),
                        pltpu.VMEM((D_MODEL, D_FF), BF16), pltpu.VMEM((D_MODEL, D_FF), BF16),
                        pltpu.VMEM((D_FF, D_MODEL), BF16),
                        pltpu.SemaphoreType.DMA((2,)), pltpu.SemaphoreType.DMA((2,))],
    )
    return pl.pallas_call(
        functools.partial(_moe_gmm_kernel, nbits_a=nbits_a, nbits_b=nbits_b),
        grid_spec=grid_spec,
        out_shape=jax.ShapeDtypeStruct((nblocks * bm, D_MODEL), F32),
        compiler_params=_cparams(1),
        name="moe_gmm",
    )(*blocks, *tabs_a, *tabs_b, ts_a, ts_b, w_gu, w_dn, bg, bu, bd, perm)


def _combine_kernel(psrc_ref, pdst_ref, plen_ref, rows_ref, ys_hbm, x2_ref, dest_ref, gate_ref, o_ref,
                    buf_ref, db_ref, gb_ref, sem_ref, *, pieces_per_tile, nbits):
    i = pl.program_id(0)
    n = pl.num_programs(0)
    slot = i % 2
    tt = x2_ref.shape[0]
    tile_rows = buf_ref.shape[1]

    def gather(tile, sl):
        p0 = tile * pieces_per_tile
        _start_pieces(ys_hbm, buf_ref.at[sl], sem_ref.at[sl], p0, p0 + pieces_per_tile,
                      psrc_ref, pdst_ref, plen_ref, nbits)

    @pl.when(i == 0)
    def _():
        buf_ref[...] = jnp.zeros(buf_ref.shape, F32)
        gather(0, 0)

    @pl.when(i + 1 < n)
    def _():
        gather(i + 1, 1 - slot)

    _wait_rows(ys_hbm, buf_ref.at[slot], sem_ref.at[slot], rows_ref[i], (tile_rows // SUBLANES).bit_length())

    dest = dest_ref[0]
    gate = gate_ref[0]
    for k in range(TOP_K):
        db_ref[k] = jnp.broadcast_to(dest[:, k:k + 1], (tt, DISPATCH_CHUNK))
        gb_ref[k] = jnp.broadcast_to(gate[:, k:k + 1], (tt, DISPATCH_CHUNK))
    li = lax.broadcasted_iota(I32, (tt, DISPATCH_CHUNK), 1)
    gms = []
    for c in range(tile_rows // DISPATCH_CHUNK):
        lic = li + c * DISPATCH_CHUNK
        gm = jnp.zeros((tt, DISPATCH_CHUNK), F32)
        for k in range(TOP_K):
            gm = jnp.where(lic == db_ref[k], gb_ref[k], gm)
        gms.append(gm.astype(BF16))
    o_ref[...] = x2_ref[...] + jnp.dot(jnp.concatenate(gms, axis=1), buf_ref[slot].astype(BF16),
                                       preferred_element_type=F32)


def _combine(psrc, pdst, plen8, rows8, ys, x2, dest, gate, tt, nbits):
    n = x2.shape[0]
    nt = n // tt
    tile_rows = _tile_rows(tt)
    pieces_per_tile = 2 * N_EXPERTS
    grid_spec = pltpu.PrefetchScalarGridSpec(
        num_scalar_prefetch=4,
        grid=(nt,),
        in_specs=[pl.BlockSpec(memory_space=pl.ANY),
                  pl.BlockSpec((tt, D_MODEL), lambda i, *_: (i, 0)),
                  pl.BlockSpec((1, tt, TOP_K), lambda i, *_: (i, 0, 0)),
                  pl.BlockSpec((1, tt, TOP_K), lambda i, *_: (i, 0, 0))],
        out_specs=pl.BlockSpec((tt, D_MODEL), lambda i, *_: (i, 0)),
        scratch_shapes=[pltpu.VMEM((2, tile_rows, D_MODEL), F32),
                        pltpu.VMEM((TOP_K, tt, DISPATCH_CHUNK), I32), pltpu.VMEM((TOP_K, tt, DISPATCH_CHUNK), F32),
                        pltpu.SemaphoreType.DMA((2,))],
    )
    return pl.pallas_call(
        functools.partial(_combine_kernel, pieces_per_tile=pieces_per_tile, nbits=nbits),
        grid_spec=grid_spec,
        out_shape=jax.ShapeDtypeStruct((n, D_MODEL), F32),
        compiler_params=_cparams(1),
        name="combine",
    )(psrc, pdst, plen8, rows8, ys, x2, dest, gate)


def _piece_tables(n8_a, off_a, rows_a, n8_b, off_b, bm, nblocks):
    nta = n8_a.shape[0]
    n8 = jnp.concatenate([n8_a, n8_b], axis=0)
    seg_off = jnp.concatenate([off_a, off_b], axis=0)
    n_tiles = n8.shape[0]
    tile_base = jnp.concatenate([jnp.arange(nta, dtype=I32) * rows_a, jnp.zeros((n_tiles - nta,), I32)])[:, None]
    tot = jnp.sum(n8, axis=0)
    pos0 = jnp.cumsum(n8, axis=0) - n8
    nblk = (tot + bm - 1) // bm
    cs = jnp.cumsum(nblk)
    bs = cs - nblk
    kblk = pos0 // bm
    len0 = jnp.minimum(n8, (kblk + 1) * bm - pos0)
    len1 = n8 - len0
    b0 = bs[None, :] + kblk
    blk = jnp.stack([b0, jnp.where(len1 > 0, b0 + 1, b0)], axis=-1)
    plen8 = jnp.stack([len0, len1], axis=-1) // SUBLANES
    ts_row = jnp.stack([tile_base + seg_off, tile_base + seg_off + len0], axis=-1)
    in_blk = jnp.stack([pos0 - kblk * bm, jnp.zeros_like(pos0)], axis=-1)
    buf_row = jnp.stack([seg_off, seg_off + len0], axis=-1)
    ys_row = blk * bm + in_blk

    em = lambda a: jnp.transpose(a, (1, 0, 2)).reshape(-1).astype(I32)
    tm = lambda a: a.reshape(-1).astype(I32)
    jj = jnp.arange(nblocks, dtype=I32)

    def gmm_tabs(sl):
        blk_em = em(blk[sl])
        first = jnp.sum(blk_em[None, :] < jj[:, None], axis=1).astype(I32)
        last = jnp.sum(blk_em[None, :] <= jj[:, None], axis=1).astype(I32)
        return first, last, em(ts_row[sl]), em(in_blk[sl]), em(plen8[sl])

    def comb_tabs(sl):
        return tm(ys_row[sl]), tm(buf_row[sl]), tm(plen8[sl]), (jnp.sum(n8[sl], axis=1) // SUBLANES).astype(I32)

    count_le = lambda v: jnp.sum(cs[None, :] <= v[:, None], axis=1)
    n_active = cs[-1]
    e_last = count_le(jnp.maximum(n_active - 1, 0).reshape(1))[0]
    block_e = jnp.minimum(count_le(jj), e_last).astype(I32)
    ee = jnp.arange(N_EXPERTS, dtype=I32)
    mine = (jj[:, None] >= bs[None, :]) & (jj[:, None] < cs[None, :])
    left = jnp.clip(tot[None, :] - (jj[:, None] - bs[None, :]) * bm, 0, bm)
    rows8 = (jnp.sum(jnp.where(mine, left, 0), axis=1) // SUBLANES).astype(I32)
    has = nblk > 0
    run = jnp.cumsum(has.astype(I32)) - 1
    later = (ee[None, :] > ee[:, None]) & has[None, :]
    nxt_e = jnp.min(jnp.where(later, ee[None, :], N_EXPERTS), axis=1)
    nxt_e = jnp.where(nxt_e == N_EXPERTS, -1, nxt_e)
    own = block_e[:, None] == ee[None, :]
    wslot = (jnp.sum(jnp.where(own, run[None, :], 0), axis=1) % 2).astype(I32)
    nxt = jnp.sum(jnp.where(own, nxt_e[None, :], 0), axis=1).astype(I32)
    a, b = slice(0, nta), slice(nta, n_tiles)
    return (block_e, rows8, wslot, nxt), gmm_tabs(a), gmm_tabs(b), comb_tabs(a), comb_tabs(b)


def _block_diag(w):
    nb, bi, bo = w.shape
    eye = jnp.eye(nb, dtype=w.dtype)
    return (eye[:, None, :, None] * w[:, :, None, :]).reshape(nb * bi, nb * bo)


def _step(x_prompt, x_sample, cache_k, cache_v, state_conv, state_h, g_mix_norm, w_in, g_q_norm, g_k_norm,
          attn_sinks, conv_w, conv_b, w_lru_a, b_lru_a, w_lru_x, b_lru_x, lru_lambda, g_attn_out, g_rnn_out,
          w_out, g_ffn_norm, w_router, b_router, w_gate_up, b_gate_up, w_down, b_down,
          *, tm, tt, tc, bm, past_len):
    B, S, D = x_prompt.shape
    NS = x_sample.shape[0]
    assert x_sample.shape[1] == 1 and D == D_MODEL
    assert (B * S) % tt == 0 and (B * S) % tm == 0 and S % tc == 0 and S % ATTN_BLOCK == 0 and S % tm == 0
    assert NS % SUBLANES == 0 and tt <= bm
    assert tt % SUBLANES == 0 and NS <= bm
    n_pt = (B * S) // tt
    total_rows = TOP_K * (B * S + NS) + (n_pt + 1) * N_EXPERTS * (SUBLANES - 1)
    nblocks = -(-total_rows // bm) + N_EXPERTS
    nbits_p = (tt // SUBLANES).bit_length()
    nbits_s = (NS // SUBLANES).bit_length()

    l = 0
    row = lambda v: v[l].reshape(1, -1)
    w_in_bf = w_in[l].astype(BF16)
    gq2 = jnp.tile(g_q_norm[l], 2).reshape(1, LANES)
    gk2 = jnp.tile(g_k_norm[l], 2).reshape(1, LANES)
    wa = _block_diag(w_lru_a[l]).astype(BF16)
    wx = _block_diag(w_lru_x[l]).astype(BF16)
    ba = b_lru_a[l].reshape(1, D_RNN)
    bx = b_lru_x[l].reshape(1, D_RNN)
    wo = w_out[l].astype(BF16)
    woa, wor = wo[:D_ATTN], wo[D_ATTN:]
    wr = w_router[l].T
    wr_hi = wr.astype(BF16)
    wrt = jnp.concatenate([wr_hi, (wr - wr_hi.astype(F32)).astype(BF16)], axis=0)
    br = b_router[l].reshape(N_EXPERTS, 1)
    low = jnp.tril(jnp.ones((N_EXPERTS, N_EXPERTS), BF16), k=-1)
    bgu = b_gate_up[l].reshape(N_EXPERTS, D_FF, 2)
    bg = bgu[:, :, 0].reshape(N_EXPERTS, 1, D_FF)
    bu = bgu[:, :, 1].reshape(N_EXPERTS, 1, D_FF)
    bd = b_down[l].reshape(N_EXPERTS, 1, D_MODEL)
    half = PERM_COLS // 2
    pr = jnp.arange(PERM_COLS)
    perm = (pr[None, :] == jnp.where(pr % 2 == 0, pr // 2, half + pr // 2)[:, None]).astype(BF16)
    sinks = attn_sinks[l]

    ctab, s1tab, s2tab = _rope_tables(jnp.arange(S))
    q, k, v, xr, yr = _in_proj(x_prompt.reshape(B * S, D), row(g_mix_norm), w_in_bf, gq2, gk2,
                               ctab, s1tab, s2tab, tm)
    an, kt_p, vt_p = _attn_prompt(q, k, v, sinks, row(g_attn_out), B, S)
    rn, h_last_p = _rnn_prompt(xr, yr, conv_w[l], row(conv_b), wa, ba, wx, bx, row(lru_lambda),
                               row(g_rnn_out), B, S, tc)
    x2_p, ts_p, dest_p, gate_p, n8_p, off_p = _mix_route(
        x_prompt.reshape(B * S, D), an, rn, woa, wor, row(g_ffn_norm), wrt, br, low, tt)

    cs_tab = _rope_tables(jnp.full((NS,), past_len, I32))
    q_s, k_s, v_s, xr_s, yr_s = _in_proj(x_sample.reshape(NS, D), row(g_mix_norm), w_in_bf, gq2, gk2,
                                         *cs_tab, NS)
    to_rows = lambda c: jnp.transpose(c, (0, 2, 3, 1)).reshape(NS * KV_W, WINDOW)
    from_rows = lambda c, n: jnp.transpose(c.reshape(n, N_KV_HEADS, HEAD_DIM, WINDOW), (0, 3, 1, 2))[None]
    an_s, kt_s, vt_s = _attn_sample(q_s, k_s, v_s, to_rows(cache_k[l]), to_rows(cache_v[l]), sinks,
                                    row(g_attn_out), SUBLANES)
    rn_s, h_last_s, hist_s = _rnn_sample(xr_s, yr_s, jnp.transpose(state_conv[l], (1, 0, 2)), state_h[l],
                                         conv_w[l], row(conv_b), wa, ba, wx, bx, row(lru_lambda),
                                         row(g_rnn_out))
    x2_s, ts_s, dest_s, gate_s, n8_s, off_s = _mix_route(
        x_sample.reshape(NS, D), an_s, rn_s, woa, wor, row(g_ffn_norm), wrt, br, low, NS)

    blocks, gmm_p, gmm_s, comb_p, comb_s = _piece_tables(
        n8_p[:, :, 0], off_p[:, :, 0], _tile_rows(tt), n8_s[:, :, 0], off_s[:, :, 0], bm, nblocks)
    ys = _moe_gmm(blocks, gmm_p, gmm_s, ts_p, ts_s, w_gate_up[l], w_down[l], bg, bu, bd, perm,
                  nblocks, bm, nbits_p, nbits_s)
    tr = lambda a: jnp.transpose(a, (0, 2, 1))
    y_p = _combine(*comb_p, ys, x2_p, tr(dest_p), tr(gate_p), tt, nbits_p)
    y_s = _combine(*comb_s, ys, x2_s, tr(dest_s), tr(gate_s), NS, nbits_s)

    cp = xr.reshape(B, S, D_RNN)[:, S - (CONV_WIDTH - 1):]
    return (y_p.reshape(B, S, D), y_s.reshape(NS, 1, D),
            from_rows(kt_p, B), from_rows(vt_p, B), cp[None], h_last_p.reshape(1, B, D_RNN),
            from_rows(kt_s, NS), from_rows(vt_s, NS), jnp.transpose(hist_s, (1, 0, 2))[None], h_last_s[None])


def kernel(x_prompt, x_sample, cache_k, cache_v, state_conv, state_h, g_mix_norm, w_in, g_q_norm, g_k_norm, attn_sinks, conv_w, conv_b, w_lru_a, b_lru_a, w_lru_x, b_lru_x, lru_lambda, g_attn_out, g_rnn_out, w_out, g_ffn_norm, w_router, b_router, w_gate_up, b_gate_up, w_down, b_down):
    return _step(x_prompt, x_sample, cache_k, cache_v, state_conv, state_h, g_mix_norm, w_in, g_q_norm,
                 g_k_norm, attn_sinks, conv_w, conv_b, w_lru_a, b_lru_a, w_lru_x, b_lru_x, lru_lambda,
                 g_attn_out, g_rnn_out, w_out, g_ffn_norm, w_router, b_router, w_gate_up, b_gate_up,
                 w_down, b_down, tm=512, tt=512, tc=256, bm=MOE_BLOCK_ROWS, past_len=PAST_LEN)
```

```python
import functools

import jax
import jax.numpy as jnp
from jax import lax
from jax.experimental import pallas as pl
from jax.experimental.pallas import tpu as pltpu

F32 = jnp.float32
BF16 = jnp.bfloat16
I32 = jnp.int32
I16 = jnp.int16

D_MODEL = 1024
HEAD_DIM = 64
N_HEADS = 8
N_KV_HEADS = 2
GROUP = 4
WINDOW = 128
ATTN_BLOCK = 128
ROT_DIM = 16
ROPE_THETA = 500000.0
D_ATTN = 512
D_RNN = 512
KV_W = 128
D_IN = 1792
CONV_WIDTH = 4
LRU_C = 8.0
N_EXPERTS = 32
TOP_K = 4
D_FF = 1024
SWIGLU_LIMIT = 7.0
SWIGLU_ALPHA = 1.702
EPS = 1e-6
PAST_LEN = 8192

LANES = 128
SUBLANES = 8
NEG_BIG = -1e30
VMEM_LIMIT = 56 * 1024 * 1024

MOE_BLOCK_ROWS = 512
PERM_COLS = 256
DISPATCH_CHUNK = 256


def _cparams(n_axes):
    return pltpu.CompilerParams(dimension_semantics=("arbitrary",) * n_axes,
                                vmem_limit_bytes=VMEM_LIMIT)


def _rmsnorm(x, g):
    ms = jnp.mean(x * x, axis=-1, keepdims=True)
    return (x * lax.rsqrt(ms + EPS)) * g


def _in_proj_kernel(x_ref, g_ref, w_ref, gq_ref, gk_ref, c_ref, s1_ref, s2_ref,
                    q_ref, k_ref, v_ref, xr_ref, yr_ref):
    tm = x_ref.shape[0]
    h = _rmsnorm(x_ref[...], g_ref[...])
    proj = jnp.dot(h.astype(BF16), w_ref[...], preferred_element_type=F32)
    lo = lax.broadcasted_iota(I32, (tm, LANES), 1) < HEAD_DIM
    c = c_ref[...]
    s1 = s1_ref[...]
    s2 = s2_ref[...]

    def head_norm_rope(t, g):
        sq = t * t
        s_lo = jnp.sum(jnp.where(lo, sq, 0.0), axis=-1, keepdims=True)
        s_hi = jnp.sum(jnp.where(lo, 0.0, sq), axis=-1, keepdims=True)
        ms = jnp.where(lo, s_lo, s_hi) * (1.0 / HEAD_DIM)
        n = (t * lax.rsqrt(ms + EPS)) * g
        up = pltpu.roll(n, LANES - ROT_DIM // 2, 1)
        dn = pltpu.roll(n, ROT_DIM // 2, 1)
        return n * c + up * s1 + dn * s2

    gq = gq_ref[...]
    for j in range(D_ATTN // LANES):
        q_ref[:, j * LANES:(j + 1) * LANES] = head_norm_rope(proj[:, j * LANES:(j + 1) * LANES], gq)
    k_ref[...] = head_norm_rope(proj[:, D_ATTN:D_ATTN + KV_W], gk_ref[...])
    v_ref[...] = proj[:, D_ATTN + KV_W:D_ATTN + 2 * KV_W]
    o = D_ATTN + 2 * KV_W
    xr_ref[...] = proj[:, o:o + D_RNN]
    yr_ref[...] = proj[:, o + D_RNN:o + 2 * D_RNN]


def _in_proj(x2d, g, w_bf, gq2, gk2, ctab, s1tab, s2tab, tm):
    n = x2d.shape[0]
    ntab = ctab.shape[0] // tm
    row = lambda i: (i, 0)
    fix = lambda i: (0, 0)
    tab = lambda i: (i % ntab, 0)
    out_shapes = (jax.ShapeDtypeStruct((n, D_ATTN), F32), jax.ShapeDtypeStruct((n, KV_W), F32),
                  jax.ShapeDtypeStruct((n, KV_W), F32), jax.ShapeDtypeStruct((n, D_RNN), F32),
                  jax.ShapeDtypeStruct((n, D_RNN), F32))
    return pl.pallas_call(
        _in_proj_kernel,
        grid=(n // tm,),
        in_specs=[pl.BlockSpec((tm, D_MODEL), row), pl.BlockSpec((1, D_MODEL), fix),
                  pl.BlockSpec((D_MODEL, D_IN), fix), pl.BlockSpec((1, LANES), fix),
                  pl.BlockSpec((1, LANES), fix), pl.BlockSpec((tm, LANES), tab),
                  pl.BlockSpec((tm, LANES), tab), pl.BlockSpec((tm, LANES), tab)],
        out_specs=(pl.BlockSpec((tm, D_ATTN), row), pl.BlockSpec((tm, KV_W), row),
                   pl.BlockSpec((tm, KV_W), row), pl.BlockSpec((tm, D_RNN), row),
                   pl.BlockSpec((tm, D_RNN), row)),
        out_shape=out_shapes,
        compiler_params=_cparams(1),
        name="in_proj",
    )(x2d, g, w_bf, gq2, gk2, ctab, s1tab, s2tab)


def _rope_tables(pos):
    half = ROT_DIM // 2
    inv = ROPE_THETA ** (-jnp.arange(0, ROT_DIM, 2, dtype=F32) / ROT_DIM)
    ang = pos.astype(F32)[:, None] * inv[None, :]
    cos = jnp.cos(ang)
    sin = jnp.sin(ang)
    n = pos.shape[0]
    ones = jnp.ones((n, HEAD_DIM - ROT_DIM), F32)
    zeros = jnp.zeros((n, HEAD_DIM - ROT_DIM), F32)
    zh = jnp.zeros((n, half), F32)
    c = jnp.concatenate([cos, cos, ones], axis=1)
    s1 = jnp.concatenate([-sin, zh, zeros], axis=1)
    s2 = jnp.concatenate([zh, sin, zeros], axis=1)
    two = lambda t: jnp.concatenate([t, t], axis=1)
    return two(c), two(s1), two(s2)


def _band_bias(qb):
    qi = jnp.arange(qb, dtype=I32)[:, None]
    c = jnp.arange(2 * qb, dtype=I32)[None, :]
    band = (c >= qi) & (c <= qi + qb)
    first = band & (c >= qb)
    one = jnp.where(jnp.stack([first, band]), 0.0, NEG_BIG).astype(F32)
    return jnp.concatenate([one, one], axis=2)


def _attn_prompt_kernel(sink_ref, q_ref, kc_ref, kp_ref, vc_ref, vp_ref, bias_ref, g_ref,
                        o_ref, kt_ref, vt_ref, s_ref, e_ref):
    j = pl.program_id(1)
    qb = ATTN_BLOCK
    kc = kc_ref[...]
    vc = vc_ref[...]
    k2 = jnp.concatenate([kp_ref[...], kc], axis=0)
    v2 = jnp.concatenate([vp_ref[...], vc], axis=0)
    k2r = pltpu.roll(k2, HEAD_DIM, 1)
    v2r = pltpu.roll(v2, HEAD_DIM, 1)
    lo_k = lax.broadcasted_iota(I32, (2 * qb, LANES), 1) < HEAD_DIM
    lo_q = lax.broadcasted_iota(I32, (qb, LANES), 1) < HEAD_DIM
    bias = bias_ref[0]
    nt = (((1,), (1,)), ((), ()))
    n_pairs = N_HEADS // 2
    kbd, vbd = [], []
    for kv in range(N_KV_HEADS):
        ka, kb = (k2, k2r) if kv == 0 else (k2r, k2)
        va, vb = (v2, v2r) if kv == 0 else (v2r, v2)
        kbd.append(jnp.concatenate([jnp.where(lo_k, ka, 0.0), jnp.where(lo_k, 0.0, kb)], axis=0).astype(BF16))
        vbd.append(jnp.concatenate([jnp.where(lo_k, va, 0.0), jnp.where(lo_k, 0.0, vb)], axis=0).astype(BF16))
    for pp in range(n_pairs):
        qp = (q_ref[:, pp * LANES:(pp + 1) * LANES] * (HEAD_DIM ** -0.5)).astype(BF16)
        s_ref[pp] = lax.dot_general(qp, kbd[pp // (GROUP // 2)], nt, preferred_element_type=F32) + bias
    invs = []
    for pp in range(n_pairs):
        inv = []
        for t in range(2):
            cols = slice(t * 2 * qb, (t + 1) * 2 * qb)
            st = s_ref[pp, :, cols]
            sink = sink_ref[2 * pp + t]
            m = jnp.maximum(jnp.max(st, axis=-1, keepdims=True), sink)
            e = jnp.exp(st - m)
            e_ref[pp, :, cols] = e.astype(BF16)
            inv.append(1.0 / (jnp.sum(e, axis=-1, keepdims=True) + jnp.exp(sink - m)))
        invs.append(jnp.where(lo_q, inv[0], inv[1]))
    outs = [jnp.dot(e_ref[pp], vbd[pp // (GROUP // 2)], preferred_element_type=F32) * invs[pp]
            for pp in range(n_pairs)]
    o_ref[...] = _rmsnorm(jnp.concatenate(outs, axis=1), g_ref[...])

    @pl.when(j == pl.num_programs(1) - 1)
    def _():
        kt_ref[0] = kc.T
        vt_ref[0] = vc.T


def _attn_prompt(q, k, v, sinks, g_attn, batch, seq):
    qb = ATTN_BLOCK
    nb = seq // qb
    cur = lambda b, j: (b * nb + j, 0)
    prev = lambda b, j: (b * nb + jnp.maximum(j - 1, 0), 0)
    fix = lambda b, j: (0, 0)
    per_b = lambda b, j: (b, 0, 0)
    return pl.pallas_call(
        _attn_prompt_kernel,
        grid=(batch, nb),
        in_specs=[pl.BlockSpec(memory_space=pltpu.SMEM),
                  pl.BlockSpec((qb, D_ATTN), cur),
                  pl.BlockSpec((qb, KV_W), cur), pl.BlockSpec((qb, KV_W), prev),
                  pl.BlockSpec((qb, KV_W), cur), pl.BlockSpec((qb, KV_W), prev),
                  pl.BlockSpec((1, qb, 4 * qb), lambda b, j: (jnp.minimum(j, 1), 0, 0)),
                  pl.BlockSpec((1, D_ATTN), fix)],
        out_specs=(pl.BlockSpec((qb, D_ATTN), cur), pl.BlockSpec((1, KV_W, qb), per_b),
                   pl.BlockSpec((1, KV_W, qb), per_b)),
        out_shape=(jax.ShapeDtypeStruct((batch * seq, D_ATTN), F32),
                   jax.ShapeDtypeStruct((batch, KV_W, qb), F32),
                   jax.ShapeDtypeStruct((batch, KV_W, qb), F32)),
        scratch_shapes=[pltpu.VMEM((N_HEADS // 2, qb, 4 * qb), F32),
                        pltpu.VMEM((N_HEADS // 2, qb, 4 * qb), BF16)],
        compiler_params=_cparams(2),
        name="attn_prompt",
    )(sinks, q, k, k, v, v, _band_bias(qb), g_attn)


def _attn_sample_kernel(sink_ref, q_ref, kn_ref, vn_ref, kt_ref, vt_ref, g_ref,
                        o_ref, nkt_ref, nvt_ref, acc_ref):
    bb = q_ref.shape[0]
    q = q_ref[...] * (HEAD_DIM ** -0.5)
    kn = kn_ref[...]
    vn = vn_ref[...]
    kt = kt_ref[...]
    vt = vt_ref[...]
    col = lax.broadcasted_iota(I32, (bb, bb * KV_W), 1)
    rowb = lax.broadcasted_iota(I32, (bb, bb * KV_W), 0)
    own_seq = (col >> (KV_W.bit_length() - 1)) == rowb
    half_hi = ((col >> (HEAD_DIM.bit_length() - 1)) & 1) == 1
    qbig = []
    for h in range(N_HEADS):
        kv = h // GROUP
        pair = q[:, (h // 2) * LANES:(h // 2 + 1) * LANES]
        if (h % 2) != kv:
            pair = pltpu.roll(pair, HEAD_DIM, 1)
        tiled = jnp.concatenate([pair] * bb, axis=1)
        keep = own_seq & (half_hi if kv == 1 else jnp.logical_not(half_hi))
        qbig.append(jnp.where(keep, tiled, 0.0))
    qbig = jnp.concatenate(qbig, axis=0)
    s = jnp.dot(qbig.astype(BF16), kt.astype(BF16), preferred_element_type=F32)
    qb16 = q.astype(BF16).astype(F32)
    kb16 = kn.astype(BF16).astype(F32)
    s_new, sink = [], []
    for h in range(N_HEADS):
        kv = h // GROUP
        s_new.append(jnp.sum(qb16[:, h * HEAD_DIM:(h + 1) * HEAD_DIM] * kb16[:, kv * HEAD_DIM:(kv + 1) * HEAD_DIM],
                             axis=-1, keepdims=True))
        sink.append(jnp.full((bb, 1), sink_ref[h], F32))
    s_new = jnp.concatenate(s_new, axis=0)
    sink = jnp.concatenate(sink, axis=0)
    m = jnp.maximum(jnp.maximum(jnp.max(s, axis=-1, keepdims=True), s_new), sink)
    e = jnp.exp(s - m)
    e_new = jnp.exp(s_new - m)
    inv = 1.0 / (jnp.sum(e, axis=-1, keepdims=True) + e_new + jnp.exp(sink - m))
    obig = lax.dot_general(e.astype(BF16), vt.astype(BF16), (((1,), (1,)), ((), ())),
                           preferred_element_type=F32)
    for h in range(N_HEADS):
        kv = h // GROUP
        blk = jnp.where(own_seq, obig[h * bb:(h + 1) * bb, :], 0.0)
        fold = blk[:, 0:KV_W]
        for t in range(1, bb):
            fold = fold + blk[:, t * KV_W:(t + 1) * KV_W]
        hs = slice(h * bb, (h + 1) * bb)
        ks = slice(kv * HEAD_DIM, (kv + 1) * HEAD_DIM)
        acc_ref[:, h * HEAD_DIM:(h + 1) * HEAD_DIM] = (fold[:, ks] + e_new[hs] * vn[:, ks]) * inv[hs]
    o_ref[...] = _rmsnorm(acc_ref[...], g_ref[...])

    last = lax.broadcasted_iota(I32, (KV_W, WINDOW), 1) == WINDOW - 1
    for b in range(bb):
        rs = slice(b * KV_W, (b + 1) * KV_W)
        kcol = jnp.broadcast_to(kn[b:b + 1, :], (KV_W, KV_W)).T
        vcol = jnp.broadcast_to(vn[b:b + 1, :], (KV_W, KV_W)).T
        nkt_ref[rs, :] = jnp.where(last, kcol, pltpu.roll(kt[rs, :], WINDOW - 1, 1))
        nvt_ref[rs, :] = jnp.where(last, vcol, pltpu.roll(vt[rs, :], WINDOW - 1, 1))


def _attn_sample(q, kn, vn, kt2d, vt2d, sinks, g_attn, bb):
    n = q.shape[0]
    row = lambda i: (i, 0)
    fix = lambda i: (0, 0)
    cache = pl.BlockSpec((bb * KV_W, WINDOW), row)
    return pl.pallas_call(
        _attn_sample_kernel,
        grid=(n // bb,),
        in_specs=[pl.BlockSpec(memory_space=pltpu.SMEM),
                  pl.BlockSpec((bb, D_ATTN), row), pl.BlockSpec((bb, KV_W), row),
                  pl.BlockSpec((bb, KV_W), row), cache, cache,
                  pl.BlockSpec((1, D_ATTN), fix)],
        out_specs=(pl.BlockSpec((bb, D_ATTN), row), cache, cache),
        out_shape=(jax.ShapeDtypeStruct((n, D_ATTN), F32),
                   jax.ShapeDtypeStruct(kt2d.shape, F32), jax.ShapeDtypeStruct(vt2d.shape, F32)),
        scratch_shapes=[pltpu.VMEM((bb, D_ATTN), F32)],
        compiler_params=_cparams(1),
        name="attn_sample",
    )(sinks, q, kn, vn, kt2d, vt2d, g_attn)


def _softplus(z):
    return jnp.maximum(z, 0.0) + jnp.log1p(jnp.exp(-jnp.abs(z)))


def _lru_gates(xc, wa_ref, ba_ref, wx_ref, bx_ref, lam_ref):
    xb = xc.astype(BF16)
    r = jax.nn.sigmoid(jnp.dot(xb, wa_ref[...], preferred_element_type=F32) + ba_ref[...])
    i = jax.nn.sigmoid(jnp.dot(xb, wx_ref[...], preferred_element_type=F32) + bx_ref[...])
    log_a = (-LRU_C * r) * _softplus(-lam_ref[...])
    a = jnp.exp(log_a)
    u = jnp.sqrt(-jnp.tanh(log_a) * (a * a + 1.0)) * (i * xc)
    return a, u


def _rnn_prompt_kernel(xr_ref, yr_ref, cw_ref, cb_ref, wa_ref, ba_ref, wx_ref, bx_ref, lam_ref, g_ref,
                       o_ref, hl_ref, ext_ref, h_ref):
    c = pl.program_id(1)
    tc = xr_ref.shape[0]
    pad = SUBLANES

    @pl.when(c == 0)
    def _():
        ext_ref[0:pad, :] = jnp.zeros((pad, D_RNN), F32)
        h_ref[...] = jnp.zeros((1, D_RNN), F32)

    ext_ref[pad:pad + tc, :] = xr_ref[...]
    cw = cw_ref[...]
    xc = cb_ref[...] + ext_ref[pad:pad + tc, :] * cw[CONV_WIDTH - 1:CONV_WIDTH, :]
    for w in range(CONV_WIDTH - 1):
        sh = CONV_WIDTH - 1 - w
        xc = xc + ext_ref[pad - sh:pad - sh + tc, :] * cw[w:w + 1, :]
    ext_ref[0:pad, :] = ext_ref[tc:tc + pad, :]

    a, u = _lru_gates(xc, wa_ref, ba_ref, wx_ref, bx_ref, lam_ref)

    ng = tc // SUBLANES
    a3 = a.reshape(ng, SUBLANES, D_RNN)
    u3 = u.reshape(ng, SUBLANES, D_RNN)
    t8 = lax.broadcasted_iota(I32, (ng, SUBLANES, D_RNN), 1)
    d = 1
    while d < SUBLANES:
        a_s = jnp.where(t8 >= d, pltpu.roll(a3, d, 1), 1.0)
        u_s = jnp.where(t8 >= d, pltpu.roll(u3, d, 1), 0.0)
        u3 = a3 * u_s + u3
        a3 = a3 * a_s
        d *= 2
    carry = h_ref[...]
    groups = []
    for g in range(ng):
        hg = a3[g] * carry + u3[g]
        groups.append(hg)
        carry = hg[SUBLANES - 1:SUBLANES, :]
    h = jnp.concatenate(groups, axis=0)
    h_ref[...] = carry
    hl_ref[0] = carry
    o_ref[...] = _rmsnorm(jax.nn.gelu(yr_ref[...]) * h, g_ref[...])


def _rnn_prompt(xr, yr, cw, cb, wa, ba, wx, bx, lam, g, batch, seq, tc):
    nc = seq // tc
    cur = lambda b, c: (b * nc + c, 0)
    fix = lambda b, c: (0, 0)
    vec = pl.BlockSpec((1, D_RNN), fix)
    return pl.pallas_call(
        _rnn_prompt_kernel,
        grid=(batch, nc),
        in_specs=[pl.BlockSpec((tc, D_RNN), cur), pl.BlockSpec((tc, D_RNN), cur),
                  pl.BlockSpec((CONV_WIDTH, D_RNN), fix), vec,
                  pl.BlockSpec((D_RNN, D_RNN), fix), vec,
                  pl.BlockSpec((D_RNN, D_RNN), fix), vec, vec, vec],
        out_specs=(pl.BlockSpec((tc, D_RNN), cur), pl.BlockSpec((1, 1, D_RNN), lambda b, c: (b, 0, 0))),
        out_shape=(jax.ShapeDtypeStruct((batch * seq, D_RNN), F32),
                   jax.ShapeDtypeStruct((batch, 1, D_RNN), F32)),
        scratch_shapes=[pltpu.VMEM((tc + SUBLANES, D_RNN), F32), pltpu.VMEM((1, D_RNN), F32)],
        compiler_params=_cparams(2),
        name="rnn_prompt",
    )(xr, yr, cw, cb, wa, ba, wx, bx, lam, g)


def _rnn_sample_kernel(xr_ref, yr_ref, hist_ref, h0_ref, cw_ref, cb_ref, wa_ref, ba_ref, wx_ref, bx_ref,
                       lam_ref, g_ref, o_ref, hl_ref, nh_ref):
    cw = cw_ref[...]
    xr = xr_ref[...]
    xc = cb_ref[...] + xr * cw[CONV_WIDTH - 1:CONV_WIDTH, :]
    for w in range(CONV_WIDTH - 1):
        xc = xc + hist_ref[w] * cw[w:w + 1, :]
    a, u = _lru_gates(xc, wa_ref, ba_ref, wx_ref, bx_ref, lam_ref)
    h = a * h0_ref[...] + u
    hl_ref[...] = h
    o_ref[...] = _rmsnorm(jax.nn.gelu(yr_ref[...]) * h, g_ref[...])
    for w in range(CONV_WIDTH - 2):
        nh_ref[w] = hist_ref[w + 1]
    nh_ref[CONV_WIDTH - 2] = xr


def _rnn_sample(xr, yr, hist, h0, cw, cb, wa, ba, wx, bx, lam, g):
    n = xr.shape[0]
    full = lambda a: pl.BlockSpec(a.shape, lambda: (0,) * a.ndim)
    args = (xr, yr, hist, h0, cw, cb, wa, ba, wx, bx, lam, g)
    return pl.pallas_call(
        _rnn_sample_kernel,
        in_specs=[full(a) for a in args],
        out_specs=(pl.BlockSpec((n, D_RNN), lambda: (0, 0)), pl.BlockSpec((n, D_RNN), lambda: (0, 0)),
                   pl.BlockSpec(hist.shape, lambda: (0, 0, 0))),
        out_shape=(jax.ShapeDtypeStruct((n, D_RNN), F32), jax.ShapeDtypeStruct((n, D_RNN), F32),
                   jax.ShapeDtypeStruct(hist.shape, F32)),
        compiler_params=pltpu.CompilerParams(vmem_limit_bytes=VMEM_LIMIT),
        name="rnn_sample",
    )(*args)


def _mix_route_kernel(x_ref, an_ref, rn_ref, woa_ref, wor_ref, g_ref, wr2_ref, br_ref, tri_ref, low_ref,
                      x2_ref, ts_ref, dest_ref, gate_ref, n8_ref, off_ref):
    tt = x_ref.shape[0]
    tile_rows = ts_ref.shape[0]
    x2 = x_ref[...] + jnp.dot(an_ref[...].astype(BF16), woa_ref[...], preferred_element_type=F32) \
        + jnp.dot(rn_ref[...].astype(BF16), wor_ref[...], preferred_element_type=F32)
    x2_ref[...] = x2
    hn = _rmsnorm(x2, g_ref[...])

    nt = (((1,), (1,)), ((), ()))
    hb = hn.astype(BF16)
    hmid = (hn - hb.astype(F32)).astype(BF16)
    wr2 = wr2_ref[...]
    both = lax.dot_general(wr2, hb, nt, preferred_element_type=F32)
    logits = (lax.dot_general(wr2[:N_EXPERTS], hmid, nt, preferred_element_type=F32)
              + both[N_EXPERTS:]) + both[:N_EXPERTS] + br_ref[...]

    ie = lax.broadcasted_iota(I32, (N_EXPERTS, tt), 0).astype(F32)
    l = logits
    vals, sels = [], []
    for _ in range(TOP_K):
        m = jnp.max(l, axis=0, keepdims=True)
        idx = jnp.min(jnp.where(l == m, ie, float(N_EXPERTS)), axis=0, keepdims=True)
        sel = ie == idx
        vals.append(m)
        sels.append(sel)
        l = jnp.where(sel, NEG_BIG, l)
    es = [jnp.exp(v - vals[0]) for v in vals]
    den = es[0] + es[1] + es[2] + es[3]
    gate_ref[0] = jnp.concatenate([e / den for e in es], axis=0)

    oh = jnp.zeros((N_EXPERTS, tt), F32)
    for sel in sels:
        oh = oh + jnp.where(sel, 1.0, 0.0)
    before = jnp.dot(oh.astype(BF16), tri_ref[...], preferred_element_type=F32)
    cnt = jnp.sum(oh, axis=1, keepdims=True).astype(I32)
    n8 = ((cnt + (SUBLANES - 1)) >> 3) << 3
    n8b = jnp.broadcast_to(n8, (N_EXPERTS, LANES))
    off = jnp.dot(low_ref[...], n8b.astype(F32).astype(BF16), preferred_element_type=F32)
    n8_ref[0] = n8b
    off_ref[0] = off.astype(I32)
    base = off[:, 0:1] + before
    dests = [jnp.sum(jnp.where(sel, base, 0.0), axis=0, keepdims=True).astype(I32) for sel in sels]
    dest_ref[0] = jnp.concatenate(dests, axis=0)

    ri = lax.broadcasted_iota(I32, (DISPATCH_CHUNK, tt), 0).astype(I16)
    d16 = [d.astype(I16) for d in dests]
    one = jnp.ones((DISPATCH_CHUNK, tt), BF16)
    for c in range(tile_rows // DISPATCH_CHUNK):
        p = jnp.zeros((DISPATCH_CHUNK, tt), BF16)
        for d in d16:
            p = jnp.where(ri == d - jnp.int16(c * DISPATCH_CHUNK), one, p)
        ts_ref[c * DISPATCH_CHUNK:(c + 1) * DISPATCH_CHUNK, :] = jnp.dot(
            p, hb, preferred_element_type=F32)


def _tile_rows(tt):
    return -(-(TOP_K * tt + N_EXPERTS * (SUBLANES - 1)) // DISPATCH_CHUNK) * DISPATCH_CHUNK


def _mix_route(x2d, an, rn, woa, wor, g, wrt, br, low, tt):
    n = x2d.shape[0]
    nt = n // tt
    tile_rows = _tile_rows(tt)
    tri = jnp.triu(jnp.ones((tt, tt), BF16), k=1)
    row = lambda i: (i, 0)
    fix = lambda i: (0, 0)
    t3 = lambda i: (i, 0, 0)
    in_specs = [pl.BlockSpec((tt, D_MODEL), row), pl.BlockSpec((tt, D_ATTN), row),
                pl.BlockSpec((tt, D_RNN), row), pl.BlockSpec((D_ATTN, D_MODEL), fix),
                pl.BlockSpec((D_RNN, D_MODEL), fix), pl.BlockSpec((1, D_MODEL), fix),
                pl.BlockSpec((2 * N_EXPERTS, D_MODEL), fix), pl.BlockSpec((N_EXPERTS, 1), fix),
                pl.BlockSpec((tt, tt), fix), pl.BlockSpec((N_EXPERTS, N_EXPERTS), fix)]
    out_shape = (jax.ShapeDtypeStruct((n, D_MODEL), F32),
                 jax.ShapeDtypeStruct((nt * tile_rows, D_MODEL), F32),
                 jax.ShapeDtypeStruct((nt, TOP_K, tt), I32),
                 jax.ShapeDtypeStruct((nt, TOP_K, tt), F32),
                 jax.ShapeDtypeStruct((nt, N_EXPERTS, LANES), I32),
                 jax.ShapeDtypeStruct((nt, N_EXPERTS, LANES), I32))
    out_specs = (pl.BlockSpec((tt, D_MODEL), row),
                 pl.BlockSpec((tile_rows, D_MODEL), row),
                 pl.BlockSpec((1, TOP_K, tt), t3), pl.BlockSpec((1, TOP_K, tt), t3),
                 pl.BlockSpec((1, N_EXPERTS, LANES), t3), pl.BlockSpec((1, N_EXPERTS, LANES), t3))
    return pl.pallas_call(
        _mix_route_kernel,
        grid=(nt,),
        in_specs=in_specs,
        out_specs=out_specs,
        out_shape=out_shape,
        compiler_params=_cparams(1),
        name="mix_route",
    )(x2d, an, rn, woa, wor, g, wrt, br, tri, low)


LOW_BITS = 4


def _start_piece(src_hbm, dst_buf, sem, s, d, l8, nbits):
    def bit_copy(c):
        size = SUBLANES << c
        low = (l8 & ((1 << c) - 1)) * SUBLANES

        @pl.when(((l8 >> c) & 1) == 1)
        def _():
            pltpu.make_async_copy(
                src_hbm.at[pl.ds(pl.multiple_of(s + low, SUBLANES), size)],
                dst_buf.at[pl.ds(pl.multiple_of(d + low, SUBLANES), size)], sem).start()

    for c in range(min(LOW_BITS, nbits)):
        bit_copy(c)
    if nbits > LOW_BITS:
        @pl.when(l8 >= (1 << LOW_BITS))
        def _():
            for c in range(LOW_BITS, nbits):
                bit_copy(c)


def _start_pieces(src_hbm, dst_buf, sem, p_lo, p_hi, psrc_ref, pdst_ref, plen_ref, nbits):
    def body(p, carry):
        _start_piece(src_hbm, dst_buf, sem, psrc_ref[p], pdst_ref[p], plen_ref[p], nbits)
        return carry

    lax.fori_loop(p_lo, p_hi, body, 0)


def _wait_rows(src_hbm, dst_buf, sem, rows8, nbits):
    for c in range(nbits):
        size = SUBLANES << c

        @pl.when(((rows8 >> c) & 1) == 1)
        def _():
            pltpu.make_async_copy(src_hbm.at[pl.ds(0, size)], dst_buf.at[pl.ds(0, size)], sem).wait()


def _moe_gmm_kernel(be_ref, rows_ref, wslot_ref, nxt_ref,
                    psa_ref, pea_ref, srca_ref, dsta_ref, lena_ref, hsrca_ref, hlena_ref,
                    psb_ref, peb_ref, srcb_ref, dstb_ref, lenb_ref, hsrcb_ref, hlenb_ref,
                    tsa_hbm, tsb_hbm, wgu_hbm, wdn_hbm, bg_ref, bu_ref, bd_ref, perm_ref,
                    ys_ref, lhs_ref, wgu_buf, wdn_buf, wg_ref, wu_ref, wd_ref, sem_ref, wsem_ref,
                    *, nbits_a, nbits_b):
    j = pl.program_id(0)
    nb = pl.num_programs(0)
    slot = j % 2
    bm = lhs_ref.shape[1]

    def gather(blk, sl):
        for ts_hbm, ps, pe, src, dst, ln, hsrc, hlen, nbits in (
                (tsa_hbm, psa_ref, pea_ref, srca_ref, dsta_ref, lena_ref, hsrca_ref, hlena_ref, nbits_a),
                (tsb_hbm, psb_ref, peb_ref, srcb_ref, dstb_ref, lenb_ref, hsrcb_ref, hlenb_ref, nbits_b)):
            _start_pieces(ts_hbm, lhs_ref.at[sl], sem_ref.at[sl], ps[blk], pe[blk], src, dst, ln, nbits)
            _start_piece(ts_hbm, lhs_ref.at[sl], sem_ref.at[sl], hsrc[blk], 0, hlen[blk], nbits)

    def weight_copies(e, ws):
        return (pltpu.make_async_copy(wgu_hbm.at[e], wgu_buf.at[ws], wsem_ref.at[ws]),
                pltpu.make_async_copy(wdn_hbm.at[e], wdn_buf.at[ws], wsem_ref.at[ws]))

    @pl.when(j == 0)
    def _():
        lhs_ref[...] = jnp.zeros(lhs_ref.shape, F32)
        gather(0, 0)
        for cp in weight_copies(be_ref[0], wslot_ref[0]):
            cp.start()

    @pl.when(j + 1 < nb)
    def _():
        gather(j + 1, 1 - slot)

    @pl.when(jnp.logical_or(j == 0, be_ref[j] != be_ref[jnp.maximum(j - 1, 0)]))
    def _():
        ws = wslot_ref[j]
        for cp in weight_copies(be_ref[j], ws):
            cp.wait()
        nxt = nxt_ref[j]

        @pl.when(nxt >= 0)
        def _():
            for cp in weight_copies(nxt, 1 - ws):
                cp.start()

        perm = perm_ref[...]
        half = PERM_COLS // 2
        for c in range(2 * D_FF // PERM_COLS):
            wb = wgu_buf[ws, :, c * PERM_COLS:(c + 1) * PERM_COLS].astype(BF16)
            wp = jnp.dot(wb, perm, preferred_element_type=F32).astype(BF16)
            wg_ref[:, c * half:(c + 1) * half] = wp[:, :half]
            wu_ref[:, c * half:(c + 1) * half] = wp[:, half:]
        wd_ref[...] = wdn_buf[ws].astype(BF16)

    rows8 = rows_ref[j]
    _wait_rows(tsa_hbm, lhs_ref.at[slot], sem_ref.at[slot], rows8, (bm // SUBLANES).bit_length())

    @pl.when(rows8 > 0)
    def _():
        x = lhs_ref[slot].astype(BF16)
        gate = jnp.dot(x, wg_ref[...], preferred_element_type=F32) + bg_ref[0]
        up = jnp.dot(x, wu_ref[...], preferred_element_type=F32) + bu_ref[0]
        gate = jnp.minimum(gate, SWIGLU_LIMIT)
        up = jnp.clip(up, -SWIGLU_LIMIT, SWIGLU_LIMIT)
        act = (up + 1.0) * (gate * jax.nn.sigmoid(SWIGLU_ALPHA * gate))
        ys_ref[...] = jnp.dot(act.astype(BF16), wd_ref[...], preferred_element_type=F32) + bd_ref[0]

    @pl.when(rows8 == 0)
    def _():
        ys_ref[...] = jnp.zeros(ys_ref.shape, F32)


def _moe_gmm(blocks, tabs_a, tabs_b, ts_a, ts_b, w_gu, w_dn, bg, bu, bd, perm, nblocks, bm, nbits_a, nbits_b):
    we = lambda j, be, *_: (be[j], 0, 0)
    grid_spec = pltpu.PrefetchScalarGridSpec(
        num_scalar_prefetch=18,
        grid=(nblocks,),
        in_specs=[pl.BlockSpec(memory_space=pl.ANY), pl.BlockSpec(memory_space=pl.ANY),
                  pl.BlockSpec(memory_space=pl.ANY), pl.BlockSpec(memory_space=pl.ANY),
                  pl.BlockSpec((1, 1, D_FF), we), pl.BlockSpec((1, 1, D_FF), we),
                  pl.BlockSpec((1, 1, D_MODEL), we),
                  pl.BlockSpec((PERM_COLS, PERM_COLS), lambda j, *_: (0, 0))],
        out_specs=pl.BlockSpec((bm, D_MODEL), lambda j, *_: (j, 0)),
        scratch_shapes=[pltpu.VMEM((2, bm, D_MODEL), F32),
                        pltpu.VMEM((2, D_MODEL, 2 * D_FF), F32), pltpu.VMEM((2, D_FF, D_MODEL), F32),
                        pltpu.VMEM((D_MODEL, D_FF), BF16), pltpu.VMEM((D_MODEL, D_FF), BF16),
                        pltpu.VMEM((D_FF, D_MODEL), BF16),
                        pltpu.SemaphoreType.DMA((2,)), pltpu.SemaphoreType.DMA((2,))],
    )
    return pl.pallas_call(
        functools.partial(_moe_gmm_kernel, nbits_a=nbits_a, nbits_b=nbits_b),
        grid_spec=grid_spec,
        out_shape=jax.ShapeDtypeStruct((nblocks * bm, D_MODEL), F32),
        compiler_params=_cparams(1),
        name="moe_gmm",
    )(*blocks, *tabs_a, *tabs_b, ts_a, ts_b, w_gu, w_dn, bg, bu, bd, perm)


def _combine_kernel(psrc_ref, pdst_ref, plen_ref, tlo_ref, thi_ref, tsrc_ref, tdst_ref, tlen_ref, rows_ref,
                    ys_hbm, x2_ref, dest_ref, gate_ref, o_ref,
                    buf_ref, db_ref, gb_ref, sem_ref, *, nbits):
    i = pl.program_id(0)
    n = pl.num_programs(0)
    slot = i % 2
    tt = x2_ref.shape[0]
    tile_rows = buf_ref.shape[1]

    def gather(tile, sl):
        _start_pieces(ys_hbm, buf_ref.at[sl], sem_ref.at[sl], tile * N_EXPERTS, (tile + 1) * N_EXPERTS,
                      psrc_ref, pdst_ref, plen_ref, nbits)
        _start_pieces(ys_hbm, buf_ref.at[sl], sem_ref.at[sl], tlo_ref[tile], thi_ref[tile],
                      tsrc_ref, tdst_ref, tlen_ref, nbits)

    @pl.when(i == 0)
    def _():
        buf_ref[...] = jnp.zeros(buf_ref.shape, F32)
        gather(0, 0)

    @pl.when(i + 1 < n)
    def _():
        gather(i + 1, 1 - slot)

    _wait_rows(ys_hbm, buf_ref.at[slot], sem_ref.at[slot], rows_ref[i], (tile_rows // SUBLANES).bit_length())

    dest = dest_ref[0]
    gate = gate_ref[0]
    for k in range(TOP_K):
        db_ref[k] = jnp.broadcast_to(dest[:, k:k + 1], (tt, DISPATCH_CHUNK)).astype(I16)
        gb_ref[k] = jnp.broadcast_to(gate[:, k:k + 1], (tt, DISPATCH_CHUNK)).astype(BF16)
    li = lax.broadcasted_iota(I32, (tt, DISPATCH_CHUNK), 1).astype(I16)
    gms = []
    for c in range(tile_rows // DISPATCH_CHUNK):
        lic = li + jnp.int16(c * DISPATCH_CHUNK)
        gm = jnp.zeros((tt, DISPATCH_CHUNK), BF16)
        for k in range(TOP_K):
            gm = jnp.where(lic == db_ref[k], gb_ref[k], gm)
        gms.append(gm)
    o_ref[...] = x2_ref[...] + jnp.dot(jnp.concatenate(gms, axis=1), buf_ref[slot].astype(BF16),
                                       preferred_element_type=F32)


def _combine(tabs, ys, x2, dest, gate, tt, nbits):
    n = x2.shape[0]
    nt = n // tt
    tile_rows = _tile_rows(tt)
    grid_spec = pltpu.PrefetchScalarGridSpec(
        num_scalar_prefetch=len(tabs),
        grid=(nt,),
        in_specs=[pl.BlockSpec(memory_space=pl.ANY),
                  pl.BlockSpec((tt, D_MODEL), lambda i, *_: (i, 0)),
                  pl.BlockSpec((1, tt, TOP_K), lambda i, *_: (i, 0, 0)),
                  pl.BlockSpec((1, tt, TOP_K), lambda i, *_: (i, 0, 0))],
        out_specs=pl.BlockSpec((tt, D_MODEL), lambda i, *_: (i, 0)),
        scratch_shapes=[pltpu.VMEM((2, tile_rows, D_MODEL), F32),
                        pltpu.VMEM((TOP_K, tt, DISPATCH_CHUNK), I16), pltpu.VMEM((TOP_K, tt, DISPATCH_CHUNK), BF16),
                        pltpu.SemaphoreType.DMA((2,))],
    )
    return pl.pallas_call(
        functools.partial(_combine_kernel, nbits=nbits),
        grid_spec=grid_spec,
        out_shape=jax.ShapeDtypeStruct((n, D_MODEL), F32),
        compiler_params=_cparams(1),
        name="combine",
    )(*tabs, ys, x2, dest, gate)


def _piece_tables(n8_a, off_a, rows_a, n8_b, off_b, bm, nblocks):
    nta = n8_a.shape[0]
    n8 = jnp.concatenate([n8_a, n8_b], axis=0)
    seg_off = jnp.concatenate([off_a, off_b], axis=0)
    n_tiles = n8.shape[0]
    tile_base = jnp.concatenate([jnp.arange(nta, dtype=I32) * rows_a, jnp.zeros((n_tiles - nta,), I32)])[:, None]
    tot = jnp.sum(n8, axis=0)
    pos0 = jnp.cumsum(n8, axis=0) - n8
    nblk = (tot + bm - 1) // bm
    cs = jnp.cumsum(nblk)
    bs = cs - nblk
    kblk = pos0 // bm
    len0 = jnp.minimum(n8, (kblk + 1) * bm - pos0)
    len1 = n8 - len0
    b0 = bs[None, :] + kblk
    src0 = tile_base + seg_off
    in_blk = pos0 - kblk * bm
    jj = jnp.arange(nblocks, dtype=I32)
    i32 = lambda v: v.astype(I32)

    def gmm_tabs(sl):
        em = lambda v: v[sl].T.reshape(-1)
        blk_em = em(b0)
        first = i32(jnp.sum(blk_em[None, :] < jj[:, None], axis=1))
        last = i32(jnp.sum(blk_em[None, :] <= jj[:, None], axis=1))
        hit = (blk_em[None, :] + 1 == jj[:, None]) & (em(len1)[None, :] > 0)
        tail_src = i32(jnp.sum(jnp.where(hit, em(src0 + len0)[None, :], 0), axis=1))
        tail_len = i32(jnp.sum(jnp.where(hit, em(len1)[None, :], 0), axis=1) // SUBLANES)
        return first, last, i32(em(src0)), i32(em(in_blk)), i32(em(len0) // SUBLANES), tail_src, tail_len

    def comb_tabs(sl):
        tm = lambda v: v[sl].reshape(-1)
        has_tail = len1[sl] > 0
        cnt = jnp.sum(has_tail, axis=1)
        lo = jnp.cumsum(cnt) - cnt
        slot_ = lo[:, None] + jnp.cumsum(has_tail, axis=1) - has_tail
        hit = (slot_.reshape(-1)[None, :] == jj[:, None]) & has_tail.reshape(-1)[None, :]
        pick = lambda v: i32(jnp.sum(jnp.where(hit, tm(v)[None, :], 0), axis=1))
        return (i32(tm(b0 * bm + in_blk)), i32(tm(seg_off)), i32(tm(len0) // SUBLANES),
                i32(lo), i32(lo + cnt), pick((b0 + 1) * bm), pick(seg_off + len0), pick(len1 // SUBLANES),
                i32(jnp.sum(n8[sl], axis=1) // SUBLANES))

    count_le = lambda v: jnp.sum(cs[None, :] <= v[:, None], axis=1)
    n_active = cs[-1]
    e_last = count_le(jnp.maximum(n_active - 1, 0).reshape(1))[0]
    block_e = jnp.minimum(count_le(jj), e_last).astype(I32)
    ee = jnp.arange(N_EXPERTS, dtype=I32)
    mine = (jj[:, None] >= bs[None, :]) & (jj[:, None] < cs[None, :])
    left = jnp.clip(tot[None, :] - (jj[:, None] - bs[None, :]) * bm, 0, bm)
    rows8 = (jnp.sum(jnp.where(mine, left, 0), axis=1) // SUBLANES).astype(I32)
    has = nblk > 0
    run = jnp.cumsum(has.astype(I32)) - 1
    later = (ee[None, :] > ee[:, None]) & has[None, :]
    nxt_e = jnp.min(jnp.where(later, ee[None, :], N_EXPERTS), axis=1)
    nxt_e = jnp.where(nxt_e == N_EXPERTS, -1, nxt_e)
    own = block_e[:, None] == ee[None, :]
    wslot = (jnp.sum(jnp.where(own, run[None, :], 0), axis=1) % 2).astype(I32)
    nxt = jnp.sum(jnp.where(own, nxt_e[None, :], 0), axis=1).astype(I32)
    a, b = slice(0, nta), slice(nta, n_tiles)
    return (block_e, rows8, wslot, nxt), gmm_tabs(a), gmm_tabs(b), comb_tabs(a), comb_tabs(b)


def _block_diag(w):
    nb, bi, bo = w.shape
    eye = jnp.eye(nb, dtype=w.dtype)
    return (eye[:, None, :, None] * w[:, :, None, :]).reshape(nb * bi, nb * bo)


def _step(x_prompt, x_sample, cache_k, cache_v, state_conv, state_h, g_mix_norm, w_in, g_q_norm, g_k_norm,
          attn_sinks, conv_w, conv_b, w_lru_a, b_lru_a, w_lru_x, b_lru_x, lru_lambda, g_attn_out, g_rnn_out,
          w_out, g_ffn_norm, w_router, b_router, w_gate_up, b_gate_up, w_down, b_down,
          *, tm, tt, tc, bm, past_len):
    B, S, D = x_prompt.shape
    NS = x_sample.shape[0]
    assert x_sample.shape[1] == 1 and D == D_MODEL
    assert (B * S) % tt == 0 and (B * S) % tm == 0 and S % tc == 0 and S % ATTN_BLOCK == 0 and S % tm == 0
    assert NS % SUBLANES == 0 and tt <= bm
    assert tt % SUBLANES == 0 and NS <= bm
    n_pt = (B * S) // tt
    total_rows = TOP_K * (B * S + NS) + (n_pt + 1) * N_EXPERTS * (SUBLANES - 1)
    nblocks = -(-total_rows // bm) + N_EXPERTS
    nbits_p = (tt // SUBLANES).bit_length()
    nbits_s = (NS // SUBLANES).bit_length()

    l = 0
    row = lambda v: v[l].reshape(1, -1)
    w_in_bf = w_in[l].astype(BF16)
    gq2 = jnp.tile(g_q_norm[l], 2).reshape(1, LANES)
    gk2 = jnp.tile(g_k_norm[l], 2).reshape(1, LANES)
    wa = _block_diag(w_lru_a[l]).astype(BF16)
    wx = _block_diag(w_lru_x[l]).astype(BF16)
    ba = b_lru_a[l].reshape(1, D_RNN)
    bx = b_lru_x[l].reshape(1, D_RNN)
    wo = w_out[l].astype(BF16)
    woa, wor = wo[:D_ATTN], wo[D_ATTN:]
    wr = w_router[l].T
    wr_hi = wr.astype(BF16)
    wrt = jnp.concatenate([wr_hi, (wr - wr_hi.astype(F32)).astype(BF16)], axis=0)
    br = b_router[l].reshape(N_EXPERTS, 1)
    low = jnp.tril(jnp.ones((N_EXPERTS, N_EXPERTS), BF16), k=-1)
    bgu = b_gate_up[l].reshape(N_EXPERTS, D_FF, 2)
    bg = bgu[:, :, 0].reshape(N_EXPERTS, 1, D_FF)
    bu = bgu[:, :, 1].reshape(N_EXPERTS, 1, D_FF)
    bd = b_down[l].reshape(N_EXPERTS, 1, D_MODEL)
    half = PERM_COLS // 2
    pr = jnp.arange(PERM_COLS)
    perm = (pr[None, :] == jnp.where(pr % 2 == 0, pr // 2, half + pr // 2)[:, None]).astype(BF16)
    sinks = attn_sinks[l]

    ctab, s1tab, s2tab = _rope_tables(jnp.arange(S))
    q, k, v, xr, yr = _in_proj(x_prompt.reshape(B * S, D), row(g_mix_norm), w_in_bf, gq2, gk2,
                               ctab, s1tab, s2tab, tm)
    an, kt_p, vt_p = _attn_prompt(q, k, v, sinks, row(g_attn_out), B, S)
    rn, h_last_p = _rnn_prompt(xr, yr, conv_w[l], row(conv_b), wa, ba, wx, bx, row(lru_lambda),
                               row(g_rnn_out), B, S, tc)
    x2_p, ts_p, dest_p, gate_p, n8_p, off_p = _mix_route(
        x_prompt.reshape(B * S, D), an, rn, woa, wor, row(g_ffn_norm), wrt, br, low, tt)

    cs_tab = _rope_tables(jnp.full((NS,), past_len, I32))
    q_s, k_s, v_s, xr_s, yr_s = _in_proj(x_sample.reshape(NS, D), row(g_mix_norm), w_in_bf, gq2, gk2,
                                         *cs_tab, NS)
    to_rows = lambda c: jnp.transpose(c, (0, 2, 3, 1)).reshape(NS * KV_W, WINDOW)
    from_rows = lambda c, n: jnp.transpose(c.reshape(n, N_KV_HEADS, HEAD_DIM, WINDOW), (0, 3, 1, 2))[None]
    an_s, kt_s, vt_s = _attn_sample(q_s, k_s, v_s, to_rows(cache_k[l]), to_rows(cache_v[l]), sinks,
                                    row(g_attn_out), SUBLANES)
    rn_s, h_last_s, hist_s = _rnn_sample(xr_s, yr_s, jnp.transpose(state_conv[l], (1, 0, 2)), state_h[l],
                                         conv_w[l], row(conv_b), wa, ba, wx, bx, row(lru_lambda),
                                         row(g_rnn_out))
    x2_s, ts_s, dest_s, gate_s, n8_s, off_s = _mix_route(
        x_sample.reshape(NS, D), an_s, rn_s, woa, wor, row(g_ffn_norm), wrt, br, low, NS)

    blocks, gmm_p, gmm_s, comb_p, comb_s = _piece_tables(
        n8_p[:, :, 0], off_p[:, :, 0], _tile_rows(tt), n8_s[:, :, 0], off_s[:, :, 0], bm, nblocks)
    ys = _moe_gmm(blocks, gmm_p, gmm_s, ts_p, ts_s, w_gate_up[l], w_down[l], bg, bu, bd, perm,
                  nblocks, bm, nbits_p, nbits_s)
    tr = lambda a: jnp.transpose(a, (0, 2, 1))
    y_p = _combine(comb_p, ys, x2_p, tr(dest_p), tr(gate_p), tt, nbits_p)
    y_s = _combine(comb_s, ys, x2_s, tr(dest_s), tr(gate_s), NS, nbits_s)

    cp = xr.reshape(B, S, D_RNN)[:, S - (CONV_WIDTH - 1):]
    return (y_p.reshape(B, S, D), y_s.reshape(NS, 1, D),
            from_rows(kt_p, B), from_rows(vt_p, B), cp[None], h_last_p.reshape(1, B, D_RNN),
            from_rows(kt_s, NS), from_rows(vt_s, NS), jnp.transpose(hist_s, (1, 0, 2))[None], h_last_s[None])


def kernel(x_prompt, x_sample, cache_k, cache_v, state_conv, state_h, g_mix_norm, w_in, g_q_norm, g_k_norm, attn_sinks, conv_w, conv_b, w_lru_a, b_lru_a, w_lru_x, b_lru_x, lru_lambda, g_attn_out, g_rnn_out, w_out, g_ffn_norm, w_router, b_router, w_gate_up, b_gate_up, w_down, b_down):
    return _step(x_prompt, x_sample, cache_k, cache_v, state_conv, state_h, g_mix_norm, w_in, g_q_norm,
                 g_k_norm, attn_sinks, conv_w, conv_b, w_lru_a, b_lru_a, w_lru_x, b_lru_x, lru_lambda,
                 g_attn_out, g_rnn_out, w_out, g_ffn_norm, w_router, b_router, w_gate_up, b_gate_up,
                 w_down, b_down, tm=512, tt=512, tc=256, bm=MOE_BLOCK_ROWS, past_len=PAST_LEN)
```

```python
import functools

import jax
import jax.numpy as jnp
from jax import lax
from jax.experimental import pallas as pl
from jax.experimental.pallas import tpu as pltpu

F32 = jnp.float32
BF16 = jnp.bfloat16
I32 = jnp.int32
I16 = jnp.int16

D_MODEL = 1024
HEAD_DIM = 64
N_HEADS = 8
N_KV_HEADS = 2
GROUP = 4
WINDOW = 128
ATTN_BLOCK = 128
ROT_DIM = 16
ROPE_THETA = 500000.0
D_ATTN = 512
D_RNN = 512
KV_W = 128
D_IN = 1792
CONV_WIDTH = 4
LRU_C = 8.0
N_EXPERTS = 32
TOP_K = 4
D_FF = 1024
SWIGLU_LIMIT = 7.0
SWIGLU_ALPHA = 1.702
EPS = 1e-6
PAST_LEN = 8192

LANES = 128
SUBLANES = 8
NEG_BIG = -1e30
VMEM_LIMIT = 56 * 1024 * 1024

MOE_BLOCK_ROWS = 512
PERM_COLS = 256
DISPATCH_CHUNK = 256


def _cparams(n_axes):
    return pltpu.CompilerParams(dimension_semantics=("arbitrary",) * n_axes,
                                vmem_limit_bytes=VMEM_LIMIT)


def _rmsnorm(x, g):
    ms = jnp.mean(x * x, axis=-1, keepdims=True)
    return (x * lax.rsqrt(ms + EPS)) * g


def _head_norm_rope(t, g, c, s1, s2, lo):
    sq = t * t
    s_lo = jnp.sum(jnp.where(lo, sq, 0.0), axis=-1, keepdims=True)
    s_hi = jnp.sum(jnp.where(lo, 0.0, sq), axis=-1, keepdims=True)
    ms = jnp.where(lo, s_lo, s_hi) * (1.0 / HEAD_DIM)
    n = (t * lax.rsqrt(ms + EPS)) * g
    up = pltpu.roll(n, LANES - ROT_DIM // 2, 1)
    dn = pltpu.roll(n, ROT_DIM // 2, 1)
    return n * c + up * s1 + dn * s2


def _in_proj_kernel(x_ref, g_ref, w_ref, gq_ref, gk_ref, c_ref, s1_ref, s2_ref,
                    q_ref, k_ref, v_ref, xr_ref, yr_ref):
    tm = x_ref.shape[0]
    h = _rmsnorm(x_ref[...], g_ref[...])
    proj = jnp.dot(h.astype(BF16), w_ref[...], preferred_element_type=F32)
    rope = (c_ref[...], s1_ref[...], s2_ref[...], lax.broadcasted_iota(I32, (tm, LANES), 1) < HEAD_DIM)
    gq = gq_ref[...]
    for j in range(D_ATTN // LANES):
        q_ref[:, j * LANES:(j + 1) * LANES] = _head_norm_rope(proj[:, j * LANES:(j + 1) * LANES], gq, *rope)
    k_ref[...] = _head_norm_rope(proj[:, D_ATTN:D_ATTN + KV_W], gk_ref[...], *rope)
    v_ref[...] = proj[:, D_ATTN + KV_W:D_ATTN + 2 * KV_W]
    o = D_ATTN + 2 * KV_W
    xr_ref[...] = proj[:, o:o + D_RNN]
    yr_ref[...] = proj[:, o + D_RNN:o + 2 * D_RNN]


def _in_proj(x2d, g, w_bf, gq2, gk2, ctab, s1tab, s2tab, tm):
    n = x2d.shape[0]
    ntab = ctab.shape[0] // tm
    row = lambda i: (i, 0)
    fix = lambda i: (0, 0)
    tab = lambda i: (i % ntab, 0)
    out_shapes = (jax.ShapeDtypeStruct((n, D_ATTN), F32), jax.ShapeDtypeStruct((n, KV_W), F32),
                  jax.ShapeDtypeStruct((n, KV_W), F32), jax.ShapeDtypeStruct((n, D_RNN), F32),
                  jax.ShapeDtypeStruct((n, D_RNN), F32))
    return pl.pallas_call(
        _in_proj_kernel,
        grid=(n // tm,),
        in_specs=[pl.BlockSpec((tm, D_MODEL), row), pl.BlockSpec((1, D_MODEL), fix),
                  pl.BlockSpec((D_MODEL, D_IN), fix), pl.BlockSpec((1, LANES), fix),
                  pl.BlockSpec((1, LANES), fix), pl.BlockSpec((tm, LANES), tab),
                  pl.BlockSpec((tm, LANES), tab), pl.BlockSpec((tm, LANES), tab)],
        out_specs=(pl.BlockSpec((tm, D_ATTN), row), pl.BlockSpec((tm, KV_W), row),
                   pl.BlockSpec((tm, KV_W), row), pl.BlockSpec((tm, D_RNN), row),
                   pl.BlockSpec((tm, D_RNN), row)),
        out_shape=out_shapes,
        compiler_params=_cparams(1),
        name="in_proj",
    )(x2d, g, w_bf, gq2, gk2, ctab, s1tab, s2tab)


def _rope_tables(pos):
    half = ROT_DIM // 2
    inv = ROPE_THETA ** (-jnp.arange(0, ROT_DIM, 2, dtype=F32) / ROT_DIM)
    ang = pos.astype(F32)[:, None] * inv[None, :]
    cos = jnp.cos(ang)
    sin = jnp.sin(ang)
    n = pos.shape[0]
    ones = jnp.ones((n, HEAD_DIM - ROT_DIM), F32)
    zeros = jnp.zeros((n, HEAD_DIM - ROT_DIM), F32)
    zh = jnp.zeros((n, half), F32)
    c = jnp.concatenate([cos, cos, ones], axis=1)
    s1 = jnp.concatenate([-sin, zh, zeros], axis=1)
    s2 = jnp.concatenate([zh, sin, zeros], axis=1)
    two = lambda t: jnp.concatenate([t, t], axis=1)
    return two(c), two(s1), two(s2)


def _band_bias(qb):
    qi = jnp.arange(qb, dtype=I32)[:, None]
    c = jnp.arange(2 * qb, dtype=I32)[None, :]
    band = (c >= qi) & (c <= qi + qb)
    first = band & (c >= qb)
    one = jnp.where(jnp.stack([first, band]), 0.0, NEG_BIG).astype(F32)
    return jnp.concatenate([one, one], axis=2)


def _attn_prompt_kernel(sink_ref, q_ref, kc_ref, kp_ref, vc_ref, vp_ref, bias_ref, g_ref,
                        o_ref, kt_ref, vt_ref, s_ref, e_ref):
    j = pl.program_id(1)
    qb = ATTN_BLOCK
    kc = kc_ref[...]
    vc = vc_ref[...]
    k2 = jnp.concatenate([kp_ref[...], kc], axis=0)
    v2 = jnp.concatenate([vp_ref[...], vc], axis=0)
    k2r = pltpu.roll(k2, HEAD_DIM, 1)
    v2r = pltpu.roll(v2, HEAD_DIM, 1)
    lo_k = lax.broadcasted_iota(I32, (2 * qb, LANES), 1) < HEAD_DIM
    lo_q = lax.broadcasted_iota(I32, (qb, LANES), 1) < HEAD_DIM
    bias = bias_ref[0]
    nt = (((1,), (1,)), ((), ()))
    n_pairs = N_HEADS // 2
    kbd, vbd = [], []
    for kv in range(N_KV_HEADS):
        ka, kb = (k2, k2r) if kv == 0 else (k2r, k2)
        va, vb = (v2, v2r) if kv == 0 else (v2r, v2)
        kbd.append(jnp.concatenate([jnp.where(lo_k, ka, 0.0), jnp.where(lo_k, 0.0, kb)], axis=0).astype(BF16))
        vbd.append(jnp.concatenate([jnp.where(lo_k, va, 0.0), jnp.where(lo_k, 0.0, vb)], axis=0).astype(BF16))
    for pp in range(n_pairs):
        qp = (q_ref[:, pp * LANES:(pp + 1) * LANES] * (HEAD_DIM ** -0.5)).astype(BF16)
        s_ref[pp] = lax.dot_general(qp, kbd[pp // (GROUP // 2)], nt, preferred_element_type=F32) + bias
    invs = []
    for pp in range(n_pairs):
        inv = []
        for t in range(2):
            cols = slice(t * 2 * qb, (t + 1) * 2 * qb)
            st = s_ref[pp, :, cols]
            sink = sink_ref[2 * pp + t]
            m = jnp.maximum(jnp.max(st, axis=-1, keepdims=True), sink)
            e = jnp.exp(st - m)
            e_ref[pp, :, cols] = e.astype(BF16)
            inv.append(1.0 / (jnp.sum(e, axis=-1, keepdims=True) + jnp.exp(sink - m)))
        invs.append(jnp.where(lo_q, inv[0], inv[1]))
    outs = [jnp.dot(e_ref[pp], vbd[pp // (GROUP // 2)], preferred_element_type=F32) * invs[pp]
            for pp in range(n_pairs)]
    o_ref[...] = _rmsnorm(jnp.concatenate(outs, axis=1), g_ref[...])

    @pl.when(j == pl.num_programs(1) - 1)
    def _():
        kt_ref[0] = kc.T
        vt_ref[0] = vc.T


def _attn_prompt(q, k, v, sinks, g_attn, batch, seq):
    qb = ATTN_BLOCK
    nb = seq // qb
    cur = lambda b, j: (b * nb + j, 0)
    prev = lambda b, j: (b * nb + jnp.maximum(j - 1, 0), 0)
    fix = lambda b, j: (0, 0)
    per_b = lambda b, j: (b, 0, 0)
    return pl.pallas_call(
        _attn_prompt_kernel,
        grid=(batch, nb),
        in_specs=[pl.BlockSpec(memory_space=pltpu.SMEM),
                  pl.BlockSpec((qb, D_ATTN), cur),
                  pl.BlockSpec((qb, KV_W), cur), pl.BlockSpec((qb, KV_W), prev),
                  pl.BlockSpec((qb, KV_W), cur), pl.BlockSpec((qb, KV_W), prev),
                  pl.BlockSpec((1, qb, 4 * qb), lambda b, j: (jnp.minimum(j, 1), 0, 0)),
                  pl.BlockSpec((1, D_ATTN), fix)],
        out_specs=(pl.BlockSpec((qb, D_ATTN), cur), pl.BlockSpec((1, KV_W, qb), per_b),
                   pl.BlockSpec((1, KV_W, qb), per_b)),
        out_shape=(jax.ShapeDtypeStruct((batch * seq, D_ATTN), F32),
                   jax.ShapeDtypeStruct((batch, KV_W, qb), F32),
                   jax.ShapeDtypeStruct((batch, KV_W, qb), F32)),
        scratch_shapes=[pltpu.VMEM((N_HEADS // 2, qb, 4 * qb), F32),
                        pltpu.VMEM((N_HEADS // 2, qb, 4 * qb), BF16)],
        compiler_params=_cparams(2),
        name="attn_prompt",
    )(sinks, q, k, k, v, v, _band_bias(qb), g_attn)


def _attn_sample_kernel(sink_ref, q_ref, kn_ref, vn_ref, kt_ref, vt_ref, g_ref,
                        o_ref, nkt_ref, nvt_ref, acc_ref):
    bb = q_ref.shape[0]
    q = q_ref[...] * (HEAD_DIM ** -0.5)
    kn = kn_ref[...]
    vn = vn_ref[...]
    kt = kt_ref[...]
    vt = vt_ref[...]
    col = lax.broadcasted_iota(I32, (bb, bb * KV_W), 1)
    rowb = lax.broadcasted_iota(I32, (bb, bb * KV_W), 0)
    own_seq = (col >> (KV_W.bit_length() - 1)) == rowb
    half_hi = ((col >> (HEAD_DIM.bit_length() - 1)) & 1) == 1
    qbig = []
    for h in range(N_HEADS):
        kv = h // GROUP
        pair = q[:, (h // 2) * LANES:(h // 2 + 1) * LANES]
        if (h % 2) != kv:
            pair = pltpu.roll(pair, HEAD_DIM, 1)
        tiled = jnp.concatenate([pair] * bb, axis=1)
        keep = own_seq & (half_hi if kv == 1 else jnp.logical_not(half_hi))
        qbig.append(jnp.where(keep, tiled, 0.0))
    qbig = jnp.concatenate(qbig, axis=0)
    s = jnp.dot(qbig.astype(BF16), kt.astype(BF16), preferred_element_type=F32)
    qb16 = q.astype(BF16).astype(F32)
    kb16 = kn.astype(BF16).astype(F32)
    s_new, sink = [], []
    for h in range(N_HEADS):
        kv = h // GROUP
        s_new.append(jnp.sum(qb16[:, h * HEAD_DIM:(h + 1) * HEAD_DIM] * kb16[:, kv * HEAD_DIM:(kv + 1) * HEAD_DIM],
                             axis=-1, keepdims=True))
        sink.append(jnp.full((bb, 1), sink_ref[h], F32))
    s_new = jnp.concatenate(s_new, axis=0)
    sink = jnp.concatenate(sink, axis=0)
    m = jnp.maximum(jnp.maximum(jnp.max(s, axis=-1, keepdims=True), s_new), sink)
    e = jnp.exp(s - m)
    e_new = jnp.exp(s_new - m)
    inv = 1.0 / (jnp.sum(e, axis=-1, keepdims=True) + e_new + jnp.exp(sink - m))
    obig = lax.dot_general(e.astype(BF16), vt.astype(BF16), (((1,), (1,)), ((), ())),
                           preferred_element_type=F32)
    for h in range(N_HEADS):
        kv = h // GROUP
        blk = jnp.where(own_seq, obig[h * bb:(h + 1) * bb, :], 0.0)
        fold = blk[:, 0:KV_W]
        for t in range(1, bb):
            fold = fold + blk[:, t * KV_W:(t + 1) * KV_W]
        hs = slice(h * bb, (h + 1) * bb)
        ks = slice(kv * HEAD_DIM, (kv + 1) * HEAD_DIM)
        acc_ref[:, h * HEAD_DIM:(h + 1) * HEAD_DIM] = (fold[:, ks] + e_new[hs] * vn[:, ks]) * inv[hs]
    o_ref[...] = _rmsnorm(acc_ref[...], g_ref[...])

    last = lax.broadcasted_iota(I32, (KV_W, WINDOW), 1) == WINDOW - 1
    for b in range(bb):
        rs = slice(b * KV_W, (b + 1) * KV_W)
        kcol = jnp.broadcast_to(kn[b:b + 1, :], (KV_W, KV_W)).T
        vcol = jnp.broadcast_to(vn[b:b + 1, :], (KV_W, KV_W)).T
        nkt_ref[rs, :] = jnp.where(last, kcol, pltpu.roll(kt[rs, :], WINDOW - 1, 1))
        nvt_ref[rs, :] = jnp.where(last, vcol, pltpu.roll(vt[rs, :], WINDOW - 1, 1))


def _attn_sample(q, kn, vn, kt2d, vt2d, sinks, g_attn, bb):
    n = q.shape[0]
    row = lambda i: (i, 0)
    fix = lambda i: (0, 0)
    cache = pl.BlockSpec((bb * KV_W, WINDOW), row)
    return pl.pallas_call(
        _attn_sample_kernel,
        grid=(n // bb,),
        in_specs=[pl.BlockSpec(memory_space=pltpu.SMEM),
                  pl.BlockSpec((bb, D_ATTN), row), pl.BlockSpec((bb, KV_W), row),
                  pl.BlockSpec((bb, KV_W), row), cache, cache,
                  pl.BlockSpec((1, D_ATTN), fix)],
        out_specs=(pl.BlockSpec((bb, D_ATTN), row), cache, cache),
        out_shape=(jax.ShapeDtypeStruct((n, D_ATTN), F32),
                   jax.ShapeDtypeStruct(kt2d.shape, F32), jax.ShapeDtypeStruct(vt2d.shape, F32)),
        scratch_shapes=[pltpu.VMEM((bb, D_ATTN), F32)],
        compiler_params=_cparams(1),
        name="attn_sample",
    )(sinks, q, kn, vn, kt2d, vt2d, g_attn)


def _softplus(z):
    return jnp.maximum(z, 0.0) + jnp.log1p(jnp.exp(-jnp.abs(z)))


def _lru_gates(xc, wa_ref, ba_ref, wx_ref, bx_ref, lam_ref):
    xb = xc.astype(BF16)
    r = jax.nn.sigmoid(jnp.dot(xb, wa_ref[...], preferred_element_type=F32) + ba_ref[...])
    i = jax.nn.sigmoid(jnp.dot(xb, wx_ref[...], preferred_element_type=F32) + bx_ref[...])
    log_a = (-LRU_C * r) * _softplus(-lam_ref[...])
    a = jnp.exp(log_a)
    u = jnp.sqrt(-jnp.tanh(log_a) * (a * a + 1.0)) * (i * xc)
    return a, u


def _lru_scan(a, u, h0):
    ng = a.shape[0] // SUBLANES
    a3 = a.reshape(ng, SUBLANES, D_RNN)
    u3 = u.reshape(ng, SUBLANES, D_RNN)
    t8 = lax.broadcasted_iota(I32, (ng, SUBLANES, D_RNN), 1)
    d = 1
    while d < SUBLANES:
        a_s = jnp.where(t8 >= d, pltpu.roll(a3, d, 1), 1.0)
        u_s = jnp.where(t8 >= d, pltpu.roll(u3, d, 1), 0.0)
        u3 = a3 * u_s + u3
        a3 = a3 * a_s
        d *= 2
    carry = h0
    groups = []
    for g in range(ng):
        hg = a3[g] * carry + u3[g]
        groups.append(hg)
        carry = hg[SUBLANES - 1:SUBLANES, :]
    return jnp.concatenate(groups, axis=0), carry


def _rnn_prompt_kernel(xr_ref, yr_ref, cw_ref, cb_ref, wa_ref, ba_ref, wx_ref, bx_ref, lam_ref, g_ref,
                       o_ref, hl_ref, ext_ref, h_ref):
    c = pl.program_id(1)
    tc = xr_ref.shape[0]
    pad = SUBLANES

    @pl.when(c == 0)
    def _():
        ext_ref[0:pad, :] = jnp.zeros((pad, D_RNN), F32)
        h_ref[...] = jnp.zeros((1, D_RNN), F32)

    ext_ref[pad:pad + tc, :] = xr_ref[...]
    cw = cw_ref[...]
    xc = cb_ref[...] + ext_ref[pad:pad + tc, :] * cw[CONV_WIDTH - 1:CONV_WIDTH, :]
    for w in range(CONV_WIDTH - 1):
        sh = CONV_WIDTH - 1 - w
        xc = xc + ext_ref[pad - sh:pad - sh + tc, :] * cw[w:w + 1, :]
    ext_ref[0:pad, :] = ext_ref[tc:tc + pad, :]

    a, u = _lru_gates(xc, wa_ref, ba_ref, wx_ref, bx_ref, lam_ref)
    h, carry = _lru_scan(a, u, h_ref[...])
    h_ref[...] = carry
    hl_ref[0] = carry
    o_ref[...] = _rmsnorm(jax.nn.gelu(yr_ref[...]) * h, g_ref[...])


def _rnn_prompt(xr, yr, cw, cb, wa, ba, wx, bx, lam, g, batch, seq, tc):
    nc = seq // tc
    cur = lambda b, c: (b * nc + c, 0)
    fix = lambda b, c: (0, 0)
    vec = pl.BlockSpec((1, D_RNN), fix)
    return pl.pallas_call(
        _rnn_prompt_kernel,
        grid=(batch, nc),
        in_specs=[pl.BlockSpec((tc, D_RNN), cur), pl.BlockSpec((tc, D_RNN), cur),
                  pl.BlockSpec((CONV_WIDTH, D_RNN), fix), vec,
                  pl.BlockSpec((D_RNN, D_RNN), fix), vec,
                  pl.BlockSpec((D_RNN, D_RNN), fix), vec, vec, vec],
        out_specs=(pl.BlockSpec((tc, D_RNN), cur), pl.BlockSpec((1, 1, D_RNN), lambda b, c: (b, 0, 0))),
        out_shape=(jax.ShapeDtypeStruct((batch * seq, D_RNN), F32),
                   jax.ShapeDtypeStruct((batch, 1, D_RNN), F32)),
        scratch_shapes=[pltpu.VMEM((tc + SUBLANES, D_RNN), F32), pltpu.VMEM((1, D_RNN), F32)],
        compiler_params=_cparams(2),
        name="rnn_prompt",
    )(xr, yr, cw, cb, wa, ba, wx, bx, lam, g)


def _rnn_sample_kernel(xr_ref, yr_ref, hist_ref, h0_ref, cw_ref, cb_ref, wa_ref, ba_ref, wx_ref, bx_ref,
                       lam_ref, g_ref, o_ref, hl_ref, nh_ref):
    cw = cw_ref[...]
    xr = xr_ref[...]
    xc = cb_ref[...] + xr * cw[CONV_WIDTH - 1:CONV_WIDTH, :]
    for w in range(CONV_WIDTH - 1):
        xc = xc + hist_ref[w] * cw[w:w + 1, :]
    a, u = _lru_gates(xc, wa_ref, ba_ref, wx_ref, bx_ref, lam_ref)
    h = a * h0_ref[...] + u
    hl_ref[...] = h
    o_ref[...] = _rmsnorm(jax.nn.gelu(yr_ref[...]) * h, g_ref[...])
    for w in range(CONV_WIDTH - 2):
        nh_ref[w] = hist_ref[w + 1]
    nh_ref[CONV_WIDTH - 2] = xr


def _rnn_sample(xr, yr, hist, h0, cw, cb, wa, ba, wx, bx, lam, g):
    n = xr.shape[0]
    full = lambda a: pl.BlockSpec(a.shape, lambda: (0,) * a.ndim)
    args = (xr, yr, hist, h0, cw, cb, wa, ba, wx, bx, lam, g)
    return pl.pallas_call(
        _rnn_sample_kernel,
        in_specs=[full(a) for a in args],
        out_specs=(pl.BlockSpec((n, D_RNN), lambda: (0, 0)), pl.BlockSpec((n, D_RNN), lambda: (0, 0)),
                   pl.BlockSpec(hist.shape, lambda: (0, 0, 0))),
        out_shape=(jax.ShapeDtypeStruct((n, D_RNN), F32), jax.ShapeDtypeStruct((n, D_RNN), F32),
                   jax.ShapeDtypeStruct(hist.shape, F32)),
        compiler_params=pltpu.CompilerParams(vmem_limit_bytes=VMEM_LIMIT),
        name="rnn_sample",
    )(*args)


def _front_kernel(sink_ref, x_ref, gm_ref, w_ref, gq_ref, gk_ref, c_ref, s1_ref, s2_ref, bias_ref, ga_ref,
                  cw_ref, cb_ref, wa_ref, ba_ref, wx_ref, bx_ref, lam_ref, gr_ref,
                  an_ref, rn_ref, kt_ref, vt_ref, hl_ref, cx_ref,
                  q_s, k_s, v_s, xr_s, yr_s, ext_ref, h_ref, s_ref, e_ref, *, tiles_per_seq):
    i = pl.program_id(0)
    tm = x_ref.shape[0]
    qb = ATTN_BLOCK
    cur = i % 2
    prv = 1 - cur
    t = jnp.maximum(i - 1, 0)
    first_tile = (t % tiles_per_seq) == 0

    @pl.when(i == 0)
    def _():
        for r in (q_s, k_s, v_s, xr_s, yr_s, ext_ref, h_ref):
            r[...] = jnp.zeros(r.shape, F32)

    h = _rmsnorm(x_ref[...], gm_ref[...])
    proj = jnp.dot(h.astype(BF16), w_ref[...], preferred_element_type=F32)
    lo = lax.broadcasted_iota(I32, (tm, LANES), 1) < HEAD_DIM
    rope = (c_ref[...], s1_ref[...], s2_ref[...], lo)
    gq = gq_ref[...]
    for j in range(D_ATTN // LANES):
        q_s[cur, :, j * LANES:(j + 1) * LANES] = _head_norm_rope(proj[:, j * LANES:(j + 1) * LANES], gq, *rope)
    k_s[cur, qb:qb + tm, :] = _head_norm_rope(proj[:, D_ATTN:D_ATTN + KV_W], gk_ref[...], *rope)
    v_s[cur, qb:qb + tm, :] = proj[:, D_ATTN + KV_W:D_ATTN + 2 * KV_W]
    k_s[cur, 0:qb, :] = k_s[prv, tm:tm + qb, :]
    v_s[cur, 0:qb, :] = v_s[prv, tm:tm + qb, :]
    o = D_ATTN + 2 * KV_W
    xr_s[cur] = proj[:, o:o + D_RNN]
    yr_s[cur] = proj[:, o + D_RNN:o + 2 * D_RNN]

    nqb = tm // qb
    n_pairs = N_HEADS // 2
    lo_k = lax.broadcasted_iota(I32, (2 * qb, LANES), 1) < HEAD_DIM
    lo_q = lax.broadcasted_iota(I32, (qb, LANES), 1) < HEAD_DIM
    nt_dims = (((1,), (1,)), ((), ()))
    vbds = []
    for jb in range(nqb):
        k2 = k_s[prv, jb * qb:(jb + 2) * qb, :]
        v2 = v_s[prv, jb * qb:(jb + 2) * qb, :]
        k2r = pltpu.roll(k2, HEAD_DIM, 1)
        v2r = pltpu.roll(v2, HEAD_DIM, 1)
        bias = bias_ref[jnp.where(first_tile, 0, 1)] if jb == 0 else bias_ref[1]
        for kv in range(N_KV_HEADS):
            ka, kb = (k2, k2r) if kv == 0 else (k2r, k2)
            va, vb = (v2, v2r) if kv == 0 else (v2r, v2)
            kbd = jnp.concatenate([jnp.where(lo_k, ka, 0.0), jnp.where(lo_k, 0.0, kb)], axis=0).astype(BF16)
            vbds.append(jnp.concatenate([jnp.where(lo_k, va, 0.0), jnp.where(lo_k, 0.0, vb)],
                                        axis=0).astype(BF16))
            for p in range(GROUP // 2):
                pp = kv * (GROUP // 2) + p
                qp = (q_s[prv, jb * qb:(jb + 1) * qb, pp * LANES:(pp + 1) * LANES]
                      * (HEAD_DIM ** -0.5)).astype(BF16)
                s_ref[jb * n_pairs + pp] = lax.dot_general(qp, kbd, nt_dims,
                                                           preferred_element_type=F32) + bias
    invs = []
    for c in range(nqb * n_pairs):
        pp = c % n_pairs
        inv = []
        for tpos in range(2):
            cols = slice(tpos * 2 * qb, (tpos + 1) * 2 * qb)
            st = s_ref[c, :, cols]
            sink = sink_ref[2 * pp + tpos]
            m = jnp.maximum(jnp.max(st, axis=-1, keepdims=True), sink)
            e = jnp.exp(st - m)
            e_ref[c, :, cols] = e.astype(BF16)
            inv.append(1.0 / (jnp.sum(e, axis=-1, keepdims=True) + jnp.exp(sink - m)))
        invs.append(jnp.where(lo_q, inv[0], inv[1]))
    for jb in range(nqb):
        outs = [jnp.dot(e_ref[jb * n_pairs + pp], vbds[jb * N_KV_HEADS + pp // (GROUP // 2)],
                        preferred_element_type=F32) * invs[jb * n_pairs + pp] for pp in range(n_pairs)]
        an_ref[jb * qb:(jb + 1) * qb, :] = _rmsnorm(jnp.concatenate(outs, axis=1), ga_ref[...])
    kt_ref[0] = k_s[prv, tm:tm + qb, :].T
    vt_ref[0] = v_s[prv, tm:tm + qb, :].T

    pad = SUBLANES
    xr = xr_s[prv]
    ext_ref[0:pad, :] = jnp.where(first_tile, 0.0, ext_ref[0:pad, :])
    ext_ref[pad:pad + tm, :] = xr
    cw = cw_ref[...]
    xc = cb_ref[...] + xr * cw[CONV_WIDTH - 1:CONV_WIDTH, :]
    for w in range(CONV_WIDTH - 1):
        sh = CONV_WIDTH - 1 - w
        xc = xc + ext_ref[pad - sh:pad - sh + tm, :] * cw[w:w + 1, :]
    ext_ref[0:pad, :] = xr[tm - pad:tm, :]
    cx_ref[0] = xr[tm - pad:tm, :]
    a, u = _lru_gates(xc, wa_ref, ba_ref, wx_ref, bx_ref, lam_ref)
    hseq, carry = _lru_scan(a, u, jnp.where(first_tile, 0.0, h_ref[...]))
    h_ref[...] = carry
    hl_ref[0] = carry
    rn_ref[...] = _rmsnorm(jax.nn.gelu(yr_s[prv]) * hseq, gr_ref[...])


def _front(x2d, sinks, g_mix, w_bf, gq2, gk2, ctab, s1tab, s2tab, g_attn,
           cw, cb, wa, ba, wx, bx, lam, g_rnn, batch, seq, tm):
    n = x2d.shape[0]
    nt = n // tm
    tps = seq // tm
    qb = ATTN_BLOCK
    cur = lambda i: (jnp.minimum(i, nt - 1), 0)
    tab = lambda i: (jnp.minimum(i, nt - 1) % tps, 0)
    fix = lambda i: (0, 0)
    prev = lambda i: (jnp.maximum(i - 1, 0), 0)
    per_seq = lambda i: (jnp.maximum(i - 1, 0) // tps, 0, 0)
    vec = lambda w: pl.BlockSpec((1, w), fix)
    return pl.pallas_call(
        functools.partial(_front_kernel, tiles_per_seq=tps),
        grid=(nt + 1,),
        in_specs=[pl.BlockSpec(memory_space=pltpu.SMEM),
                  pl.BlockSpec((tm, D_MODEL), cur), vec(D_MODEL), pl.BlockSpec((D_MODEL, D_IN), fix),
                  vec(LANES), vec(LANES),
                  pl.BlockSpec((tm, LANES), tab), pl.BlockSpec((tm, LANES), tab), pl.BlockSpec((tm, LANES), tab),
                  pl.BlockSpec((2, qb, 4 * qb), lambda i: (0, 0, 0)), vec(D_ATTN),
                  pl.BlockSpec((CONV_WIDTH, D_RNN), fix), vec(D_RNN),
                  pl.BlockSpec((D_RNN, D_RNN), fix), vec(D_RNN),
                  pl.BlockSpec((D_RNN, D_RNN), fix), vec(D_RNN), vec(D_RNN), vec(D_RNN)],
        out_specs=(pl.BlockSpec((tm, D_ATTN), prev), pl.BlockSpec((tm, D_RNN), prev),
                   pl.BlockSpec((1, KV_W, qb), per_seq), pl.BlockSpec((1, KV_W, qb), per_seq),
                   pl.BlockSpec((1, 1, D_RNN), per_seq), pl.BlockSpec((1, SUBLANES, D_RNN), per_seq)),
        out_shape=(jax.ShapeDtypeStruct((n, D_ATTN), F32), jax.ShapeDtypeStruct((n, D_RNN), F32),
                   jax.ShapeDtypeStruct((batch, KV_W, qb), F32), jax.ShapeDtypeStruct((batch, KV_W, qb), F32),
                   jax.ShapeDtypeStruct((batch, 1, D_RNN), F32),
                   jax.ShapeDtypeStruct((batch, SUBLANES, D_RNN), F32)),
        scratch_shapes=[pltpu.VMEM((2, tm, D_ATTN), F32),
                        pltpu.VMEM((2, tm + qb, KV_W), F32), pltpu.VMEM((2, tm + qb, KV_W), F32),
                        pltpu.VMEM((2, tm, D_RNN), F32), pltpu.VMEM((2, tm, D_RNN), F32),
                        pltpu.VMEM((tm + SUBLANES, D_RNN), F32), pltpu.VMEM((1, D_RNN), F32),
                        pltpu.VMEM((tm // qb * (N_HEADS // 2), qb, 4 * qb), F32),
                        pltpu.VMEM((tm // qb * (N_HEADS // 2), qb, 4 * qb), BF16)],
        compiler_params=_cparams(1),
        name="front",
    )(sinks, x2d, g_mix, w_bf, gq2, gk2, ctab, s1tab, s2tab, _band_bias(qb), g_attn,
      cw, cb, wa, ba, wx, bx, lam, g_rnn)


def _mix_route_kernel(x_ref, an_ref, rn_ref, woa_ref, wor_ref, g_ref, wr2_ref, br_ref, tri_ref, low_ref,
                      x2_ref, ts_ref, dest_ref, gate_ref, n8_ref, off_ref):
    tt = x_ref.shape[0]
    tile_rows = ts_ref.shape[0]
    x2 = x_ref[...] + jnp.dot(an_ref[...].astype(BF16), woa_ref[...], preferred_element_type=F32) \
        + jnp.dot(rn_ref[...].astype(BF16), wor_ref[...], preferred_element_type=F32)
    x2_ref[...] = x2
    hn = _rmsnorm(x2, g_ref[...])

    nt = (((1,), (1,)), ((), ()))
    hb = hn.astype(BF16)
    hmid = (hn - hb.astype(F32)).astype(BF16)
    wr2 = wr2_ref[...]
    both = lax.dot_general(wr2, hb, nt, preferred_element_type=F32)
    logits = (lax.dot_general(wr2[:N_EXPERTS], hmid, nt, preferred_element_type=F32)
              + both[N_EXPERTS:]) + both[:N_EXPERTS] + br_ref[...]

    ie = lax.broadcasted_iota(I32, (N_EXPERTS, tt), 0).astype(F32)
    l = logits
    vals, sels = [], []
    for _ in range(TOP_K):
        m = jnp.max(l, axis=0, keepdims=True)
        idx = jnp.min(jnp.where(l == m, ie, float(N_EXPERTS)), axis=0, keepdims=True)
        sel = ie == idx
        vals.append(m)
        sels.append(sel)
        l = jnp.where(sel, NEG_BIG, l)
    es = [jnp.exp(v - vals[0]) for v in vals]
    den = es[0] + es[1] + es[2] + es[3]
    gate_ref[0] = jnp.concatenate([e / den for e in es], axis=0)

    oh = jnp.zeros((N_EXPERTS, tt), F32)
    for sel in sels:
        oh = oh + jnp.where(sel, 1.0, 0.0)
    before = jnp.dot(oh.astype(BF16), tri_ref[...], preferred_element_type=F32)
    cnt = jnp.sum(oh, axis=1, keepdims=True).astype(I32)
    n8 = ((cnt + (SUBLANES - 1)) >> 3) << 3
    n8b = jnp.broadcast_to(n8, (N_EXPERTS, LANES))
    off = jnp.dot(low_ref[...], n8b.astype(F32).astype(BF16), preferred_element_type=F32)
    n8_ref[0] = n8b
    off_ref[0] = off.astype(I32)
    base = off[:, 0:1] + before
    dests = [jnp.sum(jnp.where(sel, base, 0.0), axis=0, keepdims=True).astype(I32) for sel in sels]
    dest_ref[0] = jnp.concatenate(dests, axis=0)

    ri = lax.broadcasted_iota(I32, (DISPATCH_CHUNK, tt), 0).astype(I16)
    d16 = [d.astype(I16) for d in dests]
    one = jnp.ones((DISPATCH_CHUNK, tt), BF16)
    for c in range(tile_rows // DISPATCH_CHUNK):
        p = jnp.zeros((DISPATCH_CHUNK, tt), BF16)
        for d in d16:
            p = jnp.where(ri == d - jnp.int16(c * DISPATCH_CHUNK), one, p)
        ts_ref[c * DISPATCH_CHUNK:(c + 1) * DISPATCH_CHUNK, :] = jnp.dot(
            p, hb, preferred_element_type=F32)


def _tile_rows(tt):
    return -(-(TOP_K * tt + N_EXPERTS * (SUBLANES - 1)) // DISPATCH_CHUNK) * DISPATCH_CHUNK


def _mix_route(x2d, an, rn, woa, wor, g, wrt, br, low, tt):
    n = x2d.shape[0]
    nt = n // tt
    tile_rows = _tile_rows(tt)
    tri = jnp.triu(jnp.ones((tt, tt), BF16), k=1)
    row = lambda i: (i, 0)
    fix = lambda i: (0, 0)
    t3 = lambda i: (i, 0, 0)
    in_specs = [pl.BlockSpec((tt, D_MODEL), row), pl.BlockSpec((tt, D_ATTN), row),
                pl.BlockSpec((tt, D_RNN), row), pl.BlockSpec((D_ATTN, D_MODEL), fix),
                pl.BlockSpec((D_RNN, D_MODEL), fix), pl.BlockSpec((1, D_MODEL), fix),
                pl.BlockSpec((2 * N_EXPERTS, D_MODEL), fix), pl.BlockSpec((N_EXPERTS, 1), fix),
                pl.BlockSpec((tt, tt), fix), pl.BlockSpec((N_EXPERTS, N_EXPERTS), fix)]
    out_shape = (jax.ShapeDtypeStruct((n, D_MODEL), F32),
                 jax.ShapeDtypeStruct((nt * tile_rows, D_MODEL), F32),
                 jax.ShapeDtypeStruct((nt, TOP_K, tt), I32),
                 jax.ShapeDtypeStruct((nt, TOP_K, tt), F32),
                 jax.ShapeDtypeStruct((nt, N_EXPERTS, LANES), I32),
                 jax.ShapeDtypeStruct((nt, N_EXPERTS, LANES), I32))
    out_specs = (pl.BlockSpec((tt, D_MODEL), row),
                 pl.BlockSpec((tile_rows, D_MODEL), row),
                 pl.BlockSpec((1, TOP_K, tt), t3), pl.BlockSpec((1, TOP_K, tt), t3),
                 pl.BlockSpec((1, N_EXPERTS, LANES), t3), pl.BlockSpec((1, N_EXPERTS, LANES), t3))
    return pl.pallas_call(
        _mix_route_kernel,
        grid=(nt,),
        in_specs=in_specs,
        out_specs=out_specs,
        out_shape=out_shape,
        compiler_params=_cparams(1),
        name="mix_route",
    )(x2d, an, rn, woa, wor, g, wrt, br, tri, low)


LOW_BITS = 4


def _start_piece(src_hbm, dst_buf, sem, s, d, l8, nbits):
    def bit_copy(c):
        size = SUBLANES << c
        low = (l8 & ((1 << c) - 1)) * SUBLANES

        @pl.when(((l8 >> c) & 1) == 1)
        def _():
            pltpu.make_async_copy(
                src_hbm.at[pl.ds(pl.multiple_of(s + low, SUBLANES), size)],
                dst_buf.at[pl.ds(pl.multiple_of(d + low, SUBLANES), size)], sem).start()

    for c in range(min(LOW_BITS, nbits)):
        bit_copy(c)
    if nbits > LOW_BITS:
        @pl.when(l8 >= (1 << LOW_BITS))
        def _():
            for c in range(LOW_BITS, nbits):
                bit_copy(c)


def _start_pieces(src_hbm, dst_buf, sem, p_lo, p_hi, psrc_ref, pdst_ref, plen_ref, nbits):
    def body(p, carry):
        _start_piece(src_hbm, dst_buf, sem, psrc_ref[p], pdst_ref[p], plen_ref[p], nbits)
        return carry

    lax.fori_loop(p_lo, p_hi, body, 0)


def _wait_rows(src_hbm, dst_buf, sem, rows8, nbits):
    for c in range(nbits):
        size = SUBLANES << c

        @pl.when(((rows8 >> c) & 1) == 1)
        def _():
            pltpu.make_async_copy(src_hbm.at[pl.ds(0, size)], dst_buf.at[pl.ds(0, size)], sem).wait()


def _moe_gmm_kernel(be_ref, rows_ref, wslot_ref, nxt_ref,
                    psa_ref, pea_ref, srca_ref, dsta_ref, lena_ref, hsrca_ref, hlena_ref,
                    psb_ref, peb_ref, srcb_ref, dstb_ref, lenb_ref, hsrcb_ref, hlenb_ref,
                    tsa_hbm, tsb_hbm, wgu_hbm, wdn_hbm, bg_ref, bu_ref, bd_ref, perm_ref,
                    ys_ref, lhs_ref, wgu_buf, wdn_buf, wg_ref, wu_ref, wd_ref, sem_ref, wsem_ref,
                    *, nbits_a, nbits_b):
    j = pl.program_id(0)
    nb = pl.num_programs(0)
    slot = j % 2
    bm = lhs_ref.shape[1]

    def gather(blk, sl):
        for ts_hbm, ps, pe, src, dst, ln, hsrc, hlen, nbits in (
                (tsa_hbm, psa_ref, pea_ref, srca_ref, dsta_ref, lena_ref, hsrca_ref, hlena_ref, nbits_a),
                (tsb_hbm, psb_ref, peb_ref, srcb_ref, dstb_ref, lenb_ref, hsrcb_ref, hlenb_ref, nbits_b)):
            _start_pieces(ts_hbm, lhs_ref.at[sl], sem_ref.at[sl], ps[blk], pe[blk], src, dst, ln, nbits)
            _start_piece(ts_hbm, lhs_ref.at[sl], sem_ref.at[sl], hsrc[blk], 0, hlen[blk], nbits)

    def weight_copies(e, ws):
        return (pltpu.make_async_copy(wgu_hbm.at[e], wgu_buf.at[ws], wsem_ref.at[ws]),
                pltpu.make_async_copy(wdn_hbm.at[e], wdn_buf.at[ws], wsem_ref.at[ws]))

    @pl.when(j == 0)
    def _():
        lhs_ref[...] = jnp.zeros(lhs_ref.shape, F32)
        gather(0, 0)
        for cp in weight_copies(be_ref[0], wslot_ref[0]):
            cp.start()

    @pl.when(j + 1 < nb)
    def _():
        gather(j + 1, 1 - slot)

    @pl.when(jnp.logical_or(j == 0, be_ref[j] != be_ref[jnp.maximum(j - 1, 0)]))
    def _():
        ws = wslot_ref[j]
        for cp in weight_copies(be_ref[j], ws):
            cp.wait()
        nxt = nxt_ref[j]

        @pl.when(nxt >= 0)
        def _():
            for cp in weight_copies(nxt, 1 - ws):
                cp.start()

        perm = perm_ref[...]
        half = PERM_COLS // 2
        for c in range(2 * D_FF // PERM_COLS):
            wb = wgu_buf[ws, :, c * PERM_COLS:(c + 1) * PERM_COLS].astype(BF16)
            wp = jnp.dot(wb, perm, preferred_element_type=F32).astype(BF16)
            wg_ref[:, c * half:(c + 1) * half] = wp[:, :half]
            wu_ref[:, c * half:(c + 1) * half] = wp[:, half:]
        wd_ref[...] = wdn_buf[ws].astype(BF16)

    rows8 = rows_ref[j]
    _wait_rows(tsa_hbm, lhs_ref.at[slot], sem_ref.at[slot], rows8, (bm // SUBLANES).bit_length())

    @pl.when(rows8 > 0)
    def _():
        x = lhs_ref[slot].astype(BF16)
        gate = jnp.dot(x, wg_ref[...], preferred_element_type=F32) + bg_ref[0]
        up = jnp.dot(x, wu_ref[...], preferred_element_type=F32) + bu_ref[0]
        gate = jnp.minimum(gate, SWIGLU_LIMIT)
        up = jnp.clip(up, -SWIGLU_LIMIT, SWIGLU_LIMIT)
        act = (up + 1.0) * (gate * jax.nn.sigmoid(SWIGLU_ALPHA * gate))
        ys_ref[...] = jnp.dot(act.astype(BF16), wd_ref[...], preferred_element_type=F32) + bd_ref[0]

    @pl.when(rows8 == 0)
    def _():
        ys_ref[...] = jnp.zeros(ys_ref.shape, F32)


def _moe_gmm(blocks, tabs_a, tabs_b, ts_a, ts_b, w_gu, w_dn, bg, bu, bd, perm, nblocks, bm, nbits_a, nbits_b):
    we = lambda j, be, *_: (be[j], 0, 0)
    grid_spec = pltpu.PrefetchScalarGridSpec(
        num_scalar_prefetch=18,
        grid=(nblocks,),
        in_specs=[pl.BlockSpec(memory_space=pl.ANY), pl.BlockSpec(memory_space=pl.ANY),
                  pl.BlockSpec(memory_space=pl.ANY), pl.BlockSpec(memory_space=pl.ANY),
                  pl.BlockSpec((1, 1, D_FF), we), pl.BlockSpec((1, 1, D_FF), we),
                  pl.BlockSpec((1, 1, D_MODEL), we),
                  pl.BlockSpec((PERM_COLS, PERM_COLS), lambda j, *_: (0, 0))],
        out_specs=pl.BlockSpec((bm, D_MODEL), lambda j, *_: (j, 0)),
        scratch_shapes=[pltpu.VMEM((2, bm, D_MODEL), F32),
                        pltpu.VMEM((2, D_MODEL, 2 * D_FF), F32), pltpu.VMEM((2, D_FF, D_MODEL), F32),
                        pltpu.VMEM((D_MODEL, D_FF), BF16), pltpu.VMEM((D_MODEL, D_FF), BF16),
                        pltpu.VMEM((D_FF, D_MODEL), BF16),
                        pltpu.SemaphoreType.DMA((2,)), pltpu.SemaphoreType.DMA((2,))],
    )
    return pl.pallas_call(
        functools.partial(_moe_gmm_kernel, nbits_a=nbits_a, nbits_b=nbits_b),
        grid_spec=grid_spec,
        out_shape=jax.ShapeDtypeStruct((nblocks * bm, D_MODEL), F32),
        compiler_params=_cparams(1),
        name="moe_gmm",
    )(*blocks, *tabs_a, *tabs_b, ts_a, ts_b, w_gu, w_dn, bg, bu, bd, perm)


def _combine_kernel(psrc_ref, pdst_ref, plen_ref, tlo_ref, thi_ref, tsrc_ref, tdst_ref, tlen_ref, rows_ref,
                    ys_hbm, x2_ref, dest_ref, gate_ref, o_ref,
                    buf_ref, db_ref, gb_ref, sem_ref, *, nbits):
    i = pl.program_id(0)
    n = pl.num_programs(0)
    slot = i % 2
    tt = x2_ref.shape[0]
    tile_rows = buf_ref.shape[1]

    def gather(tile, sl):
        _start_pieces(ys_hbm, buf_ref.at[sl], sem_ref.at[sl], tile * N_EXPERTS, (tile + 1) * N_EXPERTS,
                      psrc_ref, pdst_ref, plen_ref, nbits)
        _start_pieces(ys_hbm, buf_ref.at[sl], sem_ref.at[sl], tlo_ref[tile], thi_ref[tile],
                      tsrc_ref, tdst_ref, tlen_ref, nbits)

    @pl.when(i == 0)
    def _():
        buf_ref[...] = jnp.zeros(buf_ref.shape, F32)
        gather(0, 0)

    @pl.when(i + 1 < n)
    def _():
        gather(i + 1, 1 - slot)

    _wait_rows(ys_hbm, buf_ref.at[slot], sem_ref.at[slot], rows_ref[i], (tile_rows // SUBLANES).bit_length())

    dest = dest_ref[0]
    gate = gate_ref[0]
    for k in range(TOP_K):
        db_ref[k] = jnp.broadcast_to(dest[:, k:k + 1], (tt, DISPATCH_CHUNK)).astype(I16)
        gb_ref[k] = jnp.broadcast_to(gate[:, k:k + 1], (tt, DISPATCH_CHUNK)).astype(BF16)
    li = lax.broadcasted_iota(I32, (tt, DISPATCH_CHUNK), 1).astype(I16)
    gms = []
    for c in range(tile_rows // DISPATCH_CHUNK):
        lic = li + jnp.int16(c * DISPATCH_CHUNK)
        gm = jnp.zeros((tt, DISPATCH_CHUNK), BF16)
        for k in range(TOP_K):
            gm = jnp.where(lic == db_ref[k], gb_ref[k], gm)
        gms.append(gm)
    o_ref[...] = x2_ref[...] + jnp.dot(jnp.concatenate(gms, axis=1), buf_ref[slot].astype(BF16),
                                       preferred_element_type=F32)


def _combine(tabs, ys, x2, dest, gate, tt, nbits):
    n = x2.shape[0]
    nt = n // tt
    tile_rows = _tile_rows(tt)
    grid_spec = pltpu.PrefetchScalarGridSpec(
        num_scalar_prefetch=len(tabs),
        grid=(nt,),
        in_specs=[pl.BlockSpec(memory_space=pl.ANY),
                  pl.BlockSpec((tt, D_MODEL), lambda i, *_: (i, 0)),
                  pl.BlockSpec((1, tt, TOP_K), lambda i, *_: (i, 0, 0)),
                  pl.BlockSpec((1, tt, TOP_K), lambda i, *_: (i, 0, 0))],
        out_specs=pl.BlockSpec((tt, D_MODEL), lambda i, *_: (i, 0)),
        scratch_shapes=[pltpu.VMEM((2, tile_rows, D_MODEL), F32),
                        pltpu.VMEM((TOP_K, tt, DISPATCH_CHUNK), I16), pltpu.VMEM((TOP_K, tt, DISPATCH_CHUNK), BF16),
                        pltpu.SemaphoreType.DMA((2,))],
    )
    return pl.pallas_call(
        functools.partial(_combine_kernel, nbits=nbits),
        grid_spec=grid_spec,
        out_shape=jax.ShapeDtypeStruct((n, D_MODEL), F32),
        compiler_params=_cparams(1),
        name="combine",
    )(*tabs, ys, x2, dest, gate)


def _piece_tables(n8_a, off_a, rows_a, n8_b, off_b, bm, nblocks):
    nta = n8_a.shape[0]
    n8 = jnp.concatenate([n8_a, n8_b], axis=0)
    seg_off = jnp.concatenate([off_a, off_b], axis=0)
    n_tiles = n8.shape[0]
    tile_base = jnp.concatenate([jnp.arange(nta, dtype=I32) * rows_a, jnp.zeros((n_tiles - nta,), I32)])[:, None]
    tot = jnp.sum(n8, axis=0)
    pos0 = jnp.cumsum(n8, axis=0) - n8
    nblk = (tot + bm - 1) // bm
    cs = jnp.cumsum(nblk)
    bs = cs - nblk
    kblk = pos0 // bm
    len0 = jnp.minimum(n8, (kblk + 1) * bm - pos0)
    len1 = n8 - len0
    b0 = bs[None, :] + kblk
    src0 = tile_base + seg_off
    in_blk = pos0 - kblk * bm
    jj = jnp.arange(nblocks, dtype=I32)
    i32 = lambda v: v.astype(I32)

    def gmm_tabs(sl):
        em = lambda v: v[sl].T.reshape(-1)
        blk_em = em(b0)
        first = i32(jnp.sum(blk_em[None, :] < jj[:, None], axis=1))
        last = i32(jnp.sum(blk_em[None, :] <= jj[:, None], axis=1))
        hit = (blk_em[None, :] + 1 == jj[:, None]) & (em(len1)[None, :] > 0)
        tail_src = i32(jnp.sum(jnp.where(hit, em(src0 + len0)[None, :], 0), axis=1))
        tail_len = i32(jnp.sum(jnp.where(hit, em(len1)[None, :], 0), axis=1) // SUBLANES)
        return first, last, i32(em(src0)), i32(em(in_blk)), i32(em(len0) // SUBLANES), tail_src, tail_len

    def comb_tabs(sl):
        tm = lambda v: v[sl].reshape(-1)
        has_tail = len1[sl] > 0
        cnt = jnp.sum(has_tail, axis=1)
        lo = jnp.cumsum(cnt) - cnt
        slot_ = lo[:, None] + jnp.cumsum(has_tail, axis=1) - has_tail
        hit = (slot_.reshape(-1)[None, :] == jj[:, None]) & has_tail.reshape(-1)[None, :]
        pick = lambda v: i32(jnp.sum(jnp.where(hit, tm(v)[None, :], 0), axis=1))
        return (i32(tm(b0 * bm + in_blk)), i32(tm(seg_off)), i32(tm(len0) // SUBLANES),
                i32(lo), i32(lo + cnt), pick((b0 + 1) * bm), pick(seg_off + len0), pick(len1 // SUBLANES),
                i32(jnp.sum(n8[sl], axis=1) // SUBLANES))

    count_le = lambda v: jnp.sum(cs[None, :] <= v[:, None], axis=1)
    n_active = cs[-1]
    e_last = count_le(jnp.maximum(n_active - 1, 0).reshape(1))[0]
    block_e = jnp.minimum(count_le(jj), e_last).astype(I32)
    ee = jnp.arange(N_EXPERTS, dtype=I32)
    mine = (jj[:, None] >= bs[None, :]) & (jj[:, None] < cs[None, :])
    left = jnp.clip(tot[None, :] - (jj[:, None] - bs[None, :]) * bm, 0, bm)
    rows8 = (jnp.sum(jnp.where(mine, left, 0), axis=1) // SUBLANES).astype(I32)
    has = nblk > 0
    run = jnp.cumsum(has.astype(I32)) - 1
    later = (ee[None, :] > ee[:, None]) & has[None, :]
    nxt_e = jnp.min(jnp.where(later, ee[None, :], N_EXPERTS), axis=1)
    nxt_e = jnp.where(nxt_e == N_EXPERTS, -1, nxt_e)
    own = block_e[:, None] == ee[None, :]
    wslot = (jnp.sum(jnp.where(own, run[None, :], 0), axis=1) % 2).astype(I32)
    nxt = jnp.sum(jnp.where(own, nxt_e[None, :], 0), axis=1).astype(I32)
    a, b = slice(0, nta), slice(nta, n_tiles)
    return (block_e, rows8, wslot, nxt), gmm_tabs(a), gmm_tabs(b), comb_tabs(a), comb_tabs(b)


def _block_diag(w):
    nb, bi, bo = w.shape
    eye = jnp.eye(nb, dtype=w.dtype)
    return (eye[:, None, :, None] * w[:, :, None, :]).reshape(nb * bi, nb * bo)


def _step(x_prompt, x_sample, cache_k, cache_v, state_conv, state_h, g_mix_norm, w_in, g_q_norm, g_k_norm,
          attn_sinks, conv_w, conv_b, w_lru_a, b_lru_a, w_lru_x, b_lru_x, lru_lambda, g_attn_out, g_rnn_out,
          w_out, g_ffn_norm, w_router, b_router, w_gate_up, b_gate_up, w_down, b_down,
          *, tm, tt, tc, bm, past_len):
    B, S, D = x_prompt.shape
    NS = x_sample.shape[0]
    assert x_sample.shape[1] == 1 and D == D_MODEL
    assert (B * S) % tt == 0 and (B * S) % tm == 0 and S % tc == 0 and S % ATTN_BLOCK == 0 and S % tm == 0
    assert NS % SUBLANES == 0 and tt <= bm
    assert tt % SUBLANES == 0 and NS <= bm
    n_pt = (B * S) // tt
    total_rows = TOP_K * (B * S + NS) + (n_pt + 1) * N_EXPERTS * (SUBLANES - 1)
    nblocks = -(-total_rows // bm) + N_EXPERTS
    nbits_p = (tt // SUBLANES).bit_length()
    nbits_s = (NS // SUBLANES).bit_length()

    l = 0
    row = lambda v: v[l].reshape(1, -1)
    w_in_bf = w_in[l].astype(BF16)
    gq2 = jnp.tile(g_q_norm[l], 2).reshape(1, LANES)
    gk2 = jnp.tile(g_k_norm[l], 2).reshape(1, LANES)
    wa = _block_diag(w_lru_a[l]).astype(BF16)
    wx = _block_diag(w_lru_x[l]).astype(BF16)
    ba = b_lru_a[l].reshape(1, D_RNN)
    bx = b_lru_x[l].reshape(1, D_RNN)
    wo = w_out[l].astype(BF16)
    woa, wor = wo[:D_ATTN], wo[D_ATTN:]
    wr = w_router[l].T
    wr_hi = wr.astype(BF16)
    wrt = jnp.concatenate([wr_hi, (wr - wr_hi.astype(F32)).astype(BF16)], axis=0)
    br = b_router[l].reshape(N_EXPERTS, 1)
    low = jnp.tril(jnp.ones((N_EXPERTS, N_EXPERTS), BF16), k=-1)
    bgu = b_gate_up[l].reshape(N_EXPERTS, D_FF, 2)
    bg = bgu[:, :, 0].reshape(N_EXPERTS, 1, D_FF)
    bu = bgu[:, :, 1].reshape(N_EXPERTS, 1, D_FF)
    bd = b_down[l].reshape(N_EXPERTS, 1, D_MODEL)
    half = PERM_COLS // 2
    pr = jnp.arange(PERM_COLS)
    perm = (pr[None, :] == jnp.where(pr % 2 == 0, pr // 2, half + pr // 2)[:, None]).astype(BF16)
    sinks = attn_sinks[l]

    ctab, s1tab, s2tab = _rope_tables(jnp.arange(S))
    an, rn, kt_p, vt_p, h_last_p, xr_tail = _front(
        x_prompt.reshape(B * S, D), sinks, row(g_mix_norm), w_in_bf, gq2, gk2, ctab, s1tab, s2tab,
        row(g_attn_out), conv_w[l], row(conv_b), wa, ba, wx, bx, row(lru_lambda), row(g_rnn_out), B, S, tm)
    x2_p, ts_p, dest_p, gate_p, n8_p, off_p = _mix_route(
        x_prompt.reshape(B * S, D), an, rn, woa, wor, row(g_ffn_norm), wrt, br, low, tt)

    cs_tab = _rope_tables(jnp.full((NS,), past_len, I32))
    q_s, k_s, v_s, xr_s, yr_s = _in_proj(x_sample.reshape(NS, D), row(g_mix_norm), w_in_bf, gq2, gk2,
                                         *cs_tab, NS)
    to_rows = lambda c: jnp.transpose(c, (0, 2, 3, 1)).reshape(NS * KV_W, WINDOW)
    from_rows = lambda c, n: jnp.transpose(c.reshape(n, N_KV_HEADS, HEAD_DIM, WINDOW), (0, 3, 1, 2))[None]
    an_s, kt_s, vt_s = _attn_sample(q_s, k_s, v_s, to_rows(cache_k[l]), to_rows(cache_v[l]), sinks,
                                    row(g_attn_out), SUBLANES)
    rn_s, h_last_s, hist_s = _rnn_sample(xr_s, yr_s, jnp.transpose(state_conv[l], (1, 0, 2)), state_h[l],
                                         conv_w[l], row(conv_b), wa, ba, wx, bx, row(lru_lambda),
                                         row(g_rnn_out))
    x2_s, ts_s, dest_s, gate_s, n8_s, off_s = _mix_route(
        x_sample.reshape(NS, D), an_s, rn_s, woa, wor, row(g_ffn_norm), wrt, br, low, NS)

    blocks, gmm_p, gmm_s, comb_p, comb_s = _piece_tables(
        n8_p[:, :, 0], off_p[:, :, 0], _tile_rows(tt), n8_s[:, :, 0], off_s[:, :, 0], bm, nblocks)
    ys = _moe_gmm(blocks, gmm_p, gmm_s, ts_p, ts_s, w_gate_up[l], w_down[l], bg, bu, bd, perm,
                  nblocks, bm, nbits_p, nbits_s)
    tr = lambda a: jnp.transpose(a, (0, 2, 1))
    y_p = _combine(comb_p, ys, x2_p, tr(dest_p), tr(gate_p), tt, nbits_p)
    y_s = _combine(comb_s, ys, x2_s, tr(dest_s), tr(gate_s), NS, nbits_s)

    cp = xr_tail[:, SUBLANES - (CONV_WIDTH - 1):]
    return (y_p.reshape(B, S, D), y_s.reshape(NS, 1, D),
            from_rows(kt_p, B), from_rows(vt_p, B), cp[None], h_last_p.reshape(1, B, D_RNN),
            from_rows(kt_s, NS), from_rows(vt_s, NS), jnp.transpose(hist_s, (1, 0, 2))[None], h_last_s[None])


def kernel(x_prompt, x_sample, cache_k, cache_v, state_conv, state_h, g_mix_norm, w_in, g_q_norm, g_k_norm, attn_sinks, conv_w, conv_b, w_lru_a, b_lru_a, w_lru_x, b_lru_x, lru_lambda, g_attn_out, g_rnn_out, w_out, g_ffn_norm, w_router, b_router, w_gate_up, b_gate_up, w_down, b_down):
    return _step(x_prompt, x_sample, cache_k, cache_v, state_conv, state_h, g_mix_norm, w_in, g_q_norm,
                 g_k_norm, attn_sinks, conv_w, conv_b, w_lru_a, b_lru_a, w_lru_x, b_lru_x, lru_lambda,
                 g_attn_out, g_rnn_out, w_out, g_ffn_norm, w_router, b_router, w_gate_up, b_gate_up,
                 w_down, b_down, tm=512, tt=512, tc=256, bm=MOE_BLOCK_ROWS, past_len=PAST_LEN)
```

```python
import functools

import jax
import jax.numpy as jnp
from jax import lax
from jax.experimental import pallas as pl
from jax.experimental.pallas import tpu as pltpu

F32 = jnp.float32
BF16 = jnp.bfloat16
I32 = jnp.int32
I16 = jnp.int16

D_MODEL = 1024
HEAD_DIM = 64
N_HEADS = 8
N_KV_HEADS = 2
GROUP = 4
WINDOW = 128
ATTN_BLOCK = 128
ROT_DIM = 16
ROPE_THETA = 500000.0
D_ATTN = 512
D_RNN = 512
KV_W = 128
D_IN = 1792
CONV_WIDTH = 4
LRU_C = 8.0
N_EXPERTS = 32
TOP_K = 4
D_FF = 1024
SWIGLU_LIMIT = 7.0
SWIGLU_ALPHA = 1.702
EPS = 1e-6
PAST_LEN = 8192

LANES = 128
SUBLANES = 8
NEG_BIG = -1e30
VMEM_LIMIT = 56 * 1024 * 1024

MOE_BLOCK_ROWS = 512
PERM_COLS = 256
DISPATCH_CHUNK = 256


def _cparams(n_axes):
    return pltpu.CompilerParams(dimension_semantics=("arbitrary",) * n_axes,
                                vmem_limit_bytes=VMEM_LIMIT)


def _rmsnorm(x, g):
    ms = jnp.mean(x * x, axis=-1, keepdims=True)
    return (x * lax.rsqrt(ms + EPS)) * g


def _head_norm_rope(t, g, c, s1, s2, lo):
    sq = t * t
    s_lo = jnp.sum(jnp.where(lo, sq, 0.0), axis=-1, keepdims=True)
    s_hi = jnp.sum(jnp.where(lo, 0.0, sq), axis=-1, keepdims=True)
    ms = jnp.where(lo, s_lo, s_hi) * (1.0 / HEAD_DIM)
    n = (t * lax.rsqrt(ms + EPS)) * g
    up = pltpu.roll(n, LANES - ROT_DIM // 2, 1)
    dn = pltpu.roll(n, ROT_DIM // 2, 1)
    return n * c + up * s1 + dn * s2


def _in_proj_kernel(x_ref, g_ref, w_ref, gq_ref, gk_ref, c_ref, s1_ref, s2_ref,
                    q_ref, k_ref, v_ref, xr_ref, yr_ref):
    tm = x_ref.shape[0]
    h = _rmsnorm(x_ref[...], g_ref[...])
    proj = jnp.dot(h.astype(BF16), w_ref[...], preferred_element_type=F32)
    rope = (c_ref[...], s1_ref[...], s2_ref[...], lax.broadcasted_iota(I32, (tm, LANES), 1) < HEAD_DIM)
    gq = gq_ref[...]
    for j in range(D_ATTN // LANES):
        q_ref[:, j * LANES:(j + 1) * LANES] = _head_norm_rope(proj[:, j * LANES:(j + 1) * LANES], gq, *rope)
    k_ref[...] = _head_norm_rope(proj[:, D_ATTN:D_ATTN + KV_W], gk_ref[...], *rope)
    v_ref[...] = proj[:, D_ATTN + KV_W:D_ATTN + 2 * KV_W]
    o = D_ATTN + 2 * KV_W
    xr_ref[...] = proj[:, o:o + D_RNN]
    yr_ref[...] = proj[:, o + D_RNN:o + 2 * D_RNN]


def _in_proj(x2d, g, w_bf, gq2, gk2, ctab, s1tab, s2tab, tm):
    n = x2d.shape[0]
    ntab = ctab.shape[0] // tm
    row = lambda i: (i, 0)
    fix = lambda i: (0, 0)
    tab = lambda i: (i % ntab, 0)
    out_shapes = (jax.ShapeDtypeStruct((n, D_ATTN), F32), jax.ShapeDtypeStruct((n, KV_W), F32),
                  jax.ShapeDtypeStruct((n, KV_W), F32), jax.ShapeDtypeStruct((n, D_RNN), F32),
                  jax.ShapeDtypeStruct((n, D_RNN), F32))
    return pl.pallas_call(
        _in_proj_kernel,
        grid=(n // tm,),
        in_specs=[pl.BlockSpec((tm, D_MODEL), row), pl.BlockSpec((1, D_MODEL), fix),
                  pl.BlockSpec((D_MODEL, D_IN), fix), pl.BlockSpec((1, LANES), fix),
                  pl.BlockSpec((1, LANES), fix), pl.BlockSpec((tm, LANES), tab),
                  pl.BlockSpec((tm, LANES), tab), pl.BlockSpec((tm, LANES), tab)],
        out_specs=(pl.BlockSpec((tm, D_ATTN), row), pl.BlockSpec((tm, KV_W), row),
                   pl.BlockSpec((tm, KV_W), row), pl.BlockSpec((tm, D_RNN), row),
                   pl.BlockSpec((tm, D_RNN), row)),
        out_shape=out_shapes,
        compiler_params=_cparams(1),
        name="in_proj",
    )(x2d, g, w_bf, gq2, gk2, ctab, s1tab, s2tab)


def _rope_tables(pos):
    half = ROT_DIM // 2
    inv = ROPE_THETA ** (-jnp.arange(0, ROT_DIM, 2, dtype=F32) / ROT_DIM)
    ang = pos.astype(F32)[:, None] * inv[None, :]
    cos = jnp.cos(ang)
    sin = jnp.sin(ang)
    n = pos.shape[0]
    ones = jnp.ones((n, HEAD_DIM - ROT_DIM), F32)
    zeros = jnp.zeros((n, HEAD_DIM - ROT_DIM), F32)
    zh = jnp.zeros((n, half), F32)
    c = jnp.concatenate([cos, cos, ones], axis=1)
    s1 = jnp.concatenate([-sin, zh, zeros], axis=1)
    s2 = jnp.concatenate([zh, sin, zeros], axis=1)
    two = lambda t: jnp.concatenate([t, t], axis=1)
    return two(c), two(s1), two(s2)


def _band_bias(qb):
    qi = jnp.arange(qb, dtype=I32)[:, None]
    c = jnp.arange(2 * qb, dtype=I32)[None, :]
    band = (c >= qi) & (c <= qi + qb)
    first = band & (c >= qb)
    one = jnp.where(jnp.stack([first, band]), 0.0, NEG_BIG).astype(F32)
    return jnp.concatenate([one, one], axis=2)


def _attn_prompt_kernel(sink_ref, q_ref, kc_ref, kp_ref, vc_ref, vp_ref, bias_ref, g_ref,
                        o_ref, kt_ref, vt_ref, s_ref, e_ref):
    j = pl.program_id(1)
    qb = ATTN_BLOCK
    kc = kc_ref[...]
    vc = vc_ref[...]
    k2 = jnp.concatenate([kp_ref[...], kc], axis=0)
    v2 = jnp.concatenate([vp_ref[...], vc], axis=0)
    k2r = pltpu.roll(k2, HEAD_DIM, 1)
    v2r = pltpu.roll(v2, HEAD_DIM, 1)
    lo_k = lax.broadcasted_iota(I32, (2 * qb, LANES), 1) < HEAD_DIM
    lo_q = lax.broadcasted_iota(I32, (qb, LANES), 1) < HEAD_DIM
    bias = bias_ref[0]
    nt = (((1,), (1,)), ((), ()))
    n_pairs = N_HEADS // 2
    kbd, vbd = [], []
    for kv in range(N_KV_HEADS):
        ka, kb = (k2, k2r) if kv == 0 else (k2r, k2)
        va, vb = (v2, v2r) if kv == 0 else (v2r, v2)
        kbd.append(jnp.concatenate([jnp.where(lo_k, ka, 0.0), jnp.where(lo_k, 0.0, kb)], axis=0).astype(BF16))
        vbd.append(jnp.concatenate([jnp.where(lo_k, va, 0.0), jnp.where(lo_k, 0.0, vb)], axis=0).astype(BF16))
    for pp in range(n_pairs):
        qp = (q_ref[:, pp * LANES:(pp + 1) * LANES] * (HEAD_DIM ** -0.5)).astype(BF16)
        s_ref[pp] = lax.dot_general(qp, kbd[pp // (GROUP // 2)], nt, preferred_element_type=F32) + bias
    invs = []
    for pp in range(n_pairs):
        inv = []
        for t in range(2):
            cols = slice(t * 2 * qb, (t + 1) * 2 * qb)
            st = s_ref[pp, :, cols]
            sink = sink_ref[2 * pp + t]
            m = jnp.maximum(jnp.max(st, axis=-1, keepdims=True), sink)
            e = jnp.exp(st - m)
            e_ref[pp, :, cols] = e.astype(BF16)
            inv.append(1.0 / (jnp.sum(e, axis=-1, keepdims=True) + jnp.exp(sink - m)))
        invs.append(jnp.where(lo_q, inv[0], inv[1]))
    outs = [jnp.dot(e_ref[pp], vbd[pp // (GROUP // 2)], preferred_element_type=F32) * invs[pp]
            for pp in range(n_pairs)]
    o_ref[...] = _rmsnorm(jnp.concatenate(outs, axis=1), g_ref[...])

    @pl.when(j == pl.num_programs(1) - 1)
    def _():
        kt_ref[0] = kc.T
        vt_ref[0] = vc.T


def _attn_prompt(q, k, v, sinks, g_attn, batch, seq):
    qb = ATTN_BLOCK
    nb = seq // qb
    cur = lambda b, j: (b * nb + j, 0)
    prev = lambda b, j: (b * nb + jnp.maximum(j - 1, 0), 0)
    fix = lambda b, j: (0, 0)
    per_b = lambda b, j: (b, 0, 0)
    return pl.pallas_call(
        _attn_prompt_kernel,
        grid=(batch, nb),
        in_specs=[pl.BlockSpec(memory_space=pltpu.SMEM),
                  pl.BlockSpec((qb, D_ATTN), cur),
                  pl.BlockSpec((qb, KV_W), cur), pl.BlockSpec((qb, KV_W), prev),
                  pl.BlockSpec((qb, KV_W), cur), pl.BlockSpec((qb, KV_W), prev),
                  pl.BlockSpec((1, qb, 4 * qb), lambda b, j: (jnp.minimum(j, 1), 0, 0)),
                  pl.BlockSpec((1, D_ATTN), fix)],
        out_specs=(pl.BlockSpec((qb, D_ATTN), cur), pl.BlockSpec((1, KV_W, qb), per_b),
                   pl.BlockSpec((1, KV_W, qb), per_b)),
        out_shape=(jax.ShapeDtypeStruct((batch * seq, D_ATTN), F32),
                   jax.ShapeDtypeStruct((batch, KV_W, qb), F32),
                   jax.ShapeDtypeStruct((batch, KV_W, qb), F32)),
        scratch_shapes=[pltpu.VMEM((N_HEADS // 2, qb, 4 * qb), F32),
                        pltpu.VMEM((N_HEADS // 2, qb, 4 * qb), BF16)],
        compiler_params=_cparams(2),
        name="attn_prompt",
    )(sinks, q, k, k, v, v, _band_bias(qb), g_attn)


def _attn_sample_kernel(sink_ref, q_ref, kn_ref, vn_ref, kt_ref, vt_ref, g_ref,
                        o_ref, nkt_ref, nvt_ref, acc_ref):
    bb = q_ref.shape[0]
    q = q_ref[...] * (HEAD_DIM ** -0.5)
    kn = kn_ref[...]
    vn = vn_ref[...]
    kt = kt_ref[...]
    vt = vt_ref[...]
    col = lax.broadcasted_iota(I32, (bb, bb * KV_W), 1)
    rowb = lax.broadcasted_iota(I32, (bb, bb * KV_W), 0)
    own_seq = (col >> (KV_W.bit_length() - 1)) == rowb
    half_hi = ((col >> (HEAD_DIM.bit_length() - 1)) & 1) == 1
    qbig = []
    for h in range(N_HEADS):
        kv = h // GROUP
        pair = q[:, (h // 2) * LANES:(h // 2 + 1) * LANES]
        if (h % 2) != kv:
            pair = pltpu.roll(pair, HEAD_DIM, 1)
        tiled = jnp.concatenate([pair] * bb, axis=1)
        keep = own_seq & (half_hi if kv == 1 else jnp.logical_not(half_hi))
        qbig.append(jnp.where(keep, tiled, 0.0))
    qbig = jnp.concatenate(qbig, axis=0)
    s = jnp.dot(qbig.astype(BF16), kt.astype(BF16), preferred_element_type=F32)
    qb16 = q.astype(BF16).astype(F32)
    kb16 = kn.astype(BF16).astype(F32)
    s_new, sink = [], []
    for h in range(N_HEADS):
        kv = h // GROUP
        s_new.append(jnp.sum(qb16[:, h * HEAD_DIM:(h + 1) * HEAD_DIM] * kb16[:, kv * HEAD_DIM:(kv + 1) * HEAD_DIM],
                             axis=-1, keepdims=True))
        sink.append(jnp.full((bb, 1), sink_ref[h], F32))
    s_new = jnp.concatenate(s_new, axis=0)
    sink = jnp.concatenate(sink, axis=0)
    m = jnp.maximum(jnp.maximum(jnp.max(s, axis=-1, keepdims=True), s_new), sink)
    e = jnp.exp(s - m)
    e_new = jnp.exp(s_new - m)
    inv = 1.0 / (jnp.sum(e, axis=-1, keepdims=True) + e_new + jnp.exp(sink - m))
    obig = lax.dot_general(e.astype(BF16), vt.astype(BF16), (((1,), (1,)), ((), ())),
                           preferred_element_type=F32)
    for h in range(N_HEADS):
        kv = h // GROUP
        blk = jnp.where(own_seq, obig[h * bb:(h + 1) * bb, :], 0.0)
        fold = blk[:, 0:KV_W]
        for t in range(1, bb):
            fold = fold + blk[:, t * KV_W:(t + 1) * KV_W]
        hs = slice(h * bb, (h + 1) * bb)
        ks = slice(kv * HEAD_DIM, (kv + 1) * HEAD_DIM)
        acc_ref[:, h * HEAD_DIM:(h + 1) * HEAD_DIM] = (fold[:, ks] + e_new[hs] * vn[:, ks]) * inv[hs]
    o_ref[...] = _rmsnorm(acc_ref[...], g_ref[...])

    last = lax.broadcasted_iota(I32, (KV_W, WINDOW), 1) == WINDOW - 1
    for b in range(bb):
        rs = slice(b * KV_W, (b + 1) * KV_W)
        kcol = jnp.broadcast_to(kn[b:b + 1, :], (KV_W, KV_W)).T
        vcol = jnp.broadcast_to(vn[b:b + 1, :], (KV_W, KV_W)).T
        nkt_ref[rs, :] = jnp.where(last, kcol, pltpu.roll(kt[rs, :], WINDOW - 1, 1))
        nvt_ref[rs, :] = jnp.where(last, vcol, pltpu.roll(vt[rs, :], WINDOW - 1, 1))


def _attn_sample(q, kn, vn, kt2d, vt2d, sinks, g_attn, bb):
    n = q.shape[0]
    row = lambda i: (i, 0)
    fix = lambda i: (0, 0)
    cache = pl.BlockSpec((bb * KV_W, WINDOW), row)
    return pl.pallas_call(
        _attn_sample_kernel,
        grid=(n // bb,),
        in_specs=[pl.BlockSpec(memory_space=pltpu.SMEM),
                  pl.BlockSpec((bb, D_ATTN), row), pl.BlockSpec((bb, KV_W), row),
                  pl.BlockSpec((bb, KV_W), row), cache, cache,
                  pl.BlockSpec((1, D_ATTN), fix)],
        out_specs=(pl.BlockSpec((bb, D_ATTN), row), cache, cache),
        out_shape=(jax.ShapeDtypeStruct((n, D_ATTN), F32),
                   jax.ShapeDtypeStruct(kt2d.shape, F32), jax.ShapeDtypeStruct(vt2d.shape, F32)),
        scratch_shapes=[pltpu.VMEM((bb, D_ATTN), F32)],
        compiler_params=_cparams(1),
        name="attn_sample",
    )(sinks, q, kn, vn, kt2d, vt2d, g_attn)


def _softplus(z):
    return jnp.maximum(z, 0.0) + jnp.log1p(jnp.exp(-jnp.abs(z)))


def _lru_gates(xc, wa_ref, ba_ref, wx_ref, bx_ref, lam_ref):
    xb = xc.astype(BF16)
    r = jax.nn.sigmoid(jnp.dot(xb, wa_ref[...], preferred_element_type=F32) + ba_ref[...])
    i = jax.nn.sigmoid(jnp.dot(xb, wx_ref[...], preferred_element_type=F32) + bx_ref[...])
    log_a = (-LRU_C * r) * _softplus(-lam_ref[...])
    a = jnp.exp(log_a)
    z = -jnp.tanh(log_a) * (a * a + 1.0)
    u = jnp.where(z > 0.0, z * lax.rsqrt(z), 0.0) * (i * xc)
    return a, u


def _lru_scan(a, u, h0):
    ng = a.shape[0] // SUBLANES
    a3 = a.reshape(ng, SUBLANES, D_RNN)
    u3 = u.reshape(ng, SUBLANES, D_RNN)
    t8 = lax.broadcasted_iota(I32, (ng, SUBLANES, D_RNN), 1)
    d = 1
    while d < SUBLANES:
        a_s = jnp.where(t8 >= d, pltpu.roll(a3, d, 1), 1.0)
        u_s = jnp.where(t8 >= d, pltpu.roll(u3, d, 1), 0.0)
        u3 = a3 * u_s + u3
        a3 = a3 * a_s
        d *= 2
    carry = h0
    groups = []
    for g in range(ng):
        hg = a3[g] * carry + u3[g]
        groups.append(hg)
        carry = hg[SUBLANES - 1:SUBLANES, :]
    return jnp.concatenate(groups, axis=0), carry


def _lru_scan_tiles(a_ref, u_ref, h_ref, h0):
    nl, rows, _ = a_ref.shape
    ng = rows // SUBLANES
    step = lambda ref, s: jnp.concatenate(
        [ref[j, pl.ds(s, ng, stride=SUBLANES), :] for j in range(nl)], axis=1)
    prods = [step(a_ref, 0)]
    locs = [step(u_ref, 0)]
    for s in range(1, SUBLANES):
        a_s = step(a_ref, s)
        locs.append(a_s * locs[-1] + step(u_ref, s))
        prods.append(a_s * prods[-1])
    after, h_last = _lru_scan(prods[-1], locs[-1], h0)
    row = lax.broadcasted_iota(I32, (ng, D_RNN), 0)
    before = jnp.where(row == 0, h0, pltpu.roll(after, 1, 0))
    for s in range(SUBLANES):
        h_s = locs[s] + prods[s] * before
        for j in range(nl):
            h_ref[j, pl.ds(s, ng, stride=SUBLANES), :] = h_s[:, j * LANES:(j + 1) * LANES]
    return h_last


def _to_lane_tiles(ref, x):
    for j in range(ref.shape[0]):
        ref[j] = x[:, j * LANES:(j + 1) * LANES]


def _rnn_prompt_kernel(xr_ref, yr_ref, cw_ref, cb_ref, wa_ref, ba_ref, wx_ref, bx_ref, lam_ref, g_ref,
                       o_ref, hl_ref, ext_ref, h_ref):
    c = pl.program_id(1)
    tc = xr_ref.shape[0]
    pad = SUBLANES

    @pl.when(c == 0)
    def _():
        ext_ref[0:pad, :] = jnp.zeros((pad, D_RNN), F32)
        h_ref[...] = jnp.zeros((1, D_RNN), F32)

    ext_ref[pad:pad + tc, :] = xr_ref[...]
    cw = cw_ref[...]
    xc = cb_ref[...] + ext_ref[pad:pad + tc, :] * cw[CONV_WIDTH - 1:CONV_WIDTH, :]
    for w in range(CONV_WIDTH - 1):
        sh = CONV_WIDTH - 1 - w
        xc = xc + ext_ref[pad - sh:pad - sh + tc, :] * cw[w:w + 1, :]
    ext_ref[0:pad, :] = ext_ref[tc:tc + pad, :]

    a, u = _lru_gates(xc, wa_ref, ba_ref, wx_ref, bx_ref, lam_ref)
    h, carry = _lru_scan(a, u, h_ref[...])
    h_ref[...] = carry
    hl_ref[0] = carry
    o_ref[...] = _rmsnorm(jax.nn.gelu(yr_ref[...]) * h, g_ref[...])


def _rnn_prompt(xr, yr, cw, cb, wa, ba, wx, bx, lam, g, batch, seq, tc):
    nc = seq // tc
    cur = lambda b, c: (b * nc + c, 0)
    fix = lambda b, c: (0, 0)
    vec = pl.BlockSpec((1, D_RNN), fix)
    return pl.pallas_call(
        _rnn_prompt_kernel,
        grid=(batch, nc),
        in_specs=[pl.BlockSpec((tc, D_RNN), cur), pl.BlockSpec((tc, D_RNN), cur),
                  pl.BlockSpec((CONV_WIDTH, D_RNN), fix), vec,
                  pl.BlockSpec((D_RNN, D_RNN), fix), vec,
                  pl.BlockSpec((D_RNN, D_RNN), fix), vec, vec, vec],
        out_specs=(pl.BlockSpec((tc, D_RNN), cur), pl.BlockSpec((1, 1, D_RNN), lambda b, c: (b, 0, 0))),
        out_shape=(jax.ShapeDtypeStruct((batch * seq, D_RNN), F32),
                   jax.ShapeDtypeStruct((batch, 1, D_RNN), F32)),
        scratch_shapes=[pltpu.VMEM((tc + SUBLANES, D_RNN), F32), pltpu.VMEM((1, D_RNN), F32)],
        compiler_params=_cparams(2),
        name="rnn_prompt",
    )(xr, yr, cw, cb, wa, ba, wx, bx, lam, g)


def _rnn_sample_kernel(xr_ref, yr_ref, hist_ref, h0_ref, cw_ref, cb_ref, wa_ref, ba_ref, wx_ref, bx_ref,
                       lam_ref, g_ref, o_ref, hl_ref, nh_ref):
    cw = cw_ref[...]
    xr = xr_ref[...]
    xc = cb_ref[...] + xr * cw[CONV_WIDTH - 1:CONV_WIDTH, :]
    for w in range(CONV_WIDTH - 1):
        xc = xc + hist_ref[w] * cw[w:w + 1, :]
    a, u = _lru_gates(xc, wa_ref, ba_ref, wx_ref, bx_ref, lam_ref)
    h = a * h0_ref[...] + u
    hl_ref[...] = h
    o_ref[...] = _rmsnorm(jax.nn.gelu(yr_ref[...]) * h, g_ref[...])
    for w in range(CONV_WIDTH - 2):
        nh_ref[w] = hist_ref[w + 1]
    nh_ref[CONV_WIDTH - 2] = xr


def _rnn_sample(xr, yr, hist, h0, cw, cb, wa, ba, wx, bx, lam, g):
    n = xr.shape[0]
    full = lambda a: pl.BlockSpec(a.shape, lambda: (0,) * a.ndim)
    args = (xr, yr, hist, h0, cw, cb, wa, ba, wx, bx, lam, g)
    return pl.pallas_call(
        _rnn_sample_kernel,
        in_specs=[full(a) for a in args],
        out_specs=(pl.BlockSpec((n, D_RNN), lambda: (0, 0)), pl.BlockSpec((n, D_RNN), lambda: (0, 0)),
                   pl.BlockSpec(hist.shape, lambda: (0, 0, 0))),
        out_shape=(jax.ShapeDtypeStruct((n, D_RNN), F32), jax.ShapeDtypeStruct((n, D_RNN), F32),
                   jax.ShapeDtypeStruct(hist.shape, F32)),
        compiler_params=pltpu.CompilerParams(vmem_limit_bytes=VMEM_LIMIT),
        name="rnn_sample",
    )(*args)


def _front_kernel(*refs, tiles_per_seq):
    i = pl.program_id(0)
    q_s, k_s, v_s, xr_s, yr_s, ext_ref, h_ref = refs[25:32]

    @pl.when(i == 0)
    def _():
        for r in (q_s, k_s, v_s, xr_s, yr_s, ext_ref, h_ref):
            r[...] = jnp.zeros(r.shape, F32)

    for cur in range(2):
        @pl.when(i % 2 == cur)
        def _():
            _front_body(cur, 1 - cur, *refs, tiles_per_seq=tiles_per_seq)


def _front_body(cur, prv, sink_ref, x_ref, gm_ref, w_ref, gq_ref, gk_ref, c_ref, s1_ref, s2_ref, bias_ref,
                ga_ref, cw_ref, cb_ref, wa_ref, ba_ref, wx_ref, bx_ref, lam_ref, gr_ref,
                an_ref, rn_ref, kt_ref, vt_ref, hl_ref, cx_ref,
                q_s, k_s, v_s, xr_s, yr_s, ext_ref, h_ref, s_ref, e_ref, a_scr, u_scr, hs_scr,
                *, tiles_per_seq):
    i = pl.program_id(0)
    tm = x_ref.shape[0]
    qb = ATTN_BLOCK
    t = jnp.maximum(i - 1, 0)
    first_tile = (t % tiles_per_seq) == 0

    nqb = tm // qb
    n_pairs = N_HEADS // 2
    lo_k = lax.broadcasted_iota(I32, (2 * qb, LANES), 1) < HEAD_DIM
    lo_q = lax.broadcasted_iota(I32, (qb, LANES), 1) < HEAD_DIM
    nt_dims = (((1,), (1,)), ((), ()))
    vbds = []
    for jb in range(nqb):
        k2 = k_s[prv, jb * qb:(jb + 2) * qb, :]
        v2 = v_s[prv, jb * qb:(jb + 2) * qb, :]
        k2r = pltpu.roll(k2, HEAD_DIM, 1)
        v2r = pltpu.roll(v2, HEAD_DIM, 1)
        bias = bias_ref[jnp.where(first_tile, 0, 1)] if jb == 0 else bias_ref[1]
        for kv in range(N_KV_HEADS):
            ka, kb = (k2, k2r) if kv == 0 else (k2r, k2)
            va, vb = (v2, v2r) if kv == 0 else (v2r, v2)
            kbd = jnp.concatenate([jnp.where(lo_k, ka, 0.0), jnp.where(lo_k, 0.0, kb)], axis=0).astype(BF16)
            vbds.append(jnp.concatenate([jnp.where(lo_k, va, 0.0), jnp.where(lo_k, 0.0, vb)],
                                        axis=0).astype(BF16))
            for p in range(GROUP // 2):
                pp = kv * (GROUP // 2) + p
                qp = (q_s[prv, jb * qb:(jb + 1) * qb, pp * LANES:(pp + 1) * LANES]
                      * (HEAD_DIM ** -0.5)).astype(BF16)
                s_ref[jb * n_pairs + pp] = lax.dot_general(qp, kbd, nt_dims,
                                                           preferred_element_type=F32) + bias
    invs = []
    for c in range(nqb * n_pairs):
        pp = c % n_pairs
        inv = []
        for tpos in range(2):
            cols = slice(tpos * 2 * qb, (tpos + 1) * 2 * qb)
            st = s_ref[c, :, cols]
            sink = sink_ref[2 * pp + tpos]
            m = jnp.maximum(jnp.max(st, axis=-1, keepdims=True), sink)
            e = jnp.exp(st - m)
            e_ref[c, :, cols] = e.astype(BF16)
            inv.append(1.0 / (jnp.sum(e, axis=-1, keepdims=True) + jnp.exp(sink - m)))
        invs.append(jnp.where(lo_q, inv[0], inv[1]))
    for jb in range(nqb):
        outs = [jnp.dot(e_ref[jb * n_pairs + pp], vbds[jb * N_KV_HEADS + pp // (GROUP // 2)],
                        preferred_element_type=F32) * invs[jb * n_pairs + pp] for pp in range(n_pairs)]
        an_ref[jb * qb:(jb + 1) * qb, :] = _rmsnorm(jnp.concatenate(outs, axis=1), ga_ref[...])
    kt_ref[0] = k_s[prv, tm:tm + qb, :].T
    vt_ref[0] = v_s[prv, tm:tm + qb, :].T

    pad = SUBLANES
    xr = xr_s[prv]
    ext_ref[0:pad, :] = jnp.where(first_tile, 0.0, ext_ref[0:pad, :])
    ext_ref[pad:pad + tm, :] = xr
    cw = cw_ref[...]
    xc = cb_ref[...] + xr * cw[CONV_WIDTH - 1:CONV_WIDTH, :]
    for w in range(CONV_WIDTH - 1):
        sh = CONV_WIDTH - 1 - w
        xc = xc + ext_ref[pad - sh:pad - sh + tm, :] * cw[w:w + 1, :]
    ext_ref[0:pad, :] = xr[tm - pad:tm, :]
    cx_ref[0] = xr[tm - pad:tm, :]
    a, u = _lru_gates(xc, wa_ref, ba_ref, wx_ref, bx_ref, lam_ref)
    _to_lane_tiles(a_scr, a)
    _to_lane_tiles(u_scr, u)
    carry = _lru_scan_tiles(a_scr, u_scr, hs_scr, jnp.where(first_tile, 0.0, h_ref[...]))
    h_ref[...] = carry
    hl_ref[0] = carry
    hseq = jnp.concatenate([hs_scr[j] for j in range(hs_scr.shape[0])], axis=1)
    rn_ref[...] = _rmsnorm(jax.nn.gelu(yr_s[prv]) * hseq, gr_ref[...])

    h = _rmsnorm(x_ref[...], gm_ref[...])
    proj = jnp.dot(h.astype(BF16), w_ref[...], preferred_element_type=F32)
    lo = lax.broadcasted_iota(I32, (tm, LANES), 1) < HEAD_DIM
    rope = (c_ref[...], s1_ref[...], s2_ref[...], lo)
    gq = gq_ref[...]
    for j in range(D_ATTN // LANES):
        q_s[cur, :, j * LANES:(j + 1) * LANES] = _head_norm_rope(proj[:, j * LANES:(j + 1) * LANES], gq, *rope)
    k_s[cur, qb:qb + tm, :] = _head_norm_rope(proj[:, D_ATTN:D_ATTN + KV_W], gk_ref[...], *rope)
    v_s[cur, qb:qb + tm, :] = proj[:, D_ATTN + KV_W:D_ATTN + 2 * KV_W]
    k_s[cur, 0:qb, :] = k_s[prv, tm:tm + qb, :]
    v_s[cur, 0:qb, :] = v_s[prv, tm:tm + qb, :]
    o = D_ATTN + 2 * KV_W
    xr_s[cur] = proj[:, o:o + D_RNN]
    yr_s[cur] = proj[:, o + D_RNN:o + 2 * D_RNN]


def _front(x2d, sinks, g_mix, w_bf, gq2, gk2, ctab, s1tab, s2tab, g_attn,
           cw, cb, wa, ba, wx, bx, lam, g_rnn, batch, seq, tm):
    n = x2d.shape[0]
    nt = n // tm
    tps = seq // tm
    qb = ATTN_BLOCK
    cur = lambda i: (jnp.minimum(i, nt - 1), 0)
    tab = lambda i: (jnp.minimum(i, nt - 1) % tps, 0)
    fix = lambda i: (0, 0)
    prev = lambda i: (jnp.maximum(i - 1, 0), 0)
    per_seq = lambda i: (jnp.maximum(i - 1, 0) // tps, 0, 0)
    vec = lambda w: pl.BlockSpec((1, w), fix)
    return pl.pallas_call(
        functools.partial(_front_kernel, tiles_per_seq=tps),
        grid=(nt + 1,),
        in_specs=[pl.BlockSpec(memory_space=pltpu.SMEM),
                  pl.BlockSpec((tm, D_MODEL), cur), vec(D_MODEL), pl.BlockSpec((D_MODEL, D_IN), fix),
                  vec(LANES), vec(LANES),
                  pl.BlockSpec((tm, LANES), tab), pl.BlockSpec((tm, LANES), tab), pl.BlockSpec((tm, LANES), tab),
                  pl.BlockSpec((2, qb, 4 * qb), lambda i: (0, 0, 0)), vec(D_ATTN),
                  pl.BlockSpec((CONV_WIDTH, D_RNN), fix), vec(D_RNN),
                  pl.BlockSpec((D_RNN, D_RNN), fix), vec(D_RNN),
                  pl.BlockSpec((D_RNN, D_RNN), fix), vec(D_RNN), vec(D_RNN), vec(D_RNN)],
        out_specs=(pl.BlockSpec((tm, D_ATTN), prev), pl.BlockSpec((tm, D_RNN), prev),
                   pl.BlockSpec((1, KV_W, qb), per_seq), pl.BlockSpec((1, KV_W, qb), per_seq),
                   pl.BlockSpec((1, 1, D_RNN), per_seq), pl.BlockSpec((1, SUBLANES, D_RNN), per_seq)),
        out_shape=(jax.ShapeDtypeStruct((n, D_ATTN), F32), jax.ShapeDtypeStruct((n, D_RNN), F32),
                   jax.ShapeDtypeStruct((batch, KV_W, qb), F32), jax.ShapeDtypeStruct((batch, KV_W, qb), F32),
                   jax.ShapeDtypeStruct((batch, 1, D_RNN), F32),
                   jax.ShapeDtypeStruct((batch, SUBLANES, D_RNN), F32)),
        scratch_shapes=[pltpu.VMEM((2, tm, D_ATTN), F32),
                        pltpu.VMEM((2, tm + qb, KV_W), F32), pltpu.VMEM((2, tm + qb, KV_W), F32),
                        pltpu.VMEM((2, tm, D_RNN), F32), pltpu.VMEM((2, tm, D_RNN), F32),
                        pltpu.VMEM((tm + SUBLANES, D_RNN), F32), pltpu.VMEM((1, D_RNN), F32),
                        pltpu.VMEM((tm // qb * (N_HEADS // 2), qb, 4 * qb), F32),
                        pltpu.VMEM((tm // qb * (N_HEADS // 2), qb, 4 * qb), BF16),
                        pltpu.VMEM((D_RNN // LANES, tm, LANES), F32), pltpu.VMEM((D_RNN // LANES, tm, LANES), F32),
                        pltpu.VMEM((D_RNN // LANES, tm, LANES), F32)],
        compiler_params=_cparams(1),
        name="front",
    )(sinks, x2d, g_mix, w_bf, gq2, gk2, ctab, s1tab, s2tab, _band_bias(qb), g_attn,
      cw, cb, wa, ba, wx, bx, lam, g_rnn)


def _mix_route_kernel(x_ref, an_ref, rn_ref, woa_ref, wor_ref, g_ref, wr2_ref, br_ref, tri_ref, low_ref,
                      x2_ref, ts_ref, dest_ref, gate_ref, n8_ref, off_ref):
    tt = x_ref.shape[0]
    tile_rows = ts_ref.shape[0]
    x2 = x_ref[...] + jnp.dot(an_ref[...].astype(BF16), woa_ref[...], preferred_element_type=F32) \
        + jnp.dot(rn_ref[...].astype(BF16), wor_ref[...], preferred_element_type=F32)
    x2_ref[...] = x2
    hn = _rmsnorm(x2, g_ref[...])

    nt = (((1,), (1,)), ((), ()))
    hb = hn.astype(BF16)
    hmid = (hn - hb.astype(F32)).astype(BF16)
    wr2 = wr2_ref[...]
    both = lax.dot_general(wr2, hb, nt, preferred_element_type=F32)
    logits = (lax.dot_general(wr2[:N_EXPERTS], hmid, nt, preferred_element_type=F32)
              + both[N_EXPERTS:]) + both[:N_EXPERTS] + br_ref[...]

    ie = lax.broadcasted_iota(I32, (N_EXPERTS, tt), 0).astype(F32)
    l = logits
    vals, sels = [], []
    for _ in range(TOP_K):
        m = jnp.max(l, axis=0, keepdims=True)
        idx = jnp.min(jnp.where(l == m, ie, float(N_EXPERTS)), axis=0, keepdims=True)
        sel = ie == idx
        vals.append(m)
        sels.append(sel)
        l = jnp.where(sel, NEG_BIG, l)
    es = [jnp.exp(v - vals[0]) for v in vals]
    den = es[0] + es[1] + es[2] + es[3]
    gate_ref[0] = jnp.concatenate([e / den for e in es], axis=0)

    oh = jnp.zeros((N_EXPERTS, tt), F32)
    for sel in sels:
        oh = oh + jnp.where(sel, 1.0, 0.0)
    before = jnp.dot(oh.astype(BF16), tri_ref[...], preferred_element_type=F32)
    cnt = jnp.sum(oh, axis=1, keepdims=True).astype(I32)
    n8 = ((cnt + (SUBLANES - 1)) >> 3) << 3
    n8b = jnp.broadcast_to(n8, (N_EXPERTS, LANES))
    off = jnp.dot(low_ref[...], n8b.astype(F32).astype(BF16), preferred_element_type=F32)
    n8_ref[0] = n8b
    off_ref[0] = off.astype(I32)
    base = off[:, 0:1] + before
    dests = [jnp.sum(jnp.where(sel, base, 0.0), axis=0, keepdims=True).astype(I32) for sel in sels]
    dest_ref[0] = jnp.concatenate(dests, axis=0)

    ri = lax.broadcasted_iota(I32, (DISPATCH_CHUNK, tt), 0).astype(I16)
    d16 = [d.astype(I16) for d in dests]
    one = jnp.ones((DISPATCH_CHUNK, tt), BF16)
    for c in range(tile_rows // DISPATCH_CHUNK):
        p = jnp.zeros((DISPATCH_CHUNK, tt), BF16)
        for d in d16:
            p = jnp.where(ri == d - jnp.int16(c * DISPATCH_CHUNK), one, p)
        ts_ref[c * DISPATCH_CHUNK:(c + 1) * DISPATCH_CHUNK, :] = jnp.dot(
            p, hb, preferred_element_type=F32)


def _tile_rows(tt):
    return -(-(TOP_K * tt + N_EXPERTS * (SUBLANES - 1)) // DISPATCH_CHUNK) * DISPATCH_CHUNK


def _mix_route(x2d, an, rn, woa, wor, g, wrt, br, low, tt):
    n = x2d.shape[0]
    nt = n // tt
    tile_rows = _tile_rows(tt)
    tri = jnp.triu(jnp.ones((tt, tt), BF16), k=1)
    row = lambda i: (i, 0)
    fix = lambda i: (0, 0)
    t3 = lambda i: (i, 0, 0)
    in_specs = [pl.BlockSpec((tt, D_MODEL), row), pl.BlockSpec((tt, D_ATTN), row),
                pl.BlockSpec((tt, D_RNN), row), pl.BlockSpec((D_ATTN, D_MODEL), fix),
                pl.BlockSpec((D_RNN, D_MODEL), fix), pl.BlockSpec((1, D_MODEL), fix),
                pl.BlockSpec((2 * N_EXPERTS, D_MODEL), fix), pl.BlockSpec((N_EXPERTS, 1), fix),
                pl.BlockSpec((tt, tt), fix), pl.BlockSpec((N_EXPERTS, N_EXPERTS), fix)]
    out_shape = (jax.ShapeDtypeStruct((n, D_MODEL), F32),
                 jax.ShapeDtypeStruct((nt * tile_rows, D_MODEL), F32),
                 jax.ShapeDtypeStruct((nt, TOP_K, tt), I32),
                 jax.ShapeDtypeStruct((nt, TOP_K, tt), F32),
                 jax.ShapeDtypeStruct((nt, N_EXPERTS, LANES), I32),
                 jax.ShapeDtypeStruct((nt, N_EXPERTS, LANES), I32))
    out_specs = (pl.BlockSpec((tt, D_MODEL), row),
                 pl.BlockSpec((tile_rows, D_MODEL), row),
                 pl.BlockSpec((1, TOP_K, tt), t3), pl.BlockSpec((1, TOP_K, tt), t3),
                 pl.BlockSpec((1, N_EXPERTS, LANES), t3), pl.BlockSpec((1, N_EXPERTS, LANES), t3))
    return pl.pallas_call(
        _mix_route_kernel,
        grid=(nt,),
        in_specs=in_specs,
        out_specs=out_specs,
        out_shape=out_shape,
        compiler_params=_cparams(1),
        name="mix_route",
    )(x2d, an, rn, woa, wor, g, wrt, br, tri, low)


LOW_BITS = 4


def _start_piece(src_hbm, dst_buf, sem, s, d, l8, nbits):
    def bit_copy(c):
        size = SUBLANES << c
        low = (l8 & ((1 << c) - 1)) * SUBLANES

        @pl.when(((l8 >> c) & 1) == 1)
        def _():
            pltpu.make_async_copy(
                src_hbm.at[pl.ds(pl.multiple_of(s + low, SUBLANES), size)],
                dst_buf.at[pl.ds(pl.multiple_of(d + low, SUBLANES), size)], sem).start()

    for c in range(min(LOW_BITS, nbits)):
        bit_copy(c)
    if nbits > LOW_BITS:
        @pl.when(l8 >= (1 << LOW_BITS))
        def _():
            for c in range(LOW_BITS, nbits):
                bit_copy(c)


def _start_pieces(src_hbm, dst_buf, sem, p_lo, p_hi, psrc_ref, pdst_ref, plen_ref, nbits):
    def body(p, carry):
        _start_piece(src_hbm, dst_buf, sem, psrc_ref[p], pdst_ref[p], plen_ref[p], nbits)
        return carry

    lax.fori_loop(p_lo, p_hi, body, 0)


def _wait_rows(src_hbm, dst_buf, sem, rows8, nbits):
    for c in range(nbits):
        size = SUBLANES << c

        @pl.when(((rows8 >> c) & 1) == 1)
        def _():
            pltpu.make_async_copy(src_hbm.at[pl.ds(0, size)], dst_buf.at[pl.ds(0, size)], sem).wait()


def _moe_gmm_kernel(be_ref, rows_ref, wslot_ref, nxt_ref,
                    psa_ref, pea_ref, srca_ref, dsta_ref, lena_ref, hsrca_ref, hlena_ref,
                    psb_ref, peb_ref, srcb_ref, dstb_ref, lenb_ref, hsrcb_ref, hlenb_ref,
                    tsa_hbm, tsb_hbm, wgu_hbm, wdn_hbm, bg_ref, bu_ref, bd_ref, perm_ref,
                    ys_ref, lhs_ref, wgu_buf, wdn_buf, wg_ref, wu_ref, wd_ref, sem_ref, wsem_ref,
                    *, nbits_a, nbits_b):
    j = pl.program_id(0)
    nb = pl.num_programs(0)
    slot = j % 2
    bm = lhs_ref.shape[1]

    def gather(blk, sl):
        for ts_hbm, ps, pe, src, dst, ln, hsrc, hlen, nbits in (
                (tsa_hbm, psa_ref, pea_ref, srca_ref, dsta_ref, lena_ref, hsrca_ref, hlena_ref, nbits_a),
                (tsb_hbm, psb_ref, peb_ref, srcb_ref, dstb_ref, lenb_ref, hsrcb_ref, hlenb_ref, nbits_b)):
            _start_pieces(ts_hbm, lhs_ref.at[sl], sem_ref.at[sl], ps[blk], pe[blk], src, dst, ln, nbits)
            _start_piece(ts_hbm, lhs_ref.at[sl], sem_ref.at[sl], hsrc[blk], 0, hlen[blk], nbits)

    def weight_copies(e, ws):
        return (pltpu.make_async_copy(wgu_hbm.at[e], wgu_buf.at[ws], wsem_ref.at[ws]),
                pltpu.make_async_copy(wdn_hbm.at[e], wdn_buf.at[ws], wsem_ref.at[ws]))

    @pl.when(j == 0)
    def _():
        lhs_ref[...] = jnp.zeros(lhs_ref.shape, F32)
        gather(0, 0)
        for cp in weight_copies(be_ref[0], wslot_ref[0]):
            cp.start()

    @pl.when(j + 1 < nb)
    def _():
        gather(j + 1, 1 - slot)

    @pl.when(jnp.logical_or(j == 0, be_ref[j] != be_ref[jnp.maximum(j - 1, 0)]))
    def _():
        ws = wslot_ref[j]
        for cp in weight_copies(be_ref[j], ws):
            cp.wait()
        nxt = nxt_ref[j]

        @pl.when(nxt >= 0)
        def _():
            for cp in weight_copies(nxt, 1 - ws):
                cp.start()

        perm = perm_ref[...]
        half = PERM_COLS // 2
        for c in range(2 * D_FF // PERM_COLS):
            wb = wgu_buf[ws, :, c * PERM_COLS:(c + 1) * PERM_COLS].astype(BF16)
            wp = jnp.dot(wb, perm, preferred_element_type=F32).astype(BF16)
            wg_ref[:, c * half:(c + 1) * half] = wp[:, :half]
            wu_ref[:, c * half:(c + 1) * half] = wp[:, half:]
        wd_ref[...] = wdn_buf[ws].astype(BF16)

    rows8 = rows_ref[j]
    _wait_rows(tsa_hbm, lhs_ref.at[slot], sem_ref.at[slot], rows8, (bm // SUBLANES).bit_length())

    @pl.when(rows8 > 0)
    def _():
        x = lhs_ref[slot].astype(BF16)
        gate = jnp.dot(x, wg_ref[...], preferred_element_type=F32) + bg_ref[0]
        up = jnp.dot(x, wu_ref[...], preferred_element_type=F32) + bu_ref[0]
        gate = jnp.minimum(gate, SWIGLU_LIMIT)
        up = jnp.clip(up, -SWIGLU_LIMIT, SWIGLU_LIMIT)
        act = (up + 1.0) * (gate * jax.nn.sigmoid(SWIGLU_ALPHA * gate))
        ys_ref[...] = jnp.dot(act.astype(BF16), wd_ref[...], preferred_element_type=F32) + bd_ref[0]

    @pl.when(rows8 == 0)
    def _():
        ys_ref[...] = jnp.zeros(ys_ref.shape, F32)


def _moe_gmm(blocks, tabs_a, tabs_b, ts_a, ts_b, w_gu, w_dn, bg, bu, bd, perm, nblocks, bm, nbits_a, nbits_b):
    we = lambda j, be, *_: (be[j], 0, 0)
    grid_spec = pltpu.PrefetchScalarGridSpec(
        num_scalar_prefetch=18,
        grid=(nblocks,),
        in_specs=[pl.BlockSpec(memory_space=pl.ANY), pl.BlockSpec(memory_space=pl.ANY),
                  pl.BlockSpec(memory_space=pl.ANY), pl.BlockSpec(memory_space=pl.ANY),
                  pl.BlockSpec((1, 1, D_FF), we), pl.BlockSpec((1, 1, D_FF), we),
                  pl.BlockSpec((1, 1, D_MODEL), we),
                  pl.BlockSpec((PERM_COLS, PERM_COLS), lambda j, *_: (0, 0))],
        out_specs=pl.BlockSpec((bm, D_MODEL), lambda j, *_: (j, 0)),
        scratch_shapes=[pltpu.VMEM((2, bm, D_MODEL), F32),
                        pltpu.VMEM((2, D_MODEL, 2 * D_FF), F32), pltpu.VMEM((2, D_FF, D_MODEL), F32),
                        pltpu.VMEM((D_MODEL, D_FF), BF16), pltpu.VMEM((D_MODEL, D_FF), BF16),
                        pltpu.VMEM((D_FF, D_MODEL), BF16),
                        pltpu.SemaphoreType.DMA((2,)), pltpu.SemaphoreType.DMA((2,))],
    )
    return pl.pallas_call(
        functools.partial(_moe_gmm_kernel, nbits_a=nbits_a, nbits_b=nbits_b),
        grid_spec=grid_spec,
        out_shape=jax.ShapeDtypeStruct((nblocks * bm, D_MODEL), F32),
        compiler_params=_cparams(1),
        name="moe_gmm",
    )(*blocks, *tabs_a, *tabs_b, ts_a, ts_b, w_gu, w_dn, bg, bu, bd, perm)


def _combine_kernel(psrc_ref, pdst_ref, plen_ref, tlo_ref, thi_ref, tsrc_ref, tdst_ref, tlen_ref, rows_ref,
                    ys_hbm, x2_ref, dest_ref, gate_ref, o_ref,
                    buf_ref, db_ref, gb_ref, sem_ref, *, nbits):
    i = pl.program_id(0)
    n = pl.num_programs(0)
    slot = i % 2
    tt = x2_ref.shape[0]
    tile_rows = buf_ref.shape[1]

    def gather(tile, sl):
        _start_pieces(ys_hbm, buf_ref.at[sl], sem_ref.at[sl], tile * N_EXPERTS, (tile + 1) * N_EXPERTS,
                      psrc_ref, pdst_ref, plen_ref, nbits)
        _start_pieces(ys_hbm, buf_ref.at[sl], sem_ref.at[sl], tlo_ref[tile], thi_ref[tile],
                      tsrc_ref, tdst_ref, tlen_ref, nbits)

    @pl.when(i == 0)
    def _():
        buf_ref[...] = jnp.zeros(buf_ref.shape, F32)
        gather(0, 0)

    @pl.when(i + 1 < n)
    def _():
        gather(i + 1, 1 - slot)

    _wait_rows(ys_hbm, buf_ref.at[slot], sem_ref.at[slot], rows_ref[i], (tile_rows // SUBLANES).bit_length())

    dest = dest_ref[0]
    gate = gate_ref[0]
    for k in range(TOP_K):
        db_ref[k] = jnp.broadcast_to(dest[:, k:k + 1], (tt, DISPATCH_CHUNK)).astype(I16)
        gb_ref[k] = jnp.broadcast_to(gate[:, k:k + 1], (tt, DISPATCH_CHUNK)).astype(BF16)
    li = lax.broadcasted_iota(I32, (tt, DISPATCH_CHUNK), 1).astype(I16)
    gms = []
    for c in range(tile_rows // DISPATCH_CHUNK):
        lic = li + jnp.int16(c * DISPATCH_CHUNK)
        gm = jnp.zeros((tt, DISPATCH_CHUNK), BF16)
        for k in range(TOP_K):
            gm = jnp.where(lic == db_ref[k], gb_ref[k], gm)
        gms.append(gm)
    o_ref[...] = x2_ref[...] + jnp.dot(jnp.concatenate(gms, axis=1), buf_ref[slot].astype(BF16),
                                       preferred_element_type=F32)


def _combine(tabs, ys, x2, dest, gate, tt, nbits):
    n = x2.shape[0]
    nt = n // tt
    tile_rows = _tile_rows(tt)
    grid_spec = pltpu.PrefetchScalarGridSpec(
        num_scalar_prefetch=len(tabs),
        grid=(nt,),
        in_specs=[pl.BlockSpec(memory_space=pl.ANY),
                  pl.BlockSpec((tt, D_MODEL), lambda i, *_: (i, 0)),
                  pl.BlockSpec((1, tt, TOP_K), lambda i, *_: (i, 0, 0)),
                  pl.BlockSpec((1, tt, TOP_K), lambda i, *_: (i, 0, 0))],
        out_specs=pl.BlockSpec((tt, D_MODEL), lambda i, *_: (i, 0)),
        scratch_shapes=[pltpu.VMEM((2, tile_rows, D_MODEL), F32),
                        pltpu.VMEM((TOP_K, tt, DISPATCH_CHUNK), I16), pltpu.VMEM((TOP_K, tt, DISPATCH_CHUNK), BF16),
                        pltpu.SemaphoreType.DMA((2,))],
    )
    return pl.pallas_call(
        functools.partial(_combine_kernel, nbits=nbits),
        grid_spec=grid_spec,
        out_shape=jax.ShapeDtypeStruct((n, D_MODEL), F32),
        compiler_params=_cparams(1),
        name="combine",
    )(*tabs, ys, x2, dest, gate)


def _piece_tables(n8_a, off_a, rows_a, n8_b, off_b, bm, nblocks):
    nta = n8_a.shape[0]
    n8 = jnp.concatenate([n8_a, n8_b], axis=0)
    seg_off = jnp.concatenate([off_a, off_b], axis=0)
    n_tiles = n8.shape[0]
    tile_base = jnp.concatenate([jnp.arange(nta, dtype=I32) * rows_a, jnp.zeros((n_tiles - nta,), I32)])[:, None]
    tot = jnp.sum(n8, axis=0)
    pos0 = jnp.cumsum(n8, axis=0) - n8
    nblk = (tot + bm - 1) // bm
    cs = jnp.cumsum(nblk)
    bs = cs - nblk
    kblk = pos0 // bm
    len0 = jnp.minimum(n8, (kblk + 1) * bm - pos0)
    len1 = n8 - len0
    b0 = bs[None, :] + kblk
    src0 = tile_base + seg_off
    in_blk = pos0 - kblk * bm
    jj = jnp.arange(nblocks, dtype=I32)
    i32 = lambda v: v.astype(I32)

    def gmm_tabs(sl):
        em = lambda v: v[sl].T.reshape(-1)
        blk_em = em(b0)
        first = i32(jnp.sum(blk_em[None, :] < jj[:, None], axis=1))
        last = i32(jnp.sum(blk_em[None, :] <= jj[:, None], axis=1))
        hit = (blk_em[None, :] + 1 == jj[:, None]) & (em(len1)[None, :] > 0)
        tail_src = i32(jnp.sum(jnp.where(hit, em(src0 + len0)[None, :], 0), axis=1))
        tail_len = i32(jnp.sum(jnp.where(hit, em(len1)[None, :], 0), axis=1) // SUBLANES)
        return first, last, i32(em(src0)), i32(em(in_blk)), i32(em(len0) // SUBLANES), tail_src, tail_len

    def comb_tabs(sl):
        tm = lambda v: v[sl].reshape(-1)
        has_tail = len1[sl] > 0
        cnt = jnp.sum(has_tail, axis=1)
        lo = jnp.cumsum(cnt) - cnt
        slot_ = lo[:, None] + jnp.cumsum(has_tail, axis=1) - has_tail
        hit = (slot_.reshape(-1)[None, :] == jj[:, None]) & has_tail.reshape(-1)[None, :]
        pick = lambda v: i32(jnp.sum(jnp.where(hit, tm(v)[None, :], 0), axis=1))
        return (i32(tm(b0 * bm + in_blk)), i32(tm(seg_off)), i32(tm(len0) // SUBLANES),
                i32(lo), i32(lo + cnt), pick((b0 + 1) * bm), pick(seg_off + len0), pick(len1 // SUBLANES),
                i32(jnp.sum(n8[sl], axis=1) // SUBLANES))

    count_le = lambda v: jnp.sum(cs[None, :] <= v[:, None], axis=1)
    n_active = cs[-1]
    e_last = count_le(jnp.maximum(n_active - 1, 0).reshape(1))[0]
    block_e = jnp.minimum(count_le(jj), e_last).astype(I32)
    ee = jnp.arange(N_EXPERTS, dtype=I32)
    mine = (jj[:, None] >= bs[None, :]) & (jj[:, None] < cs[None, :])
    left = jnp.clip(tot[None, :] - (jj[:, None] - bs[None, :]) * bm, 0, bm)
    rows8 = (jnp.sum(jnp.where(mine, left, 0), axis=1) // SUBLANES).astype(I32)
    has = nblk > 0
    run = jnp.cumsum(has.astype(I32)) - 1
    later = (ee[None, :] > ee[:, None]) & has[None, :]
    nxt_e = jnp.min(jnp.where(later, ee[None, :], N_EXPERTS), axis=1)
    nxt_e = jnp.where(nxt_e == N_EXPERTS, -1, nxt_e)
    own = block_e[:, None] == ee[None, :]
    wslot = (jnp.sum(jnp.where(own, run[None, :], 0), axis=1) % 2).astype(I32)
    nxt = jnp.sum(jnp.where(own, nxt_e[None, :], 0), axis=1).astype(I32)
    a, b = slice(0, nta), slice(nta, n_tiles)
    return (block_e, rows8, wslot, nxt), gmm_tabs(a), gmm_tabs(b), comb_tabs(a), comb_tabs(b)


def _block_diag(w):
    nb, bi, bo = w.shape
    eye = jnp.eye(nb, dtype=w.dtype)
    return (eye[:, None, :, None] * w[:, :, None, :]).reshape(nb * bi, nb * bo)


def _step(x_prompt, x_sample, cache_k, cache_v, state_conv, state_h, g_mix_norm, w_in, g_q_norm, g_k_norm,
          attn_sinks, conv_w, conv_b, w_lru_a, b_lru_a, w_lru_x, b_lru_x, lru_lambda, g_attn_out, g_rnn_out,
          w_out, g_ffn_norm, w_router, b_router, w_gate_up, b_gate_up, w_down, b_down,
          *, tm, tt, tc, bm, past_len):
    B, S, D = x_prompt.shape
    NS = x_sample.shape[0]
    assert x_sample.shape[1] == 1 and D == D_MODEL
    assert (B * S) % tt == 0 and (B * S) % tm == 0 and S % tc == 0 and S % ATTN_BLOCK == 0 and S % tm == 0
    assert NS % SUBLANES == 0 and tt <= bm
    assert tt % SUBLANES == 0 and NS <= bm
    n_pt = (B * S) // tt
    total_rows = TOP_K * (B * S + NS) + (n_pt + 1) * N_EXPERTS * (SUBLANES - 1)
    nblocks = -(-total_rows // bm) + N_EXPERTS
    nbits_p = (tt // SUBLANES).bit_length()
    nbits_s = (NS // SUBLANES).bit_length()

    l = 0
    row = lambda v: v[l].reshape(1, -1)
    w_in_bf = w_in[l].astype(BF16)
    gq2 = jnp.tile(g_q_norm[l], 2).reshape(1, LANES)
    gk2 = jnp.tile(g_k_norm[l], 2).reshape(1, LANES)
    wa = _block_diag(w_lru_a[l]).astype(BF16)
    wx = _block_diag(w_lru_x[l]).astype(BF16)
    ba = b_lru_a[l].reshape(1, D_RNN)
    bx = b_lru_x[l].reshape(1, D_RNN)
    wo = w_out[l].astype(BF16)
    woa, wor = wo[:D_ATTN], wo[D_ATTN:]
    wr = w_router[l].T
    wr_hi = wr.astype(BF16)
    wrt = jnp.concatenate([wr_hi, (wr - wr_hi.astype(F32)).astype(BF16)], axis=0)
    br = b_router[l].reshape(N_EXPERTS, 1)
    low = jnp.tril(jnp.ones((N_EXPERTS, N_EXPERTS), BF16), k=-1)
    bgu = b_gate_up[l].reshape(N_EXPERTS, D_FF, 2)
    bg = bgu[:, :, 0].reshape(N_EXPERTS, 1, D_FF)
    bu = bgu[:, :, 1].reshape(N_EXPERTS, 1, D_FF)
    bd = b_down[l].reshape(N_EXPERTS, 1, D_MODEL)
    half = PERM_COLS // 2
    pr = jnp.arange(PERM_COLS)
    perm = (pr[None, :] == jnp.where(pr % 2 == 0, pr // 2, half + pr // 2)[:, None]).astype(BF16)
    sinks = attn_sinks[l]

    ctab, s1tab, s2tab = _rope_tables(jnp.arange(S))
    an, rn, kt_p, vt_p, h_last_p, xr_tail = _front(
        x_prompt.reshape(B * S, D), sinks, row(g_mix_norm), w_in_bf, gq2, gk2, ctab, s1tab, s2tab,
        row(g_attn_out), conv_w[l], row(conv_b), wa, ba, wx, bx, row(lru_lambda), row(g_rnn_out), B, S, tm)
    x2_p, ts_p, dest_p, gate_p, n8_p, off_p = _mix_route(
        x_prompt.reshape(B * S, D), an, rn, woa, wor, row(g_ffn_norm), wrt, br, low, tt)

    cs_tab = _rope_tables(jnp.full((NS,), past_len, I32))
    q_s, k_s, v_s, xr_s, yr_s = _in_proj(x_sample.reshape(NS, D), row(g_mix_norm), w_in_bf, gq2, gk2,
                                         *cs_tab, NS)
    to_rows = lambda c: jnp.transpose(c, (0, 2, 3, 1)).reshape(NS * KV_W, WINDOW)
    from_rows = lambda c, n: jnp.transpose(c.reshape(n, N_KV_HEADS, HEAD_DIM, WINDOW), (0, 3, 1, 2))[None]
    an_s, kt_s, vt_s = _attn_sample(q_s, k_s, v_s, to_rows(cache_k[l]), to_rows(cache_v[l]), sinks,
                                    row(g_attn_out), SUBLANES)
    rn_s, h_last_s, hist_s = _rnn_sample(xr_s, yr_s, jnp.transpose(state_conv[l], (1, 0, 2)), state_h[l],
                                         conv_w[l], row(conv_b), wa, ba, wx, bx, row(lru_lambda),
                                         row(g_rnn_out))
    x2_s, ts_s, dest_s, gate_s, n8_s, off_s = _mix_route(
        x_sample.reshape(NS, D), an_s, rn_s, woa, wor, row(g_ffn_norm), wrt, br, low, NS)

    blocks, gmm_p, gmm_s, comb_p, comb_s = _piece_tables(
        n8_p[:, :, 0], off_p[:, :, 0], _tile_rows(tt), n8_s[:, :, 0], off_s[:, :, 0], bm, nblocks)
    ys = _moe_gmm(blocks, gmm_p, gmm_s, ts_p, ts_s, w_gate_up[l], w_down[l], bg, bu, bd, perm,
                  nblocks, bm, nbits_p, nbits_s)
    tr = lambda a: jnp.transpose(a, (0, 2, 1))
    y_p = _combine(comb_p, ys, x2_p, tr(dest_p), tr(gate_p), tt, nbits_p)
    y_s = _combine(comb_s, ys, x2_s, tr(dest_s), tr(gate_s), NS, nbits_s)

    cp = xr_tail[:, SUBLANES - (CONV_WIDTH - 1):]
    return (y_p.reshape(B, S, D), y_s.reshape(NS, 1, D),
            from_rows(kt_p, B), from_rows(vt_p, B), cp[None], h_last_p.reshape(1, B, D_RNN),
            from_rows(kt_s, NS), from_rows(vt_s, NS), jnp.transpose(hist_s, (1, 0, 2))[None], h_last_s[None])


def kernel(x_prompt, x_sample, cache_k, cache_v, state_conv, state_h, g_mix_norm, w_in, g_q_norm, g_k_norm, attn_sinks, conv_w, conv_b, w_lru_a, b_lru_a, w_lru_x, b_lru_x, lru_lambda, g_attn_out, g_rnn_out, w_out, g_ffn_norm, w_router, b_router, w_gate_up, b_gate_up, w_down, b_down):
    return _step(x_prompt, x_sample, cache_k, cache_v, state_conv, state_h, g_mix_norm, w_in, g_q_norm,
                 g_k_norm, attn_sinks, conv_w, conv_b, w_lru_a, b_lru_a, w_lru_x, b_lru_x, lru_lambda,
                 g_attn_out, g_rnn_out, w_out, g_ffn_norm, w_router, b_router, w_gate_up, b_gate_up,
                 w_down, b_down, tm=512, tt=512, tc=256, bm=MOE_BLOCK_ROWS, past_len=PAST_LEN)
```

```python
import functools

import jax
import jax.numpy as jnp
from jax import lax
from jax.experimental import pallas as pl
from jax.experimental.pallas import tpu as pltpu

F32 = jnp.float32
BF16 = jnp.bfloat16
I32 = jnp.int32
I16 = jnp.int16

D_MODEL = 1024
HEAD_DIM = 64
N_HEADS = 8
N_KV_HEADS = 2
GROUP = 4
WINDOW = 128
ATTN_BLOCK = 128
ROT_DIM = 16
ROPE_THETA = 500000.0
D_ATTN = 512
D_RNN = 512
KV_W = 128
D_IN = 1792
CONV_WIDTH = 4
LRU_C = 8.0
N_EXPERTS = 32
TOP_K = 4
D_FF = 1024
SWIGLU_LIMIT = 7.0
SWIGLU_ALPHA = 1.702
EPS = 1e-6
PAST_LEN = 8192

LANES = 128
SUBLANES = 8
NEG_BIG = -1e30
VMEM_LIMIT = 56 * 1024 * 1024

MOE_BLOCK_ROWS = 512
PERM_COLS = 256
DISPATCH_CHUNK = 256


def _cparams(n_axes):
    return pltpu.CompilerParams(dimension_semantics=("arbitrary",) * n_axes,
                                vmem_limit_bytes=VMEM_LIMIT)


def _rmsnorm(x, g):
    ms = jnp.mean(x * x, axis=-1, keepdims=True)
    return (x * lax.rsqrt(ms + EPS)) * g


def _head_norm_rope(t, g, c, s1, s2, lo):
    sq = t * t
    s_lo = jnp.sum(jnp.where(lo, sq, 0.0), axis=-1, keepdims=True)
    s_hi = jnp.sum(jnp.where(lo, 0.0, sq), axis=-1, keepdims=True)
    ms = jnp.where(lo, s_lo, s_hi) * (1.0 / HEAD_DIM)
    n = (t * lax.rsqrt(ms + EPS)) * g
    up = pltpu.roll(n, LANES - ROT_DIM // 2, 1)
    dn = pltpu.roll(n, ROT_DIM // 2, 1)
    return n * c + up * s1 + dn * s2


def _in_proj_kernel(x_ref, g_ref, w_ref, gq_ref, gk_ref, c_ref, s1_ref, s2_ref,
                    q_ref, k_ref, v_ref, xr_ref, yr_ref):
    tm = x_ref.shape[0]
    h = _rmsnorm(x_ref[...], g_ref[...])
    proj = jnp.dot(h.astype(BF16), w_ref[...], preferred_element_type=F32)
    rope = (c_ref[...], s1_ref[...], s2_ref[...], lax.broadcasted_iota(I32, (tm, LANES), 1) < HEAD_DIM)
    gq = gq_ref[...]
    for j in range(D_ATTN // LANES):
        q_ref[:, j * LANES:(j + 1) * LANES] = _head_norm_rope(proj[:, j * LANES:(j + 1) * LANES], gq, *rope)
    k_ref[...] = _head_norm_rope(proj[:, D_ATTN:D_ATTN + KV_W], gk_ref[...], *rope)
    v_ref[...] = proj[:, D_ATTN + KV_W:D_ATTN + 2 * KV_W]
    o = D_ATTN + 2 * KV_W
    xr_ref[...] = proj[:, o:o + D_RNN]
    yr_ref[...] = proj[:, o + D_RNN:o + 2 * D_RNN]


def _in_proj(x2d, g, w_bf, gq2, gk2, ctab, s1tab, s2tab, tm):
    n = x2d.shape[0]
    ntab = ctab.shape[0] // tm
    row = lambda i: (i, 0)
    fix = lambda i: (0, 0)
    tab = lambda i: (i % ntab, 0)
    out_shapes = (jax.ShapeDtypeStruct((n, D_ATTN), F32), jax.ShapeDtypeStruct((n, KV_W), F32),
                  jax.ShapeDtypeStruct((n, KV_W), F32), jax.ShapeDtypeStruct((n, D_RNN), F32),
                  jax.ShapeDtypeStruct((n, D_RNN), F32))
    return pl.pallas_call(
        _in_proj_kernel,
        grid=(n // tm,),
        in_specs=[pl.BlockSpec((tm, D_MODEL), row), pl.BlockSpec((1, D_MODEL), fix),
                  pl.BlockSpec((D_MODEL, D_IN), fix), pl.BlockSpec((1, LANES), fix),
                  pl.BlockSpec((1, LANES), fix), pl.BlockSpec((tm, LANES), tab),
                  pl.BlockSpec((tm, LANES), tab), pl.BlockSpec((tm, LANES), tab)],
        out_specs=(pl.BlockSpec((tm, D_ATTN), row), pl.BlockSpec((tm, KV_W), row),
                   pl.BlockSpec((tm, KV_W), row), pl.BlockSpec((tm, D_RNN), row),
                   pl.BlockSpec((tm, D_RNN), row)),
        out_shape=out_shapes,
        compiler_params=_cparams(1),
        name="in_proj",
    )(x2d, g, w_bf, gq2, gk2, ctab, s1tab, s2tab)


def _rope_tables(pos):
    half = ROT_DIM // 2
    inv = ROPE_THETA ** (-jnp.arange(0, ROT_DIM, 2, dtype=F32) / ROT_DIM)
    ang = pos.astype(F32)[:, None] * inv[None, :]
    cos = jnp.cos(ang)
    sin = jnp.sin(ang)
    n = pos.shape[0]
    ones = jnp.ones((n, HEAD_DIM - ROT_DIM), F32)
    zeros = jnp.zeros((n, HEAD_DIM - ROT_DIM), F32)
    zh = jnp.zeros((n, half), F32)
    c = jnp.concatenate([cos, cos, ones], axis=1)
    s1 = jnp.concatenate([-sin, zh, zeros], axis=1)
    s2 = jnp.concatenate([zh, sin, zeros], axis=1)
    two = lambda t: jnp.concatenate([t, t], axis=1)
    return two(c), two(s1), two(s2)


def _band_bias(qb):
    qi = jnp.arange(qb, dtype=I32)[:, None]
    c = jnp.arange(2 * qb, dtype=I32)[None, :]
    band = (c >= qi) & (c <= qi + qb)
    first = band & (c >= qb)
    one = jnp.where(jnp.stack([first, band]), 0.0, NEG_BIG).astype(F32)
    return jnp.concatenate([one, one], axis=2)


def _attn_prompt_kernel(sink_ref, q_ref, kc_ref, kp_ref, vc_ref, vp_ref, bias_ref, g_ref,
                        o_ref, kt_ref, vt_ref, s_ref, e_ref):
    j = pl.program_id(1)
    qb = ATTN_BLOCK
    kc = kc_ref[...]
    vc = vc_ref[...]
    k2 = jnp.concatenate([kp_ref[...], kc], axis=0)
    v2 = jnp.concatenate([vp_ref[...], vc], axis=0)
    k2r = pltpu.roll(k2, HEAD_DIM, 1)
    v2r = pltpu.roll(v2, HEAD_DIM, 1)
    lo_k = lax.broadcasted_iota(I32, (2 * qb, LANES), 1) < HEAD_DIM
    lo_q = lax.broadcasted_iota(I32, (qb, LANES), 1) < HEAD_DIM
    bias = bias_ref[0]
    nt = (((1,), (1,)), ((), ()))
    n_pairs = N_HEADS // 2
    kbd, vbd = [], []
    for kv in range(N_KV_HEADS):
        ka, kb = (k2, k2r) if kv == 0 else (k2r, k2)
        va, vb = (v2, v2r) if kv == 0 else (v2r, v2)
        kbd.append(jnp.concatenate([jnp.where(lo_k, ka, 0.0), jnp.where(lo_k, 0.0, kb)], axis=0).astype(BF16))
        vbd.append(jnp.concatenate([jnp.where(lo_k, va, 0.0), jnp.where(lo_k, 0.0, vb)], axis=0).astype(BF16))
    for pp in range(n_pairs):
        qp = (q_ref[:, pp * LANES:(pp + 1) * LANES] * (HEAD_DIM ** -0.5)).astype(BF16)
        s_ref[pp] = lax.dot_general(qp, kbd[pp // (GROUP // 2)], nt, preferred_element_type=F32) + bias
    invs = []
    for pp in range(n_pairs):
        inv = []
        for t in range(2):
            cols = slice(t * 2 * qb, (t + 1) * 2 * qb)
            st = s_ref[pp, :, cols]
            sink = sink_ref[2 * pp + t]
            m = jnp.maximum(jnp.max(st, axis=-1, keepdims=True), sink)
            e = jnp.exp(st - m)
            e_ref[pp, :, cols] = e.astype(BF16)
            inv.append(1.0 / (jnp.sum(e, axis=-1, keepdims=True) + jnp.exp(sink - m)))
        invs.append(jnp.where(lo_q, inv[0], inv[1]))
    outs = [jnp.dot(e_ref[pp], vbd[pp // (GROUP // 2)], preferred_element_type=F32) * invs[pp]
            for pp in range(n_pairs)]
    o_ref[...] = _rmsnorm(jnp.concatenate(outs, axis=1), g_ref[...])

    @pl.when(j == pl.num_programs(1) - 1)
    def _():
        kt_ref[0] = kc.T
        vt_ref[0] = vc.T


def _attn_prompt(q, k, v, sinks, g_attn, batch, seq):
    qb = ATTN_BLOCK
    nb = seq // qb
    cur = lambda b, j: (b * nb + j, 0)
    prev = lambda b, j: (b * nb + jnp.maximum(j - 1, 0), 0)
    fix = lambda b, j: (0, 0)
    per_b = lambda b, j: (b, 0, 0)
    return pl.pallas_call(
        _attn_prompt_kernel,
        grid=(batch, nb),
        in_specs=[pl.BlockSpec(memory_space=pltpu.SMEM),
                  pl.BlockSpec((qb, D_ATTN), cur),
                  pl.BlockSpec((qb, KV_W), cur), pl.BlockSpec((qb, KV_W), prev),
                  pl.BlockSpec((qb, KV_W), cur), pl.BlockSpec((qb, KV_W), prev),
                  pl.BlockSpec((1, qb, 4 * qb), lambda b, j: (jnp.minimum(j, 1), 0, 0)),
                  pl.BlockSpec((1, D_ATTN), fix)],
        out_specs=(pl.BlockSpec((qb, D_ATTN), cur), pl.BlockSpec((1, KV_W, qb), per_b),
                   pl.BlockSpec((1, KV_W, qb), per_b)),
        out_shape=(jax.ShapeDtypeStruct((batch * seq, D_ATTN), F32),
                   jax.ShapeDtypeStruct((batch, KV_W, qb), F32),
                   jax.ShapeDtypeStruct((batch, KV_W, qb), F32)),
        scratch_shapes=[pltpu.VMEM((N_HEADS // 2, qb, 4 * qb), F32),
                        pltpu.VMEM((N_HEADS // 2, qb, 4 * qb), BF16)],
        compiler_params=_cparams(2),
        name="attn_prompt",
    )(sinks, q, k, k, v, v, _band_bias(qb), g_attn)


def _attn_sample_kernel(sink_ref, q_ref, kn_ref, vn_ref, kt_ref, vt_ref, g_ref,
                        o_ref, nkt_ref, nvt_ref, acc_ref):
    bb = q_ref.shape[0]
    q = q_ref[...] * (HEAD_DIM ** -0.5)
    kn = kn_ref[...]
    vn = vn_ref[...]
    kt = kt_ref[...]
    vt = vt_ref[...]
    col = lax.broadcasted_iota(I32, (bb, bb * KV_W), 1)
    rowb = lax.broadcasted_iota(I32, (bb, bb * KV_W), 0)
    own_seq = (col >> (KV_W.bit_length() - 1)) == rowb
    half_hi = ((col >> (HEAD_DIM.bit_length() - 1)) & 1) == 1
    qbig = []
    for h in range(N_HEADS):
        kv = h // GROUP
        pair = q[:, (h // 2) * LANES:(h // 2 + 1) * LANES]
        if (h % 2) != kv:
            pair = pltpu.roll(pair, HEAD_DIM, 1)
        tiled = jnp.concatenate([pair] * bb, axis=1)
        keep = own_seq & (half_hi if kv == 1 else jnp.logical_not(half_hi))
        qbig.append(jnp.where(keep, tiled, 0.0))
    qbig = jnp.concatenate(qbig, axis=0)
    s = jnp.dot(qbig.astype(BF16), kt.astype(BF16), preferred_element_type=F32)
    qb16 = q.astype(BF16).astype(F32)
    kb16 = kn.astype(BF16).astype(F32)
    s_new, sink = [], []
    for h in range(N_HEADS):
        kv = h // GROUP
        s_new.append(jnp.sum(qb16[:, h * HEAD_DIM:(h + 1) * HEAD_DIM] * kb16[:, kv * HEAD_DIM:(kv + 1) * HEAD_DIM],
                             axis=-1, keepdims=True))
        sink.append(jnp.full((bb, 1), sink_ref[h], F32))
    s_new = jnp.concatenate(s_new, axis=0)
    sink = jnp.concatenate(sink, axis=0)
    m = jnp.maximum(jnp.maximum(jnp.max(s, axis=-1, keepdims=True), s_new), sink)
    e = jnp.exp(s - m)
    e_new = jnp.exp(s_new - m)
    inv = 1.0 / (jnp.sum(e, axis=-1, keepdims=True) + e_new + jnp.exp(sink - m))
    obig = lax.dot_general(e.astype(BF16), vt.astype(BF16), (((1,), (1,)), ((), ())),
                           preferred_element_type=F32)
    for h in range(N_HEADS):
        kv = h // GROUP
        blk = jnp.where(own_seq, obig[h * bb:(h + 1) * bb, :], 0.0)
        fold = blk[:, 0:KV_W]
        for t in range(1, bb):
            fold = fold + blk[:, t * KV_W:(t + 1) * KV_W]
        hs = slice(h * bb, (h + 1) * bb)
        ks = slice(kv * HEAD_DIM, (kv + 1) * HEAD_DIM)
        acc_ref[:, h * HEAD_DIM:(h + 1) * HEAD_DIM] = (fold[:, ks] + e_new[hs] * vn[:, ks]) * inv[hs]
    o_ref[...] = _rmsnorm(acc_ref[...], g_ref[...])

    last = lax.broadcasted_iota(I32, (KV_W, WINDOW), 1) == WINDOW - 1
    for b in range(bb):
        rs = slice(b * KV_W, (b + 1) * KV_W)
        kcol = jnp.broadcast_to(kn[b:b + 1, :], (KV_W, KV_W)).T
        vcol = jnp.broadcast_to(vn[b:b + 1, :], (KV_W, KV_W)).T
        nkt_ref[rs, :] = jnp.where(last, kcol, pltpu.roll(kt[rs, :], WINDOW - 1, 1))
        nvt_ref[rs, :] = jnp.where(last, vcol, pltpu.roll(vt[rs, :], WINDOW - 1, 1))


def _attn_sample(q, kn, vn, kt2d, vt2d, sinks, g_attn, bb):
    n = q.shape[0]
    row = lambda i: (i, 0)
    fix = lambda i: (0, 0)
    cache = pl.BlockSpec((bb * KV_W, WINDOW), row)
    return pl.pallas_call(
        _attn_sample_kernel,
        grid=(n // bb,),
        in_specs=[pl.BlockSpec(memory_space=pltpu.SMEM),
                  pl.BlockSpec((bb, D_ATTN), row), pl.BlockSpec((bb, KV_W), row),
                  pl.BlockSpec((bb, KV_W), row), cache, cache,
                  pl.BlockSpec((1, D_ATTN), fix)],
        out_specs=(pl.BlockSpec((bb, D_ATTN), row), cache, cache),
        out_shape=(jax.ShapeDtypeStruct((n, D_ATTN), F32),
                   jax.ShapeDtypeStruct(kt2d.shape, F32), jax.ShapeDtypeStruct(vt2d.shape, F32)),
        scratch_shapes=[pltpu.VMEM((bb, D_ATTN), F32)],
        compiler_params=_cparams(1),
        name="attn_sample",
    )(sinks, q, kn, vn, kt2d, vt2d, g_attn)


def _softplus(z):
    return jnp.maximum(z, 0.0) + jnp.log1p(jnp.exp(-jnp.abs(z)))


def _lru_gates(xc, wa_ref, ba_ref, wx_ref, bx_ref, lam_ref):
    xb = xc.astype(BF16)
    r = jax.nn.sigmoid(jnp.dot(xb, wa_ref[...], preferred_element_type=F32) + ba_ref[...])
    i = jax.nn.sigmoid(jnp.dot(xb, wx_ref[...], preferred_element_type=F32) + bx_ref[...])
    log_a = (-LRU_C * r) * _softplus(-lam_ref[...])
    a = jnp.exp(log_a)
    z = -jnp.tanh(log_a) * (a * a + 1.0)
    u = jnp.where(z > 0.0, z * lax.rsqrt(z), 0.0) * (i * xc)
    return a, u


def _lru_scan(a, u, h0):
    ng = a.shape[0] // SUBLANES
    a3 = a.reshape(ng, SUBLANES, D_RNN)
    u3 = u.reshape(ng, SUBLANES, D_RNN)
    t8 = lax.broadcasted_iota(I32, (ng, SUBLANES, D_RNN), 1)
    d = 1
    while d < SUBLANES:
        a_s = jnp.where(t8 >= d, pltpu.roll(a3, d, 1), 1.0)
        u_s = jnp.where(t8 >= d, pltpu.roll(u3, d, 1), 0.0)
        u3 = a3 * u_s + u3
        a3 = a3 * a_s
        d *= 2
    carry = h0
    groups = []
    for g in range(ng):
        hg = a3[g] * carry + u3[g]
        groups.append(hg)
        carry = hg[SUBLANES - 1:SUBLANES, :]
    return jnp.concatenate(groups, axis=0), carry


def _lru_scan_tiles(a_ref, u_ref, h_ref, h0):
    nl, rows, _ = a_ref.shape
    ng = rows // SUBLANES
    step = lambda ref, s: jnp.concatenate(
        [ref[j, pl.ds(s, ng, stride=SUBLANES), :] for j in range(nl)], axis=1)
    prods = [step(a_ref, 0)]
    locs = [step(u_ref, 0)]
    for s in range(1, SUBLANES):
        a_s = step(a_ref, s)
        locs.append(a_s * locs[-1] + step(u_ref, s))
        prods.append(a_s * prods[-1])
    after, h_last = _lru_scan(prods[-1], locs[-1], h0)
    row = lax.broadcasted_iota(I32, (ng, D_RNN), 0)
    before = jnp.where(row == 0, h0, pltpu.roll(after, 1, 0))
    for s in range(SUBLANES):
        h_s = locs[s] + prods[s] * before
        for j in range(nl):
            h_ref[j, pl.ds(s, ng, stride=SUBLANES), :] = h_s[:, j * LANES:(j + 1) * LANES]
    return h_last


def _to_lane_tiles(ref, x):
    for j in range(ref.shape[0]):
        ref[j] = x[:, j * LANES:(j + 1) * LANES]


def _rnn_prompt_kernel(xr_ref, yr_ref, cw_ref, cb_ref, wa_ref, ba_ref, wx_ref, bx_ref, lam_ref, g_ref,
                       o_ref, hl_ref, ext_ref, h_ref):
    c = pl.program_id(1)
    tc = xr_ref.shape[0]
    pad = SUBLANES

    @pl.when(c == 0)
    def _():
        ext_ref[0:pad, :] = jnp.zeros((pad, D_RNN), F32)
        h_ref[...] = jnp.zeros((1, D_RNN), F32)

    ext_ref[pad:pad + tc, :] = xr_ref[...]
    cw = cw_ref[...]
    xc = cb_ref[...] + ext_ref[pad:pad + tc, :] * cw[CONV_WIDTH - 1:CONV_WIDTH, :]
    for w in range(CONV_WIDTH - 1):
        sh = CONV_WIDTH - 1 - w
        xc = xc + ext_ref[pad - sh:pad - sh + tc, :] * cw[w:w + 1, :]
    ext_ref[0:pad, :] = ext_ref[tc:tc + pad, :]

    a, u = _lru_gates(xc, wa_ref, ba_ref, wx_ref, bx_ref, lam_ref)
    h, carry = _lru_scan(a, u, h_ref[...])
    h_ref[...] = carry
    hl_ref[0] = carry
    o_ref[...] = _rmsnorm(jax.nn.gelu(yr_ref[...]) * h, g_ref[...])


def _rnn_prompt(xr, yr, cw, cb, wa, ba, wx, bx, lam, g, batch, seq, tc):
    nc = seq // tc
    cur = lambda b, c: (b * nc + c, 0)
    fix = lambda b, c: (0, 0)
    vec = pl.BlockSpec((1, D_RNN), fix)
    return pl.pallas_call(
        _rnn_prompt_kernel,
        grid=(batch, nc),
        in_specs=[pl.BlockSpec((tc, D_RNN), cur), pl.BlockSpec((tc, D_RNN), cur),
                  pl.BlockSpec((CONV_WIDTH, D_RNN), fix), vec,
                  pl.BlockSpec((D_RNN, D_RNN), fix), vec,
                  pl.BlockSpec((D_RNN, D_RNN), fix), vec, vec, vec],
        out_specs=(pl.BlockSpec((tc, D_RNN), cur), pl.BlockSpec((1, 1, D_RNN), lambda b, c: (b, 0, 0))),
        out_shape=(jax.ShapeDtypeStruct((batch * seq, D_RNN), F32),
                   jax.ShapeDtypeStruct((batch, 1, D_RNN), F32)),
        scratch_shapes=[pltpu.VMEM((tc + SUBLANES, D_RNN), F32), pltpu.VMEM((1, D_RNN), F32)],
        compiler_params=_cparams(2),
        name="rnn_prompt",
    )(xr, yr, cw, cb, wa, ba, wx, bx, lam, g)


def _rnn_sample_kernel(xr_ref, yr_ref, hist_ref, h0_ref, cw_ref, cb_ref, wa_ref, ba_ref, wx_ref, bx_ref,
                       lam_ref, g_ref, o_ref, hl_ref, nh_ref):
    cw = cw_ref[...]
    xr = xr_ref[...]
    xc = cb_ref[...] + xr * cw[CONV_WIDTH - 1:CONV_WIDTH, :]
    for w in range(CONV_WIDTH - 1):
        xc = xc + hist_ref[w] * cw[w:w + 1, :]
    a, u = _lru_gates(xc, wa_ref, ba_ref, wx_ref, bx_ref, lam_ref)
    h = a * h0_ref[...] + u
    hl_ref[...] = h
    o_ref[...] = _rmsnorm(jax.nn.gelu(yr_ref[...]) * h, g_ref[...])
    for w in range(CONV_WIDTH - 2):
        nh_ref[w] = hist_ref[w + 1]
    nh_ref[CONV_WIDTH - 2] = xr


def _rnn_sample(xr, yr, hist, h0, cw, cb, wa, ba, wx, bx, lam, g):
    n = xr.shape[0]
    full = lambda a: pl.BlockSpec(a.shape, lambda: (0,) * a.ndim)
    args = (xr, yr, hist, h0, cw, cb, wa, ba, wx, bx, lam, g)
    return pl.pallas_call(
        _rnn_sample_kernel,
        in_specs=[full(a) for a in args],
        out_specs=(pl.BlockSpec((n, D_RNN), lambda: (0, 0)), pl.BlockSpec((n, D_RNN), lambda: (0, 0)),
                   pl.BlockSpec(hist.shape, lambda: (0, 0, 0))),
        out_shape=(jax.ShapeDtypeStruct((n, D_RNN), F32), jax.ShapeDtypeStruct((n, D_RNN), F32),
                   jax.ShapeDtypeStruct(hist.shape, F32)),
        compiler_params=pltpu.CompilerParams(vmem_limit_bytes=VMEM_LIMIT),
        name="rnn_sample",
    )(*args)


def _front_kernel(*refs, tiles_per_seq):
    i = pl.program_id(0)
    q_s, k_s, v_s, xr_s, yr_s, ext_ref, h_ref = refs[25:32]

    @pl.when(i == 0)
    def _():
        for r in (q_s, k_s, v_s, xr_s, yr_s, ext_ref, h_ref):
            r[...] = jnp.zeros(r.shape, F32)

    for cur in range(2):
        @pl.when(i % 2 == cur)
        def _():
            _front_body(cur, 1 - cur, *refs, tiles_per_seq=tiles_per_seq)


def _front_body(cur, prv, sink_ref, x_ref, gm_ref, w_ref, gq_ref, gk_ref, c_ref, s1_ref, s2_ref, bias_ref,
                ga_ref, cw_ref, cb_ref, wa_ref, ba_ref, wx_ref, bx_ref, lam_ref, gr_ref,
                an_ref, rn_ref, kt_ref, vt_ref, hl_ref, cx_ref,
                q_s, k_s, v_s, xr_s, yr_s, ext_ref, h_ref, s_ref, e_ref, a_scr, u_scr, hs_scr,
                *, tiles_per_seq):
    i = pl.program_id(0)
    tm = x_ref.shape[0]
    qb = ATTN_BLOCK
    t = jnp.maximum(i - 1, 0)
    first_tile = (t % tiles_per_seq) == 0

    nqb = tm // qb
    n_pairs = N_HEADS // 2
    lo_k = lax.broadcasted_iota(I32, (2 * qb, LANES), 1) < HEAD_DIM
    lo_q = lax.broadcasted_iota(I32, (qb, LANES), 1) < HEAD_DIM
    nt_dims = (((1,), (1,)), ((), ()))
    vbds = []
    for jb in range(nqb):
        k2 = k_s[prv, jb * qb:(jb + 2) * qb, :]
        v2 = v_s[prv, jb * qb:(jb + 2) * qb, :]
        k2r = pltpu.roll(k2, HEAD_DIM, 1)
        v2r = pltpu.roll(v2, HEAD_DIM, 1)
        bias = bias_ref[jnp.where(first_tile, 0, 1)] if jb == 0 else bias_ref[1]
        for kv in range(N_KV_HEADS):
            ka, kb = (k2, k2r) if kv == 0 else (k2r, k2)
            va, vb = (v2, v2r) if kv == 0 else (v2r, v2)
            kbd = jnp.concatenate([jnp.where(lo_k, ka, 0.0), jnp.where(lo_k, 0.0, kb)], axis=0).astype(BF16)
            vbds.append(jnp.concatenate([jnp.where(lo_k, va, 0.0), jnp.where(lo_k, 0.0, vb)],
                                        axis=0).astype(BF16))
            for p in range(GROUP // 2):
                pp = kv * (GROUP // 2) + p
                qp = (q_s[prv, jb * qb:(jb + 1) * qb, pp * LANES:(pp + 1) * LANES]
                      * (HEAD_DIM ** -0.5)).astype(BF16)
                s_ref[jb * n_pairs + pp] = lax.dot_general(qp, kbd, nt_dims,
                                                           preferred_element_type=F32) + bias
    invs = []
    for c in range(nqb * n_pairs):
        pp = c % n_pairs
        inv = []
        for tpos in range(2):
            cols = slice(tpos * 2 * qb, (tpos + 1) * 2 * qb)
            st = s_ref[c, :, cols]
            sink = sink_ref[2 * pp + tpos]
            m = jnp.maximum(jnp.max(st, axis=-1, keepdims=True), sink)
            e = jnp.exp(st - m)
            e_ref[c, :, cols] = e.astype(BF16)
            inv.append(1.0 / (jnp.sum(e, axis=-1, keepdims=True) + jnp.exp(sink - m)))
        invs.append(jnp.where(lo_q, inv[0], inv[1]))
    for jb in range(nqb):
        outs = [jnp.dot(e_ref[jb * n_pairs + pp], vbds[jb * N_KV_HEADS + pp // (GROUP // 2)],
                        preferred_element_type=F32) * invs[jb * n_pairs + pp] for pp in range(n_pairs)]
        an_ref[jb * qb:(jb + 1) * qb, :] = _rmsnorm(jnp.concatenate(outs, axis=1), ga_ref[...])
    kt_ref[0] = k_s[prv, tm:tm + qb, :].T
    vt_ref[0] = v_s[prv, tm:tm + qb, :].T

    pad = SUBLANES
    xr = xr_s[prv]
    ext_ref[0:pad, :] = jnp.where(first_tile, 0.0, ext_ref[0:pad, :])
    ext_ref[pad:pad + tm, :] = xr
    cw = cw_ref[...]
    xc = cb_ref[...] + xr * cw[CONV_WIDTH - 1:CONV_WIDTH, :]
    for w in range(CONV_WIDTH - 1):
        sh = CONV_WIDTH - 1 - w
        xc = xc + ext_ref[pad - sh:pad - sh + tm, :] * cw[w:w + 1, :]
    ext_ref[0:pad, :] = xr[tm - pad:tm, :]
    cx_ref[0] = xr[tm - pad:tm, :]
    a, u = _lru_gates(xc, wa_ref, ba_ref, wx_ref, bx_ref, lam_ref)
    _to_lane_tiles(a_scr, a)
    _to_lane_tiles(u_scr, u)
    carry = _lru_scan_tiles(a_scr, u_scr, hs_scr, jnp.where(first_tile, 0.0, h_ref[...]))
    h_ref[...] = carry
    hl_ref[0] = carry
    hseq = jnp.concatenate([hs_scr[j] for j in range(hs_scr.shape[0])], axis=1)
    rn_ref[...] = _rmsnorm(jax.nn.gelu(yr_s[prv]) * hseq, gr_ref[...])

    h = _rmsnorm(x_ref[...], gm_ref[...])
    proj = jnp.dot(h.astype(BF16), w_ref[...], preferred_element_type=F32)
    lo = lax.broadcasted_iota(I32, (tm, LANES), 1) < HEAD_DIM
    rope = (c_ref[...], s1_ref[...], s2_ref[...], lo)
    gq = gq_ref[...]
    for j in range(D_ATTN // LANES):
        q_s[cur, :, j * LANES:(j + 1) * LANES] = _head_norm_rope(proj[:, j * LANES:(j + 1) * LANES], gq, *rope)
    k_s[cur, qb:qb + tm, :] = _head_norm_rope(proj[:, D_ATTN:D_ATTN + KV_W], gk_ref[...], *rope)
    v_s[cur, qb:qb + tm, :] = proj[:, D_ATTN + KV_W:D_ATTN + 2 * KV_W]
    k_s[cur, 0:qb, :] = k_s[prv, tm:tm + qb, :]
    v_s[cur, 0:qb, :] = v_s[prv, tm:tm + qb, :]
    o = D_ATTN + 2 * KV_W
    xr_s[cur] = proj[:, o:o + D_RNN]
    yr_s[cur] = proj[:, o + D_RNN:o + 2 * D_RNN]


def _front(x2d, sinks, g_mix, w_bf, gq2, gk2, ctab, s1tab, s2tab, g_attn,
           cw, cb, wa, ba, wx, bx, lam, g_rnn, batch, seq, tm):
    n = x2d.shape[0]
    nt = n // tm
    tps = seq // tm
    qb = ATTN_BLOCK
    cur = lambda i: (jnp.minimum(i, nt - 1), 0)
    tab = lambda i: (jnp.minimum(i, nt - 1) % tps, 0)
    fix = lambda i: (0, 0)
    prev = lambda i: (jnp.maximum(i - 1, 0), 0)
    per_seq = lambda i: (jnp.maximum(i - 1, 0) // tps, 0, 0)
    vec = lambda w: pl.BlockSpec((1, w), fix)
    return pl.pallas_call(
        functools.partial(_front_kernel, tiles_per_seq=tps),
        grid=(nt + 1,),
        in_specs=[pl.BlockSpec(memory_space=pltpu.SMEM),
                  pl.BlockSpec((tm, D_MODEL), cur), vec(D_MODEL), pl.BlockSpec((D_MODEL, D_IN), fix),
                  vec(LANES), vec(LANES),
                  pl.BlockSpec((tm, LANES), tab), pl.BlockSpec((tm, LANES), tab), pl.BlockSpec((tm, LANES), tab),
                  pl.BlockSpec((2, qb, 4 * qb), lambda i: (0, 0, 0)), vec(D_ATTN),
                  pl.BlockSpec((CONV_WIDTH, D_RNN), fix), vec(D_RNN),
                  pl.BlockSpec((D_RNN, D_RNN), fix), vec(D_RNN),
                  pl.BlockSpec((D_RNN, D_RNN), fix), vec(D_RNN), vec(D_RNN), vec(D_RNN)],
        out_specs=(pl.BlockSpec((tm, D_ATTN), prev), pl.BlockSpec((tm, D_RNN), prev),
                   pl.BlockSpec((1, KV_W, qb), per_seq), pl.BlockSpec((1, KV_W, qb), per_seq),
                   pl.BlockSpec((1, 1, D_RNN), per_seq), pl.BlockSpec((1, SUBLANES, D_RNN), per_seq)),
        out_shape=(jax.ShapeDtypeStruct((n, D_ATTN), F32), jax.ShapeDtypeStruct((n, D_RNN), F32),
                   jax.ShapeDtypeStruct((batch, KV_W, qb), F32), jax.ShapeDtypeStruct((batch, KV_W, qb), F32),
                   jax.ShapeDtypeStruct((batch, 1, D_RNN), F32),
                   jax.ShapeDtypeStruct((batch, SUBLANES, D_RNN), F32)),
        scratch_shapes=[pltpu.VMEM((2, tm, D_ATTN), F32),
                        pltpu.VMEM((2, tm + qb, KV_W), F32), pltpu.VMEM((2, tm + qb, KV_W), F32),
                        pltpu.VMEM((2, tm, D_RNN), F32), pltpu.VMEM((2, tm, D_RNN), F32),
                        pltpu.VMEM((tm + SUBLANES, D_RNN), F32), pltpu.VMEM((1, D_RNN), F32),
                        pltpu.VMEM((tm // qb * (N_HEADS // 2), qb, 4 * qb), F32),
                        pltpu.VMEM((tm // qb * (N_HEADS // 2), qb, 4 * qb), BF16),
                        pltpu.VMEM((D_RNN // LANES, tm, LANES), F32), pltpu.VMEM((D_RNN // LANES, tm, LANES), F32),
                        pltpu.VMEM((D_RNN // LANES, tm, LANES), F32)],
        compiler_params=_cparams(1),
        name="front",
    )(sinks, x2d, g_mix, w_bf, gq2, gk2, ctab, s1tab, s2tab, _band_bias(qb), g_attn,
      cw, cb, wa, ba, wx, bx, lam, g_rnn)


def _mix_route_kernel(x_ref, an_ref, rn_ref, woa_ref, wor_ref, g_ref, wr2_ref, br_ref, tri_ref, low_ref,
                      x2_ref, ts_ref, dest_ref, gate_ref, n8_ref, off_ref):
    tt = x_ref.shape[0]
    tile_rows = ts_ref.shape[0]
    x2 = x_ref[...] + jnp.dot(an_ref[...].astype(BF16), woa_ref[...], preferred_element_type=F32) \
        + jnp.dot(rn_ref[...].astype(BF16), wor_ref[...], preferred_element_type=F32)
    x2_ref[...] = x2
    hn = _rmsnorm(x2, g_ref[...])

    nt = (((1,), (1,)), ((), ()))
    hb = hn.astype(BF16)
    hmid = (hn - hb.astype(F32)).astype(BF16)
    wr2 = wr2_ref[...]
    both = lax.dot_general(wr2, hb, nt, preferred_element_type=F32)
    logits = (lax.dot_general(wr2[:N_EXPERTS], hmid, nt, preferred_element_type=F32)
              + both[N_EXPERTS:]) + both[:N_EXPERTS] + br_ref[...]

    ie = lax.broadcasted_iota(I32, (N_EXPERTS, tt), 0).astype(F32)
    l = logits
    vals, sels = [], []
    for _ in range(TOP_K):
        m = jnp.max(l, axis=0, keepdims=True)
        idx = jnp.min(jnp.where(l == m, ie, float(N_EXPERTS)), axis=0, keepdims=True)
        sel = ie == idx
        vals.append(m)
        sels.append(sel)
        l = jnp.where(sel, NEG_BIG, l)
    es = [jnp.exp(v - vals[0]) for v in vals]
    den = es[0] + es[1] + es[2] + es[3]
    gate_ref[0] = jnp.concatenate([e / den for e in es], axis=0)

    oh = jnp.zeros((N_EXPERTS, tt), F32)
    for sel in sels:
        oh = oh + jnp.where(sel, 1.0, 0.0)
    before = jnp.dot(oh.astype(BF16), tri_ref[...], preferred_element_type=F32)
    cnt = jnp.sum(oh, axis=1, keepdims=True).astype(I32)
    n8 = ((cnt + (SUBLANES - 1)) >> 3) << 3
    n8b = jnp.broadcast_to(n8, (N_EXPERTS, LANES))
    off = jnp.dot(low_ref[...], n8b.astype(F32).astype(BF16), preferred_element_type=F32)
    n8_ref[0] = n8b
    off_ref[0] = off.astype(I32)
    base = off[:, 0:1] + before
    dests = [jnp.sum(jnp.where(sel, base, 0.0), axis=0, keepdims=True).astype(I32) for sel in sels]
    dest_ref[0] = jnp.concatenate(dests, axis=0)

    ri = lax.broadcasted_iota(I32, (DISPATCH_CHUNK, tt), 0).astype(I16)
    d16 = [d.astype(I16) for d in dests]
    one = jnp.ones((DISPATCH_CHUNK, tt), BF16)
    for c in range(tile_rows // DISPATCH_CHUNK):
        p = jnp.zeros((DISPATCH_CHUNK, tt), BF16)
        for d in d16:
            p = jnp.where(ri == d - jnp.int16(c * DISPATCH_CHUNK), one, p)
        ts_ref[c * DISPATCH_CHUNK:(c + 1) * DISPATCH_CHUNK, :] = jnp.dot(
            p, hb, preferred_element_type=F32)


def _tile_rows(tt):
    return -(-(TOP_K * tt + N_EXPERTS * (SUBLANES - 1)) // DISPATCH_CHUNK) * DISPATCH_CHUNK


def _mix_route(x2d, an, rn, woa, wor, g, wrt, br, low, tt):
    n = x2d.shape[0]
    nt = n // tt
    tile_rows = _tile_rows(tt)
    tri = jnp.triu(jnp.ones((tt, tt), BF16), k=1)
    row = lambda i: (i, 0)
    fix = lambda i: (0, 0)
    t3 = lambda i: (i, 0, 0)
    in_specs = [pl.BlockSpec((tt, D_MODEL), row), pl.BlockSpec((tt, D_ATTN), row),
                pl.BlockSpec((tt, D_RNN), row), pl.BlockSpec((D_ATTN, D_MODEL), fix),
                pl.BlockSpec((D_RNN, D_MODEL), fix), pl.BlockSpec((1, D_MODEL), fix),
                pl.BlockSpec((2 * N_EXPERTS, D_MODEL), fix), pl.BlockSpec((N_EXPERTS, 1), fix),
                pl.BlockSpec((tt, tt), fix), pl.BlockSpec((N_EXPERTS, N_EXPERTS), fix)]
    out_shape = (jax.ShapeDtypeStruct((n, D_MODEL), F32),
                 jax.ShapeDtypeStruct((nt * tile_rows, D_MODEL), F32),
                 jax.ShapeDtypeStruct((nt, TOP_K, tt), I32),
                 jax.ShapeDtypeStruct((nt, TOP_K, tt), F32),
                 jax.ShapeDtypeStruct((nt, N_EXPERTS, LANES), I32),
                 jax.ShapeDtypeStruct((nt, N_EXPERTS, LANES), I32))
    out_specs = (pl.BlockSpec((tt, D_MODEL), row),
                 pl.BlockSpec((tile_rows, D_MODEL), row),
                 pl.BlockSpec((1, TOP_K, tt), t3), pl.BlockSpec((1, TOP_K, tt), t3),
                 pl.BlockSpec((1, N_EXPERTS, LANES), t3), pl.BlockSpec((1, N_EXPERTS, LANES), t3))
    return pl.pallas_call(
        _mix_route_kernel,
        grid=(nt,),
        in_specs=in_specs,
        out_specs=out_specs,
        out_shape=out_shape,
        compiler_params=_cparams(1),
        name="mix_route",
    )(x2d, an, rn, woa, wor, g, wrt, br, tri, low)


LOW_BITS = 4


def _start_piece(src_hbm, dst_buf, sem, s, d, l8, nbits):
    def bit_copy(c):
        size = SUBLANES << c
        low = (l8 & ((1 << c) - 1)) * SUBLANES

        @pl.when(((l8 >> c) & 1) == 1)
        def _():
            pltpu.make_async_copy(
                src_hbm.at[pl.ds(pl.multiple_of(s + low, SUBLANES), size)],
                dst_buf.at[pl.ds(pl.multiple_of(d + low, SUBLANES), size)], sem).start()

    for c in range(min(LOW_BITS, nbits)):
        bit_copy(c)
    if nbits > LOW_BITS:
        def long_copies(_, carry):
            for c in range(LOW_BITS, nbits):
                bit_copy(c)
            return carry

        lax.fori_loop(0, jnp.where(l8 >= (1 << LOW_BITS), 1, 0), long_copies, 0)


def _start_pieces(src_hbm, dst_buf, sem, p_lo, p_hi, psrc_ref, pdst_ref, plen_ref, nbits):
    def body(p, carry):
        _start_piece(src_hbm, dst_buf, sem, psrc_ref[p], pdst_ref[p], plen_ref[p], nbits)
        return carry

    lax.fori_loop(p_lo, p_hi, body, 0)


def _wait_rows(src_hbm, dst_buf, sem, rows8, nbits):
    for c in range(nbits):
        size = SUBLANES << c

        @pl.when(((rows8 >> c) & 1) == 1)
        def _():
            pltpu.make_async_copy(src_hbm.at[pl.ds(0, size)], dst_buf.at[pl.ds(0, size)], sem).wait()


def _moe_gmm_kernel(be_ref, rows_ref, wslot_ref, nxt_ref,
                    psa_ref, pea_ref, srca_ref, dsta_ref, lena_ref, hsrca_ref, hlena_ref,
                    psb_ref, peb_ref, srcb_ref, dstb_ref, lenb_ref, hsrcb_ref, hlenb_ref,
                    tsa_hbm, tsb_hbm, wgu_hbm, wdn_hbm, bg_ref, bu_ref, bd_ref, perm_ref,
                    ys_ref, lhs_ref, wgu_buf, wdn_buf, wg_ref, wu_ref, wd_ref, sem_ref, wsem_ref,
                    *, nbits_a, nbits_b):
    j = pl.program_id(0)
    nb = pl.num_programs(0)
    slot = j % 2
    bm = lhs_ref.shape[1]

    def gather(blk, sl):
        for ts_hbm, ps, pe, src, dst, ln, hsrc, hlen, nbits in (
                (tsa_hbm, psa_ref, pea_ref, srca_ref, dsta_ref, lena_ref, hsrca_ref, hlena_ref, nbits_a),
                (tsb_hbm, psb_ref, peb_ref, srcb_ref, dstb_ref, lenb_ref, hsrcb_ref, hlenb_ref, nbits_b)):
            _start_pieces(ts_hbm, lhs_ref.at[sl], sem_ref.at[sl], ps[blk], pe[blk], src, dst, ln, nbits)
            _start_piece(ts_hbm, lhs_ref.at[sl], sem_ref.at[sl], hsrc[blk], 0, hlen[blk], nbits)

    def weight_copies(e, ws):
        return (pltpu.make_async_copy(wgu_hbm.at[e], wgu_buf.at[ws], wsem_ref.at[ws]),
                pltpu.make_async_copy(wdn_hbm.at[e], wdn_buf.at[ws], wsem_ref.at[ws]))

    @pl.when(j == 0)
    def _():
        lhs_ref[...] = jnp.zeros(lhs_ref.shape, F32)
        gather(0, 0)
        for cp in weight_copies(be_ref[0], wslot_ref[0]):
            cp.start()

    @pl.when(j + 1 < nb)
    def _():
        gather(j + 1, 1 - slot)

    @pl.when(jnp.logical_or(j == 0, be_ref[j] != be_ref[jnp.maximum(j - 1, 0)]))
    def _():
        ws = wslot_ref[j]
        for cp in weight_copies(be_ref[j], ws):
            cp.wait()
        nxt = nxt_ref[j]

        @pl.when(nxt >= 0)
        def _():
            for cp in weight_copies(nxt, 1 - ws):
                cp.start()

        perm = perm_ref[...]
        half = PERM_COLS // 2
        for c in range(2 * D_FF // PERM_COLS):
            wb = wgu_buf[ws, :, c * PERM_COLS:(c + 1) * PERM_COLS].astype(BF16)
            wp = jnp.dot(wb, perm, preferred_element_type=F32).astype(BF16)
            wg_ref[:, c * half:(c + 1) * half] = wp[:, :half]
            wu_ref[:, c * half:(c + 1) * half] = wp[:, half:]
        wd_ref[...] = wdn_buf[ws].astype(BF16)

    rows8 = rows_ref[j]
    _wait_rows(tsa_hbm, lhs_ref.at[slot], sem_ref.at[slot], rows8, (bm // SUBLANES).bit_length())

    def expert_rows(rs):
        x = lhs_ref[slot, rs, :].astype(BF16)
        gate = jnp.dot(x, wg_ref[...], preferred_element_type=F32) + bg_ref[0]
        up = jnp.dot(x, wu_ref[...], preferred_element_type=F32) + bu_ref[0]
        gate = jnp.minimum(gate, SWIGLU_LIMIT)
        up = jnp.clip(up, -SWIGLU_LIMIT, SWIGLU_LIMIT)
        act = (up + 1.0) * (gate * jax.nn.sigmoid(SWIGLU_ALPHA * gate))
        ys_ref[rs, :] = jnp.dot(act.astype(BF16), wd_ref[...], preferred_element_type=F32) + bd_ref[0]

    half8 = bm // 2 // SUBLANES

    @pl.when(rows8 > half8)
    def _():
        expert_rows(slice(0, bm))

    @pl.when(jnp.logical_and(rows8 > 0, rows8 <= half8))
    def _():
        expert_rows(slice(0, bm // 2))
        ys_ref[bm // 2:, :] = jnp.zeros((bm // 2, D_MODEL), F32)

    @pl.when(rows8 == 0)
    def _():
        ys_ref[...] = jnp.zeros(ys_ref.shape, F32)


def _moe_gmm(blocks, tabs_a, tabs_b, ts_a, ts_b, w_gu, w_dn, bg, bu, bd, perm, nblocks, bm, nbits_a, nbits_b):
    we = lambda j, be, *_: (be[j], 0, 0)
    grid_spec = pltpu.PrefetchScalarGridSpec(
        num_scalar_prefetch=18,
        grid=(nblocks,),
        in_specs=[pl.BlockSpec(memory_space=pl.ANY), pl.BlockSpec(memory_space=pl.ANY),
                  pl.BlockSpec(memory_space=pl.ANY), pl.BlockSpec(memory_space=pl.ANY),
                  pl.BlockSpec((1, 1, D_FF), we), pl.BlockSpec((1, 1, D_FF), we),
                  pl.BlockSpec((1, 1, D_MODEL), we),
                  pl.BlockSpec((PERM_COLS, PERM_COLS), lambda j, *_: (0, 0))],
        out_specs=pl.BlockSpec((bm, D_MODEL), lambda j, *_: (j, 0)),
        scratch_shapes=[pltpu.VMEM((2, bm, D_MODEL), F32),
                        pltpu.VMEM((2, D_MODEL, 2 * D_FF), F32), pltpu.VMEM((2, D_FF, D_MODEL), F32),
                        pltpu.VMEM((D_MODEL, D_FF), BF16), pltpu.VMEM((D_MODEL, D_FF), BF16),
                        pltpu.VMEM((D_FF, D_MODEL), BF16),
                        pltpu.SemaphoreType.DMA((2,)), pltpu.SemaphoreType.DMA((2,))],
    )
    return pl.pallas_call(
        functools.partial(_moe_gmm_kernel, nbits_a=nbits_a, nbits_b=nbits_b),
        grid_spec=grid_spec,
        out_shape=jax.ShapeDtypeStruct((nblocks * bm, D_MODEL), F32),
        compiler_params=_cparams(1),
        name="moe_gmm",
    )(*blocks, *tabs_a, *tabs_b, ts_a, ts_b, w_gu, w_dn, bg, bu, bd, perm)


def _combine_kernel(psrc_ref, pdst_ref, plen_ref, tlo_ref, thi_ref, tsrc_ref, tdst_ref, tlen_ref, rows_ref,
                    ys_hbm, x2_ref, dest_ref, gate_ref, o_ref,
                    buf_ref, db_ref, gb_ref, sem_ref, *, nbits):
    i = pl.program_id(0)
    n = pl.num_programs(0)
    slot = i % 2
    tt = x2_ref.shape[0]
    tile_rows = buf_ref.shape[1]

    def gather(tile, sl):
        _start_pieces(ys_hbm, buf_ref.at[sl], sem_ref.at[sl], tile * N_EXPERTS, (tile + 1) * N_EXPERTS,
                      psrc_ref, pdst_ref, plen_ref, nbits)
        _start_pieces(ys_hbm, buf_ref.at[sl], sem_ref.at[sl], tlo_ref[tile], thi_ref[tile],
                      tsrc_ref, tdst_ref, tlen_ref, nbits)

    @pl.when(i == 0)
    def _():
        buf_ref[...] = jnp.zeros(buf_ref.shape, F32)
        gather(0, 0)

    @pl.when(i + 1 < n)
    def _():
        gather(i + 1, 1 - slot)

    _wait_rows(ys_hbm, buf_ref.at[slot], sem_ref.at[slot], rows_ref[i], (tile_rows // SUBLANES).bit_length())

    dest = dest_ref[0]
    gate = gate_ref[0]
    for k in range(TOP_K):
        db_ref[k] = jnp.broadcast_to(dest[:, k:k + 1], (tt, DISPATCH_CHUNK)).astype(I16)
        gb_ref[k] = jnp.broadcast_to(gate[:, k:k + 1], (tt, DISPATCH_CHUNK)).astype(BF16)
    li = lax.broadcasted_iota(I32, (tt, DISPATCH_CHUNK), 1).astype(I16)
    gms = []
    for c in range(tile_rows // DISPATCH_CHUNK):
        lic = li + jnp.int16(c * DISPATCH_CHUNK)
        gm = jnp.zeros((tt, DISPATCH_CHUNK), BF16)
        for k in range(TOP_K):
            gm = jnp.where(lic == db_ref[k], gb_ref[k], gm)
        gms.append(gm)
    o_ref[...] = x2_ref[...] + jnp.dot(jnp.concatenate(gms, axis=1), buf_ref[slot].astype(BF16),
                                       preferred_element_type=F32)


def _combine(tabs, ys, x2, dest, gate, tt, nbits):
    n = x2.shape[0]
    nt = n // tt
    tile_rows = _tile_rows(tt)
    grid_spec = pltpu.PrefetchScalarGridSpec(
        num_scalar_prefetch=len(tabs),
        grid=(nt,),
        in_specs=[pl.BlockSpec(memory_space=pl.ANY),
                  pl.BlockSpec((tt, D_MODEL), lambda i, *_: (i, 0)),
                  pl.BlockSpec((1, tt, TOP_K), lambda i, *_: (i, 0, 0)),
                  pl.BlockSpec((1, tt, TOP_K), lambda i, *_: (i, 0, 0))],
        out_specs=pl.BlockSpec((tt, D_MODEL), lambda i, *_: (i, 0)),
        scratch_shapes=[pltpu.VMEM((2, tile_rows, D_MODEL), F32),
                        pltpu.VMEM((TOP_K, tt, DISPATCH_CHUNK), I16), pltpu.VMEM((TOP_K, tt, DISPATCH_CHUNK), BF16),
                        pltpu.SemaphoreType.DMA((2,))],
    )
    return pl.pallas_call(
        functools.partial(_combine_kernel, nbits=nbits),
        grid_spec=grid_spec,
        out_shape=jax.ShapeDtypeStruct((n, D_MODEL), F32),
        compiler_params=_cparams(1),
        name="combine",
    )(*tabs, ys, x2, dest, gate)


def _piece_tables(n8_a, off_a, rows_a, n8_b, off_b, bm, nblocks):
    nta = n8_a.shape[0]
    n8 = jnp.concatenate([n8_a, n8_b], axis=0)
    seg_off = jnp.concatenate([off_a, off_b], axis=0)
    n_tiles = n8.shape[0]
    tile_base = jnp.concatenate([jnp.arange(nta, dtype=I32) * rows_a, jnp.zeros((n_tiles - nta,), I32)])[:, None]
    tot = jnp.sum(n8, axis=0)
    pos0 = jnp.cumsum(n8, axis=0) - n8
    nblk = (tot + bm - 1) // bm
    cs = jnp.cumsum(nblk)
    bs = cs - nblk
    kblk = pos0 // bm
    len0 = jnp.minimum(n8, (kblk + 1) * bm - pos0)
    len1 = n8 - len0
    b0 = bs[None, :] + kblk
    src0 = tile_base + seg_off
    in_blk = pos0 - kblk * bm
    jj = jnp.arange(nblocks, dtype=I32)
    i32 = lambda v: v.astype(I32)

    def gmm_tabs(sl):
        em = lambda v: v[sl].T.reshape(-1)
        blk_em = em(b0)
        first = i32(jnp.sum(blk_em[None, :] < jj[:, None], axis=1))
        last = i32(jnp.sum(blk_em[None, :] <= jj[:, None], axis=1))
        hit = (blk_em[None, :] + 1 == jj[:, None]) & (em(len1)[None, :] > 0)
        tail_src = i32(jnp.sum(jnp.where(hit, em(src0 + len0)[None, :], 0), axis=1))
        tail_len = i32(jnp.sum(jnp.where(hit, em(len1)[None, :], 0), axis=1) // SUBLANES)
        return first, last, i32(em(src0)), i32(em(in_blk)), i32(em(len0) // SUBLANES), tail_src, tail_len

    def comb_tabs(sl):
        tm = lambda v: v[sl].reshape(-1)
        has_tail = len1[sl] > 0
        cnt = jnp.sum(has_tail, axis=1)
        lo = jnp.cumsum(cnt) - cnt
        slot_ = lo[:, None] + jnp.cumsum(has_tail, axis=1) - has_tail
        hit = (slot_.reshape(-1)[None, :] == jj[:, None]) & has_tail.reshape(-1)[None, :]
        pick = lambda v: i32(jnp.sum(jnp.where(hit, tm(v)[None, :], 0), axis=1))
        return (i32(tm(b0 * bm + in_blk)), i32(tm(seg_off)), i32(tm(len0) // SUBLANES),
                i32(lo), i32(lo + cnt), pick((b0 + 1) * bm), pick(seg_off + len0), pick(len1 // SUBLANES),
                i32(jnp.sum(n8[sl], axis=1) // SUBLANES))

    count_le = lambda v: jnp.sum(cs[None, :] <= v[:, None], axis=1)
    n_active = cs[-1]
    e_last = count_le(jnp.maximum(n_active - 1, 0).reshape(1))[0]
    block_e = jnp.minimum(count_le(jj), e_last).astype(I32)
    ee = jnp.arange(N_EXPERTS, dtype=I32)
    mine = (jj[:, None] >= bs[None, :]) & (jj[:, None] < cs[None, :])
    left = jnp.clip(tot[None, :] - (jj[:, None] - bs[None, :]) * bm, 0, bm)
    rows8 = (jnp.sum(jnp.where(mine, left, 0), axis=1) // SUBLANES).astype(I32)
    has = nblk > 0
    run = jnp.cumsum(has.astype(I32)) - 1
    later = (ee[None, :] > ee[:, None]) & has[None, :]
    nxt_e = jnp.min(jnp.where(later, ee[None, :], N_EXPERTS), axis=1)
    nxt_e = jnp.where(nxt_e == N_EXPERTS, -1, nxt_e)
    own = block_e[:, None] == ee[None, :]
    wslot = (jnp.sum(jnp.where(own, run[None, :], 0), axis=1) % 2).astype(I32)
    nxt = jnp.sum(jnp.where(own, nxt_e[None, :], 0), axis=1).astype(I32)
    a, b = slice(0, nta), slice(nta, n_tiles)
    return (block_e, rows8, wslot, nxt), gmm_tabs(a), gmm_tabs(b), comb_tabs(a), comb_tabs(b)


def _block_diag(w):
    nb, bi, bo = w.shape
    eye = jnp.eye(nb, dtype=w.dtype)
    return (eye[:, None, :, None] * w[:, :, None, :]).reshape(nb * bi, nb * bo)


def _step(x_prompt, x_sample, cache_k, cache_v, state_conv, state_h, g_mix_norm, w_in, g_q_norm, g_k_norm,
          attn_sinks, conv_w, conv_b, w_lru_a, b_lru_a, w_lru_x, b_lru_x, lru_lambda, g_attn_out, g_rnn_out,
          w_out, g_ffn_norm, w_router, b_router, w_gate_up, b_gate_up, w_down, b_down,
          *, tm, tt, tc, bm, past_len):
    B, S, D = x_prompt.shape
    NS = x_sample.shape[0]
    assert x_sample.shape[1] == 1 and D == D_MODEL
    assert (B * S) % tt == 0 and (B * S) % tm == 0 and S % tc == 0 and S % ATTN_BLOCK == 0 and S % tm == 0
    assert NS % SUBLANES == 0 and tt <= bm
    assert tt % SUBLANES == 0 and NS <= bm
    n_pt = (B * S) // tt
    total_rows = TOP_K * (B * S + NS) + (n_pt + 1) * N_EXPERTS * (SUBLANES - 1)
    nblocks = -(-total_rows // bm) + N_EXPERTS
    nbits_p = (tt // SUBLANES).bit_length()
    nbits_s = (NS // SUBLANES).bit_length()

    l = 0
    row = lambda v: v[l].reshape(1, -1)
    w_in_bf = w_in[l].astype(BF16)
    gq2 = jnp.tile(g_q_norm[l], 2).reshape(1, LANES)
    gk2 = jnp.tile(g_k_norm[l], 2).reshape(1, LANES)
    wa = _block_diag(w_lru_a[l]).astype(BF16)
    wx = _block_diag(w_lru_x[l]).astype(BF16)
    ba = b_lru_a[l].reshape(1, D_RNN)
    bx = b_lru_x[l].reshape(1, D_RNN)
    wo = w_out[l].astype(BF16)
    woa, wor = wo[:D_ATTN], wo[D_ATTN:]
    wr = w_router[l].T
    wr_hi = wr.astype(BF16)
    wrt = jnp.concatenate([wr_hi, (wr - wr_hi.astype(F32)).astype(BF16)], axis=0)
    br = b_router[l].reshape(N_EXPERTS, 1)
    low = jnp.tril(jnp.ones((N_EXPERTS, N_EXPERTS), BF16), k=-1)
    bgu = b_gate_up[l].reshape(N_EXPERTS, D_FF, 2)
    bg = bgu[:, :, 0].reshape(N_EXPERTS, 1, D_FF)
    bu = bgu[:, :, 1].reshape(N_EXPERTS, 1, D_FF)
    bd = b_down[l].reshape(N_EXPERTS, 1, D_MODEL)
    half = PERM_COLS // 2
    pr = jnp.arange(PERM_COLS)
    perm = (pr[None, :] == jnp.where(pr % 2 == 0, pr // 2, half + pr // 2)[:, None]).astype(BF16)
    sinks = attn_sinks[l]

    ctab, s1tab, s2tab = _rope_tables(jnp.arange(S))
    an, rn, kt_p, vt_p, h_last_p, xr_tail = _front(
        x_prompt.reshape(B * S, D), sinks, row(g_mix_norm), w_in_bf, gq2, gk2, ctab, s1tab, s2tab,
        row(g_attn_out), conv_w[l], row(conv_b), wa, ba, wx, bx, row(lru_lambda), row(g_rnn_out), B, S, tm)
    x2_p, ts_p, dest_p, gate_p, n8_p, off_p = _mix_route(
        x_prompt.reshape(B * S, D), an, rn, woa, wor, row(g_ffn_norm), wrt, br, low, tt)

    cs_tab = _rope_tables(jnp.full((NS,), past_len, I32))
    q_s, k_s, v_s, xr_s, yr_s = _in_proj(x_sample.reshape(NS, D), row(g_mix_norm), w_in_bf, gq2, gk2,
                                         *cs_tab, NS)
    to_rows = lambda c: jnp.transpose(c, (0, 2, 3, 1)).reshape(NS * KV_W, WINDOW)
    from_rows = lambda c, n: jnp.transpose(c.reshape(n, N_KV_HEADS, HEAD_DIM, WINDOW), (0, 3, 1, 2))[None]
    an_s, kt_s, vt_s = _attn_sample(q_s, k_s, v_s, to_rows(cache_k[l]), to_rows(cache_v[l]), sinks,
                                    row(g_attn_out), SUBLANES)
    rn_s, h_last_s, hist_s = _rnn_sample(xr_s, yr_s, jnp.transpose(state_conv[l], (1, 0, 2)), state_h[l],
                                         conv_w[l], row(conv_b), wa, ba, wx, bx, row(lru_lambda),
                                         row(g_rnn_out))
    x2_s, ts_s, dest_s, gate_s, n8_s, off_s = _mix_route(
        x_sample.reshape(NS, D), an_s, rn_s, woa, wor, row(g_ffn_norm), wrt, br, low, NS)

    blocks, gmm_p, gmm_s, comb_p, comb_s = _piece_tables(
        n8_p[:, :, 0], off_p[:, :, 0], _tile_rows(tt), n8_s[:, :, 0], off_s[:, :, 0], bm, nblocks)
    ys = _moe_gmm(blocks, gmm_p, gmm_s, ts_p, ts_s, w_gate_up[l], w_down[l], bg, bu, bd, perm,
                  nblocks, bm, nbits_p, nbits_s)
    tr = lambda a: jnp.transpose(a, (0, 2, 1))
    y_p = _combine(comb_p, ys, x2_p, tr(dest_p), tr(gate_p), tt, nbits_p)
    y_s = _combine(comb_s, ys, x2_s, tr(dest_s), tr(gate_s), NS, nbits_s)

    cp = xr_tail[:, SUBLANES - (CONV_WIDTH - 1):]
    return (y_p.reshape(B, S, D), y_s.reshape(NS, 1, D),
            from_rows(kt_p, B), from_rows(vt_p, B), cp[None], h_last_p.reshape(1, B, D_RNN),
            from_rows(kt_s, NS), from_rows(vt_s, NS), jnp.transpose(hist_s, (1, 0, 2))[None], h_last_s[None])


def kernel(x_prompt, x_sample, cache_k, cache_v, state_conv, state_h, g_mix_norm, w_in, g_q_norm, g_k_norm, attn_sinks, conv_w, conv_b, w_lru_a, b_lru_a, w_lru_x, b_lru_x, lru_lambda, g_attn_out, g_rnn_out, w_out, g_ffn_norm, w_router, b_router, w_gate_up, b_gate_up, w_down, b_down):
    return _step(x_prompt, x_sample, cache_k, cache_v, state_conv, state_h, g_mix_norm, w_in, g_q_norm,
                 g_k_norm, attn_sinks, conv_w, conv_b, w_lru_a, b_lru_a, w_lru_x, b_lru_x, lru_lambda,
                 g_attn_out, g_rnn_out, w_out, g_ffn_norm, w_router, b_router, w_gate_up, b_gate_up,
                 w_down, b_down, tm=512, tt=512, tc=256, bm=MOE_BLOCK_ROWS, past_len=PAST_LEN)
```

```python
import functools

import jax
import jax.numpy as jnp
from jax import lax
from jax.experimental import pallas as pl
from jax.experimental.pallas import tpu as pltpu

F32 = jnp.float32
BF16 = jnp.bfloat16
I32 = jnp.int32
I16 = jnp.int16

D_MODEL = 1024
HEAD_DIM = 64
N_HEADS = 8
N_KV_HEADS = 2
GROUP = 4
WINDOW = 128
ATTN_BLOCK = 128
ROT_DIM = 16
ROPE_THETA = 500000.0
D_ATTN = 512
D_RNN = 512
KV_W = 128
D_IN = 1792
CONV_WIDTH = 4
LRU_C = 8.0
N_EXPERTS = 32
TOP_K = 4
D_FF = 1024
SWIGLU_LIMIT = 7.0
SWIGLU_ALPHA = 1.702
EPS = 1e-6
PAST_LEN = 8192

LANES = 128
SUBLANES = 8
NEG_BIG = -1e30
VMEM_LIMIT = 56 * 1024 * 1024

MOE_BLOCK_ROWS = 512
PERM_COLS = 256
DISPATCH_CHUNK = 256


def _cparams(n_axes):
    return pltpu.CompilerParams(dimension_semantics=("arbitrary",) * n_axes,
                                vmem_limit_bytes=VMEM_LIMIT)


def _rmsnorm(x, g):
    ms = jnp.mean(x * x, axis=-1, keepdims=True)
    return (x * lax.rsqrt(ms + EPS)) * g


def _head_norm_rope(t, g, c, s1, s2, lo):
    sq = t * t
    s_lo = jnp.sum(jnp.where(lo, sq, 0.0), axis=-1, keepdims=True)
    s_hi = jnp.sum(jnp.where(lo, 0.0, sq), axis=-1, keepdims=True)
    ms = jnp.where(lo, s_lo, s_hi) * (1.0 / HEAD_DIM)
    n = (t * lax.rsqrt(ms + EPS)) * g
    up = pltpu.roll(n, LANES - ROT_DIM // 2, 1)
    dn = pltpu.roll(n, ROT_DIM // 2, 1)
    return n * c + up * s1 + dn * s2


def _in_proj_kernel(x_ref, g_ref, w_ref, gq_ref, gk_ref, c_ref, s1_ref, s2_ref,
                    q_ref, k_ref, v_ref, xr_ref, yr_ref):
    tm = x_ref.shape[0]
    h = _rmsnorm(x_ref[...], g_ref[...])
    proj = jnp.dot(h.astype(BF16), w_ref[...], preferred_element_type=F32)
    rope = (c_ref[...], s1_ref[...], s2_ref[...], lax.broadcasted_iota(I32, (tm, LANES), 1) < HEAD_DIM)
    gq = gq_ref[...]
    for j in range(D_ATTN // LANES):
        q_ref[:, j * LANES:(j + 1) * LANES] = _head_norm_rope(proj[:, j * LANES:(j + 1) * LANES], gq, *rope)
    k_ref[...] = _head_norm_rope(proj[:, D_ATTN:D_ATTN + KV_W], gk_ref[...], *rope)
    v_ref[...] = proj[:, D_ATTN + KV_W:D_ATTN + 2 * KV_W]
    o = D_ATTN + 2 * KV_W
    xr_ref[...] = proj[:, o:o + D_RNN]
    yr_ref[...] = proj[:, o + D_RNN:o + 2 * D_RNN]


def _in_proj(x2d, g, w_bf, gq2, gk2, ctab, s1tab, s2tab, tm):
    n = x2d.shape[0]
    ntab = ctab.shape[0] // tm
    row = lambda i: (i, 0)
    fix = lambda i: (0, 0)
    tab = lambda i: (i % ntab, 0)
    out_shapes = (jax.ShapeDtypeStruct((n, D_ATTN), F32), jax.ShapeDtypeStruct((n, KV_W), F32),
                  jax.ShapeDtypeStruct((n, KV_W), F32), jax.ShapeDtypeStruct((n, D_RNN), F32),
                  jax.ShapeDtypeStruct((n, D_RNN), F32))
    return pl.pallas_call(
        _in_proj_kernel,
        grid=(n // tm,),
        in_specs=[pl.BlockSpec((tm, D_MODEL), row), pl.BlockSpec((1, D_MODEL), fix),
                  pl.BlockSpec((D_MODEL, D_IN), fix), pl.BlockSpec((1, LANES), fix),
                  pl.BlockSpec((1, LANES), fix), pl.BlockSpec((tm, LANES), tab),
                  pl.BlockSpec((tm, LANES), tab), pl.BlockSpec((tm, LANES), tab)],
        out_specs=(pl.BlockSpec((tm, D_ATTN), row), pl.BlockSpec((tm, KV_W), row),
                   pl.BlockSpec((tm, KV_W), row), pl.BlockSpec((tm, D_RNN), row),
                   pl.BlockSpec((tm, D_RNN), row)),
        out_shape=out_shapes,
        compiler_params=_cparams(1),
        name="in_proj",
    )(x2d, g, w_bf, gq2, gk2, ctab, s1tab, s2tab)


def _rope_tables(pos):
    half = ROT_DIM // 2
    inv = ROPE_THETA ** (-jnp.arange(0, ROT_DIM, 2, dtype=F32) / ROT_DIM)
    ang = pos.astype(F32)[:, None] * inv[None, :]
    cos = jnp.cos(ang)
    sin = jnp.sin(ang)
    n = pos.shape[0]
    ones = jnp.ones((n, HEAD_DIM - ROT_DIM), F32)
    zeros = jnp.zeros((n, HEAD_DIM - ROT_DIM), F32)
    zh = jnp.zeros((n, half), F32)
    c = jnp.concatenate([cos, cos, ones], axis=1)
    s1 = jnp.concatenate([-sin, zh, zeros], axis=1)
    s2 = jnp.concatenate([zh, sin, zeros], axis=1)
    two = lambda t: jnp.concatenate([t, t], axis=1)
    return two(c), two(s1), two(s2)


def _band_bias(qb):
    qi = jnp.arange(qb, dtype=I32)[:, None]
    c = jnp.arange(2 * qb, dtype=I32)[None, :]
    band = (c >= qi) & (c <= qi + qb)
    first = band & (c >= qb)
    one = jnp.where(jnp.stack([first, band]), 0.0, NEG_BIG).astype(F32)
    return jnp.concatenate([one, one], axis=2)


def _attn_prompt_kernel(sink_ref, q_ref, kc_ref, kp_ref, vc_ref, vp_ref, bias_ref, g_ref,
                        o_ref, kt_ref, vt_ref, s_ref, e_ref):
    j = pl.program_id(1)
    qb = ATTN_BLOCK
    kc = kc_ref[...]
    vc = vc_ref[...]
    k2 = jnp.concatenate([kp_ref[...], kc], axis=0)
    v2 = jnp.concatenate([vp_ref[...], vc], axis=0)
    k2r = pltpu.roll(k2, HEAD_DIM, 1)
    v2r = pltpu.roll(v2, HEAD_DIM, 1)
    lo_k = lax.broadcasted_iota(I32, (2 * qb, LANES), 1) < HEAD_DIM
    lo_q = lax.broadcasted_iota(I32, (qb, LANES), 1) < HEAD_DIM
    bias = bias_ref[0]
    nt = (((1,), (1,)), ((), ()))
    n_pairs = N_HEADS // 2
    kbd, vbd = [], []
    for kv in range(N_KV_HEADS):
        ka, kb = (k2, k2r) if kv == 0 else (k2r, k2)
        va, vb = (v2, v2r) if kv == 0 else (v2r, v2)
        kbd.append(jnp.concatenate([jnp.where(lo_k, ka, 0.0), jnp.where(lo_k, 0.0, kb)], axis=0).astype(BF16))
        vbd.append(jnp.concatenate([jnp.where(lo_k, va, 0.0), jnp.where(lo_k, 0.0, vb)], axis=0).astype(BF16))
    for pp in range(n_pairs):
        qp = (q_ref[:, pp * LANES:(pp + 1) * LANES] * (HEAD_DIM ** -0.5)).astype(BF16)
        s_ref[pp] = lax.dot_general(qp, kbd[pp // (GROUP // 2)], nt, preferred_element_type=F32) + bias
    invs = []
    for pp in range(n_pairs):
        inv = []
        for t in range(2):
            cols = slice(t * 2 * qb, (t + 1) * 2 * qb)
            st = s_ref[pp, :, cols]
            sink = sink_ref[2 * pp + t]
            m = jnp.maximum(jnp.max(st, axis=-1, keepdims=True), sink)
            e = jnp.exp(st - m)
            e_ref[pp, :, cols] = e.astype(BF16)
            inv.append(1.0 / (jnp.sum(e, axis=-1, keepdims=True) + jnp.exp(sink - m)))
        invs.append(jnp.where(lo_q, inv[0], inv[1]))
    outs = [jnp.dot(e_ref[pp], vbd[pp // (GROUP // 2)], preferred_element_type=F32) * invs[pp]
            for pp in range(n_pairs)]
    o_ref[...] = _rmsnorm(jnp.concatenate(outs, axis=1), g_ref[...])

    @pl.when(j == pl.num_programs(1) - 1)
    def _():
        kt_ref[0] = kc.T
        vt_ref[0] = vc.T


def _attn_prompt(q, k, v, sinks, g_attn, batch, seq):
    qb = ATTN_BLOCK
    nb = seq // qb
    cur = lambda b, j: (b * nb + j, 0)
    prev = lambda b, j: (b * nb + jnp.maximum(j - 1, 0), 0)
    fix = lambda b, j: (0, 0)
    per_b = lambda b, j: (b, 0, 0)
    return pl.pallas_call(
        _attn_prompt_kernel,
        grid=(batch, nb),
        in_specs=[pl.BlockSpec(memory_space=pltpu.SMEM),
                  pl.BlockSpec((qb, D_ATTN), cur),
                  pl.BlockSpec((qb, KV_W), cur), pl.BlockSpec((qb, KV_W), prev),
                  pl.BlockSpec((qb, KV_W), cur), pl.BlockSpec((qb, KV_W), prev),
                  pl.BlockSpec((1, qb, 4 * qb), lambda b, j: (jnp.minimum(j, 1), 0, 0)),
                  pl.BlockSpec((1, D_ATTN), fix)],
        out_specs=(pl.BlockSpec((qb, D_ATTN), cur), pl.BlockSpec((1, KV_W, qb), per_b),
                   pl.BlockSpec((1, KV_W, qb), per_b)),
        out_shape=(jax.ShapeDtypeStruct((batch * seq, D_ATTN), F32),
                   jax.ShapeDtypeStruct((batch, KV_W, qb), F32),
                   jax.ShapeDtypeStruct((batch, KV_W, qb), F32)),
        scratch_shapes=[pltpu.VMEM((N_HEADS // 2, qb, 4 * qb), F32),
                        pltpu.VMEM((N_HEADS // 2, qb, 4 * qb), BF16)],
        compiler_params=_cparams(2),
        name="attn_prompt",
    )(sinks, q, k, k, v, v, _band_bias(qb), g_attn)


def _attn_sample_kernel(sink_ref, q_ref, kn_ref, vn_ref, kt_ref, vt_ref, g_ref,
                        o_ref, nkt_ref, nvt_ref, acc_ref):
    bb = q_ref.shape[0]
    q = q_ref[...] * (HEAD_DIM ** -0.5)
    kn = kn_ref[...]
    vn = vn_ref[...]
    kt = kt_ref[...]
    vt = vt_ref[...]
    col = lax.broadcasted_iota(I32, (bb, bb * KV_W), 1)
    rowb = lax.broadcasted_iota(I32, (bb, bb * KV_W), 0)
    own_seq = (col >> (KV_W.bit_length() - 1)) == rowb
    half_hi = ((col >> (HEAD_DIM.bit_length() - 1)) & 1) == 1
    qbig = []
    for h in range(N_HEADS):
        kv = h // GROUP
        pair = q[:, (h // 2) * LANES:(h // 2 + 1) * LANES]
        if (h % 2) != kv:
            pair = pltpu.roll(pair, HEAD_DIM, 1)
        tiled = jnp.concatenate([pair] * bb, axis=1)
        keep = own_seq & (half_hi if kv == 1 else jnp.logical_not(half_hi))
        qbig.append(jnp.where(keep, tiled, 0.0))
    qbig = jnp.concatenate(qbig, axis=0)
    s = jnp.dot(qbig.astype(BF16), kt.astype(BF16), preferred_element_type=F32)
    qb16 = q.astype(BF16).astype(F32)
    kb16 = kn.astype(BF16).astype(F32)
    s_new, sink = [], []
    for h in range(N_HEADS):
        kv = h // GROUP
        s_new.append(jnp.sum(qb16[:, h * HEAD_DIM:(h + 1) * HEAD_DIM] * kb16[:, kv * HEAD_DIM:(kv + 1) * HEAD_DIM],
                             axis=-1, keepdims=True))
        sink.append(jnp.full((bb, 1), sink_ref[h], F32))
    s_new = jnp.concatenate(s_new, axis=0)
    sink = jnp.concatenate(sink, axis=0)
    m = jnp.maximum(jnp.maximum(jnp.max(s, axis=-1, keepdims=True), s_new), sink)
    e = jnp.exp(s - m)
    e_new = jnp.exp(s_new - m)
    inv = 1.0 / (jnp.sum(e, axis=-1, keepdims=True) + e_new + jnp.exp(sink - m))
    obig = lax.dot_general(e.astype(BF16), vt.astype(BF16), (((1,), (1,)), ((), ())),
                           preferred_element_type=F32)
    for h in range(N_HEADS):
        kv = h // GROUP
        blk = jnp.where(own_seq, obig[h * bb:(h + 1) * bb, :], 0.0)
        fold = blk[:, 0:KV_W]
        for t in range(1, bb):
            fold = fold + blk[:, t * KV_W:(t + 1) * KV_W]
        hs = slice(h * bb, (h + 1) * bb)
        ks = slice(kv * HEAD_DIM, (kv + 1) * HEAD_DIM)
        acc_ref[:, h * HEAD_DIM:(h + 1) * HEAD_DIM] = (fold[:, ks] + e_new[hs] * vn[:, ks]) * inv[hs]
    o_ref[...] = _rmsnorm(acc_ref[...], g_ref[...])

    last = lax.broadcasted_iota(I32, (KV_W, WINDOW), 1) == WINDOW - 1
    for b in range(bb):
        rs = slice(b * KV_W, (b + 1) * KV_W)
        kcol = jnp.broadcast_to(kn[b:b + 1, :], (KV_W, KV_W)).T
        vcol = jnp.broadcast_to(vn[b:b + 1, :], (KV_W, KV_W)).T
        nkt_ref[rs, :] = jnp.where(last, kcol, pltpu.roll(kt[rs, :], WINDOW - 1, 1))
        nvt_ref[rs, :] = jnp.where(last, vcol, pltpu.roll(vt[rs, :], WINDOW - 1, 1))


def _attn_sample(q, kn, vn, kt2d, vt2d, sinks, g_attn, bb):
    n = q.shape[0]
    row = lambda i: (i, 0)
    fix = lambda i: (0, 0)
    cache = pl.BlockSpec((bb * KV_W, WINDOW), row)
    return pl.pallas_call(
        _attn_sample_kernel,
        grid=(n // bb,),
        in_specs=[pl.BlockSpec(memory_space=pltpu.SMEM),
                  pl.BlockSpec((bb, D_ATTN), row), pl.BlockSpec((bb, KV_W), row),
                  pl.BlockSpec((bb, KV_W), row), cache, cache,
                  pl.BlockSpec((1, D_ATTN), fix)],
        out_specs=(pl.BlockSpec((bb, D_ATTN), row), cache, cache),
        out_shape=(jax.ShapeDtypeStruct((n, D_ATTN), F32),
                   jax.ShapeDtypeStruct(kt2d.shape, F32), jax.ShapeDtypeStruct(vt2d.shape, F32)),
        scratch_shapes=[pltpu.VMEM((bb, D_ATTN), F32)],
        compiler_params=_cparams(1),
        name="attn_sample",
    )(sinks, q, kn, vn, kt2d, vt2d, g_attn)


def _softplus(z):
    return jnp.maximum(z, 0.0) + jnp.log1p(jnp.exp(-jnp.abs(z)))


def _lru_gates(xc, wa_ref, ba_ref, wx_ref, bx_ref, lam_ref):
    xb = xc.astype(BF16)
    r = jax.nn.sigmoid(jnp.dot(xb, wa_ref[...], preferred_element_type=F32) + ba_ref[...])
    i = jax.nn.sigmoid(jnp.dot(xb, wx_ref[...], preferred_element_type=F32) + bx_ref[...])
    log_a = (-LRU_C * r) * _softplus(-lam_ref[...])
    a = jnp.exp(log_a)
    z = -jnp.tanh(log_a) * (a * a + 1.0)
    u = jnp.where(z > 0.0, z * lax.rsqrt(z), 0.0) * (i * xc)
    return a, u


def _lru_scan(a, u, h0):
    ng = a.shape[0] // SUBLANES
    a3 = a.reshape(ng, SUBLANES, D_RNN)
    u3 = u.reshape(ng, SUBLANES, D_RNN)
    t8 = lax.broadcasted_iota(I32, (ng, SUBLANES, D_RNN), 1)
    d = 1
    while d < SUBLANES:
        a_s = jnp.where(t8 >= d, pltpu.roll(a3, d, 1), 1.0)
        u_s = jnp.where(t8 >= d, pltpu.roll(u3, d, 1), 0.0)
        u3 = a3 * u_s + u3
        a3 = a3 * a_s
        d *= 2
    carry = h0
    groups = []
    for g in range(ng):
        hg = a3[g] * carry + u3[g]
        groups.append(hg)
        carry = hg[SUBLANES - 1:SUBLANES, :]
    return jnp.concatenate(groups, axis=0), carry


def _lru_scan_tiles(a_ref, u_ref, h_ref, h0):
    nl, rows, _ = a_ref.shape
    ng = rows // SUBLANES
    step = lambda ref, s: jnp.concatenate(
        [ref[j, pl.ds(s, ng, stride=SUBLANES), :] for j in range(nl)], axis=1)
    prods = [step(a_ref, 0)]
    locs = [step(u_ref, 0)]
    for s in range(1, SUBLANES):
        a_s = step(a_ref, s)
        locs.append(a_s * locs[-1] + step(u_ref, s))
        prods.append(a_s * prods[-1])
    after, h_last = _lru_scan(prods[-1], locs[-1], h0)
    row = lax.broadcasted_iota(I32, (ng, D_RNN), 0)
    before = jnp.where(row == 0, h0, pltpu.roll(after, 1, 0))
    for s in range(SUBLANES):
        h_s = locs[s] + prods[s] * before
        for j in range(nl):
            h_ref[j, pl.ds(s, ng, stride=SUBLANES), :] = h_s[:, j * LANES:(j + 1) * LANES]
    return h_last


def _to_lane_tiles(ref, x):
    for j in range(ref.shape[0]):
        ref[j] = x[:, j * LANES:(j + 1) * LANES]


def _rnn_prompt_kernel(xr_ref, yr_ref, cw_ref, cb_ref, wa_ref, ba_ref, wx_ref, bx_ref, lam_ref, g_ref,
                       o_ref, hl_ref, ext_ref, h_ref):
    c = pl.program_id(1)
    tc = xr_ref.shape[0]
    pad = SUBLANES

    @pl.when(c == 0)
    def _():
        ext_ref[0:pad, :] = jnp.zeros((pad, D_RNN), F32)
        h_ref[...] = jnp.zeros((1, D_RNN), F32)

    ext_ref[pad:pad + tc, :] = xr_ref[...]
    cw = cw_ref[...]
    xc = cb_ref[...] + ext_ref[pad:pad + tc, :] * cw[CONV_WIDTH - 1:CONV_WIDTH, :]
    for w in range(CONV_WIDTH - 1):
        sh = CONV_WIDTH - 1 - w
        xc = xc + ext_ref[pad - sh:pad - sh + tc, :] * cw[w:w + 1, :]
    ext_ref[0:pad, :] = ext_ref[tc:tc + pad, :]

    a, u = _lru_gates(xc, wa_ref, ba_ref, wx_ref, bx_ref, lam_ref)
    h, carry = _lru_scan(a, u, h_ref[...])
    h_ref[...] = carry
    hl_ref[0] = carry
    o_ref[...] = _rmsnorm(jax.nn.gelu(yr_ref[...]) * h, g_ref[...])


def _rnn_prompt(xr, yr, cw, cb, wa, ba, wx, bx, lam, g, batch, seq, tc):
    nc = seq // tc
    cur = lambda b, c: (b * nc + c, 0)
    fix = lambda b, c: (0, 0)
    vec = pl.BlockSpec((1, D_RNN), fix)
    return pl.pallas_call(
        _rnn_prompt_kernel,
        grid=(batch, nc),
        in_specs=[pl.BlockSpec((tc, D_RNN), cur), pl.BlockSpec((tc, D_RNN), cur),
                  pl.BlockSpec((CONV_WIDTH, D_RNN), fix), vec,
                  pl.BlockSpec((D_RNN, D_RNN), fix), vec,
                  pl.BlockSpec((D_RNN, D_RNN), fix), vec, vec, vec],
        out_specs=(pl.BlockSpec((tc, D_RNN), cur), pl.BlockSpec((1, 1, D_RNN), lambda b, c: (b, 0, 0))),
        out_shape=(jax.ShapeDtypeStruct((batch * seq, D_RNN), F32),
                   jax.ShapeDtypeStruct((batch, 1, D_RNN), F32)),
        scratch_shapes=[pltpu.VMEM((tc + SUBLANES, D_RNN), F32), pltpu.VMEM((1, D_RNN), F32)],
        compiler_params=_cparams(2),
        name="rnn_prompt",
    )(xr, yr, cw, cb, wa, ba, wx, bx, lam, g)


def _rnn_sample_kernel(xr_ref, yr_ref, hist_ref, h0_ref, cw_ref, cb_ref, wa_ref, ba_ref, wx_ref, bx_ref,
                       lam_ref, g_ref, o_ref, hl_ref, nh_ref):
    cw = cw_ref[...]
    xr = xr_ref[...]
    xc = cb_ref[...] + xr * cw[CONV_WIDTH - 1:CONV_WIDTH, :]
    for w in range(CONV_WIDTH - 1):
        xc = xc + hist_ref[w] * cw[w:w + 1, :]
    a, u = _lru_gates(xc, wa_ref, ba_ref, wx_ref, bx_ref, lam_ref)
    h = a * h0_ref[...] + u
    hl_ref[...] = h
    o_ref[...] = _rmsnorm(jax.nn.gelu(yr_ref[...]) * h, g_ref[...])
    for w in range(CONV_WIDTH - 2):
        nh_ref[w] = hist_ref[w + 1]
    nh_ref[CONV_WIDTH - 2] = xr


def _rnn_sample(xr, yr, hist, h0, cw, cb, wa, ba, wx, bx, lam, g):
    n = xr.shape[0]
    full = lambda a: pl.BlockSpec(a.shape, lambda: (0,) * a.ndim)
    args = (xr, yr, hist, h0, cw, cb, wa, ba, wx, bx, lam, g)
    return pl.pallas_call(
        _rnn_sample_kernel,
        in_specs=[full(a) for a in args],
        out_specs=(pl.BlockSpec((n, D_RNN), lambda: (0, 0)), pl.BlockSpec((n, D_RNN), lambda: (0, 0)),
                   pl.BlockSpec(hist.shape, lambda: (0, 0, 0))),
        out_shape=(jax.ShapeDtypeStruct((n, D_RNN), F32), jax.ShapeDtypeStruct((n, D_RNN), F32),
                   jax.ShapeDtypeStruct(hist.shape, F32)),
        compiler_params=pltpu.CompilerParams(vmem_limit_bytes=VMEM_LIMIT),
        name="rnn_sample",
    )(*args)


def _front_kernel(*refs, tiles_per_seq):
    i = pl.program_id(0)
    q_s, k_s, v_s, xr_s, yr_s, ext_ref, h_ref = refs[25:32]

    @pl.when(i == 0)
    def _():
        for r in (q_s, k_s, v_s, xr_s, yr_s, ext_ref, h_ref):
            r[...] = jnp.zeros(r.shape, F32)

    for cur in range(2):
        @pl.when(i % 2 == cur)
        def _():
            _front_body(cur, 1 - cur, *refs, tiles_per_seq=tiles_per_seq)


def _front_body(cur, prv, sink_ref, x_ref, gm_ref, w_ref, gq_ref, gk_ref, c_ref, s1_ref, s2_ref, bias_ref,
                ga_ref, cw_ref, cb_ref, wa_ref, ba_ref, wx_ref, bx_ref, lam_ref, gr_ref,
                an_ref, rn_ref, kt_ref, vt_ref, hl_ref, cx_ref,
                q_s, k_s, v_s, xr_s, yr_s, ext_ref, h_ref, s_ref, e_ref, a_scr, u_scr, hs_scr,
                *, tiles_per_seq):
    i = pl.program_id(0)
    tm = x_ref.shape[0]
    qb = ATTN_BLOCK
    t = jnp.maximum(i - 1, 0)
    first_tile = (t % tiles_per_seq) == 0

    nqb = tm // qb
    n_pairs = N_HEADS // 2
    lo_k = lax.broadcasted_iota(I32, (2 * qb, LANES), 1) < HEAD_DIM
    lo_q = lax.broadcasted_iota(I32, (qb, LANES), 1) < HEAD_DIM
    nt_dims = (((1,), (1,)), ((), ()))
    vbds = []
    for jb in range(nqb):
        k2 = k_s[prv, jb * qb:(jb + 2) * qb, :]
        v2 = v_s[prv, jb * qb:(jb + 2) * qb, :]
        k2r = pltpu.roll(k2, HEAD_DIM, 1)
        v2r = pltpu.roll(v2, HEAD_DIM, 1)
        bias = bias_ref[jnp.where(first_tile, 0, 1)] if jb == 0 else bias_ref[1]
        for kv in range(N_KV_HEADS):
            ka, kb = (k2, k2r) if kv == 0 else (k2r, k2)
            va, vb = (v2, v2r) if kv == 0 else (v2r, v2)
            kbd = jnp.concatenate([jnp.where(lo_k, ka, 0.0), jnp.where(lo_k, 0.0, kb)], axis=0).astype(BF16)
            vbds.append(jnp.concatenate([jnp.where(lo_k, va, 0.0), jnp.where(lo_k, 0.0, vb)],
                                        axis=0).astype(BF16))
            for p in range(GROUP // 2):
                pp = kv * (GROUP // 2) + p
                qp = (q_s[prv, jb * qb:(jb + 1) * qb, pp * LANES:(pp + 1) * LANES]
                      * (HEAD_DIM ** -0.5)).astype(BF16)
                s_ref[jb * n_pairs + pp] = lax.dot_general(qp, kbd, nt_dims,
                                                           preferred_element_type=F32) + bias
    invs = []
    for c in range(nqb * n_pairs):
        pp = c % n_pairs
        inv = []
        for tpos in range(2):
            cols = slice(tpos * 2 * qb, (tpos + 1) * 2 * qb)
            st = s_ref[c, :, cols]
            sink = sink_ref[2 * pp + tpos]
            m = jnp.maximum(jnp.max(st, axis=-1, keepdims=True), sink)
            e = jnp.exp(st - m)
            e_ref[c, :, cols] = e.astype(BF16)
            inv.append(1.0 / (jnp.sum(e, axis=-1, keepdims=True) + jnp.exp(sink - m)))
        invs.append(jnp.where(lo_q, inv[0], inv[1]))
    for jb in range(nqb):
        outs = [jnp.dot(e_ref[jb * n_pairs + pp], vbds[jb * N_KV_HEADS + pp // (GROUP // 2)],
                        preferred_element_type=F32) * invs[jb * n_pairs + pp] for pp in range(n_pairs)]
        an_ref[jb * qb:(jb + 1) * qb, :] = _rmsnorm(jnp.concatenate(outs, axis=1), ga_ref[...])
    kt_ref[0] = k_s[prv, tm:tm + qb, :].T
    vt_ref[0] = v_s[prv, tm:tm + qb, :].T

    pad = SUBLANES
    xr = xr_s[prv]
    ext_ref[0:pad, :] = jnp.where(first_tile, 0.0, ext_ref[0:pad, :])
    ext_ref[pad:pad + tm, :] = xr
    cw = cw_ref[...]
    xc = cb_ref[...] + xr * cw[CONV_WIDTH - 1:CONV_WIDTH, :]
    for w in range(CONV_WIDTH - 1):
        sh = CONV_WIDTH - 1 - w
        xc = xc + ext_ref[pad - sh:pad - sh + tm, :] * cw[w:w + 1, :]
    ext_ref[0:pad, :] = xr[tm - pad:tm, :]
    cx_ref[0] = xr[tm - pad:tm, :]
    a, u = _lru_gates(xc, wa_ref, ba_ref, wx_ref, bx_ref, lam_ref)
    _to_lane_tiles(a_scr, a)
    _to_lane_tiles(u_scr, u)
    carry = _lru_scan_tiles(a_scr, u_scr, hs_scr, jnp.where(first_tile, 0.0, h_ref[...]))
    h_ref[...] = carry
    hl_ref[0] = carry
    hseq = jnp.concatenate([hs_scr[j] for j in range(hs_scr.shape[0])], axis=1)
    rn_ref[...] = _rmsnorm(jax.nn.gelu(yr_s[prv]) * hseq, gr_ref[...])

    h = _rmsnorm(x_ref[...], gm_ref[...])
    proj = jnp.dot(h.astype(BF16), w_ref[...], preferred_element_type=F32)
    lo = lax.broadcasted_iota(I32, (tm, LANES), 1) < HEAD_DIM
    rope = (c_ref[...], s1_ref[...], s2_ref[...], lo)
    gq = gq_ref[...]
    for j in range(D_ATTN // LANES):
        q_s[cur, :, j * LANES:(j + 1) * LANES] = _head_norm_rope(proj[:, j * LANES:(j + 1) * LANES], gq, *rope)
    k_s[cur, qb:qb + tm, :] = _head_norm_rope(proj[:, D_ATTN:D_ATTN + KV_W], gk_ref[...], *rope)
    v_s[cur, qb:qb + tm, :] = proj[:, D_ATTN + KV_W:D_ATTN + 2 * KV_W]
    k_s[cur, 0:qb, :] = k_s[prv, tm:tm + qb, :]
    v_s[cur, 0:qb, :] = v_s[prv, tm:tm + qb, :]
    o = D_ATTN + 2 * KV_W
    xr_s[cur] = proj[:, o:o + D_RNN]
    yr_s[cur] = proj[:, o + D_RNN:o + 2 * D_RNN]


def _front(x2d, sinks, g_mix, w_bf, gq2, gk2, ctab, s1tab, s2tab, g_attn,
           cw, cb, wa, ba, wx, bx, lam, g_rnn, batch, seq, tm):
    n = x2d.shape[0]
    nt = n // tm
    tps = seq // tm
    qb = ATTN_BLOCK
    cur = lambda i: (jnp.minimum(i, nt - 1), 0)
    tab = lambda i: (jnp.minimum(i, nt - 1) % tps, 0)
    fix = lambda i: (0, 0)
    prev = lambda i: (jnp.maximum(i - 1, 0), 0)
    per_seq = lambda i: (jnp.maximum(i - 1, 0) // tps, 0, 0)
    vec = lambda w: pl.BlockSpec((1, w), fix)
    return pl.pallas_call(
        functools.partial(_front_kernel, tiles_per_seq=tps),
        grid=(nt + 1,),
        in_specs=[pl.BlockSpec(memory_space=pltpu.SMEM),
                  pl.BlockSpec((tm, D_MODEL), cur), vec(D_MODEL), pl.BlockSpec((D_MODEL, D_IN), fix),
                  vec(LANES), vec(LANES),
                  pl.BlockSpec((tm, LANES), tab), pl.BlockSpec((tm, LANES), tab), pl.BlockSpec((tm, LANES), tab),
                  pl.BlockSpec((2, qb, 4 * qb), lambda i: (0, 0, 0)), vec(D_ATTN),
                  pl.BlockSpec((CONV_WIDTH, D_RNN), fix), vec(D_RNN),
                  pl.BlockSpec((D_RNN, D_RNN), fix), vec(D_RNN),
                  pl.BlockSpec((D_RNN, D_RNN), fix), vec(D_RNN), vec(D_RNN), vec(D_RNN)],
        out_specs=(pl.BlockSpec((tm, D_ATTN), prev), pl.BlockSpec((tm, D_RNN), prev),
                   pl.BlockSpec((1, KV_W, qb), per_seq), pl.BlockSpec((1, KV_W, qb), per_seq),
                   pl.BlockSpec((1, 1, D_RNN), per_seq), pl.BlockSpec((1, SUBLANES, D_RNN), per_seq)),
        out_shape=(jax.ShapeDtypeStruct((n, D_ATTN), F32), jax.ShapeDtypeStruct((n, D_RNN), F32),
                   jax.ShapeDtypeStruct((batch, KV_W, qb), F32), jax.ShapeDtypeStruct((batch, KV_W, qb), F32),
                   jax.ShapeDtypeStruct((batch, 1, D_RNN), F32),
                   jax.ShapeDtypeStruct((batch, SUBLANES, D_RNN), F32)),
        scratch_shapes=[pltpu.VMEM((2, tm, D_ATTN), F32),
                        pltpu.VMEM((2, tm + qb, KV_W), F32), pltpu.VMEM((2, tm + qb, KV_W), F32),
                        pltpu.VMEM((2, tm, D_RNN), F32), pltpu.VMEM((2, tm, D_RNN), F32),
                        pltpu.VMEM((tm + SUBLANES, D_RNN), F32), pltpu.VMEM((1, D_RNN), F32),
                        pltpu.VMEM((tm // qb * (N_HEADS // 2), qb, 4 * qb), F32),
                        pltpu.VMEM((tm // qb * (N_HEADS // 2), qb, 4 * qb), BF16),
                        pltpu.VMEM((D_RNN // LANES, tm, LANES), F32), pltpu.VMEM((D_RNN // LANES, tm, LANES), F32),
                        pltpu.VMEM((D_RNN // LANES, tm, LANES), F32)],
        compiler_params=_cparams(1),
        name="front",
    )(sinks, x2d, g_mix, w_bf, gq2, gk2, ctab, s1tab, s2tab, _band_bias(qb), g_attn,
      cw, cb, wa, ba, wx, bx, lam, g_rnn)


def _mix_route_kernel(*refs):
    i = pl.program_id(0)
    hb_s, d_s = refs[16:18]

    @pl.when(i == 0)
    def _():
        hb_s[...] = jnp.zeros(hb_s.shape, BF16)
        d_s[...] = jnp.zeros(d_s.shape, I32)

    for cur in range(2):
        @pl.when(i % 2 == cur)
        def _():
            _mix_route_body(cur, 1 - cur, *refs)


def _mix_route_body(cur, prv, x_ref, an_ref, rn_ref, woa_ref, wor_ref, g_ref, wr2_ref, br_ref, tri_ref, low_ref,
                    x2_ref, ts_ref, dest_ref, gate_ref, n8_ref, off_ref, hb_s, d_s):
    tt = x_ref.shape[0]
    tile_rows = ts_ref.shape[0]
    n_chunks = tile_rows // DISPATCH_CHUNK

    hb_prev = hb_s[prv]
    d16 = [d_s[prv, k:k + 1, :].astype(I16) for k in range(TOP_K)]
    ri = lax.broadcasted_iota(I32, (DISPATCH_CHUNK, tt), 0).astype(I16)
    one = jnp.ones((DISPATCH_CHUNK, tt), BF16)

    def dispatch(chunks):
        for c in chunks:
            p = jnp.zeros((DISPATCH_CHUNK, tt), BF16)
            for d in d16:
                p = jnp.where(ri == d - jnp.int16(c * DISPATCH_CHUNK), one, p)
            ts_ref[c * DISPATCH_CHUNK:(c + 1) * DISPATCH_CHUNK, :] = jnp.dot(
                p, hb_prev, preferred_element_type=F32)

    third = -(-n_chunks // 3)

    x2 = x_ref[...] + jnp.dot(an_ref[...].astype(BF16), woa_ref[...], preferred_element_type=F32) \
        + jnp.dot(rn_ref[...].astype(BF16), wor_ref[...], preferred_element_type=F32)
    x2_ref[...] = x2
    dispatch(range(0, third))
    hn = _rmsnorm(x2, g_ref[...])

    nt = (((1,), (1,)), ((), ()))
    hb = hn.astype(BF16)
    hb_s[cur] = hb
    hmid = (hn - hb.astype(F32)).astype(BF16)
    wr2 = wr2_ref[...]
    both = lax.dot_general(wr2, hb, nt, preferred_element_type=F32)
    logits = (lax.dot_general(wr2[:N_EXPERTS], hmid, nt, preferred_element_type=F32)
              + both[N_EXPERTS:]) + both[:N_EXPERTS] + br_ref[...]
    dispatch(range(third, 2 * third))

    ie = lax.broadcasted_iota(I32, (N_EXPERTS, tt), 0).astype(F32)
    l = logits
    vals, sels = [], []
    for _ in range(TOP_K):
        m = jnp.max(l, axis=0, keepdims=True)
        idx = jnp.min(jnp.where(l == m, ie, float(N_EXPERTS)), axis=0, keepdims=True)
        sel = ie == idx
        vals.append(m)
        sels.append(sel)
        l = jnp.where(sel, NEG_BIG, l)
    es = [jnp.exp(v - vals[0]) for v in vals]
    den = es[0] + es[1] + es[2] + es[3]
    gate_ref[0] = jnp.concatenate([e / den for e in es], axis=0)
    dispatch(range(2 * third, n_chunks))

    oh = jnp.zeros((N_EXPERTS, tt), F32)
    for sel in sels:
        oh = oh + jnp.where(sel, 1.0, 0.0)
    before = jnp.dot(oh.astype(BF16), tri_ref[...], preferred_element_type=F32)
    cnt = jnp.sum(oh, axis=1, keepdims=True).astype(I32)
    n8 = ((cnt + (SUBLANES - 1)) >> 3) << 3
    n8b = jnp.broadcast_to(n8, (N_EXPERTS, LANES))
    off = jnp.dot(low_ref[...], n8b.astype(F32).astype(BF16), preferred_element_type=F32)
    n8_ref[0] = n8b
    off_ref[0] = off.astype(I32)
    base = off[:, 0:1] + before
    dests = jnp.concatenate(
        [jnp.sum(jnp.where(sel, base, 0.0), axis=0, keepdims=True).astype(I32) for sel in sels], axis=0)
    dest_ref[0] = dests
    d_s[cur, 0:TOP_K, :] = dests


def _tile_rows(tt):
    return -(-(TOP_K * tt + N_EXPERTS * (SUBLANES - 1)) // DISPATCH_CHUNK) * DISPATCH_CHUNK


def _mix_route(x2d, an, rn, woa, wor, g, wrt, br, low, tt):
    n = x2d.shape[0]
    nt = n // tt
    tile_rows = _tile_rows(tt)
    tri = jnp.triu(jnp.ones((tt, tt), BF16), k=1)
    row = lambda i: (jnp.minimum(i, nt - 1), 0)
    prev = lambda i: (jnp.maximum(i - 1, 0), 0)
    fix = lambda i: (0, 0)
    t3 = lambda i: (jnp.minimum(i, nt - 1), 0, 0)
    in_specs = [pl.BlockSpec((tt, D_MODEL), row), pl.BlockSpec((tt, D_ATTN), row),
                pl.BlockSpec((tt, D_RNN), row), pl.BlockSpec((D_ATTN, D_MODEL), fix),
                pl.BlockSpec((D_RNN, D_MODEL), fix), pl.BlockSpec((1, D_MODEL), fix),
                pl.BlockSpec((2 * N_EXPERTS, D_MODEL), fix), pl.BlockSpec((N_EXPERTS, 1), fix),
                pl.BlockSpec((tt, tt), fix), pl.BlockSpec((N_EXPERTS, N_EXPERTS), fix)]
    out_shape = (jax.ShapeDtypeStruct((n, D_MODEL), F32),
                 jax.ShapeDtypeStruct((nt * tile_rows, D_MODEL), F32),
                 jax.ShapeDtypeStruct((nt, TOP_K, tt), I32),
                 jax.ShapeDtypeStruct((nt, TOP_K, tt), F32),
                 jax.ShapeDtypeStruct((nt, N_EXPERTS, LANES), I32),
                 jax.ShapeDtypeStruct((nt, N_EXPERTS, LANES), I32))
    out_specs = (pl.BlockSpec((tt, D_MODEL), row),
                 pl.BlockSpec((tile_rows, D_MODEL), prev),
                 pl.BlockSpec((1, TOP_K, tt), t3), pl.BlockSpec((1, TOP_K, tt), t3),
                 pl.BlockSpec((1, N_EXPERTS, LANES), t3), pl.BlockSpec((1, N_EXPERTS, LANES), t3))
    return pl.pallas_call(
        _mix_route_kernel,
        grid=(nt + 1,),
        in_specs=in_specs,
        out_specs=out_specs,
        out_shape=out_shape,
        scratch_shapes=[pltpu.VMEM((2, tt, D_MODEL), BF16), pltpu.VMEM((2, SUBLANES, tt), I32)],
        compiler_params=_cparams(1),
        name="mix_route",
    )(x2d, an, rn, woa, wor, g, wrt, br, tri, low)


LOW_BITS = 4


def _start_piece(src_hbm, dst_buf, sem, s, d, l8, nbits):
    def bit_copy(c):
        size = SUBLANES << c
        low = (l8 & ((1 << c) - 1)) * SUBLANES

        @pl.when(((l8 >> c) & 1) == 1)
        def _():
            pltpu.make_async_copy(
                src_hbm.at[pl.ds(pl.multiple_of(s + low, SUBLANES), size)],
                dst_buf.at[pl.ds(pl.multiple_of(d + low, SUBLANES), size)], sem).start()

    for c in range(min(LOW_BITS, nbits)):
        bit_copy(c)
    if nbits > LOW_BITS:
        def long_copies(_, carry):
            for c in range(LOW_BITS, nbits):
                bit_copy(c)
            return carry

        lax.fori_loop(0, jnp.where(l8 >= (1 << LOW_BITS), 1, 0), long_copies, 0)


def _start_pieces(src_hbm, dst_buf, sem, p_lo, p_hi, psrc_ref, pdst_ref, plen_ref, nbits):
    def body(p, carry):
        _start_piece(src_hbm, dst_buf, sem, psrc_ref[p], pdst_ref[p], plen_ref[p], nbits)
        return carry

    lax.fori_loop(p_lo, p_hi, body, 0)


def _wait_rows(src_hbm, dst_buf, sem, rows8, nbits):
    for c in range(nbits):
        size = SUBLANES << c

        @pl.when(((rows8 >> c) & 1) == 1)
        def _():
            pltpu.make_async_copy(src_hbm.at[pl.ds(0, size)], dst_buf.at[pl.ds(0, size)], sem).wait()


def _moe_gmm_kernel(be_ref, rows_ref, wslot_ref, nxt_ref,
                    psa_ref, pea_ref, srca_ref, dsta_ref, lena_ref, hsrca_ref, hlena_ref,
                    psb_ref, peb_ref, srcb_ref, dstb_ref, lenb_ref, hsrcb_ref, hlenb_ref,
                    tsa_hbm, tsb_hbm, wgu_hbm, wdn_hbm, bg_ref, bu_ref, bd_ref, perm_ref,
                    ys_ref, lhs_ref, wgu_buf, wdn_buf, wg_ref, wu_ref, wd_ref, sem_ref, wsem_ref,
                    *, nbits_a, nbits_b):
    j = pl.program_id(0)
    nb = pl.num_programs(0)
    slot = j % 2
    bm = lhs_ref.shape[1]

    def gather(blk, sl):
        for ts_hbm, ps, pe, src, dst, ln, hsrc, hlen, nbits in (
                (tsa_hbm, psa_ref, pea_ref, srca_ref, dsta_ref, lena_ref, hsrca_ref, hlena_ref, nbits_a),
                (tsb_hbm, psb_ref, peb_ref, srcb_ref, dstb_ref, lenb_ref, hsrcb_ref, hlenb_ref, nbits_b)):
            _start_pieces(ts_hbm, lhs_ref.at[sl], sem_ref.at[sl], ps[blk], pe[blk], src, dst, ln, nbits)
            _start_piece(ts_hbm, lhs_ref.at[sl], sem_ref.at[sl], hsrc[blk], 0, hlen[blk], nbits)

    def weight_copies(e, ws):
        return (pltpu.make_async_copy(wgu_hbm.at[e], wgu_buf.at[ws], wsem_ref.at[ws]),
                pltpu.make_async_copy(wdn_hbm.at[e], wdn_buf.at[ws], wsem_ref.at[ws]))

    @pl.when(j == 0)
    def _():
        lhs_ref[...] = jnp.zeros(lhs_ref.shape, F32)
        gather(0, 0)
        for cp in weight_copies(be_ref[0], wslot_ref[0]):
            cp.start()

    @pl.when(j + 1 < nb)
    def _():
        gather(j + 1, 1 - slot)

    @pl.when(jnp.logical_or(j == 0, be_ref[j] != be_ref[jnp.maximum(j - 1, 0)]))
    def _():
        ws = wslot_ref[j]
        for cp in weight_copies(be_ref[j], ws):
            cp.wait()
        nxt = nxt_ref[j]

        @pl.when(nxt >= 0)
        def _():
            for cp in weight_copies(nxt, 1 - ws):
                cp.start()

        perm = perm_ref[...]
        half = PERM_COLS // 2
        for c in range(2 * D_FF // PERM_COLS):
            wb = wgu_buf[ws, :, c * PERM_COLS:(c + 1) * PERM_COLS].astype(BF16)
            wp = jnp.dot(wb, perm, preferred_element_type=F32).astype(BF16)
            wg_ref[:, c * half:(c + 1) * half] = wp[:, :half]
            wu_ref[:, c * half:(c + 1) * half] = wp[:, half:]
        wd_ref[...] = wdn_buf[ws].astype(BF16)

    rows8 = rows_ref[j]
    _wait_rows(tsa_hbm, lhs_ref.at[slot], sem_ref.at[slot], rows8, (bm // SUBLANES).bit_length())

    def expert_rows(rs):
        x = lhs_ref[slot, rs, :].astype(BF16)
        gate = jnp.dot(x, wg_ref[...], preferred_element_type=F32) + bg_ref[0]
        up = jnp.dot(x, wu_ref[...], preferred_element_type=F32) + bu_ref[0]
        gate = jnp.minimum(gate, SWIGLU_LIMIT)
        up = jnp.clip(up, -SWIGLU_LIMIT, SWIGLU_LIMIT)
        act = (up + 1.0) * (gate * jax.nn.sigmoid(SWIGLU_ALPHA * gate))
        ys_ref[rs, :] = jnp.dot(act.astype(BF16), wd_ref[...], preferred_element_type=F32) + bd_ref[0]

    half8 = bm // 2 // SUBLANES

    @pl.when(rows8 > half8)
    def _():
        expert_rows(slice(0, bm))

    @pl.when(jnp.logical_and(rows8 > 0, rows8 <= half8))
    def _():
        expert_rows(slice(0, bm // 2))
        ys_ref[bm // 2:, :] = jnp.zeros((bm // 2, D_MODEL), F32)

    @pl.when(rows8 == 0)
    def _():
        ys_ref[...] = jnp.zeros(ys_ref.shape, F32)


def _moe_gmm(blocks, tabs_a, tabs_b, ts_a, ts_b, w_gu, w_dn, bg, bu, bd, perm, nblocks, bm, nbits_a, nbits_b):
    we = lambda j, be, *_: (be[j], 0, 0)
    grid_spec = pltpu.PrefetchScalarGridSpec(
        num_scalar_prefetch=18,
        grid=(nblocks,),
        in_specs=[pl.BlockSpec(memory_space=pl.ANY), pl.BlockSpec(memory_space=pl.ANY),
                  pl.BlockSpec(memory_space=pl.ANY), pl.BlockSpec(memory_space=pl.ANY),
                  pl.BlockSpec((1, 1, D_FF), we), pl.BlockSpec((1, 1, D_FF), we),
                  pl.BlockSpec((1, 1, D_MODEL), we),
                  pl.BlockSpec((PERM_COLS, PERM_COLS), lambda j, *_: (0, 0))],
        out_specs=pl.BlockSpec((bm, D_MODEL), lambda j, *_: (j, 0)),
        scratch_shapes=[pltpu.VMEM((2, bm, D_MODEL), F32),
                        pltpu.VMEM((2, D_MODEL, 2 * D_FF), F32), pltpu.VMEM((2, D_FF, D_MODEL), F32),
                        pltpu.VMEM((D_MODEL, D_FF), BF16), pltpu.VMEM((D_MODEL, D_FF), BF16),
                        pltpu.VMEM((D_FF, D_MODEL), BF16),
                        pltpu.SemaphoreType.DMA((2,)), pltpu.SemaphoreType.DMA((2,))],
    )
    return pl.pallas_call(
        functools.partial(_moe_gmm_kernel, nbits_a=nbits_a, nbits_b=nbits_b),
        grid_spec=grid_spec,
        out_shape=jax.ShapeDtypeStruct((nblocks * bm, D_MODEL), F32),
        compiler_params=_cparams(1),
        name="moe_gmm",
    )(*blocks, *tabs_a, *tabs_b, ts_a, ts_b, w_gu, w_dn, bg, bu, bd, perm)


def _combine_kernel(psrc_ref, pdst_ref, plen_ref, tlo_ref, thi_ref, tsrc_ref, tdst_ref, tlen_ref, rows_ref,
                    ys_hbm, x2_ref, dest_ref, gate_ref, o_ref,
                    buf_ref, db_ref, gb_ref, sem_ref, *, nbits):
    i = pl.program_id(0)
    n = pl.num_programs(0)
    slot = i % 2
    tt = x2_ref.shape[0]
    tile_rows = buf_ref.shape[1]

    def gather(tile, sl):
        _start_pieces(ys_hbm, buf_ref.at[sl], sem_ref.at[sl], tile * N_EXPERTS, (tile + 1) * N_EXPERTS,
                      psrc_ref, pdst_ref, plen_ref, nbits)
        _start_pieces(ys_hbm, buf_ref.at[sl], sem_ref.at[sl], tlo_ref[tile], thi_ref[tile],
                      tsrc_ref, tdst_ref, tlen_ref, nbits)

    @pl.when(i == 0)
    def _():
        buf_ref[...] = jnp.zeros(buf_ref.shape, F32)
        gather(0, 0)

    @pl.when(i + 1 < n)
    def _():
        gather(i + 1, 1 - slot)

    _wait_rows(ys_hbm, buf_ref.at[slot], sem_ref.at[slot], rows_ref[i], (tile_rows // SUBLANES).bit_length())

    dest = dest_ref[0]
    gate = gate_ref[0]
    for k in range(TOP_K):
        db_ref[k] = jnp.broadcast_to(dest[:, k:k + 1], (tt, DISPATCH_CHUNK)).astype(I16)
        gb_ref[k] = jnp.broadcast_to(gate[:, k:k + 1], (tt, DISPATCH_CHUNK)).astype(BF16)
    li = lax.broadcasted_iota(I32, (tt, DISPATCH_CHUNK), 1).astype(I16)
    gms = []
    for c in range(tile_rows // DISPATCH_CHUNK):
        lic = li + jnp.int16(c * DISPATCH_CHUNK)
        gm = jnp.zeros((tt, DISPATCH_CHUNK), BF16)
        for k in range(TOP_K):
            gm = jnp.where(lic == db_ref[k], gb_ref[k], gm)
        gms.append(gm)
    o_ref[...] = x2_ref[...] + jnp.dot(jnp.concatenate(gms, axis=1), buf_ref[slot].astype(BF16),
                                       preferred_element_type=F32)


def _combine(tabs, ys, x2, dest, gate, tt, nbits):
    n = x2.shape[0]
    nt = n // tt
    tile_rows = _tile_rows(tt)
    grid_spec = pltpu.PrefetchScalarGridSpec(
        num_scalar_prefetch=len(tabs),
        grid=(nt,),
        in_specs=[pl.BlockSpec(memory_space=pl.ANY),
                  pl.BlockSpec((tt, D_MODEL), lambda i, *_: (i, 0)),
                  pl.BlockSpec((1, tt, TOP_K), lambda i, *_: (i, 0, 0)),
                  pl.BlockSpec((1, tt, TOP_K), lambda i, *_: (i, 0, 0))],
        out_specs=pl.BlockSpec((tt, D_MODEL), lambda i, *_: (i, 0)),
        scratch_shapes=[pltpu.VMEM((2, tile_rows, D_MODEL), F32),
                        pltpu.VMEM((TOP_K, tt, DISPATCH_CHUNK), I16), pltpu.VMEM((TOP_K, tt, DISPATCH_CHUNK), BF16),
                        pltpu.SemaphoreType.DMA((2,))],
    )
    return pl.pallas_call(
        functools.partial(_combine_kernel, nbits=nbits),
        grid_spec=grid_spec,
        out_shape=jax.ShapeDtypeStruct((n, D_MODEL), F32),
        compiler_params=_cparams(1),
        name="combine",
    )(*tabs, ys, x2, dest, gate)


def _piece_tables(n8_a, off_a, rows_a, n8_b, off_b, bm, nblocks):
    nta = n8_a.shape[0]
    n8 = jnp.concatenate([n8_a, n8_b], axis=0)
    seg_off = jnp.concatenate([off_a, off_b], axis=0)
    n_tiles = n8.shape[0]
    tile_base = jnp.concatenate([jnp.arange(nta, dtype=I32) * rows_a, jnp.zeros((n_tiles - nta,), I32)])[:, None]
    tot = jnp.sum(n8, axis=0)
    pos0 = jnp.cumsum(n8, axis=0) - n8
    nblk = (tot + bm - 1) // bm
    cs = jnp.cumsum(nblk)
    bs = cs - nblk
    kblk = pos0 // bm
    len0 = jnp.minimum(n8, (kblk + 1) * bm - pos0)
    len1 = n8 - len0
    b0 = bs[None, :] + kblk
    src0 = tile_base + seg_off
    in_blk = pos0 - kblk * bm
    jj = jnp.arange(nblocks, dtype=I32)
    i32 = lambda v: v.astype(I32)

    def gmm_tabs(sl):
        em = lambda v: v[sl].T.reshape(-1)
        blk_em = em(b0)
        first = i32(jnp.sum(blk_em[None, :] < jj[:, None], axis=1))
        last = i32(jnp.sum(blk_em[None, :] <= jj[:, None], axis=1))
        hit = (blk_em[None, :] + 1 == jj[:, None]) & (em(len1)[None, :] > 0)
        tail_src = i32(jnp.sum(jnp.where(hit, em(src0 + len0)[None, :], 0), axis=1))
        tail_len = i32(jnp.sum(jnp.where(hit, em(len1)[None, :], 0), axis=1) // SUBLANES)
        return first, last, i32(em(src0)), i32(em(in_blk)), i32(em(len0) // SUBLANES), tail_src, tail_len

    def comb_tabs(sl):
        tm = lambda v: v[sl].reshape(-1)
        has_tail = len1[sl] > 0
        cnt = jnp.sum(has_tail, axis=1)
        lo = jnp.cumsum(cnt) - cnt
        slot_ = lo[:, None] + jnp.cumsum(has_tail, axis=1) - has_tail
        hit = (slot_.reshape(-1)[None, :] == jj[:, None]) & has_tail.reshape(-1)[None, :]
        pick = lambda v: i32(jnp.sum(jnp.where(hit, tm(v)[None, :], 0), axis=1))
        return (i32(tm(b0 * bm + in_blk)), i32(tm(seg_off)), i32(tm(len0) // SUBLANES),
                i32(lo), i32(lo + cnt), pick((b0 + 1) * bm), pick(seg_off + len0), pick(len1 // SUBLANES),
                i32(jnp.sum(n8[sl], axis=1) // SUBLANES))

    count_le = lambda v: jnp.sum(cs[None, :] <= v[:, None], axis=1)
    n_active = cs[-1]
    e_last = count_le(jnp.maximum(n_active - 1, 0).reshape(1))[0]
    block_e = jnp.minimum(count_le(jj), e_last).astype(I32)
    ee = jnp.arange(N_EXPERTS, dtype=I32)
    mine = (jj[:, None] >= bs[None, :]) & (jj[:, None] < cs[None, :])
    left = jnp.clip(tot[None, :] - (jj[:, None] - bs[None, :]) * bm, 0, bm)
    rows8 = (jnp.sum(jnp.where(mine, left, 0), axis=1) // SUBLANES).astype(I32)
    has = nblk > 0
    run = jnp.cumsum(has.astype(I32)) - 1
    later = (ee[None, :] > ee[:, None]) & has[None, :]
    nxt_e = jnp.min(jnp.where(later, ee[None, :], N_EXPERTS), axis=1)
    nxt_e = jnp.where(nxt_e == N_EXPERTS, -1, nxt_e)
    own = block_e[:, None] == ee[None, :]
    wslot = (jnp.sum(jnp.where(own, run[None, :], 0), axis=1) % 2).astype(I32)
    nxt = jnp.sum(jnp.where(own, nxt_e[None, :], 0), axis=1).astype(I32)
    a, b = slice(0, nta), slice(nta, n_tiles)
    return (block_e, rows8, wslot, nxt), gmm_tabs(a), gmm_tabs(b), comb_tabs(a), comb_tabs(b)


def _block_diag(w):
    nb, bi, bo = w.shape
    eye = jnp.eye(nb, dtype=w.dtype)
    return (eye[:, None, :, None] * w[:, :, None, :]).reshape(nb * bi, nb * bo)


def _step(x_prompt, x_sample, cache_k, cache_v, state_conv, state_h, g_mix_norm, w_in, g_q_norm, g_k_norm,
          attn_sinks, conv_w, conv_b, w_lru_a, b_lru_a, w_lru_x, b_lru_x, lru_lambda, g_attn_out, g_rnn_out,
          w_out, g_ffn_norm, w_router, b_router, w_gate_up, b_gate_up, w_down, b_down,
          *, tm, tt, tc, bm, past_len):
    B, S, D = x_prompt.shape
    NS = x_sample.shape[0]
    assert x_sample.shape[1] == 1 and D == D_MODEL
    assert (B * S) % tt == 0 and (B * S) % tm == 0 and S % tc == 0 and S % ATTN_BLOCK == 0 and S % tm == 0
    assert NS % SUBLANES == 0 and tt <= bm
    assert tt % SUBLANES == 0 and NS <= bm
    n_pt = (B * S) // tt
    total_rows = TOP_K * (B * S + NS) + (n_pt + 1) * N_EXPERTS * (SUBLANES - 1)
    nblocks = -(-total_rows // bm) + N_EXPERTS
    nbits_p = (tt // SUBLANES).bit_length()
    nbits_s = (NS // SUBLANES).bit_length()

    l = 0
    row = lambda v: v[l].reshape(1, -1)
    w_in_bf = w_in[l].astype(BF16)
    gq2 = jnp.tile(g_q_norm[l], 2).reshape(1, LANES)
    gk2 = jnp.tile(g_k_norm[l], 2).reshape(1, LANES)
    wa = _block_diag(w_lru_a[l]).astype(BF16)
    wx = _block_diag(w_lru_x[l]).astype(BF16)
    ba = b_lru_a[l].reshape(1, D_RNN)
    bx = b_lru_x[l].reshape(1, D_RNN)
    wo = w_out[l].astype(BF16)
    woa, wor = wo[:D_ATTN], wo[D_ATTN:]
    wr = w_router[l].T
    wr_hi = wr.astype(BF16)
    wrt = jnp.concatenate([wr_hi, (wr - wr_hi.astype(F32)).astype(BF16)], axis=0)
    br = b_router[l].reshape(N_EXPERTS, 1)
    low = jnp.tril(jnp.ones((N_EXPERTS, N_EXPERTS), BF16), k=-1)
    bgu = b_gate_up[l].reshape(N_EXPERTS, D_FF, 2)
    bg = bgu[:, :, 0].reshape(N_EXPERTS, 1, D_FF)
    bu = bgu[:, :, 1].reshape(N_EXPERTS, 1, D_FF)
    bd = b_down[l].reshape(N_EXPERTS, 1, D_MODEL)
    half = PERM_COLS // 2
    pr = jnp.arange(PERM_COLS)
    perm = (pr[None, :] == jnp.where(pr % 2 == 0, pr // 2, half + pr // 2)[:, None]).astype(BF16)
    sinks = attn_sinks[l]

    ctab, s1tab, s2tab = _rope_tables(jnp.arange(S))
    an, rn, kt_p, vt_p, h_last_p, xr_tail = _front(
        x_prompt.reshape(B * S, D), sinks, row(g_mix_norm), w_in_bf, gq2, gk2, ctab, s1tab, s2tab,
        row(g_attn_out), conv_w[l], row(conv_b), wa, ba, wx, bx, row(lru_lambda), row(g_rnn_out), B, S, tm)
    x2_p, ts_p, dest_p, gate_p, n8_p, off_p = _mix_route(
        x_prompt.reshape(B * S, D), an, rn, woa, wor, row(g_ffn_norm), wrt, br, low, tt)

    cs_tab = _rope_tables(jnp.full((NS,), past_len, I32))
    q_s, k_s, v_s, xr_s, yr_s = _in_proj(x_sample.reshape(NS, D), row(g_mix_norm), w_in_bf, gq2, gk2,
                                         *cs_tab, NS)
    to_rows = lambda c: jnp.transpose(c, (0, 2, 3, 1)).reshape(NS * KV_W, WINDOW)
    from_rows = lambda c, n: jnp.transpose(c.reshape(n, N_KV_HEADS, HEAD_DIM, WINDOW), (0, 3, 1, 2))[None]
    an_s, kt_s, vt_s = _attn_sample(q_s, k_s, v_s, to_rows(cache_k[l]), to_rows(cache_v[l]), sinks,
                                    row(g_attn_out), SUBLANES)
    rn_s, h_last_s, hist_s = _rnn_sample(xr_s, yr_s, jnp.transpose(state_conv[l], (1, 0, 2)), state_h[l],
                                         conv_w[l], row(conv_b), wa, ba, wx, bx, row(lru_lambda),
                                         row(g_rnn_out))
    x2_s, ts_s, dest_s, gate_s, n8_s, off_s = _mix_route(
        x_sample.reshape(NS, D), an_s, rn_s, woa, wor, row(g_ffn_norm), wrt, br, low, NS)

    blocks, gmm_p, gmm_s, comb_p, comb_s = _piece_tables(
        n8_p[:, :, 0], off_p[:, :, 0], _tile_rows(tt), n8_s[:, :, 0], off_s[:, :, 0], bm, nblocks)
    ys = _moe_gmm(blocks, gmm_p, gmm_s, ts_p, ts_s, w_gate_up[l], w_down[l], bg, bu, bd, perm,
                  nblocks, bm, nbits_p, nbits_s)
    tr = lambda a: jnp.transpose(a, (0, 2, 1))
    y_p = _combine(comb_p, ys, x2_p, tr(dest_p), tr(gate_p), tt, nbits_p)
    y_s = _combine(comb_s, ys, x2_s, tr(dest_s), tr(gate_s), NS, nbits_s)

    cp = xr_tail[:, SUBLANES - (CONV_WIDTH - 1):]
    return (y_p.reshape(B, S, D), y_s.reshape(NS, 1, D),
            from_rows(kt_p, B), from_rows(vt_p, B), cp[None], h_last_p.reshape(1, B, D_RNN),
            from_rows(kt_s, NS), from_rows(vt_s, NS), jnp.transpose(hist_s, (1, 0, 2))[None], h_last_s[None])


def kernel(x_prompt, x_sample, cache_k, cache_v, state_conv, state_h, g_mix_norm, w_in, g_q_norm, g_k_norm, attn_sinks, conv_w, conv_b, w_lru_a, b_lru_a, w_lru_x, b_lru_x, lru_lambda, g_attn_out, g_rnn_out, w_out, g_ffn_norm, w_router, b_router, w_gate_up, b_gate_up, w_down, b_down):
    return _step(x_prompt, x_sample, cache_k, cache_v, state_conv, state_h, g_mix_norm, w_in, g_q_norm,
                 g_k_norm, attn_sinks, conv_w, conv_b, w_lru_a, b_lru_a, w_lru_x, b_lru_x, lru_lambda,
                 g_attn_out, g_rnn_out, w_out, g_ffn_norm, w_router, b_router, w_gate_up, b_gate_up,
                 w_down, b_down, tm=512, tt=512, tc=256, bm=MOE_BLOCK_ROWS, past_len=PAST_LEN)
```

```python
import functools

import jax
import jax.numpy as jnp
from jax import lax
from jax.experimental import pallas as pl
from jax.experimental.pallas import tpu as pltpu

F32 = jnp.float32
BF16 = jnp.bfloat16
I32 = jnp.int32
I16 = jnp.int16

D_MODEL = 1024
HEAD_DIM = 64
N_HEADS = 8
N_KV_HEADS = 2
GROUP = 4
WINDOW = 128
ATTN_BLOCK = 128
ROT_DIM = 16
ROPE_THETA = 500000.0
D_ATTN = 512
D_RNN = 512
KV_W = 128
D_IN = 1792
CONV_WIDTH = 4
LRU_C = 8.0
N_EXPERTS = 32
TOP_K = 4
D_FF = 1024
SWIGLU_LIMIT = 7.0
SWIGLU_ALPHA = 1.702
EPS = 1e-6
PAST_LEN = 8192

LANES = 128
SUBLANES = 8
NEG_BIG = -1e30
VMEM_LIMIT = 56 * 1024 * 1024

MOE_BLOCK_ROWS = 512
PERM_COLS = 256
DISPATCH_CHUNK = 256


def _cparams(n_axes):
    return pltpu.CompilerParams(dimension_semantics=("arbitrary",) * n_axes,
                                vmem_limit_bytes=VMEM_LIMIT)


U32 = jnp.uint32
HALF = D_MODEL // 2
HI16 = 0xFFFF0000


def _pack_bf16_pairs(x):
    bits = pltpu.bitcast(x, U32)
    return (bits[:, HALF:] & U32(HI16)) | (bits[:, :HALF] >> 16)


def _unpack_bf16_pairs(u):
    lo = pltpu.bitcast(u << 16, F32)
    hi = pltpu.bitcast(u & U32(HI16), F32)
    return jnp.concatenate([lo, hi], axis=1).astype(BF16)


def _rmsnorm(x, g):
    ms = jnp.mean(x * x, axis=-1, keepdims=True)
    return (x * lax.rsqrt(ms + EPS)) * g


def _head_norm_rope(t, g, c, s1, s2, lo):
    sq = t * t
    s_lo = jnp.sum(jnp.where(lo, sq, 0.0), axis=-1, keepdims=True)
    s_hi = jnp.sum(jnp.where(lo, 0.0, sq), axis=-1, keepdims=True)
    ms = jnp.where(lo, s_lo, s_hi) * (1.0 / HEAD_DIM)
    n = (t * lax.rsqrt(ms + EPS)) * g
    up = pltpu.roll(n, LANES - ROT_DIM // 2, 1)
    dn = pltpu.roll(n, ROT_DIM // 2, 1)
    return n * c + up * s1 + dn * s2


def _in_proj_kernel(x_ref, g_ref, w_ref, gq_ref, gk_ref, c_ref, s1_ref, s2_ref,
                    q_ref, k_ref, v_ref, xr_ref, yr_ref):
    tm = x_ref.shape[0]
    h = _rmsnorm(x_ref[...], g_ref[...])
    proj = jnp.dot(h.astype(BF16), w_ref[...], preferred_element_type=F32)
    rope = (c_ref[...], s1_ref[...], s2_ref[...], lax.broadcasted_iota(I32, (tm, LANES), 1) < HEAD_DIM)
    gq = gq_ref[...]
    for j in range(D_ATTN // LANES):
        q_ref[:, j * LANES:(j + 1) * LANES] = _head_norm_rope(proj[:, j * LANES:(j + 1) * LANES], gq, *rope)
    k_ref[...] = _head_norm_rope(proj[:, D_ATTN:D_ATTN + KV_W], gk_ref[...], *rope)
    v_ref[...] = proj[:, D_ATTN + KV_W:D_ATTN + 2 * KV_W]
    o = D_ATTN + 2 * KV_W
    xr_ref[...] = proj[:, o:o + D_RNN]
    yr_ref[...] = proj[:, o + D_RNN:o + 2 * D_RNN]


def _in_proj(x2d, g, w_bf, gq2, gk2, ctab, s1tab, s2tab, tm):
    n = x2d.shape[0]
    ntab = ctab.shape[0] // tm
    row = lambda i: (i, 0)
    fix = lambda i: (0, 0)
    tab = lambda i: (i % ntab, 0)
    out_shapes = (jax.ShapeDtypeStruct((n, D_ATTN), F32), jax.ShapeDtypeStruct((n, KV_W), F32),
                  jax.ShapeDtypeStruct((n, KV_W), F32), jax.ShapeDtypeStruct((n, D_RNN), F32),
                  jax.ShapeDtypeStruct((n, D_RNN), F32))
    return pl.pallas_call(
        _in_proj_kernel,
        grid=(n // tm,),
        in_specs=[pl.BlockSpec((tm, D_MODEL), row), pl.BlockSpec((1, D_MODEL), fix),
                  pl.BlockSpec((D_MODEL, D_IN), fix), pl.BlockSpec((1, LANES), fix),
                  pl.BlockSpec((1, LANES), fix), pl.BlockSpec((tm, LANES), tab),
                  pl.BlockSpec((tm, LANES), tab), pl.BlockSpec((tm, LANES), tab)],
        out_specs=(pl.BlockSpec((tm, D_ATTN), row), pl.BlockSpec((tm, KV_W), row),
                   pl.BlockSpec((tm, KV_W), row), pl.BlockSpec((tm, D_RNN), row),
                   pl.BlockSpec((tm, D_RNN), row)),
        out_shape=out_shapes,
        compiler_params=_cparams(1),
        name="in_proj",
    )(x2d, g, w_bf, gq2, gk2, ctab, s1tab, s2tab)


def _rope_tables(pos):
    half = ROT_DIM // 2
    inv = ROPE_THETA ** (-jnp.arange(0, ROT_DIM, 2, dtype=F32) / ROT_DIM)
    ang = pos.astype(F32)[:, None] * inv[None, :]
    cos = jnp.cos(ang)
    sin = jnp.sin(ang)
    n = pos.shape[0]
    ones = jnp.ones((n, HEAD_DIM - ROT_DIM), F32)
    zeros = jnp.zeros((n, HEAD_DIM - ROT_DIM), F32)
    zh = jnp.zeros((n, half), F32)
    c = jnp.concatenate([cos, cos, ones], axis=1)
    s1 = jnp.concatenate([-sin, zh, zeros], axis=1)
    s2 = jnp.concatenate([zh, sin, zeros], axis=1)
    two = lambda t: jnp.concatenate([t, t], axis=1)
    return two(c), two(s1), two(s2)


def _band_bias(qb):
    qi = jnp.arange(qb, dtype=I32)[:, None]
    c = jnp.arange(2 * qb, dtype=I32)[None, :]
    band = (c >= qi) & (c <= qi + qb)
    first = band & (c >= qb)
    one = jnp.where(jnp.stack([first, band]), 0.0, NEG_BIG).astype(F32)
    return jnp.concatenate([one, one], axis=2)


def _attn_prompt_kernel(sink_ref, q_ref, kc_ref, kp_ref, vc_ref, vp_ref, bias_ref, g_ref,
                        o_ref, kt_ref, vt_ref, s_ref, e_ref):
    j = pl.program_id(1)
    qb = ATTN_BLOCK
    kc = kc_ref[...]
    vc = vc_ref[...]
    k2 = jnp.concatenate([kp_ref[...], kc], axis=0)
    v2 = jnp.concatenate([vp_ref[...], vc], axis=0)
    k2r = pltpu.roll(k2, HEAD_DIM, 1)
    v2r = pltpu.roll(v2, HEAD_DIM, 1)
    lo_k = lax.broadcasted_iota(I32, (2 * qb, LANES), 1) < HEAD_DIM
    lo_q = lax.broadcasted_iota(I32, (qb, LANES), 1) < HEAD_DIM
    bias = bias_ref[0]
    nt = (((1,), (1,)), ((), ()))
    n_pairs = N_HEADS // 2
    kbd, vbd = [], []
    for kv in range(N_KV_HEADS):
        ka, kb = (k2, k2r) if kv == 0 else (k2r, k2)
        va, vb = (v2, v2r) if kv == 0 else (v2r, v2)
        kbd.append(jnp.concatenate([jnp.where(lo_k, ka, 0.0), jnp.where(lo_k, 0.0, kb)], axis=0).astype(BF16))
        vbd.append(jnp.concatenate([jnp.where(lo_k, va, 0.0), jnp.where(lo_k, 0.0, vb)], axis=0).astype(BF16))
    for pp in range(n_pairs):
        qp = (q_ref[:, pp * LANES:(pp + 1) * LANES] * (HEAD_DIM ** -0.5)).astype(BF16)
        s_ref[pp] = lax.dot_general(qp, kbd[pp // (GROUP // 2)], nt, preferred_element_type=F32) + bias
    invs = []
    for pp in range(n_pairs):
        inv = []
        for t in range(2):
            cols = slice(t * 2 * qb, (t + 1) * 2 * qb)
            st = s_ref[pp, :, cols]
            sink = sink_ref[2 * pp + t]
            m = jnp.maximum(jnp.max(st, axis=-1, keepdims=True), sink)
            e = jnp.exp(st - m)
            e_ref[pp, :, cols] = e.astype(BF16)
            inv.append(1.0 / (jnp.sum(e, axis=-1, keepdims=True) + jnp.exp(sink - m)))
        invs.append(jnp.where(lo_q, inv[0], inv[1]))
    outs = [jnp.dot(e_ref[pp], vbd[pp // (GROUP // 2)], preferred_element_type=F32) * invs[pp]
            for pp in range(n_pairs)]
    o_ref[...] = _rmsnorm(jnp.concatenate(outs, axis=1), g_ref[...])

    @pl.when(j == pl.num_programs(1) - 1)
    def _():
        kt_ref[0] = kc.T
        vt_ref[0] = vc.T


def _attn_prompt(q, k, v, sinks, g_attn, batch, seq):
    qb = ATTN_BLOCK
    nb = seq // qb
    cur = lambda b, j: (b * nb + j, 0)
    prev = lambda b, j: (b * nb + jnp.maximum(j - 1, 0), 0)
    fix = lambda b, j: (0, 0)
    per_b = lambda b, j: (b, 0, 0)
    return pl.pallas_call(
        _attn_prompt_kernel,
        grid=(batch, nb),
        in_specs=[pl.BlockSpec(memory_space=pltpu.SMEM),
                  pl.BlockSpec((qb, D_ATTN), cur),
                  pl.BlockSpec((qb, KV_W), cur), pl.BlockSpec((qb, KV_W), prev),
                  pl.BlockSpec((qb, KV_W), cur), pl.BlockSpec((qb, KV_W), prev),
                  pl.BlockSpec((1, qb, 4 * qb), lambda b, j: (jnp.minimum(j, 1), 0, 0)),
                  pl.BlockSpec((1, D_ATTN), fix)],
        out_specs=(pl.BlockSpec((qb, D_ATTN), cur), pl.BlockSpec((1, KV_W, qb), per_b),
                   pl.BlockSpec((1, KV_W, qb), per_b)),
        out_shape=(jax.ShapeDtypeStruct((batch * seq, D_ATTN), F32),
                   jax.ShapeDtypeStruct((batch, KV_W, qb), F32),
                   jax.ShapeDtypeStruct((batch, KV_W, qb), F32)),
        scratch_shapes=[pltpu.VMEM((N_HEADS // 2, qb, 4 * qb), F32),
                        pltpu.VMEM((N_HEADS // 2, qb, 4 * qb), BF16)],
        compiler_params=_cparams(2),
        name="attn_prompt",
    )(sinks, q, k, k, v, v, _band_bias(qb), g_attn)


def _attn_sample_kernel(sink_ref, q_ref, kn_ref, vn_ref, kt_ref, vt_ref, g_ref,
                        o_ref, nkt_ref, nvt_ref, acc_ref):
    bb = q_ref.shape[0]
    q = q_ref[...] * (HEAD_DIM ** -0.5)
    kn = kn_ref[...]
    vn = vn_ref[...]
    kt = kt_ref[...]
    vt = vt_ref[...]
    col = lax.broadcasted_iota(I32, (bb, bb * KV_W), 1)
    rowb = lax.broadcasted_iota(I32, (bb, bb * KV_W), 0)
    own_seq = (col >> (KV_W.bit_length() - 1)) == rowb
    half_hi = ((col >> (HEAD_DIM.bit_length() - 1)) & 1) == 1
    qbig = []
    for h in range(N_HEADS):
        kv = h // GROUP
        pair = q[:, (h // 2) * LANES:(h // 2 + 1) * LANES]
        if (h % 2) != kv:
            pair = pltpu.roll(pair, HEAD_DIM, 1)
        tiled = jnp.concatenate([pair] * bb, axis=1)
        keep = own_seq & (half_hi if kv == 1 else jnp.logical_not(half_hi))
        qbig.append(jnp.where(keep, tiled, 0.0))
    qbig = jnp.concatenate(qbig, axis=0)
    s = jnp.dot(qbig.astype(BF16), kt.astype(BF16), preferred_element_type=F32)
    qb16 = q.astype(BF16).astype(F32)
    kb16 = kn.astype(BF16).astype(F32)
    s_new, sink = [], []
    for h in range(N_HEADS):
        kv = h // GROUP
        s_new.append(jnp.sum(qb16[:, h * HEAD_DIM:(h + 1) * HEAD_DIM] * kb16[:, kv * HEAD_DIM:(kv + 1) * HEAD_DIM],
                             axis=-1, keepdims=True))
        sink.append(jnp.full((bb, 1), sink_ref[h], F32))
    s_new = jnp.concatenate(s_new, axis=0)
    sink = jnp.concatenate(sink, axis=0)
    m = jnp.maximum(jnp.maximum(jnp.max(s, axis=-1, keepdims=True), s_new), sink)
    e = jnp.exp(s - m)
    e_new = jnp.exp(s_new - m)
    inv = 1.0 / (jnp.sum(e, axis=-1, keepdims=True) + e_new + jnp.exp(sink - m))
    obig = lax.dot_general(e.astype(BF16), vt.astype(BF16), (((1,), (1,)), ((), ())),
                           preferred_element_type=F32)
    for h in range(N_HEADS):
        kv = h // GROUP
        blk = jnp.where(own_seq, obig[h * bb:(h + 1) * bb, :], 0.0)
        fold = blk[:, 0:KV_W]
        for t in range(1, bb):
            fold = fold + blk[:, t * KV_W:(t + 1) * KV_W]
        hs = slice(h * bb, (h + 1) * bb)
        ks = slice(kv * HEAD_DIM, (kv + 1) * HEAD_DIM)
        acc_ref[:, h * HEAD_DIM:(h + 1) * HEAD_DIM] = (fold[:, ks] + e_new[hs] * vn[:, ks]) * inv[hs]
    o_ref[...] = _rmsnorm(acc_ref[...], g_ref[...])

    last = lax.broadcasted_iota(I32, (KV_W, WINDOW), 1) == WINDOW - 1
    for b in range(bb):
        rs = slice(b * KV_W, (b + 1) * KV_W)
        kcol = jnp.broadcast_to(kn[b:b + 1, :], (KV_W, KV_W)).T
        vcol = jnp.broadcast_to(vn[b:b + 1, :], (KV_W, KV_W)).T
        nkt_ref[rs, :] = jnp.where(last, kcol, pltpu.roll(kt[rs, :], WINDOW - 1, 1))
        nvt_ref[rs, :] = jnp.where(last, vcol, pltpu.roll(vt[rs, :], WINDOW - 1, 1))


def _attn_sample(q, kn, vn, kt2d, vt2d, sinks, g_attn, bb):
    n = q.shape[0]
    row = lambda i: (i, 0)
    fix = lambda i: (0, 0)
    cache = pl.BlockSpec((bb * KV_W, WINDOW), row)
    return pl.pallas_call(
        _attn_sample_kernel,
        grid=(n // bb,),
        in_specs=[pl.BlockSpec(memory_space=pltpu.SMEM),
                  pl.BlockSpec((bb, D_ATTN), row), pl.BlockSpec((bb, KV_W), row),
                  pl.BlockSpec((bb, KV_W), row), cache, cache,
                  pl.BlockSpec((1, D_ATTN), fix)],
        out_specs=(pl.BlockSpec((bb, D_ATTN), row), cache, cache),
        out_shape=(jax.ShapeDtypeStruct((n, D_ATTN), F32),
                   jax.ShapeDtypeStruct(kt2d.shape, F32), jax.ShapeDtypeStruct(vt2d.shape, F32)),
        scratch_shapes=[pltpu.VMEM((bb, D_ATTN), F32)],
        compiler_params=_cparams(1),
        name="attn_sample",
    )(sinks, q, kn, vn, kt2d, vt2d, g_attn)


def _softplus(z):
    return jnp.maximum(z, 0.0) + jnp.log1p(jnp.exp(-jnp.abs(z)))


def _lru_gates(xc, wa_ref, ba_ref, wx_ref, bx_ref, lam_ref):
    xb = xc.astype(BF16)
    r = jax.nn.sigmoid(jnp.dot(xb, wa_ref[...], preferred_element_type=F32) + ba_ref[...])
    i = jax.nn.sigmoid(jnp.dot(xb, wx_ref[...], preferred_element_type=F32) + bx_ref[...])
    log_a = (-LRU_C * r) * _softplus(-lam_ref[...])
    a = jnp.exp(log_a)
    z = -jnp.tanh(log_a) * (a * a + 1.0)
    u = jnp.where(z > 0.0, z * lax.rsqrt(z), 0.0) * (i * xc)
    return a, u


def _lru_scan(a, u, h0):
    ng = a.shape[0] // SUBLANES
    a3 = a.reshape(ng, SUBLANES, D_RNN)
    u3 = u.reshape(ng, SUBLANES, D_RNN)
    t8 = lax.broadcasted_iota(I32, (ng, SUBLANES, D_RNN), 1)
    d = 1
    while d < SUBLANES:
        a_s = jnp.where(t8 >= d, pltpu.roll(a3, d, 1), 1.0)
        u_s = jnp.where(t8 >= d, pltpu.roll(u3, d, 1), 0.0)
        u3 = a3 * u_s + u3
        a3 = a3 * a_s
        d *= 2
    carry = h0
    groups = []
    for g in range(ng):
        hg = a3[g] * carry + u3[g]
        groups.append(hg)
        carry = hg[SUBLANES - 1:SUBLANES, :]
    return jnp.concatenate(groups, axis=0), carry


def _lru_scan_tiles(a_ref, u_ref, h_ref, h0):
    nl, rows, _ = a_ref.shape
    ng = rows // SUBLANES
    step = lambda ref, s: jnp.concatenate(
        [ref[j, pl.ds(s, ng, stride=SUBLANES), :] for j in range(nl)], axis=1)
    prods = [step(a_ref, 0)]
    locs = [step(u_ref, 0)]
    for s in range(1, SUBLANES):
        a_s = step(a_ref, s)
        locs.append(a_s * locs[-1] + step(u_ref, s))
        prods.append(a_s * prods[-1])
    after, h_last = _lru_scan(prods[-1], locs[-1], h0)
    row = lax.broadcasted_iota(I32, (ng, D_RNN), 0)
    before = jnp.where(row == 0, h0, pltpu.roll(after, 1, 0))
    for s in range(SUBLANES):
        h_s = locs[s] + prods[s] * before
        for j in range(nl):
            h_ref[j, pl.ds(s, ng, stride=SUBLANES), :] = h_s[:, j * LANES:(j + 1) * LANES]
    return h_last


def _to_lane_tiles(ref, x):
    for j in range(ref.shape[0]):
        ref[j] = x[:, j * LANES:(j + 1) * LANES]


def _rnn_prompt_kernel(xr_ref, yr_ref, cw_ref, cb_ref, wa_ref, ba_ref, wx_ref, bx_ref, lam_ref, g_ref,
                       o_ref, hl_ref, ext_ref, h_ref):
    c = pl.program_id(1)
    tc = xr_ref.shape[0]
    pad = SUBLANES

    @pl.when(c == 0)
    def _():
        ext_ref[0:pad, :] = jnp.zeros((pad, D_RNN), F32)
        h_ref[...] = jnp.zeros((1, D_RNN), F32)

    ext_ref[pad:pad + tc, :] = xr_ref[...]
    cw = cw_ref[...]
    xc = cb_ref[...] + ext_ref[pad:pad + tc, :] * cw[CONV_WIDTH - 1:CONV_WIDTH, :]
    for w in range(CONV_WIDTH - 1):
        sh = CONV_WIDTH - 1 - w
        xc = xc + ext_ref[pad - sh:pad - sh + tc, :] * cw[w:w + 1, :]
    ext_ref[0:pad, :] = ext_ref[tc:tc + pad, :]

    a, u = _lru_gates(xc, wa_ref, ba_ref, wx_ref, bx_ref, lam_ref)
    h, carry = _lru_scan(a, u, h_ref[...])
    h_ref[...] = carry
    hl_ref[0] = carry
    o_ref[...] = _rmsnorm(jax.nn.gelu(yr_ref[...]) * h, g_ref[...])


def _rnn_prompt(xr, yr, cw, cb, wa, ba, wx, bx, lam, g, batch, seq, tc):
    nc = seq // tc
    cur = lambda b, c: (b * nc + c, 0)
    fix = lambda b, c: (0, 0)
    vec = pl.BlockSpec((1, D_RNN), fix)
    return pl.pallas_call(
        _rnn_prompt_kernel,
        grid=(batch, nc),
        in_specs=[pl.BlockSpec((tc, D_RNN), cur), pl.BlockSpec((tc, D_RNN), cur),
                  pl.BlockSpec((CONV_WIDTH, D_RNN), fix), vec,
                  pl.BlockSpec((D_RNN, D_RNN), fix), vec,
                  pl.BlockSpec((D_RNN, D_RNN), fix), vec, vec, vec],
        out_specs=(pl.BlockSpec((tc, D_RNN), cur), pl.BlockSpec((1, 1, D_RNN), lambda b, c: (b, 0, 0))),
        out_shape=(jax.ShapeDtypeStruct((batch * seq, D_RNN), F32),
                   jax.ShapeDtypeStruct((batch, 1, D_RNN), F32)),
        scratch_shapes=[pltpu.VMEM((tc + SUBLANES, D_RNN), F32), pltpu.VMEM((1, D_RNN), F32)],
        compiler_params=_cparams(2),
        name="rnn_prompt",
    )(xr, yr, cw, cb, wa, ba, wx, bx, lam, g)


def _rnn_sample_kernel(xr_ref, yr_ref, hist_ref, h0_ref, cw_ref, cb_ref, wa_ref, ba_ref, wx_ref, bx_ref,
                       lam_ref, g_ref, o_ref, hl_ref, nh_ref):
    cw = cw_ref[...]
    xr = xr_ref[...]
    xc = cb_ref[...] + xr * cw[CONV_WIDTH - 1:CONV_WIDTH, :]
    for w in range(CONV_WIDTH - 1):
        xc = xc + hist_ref[w] * cw[w:w + 1, :]
    a, u = _lru_gates(xc, wa_ref, ba_ref, wx_ref, bx_ref, lam_ref)
    h = a * h0_ref[...] + u
    hl_ref[...] = h
    o_ref[...] = _rmsnorm(jax.nn.gelu(yr_ref[...]) * h, g_ref[...])
    for w in range(CONV_WIDTH - 2):
        nh_ref[w] = hist_ref[w + 1]
    nh_ref[CONV_WIDTH - 2] = xr


def _rnn_sample(xr, yr, hist, h0, cw, cb, wa, ba, wx, bx, lam, g):
    n = xr.shape[0]
    full = lambda a: pl.BlockSpec(a.shape, lambda: (0,) * a.ndim)
    args = (xr, yr, hist, h0, cw, cb, wa, ba, wx, bx, lam, g)
    return pl.pallas_call(
        _rnn_sample_kernel,
        in_specs=[full(a) for a in args],
        out_specs=(pl.BlockSpec((n, D_RNN), lambda: (0, 0)), pl.BlockSpec((n, D_RNN), lambda: (0, 0)),
                   pl.BlockSpec(hist.shape, lambda: (0, 0, 0))),
        out_shape=(jax.ShapeDtypeStruct((n, D_RNN), F32), jax.ShapeDtypeStruct((n, D_RNN), F32),
                   jax.ShapeDtypeStruct(hist.shape, F32)),
        compiler_params=pltpu.CompilerParams(vmem_limit_bytes=VMEM_LIMIT),
        name="rnn_sample",
    )(*args)


def _front_kernel(*refs, tiles_per_seq):
    i = pl.program_id(0)
    q_s, k_s, v_s, xr_s, yr_s, ext_ref, h_ref = refs[25:32]

    @pl.when(i == 0)
    def _():
        for r in (q_s, k_s, v_s, xr_s, yr_s, ext_ref, h_ref):
            r[...] = jnp.zeros(r.shape, F32)

    for cur in range(2):
        @pl.when(i % 2 == cur)
        def _():
            _front_body(cur, 1 - cur, *refs, tiles_per_seq=tiles_per_seq)


def _front_body(cur, prv, sink_ref, x_ref, gm_ref, w_ref, gq_ref, gk_ref, c_ref, s1_ref, s2_ref, bias_ref,
                ga_ref, cw_ref, cb_ref, wa_ref, ba_ref, wx_ref, bx_ref, lam_ref, gr_ref,
                an_ref, rn_ref, kt_ref, vt_ref, hl_ref, cx_ref,
                q_s, k_s, v_s, xr_s, yr_s, ext_ref, h_ref, s_ref, e_ref, a_scr, u_scr, hs_scr,
                *, tiles_per_seq):
    i = pl.program_id(0)
    tm = x_ref.shape[0]
    qb = ATTN_BLOCK
    t = jnp.maximum(i - 1, 0)
    first_tile = (t % tiles_per_seq) == 0

    nqb = tm // qb
    n_pairs = N_HEADS // 2
    lo_k = lax.broadcasted_iota(I32, (2 * qb, LANES), 1) < HEAD_DIM
    lo_q = lax.broadcasted_iota(I32, (qb, LANES), 1) < HEAD_DIM
    nt_dims = (((1,), (1,)), ((), ()))
    vbds = []
    for jb in range(nqb):
        k2 = k_s[prv, jb * qb:(jb + 2) * qb, :]
        v2 = v_s[prv, jb * qb:(jb + 2) * qb, :]
        k2r = pltpu.roll(k2, HEAD_DIM, 1)
        v2r = pltpu.roll(v2, HEAD_DIM, 1)
        bias = bias_ref[jnp.where(first_tile, 0, 1)] if jb == 0 else bias_ref[1]
        for kv in range(N_KV_HEADS):
            ka, kb = (k2, k2r) if kv == 0 else (k2r, k2)
            va, vb = (v2, v2r) if kv == 0 else (v2r, v2)
            kbd = jnp.concatenate([jnp.where(lo_k, ka, 0.0), jnp.where(lo_k, 0.0, kb)], axis=0).astype(BF16)
            vbds.append(jnp.concatenate([jnp.where(lo_k, va, 0.0), jnp.where(lo_k, 0.0, vb)],
                                        axis=0).astype(BF16))
            for p in range(GROUP // 2):
                pp = kv * (GROUP // 2) + p
                qp = (q_s[prv, jb * qb:(jb + 1) * qb, pp * LANES:(pp + 1) * LANES]
                      * (HEAD_DIM ** -0.5)).astype(BF16)
                s_ref[jb * n_pairs + pp] = lax.dot_general(qp, kbd, nt_dims,
                                                           preferred_element_type=F32) + bias
    invs = []
    for c in range(nqb * n_pairs):
        pp = c % n_pairs
        inv = []
        for tpos in range(2):
            cols = slice(tpos * 2 * qb, (tpos + 1) * 2 * qb)
            st = s_ref[c, :, cols]
            sink = sink_ref[2 * pp + tpos]
            m = jnp.maximum(jnp.max(st, axis=-1, keepdims=True), sink)
            e = jnp.exp(st - m)
            e_ref[c, :, cols] = e.astype(BF16)
            inv.append(1.0 / (jnp.sum(e, axis=-1, keepdims=True) + jnp.exp(sink - m)))
        invs.append(jnp.where(lo_q, inv[0], inv[1]))
    for jb in range(nqb):
        outs = [jnp.dot(e_ref[jb * n_pairs + pp], vbds[jb * N_KV_HEADS + pp // (GROUP // 2)],
                        preferred_element_type=F32) * invs[jb * n_pairs + pp] for pp in range(n_pairs)]
        an_ref[jb * qb:(jb + 1) * qb, :] = _rmsnorm(jnp.concatenate(outs, axis=1), ga_ref[...])
    kt_ref[0] = k_s[prv, tm:tm + qb, :].T
    vt_ref[0] = v_s[prv, tm:tm + qb, :].T

    pad = SUBLANES
    xr = xr_s[prv]
    ext_ref[0:pad, :] = jnp.where(first_tile, 0.0, ext_ref[0:pad, :])
    ext_ref[pad:pad + tm, :] = xr
    cw = cw_ref[...]
    xc = cb_ref[...] + xr * cw[CONV_WIDTH - 1:CONV_WIDTH, :]
    for w in range(CONV_WIDTH - 1):
        sh = CONV_WIDTH - 1 - w
        xc = xc + ext_ref[pad - sh:pad - sh + tm, :] * cw[w:w + 1, :]
    ext_ref[0:pad, :] = xr[tm - pad:tm, :]
    cx_ref[0] = xr[tm - pad:tm, :]
    a, u = _lru_gates(xc, wa_ref, ba_ref, wx_ref, bx_ref, lam_ref)
    _to_lane_tiles(a_scr, a)
    _to_lane_tiles(u_scr, u)
    carry = _lru_scan_tiles(a_scr, u_scr, hs_scr, jnp.where(first_tile, 0.0, h_ref[...]))
    h_ref[...] = carry
    hl_ref[0] = carry
    hseq = jnp.concatenate([hs_scr[j] for j in range(hs_scr.shape[0])], axis=1)
    rn_ref[...] = _rmsnorm(jax.nn.gelu(yr_s[prv]) * hseq, gr_ref[...])

    h = _rmsnorm(x_ref[...], gm_ref[...])
    proj = jnp.dot(h.astype(BF16), w_ref[...], preferred_element_type=F32)
    lo = lax.broadcasted_iota(I32, (tm, LANES), 1) < HEAD_DIM
    rope = (c_ref[...], s1_ref[...], s2_ref[...], lo)
    gq = gq_ref[...]
    for j in range(D_ATTN // LANES):
        q_s[cur, :, j * LANES:(j + 1) * LANES] = _head_norm_rope(proj[:, j * LANES:(j + 1) * LANES], gq, *rope)
    k_s[cur, qb:qb + tm, :] = _head_norm_rope(proj[:, D_ATTN:D_ATTN + KV_W], gk_ref[...], *rope)
    v_s[cur, qb:qb + tm, :] = proj[:, D_ATTN + KV_W:D_ATTN + 2 * KV_W]
    k_s[cur, 0:qb, :] = k_s[prv, tm:tm + qb, :]
    v_s[cur, 0:qb, :] = v_s[prv, tm:tm + qb, :]
    o = D_ATTN + 2 * KV_W
    xr_s[cur] = proj[:, o:o + D_RNN]
    yr_s[cur] = proj[:, o + D_RNN:o + 2 * D_RNN]


def _front(x2d, sinks, g_mix, w_bf, gq2, gk2, ctab, s1tab, s2tab, g_attn,
           cw, cb, wa, ba, wx, bx, lam, g_rnn, batch, seq, tm):
    n = x2d.shape[0]
    nt = n // tm
    tps = seq // tm
    qb = ATTN_BLOCK
    cur = lambda i: (jnp.minimum(i, nt - 1), 0)
    tab = lambda i: (jnp.minimum(i, nt - 1) % tps, 0)
    fix = lambda i: (0, 0)
    prev = lambda i: (jnp.maximum(i - 1, 0), 0)
    per_seq = lambda i: (jnp.maximum(i - 1, 0) // tps, 0, 0)
    vec = lambda w: pl.BlockSpec((1, w), fix)
    return pl.pallas_call(
        functools.partial(_front_kernel, tiles_per_seq=tps),
        grid=(nt + 1,),
        in_specs=[pl.BlockSpec(memory_space=pltpu.SMEM),
                  pl.BlockSpec((tm, D_MODEL), cur), vec(D_MODEL), pl.BlockSpec((D_MODEL, D_IN), fix),
                  vec(LANES), vec(LANES),
                  pl.BlockSpec((tm, LANES), tab), pl.BlockSpec((tm, LANES), tab), pl.BlockSpec((tm, LANES), tab),
                  pl.BlockSpec((2, qb, 4 * qb), lambda i: (0, 0, 0)), vec(D_ATTN),
                  pl.BlockSpec((CONV_WIDTH, D_RNN), fix), vec(D_RNN),
                  pl.BlockSpec((D_RNN, D_RNN), fix), vec(D_RNN),
                  pl.BlockSpec((D_RNN, D_RNN), fix), vec(D_RNN), vec(D_RNN), vec(D_RNN)],
        out_specs=(pl.BlockSpec((tm, D_ATTN), prev), pl.BlockSpec((tm, D_RNN), prev),
                   pl.BlockSpec((1, KV_W, qb), per_seq), pl.BlockSpec((1, KV_W, qb), per_seq),
                   pl.BlockSpec((1, 1, D_RNN), per_seq), pl.BlockSpec((1, SUBLANES, D_RNN), per_seq)),
        out_shape=(jax.ShapeDtypeStruct((n, D_ATTN), F32), jax.ShapeDtypeStruct((n, D_RNN), F32),
                   jax.ShapeDtypeStruct((batch, KV_W, qb), F32), jax.ShapeDtypeStruct((batch, KV_W, qb), F32),
                   jax.ShapeDtypeStruct((batch, 1, D_RNN), F32),
                   jax.ShapeDtypeStruct((batch, SUBLANES, D_RNN), F32)),
        scratch_shapes=[pltpu.VMEM((2, tm, D_ATTN), F32),
                        pltpu.VMEM((2, tm + qb, KV_W), F32), pltpu.VMEM((2, tm + qb, KV_W), F32),
                        pltpu.VMEM((2, tm, D_RNN), F32), pltpu.VMEM((2, tm, D_RNN), F32),
                        pltpu.VMEM((tm + SUBLANES, D_RNN), F32), pltpu.VMEM((1, D_RNN), F32),
                        pltpu.VMEM((tm // qb * (N_HEADS // 2), qb, 4 * qb), F32),
                        pltpu.VMEM((tm // qb * (N_HEADS // 2), qb, 4 * qb), BF16),
                        pltpu.VMEM((D_RNN // LANES, tm, LANES), F32), pltpu.VMEM((D_RNN // LANES, tm, LANES), F32),
                        pltpu.VMEM((D_RNN // LANES, tm, LANES), F32)],
        compiler_params=_cparams(1),
        name="front",
    )(sinks, x2d, g_mix, w_bf, gq2, gk2, ctab, s1tab, s2tab, _band_bias(qb), g_attn,
      cw, cb, wa, ba, wx, bx, lam, g_rnn)


def _mix_route_kernel(*refs):
    i = pl.program_id(0)
    hb_s, d_s = refs[16:18]

    @pl.when(i == 0)
    def _():
        hb_s[...] = jnp.zeros(hb_s.shape, BF16)
        d_s[...] = jnp.zeros(d_s.shape, I32)

    for cur in range(2):
        @pl.when(i % 2 == cur)
        def _():
            _mix_route_body(cur, 1 - cur, *refs)


def _mix_route_body(cur, prv, x_ref, an_ref, rn_ref, woa_ref, wor_ref, g_ref, wr2_ref, br_ref, tri_ref, low_ref,
                    x2_ref, ts_ref, dest_ref, gate_ref, n8_ref, off_ref, hb_s, d_s):
    tt = x_ref.shape[0]
    tile_rows = ts_ref.shape[0]
    n_chunks = tile_rows // DISPATCH_CHUNK

    hb_prev = hb_s[prv]
    d16 = [d_s[prv, k:k + 1, :].astype(I16) for k in range(TOP_K)]
    ri = lax.broadcasted_iota(I32, (DISPATCH_CHUNK, tt), 0).astype(I16)
    one = jnp.ones((DISPATCH_CHUNK, tt), BF16)

    def dispatch(chunks):
        for c in chunks:
            p = jnp.zeros((DISPATCH_CHUNK, tt), BF16)
            for d in d16:
                p = jnp.where(ri == d - jnp.int16(c * DISPATCH_CHUNK), one, p)
            ts_ref[c * DISPATCH_CHUNK:(c + 1) * DISPATCH_CHUNK, :] = _pack_bf16_pairs(
                jnp.dot(p, hb_prev, preferred_element_type=F32))

    third = -(-n_chunks // 3)

    x2 = x_ref[...] + jnp.dot(an_ref[...].astype(BF16), woa_ref[...], preferred_element_type=F32) \
        + jnp.dot(rn_ref[...].astype(BF16), wor_ref[...], preferred_element_type=F32)
    x2_ref[...] = x2
    dispatch(range(0, third))
    hn = _rmsnorm(x2, g_ref[...])

    nt = (((1,), (1,)), ((), ()))
    hb = hn.astype(BF16)
    hb_s[cur] = hb
    hmid = (hn - hb.astype(F32)).astype(BF16)
    wr2 = wr2_ref[...]
    both = lax.dot_general(wr2, hb, nt, preferred_element_type=F32)
    logits = (lax.dot_general(wr2[:N_EXPERTS], hmid, nt, preferred_element_type=F32)
              + both[N_EXPERTS:]) + both[:N_EXPERTS] + br_ref[...]
    dispatch(range(third, 2 * third))

    ie = lax.broadcasted_iota(I32, (N_EXPERTS, tt), 0).astype(F32)
    l = logits
    vals, sels = [], []
    for _ in range(TOP_K):
        m = jnp.max(l, axis=0, keepdims=True)
        idx = jnp.min(jnp.where(l == m, ie, float(N_EXPERTS)), axis=0, keepdims=True)
        sel = ie == idx
        vals.append(m)
        sels.append(sel)
        l = jnp.where(sel, NEG_BIG, l)
    es = [jnp.exp(v - vals[0]) for v in vals]
    den = es[0] + es[1] + es[2] + es[3]
    gate_ref[0] = jnp.concatenate([e / den for e in es], axis=0)
    dispatch(range(2 * third, n_chunks))

    oh = jnp.zeros((N_EXPERTS, tt), F32)
    for sel in sels:
        oh = oh + jnp.where(sel, 1.0, 0.0)
    before = jnp.dot(oh.astype(BF16), tri_ref[...], preferred_element_type=F32)
    cnt = jnp.sum(oh, axis=1, keepdims=True).astype(I32)
    n8 = ((cnt + (SUBLANES - 1)) >> 3) << 3
    n8b = jnp.broadcast_to(n8, (N_EXPERTS, LANES))
    off = jnp.dot(low_ref[...], n8b.astype(F32).astype(BF16), preferred_element_type=F32)
    n8_ref[0] = n8b
    off_ref[0] = off.astype(I32)
    base = off[:, 0:1] + before
    dests = jnp.concatenate(
        [jnp.sum(jnp.where(sel, base, 0.0), axis=0, keepdims=True).astype(I32) for sel in sels], axis=0)
    dest_ref[0] = dests
    d_s[cur, 0:TOP_K, :] = dests


def _tile_rows(tt):
    return -(-(TOP_K * tt + N_EXPERTS * (SUBLANES - 1)) // DISPATCH_CHUNK) * DISPATCH_CHUNK


def _mix_route(x2d, an, rn, woa, wor, g, wrt, br, low, tt):
    n = x2d.shape[0]
    nt = n // tt
    tile_rows = _tile_rows(tt)
    tri = jnp.triu(jnp.ones((tt, tt), BF16), k=1)
    row = lambda i: (jnp.minimum(i, nt - 1), 0)
    prev = lambda i: (jnp.maximum(i - 1, 0), 0)
    fix = lambda i: (0, 0)
    t3 = lambda i: (jnp.minimum(i, nt - 1), 0, 0)
    in_specs = [pl.BlockSpec((tt, D_MODEL), row), pl.BlockSpec((tt, D_ATTN), row),
                pl.BlockSpec((tt, D_RNN), row), pl.BlockSpec((D_ATTN, D_MODEL), fix),
                pl.BlockSpec((D_RNN, D_MODEL), fix), pl.BlockSpec((1, D_MODEL), fix),
                pl.BlockSpec((2 * N_EXPERTS, D_MODEL), fix), pl.BlockSpec((N_EXPERTS, 1), fix),
                pl.BlockSpec((tt, tt), fix), pl.BlockSpec((N_EXPERTS, N_EXPERTS), fix)]
    out_shape = (jax.ShapeDtypeStruct((n, D_MODEL), F32),
                 jax.ShapeDtypeStruct((nt * tile_rows, HALF), U32),
                 jax.ShapeDtypeStruct((nt, TOP_K, tt), I32),
                 jax.ShapeDtypeStruct((nt, TOP_K, tt), F32),
                 jax.ShapeDtypeStruct((nt, N_EXPERTS, LANES), I32),
                 jax.ShapeDtypeStruct((nt, N_EXPERTS, LANES), I32))
    out_specs = (pl.BlockSpec((tt, D_MODEL), row),
                 pl.BlockSpec((tile_rows, HALF), prev),
                 pl.BlockSpec((1, TOP_K, tt), t3), pl.BlockSpec((1, TOP_K, tt), t3),
                 pl.BlockSpec((1, N_EXPERTS, LANES), t3), pl.BlockSpec((1, N_EXPERTS, LANES), t3))
    return pl.pallas_call(
        _mix_route_kernel,
        grid=(nt + 1,),
        in_specs=in_specs,
        out_specs=out_specs,
        out_shape=out_shape,
        scratch_shapes=[pltpu.VMEM((2, tt, D_MODEL), BF16), pltpu.VMEM((2, SUBLANES, tt), I32)],
        compiler_params=_cparams(1),
        name="mix_route",
    )(x2d, an, rn, woa, wor, g, wrt, br, tri, low)


LOW_BITS = 4


def _start_piece(src_hbm, dst_buf, sem, s, d, l8, nbits):
    def bit_copy(c):
        size = SUBLANES << c
        low = (l8 & ((1 << c) - 1)) * SUBLANES

        @pl.when(((l8 >> c) & 1) == 1)
        def _():
            pltpu.make_async_copy(
                src_hbm.at[pl.ds(pl.multiple_of(s + low, SUBLANES), size)],
                dst_buf.at[pl.ds(pl.multiple_of(d + low, SUBLANES), size)], sem).start()

    for c in range(min(LOW_BITS, nbits)):
        bit_copy(c)
    if nbits > LOW_BITS:
        def long_copies(_, carry):
            for c in range(LOW_BITS, nbits):
                bit_copy(c)
            return carry

        lax.fori_loop(0, jnp.where(l8 >= (1 << LOW_BITS), 1, 0), long_copies, 0)


def _start_pieces(src_hbm, dst_buf, sem, p_lo, p_hi, psrc_ref, pdst_ref, plen_ref, nbits):
    def body(p, carry):
        _start_piece(src_hbm, dst_buf, sem, psrc_ref[p], pdst_ref[p], plen_ref[p], nbits)
        return carry

    lax.fori_loop(p_lo, p_hi, body, 0)


def _wait_rows(src_hbm, dst_buf, sem, rows8, nbits):
    for c in range(nbits):
        size = SUBLANES << c

        @pl.when(((rows8 >> c) & 1) == 1)
        def _():
            pltpu.make_async_copy(src_hbm.at[pl.ds(0, size)], dst_buf.at[pl.ds(0, size)], sem).wait()


def _moe_gmm_kernel(be_ref, rows_ref, wslot_ref, nxt_ref,
                    psa_ref, pea_ref, srca_ref, dsta_ref, lena_ref, hsrca_ref, hlena_ref,
                    psb_ref, peb_ref, srcb_ref, dstb_ref, lenb_ref, hsrcb_ref, hlenb_ref,
                    tsa_hbm, tsb_hbm, wgu_hbm, wdn_hbm, bg_ref, bu_ref, bd_ref, perm_ref,
                    ys_ref, lhs_ref, wgu_buf, wdn_buf, wg_ref, wu_ref, wd_ref, sem_ref, wsem_ref,
                    *, nbits_a, nbits_b):
    j = pl.program_id(0)
    nb = pl.num_programs(0)
    slot = j % 2
    bm = lhs_ref.shape[1]

    def gather(blk, sl):
        for ts_hbm, ps, pe, src, dst, ln, hsrc, hlen, nbits in (
                (tsa_hbm, psa_ref, pea_ref, srca_ref, dsta_ref, lena_ref, hsrca_ref, hlena_ref, nbits_a),
                (tsb_hbm, psb_ref, peb_ref, srcb_ref, dstb_ref, lenb_ref, hsrcb_ref, hlenb_ref, nbits_b)):
            _start_pieces(ts_hbm, lhs_ref.at[sl], sem_ref.at[sl], ps[blk], pe[blk], src, dst, ln, nbits)
            _start_piece(ts_hbm, lhs_ref.at[sl], sem_ref.at[sl], hsrc[blk], 0, hlen[blk], nbits)

    def weight_copies(e, ws):
        return (pltpu.make_async_copy(wgu_hbm.at[e], wgu_buf.at[ws], wsem_ref.at[ws]),
                pltpu.make_async_copy(wdn_hbm.at[e], wdn_buf.at[ws], wsem_ref.at[ws]))

    @pl.when(j == 0)
    def _():
        lhs_ref[...] = jnp.zeros(lhs_ref.shape, U32)
        gather(0, 0)
        for cp in weight_copies(be_ref[0], wslot_ref[0]):
            cp.start()

    @pl.when(j + 1 < nb)
    def _():
        gather(j + 1, 1 - slot)

    @pl.when(jnp.logical_or(j == 0, be_ref[j] != be_ref[jnp.maximum(j - 1, 0)]))
    def _():
        ws = wslot_ref[j]
        for cp in weight_copies(be_ref[j], ws):
            cp.wait()
        nxt = nxt_ref[j]

        @pl.when(nxt >= 0)
        def _():
            for cp in weight_copies(nxt, 1 - ws):
                cp.start()

        perm = perm_ref[...]
        half = PERM_COLS // 2
        for c in range(2 * D_FF // PERM_COLS):
            wb = wgu_buf[ws, :, c * PERM_COLS:(c + 1) * PERM_COLS].astype(BF16)
            wp = jnp.dot(wb, perm, preferred_element_type=F32).astype(BF16)
            wg_ref[:, c * half:(c + 1) * half] = wp[:, :half]
            wu_ref[:, c * half:(c + 1) * half] = wp[:, half:]
        wd_ref[...] = wdn_buf[ws].astype(BF16)

    rows8 = rows_ref[j]
    _wait_rows(tsa_hbm, lhs_ref.at[slot], sem_ref.at[slot], rows8, (bm // SUBLANES).bit_length())

    def expert_rows(rs):
        x = _unpack_bf16_pairs(lhs_ref[slot, rs, :])
        gate = jnp.dot(x, wg_ref[...], preferred_element_type=F32) + bg_ref[0]
        up = jnp.dot(x, wu_ref[...], preferred_element_type=F32) + bu_ref[0]
        gate = jnp.minimum(gate, SWIGLU_LIMIT)
        up = jnp.clip(up, -SWIGLU_LIMIT, SWIGLU_LIMIT)
        act = (up + 1.0) * (gate * jax.nn.sigmoid(SWIGLU_ALPHA * gate))
        y = jnp.dot(act.astype(BF16), wd_ref[...], preferred_element_type=F32) + bd_ref[0]
        ys_ref[rs, :] = _pack_bf16_pairs(y.astype(BF16).astype(F32))

    half8 = bm // 2 // SUBLANES

    @pl.when(rows8 > half8)
    def _():
        expert_rows(slice(0, bm))

    @pl.when(jnp.logical_and(rows8 > 0, rows8 <= half8))
    def _():
        expert_rows(slice(0, bm // 2))
        ys_ref[bm // 2:, :] = jnp.zeros((bm // 2, HALF), U32)

    @pl.when(rows8 == 0)
    def _():
        ys_ref[...] = jnp.zeros(ys_ref.shape, U32)


def _moe_gmm(blocks, tabs_a, tabs_b, ts_a, ts_b, w_gu, w_dn, bg, bu, bd, perm, nblocks, bm, nbits_a, nbits_b):
    we = lambda j, be, *_: (be[j], 0, 0)
    grid_spec = pltpu.PrefetchScalarGridSpec(
        num_scalar_prefetch=18,
        grid=(nblocks,),
        in_specs=[pl.BlockSpec(memory_space=pl.ANY), pl.BlockSpec(memory_space=pl.ANY),
                  pl.BlockSpec(memory_space=pl.ANY), pl.BlockSpec(memory_space=pl.ANY),
                  pl.BlockSpec((1, 1, D_FF), we), pl.BlockSpec((1, 1, D_FF), we),
                  pl.BlockSpec((1, 1, D_MODEL), we),
                  pl.BlockSpec((PERM_COLS, PERM_COLS), lambda j, *_: (0, 0))],
        out_specs=pl.BlockSpec((bm, HALF), lambda j, *_: (j, 0)),
        scratch_shapes=[pltpu.VMEM((2, bm, HALF), U32),
                        pltpu.VMEM((2, D_MODEL, 2 * D_FF), F32), pltpu.VMEM((2, D_FF, D_MODEL), F32),
                        pltpu.VMEM((D_MODEL, D_FF), BF16), pltpu.VMEM((D_MODEL, D_FF), BF16),
                        pltpu.VMEM((D_FF, D_MODEL), BF16),
                        pltpu.SemaphoreType.DMA((2,)), pltpu.SemaphoreType.DMA((2,))],
    )
    return pl.pallas_call(
        functools.partial(_moe_gmm_kernel, nbits_a=nbits_a, nbits_b=nbits_b),
        grid_spec=grid_spec,
        out_shape=jax.ShapeDtypeStruct((nblocks * bm, HALF), U32),
        compiler_params=_cparams(1),
        name="moe_gmm",
    )(*blocks, *tabs_a, *tabs_b, ts_a, ts_b, w_gu, w_dn, bg, bu, bd, perm)


def _combine_kernel(psrc_ref, pdst_ref, plen_ref, tlo_ref, thi_ref, tsrc_ref, tdst_ref, tlen_ref, rows_ref,
                    ys_hbm, x2_ref, dest_ref, gate_ref, o_ref,
                    buf_ref, db_ref, gb_ref, sem_ref, *, nbits):
    i = pl.program_id(0)
    n = pl.num_programs(0)
    slot = i % 2
    tt = x2_ref.shape[0]
    tile_rows = buf_ref.shape[1]

    def gather(tile, sl):
        _start_pieces(ys_hbm, buf_ref.at[sl], sem_ref.at[sl], tile * N_EXPERTS, (tile + 1) * N_EXPERTS,
                      psrc_ref, pdst_ref, plen_ref, nbits)
        _start_pieces(ys_hbm, buf_ref.at[sl], sem_ref.at[sl], tlo_ref[tile], thi_ref[tile],
                      tsrc_ref, tdst_ref, tlen_ref, nbits)

    @pl.when(i == 0)
    def _():
        buf_ref[...] = jnp.zeros(buf_ref.shape, U32)
        gather(0, 0)

    @pl.when(i + 1 < n)
    def _():
        gather(i + 1, 1 - slot)

    _wait_rows(ys_hbm, buf_ref.at[slot], sem_ref.at[slot], rows_ref[i], (tile_rows // SUBLANES).bit_length())

    dest = dest_ref[0]
    gate = gate_ref[0]
    for k in range(TOP_K):
        db_ref[k] = jnp.broadcast_to(dest[:, k:k + 1], (tt, DISPATCH_CHUNK)).astype(I16)
        gb_ref[k] = jnp.broadcast_to(gate[:, k:k + 1], (tt, DISPATCH_CHUNK)).astype(BF16)
    li = lax.broadcasted_iota(I32, (tt, DISPATCH_CHUNK), 1).astype(I16)
    gms = []
    for c in range(tile_rows // DISPATCH_CHUNK):
        lic = li + jnp.int16(c * DISPATCH_CHUNK)
        gm = jnp.zeros((tt, DISPATCH_CHUNK), BF16)
        for k in range(TOP_K):
            gm = jnp.where(lic == db_ref[k], gb_ref[k], gm)
        gms.append(gm)
    o_ref[...] = x2_ref[...] + jnp.dot(jnp.concatenate(gms, axis=1), _unpack_bf16_pairs(buf_ref[slot]),
                                       preferred_element_type=F32)


def _combine(tabs, ys, x2, dest, gate, tt, nbits):
    n = x2.shape[0]
    nt = n // tt
    tile_rows = _tile_rows(tt)
    grid_spec = pltpu.PrefetchScalarGridSpec(
        num_scalar_prefetch=len(tabs),
        grid=(nt,),
        in_specs=[pl.BlockSpec(memory_space=pl.ANY),
                  pl.BlockSpec((tt, D_MODEL), lambda i, *_: (i, 0)),
                  pl.BlockSpec((1, tt, TOP_K), lambda i, *_: (i, 0, 0)),
                  pl.BlockSpec((1, tt, TOP_K), lambda i, *_: (i, 0, 0))],
        out_specs=pl.BlockSpec((tt, D_MODEL), lambda i, *_: (i, 0)),
        scratch_shapes=[pltpu.VMEM((2, tile_rows, HALF), U32),
                        pltpu.VMEM((TOP_K, tt, DISPATCH_CHUNK), I16), pltpu.VMEM((TOP_K, tt, DISPATCH_CHUNK), BF16),
                        pltpu.SemaphoreType.DMA((2,))],
    )
    return pl.pallas_call(
        functools.partial(_combine_kernel, nbits=nbits),
        grid_spec=grid_spec,
        out_shape=jax.ShapeDtypeStruct((n, D_MODEL), F32),
        compiler_params=_cparams(1),
        name="combine",
    )(*tabs, ys, x2, dest, gate)


def _piece_tables(n8_a, off_a, rows_a, n8_b, off_b, bm, nblocks):
    nta = n8_a.shape[0]
    n8 = jnp.concatenate([n8_a, n8_b], axis=0)
    seg_off = jnp.concatenate([off_a, off_b], axis=0)
    n_tiles = n8.shape[0]
    tile_base = jnp.concatenate([jnp.arange(nta, dtype=I32) * rows_a, jnp.zeros((n_tiles - nta,), I32)])[:, None]
    tot = jnp.sum(n8, axis=0)
    pos0 = jnp.cumsum(n8, axis=0) - n8
    nblk = (tot + bm - 1) // bm
    cs = jnp.cumsum(nblk)
    bs = cs - nblk
    kblk = pos0 // bm
    len0 = jnp.minimum(n8, (kblk + 1) * bm - pos0)
    len1 = n8 - len0
    b0 = bs[None, :] + kblk
    src0 = tile_base + seg_off
    in_blk = pos0 - kblk * bm
    jj = jnp.arange(nblocks, dtype=I32)
    i32 = lambda v: v.astype(I32)

    def gmm_tabs(sl):
        em = lambda v: v[sl].T.reshape(-1)
        blk_em = em(b0)
        first = i32(jnp.sum(blk_em[None, :] < jj[:, None], axis=1))
        last = i32(jnp.sum(blk_em[None, :] <= jj[:, None], axis=1))
        hit = (blk_em[None, :] + 1 == jj[:, None]) & (em(len1)[None, :] > 0)
        tail_src = i32(jnp.sum(jnp.where(hit, em(src0 + len0)[None, :], 0), axis=1))
        tail_len = i32(jnp.sum(jnp.where(hit, em(len1)[None, :], 0), axis=1) // SUBLANES)
        return first, last, i32(em(src0)), i32(em(in_blk)), i32(em(len0) // SUBLANES), tail_src, tail_len

    def comb_tabs(sl):
        tm = lambda v: v[sl].reshape(-1)
        has_tail = len1[sl] > 0
        cnt = jnp.sum(has_tail, axis=1)
        lo = jnp.cumsum(cnt) - cnt
        slot_ = lo[:, None] + jnp.cumsum(has_tail, axis=1) - has_tail
        hit = (slot_.reshape(-1)[None, :] == jj[:, None]) & has_tail.reshape(-1)[None, :]
        pick = lambda v: i32(jnp.sum(jnp.where(hit, tm(v)[None, :], 0), axis=1))
        return (i32(tm(b0 * bm + in_blk)), i32(tm(seg_off)), i32(tm(len0) // SUBLANES),
                i32(lo), i32(lo + cnt), pick((b0 + 1) * bm), pick(seg_off + len0), pick(len1 // SUBLANES),
                i32(jnp.sum(n8[sl], axis=1) // SUBLANES))

    count_le = lambda v: jnp.sum(cs[None, :] <= v[:, None], axis=1)
    n_active = cs[-1]
    e_last = count_le(jnp.maximum(n_active - 1, 0).reshape(1))[0]
    block_e = jnp.minimum(count_le(jj), e_last).astype(I32)
    ee = jnp.arange(N_EXPERTS, dtype=I32)
    mine = (jj[:, None] >= bs[None, :]) & (jj[:, None] < cs[None, :])
    left = jnp.clip(tot[None, :] - (jj[:, None] - bs[None, :]) * bm, 0, bm)
    rows8 = (jnp.sum(jnp.where(mine, left, 0), axis=1) // SUBLANES).astype(I32)
    has = nblk > 0
    run = jnp.cumsum(has.astype(I32)) - 1
    later = (ee[None, :] > ee[:, None]) & has[None, :]
    nxt_e = jnp.min(jnp.where(later, ee[None, :], N_EXPERTS), axis=1)
    nxt_e = jnp.where(nxt_e == N_EXPERTS, -1, nxt_e)
    own = block_e[:, None] == ee[None, :]
    wslot = (jnp.sum(jnp.where(own, run[None, :], 0), axis=1) % 2).astype(I32)
    nxt = jnp.sum(jnp.where(own, nxt_e[None, :], 0), axis=1).astype(I32)
    a, b = slice(0, nta), slice(nta, n_tiles)
    return (block_e, rows8, wslot, nxt), gmm_tabs(a), gmm_tabs(b), comb_tabs(a), comb_tabs(b)


def _block_diag(w):
    nb, bi, bo = w.shape
    eye = jnp.eye(nb, dtype=w.dtype)
    return (eye[:, None, :, None] * w[:, :, None, :]).reshape(nb * bi, nb * bo)


def _step(x_prompt, x_sample, cache_k, cache_v, state_conv, state_h, g_mix_norm, w_in, g_q_norm, g_k_norm,
          attn_sinks, conv_w, conv_b, w_lru_a, b_lru_a, w_lru_x, b_lru_x, lru_lambda, g_attn_out, g_rnn_out,
          w_out, g_ffn_norm, w_router, b_router, w_gate_up, b_gate_up, w_down, b_down,
          *, tm, tt, tc, bm, past_len):
    B, S, D = x_prompt.shape
    NS = x_sample.shape[0]
    assert x_sample.shape[1] == 1 and D == D_MODEL
    assert (B * S) % tt == 0 and (B * S) % tm == 0 and S % tc == 0 and S % ATTN_BLOCK == 0 and S % tm == 0
    assert NS % SUBLANES == 0 and tt <= bm
    assert tt % SUBLANES == 0 and NS <= bm
    n_pt = (B * S) // tt
    total_rows = TOP_K * (B * S + NS) + (n_pt + 1) * N_EXPERTS * (SUBLANES - 1)
    nblocks = -(-total_rows // bm) + N_EXPERTS
    nbits_p = (tt // SUBLANES).bit_length()
    nbits_s = (NS // SUBLANES).bit_length()

    l = 0
    row = lambda v: v[l].reshape(1, -1)
    w_in_bf = w_in[l].astype(BF16)
    gq2 = jnp.tile(g_q_norm[l], 2).reshape(1, LANES)
    gk2 = jnp.tile(g_k_norm[l], 2).reshape(1, LANES)
    wa = _block_diag(w_lru_a[l]).astype(BF16)
    wx = _block_diag(w_lru_x[l]).astype(BF16)
    ba = b_lru_a[l].reshape(1, D_RNN)
    bx = b_lru_x[l].reshape(1, D_RNN)
    wo = w_out[l].astype(BF16)
    woa, wor = wo[:D_ATTN], wo[D_ATTN:]
    wr = w_router[l].T
    wr_hi = wr.astype(BF16)
    wrt = jnp.concatenate([wr_hi, (wr - wr_hi.astype(F32)).astype(BF16)], axis=0)
    br = b_router[l].reshape(N_EXPERTS, 1)
    low = jnp.tril(jnp.ones((N_EXPERTS, N_EXPERTS), BF16), k=-1)
    bgu = b_gate_up[l].reshape(N_EXPERTS, D_FF, 2)
    bg = bgu[:, :, 0].reshape(N_EXPERTS, 1, D_FF)
    bu = bgu[:, :, 1].reshape(N_EXPERTS, 1, D_FF)
    bd = b_down[l].reshape(N_EXPERTS, 1, D_MODEL)
    half = PERM_COLS // 2
    pr = jnp.arange(PERM_COLS)
    perm = (pr[None, :] == jnp.where(pr % 2 == 0, pr // 2, half + pr // 2)[:, None]).astype(BF16)
    sinks = attn_sinks[l]

    ctab, s1tab, s2tab = _rope_tables(jnp.arange(S))
    an, rn, kt_p, vt_p, h_last_p, xr_tail = _front(
        x_prompt.reshape(B * S, D), sinks, row(g_mix_norm), w_in_bf, gq2, gk2, ctab, s1tab, s2tab,
        row(g_attn_out), conv_w[l], row(conv_b), wa, ba, wx, bx, row(lru_lambda), row(g_rnn_out), B, S, tm)
    x2_p, ts_p, dest_p, gate_p, n8_p, off_p = _mix_route(
        x_prompt.reshape(B * S, D), an, rn, woa, wor, row(g_ffn_norm), wrt, br, low, tt)

    cs_tab = _rope_tables(jnp.full((NS,), past_len, I32))
    q_s, k_s, v_s, xr_s, yr_s = _in_proj(x_sample.reshape(NS, D), row(g_mix_norm), w_in_bf, gq2, gk2,
                                         *cs_tab, NS)
    to_rows = lambda c: jnp.transpose(c, (0, 2, 3, 1)).reshape(NS * KV_W, WINDOW)
    from_rows = lambda c, n: jnp.transpose(c.reshape(n, N_KV_HEADS, HEAD_DIM, WINDOW), (0, 3, 1, 2))[None]
    an_s, kt_s, vt_s = _attn_sample(q_s, k_s, v_s, to_rows(cache_k[l]), to_rows(cache_v[l]), sinks,
                                    row(g_attn_out), SUBLANES)
    rn_s, h_last_s, hist_s = _rnn_sample(xr_s, yr_s, jnp.transpose(state_conv[l], (1, 0, 2)), state_h[l],
                                         conv_w[l], row(conv_b), wa, ba, wx, bx, row(lru_lambda),
                                         row(g_rnn_out))
    x2_s, ts_s, dest_s, gate_s, n8_s, off_s = _mix_route(
        x_sample.reshape(NS, D), an_s, rn_s, woa, wor, row(g_ffn_norm), wrt, br, low, NS)

    blocks, gmm_p, gmm_s, comb_p, comb_s = _piece_tables(
        n8_p[:, :, 0], off_p[:, :, 0], _tile_rows(tt), n8_s[:, :, 0], off_s[:, :, 0], bm, nblocks)
    ys = _moe_gmm(blocks, gmm_p, gmm_s, ts_p, ts_s, w_gate_up[l], w_down[l], bg, bu, bd, perm,
                  nblocks, bm, nbits_p, nbits_s)
    tr = lambda a: jnp.transpose(a, (0, 2, 1))
    y_p = _combine(comb_p, ys, x2_p, tr(dest_p), tr(gate_p), tt, nbits_p)
    y_s = _combine(comb_s, ys, x2_s, tr(dest_s), tr(gate_s), NS, nbits_s)

    cp = xr_tail[:, SUBLANES - (CONV_WIDTH - 1):]
    return (y_p.reshape(B, S, D), y_s.reshape(NS, 1, D),
            from_rows(kt_p, B), from_rows(vt_p, B), cp[None], h_last_p.reshape(1, B, D_RNN),
            from_rows(kt_s, NS), from_rows(vt_s, NS), jnp.transpose(hist_s, (1, 0, 2))[None], h_last_s[None])


def kernel(x_prompt, x_sample, cache_k, cache_v, state_conv, state_h, g_mix_norm, w_in, g_q_norm, g_k_norm, attn_sinks, conv_w, conv_b, w_lru_a, b_lru_a, w_lru_x, b_lru_x, lru_lambda, g_attn_out, g_rnn_out, w_out, g_ffn_norm, w_router, b_router, w_gate_up, b_gate_up, w_down, b_down):
    return _step(x_prompt, x_sample, cache_k, cache_v, state_conv, state_h, g_mix_norm, w_in, g_q_norm,
                 g_k_norm, attn_sinks, conv_w, conv_b, w_lru_a, b_lru_a, w_lru_x, b_lru_x, lru_lambda,
                 g_attn_out, g_rnn_out, w_out, g_ffn_norm, w_router, b_router, w_gate_up, b_gate_up,
                 w_down, b_down, tm=512, tt=512, tc=256, bm=MOE_BLOCK_ROWS, past_len=PAST_LEN)
```

```python
import functools

import jax
import jax.numpy as jnp
from jax import lax
from jax.experimental import pallas as pl
from jax.experimental.pallas import tpu as pltpu

F32 = jnp.float32
BF16 = jnp.bfloat16
I32 = jnp.int32
I16 = jnp.int16

D_MODEL = 1024
HEAD_DIM = 64
N_HEADS = 8
N_KV_HEADS = 2
GROUP = 4
WINDOW = 128
ATTN_BLOCK = 128
ROT_DIM = 16
ROPE_THETA = 500000.0
D_ATTN = 512
D_RNN = 512
KV_W = 128
D_IN = 1792
CONV_WIDTH = 4
LRU_C = 8.0
N_EXPERTS = 32
TOP_K = 4
D_FF = 1024
SWIGLU_LIMIT = 7.0
SWIGLU_ALPHA = 1.702
EPS = 1e-6
PAST_LEN = 8192

LANES = 128
SUBLANES = 8
NEG_BIG = -1e30
VMEM_LIMIT = 56 * 1024 * 1024

MOE_BLOCK_ROWS = 512
PERM_COLS = 256
DISPATCH_CHUNK = 256


def _cparams(n_axes):
    return pltpu.CompilerParams(dimension_semantics=("arbitrary",) * n_axes,
                                vmem_limit_bytes=VMEM_LIMIT)


U32 = jnp.uint32
HALF = D_MODEL // 2
HI16 = 0xFFFF0000


def _pack_bf16_pairs(x):
    bits = pltpu.bitcast(x, U32)
    return (bits[:, HALF:] & U32(HI16)) | (bits[:, :HALF] >> 16)


def _unpack_bf16_pairs(u):
    lo = pltpu.bitcast(u << 16, F32)
    hi = pltpu.bitcast(u & U32(HI16), F32)
    return jnp.concatenate([lo, hi], axis=1).astype(BF16)


def _rmsnorm(x, g):
    ms = jnp.mean(x * x, axis=-1, keepdims=True)
    return (x * lax.rsqrt(ms + EPS)) * g


def _head_norm_rope(t, g, c, s1, s2, lo):
    sq = t * t
    s_lo = jnp.sum(jnp.where(lo, sq, 0.0), axis=-1, keepdims=True)
    s_hi = jnp.sum(jnp.where(lo, 0.0, sq), axis=-1, keepdims=True)
    ms = jnp.where(lo, s_lo, s_hi) * (1.0 / HEAD_DIM)
    n = (t * lax.rsqrt(ms + EPS)) * g
    up = pltpu.roll(n, LANES - ROT_DIM // 2, 1)
    dn = pltpu.roll(n, ROT_DIM // 2, 1)
    return n * c + up * s1 + dn * s2


def _in_proj_kernel(x_ref, g_ref, w_ref, gq_ref, gk_ref, c_ref, s1_ref, s2_ref,
                    q_ref, k_ref, v_ref, xr_ref, yr_ref):
    tm = x_ref.shape[0]
    h = _rmsnorm(x_ref[...], g_ref[...])
    proj = jnp.dot(h.astype(BF16), w_ref[...], preferred_element_type=F32)
    rope = (c_ref[...], s1_ref[...], s2_ref[...], lax.broadcasted_iota(I32, (tm, LANES), 1) < HEAD_DIM)
    gq = gq_ref[...]
    for j in range(D_ATTN // LANES):
        q_ref[:, j * LANES:(j + 1) * LANES] = _head_norm_rope(proj[:, j * LANES:(j + 1) * LANES], gq, *rope)
    k_ref[...] = _head_norm_rope(proj[:, D_ATTN:D_ATTN + KV_W], gk_ref[...], *rope)
    v_ref[...] = proj[:, D_ATTN + KV_W:D_ATTN + 2 * KV_W]
    o = D_ATTN + 2 * KV_W
    xr_ref[...] = proj[:, o:o + D_RNN]
    yr_ref[...] = proj[:, o + D_RNN:o + 2 * D_RNN]


def _in_proj(x2d, g, w_bf, gq2, gk2, ctab, s1tab, s2tab, tm):
    n = x2d.shape[0]
    ntab = ctab.shape[0] // tm
    row = lambda i: (i, 0)
    fix = lambda i: (0, 0)
    tab = lambda i: (i % ntab, 0)
    out_shapes = (jax.ShapeDtypeStruct((n, D_ATTN), F32), jax.ShapeDtypeStruct((n, KV_W), F32),
                  jax.ShapeDtypeStruct((n, KV_W), F32), jax.ShapeDtypeStruct((n, D_RNN), F32),
                  jax.ShapeDtypeStruct((n, D_RNN), F32))
    return pl.pallas_call(
        _in_proj_kernel,
        grid=(n // tm,),
        in_specs=[pl.BlockSpec((tm, D_MODEL), row), pl.BlockSpec((1, D_MODEL), fix),
                  pl.BlockSpec((D_MODEL, D_IN), fix), pl.BlockSpec((1, LANES), fix),
                  pl.BlockSpec((1, LANES), fix), pl.BlockSpec((tm, LANES), tab),
                  pl.BlockSpec((tm, LANES), tab), pl.BlockSpec((tm, LANES), tab)],
        out_specs=(pl.BlockSpec((tm, D_ATTN), row), pl.BlockSpec((tm, KV_W), row),
                   pl.BlockSpec((tm, KV_W), row), pl.BlockSpec((tm, D_RNN), row),
                   pl.BlockSpec((tm, D_RNN), row)),
        out_shape=out_shapes,
        compiler_params=_cparams(1),
        name="in_proj",
    )(x2d, g, w_bf, gq2, gk2, ctab, s1tab, s2tab)


def _rope_tables(pos):
    half = ROT_DIM // 2
    inv = ROPE_THETA ** (-jnp.arange(0, ROT_DIM, 2, dtype=F32) / ROT_DIM)
    ang = pos.astype(F32)[:, None] * inv[None, :]
    cos = jnp.cos(ang)
    sin = jnp.sin(ang)
    n = pos.shape[0]
    ones = jnp.ones((n, HEAD_DIM - ROT_DIM), F32)
    zeros = jnp.zeros((n, HEAD_DIM - ROT_DIM), F32)
    zh = jnp.zeros((n, half), F32)
    c = jnp.concatenate([cos, cos, ones], axis=1)
    s1 = jnp.concatenate([-sin, zh, zeros], axis=1)
    s2 = jnp.concatenate([zh, sin, zeros], axis=1)
    two = lambda t: jnp.concatenate([t, t], axis=1)
    return two(c), two(s1), two(s2)


def _band_bias(qb):
    qi = jnp.arange(qb, dtype=I32)[:, None]
    c = jnp.arange(2 * qb, dtype=I32)[None, :]
    band = (c >= qi) & (c <= qi + qb)
    first = band & (c >= qb)
    one = jnp.where(jnp.stack([first, band]), 0.0, NEG_BIG).astype(F32)
    return jnp.concatenate([one, one], axis=2)


def _attn_prompt_kernel(sink_ref, q_ref, kc_ref, kp_ref, vc_ref, vp_ref, bias_ref, g_ref,
                        o_ref, kt_ref, vt_ref, s_ref, e_ref):
    j = pl.program_id(1)
    qb = ATTN_BLOCK
    kc = kc_ref[...]
    vc = vc_ref[...]
    k2 = jnp.concatenate([kp_ref[...], kc], axis=0)
    v2 = jnp.concatenate([vp_ref[...], vc], axis=0)
    k2r = pltpu.roll(k2, HEAD_DIM, 1)
    v2r = pltpu.roll(v2, HEAD_DIM, 1)
    lo_k = lax.broadcasted_iota(I32, (2 * qb, LANES), 1) < HEAD_DIM
    lo_q = lax.broadcasted_iota(I32, (qb, LANES), 1) < HEAD_DIM
    bias = bias_ref[0]
    nt = (((1,), (1,)), ((), ()))
    n_pairs = N_HEADS // 2
    kbd, vbd = [], []
    for kv in range(N_KV_HEADS):
        ka, kb = (k2, k2r) if kv == 0 else (k2r, k2)
        va, vb = (v2, v2r) if kv == 0 else (v2r, v2)
        kbd.append(jnp.concatenate([jnp.where(lo_k, ka, 0.0), jnp.where(lo_k, 0.0, kb)], axis=0).astype(BF16))
        vbd.append(jnp.concatenate([jnp.where(lo_k, va, 0.0), jnp.where(lo_k, 0.0, vb)], axis=0).astype(BF16))
    for pp in range(n_pairs):
        qp = (q_ref[:, pp * LANES:(pp + 1) * LANES] * (HEAD_DIM ** -0.5)).astype(BF16)
        s_ref[pp] = lax.dot_general(qp, kbd[pp // (GROUP // 2)], nt, preferred_element_type=F32) + bias
    invs = []
    for pp in range(n_pairs):
        inv = []
        for t in range(2):
            cols = slice(t * 2 * qb, (t + 1) * 2 * qb)
            st = s_ref[pp, :, cols]
            sink = sink_ref[2 * pp + t]
            m = jnp.maximum(jnp.max(st, axis=-1, keepdims=True), sink)
            e = jnp.exp(st - m)
            e_ref[pp, :, cols] = e.astype(BF16)
            inv.append(1.0 / (jnp.sum(e, axis=-1, keepdims=True) + jnp.exp(sink - m)))
        invs.append(jnp.where(lo_q, inv[0], inv[1]))
    outs = [jnp.dot(e_ref[pp], vbd[pp // (GROUP // 2)], preferred_element_type=F32) * invs[pp]
            for pp in range(n_pairs)]
    o_ref[...] = _rmsnorm(jnp.concatenate(outs, axis=1), g_ref[...])

    @pl.when(j == pl.num_programs(1) - 1)
    def _():
        kt_ref[0] = kc.T
        vt_ref[0] = vc.T


def _attn_prompt(q, k, v, sinks, g_attn, batch, seq):
    qb = ATTN_BLOCK
    nb = seq // qb
    cur = lambda b, j: (b * nb + j, 0)
    prev = lambda b, j: (b * nb + jnp.maximum(j - 1, 0), 0)
    fix = lambda b, j: (0, 0)
    per_b = lambda b, j: (b, 0, 0)
    return pl.pallas_call(
        _attn_prompt_kernel,
        grid=(batch, nb),
        in_specs=[pl.BlockSpec(memory_space=pltpu.SMEM),
                  pl.BlockSpec((qb, D_ATTN), cur),
                  pl.BlockSpec((qb, KV_W), cur), pl.BlockSpec((qb, KV_W), prev),
                  pl.BlockSpec((qb, KV_W), cur), pl.BlockSpec((qb, KV_W), prev),
                  pl.BlockSpec((1, qb, 4 * qb), lambda b, j: (jnp.minimum(j, 1), 0, 0)),
                  pl.BlockSpec((1, D_ATTN), fix)],
        out_specs=(pl.BlockSpec((qb, D_ATTN), cur), pl.BlockSpec((1, KV_W, qb), per_b),
                   pl.BlockSpec((1, KV_W, qb), per_b)),
        out_shape=(jax.ShapeDtypeStruct((batch * seq, D_ATTN), F32),
                   jax.ShapeDtypeStruct((batch, KV_W, qb), F32),
                   jax.ShapeDtypeStruct((batch, KV_W, qb), F32)),
        scratch_shapes=[pltpu.VMEM((N_HEADS // 2, qb, 4 * qb), F32),
                        pltpu.VMEM((N_HEADS // 2, qb, 4 * qb), BF16)],
        compiler_params=_cparams(2),
        name="attn_prompt",
    )(sinks, q, k, k, v, v, _band_bias(qb), g_attn)


def _attn_sample_kernel(sink_ref, q_ref, kn_ref, vn_ref, kt_ref, vt_ref, g_ref,
                        o_ref, nkt_ref, nvt_ref, acc_ref):
    bb = q_ref.shape[0]
    q = q_ref[...] * (HEAD_DIM ** -0.5)
    kn = kn_ref[...]
    vn = vn_ref[...]
    kt = kt_ref[...]
    vt = vt_ref[...]
    col = lax.broadcasted_iota(I32, (bb, bb * KV_W), 1)
    rowb = lax.broadcasted_iota(I32, (bb, bb * KV_W), 0)
    own_seq = (col >> (KV_W.bit_length() - 1)) == rowb
    half_hi = ((col >> (HEAD_DIM.bit_length() - 1)) & 1) == 1
    qbig = []
    for h in range(N_HEADS):
        kv = h // GROUP
        pair = q[:, (h // 2) * LANES:(h // 2 + 1) * LANES]
        if (h % 2) != kv:
            pair = pltpu.roll(pair, HEAD_DIM, 1)
        tiled = jnp.concatenate([pair] * bb, axis=1)
        keep = own_seq & (half_hi if kv == 1 else jnp.logical_not(half_hi))
        qbig.append(jnp.where(keep, tiled, 0.0))
    qbig = jnp.concatenate(qbig, axis=0)
    s = jnp.dot(qbig.astype(BF16), kt.astype(BF16), preferred_element_type=F32)
    qb16 = q.astype(BF16).astype(F32)
    kb16 = kn.astype(BF16).astype(F32)
    s_new, sink = [], []
    for h in range(N_HEADS):
        kv = h // GROUP
        s_new.append(jnp.sum(qb16[:, h * HEAD_DIM:(h + 1) * HEAD_DIM] * kb16[:, kv * HEAD_DIM:(kv + 1) * HEAD_DIM],
                             axis=-1, keepdims=True))
        sink.append(jnp.full((bb, 1), sink_ref[h], F32))
    s_new = jnp.concatenate(s_new, axis=0)
    sink = jnp.concatenate(sink, axis=0)
    m = jnp.maximum(jnp.maximum(jnp.max(s, axis=-1, keepdims=True), s_new), sink)
    e = jnp.exp(s - m)
    e_new = jnp.exp(s_new - m)
    inv = 1.0 / (jnp.sum(e, axis=-1, keepdims=True) + e_new + jnp.exp(sink - m))
    obig = lax.dot_general(e.astype(BF16), vt.astype(BF16), (((1,), (1,)), ((), ())),
                           preferred_element_type=F32)
    for h in range(N_HEADS):
        kv = h // GROUP
        blk = jnp.where(own_seq, obig[h * bb:(h + 1) * bb, :], 0.0)
        fold = blk[:, 0:KV_W]
        for t in range(1, bb):
            fold = fold + blk[:, t * KV_W:(t + 1) * KV_W]
        hs = slice(h * bb, (h + 1) * bb)
        ks = slice(kv * HEAD_DIM, (kv + 1) * HEAD_DIM)
        acc_ref[:, h * HEAD_DIM:(h + 1) * HEAD_DIM] = (fold[:, ks] + e_new[hs] * vn[:, ks]) * inv[hs]
    o_ref[...] = _rmsnorm(acc_ref[...], g_ref[...])

    last = lax.broadcasted_iota(I32, (KV_W, WINDOW), 1) == WINDOW - 1
    for b in range(bb):
        rs = slice(b * KV_W, (b + 1) * KV_W)
        kcol = jnp.broadcast_to(kn[b:b + 1, :], (KV_W, KV_W)).T
        vcol = jnp.broadcast_to(vn[b:b + 1, :], (KV_W, KV_W)).T
        nkt_ref[rs, :] = jnp.where(last, kcol, pltpu.roll(kt[rs, :], WINDOW - 1, 1))
        nvt_ref[rs, :] = jnp.where(last, vcol, pltpu.roll(vt[rs, :], WINDOW - 1, 1))


def _attn_sample(q, kn, vn, kt2d, vt2d, sinks, g_attn, bb):
    n = q.shape[0]
    row = lambda i: (i, 0)
    fix = lambda i: (0, 0)
    cache = pl.BlockSpec((bb * KV_W, WINDOW), row)
    return pl.pallas_call(
        _attn_sample_kernel,
        grid=(n // bb,),
        in_specs=[pl.BlockSpec(memory_space=pltpu.SMEM),
                  pl.BlockSpec((bb, D_ATTN), row), pl.BlockSpec((bb, KV_W), row),
                  pl.BlockSpec((bb, KV_W), row), cache, cache,
                  pl.BlockSpec((1, D_ATTN), fix)],
        out_specs=(pl.BlockSpec((bb, D_ATTN), row), cache, cache),
        out_shape=(jax.ShapeDtypeStruct((n, D_ATTN), F32),
                   jax.ShapeDtypeStruct(kt2d.shape, F32), jax.ShapeDtypeStruct(vt2d.shape, F32)),
        scratch_shapes=[pltpu.VMEM((bb, D_ATTN), F32)],
        compiler_params=_cparams(1),
        name="attn_sample",
    )(sinks, q, kn, vn, kt2d, vt2d, g_attn)


def _softplus(z):
    return jnp.maximum(z, 0.0) + jnp.log1p(jnp.exp(-jnp.abs(z)))


def _lru_gates(xc, wa_ref, ba_ref, wx_ref, bx_ref, lam_ref):
    xb = xc.astype(BF16)
    r = jax.nn.sigmoid(jnp.dot(xb, wa_ref[...], preferred_element_type=F32) + ba_ref[...])
    i = jax.nn.sigmoid(jnp.dot(xb, wx_ref[...], preferred_element_type=F32) + bx_ref[...])
    log_a = (-LRU_C * r) * _softplus(-lam_ref[...])
    a = jnp.exp(log_a)
    z = -jnp.tanh(log_a) * (a * a + 1.0)
    u = jnp.where(z > 0.0, z * lax.rsqrt(z), 0.0) * (i * xc)
    return a, u


def _lru_scan(a, u, h0):
    ng = a.shape[0] // SUBLANES
    a3 = a.reshape(ng, SUBLANES, D_RNN)
    u3 = u.reshape(ng, SUBLANES, D_RNN)
    t8 = lax.broadcasted_iota(I32, (ng, SUBLANES, D_RNN), 1)
    d = 1
    while d < SUBLANES:
        a_s = jnp.where(t8 >= d, pltpu.roll(a3, d, 1), 1.0)
        u_s = jnp.where(t8 >= d, pltpu.roll(u3, d, 1), 0.0)
        u3 = a3 * u_s + u3
        a3 = a3 * a_s
        d *= 2
    carry = h0
    groups = []
    for g in range(ng):
        hg = a3[g] * carry + u3[g]
        groups.append(hg)
        carry = hg[SUBLANES - 1:SUBLANES, :]
    return jnp.concatenate(groups, axis=0), carry


def _lru_scan_tiles(a_ref, u_ref, h_ref, h0):
    nl, rows, _ = a_ref.shape
    ng = rows // SUBLANES
    step = lambda ref, s: jnp.concatenate(
        [ref[j, pl.ds(s, ng, stride=SUBLANES), :] for j in range(nl)], axis=1)
    prods = [step(a_ref, 0)]
    locs = [step(u_ref, 0)]
    for s in range(1, SUBLANES):
        a_s = step(a_ref, s)
        locs.append(a_s * locs[-1] + step(u_ref, s))
        prods.append(a_s * prods[-1])
    after, h_last = _lru_scan(prods[-1], locs[-1], h0)
    row = lax.broadcasted_iota(I32, (ng, D_RNN), 0)
    before = jnp.where(row == 0, h0, pltpu.roll(after, 1, 0))
    for s in range(SUBLANES):
        h_s = locs[s] + prods[s] * before
        for j in range(nl):
            h_ref[j, pl.ds(s, ng, stride=SUBLANES), :] = h_s[:, j * LANES:(j + 1) * LANES]
    return h_last


def _to_lane_tiles(ref, x):
    for j in range(ref.shape[0]):
        ref[j] = x[:, j * LANES:(j + 1) * LANES]


def _rnn_prompt_kernel(xr_ref, yr_ref, cw_ref, cb_ref, wa_ref, ba_ref, wx_ref, bx_ref, lam_ref, g_ref,
                       o_ref, hl_ref, ext_ref, h_ref):
    c = pl.program_id(1)
    tc = xr_ref.shape[0]
    pad = SUBLANES

    @pl.when(c == 0)
    def _():
        ext_ref[0:pad, :] = jnp.zeros((pad, D_RNN), F32)
        h_ref[...] = jnp.zeros((1, D_RNN), F32)

    ext_ref[pad:pad + tc, :] = xr_ref[...]
    cw = cw_ref[...]
    xc = cb_ref[...] + ext_ref[pad:pad + tc, :] * cw[CONV_WIDTH - 1:CONV_WIDTH, :]
    for w in range(CONV_WIDTH - 1):
        sh = CONV_WIDTH - 1 - w
        xc = xc + ext_ref[pad - sh:pad - sh + tc, :] * cw[w:w + 1, :]
    ext_ref[0:pad, :] = ext_ref[tc:tc + pad, :]

    a, u = _lru_gates(xc, wa_ref, ba_ref, wx_ref, bx_ref, lam_ref)
    h, carry = _lru_scan(a, u, h_ref[...])
    h_ref[...] = carry
    hl_ref[0] = carry
    o_ref[...] = _rmsnorm(jax.nn.gelu(yr_ref[...]) * h, g_ref[...])


def _rnn_prompt(xr, yr, cw, cb, wa, ba, wx, bx, lam, g, batch, seq, tc):
    nc = seq // tc
    cur = lambda b, c: (b * nc + c, 0)
    fix = lambda b, c: (0, 0)
    vec = pl.BlockSpec((1, D_RNN), fix)
    return pl.pallas_call(
        _rnn_prompt_kernel,
        grid=(batch, nc),
        in_specs=[pl.BlockSpec((tc, D_RNN), cur), pl.BlockSpec((tc, D_RNN), cur),
                  pl.BlockSpec((CONV_WIDTH, D_RNN), fix), vec,
                  pl.BlockSpec((D_RNN, D_RNN), fix), vec,
                  pl.BlockSpec((D_RNN, D_RNN), fix), vec, vec, vec],
        out_specs=(pl.BlockSpec((tc, D_RNN), cur), pl.BlockSpec((1, 1, D_RNN), lambda b, c: (b, 0, 0))),
        out_shape=(jax.ShapeDtypeStruct((batch * seq, D_RNN), F32),
                   jax.ShapeDtypeStruct((batch, 1, D_RNN), F32)),
        scratch_shapes=[pltpu.VMEM((tc + SUBLANES, D_RNN), F32), pltpu.VMEM((1, D_RNN), F32)],
        compiler_params=_cparams(2),
        name="rnn_prompt",
    )(xr, yr, cw, cb, wa, ba, wx, bx, lam, g)


def _rnn_sample_kernel(xr_ref, yr_ref, hist_ref, h0_ref, cw_ref, cb_ref, wa_ref, ba_ref, wx_ref, bx_ref,
                       lam_ref, g_ref, o_ref, hl_ref, nh_ref):
    cw = cw_ref[...]
    xr = xr_ref[...]
    xc = cb_ref[...] + xr * cw[CONV_WIDTH - 1:CONV_WIDTH, :]
    for w in range(CONV_WIDTH - 1):
        xc = xc + hist_ref[w] * cw[w:w + 1, :]
    a, u = _lru_gates(xc, wa_ref, ba_ref, wx_ref, bx_ref, lam_ref)
    h = a * h0_ref[...] + u
    hl_ref[...] = h
    o_ref[...] = _rmsnorm(jax.nn.gelu(yr_ref[...]) * h, g_ref[...])
    for w in range(CONV_WIDTH - 2):
        nh_ref[w] = hist_ref[w + 1]
    nh_ref[CONV_WIDTH - 2] = xr


def _rnn_sample(xr, yr, hist, h0, cw, cb, wa, ba, wx, bx, lam, g):
    n = xr.shape[0]
    full = lambda a: pl.BlockSpec(a.shape, lambda: (0,) * a.ndim)
    args = (xr, yr, hist, h0, cw, cb, wa, ba, wx, bx, lam, g)
    return pl.pallas_call(
        _rnn_sample_kernel,
        in_specs=[full(a) for a in args],
        out_specs=(pl.BlockSpec((n, D_RNN), lambda: (0, 0)), pl.BlockSpec((n, D_RNN), lambda: (0, 0)),
                   pl.BlockSpec(hist.shape, lambda: (0, 0, 0))),
        out_shape=(jax.ShapeDtypeStruct((n, D_RNN), F32), jax.ShapeDtypeStruct((n, D_RNN), F32),
                   jax.ShapeDtypeStruct(hist.shape, F32)),
        compiler_params=pltpu.CompilerParams(vmem_limit_bytes=VMEM_LIMIT),
        name="rnn_sample",
    )(*args)


def _front_kernel(*refs, tiles_per_seq):
    i = pl.program_id(0)
    q_s, k_s, v_s, xr_s, yr_s, ext_ref, h_ref = refs[25:32]

    @pl.when(i == 0)
    def _():
        for r in (q_s, k_s, v_s, xr_s, yr_s, ext_ref, h_ref):
            r[...] = jnp.zeros(r.shape, F32)

    for cur in range(2):
        @pl.when(i % 2 == cur)
        def _():
            _front_body(cur, 1 - cur, *refs, tiles_per_seq=tiles_per_seq)


def _front_body(cur, prv, sink_ref, x_ref, gm_ref, w_ref, gq_ref, gk_ref, c_ref, s1_ref, s2_ref, bias_ref,
                ga_ref, cw_ref, cb_ref, wa_ref, ba_ref, wx_ref, bx_ref, lam_ref, gr_ref,
                an_ref, rn_ref, kt_ref, vt_ref, hl_ref, cx_ref,
                q_s, k_s, v_s, xr_s, yr_s, ext_ref, h_ref, s_ref, e_ref, a_scr, u_scr, hs_scr,
                *, tiles_per_seq):
    i = pl.program_id(0)
    tm = x_ref.shape[0]
    qb = ATTN_BLOCK
    t = jnp.maximum(i - 1, 0)
    first_tile = (t % tiles_per_seq) == 0

    hx = _rmsnorm(x_ref[...], gm_ref[...]).astype(BF16)
    rope = (c_ref[...], s1_ref[...], s2_ref[...], lax.broadcasted_iota(I32, (tm, LANES), 1) < HEAD_DIM)
    project = lambda c0, c1: jnp.dot(hx, w_ref[:, c0:c1], preferred_element_type=F32)

    nqb = tm // qb
    n_pairs = N_HEADS // 2
    lo_k = lax.broadcasted_iota(I32, (2 * qb, LANES), 1) < HEAD_DIM
    lo_q = lax.broadcasted_iota(I32, (qb, LANES), 1) < HEAD_DIM
    nt_dims = (((1,), (1,)), ((), ()))
    vbds = []
    for jb in range(nqb):
        k2 = k_s[prv, jb * qb:(jb + 2) * qb, :]
        v2 = v_s[prv, jb * qb:(jb + 2) * qb, :]
        k2r = pltpu.roll(k2, HEAD_DIM, 1)
        v2r = pltpu.roll(v2, HEAD_DIM, 1)
        bias = bias_ref[jnp.where(first_tile, 0, 1)] if jb == 0 else bias_ref[1]
        for kv in range(N_KV_HEADS):
            ka, kb = (k2, k2r) if kv == 0 else (k2r, k2)
            va, vb = (v2, v2r) if kv == 0 else (v2r, v2)
            kbd = jnp.concatenate([jnp.where(lo_k, ka, 0.0), jnp.where(lo_k, 0.0, kb)], axis=0).astype(BF16)
            vbds.append(jnp.concatenate([jnp.where(lo_k, va, 0.0), jnp.where(lo_k, 0.0, vb)],
                                        axis=0).astype(BF16))
            for p in range(GROUP // 2):
                pp = kv * (GROUP // 2) + p
                qp = (q_s[prv, jb * qb:(jb + 1) * qb, pp * LANES:(pp + 1) * LANES]
                      * (HEAD_DIM ** -0.5)).astype(BF16)
                s_ref[jb * n_pairs + pp] = lax.dot_general(qp, kbd, nt_dims,
                                                           preferred_element_type=F32) + bias
    pq = project(0, D_ATTN)
    for j in range(D_ATTN // LANES):
        q_s[cur, :, j * LANES:(j + 1) * LANES] = _head_norm_rope(pq[:, j * LANES:(j + 1) * LANES], gq_ref[...],
                                                                 *rope)
    invs = []
    for c in range(nqb * n_pairs):
        pp = c % n_pairs
        inv = []
        for tpos in range(2):
            cols = slice(tpos * 2 * qb, (tpos + 1) * 2 * qb)
            st = s_ref[c, :, cols]
            sink = sink_ref[2 * pp + tpos]
            m = jnp.maximum(jnp.max(st, axis=-1, keepdims=True), sink)
            e = jnp.exp(st - m)
            e_ref[c, :, cols] = e.astype(BF16)
            inv.append(1.0 / (jnp.sum(e, axis=-1, keepdims=True) + jnp.exp(sink - m)))
        invs.append(jnp.where(lo_q, inv[0], inv[1]))
    pkv = project(D_ATTN, D_ATTN + 2 * KV_W)
    k_s[cur, qb:qb + tm, :] = _head_norm_rope(pkv[:, :KV_W], gk_ref[...], *rope)
    v_s[cur, qb:qb + tm, :] = pkv[:, KV_W:]
    k_s[cur, 0:qb, :] = k_s[prv, tm:tm + qb, :]
    v_s[cur, 0:qb, :] = v_s[prv, tm:tm + qb, :]
    for jb in range(nqb):
        outs = [jnp.dot(e_ref[jb * n_pairs + pp], vbds[jb * N_KV_HEADS + pp // (GROUP // 2)],
                        preferred_element_type=F32) * invs[jb * n_pairs + pp] for pp in range(n_pairs)]
        an_ref[jb * qb:(jb + 1) * qb, :] = _rmsnorm(jnp.concatenate(outs, axis=1), ga_ref[...])
    kt_ref[0] = k_s[prv, tm:tm + qb, :].T
    vt_ref[0] = v_s[prv, tm:tm + qb, :].T

    o = D_ATTN + 2 * KV_W
    xr_s[cur] = project(o, o + D_RNN)

    pad = SUBLANES
    xr = xr_s[prv]
    ext_ref[0:pad, :] = jnp.where(first_tile, 0.0, ext_ref[0:pad, :])
    ext_ref[pad:pad + tm, :] = xr
    cw = cw_ref[...]
    xc = cb_ref[...] + xr * cw[CONV_WIDTH - 1:CONV_WIDTH, :]
    for w in range(CONV_WIDTH - 1):
        sh = CONV_WIDTH - 1 - w
        xc = xc + ext_ref[pad - sh:pad - sh + tm, :] * cw[w:w + 1, :]
    ext_ref[0:pad, :] = xr[tm - pad:tm, :]
    cx_ref[0] = xr[tm - pad:tm, :]
    a, u = _lru_gates(xc, wa_ref, ba_ref, wx_ref, bx_ref, lam_ref)
    yr_s[cur] = project(o + D_RNN, o + 2 * D_RNN)
    _to_lane_tiles(a_scr, a)
    _to_lane_tiles(u_scr, u)
    carry = _lru_scan_tiles(a_scr, u_scr, hs_scr, jnp.where(first_tile, 0.0, h_ref[...]))
    h_ref[...] = carry
    hl_ref[0] = carry
    hseq = jnp.concatenate([hs_scr[j] for j in range(hs_scr.shape[0])], axis=1)
    rn_ref[...] = _rmsnorm(jax.nn.gelu(yr_s[prv]) * hseq, gr_ref[...])


def _front(x2d, sinks, g_mix, w_bf, gq2, gk2, ctab, s1tab, s2tab, g_attn,
           cw, cb, wa, ba, wx, bx, lam, g_rnn, batch, seq, tm):
    n = x2d.shape[0]
    nt = n // tm
    tps = seq // tm
    qb = ATTN_BLOCK
    cur = lambda i: (jnp.minimum(i, nt - 1), 0)
    tab = lambda i: (jnp.minimum(i, nt - 1) % tps, 0)
    fix = lambda i: (0, 0)
    prev = lambda i: (jnp.maximum(i - 1, 0), 0)
    per_seq = lambda i: (jnp.maximum(i - 1, 0) // tps, 0, 0)
    vec = lambda w: pl.BlockSpec((1, w), fix)
    return pl.pallas_call(
        functools.partial(_front_kernel, tiles_per_seq=tps),
        grid=(nt + 1,),
        in_specs=[pl.BlockSpec(memory_space=pltpu.SMEM),
                  pl.BlockSpec((tm, D_MODEL), cur), vec(D_MODEL), pl.BlockSpec((D_MODEL, D_IN), fix),
                  vec(LANES), vec(LANES),
                  pl.BlockSpec((tm, LANES), tab), pl.BlockSpec((tm, LANES), tab), pl.BlockSpec((tm, LANES), tab),
                  pl.BlockSpec((2, qb, 4 * qb), lambda i: (0, 0, 0)), vec(D_ATTN),
                  pl.BlockSpec((CONV_WIDTH, D_RNN), fix), vec(D_RNN),
                  pl.BlockSpec((D_RNN, D_RNN), fix), vec(D_RNN),
                  pl.BlockSpec((D_RNN, D_RNN), fix), vec(D_RNN), vec(D_RNN), vec(D_RNN)],
        out_specs=(pl.BlockSpec((tm, D_ATTN), prev), pl.BlockSpec((tm, D_RNN), prev),
                   pl.BlockSpec((1, KV_W, qb), per_seq), pl.BlockSpec((1, KV_W, qb), per_seq),
                   pl.BlockSpec((1, 1, D_RNN), per_seq), pl.BlockSpec((1, SUBLANES, D_RNN), per_seq)),
        out_shape=(jax.ShapeDtypeStruct((n, D_ATTN), F32), jax.ShapeDtypeStruct((n, D_RNN), F32),
                   jax.ShapeDtypeStruct((batch, KV_W, qb), F32), jax.ShapeDtypeStruct((batch, KV_W, qb), F32),
                   jax.ShapeDtypeStruct((batch, 1, D_RNN), F32),
                   jax.ShapeDtypeStruct((batch, SUBLANES, D_RNN), F32)),
        scratch_shapes=[pltpu.VMEM((2, tm, D_ATTN), F32),
                        pltpu.VMEM((2, tm + qb, KV_W), F32), pltpu.VMEM((2, tm + qb, KV_W), F32),
                        pltpu.VMEM((2, tm, D_RNN), F32), pltpu.VMEM((2, tm, D_RNN), F32),
                        pltpu.VMEM((tm + SUBLANES, D_RNN), F32), pltpu.VMEM((1, D_RNN), F32),
                        pltpu.VMEM((tm // qb * (N_HEADS // 2), qb, 4 * qb), F32),
                        pltpu.VMEM((tm // qb * (N_HEADS // 2), qb, 4 * qb), BF16),
                        pltpu.VMEM((D_RNN // LANES, tm, LANES), F32), pltpu.VMEM((D_RNN // LANES, tm, LANES), F32),
                        pltpu.VMEM((D_RNN // LANES, tm, LANES), F32)],
        compiler_params=_cparams(1),
        name="front",
    )(sinks, x2d, g_mix, w_bf, gq2, gk2, ctab, s1tab, s2tab, _band_bias(qb), g_attn,
      cw, cb, wa, ba, wx, bx, lam, g_rnn)


def _mix_route_kernel(*refs):
    i = pl.program_id(0)
    hb_s, d_s = refs[16:18]

    @pl.when(i == 0)
    def _():
        hb_s[...] = jnp.zeros(hb_s.shape, BF16)
        d_s[...] = jnp.zeros(d_s.shape, I32)

    for cur in range(2):
        @pl.when(i % 2 == cur)
        def _():
            _mix_route_body(cur, 1 - cur, *refs)


def _mix_route_body(cur, prv, x_ref, an_ref, rn_ref, woa_ref, wor_ref, g_ref, wr2_ref, br_ref, tri_ref, low_ref,
                    x2_ref, ts_ref, dest_ref, gate_ref, n8_ref, off_ref, hb_s, d_s):
    tt = x_ref.shape[0]
    tile_rows = ts_ref.shape[0]
    n_chunks = tile_rows // DISPATCH_CHUNK

    hb_prev = hb_s[prv]
    d16 = [d_s[prv, k:k + 1, :].astype(I16) for k in range(TOP_K)]
    ri = lax.broadcasted_iota(I32, (DISPATCH_CHUNK, tt), 0).astype(I16)
    one = jnp.ones((DISPATCH_CHUNK, tt), BF16)

    def dispatch(chunks):
        for c in chunks:
            p = jnp.zeros((DISPATCH_CHUNK, tt), BF16)
            for d in d16:
                p = jnp.where(ri == d - jnp.int16(c * DISPATCH_CHUNK), one, p)
            ts_ref[c * DISPATCH_CHUNK:(c + 1) * DISPATCH_CHUNK, :] = _pack_bf16_pairs(
                jnp.dot(p, hb_prev, preferred_element_type=F32))

    third = -(-n_chunks // 3)

    x2 = x_ref[...] + jnp.dot(an_ref[...].astype(BF16), woa_ref[...], preferred_element_type=F32) \
        + jnp.dot(rn_ref[...].astype(BF16), wor_ref[...], preferred_element_type=F32)
    x2_ref[...] = x2
    dispatch(range(0, third))
    hn = _rmsnorm(x2, g_ref[...])

    nt = (((1,), (1,)), ((), ()))
    hb = hn.astype(BF16)
    hb_s[cur] = hb
    hmid = (hn - hb.astype(F32)).astype(BF16)
    wr2 = wr2_ref[...]
    both = lax.dot_general(wr2, hb, nt, preferred_element_type=F32)
    logits = (lax.dot_general(wr2[:N_EXPERTS], hmid, nt, preferred_element_type=F32)
              + both[N_EXPERTS:]) + both[:N_EXPERTS] + br_ref[...]
    dispatch(range(third, 2 * third))

    ie = lax.broadcasted_iota(I32, (N_EXPERTS, tt), 0).astype(F32)
    l = logits
    vals, sels = [], []
    for _ in range(TOP_K):
        m = jnp.max(l, axis=0, keepdims=True)
        idx = jnp.min(jnp.where(l == m, ie, float(N_EXPERTS)), axis=0, keepdims=True)
        sel = ie == idx
        vals.append(m)
        sels.append(sel)
        l = jnp.where(sel, NEG_BIG, l)
    es = [jnp.exp(v - vals[0]) for v in vals]
    den = es[0] + es[1] + es[2] + es[3]
    gate_ref[0] = jnp.concatenate([e / den for e in es], axis=0)
    dispatch(range(2 * third, n_chunks))

    oh = jnp.zeros((N_EXPERTS, tt), F32)
    for sel in sels:
        oh = oh + jnp.where(sel, 1.0, 0.0)
    before = jnp.dot(oh.astype(BF16), tri_ref[...], preferred_element_type=F32)
    cnt = jnp.sum(oh, axis=1, keepdims=True).astype(I32)
    n8 = ((cnt + (SUBLANES - 1)) >> 3) << 3
    n8b = jnp.broadcast_to(n8, (N_EXPERTS, LANES))
    off = jnp.dot(low_ref[...], n8b.astype(F32).astype(BF16), preferred_element_type=F32)
    n8_ref[0] = n8b
    off_ref[0] = off.astype(I32)
    base = off[:, 0:1] + before
    dests = jnp.concatenate(
        [jnp.sum(jnp.where(sel, base, 0.0), axis=0, keepdims=True).astype(I32) for sel in sels], axis=0)
    dest_ref[0] = dests
    d_s[cur, 0:TOP_K, :] = dests


def _tile_rows(tt):
    return -(-(TOP_K * tt + N_EXPERTS * (SUBLANES - 1)) // DISPATCH_CHUNK) * DISPATCH_CHUNK


def _mix_route(x2d, an, rn, woa, wor, g, wrt, br, low, tt):
    n = x2d.shape[0]
    nt = n // tt
    tile_rows = _tile_rows(tt)
    tri = jnp.triu(jnp.ones((tt, tt), BF16), k=1)
    row = lambda i: (jnp.minimum(i, nt - 1), 0)
    prev = lambda i: (jnp.maximum(i - 1, 0), 0)
    fix = lambda i: (0, 0)
    t3 = lambda i: (jnp.minimum(i, nt - 1), 0, 0)
    in_specs = [pl.BlockSpec((tt, D_MODEL), row), pl.BlockSpec((tt, D_ATTN), row),
                pl.BlockSpec((tt, D_RNN), row), pl.BlockSpec((D_ATTN, D_MODEL), fix),
                pl.BlockSpec((D_RNN, D_MODEL), fix), pl.BlockSpec((1, D_MODEL), fix),
                pl.BlockSpec((2 * N_EXPERTS, D_MODEL), fix), pl.BlockSpec((N_EXPERTS, 1), fix),
                pl.BlockSpec((tt, tt), fix), pl.BlockSpec((N_EXPERTS, N_EXPERTS), fix)]
    out_shape = (jax.ShapeDtypeStruct((n, D_MODEL), F32),
                 jax.ShapeDtypeStruct((nt * tile_rows, HALF), U32),
                 jax.ShapeDtypeStruct((nt, TOP_K, tt), I32),
                 jax.ShapeDtypeStruct((nt, TOP_K, tt), F32),
                 jax.ShapeDtypeStruct((nt, N_EXPERTS, LANES), I32),
                 jax.ShapeDtypeStruct((nt, N_EXPERTS, LANES), I32))
    out_specs = (pl.BlockSpec((tt, D_MODEL), row),
                 pl.BlockSpec((tile_rows, HALF), prev),
                 pl.BlockSpec((1, TOP_K, tt), t3), pl.BlockSpec((1, TOP_K, tt), t3),
                 pl.BlockSpec((1, N_EXPERTS, LANES), t3), pl.BlockSpec((1, N_EXPERTS, LANES), t3))
    return pl.pallas_call(
        _mix_route_kernel,
        grid=(nt + 1,),
        in_specs=in_specs,
        out_specs=out_specs,
        out_shape=out_shape,
        scratch_shapes=[pltpu.VMEM((2, tt, D_MODEL), BF16), pltpu.VMEM((2, SUBLANES, tt), I32)],
        compiler_params=_cparams(1),
        name="mix_route",
    )(x2d, an, rn, woa, wor, g, wrt, br, tri, low)


LOW_BITS = 4


def _start_piece(src_hbm, dst_buf, sem, s, d, l8, nbits):
    def bit_copy(c):
        size = SUBLANES << c
        low = (l8 & ((1 << c) - 1)) * SUBLANES

        @pl.when(((l8 >> c) & 1) == 1)
        def _():
            pltpu.make_async_copy(
                src_hbm.at[pl.ds(pl.multiple_of(s + low, SUBLANES), size)],
                dst_buf.at[pl.ds(pl.multiple_of(d + low, SUBLANES), size)], sem).start()

    for c in range(min(LOW_BITS, nbits)):
        bit_copy(c)
    if nbits > LOW_BITS:
        def long_copies(_, carry):
            for c in range(LOW_BITS, nbits):
                bit_copy(c)
            return carry

        lax.fori_loop(0, jnp.where(l8 >= (1 << LOW_BITS), 1, 0), long_copies, 0)


def _start_pieces(src_hbm, dst_buf, sem, p_lo, p_hi, psrc_ref, pdst_ref, plen_ref, nbits):
    def body(p, carry):
        _start_piece(src_hbm, dst_buf, sem, psrc_ref[p], pdst_ref[p], plen_ref[p], nbits)
        return carry

    lax.fori_loop(p_lo, p_hi, body, 0)


def _wait_rows(src_hbm, dst_buf, sem, rows8, nbits):
    for c in range(nbits):
        size = SUBLANES << c

        @pl.when(((rows8 >> c) & 1) == 1)
        def _():
            pltpu.make_async_copy(src_hbm.at[pl.ds(0, size)], dst_buf.at[pl.ds(0, size)], sem).wait()


def _moe_gmm_kernel(be_ref, rows_ref, wslot_ref, nxt_ref,
                    psa_ref, pea_ref, srca_ref, dsta_ref, lena_ref, hsrca_ref, hlena_ref,
                    psb_ref, peb_ref, srcb_ref, dstb_ref, lenb_ref, hsrcb_ref, hlenb_ref,
                    tsa_hbm, tsb_hbm, wgu_hbm, wdn_hbm, bg_ref, bu_ref, bd_ref, perm_ref,
                    ys_ref, lhs_ref, wgu_buf, wdn_buf, wg_ref, wu_ref, wd_ref, sem_ref, wsem_ref,
                    *, nbits_a, nbits_b):
    j = pl.program_id(0)
    nb = pl.num_programs(0)
    slot = j % 2
    bm = lhs_ref.shape[1]

    def gather(blk, sl):
        for ts_hbm, ps, pe, src, dst, ln, hsrc, hlen, nbits in (
                (tsa_hbm, psa_ref, pea_ref, srca_ref, dsta_ref, lena_ref, hsrca_ref, hlena_ref, nbits_a),
                (tsb_hbm, psb_ref, peb_ref, srcb_ref, dstb_ref, lenb_ref, hsrcb_ref, hlenb_ref, nbits_b)):
            _start_pieces(ts_hbm, lhs_ref.at[sl], sem_ref.at[sl], ps[blk], pe[blk], src, dst, ln, nbits)
            _start_piece(ts_hbm, lhs_ref.at[sl], sem_ref.at[sl], hsrc[blk], 0, hlen[blk], nbits)

    def weight_copies(e, ws):
        return (pltpu.make_async_copy(wgu_hbm.at[e], wgu_buf.at[ws], wsem_ref.at[ws]),
                pltpu.make_async_copy(wdn_hbm.at[e], wdn_buf.at[ws], wsem_ref.at[ws]))

    @pl.when(j == 0)
    def _():
        lhs_ref[...] = jnp.zeros(lhs_ref.shape, U32)
        gather(0, 0)
        for cp in weight_copies(be_ref[0], wslot_ref[0]):
            cp.start()

    @pl.when(j + 1 < nb)
    def _():
        gather(j + 1, 1 - slot)

    @pl.when(jnp.logical_or(j == 0, be_ref[j] != be_ref[jnp.maximum(j - 1, 0)]))
    def _():
        ws = wslot_ref[j]
        for cp in weight_copies(be_ref[j], ws):
            cp.wait()
        nxt = nxt_ref[j]

        @pl.when(nxt >= 0)
        def _():
            for cp in weight_copies(nxt, 1 - ws):
                cp.start()

        perm = perm_ref[...]
        half = PERM_COLS // 2
        for c in range(2 * D_FF // PERM_COLS):
            wb = wgu_buf[ws, :, c * PERM_COLS:(c + 1) * PERM_COLS].astype(BF16)
            wp = jnp.dot(wb, perm, preferred_element_type=F32).astype(BF16)
            wg_ref[:, c * half:(c + 1) * half] = wp[:, :half]
            wu_ref[:, c * half:(c + 1) * half] = wp[:, half:]
        wd_ref[...] = wdn_buf[ws].astype(BF16)

    rows8 = rows_ref[j]
    _wait_rows(tsa_hbm, lhs_ref.at[slot], sem_ref.at[slot], rows8, (bm // SUBLANES).bit_length())

    def expert_rows(rs):
        x = _unpack_bf16_pairs(lhs_ref[slot, rs, :])
        gate = jnp.dot(x, wg_ref[...], preferred_element_type=F32) + bg_ref[0]
        up = jnp.dot(x, wu_ref[...], preferred_element_type=F32) + bu_ref[0]
        gate = jnp.minimum(gate, SWIGLU_LIMIT)
        up = jnp.clip(up, -SWIGLU_LIMIT, SWIGLU_LIMIT)
        act = (up + 1.0) * (gate * jax.nn.sigmoid(SWIGLU_ALPHA * gate))
        y = jnp.dot(act.astype(BF16), wd_ref[...], preferred_element_type=F32) + bd_ref[0]
        ys_ref[rs, :] = _pack_bf16_pairs(y.astype(BF16).astype(F32))

    half8 = bm // 2 // SUBLANES

    @pl.when(rows8 > half8)
    def _():
        expert_rows(slice(0, bm))

    @pl.when(jnp.logical_and(rows8 > 0, rows8 <= half8))
    def _():
        expert_rows(slice(0, bm // 2))
        ys_ref[bm // 2:, :] = jnp.zeros((bm // 2, HALF), U32)

    @pl.when(rows8 == 0)
    def _():
        ys_ref[...] = jnp.zeros(ys_ref.shape, U32)


def _moe_gmm(blocks, tabs_a, tabs_b, ts_a, ts_b, w_gu, w_dn, bg, bu, bd, perm, nblocks, bm, nbits_a, nbits_b):
    we = lambda j, be, *_: (be[j], 0, 0)
    grid_spec = pltpu.PrefetchScalarGridSpec(
        num_scalar_prefetch=18,
        grid=(nblocks,),
        in_specs=[pl.BlockSpec(memory_space=pl.ANY), pl.BlockSpec(memory_space=pl.ANY),
                  pl.BlockSpec(memory_space=pl.ANY), pl.BlockSpec(memory_space=pl.ANY),
                  pl.BlockSpec((1, 1, D_FF), we), pl.BlockSpec((1, 1, D_FF), we),
                  pl.BlockSpec((1, 1, D_MODEL), we),
                  pl.BlockSpec((PERM_COLS, PERM_COLS), lambda j, *_: (0, 0))],
        out_specs=pl.BlockSpec((bm, HALF), lambda j, *_: (j, 0)),
        scratch_shapes=[pltpu.VMEM((2, bm, HALF), U32),
                        pltpu.VMEM((2, D_MODEL, 2 * D_FF), F32), pltpu.VMEM((2, D_FF, D_MODEL), F32),
                        pltpu.VMEM((D_MODEL, D_FF), BF16), pltpu.VMEM((D_MODEL, D_FF), BF16),
                        pltpu.VMEM((D_FF, D_MODEL), BF16),
                        pltpu.SemaphoreType.DMA((2,)), pltpu.SemaphoreType.DMA((2,))],
    )
    return pl.pallas_call(
        functools.partial(_moe_gmm_kernel, nbits_a=nbits_a, nbits_b=nbits_b),
        grid_spec=grid_spec,
        out_shape=jax.ShapeDtypeStruct((nblocks * bm, HALF), U32),
        compiler_params=_cparams(1),
        name="moe_gmm",
    )(*blocks, *tabs_a, *tabs_b, ts_a, ts_b, w_gu, w_dn, bg, bu, bd, perm)


def _combine_kernel(psrc_ref, pdst_ref, plen_ref, tlo_ref, thi_ref, tsrc_ref, tdst_ref, tlen_ref, rows_ref,
                    ys_hbm, x2_ref, dest_ref, gate_ref, o_ref,
                    buf_ref, db_ref, gb_ref, sem_ref, *, nbits):
    i = pl.program_id(0)
    n = pl.num_programs(0)
    slot = i % 2
    tt = x2_ref.shape[0]
    tile_rows = buf_ref.shape[1]

    def gather(tile, sl):
        _start_pieces(ys_hbm, buf_ref.at[sl], sem_ref.at[sl], tile * N_EXPERTS, (tile + 1) * N_EXPERTS,
                      psrc_ref, pdst_ref, plen_ref, nbits)
        _start_pieces(ys_hbm, buf_ref.at[sl], sem_ref.at[sl], tlo_ref[tile], thi_ref[tile],
                      tsrc_ref, tdst_ref, tlen_ref, nbits)

    @pl.when(i == 0)
    def _():
        buf_ref[...] = jnp.zeros(buf_ref.shape, U32)
        gather(0, 0)

    @pl.when(i + 1 < n)
    def _():
        gather(i + 1, 1 - slot)

    _wait_rows(ys_hbm, buf_ref.at[slot], sem_ref.at[slot], rows_ref[i], (tile_rows // SUBLANES).bit_length())

    dest = dest_ref[0]
    gate = gate_ref[0]
    for k in range(TOP_K):
        db_ref[k] = jnp.broadcast_to(dest[:, k:k + 1], (tt, DISPATCH_CHUNK)).astype(I16)
        gb_ref[k] = jnp.broadcast_to(gate[:, k:k + 1], (tt, DISPATCH_CHUNK)).astype(BF16)
    li = lax.broadcasted_iota(I32, (tt, DISPATCH_CHUNK), 1).astype(I16)
    gms = []
    for c in range(tile_rows // DISPATCH_CHUNK):
        lic = li + jnp.int16(c * DISPATCH_CHUNK)
        gm = jnp.zeros((tt, DISPATCH_CHUNK), BF16)
        for k in range(TOP_K):
            gm = jnp.where(lic == db_ref[k], gb_ref[k], gm)
        gms.append(gm)
    o_ref[...] = x2_ref[...] + jnp.dot(jnp.concatenate(gms, axis=1), _unpack_bf16_pairs(buf_ref[slot]),
                                       preferred_element_type=F32)


def _combine(tabs, ys, x2, dest, gate, tt, nbits):
    n = x2.shape[0]
    nt = n // tt
    tile_rows = _tile_rows(tt)
    grid_spec = pltpu.PrefetchScalarGridSpec(
        num_scalar_prefetch=len(tabs),
        grid=(nt,),
        in_specs=[pl.BlockSpec(memory_space=pl.ANY),
                  pl.BlockSpec((tt, D_MODEL), lambda i, *_: (i, 0)),
                  pl.BlockSpec((1, tt, TOP_K), lambda i, *_: (i, 0, 0)),
                  pl.BlockSpec((1, tt, TOP_K), lambda i, *_: (i, 0, 0))],
        out_specs=pl.BlockSpec((tt, D_MODEL), lambda i, *_: (i, 0)),
        scratch_shapes=[pltpu.VMEM((2, tile_rows, HALF), U32),
                        pltpu.VMEM((TOP_K, tt, DISPATCH_CHUNK), I16), pltpu.VMEM((TOP_K, tt, DISPATCH_CHUNK), BF16),
                        pltpu.SemaphoreType.DMA((2,))],
    )
    return pl.pallas_call(
        functools.partial(_combine_kernel, nbits=nbits),
        grid_spec=grid_spec,
        out_shape=jax.ShapeDtypeStruct((n, D_MODEL), F32),
        compiler_params=_cparams(1),
        name="combine",
    )(*tabs, ys, x2, dest, gate)


def _piece_tables(n8_a, off_a, rows_a, n8_b, off_b, bm, nblocks):
    nta = n8_a.shape[0]
    n8 = jnp.concatenate([n8_a, n8_b], axis=0)
    seg_off = jnp.concatenate([off_a, off_b], axis=0)
    n_tiles = n8.shape[0]
    tile_base = jnp.concatenate([jnp.arange(nta, dtype=I32) * rows_a, jnp.zeros((n_tiles - nta,), I32)])[:, None]
    tot = jnp.sum(n8, axis=0)
    pos0 = jnp.cumsum(n8, axis=0) - n8
    nblk = (tot + bm - 1) // bm
    cs = jnp.cumsum(nblk)
    bs = cs - nblk
    kblk = pos0 // bm
    len0 = jnp.minimum(n8, (kblk + 1) * bm - pos0)
    len1 = n8 - len0
    b0 = bs[None, :] + kblk
    src0 = tile_base + seg_off
    in_blk = pos0 - kblk * bm
    jj = jnp.arange(nblocks, dtype=I32)
    i32 = lambda v: v.astype(I32)

    def gmm_tabs(sl):
        em = lambda v: v[sl].T.reshape(-1)
        blk_em = em(b0)
        first = i32(jnp.sum(blk_em[None, :] < jj[:, None], axis=1))
        last = i32(jnp.sum(blk_em[None, :] <= jj[:, None], axis=1))
        hit = (blk_em[None, :] + 1 == jj[:, None]) & (em(len1)[None, :] > 0)
        tail_src = i32(jnp.sum(jnp.where(hit, em(src0 + len0)[None, :], 0), axis=1))
        tail_len = i32(jnp.sum(jnp.where(hit, em(len1)[None, :], 0), axis=1) // SUBLANES)
        return first, last, i32(em(src0)), i32(em(in_blk)), i32(em(len0) // SUBLANES), tail_src, tail_len

    def comb_tabs(sl):
        tm = lambda v: v[sl].reshape(-1)
        has_tail = len1[sl] > 0
        cnt = jnp.sum(has_tail, axis=1)
        lo = jnp.cumsum(cnt) - cnt
        slot_ = lo[:, None] + jnp.cumsum(has_tail, axis=1) - has_tail
        hit = (slot_.reshape(-1)[None, :] == jj[:, None]) & has_tail.reshape(-1)[None, :]
        pick = lambda v: i32(jnp.sum(jnp.where(hit, tm(v)[None, :], 0), axis=1))
        return (i32(tm(b0 * bm + in_blk)), i32(tm(seg_off)), i32(tm(len0) // SUBLANES),
                i32(lo), i32(lo + cnt), pick((b0 + 1) * bm), pick(seg_off + len0), pick(len1 // SUBLANES),
                i32(jnp.sum(n8[sl], axis=1) // SUBLANES))

    count_le = lambda v: jnp.sum(cs[None, :] <= v[:, None], axis=1)
    n_active = cs[-1]
    e_last = count_le(jnp.maximum(n_active - 1, 0).reshape(1))[0]
    block_e = jnp.minimum(count_le(jj), e_last).astype(I32)
    ee = jnp.arange(N_EXPERTS, dtype=I32)
    mine = (jj[:, None] >= bs[None, :]) & (jj[:, None] < cs[None, :])
    left = jnp.clip(tot[None, :] - (jj[:, None] - bs[None, :]) * bm, 0, bm)
    rows8 = (jnp.sum(jnp.where(mine, left, 0), axis=1) // SUBLANES).astype(I32)
    has = nblk > 0
    run = jnp.cumsum(has.astype(I32)) - 1
    later = (ee[None, :] > ee[:, None]) & has[None, :]
    nxt_e = jnp.min(jnp.where(later, ee[None, :], N_EXPERTS), axis=1)
    nxt_e = jnp.where(nxt_e == N_EXPERTS, -1, nxt_e)
    own = block_e[:, None] == ee[None, :]
    wslot = (jnp.sum(jnp.where(own, run[None, :], 0), axis=1) % 2).astype(I32)
    nxt = jnp.sum(jnp.where(own, nxt_e[None, :], 0), axis=1).astype(I32)
    a, b = slice(0, nta), slice(nta, n_tiles)
    return (block_e, rows8, wslot, nxt), gmm_tabs(a), gmm_tabs(b), comb_tabs(a), comb_tabs(b)


def _block_diag(w):
    nb, bi, bo = w.shape
    eye = jnp.eye(nb, dtype=w.dtype)
    return (eye[:, None, :, None] * w[:, :, None, :]).reshape(nb * bi, nb * bo)


def _step(x_prompt, x_sample, cache_k, cache_v, state_conv, state_h, g_mix_norm, w_in, g_q_norm, g_k_norm,
          attn_sinks, conv_w, conv_b, w_lru_a, b_lru_a, w_lru_x, b_lru_x, lru_lambda, g_attn_out, g_rnn_out,
          w_out, g_ffn_norm, w_router, b_router, w_gate_up, b_gate_up, w_down, b_down,
          *, tm, tt, tc, bm, past_len):
    B, S, D = x_prompt.shape
    NS = x_sample.shape[0]
    assert x_sample.shape[1] == 1 and D == D_MODEL
    assert (B * S) % tt == 0 and (B * S) % tm == 0 and S % tc == 0 and S % ATTN_BLOCK == 0 and S % tm == 0
    assert NS % SUBLANES == 0 and tt <= bm
    assert tt % SUBLANES == 0 and NS <= bm
    n_pt = (B * S) // tt
    total_rows = TOP_K * (B * S + NS) + (n_pt + 1) * N_EXPERTS * (SUBLANES - 1)
    nblocks = -(-total_rows // bm) + N_EXPERTS
    nbits_p = (tt // SUBLANES).bit_length()
    nbits_s = (NS // SUBLANES).bit_length()

    l = 0
    row = lambda v: v[l].reshape(1, -1)
    w_in_bf = w_in[l].astype(BF16)
    gq2 = jnp.tile(g_q_norm[l], 2).reshape(1, LANES)
    gk2 = jnp.tile(g_k_norm[l], 2).reshape(1, LANES)
    wa = _block_diag(w_lru_a[l]).astype(BF16)
    wx = _block_diag(w_lru_x[l]).astype(BF16)
    ba = b_lru_a[l].reshape(1, D_RNN)
    bx = b_lru_x[l].reshape(1, D_RNN)
    wo = w_out[l].astype(BF16)
    woa, wor = wo[:D_ATTN], wo[D_ATTN:]
    wr = w_router[l].T
    wr_hi = wr.astype(BF16)
    wrt = jnp.concatenate([wr_hi, (wr - wr_hi.astype(F32)).astype(BF16)], axis=0)
    br = b_router[l].reshape(N_EXPERTS, 1)
    low = jnp.tril(jnp.ones((N_EXPERTS, N_EXPERTS), BF16), k=-1)
    bgu = b_gate_up[l].reshape(N_EXPERTS, D_FF, 2)
    bg = bgu[:, :, 0].reshape(N_EXPERTS, 1, D_FF)
    bu = bgu[:, :, 1].reshape(N_EXPERTS, 1, D_FF)
    bd = b_down[l].reshape(N_EXPERTS, 1, D_MODEL)
    half = PERM_COLS // 2
    pr = jnp.arange(PERM_COLS)
    perm = (pr[None, :] == jnp.where(pr % 2 == 0, pr // 2, half + pr // 2)[:, None]).astype(BF16)
    sinks = attn_sinks[l]

    ctab, s1tab, s2tab = _rope_tables(jnp.arange(S))
    an, rn, kt_p, vt_p, h_last_p, xr_tail = _front(
        x_prompt.reshape(B * S, D), sinks, row(g_mix_norm), w_in_bf, gq2, gk2, ctab, s1tab, s2tab,
        row(g_attn_out), conv_w[l], row(conv_b), wa, ba, wx, bx, row(lru_lambda), row(g_rnn_out), B, S, tm)
    x2_p, ts_p, dest_p, gate_p, n8_p, off_p = _mix_route(
        x_prompt.reshape(B * S, D), an, rn, woa, wor, row(g_ffn_norm), wrt, br, low, tt)

    cs_tab = _rope_tables(jnp.full((NS,), past_len, I32))
    q_s, k_s, v_s, xr_s, yr_s = _in_proj(x_sample.reshape(NS, D), row(g_mix_norm), w_in_bf, gq2, gk2,
                                         *cs_tab, NS)
    to_rows = lambda c: jnp.transpose(c, (0, 2, 3, 1)).reshape(NS * KV_W, WINDOW)
    from_rows = lambda c, n: jnp.transpose(c.reshape(n, N_KV_HEADS, HEAD_DIM, WINDOW), (0, 3, 1, 2))[None]
    an_s, kt_s, vt_s = _attn_sample(q_s, k_s, v_s, to_rows(cache_k[l]), to_rows(cache_v[l]), sinks,
                                    row(g_attn_out), SUBLANES)
    rn_s, h_last_s, hist_s = _rnn_sample(xr_s, yr_s, jnp.transpose(state_conv[l], (1, 0, 2)), state_h[l],
                                         conv_w[l], row(conv_b), wa, ba, wx, bx, row(lru_lambda),
                                         row(g_rnn_out))
    x2_s, ts_s, dest_s, gate_s, n8_s, off_s = _mix_route(
        x_sample.reshape(NS, D), an_s, rn_s, woa, wor, row(g_ffn_norm), wrt, br, low, NS)

    blocks, gmm_p, gmm_s, comb_p, comb_s = _piece_tables(
        n8_p[:, :, 0], off_p[:, :, 0], _tile_rows(tt), n8_s[:, :, 0], off_s[:, :, 0], bm, nblocks)
    ys = _moe_gmm(blocks, gmm_p, gmm_s, ts_p, ts_s, w_gate_up[l], w_down[l], bg, bu, bd, perm,
                  nblocks, bm, nbits_p, nbits_s)
    tr = lambda a: jnp.transpose(a, (0, 2, 1))
    y_p = _combine(comb_p, ys, x2_p, tr(dest_p), tr(gate_p), tt, nbits_p)
    y_s = _combine(comb_s, ys, x2_s, tr(dest_s), tr(gate_s), NS, nbits_s)

    cp = xr_tail[:, SUBLANES - (CONV_WIDTH - 1):]
    return (y_p.reshape(B, S, D), y_s.reshape(NS, 1, D),
            from_rows(kt_p, B), from_rows(vt_p, B), cp[None], h_last_p.reshape(1, B, D_RNN),
            from_rows(kt_s, NS), from_rows(vt_s, NS), jnp.transpose(hist_s, (1, 0, 2))[None], h_last_s[None])


def kernel(x_prompt, x_sample, cache_k, cache_v, state_conv, state_h, g_mix_norm, w_in, g_q_norm, g_k_norm, attn_sinks, conv_w, conv_b, w_lru_a, b_lru_a, w_lru_x, b_lru_x, lru_lambda, g_attn_out, g_rnn_out, w_out, g_ffn_norm, w_router, b_router, w_gate_up, b_gate_up, w_down, b_down):
    return _step(x_prompt, x_sample, cache_k, cache_v, state_conv, state_h, g_mix_norm, w_in, g_q_norm,
                 g_k_norm, attn_sinks, conv_w, conv_b, w_lru_a, b_lru_a, w_lru_x, b_lru_x, lru_lambda,
                 g_attn_out, g_rnn_out, w_out, g_ffn_norm, w_router, b_router, w_gate_up, b_gate_up,
                 w_down, b_down, tm=512, tt=512, tc=256, bm=MOE_BLOCK_ROWS, past_len=PAST_LEN)
```

```python
import functools

import jax
import jax.numpy as jnp
from jax import lax
from jax.experimental import pallas as pl
from jax.experimental.pallas import tpu as pltpu

F32 = jnp.float32
BF16 = jnp.bfloat16
I32 = jnp.int32
I16 = jnp.int16

D_MODEL = 1024
HEAD_DIM = 64
N_HEADS = 8
N_KV_HEADS = 2
GROUP = 4
WINDOW = 128
ATTN_BLOCK = 128
ROT_DIM = 16
ROPE_THETA = 500000.0
D_ATTN = 512
D_RNN = 512
KV_W = 128
D_IN = 1792
CONV_WIDTH = 4
LRU_C = 8.0
N_EXPERTS = 32
TOP_K = 4
D_FF = 1024
SWIGLU_LIMIT = 7.0
SWIGLU_ALPHA = 1.702
EPS = 1e-6
PAST_LEN = 8192

LANES = 128
SUBLANES = 8
NEG_BIG = -1e30
VMEM_LIMIT = 56 * 1024 * 1024

MOE_BLOCK_ROWS = 512
PERM_COLS = 256
DISPATCH_CHUNK = 256
SAMPLE_ATTN_SEQS = 32


def _cparams(n_axes):
    return pltpu.CompilerParams(dimension_semantics=("arbitrary",) * n_axes,
                                vmem_limit_bytes=VMEM_LIMIT)


U32 = jnp.uint32
HALF = D_MODEL // 2
HI16 = 0xFFFF0000


def _pack_bf16_pairs(x):
    bits = pltpu.bitcast(x, U32)
    return (bits[:, HALF:] & U32(HI16)) | (bits[:, :HALF] >> 16)


def _unpack_bf16_pairs(u):
    lo = pltpu.bitcast(u << 16, F32)
    hi = pltpu.bitcast(u & U32(HI16), F32)
    return jnp.concatenate([lo, hi], axis=1).astype(BF16)


def _rmsnorm(x, g):
    ms = jnp.mean(x * x, axis=-1, keepdims=True)
    return (x * lax.rsqrt(ms + EPS)) * g


def _head_norm_rope(t, g, c, s1, s2, lo):
    sq = t * t
    s_lo = jnp.sum(jnp.where(lo, sq, 0.0), axis=-1, keepdims=True)
    s_hi = jnp.sum(jnp.where(lo, 0.0, sq), axis=-1, keepdims=True)
    ms = jnp.where(lo, s_lo, s_hi) * (1.0 / HEAD_DIM)
    n = (t * lax.rsqrt(ms + EPS)) * g
    up = pltpu.roll(n, LANES - ROT_DIM // 2, 1)
    dn = pltpu.roll(n, ROT_DIM // 2, 1)
    return n * c + up * s1 + dn * s2


def _in_proj_kernel(x_ref, g_ref, w_ref, gq_ref, gk_ref, c_ref, s1_ref, s2_ref,
                    q_ref, k_ref, v_ref, xr_ref, yr_ref):
    tm = x_ref.shape[0]
    h = _rmsnorm(x_ref[...], g_ref[...])
    proj = jnp.dot(h.astype(BF16), w_ref[...], preferred_element_type=F32)
    rope = (c_ref[...], s1_ref[...], s2_ref[...], lax.broadcasted_iota(I32, (tm, LANES), 1) < HEAD_DIM)
    gq = gq_ref[...]
    for j in range(D_ATTN // LANES):
        q_ref[:, j * LANES:(j + 1) * LANES] = _head_norm_rope(proj[:, j * LANES:(j + 1) * LANES], gq, *rope)
    k_ref[...] = _head_norm_rope(proj[:, D_ATTN:D_ATTN + KV_W], gk_ref[...], *rope)
    v_ref[...] = proj[:, D_ATTN + KV_W:D_ATTN + 2 * KV_W]
    o = D_ATTN + 2 * KV_W
    xr_ref[...] = proj[:, o:o + D_RNN]
    yr_ref[...] = proj[:, o + D_RNN:o + 2 * D_RNN]


def _in_proj(x2d, g, w_bf, gq2, gk2, ctab, s1tab, s2tab, tm):
    n = x2d.shape[0]
    ntab = ctab.shape[0] // tm
    row = lambda i: (i, 0)
    fix = lambda i: (0, 0)
    tab = lambda i: (i % ntab, 0)
    out_shapes = (jax.ShapeDtypeStruct((n, D_ATTN), F32), jax.ShapeDtypeStruct((n, KV_W), F32),
                  jax.ShapeDtypeStruct((n, KV_W), F32), jax.ShapeDtypeStruct((n, D_RNN), F32),
                  jax.ShapeDtypeStruct((n, D_RNN), F32))
    return pl.pallas_call(
        _in_proj_kernel,
        grid=(n // tm,),
        in_specs=[pl.BlockSpec((tm, D_MODEL), row), pl.BlockSpec((1, D_MODEL), fix),
                  pl.BlockSpec((D_MODEL, D_IN), fix), pl.BlockSpec((1, LANES), fix),
                  pl.BlockSpec((1, LANES), fix), pl.BlockSpec((tm, LANES), tab),
                  pl.BlockSpec((tm, LANES), tab), pl.BlockSpec((tm, LANES), tab)],
        out_specs=(pl.BlockSpec((tm, D_ATTN), row), pl.BlockSpec((tm, KV_W), row),
                   pl.BlockSpec((tm, KV_W), row), pl.BlockSpec((tm, D_RNN), row),
                   pl.BlockSpec((tm, D_RNN), row)),
        out_shape=out_shapes,
        compiler_params=_cparams(1),
        name="in_proj",
    )(x2d, g, w_bf, gq2, gk2, ctab, s1tab, s2tab)


def _rope_tables(pos):
    half = ROT_DIM // 2
    inv = ROPE_THETA ** (-jnp.arange(0, ROT_DIM, 2, dtype=F32) / ROT_DIM)
    ang = pos.astype(F32)[:, None] * inv[None, :]
    cos = jnp.cos(ang)
    sin = jnp.sin(ang)
    n = pos.shape[0]
    ones = jnp.ones((n, HEAD_DIM - ROT_DIM), F32)
    zeros = jnp.zeros((n, HEAD_DIM - ROT_DIM), F32)
    zh = jnp.zeros((n, half), F32)
    c = jnp.concatenate([cos, cos, ones], axis=1)
    s1 = jnp.concatenate([-sin, zh, zeros], axis=1)
    s2 = jnp.concatenate([zh, sin, zeros], axis=1)
    two = lambda t: jnp.concatenate([t, t], axis=1)
    return two(c), two(s1), two(s2)


def _band_bias(qb):
    qi = jnp.arange(qb, dtype=I32)[:, None]
    c = jnp.arange(2 * qb, dtype=I32)[None, :]
    band = (c >= qi) & (c <= qi + qb)
    first = band & (c >= qb)
    one = jnp.where(jnp.stack([first, band]), 0.0, NEG_BIG).astype(F32)
    return jnp.concatenate([one, one], axis=2)


def _attn_sample_kernel(sink_ref, q_ref, kn_ref, vn_ref, kt_ref, vt_ref, g_ref,
                        o_ref, nkt_ref, nvt_ref, acc_ref):
    bb = q_ref.shape[0]
    q = q_ref[...] * (HEAD_DIM ** -0.5)
    kn = kn_ref[...]
    vn = vn_ref[...]
    kt = kt_ref[...]
    vt = vt_ref[...]
    col = lax.broadcasted_iota(I32, (bb, bb * KV_W), 1)
    rowb = lax.broadcasted_iota(I32, (bb, bb * KV_W), 0)
    own_seq = (col >> (KV_W.bit_length() - 1)) == rowb
    half_hi = ((col >> (HEAD_DIM.bit_length() - 1)) & 1) == 1
    qbig = []
    for h in range(N_HEADS):
        kv = h // GROUP
        pair = q[:, (h // 2) * LANES:(h // 2 + 1) * LANES]
        if (h % 2) != kv:
            pair = pltpu.roll(pair, HEAD_DIM, 1)
        tiled = jnp.concatenate([pair] * bb, axis=1)
        keep = own_seq & (half_hi if kv == 1 else jnp.logical_not(half_hi))
        qbig.append(jnp.where(keep, tiled, 0.0))
    qbig = jnp.concatenate(qbig, axis=0)
    s = jnp.dot(qbig.astype(BF16), kt.astype(BF16), preferred_element_type=F32)
    qb16 = q.astype(BF16).astype(F32)
    kb16 = kn.astype(BF16).astype(F32)
    s_new, sink = [], []
    for h in range(N_HEADS):
        kv = h // GROUP
        s_new.append(jnp.sum(qb16[:, h * HEAD_DIM:(h + 1) * HEAD_DIM] * kb16[:, kv * HEAD_DIM:(kv + 1) * HEAD_DIM],
                             axis=-1, keepdims=True))
        sink.append(jnp.full((bb, 1), sink_ref[h], F32))
    s_new = jnp.concatenate(s_new, axis=0)
    sink = jnp.concatenate(sink, axis=0)
    m = jnp.maximum(jnp.maximum(jnp.max(s, axis=-1, keepdims=True), s_new), sink)
    e = jnp.exp(s - m)
    e_new = jnp.exp(s_new - m)
    inv = 1.0 / (jnp.sum(e, axis=-1, keepdims=True) + e_new + jnp.exp(sink - m))
    obig = lax.dot_general(e.astype(BF16), vt.astype(BF16), (((1,), (1,)), ((), ())),
                           preferred_element_type=F32)
    for h in range(N_HEADS):
        kv = h // GROUP
        blk = jnp.where(own_seq, obig[h * bb:(h + 1) * bb, :], 0.0)
        fold = blk[:, 0:KV_W]
        for t in range(1, bb):
            fold = fold + blk[:, t * KV_W:(t + 1) * KV_W]
        hs = slice(h * bb, (h + 1) * bb)
        ks = slice(kv * HEAD_DIM, (kv + 1) * HEAD_DIM)
        acc_ref[:, h * HEAD_DIM:(h + 1) * HEAD_DIM] = (fold[:, ks] + e_new[hs] * vn[:, ks]) * inv[hs]
    o_ref[...] = _rmsnorm(acc_ref[...], g_ref[...])

    last = lax.broadcasted_iota(I32, (KV_W, WINDOW), 1) == WINDOW - 1
    for b in range(bb):
        rs = slice(b * KV_W, (b + 1) * KV_W)
        kcol = jnp.broadcast_to(kn[b:b + 1, :], (KV_W, KV_W)).T
        vcol = jnp.broadcast_to(vn[b:b + 1, :], (KV_W, KV_W)).T
        nkt_ref[rs, :] = jnp.where(last, kcol, pltpu.roll(kt[rs, :], WINDOW - 1, 1))
        nvt_ref[rs, :] = jnp.where(last, vcol, pltpu.roll(vt[rs, :], WINDOW - 1, 1))


def _attn_sample(q, kn, vn, kt2d, vt2d, sinks, g_attn, bb):
    n = q.shape[0]
    row = lambda i: (i, 0)
    fix = lambda i: (0, 0)
    cache = pl.BlockSpec((bb * KV_W, WINDOW), row)
    return pl.pallas_call(
        _attn_sample_kernel,
        grid=(n // bb,),
        in_specs=[pl.BlockSpec(memory_space=pltpu.SMEM),
                  pl.BlockSpec((bb, D_ATTN), row), pl.BlockSpec((bb, KV_W), row),
                  pl.BlockSpec((bb, KV_W), row), cache, cache,
                  pl.BlockSpec((1, D_ATTN), fix)],
        out_specs=(pl.BlockSpec((bb, D_ATTN), row), cache, cache),
        out_shape=(jax.ShapeDtypeStruct((n, D_ATTN), F32),
                   jax.ShapeDtypeStruct(kt2d.shape, F32), jax.ShapeDtypeStruct(vt2d.shape, F32)),
        scratch_shapes=[pltpu.VMEM((bb, D_ATTN), F32)],
        compiler_params=_cparams(1),
        name="attn_sample",
    )(sinks, q, kn, vn, kt2d, vt2d, g_attn)


def _softplus(z):
    return jnp.maximum(z, 0.0) + jnp.log1p(jnp.exp(-jnp.abs(z)))


def _lru_gates(xc, wa_ref, ba_ref, wx_ref, bx_ref, lam_ref):
    xb = xc.astype(BF16)
    r = jax.nn.sigmoid(jnp.dot(xb, wa_ref[...], preferred_element_type=F32) + ba_ref[...])
    i = jax.nn.sigmoid(jnp.dot(xb, wx_ref[...], preferred_element_type=F32) + bx_ref[...])
    log_a = (-LRU_C * r) * _softplus(-lam_ref[...])
    a = jnp.exp(log_a)
    z = -jnp.tanh(log_a) * (a * a + 1.0)
    u = jnp.where(z > 0.0, z * lax.rsqrt(z), 0.0) * (i * xc)
    return a, u


def _lru_scan(a, u, h0):
    ng = a.shape[0] // SUBLANES
    a3 = a.reshape(ng, SUBLANES, D_RNN)
    u3 = u.reshape(ng, SUBLANES, D_RNN)
    t8 = lax.broadcasted_iota(I32, (ng, SUBLANES, D_RNN), 1)
    d = 1
    while d < SUBLANES:
        a_s = jnp.where(t8 >= d, pltpu.roll(a3, d, 1), 1.0)
        u_s = jnp.where(t8 >= d, pltpu.roll(u3, d, 1), 0.0)
        u3 = a3 * u_s + u3
        a3 = a3 * a_s
        d *= 2
    carry = h0
    groups = []
    for g in range(ng):
        hg = a3[g] * carry + u3[g]
        groups.append(hg)
        carry = hg[SUBLANES - 1:SUBLANES, :]
    return jnp.concatenate(groups, axis=0), carry


def _lru_scan_tiles(a_ref, u_ref, h_ref, h0):
    nl, rows, _ = a_ref.shape
    ng = rows // SUBLANES
    step = lambda ref, s: jnp.concatenate(
        [ref[j, pl.ds(s, ng, stride=SUBLANES), :] for j in range(nl)], axis=1)
    prods = [step(a_ref, 0)]
    locs = [step(u_ref, 0)]
    for s in range(1, SUBLANES):
        a_s = step(a_ref, s)
        locs.append(a_s * locs[-1] + step(u_ref, s))
        prods.append(a_s * prods[-1])
    after, h_last = _lru_scan(prods[-1], locs[-1], h0)
    row = lax.broadcasted_iota(I32, (ng, D_RNN), 0)
    before = jnp.where(row == 0, h0, pltpu.roll(after, 1, 0))
    for s in range(SUBLANES):
        h_s = locs[s] + prods[s] * before
        for j in range(nl):
            h_ref[j, pl.ds(s, ng, stride=SUBLANES), :] = h_s[:, j * LANES:(j + 1) * LANES]
    return h_last


def _to_lane_tiles(ref, x):
    for j in range(ref.shape[0]):
        ref[j] = x[:, j * LANES:(j + 1) * LANES]


def _rnn_sample_kernel(xr_ref, yr_ref, hist_ref, h0_ref, cw_ref, cb_ref, wa_ref, ba_ref, wx_ref, bx_ref,
                       lam_ref, g_ref, o_ref, hl_ref, nh_ref):
    cw = cw_ref[...]
    xr = xr_ref[...]
    xc = cb_ref[...] + xr * cw[CONV_WIDTH - 1:CONV_WIDTH, :]
    for w in range(CONV_WIDTH - 1):
        xc = xc + hist_ref[w] * cw[w:w + 1, :]
    a, u = _lru_gates(xc, wa_ref, ba_ref, wx_ref, bx_ref, lam_ref)
    h = a * h0_ref[...] + u
    hl_ref[...] = h
    o_ref[...] = _rmsnorm(jax.nn.gelu(yr_ref[...]) * h, g_ref[...])
    for w in range(CONV_WIDTH - 2):
        nh_ref[w] = hist_ref[w + 1]
    nh_ref[CONV_WIDTH - 2] = xr


def _rnn_sample(xr, yr, hist, h0, cw, cb, wa, ba, wx, bx, lam, g):
    n = xr.shape[0]
    full = lambda a: pl.BlockSpec(a.shape, lambda: (0,) * a.ndim)
    args = (xr, yr, hist, h0, cw, cb, wa, ba, wx, bx, lam, g)
    return pl.pallas_call(
        _rnn_sample_kernel,
        in_specs=[full(a) for a in args],
        out_specs=(pl.BlockSpec((n, D_RNN), lambda: (0, 0)), pl.BlockSpec((n, D_RNN), lambda: (0, 0)),
                   pl.BlockSpec(hist.shape, lambda: (0, 0, 0))),
        out_shape=(jax.ShapeDtypeStruct((n, D_RNN), F32), jax.ShapeDtypeStruct((n, D_RNN), F32),
                   jax.ShapeDtypeStruct(hist.shape, F32)),
        compiler_params=pltpu.CompilerParams(vmem_limit_bytes=VMEM_LIMIT),
        name="rnn_sample",
    )(*args)


def _front_kernel(*refs, tiles_per_seq):
    i = pl.program_id(0)
    q_s, k_s, v_s, xr_s, yr_s, ext_ref, h_ref = refs[25:32]

    @pl.when(i == 0)
    def _():
        for r in (q_s, k_s, v_s, xr_s, yr_s, ext_ref, h_ref):
            r[...] = jnp.zeros(r.shape, F32)

    for cur in range(2):
        @pl.when(i % 2 == cur)
        def _():
            _front_body(cur, 1 - cur, *refs, tiles_per_seq=tiles_per_seq)


def _front_body(cur, prv, sink_ref, x_ref, gm_ref, w_ref, gq_ref, gk_ref, c_ref, s1_ref, s2_ref, bias_ref,
                ga_ref, cw_ref, cb_ref, wa_ref, ba_ref, wx_ref, bx_ref, lam_ref, gr_ref,
                an_ref, rn_ref, kt_ref, vt_ref, hl_ref, cx_ref,
                q_s, k_s, v_s, xr_s, yr_s, ext_ref, h_ref, s_ref, e_ref, a_scr, u_scr, hs_scr,
                *, tiles_per_seq):
    i = pl.program_id(0)
    tm = x_ref.shape[0]
    qb = ATTN_BLOCK
    t = jnp.maximum(i - 1, 0)
    first_tile = (t % tiles_per_seq) == 0

    hx = _rmsnorm(x_ref[...], gm_ref[...]).astype(BF16)
    rope = (c_ref[...], s1_ref[...], s2_ref[...], lax.broadcasted_iota(I32, (tm, LANES), 1) < HEAD_DIM)
    project = lambda c0, c1: jnp.dot(hx, w_ref[:, c0:c1], preferred_element_type=F32)

    nqb = tm // qb
    n_pairs = N_HEADS // 2
    lo_k = lax.broadcasted_iota(I32, (2 * qb, LANES), 1) < HEAD_DIM
    lo_q = lax.broadcasted_iota(I32, (qb, LANES), 1) < HEAD_DIM
    nt_dims = (((1,), (1,)), ((), ()))
    vbds = []
    for jb in range(nqb):
        k2 = k_s[prv, jb * qb:(jb + 2) * qb, :]
        v2 = v_s[prv, jb * qb:(jb + 2) * qb, :]
        k2r = pltpu.roll(k2, HEAD_DIM, 1)
        v2r = pltpu.roll(v2, HEAD_DIM, 1)
        bias = bias_ref[jnp.where(first_tile, 0, 1)] if jb == 0 else bias_ref[1]
        for kv in range(N_KV_HEADS):
            ka, kb = (k2, k2r) if kv == 0 else (k2r, k2)
            va, vb = (v2, v2r) if kv == 0 else (v2r, v2)
            kbd = jnp.concatenate([jnp.where(lo_k, ka, 0.0), jnp.where(lo_k, 0.0, kb)], axis=0).astype(BF16)
            vbds.append(jnp.concatenate([jnp.where(lo_k, va, 0.0), jnp.where(lo_k, 0.0, vb)],
                                        axis=0).astype(BF16))
            for p in range(GROUP // 2):
                pp = kv * (GROUP // 2) + p
                qp = (q_s[prv, jb * qb:(jb + 1) * qb, pp * LANES:(pp + 1) * LANES]
                      * (HEAD_DIM ** -0.5)).astype(BF16)
                s_ref[jb * n_pairs + pp] = lax.dot_general(qp, kbd, nt_dims,
                                                           preferred_element_type=F32) + bias
    pq = project(0, D_ATTN)
    for j in range(D_ATTN // LANES):
        q_s[cur, :, j * LANES:(j + 1) * LANES] = _head_norm_rope(pq[:, j * LANES:(j + 1) * LANES], gq_ref[...],
                                                                 *rope)
    invs = []
    for c in range(nqb * n_pairs):
        pp = c % n_pairs
        inv = []
        for tpos in range(2):
            cols = slice(tpos * 2 * qb, (tpos + 1) * 2 * qb)
            st = s_ref[c, :, cols]
            sink = sink_ref[2 * pp + tpos]
            m = jnp.maximum(jnp.max(st, axis=-1, keepdims=True), sink)
            e = jnp.exp(st - m)
            e_ref[c, :, cols] = e.astype(BF16)
            inv.append(1.0 / (jnp.sum(e, axis=-1, keepdims=True) + jnp.exp(sink - m)))
        invs.append(jnp.where(lo_q, inv[0], inv[1]))
    pkv = project(D_ATTN, D_ATTN + 2 * KV_W)
    k_s[cur, qb:qb + tm, :] = _head_norm_rope(pkv[:, :KV_W], gk_ref[...], *rope)
    v_s[cur, qb:qb + tm, :] = pkv[:, KV_W:]
    k_s[cur, 0:qb, :] = k_s[prv, tm:tm + qb, :]
    v_s[cur, 0:qb, :] = v_s[prv, tm:tm + qb, :]
    for jb in range(nqb):
        outs = [jnp.dot(e_ref[jb * n_pairs + pp], vbds[jb * N_KV_HEADS + pp // (GROUP // 2)],
                        preferred_element_type=F32) * invs[jb * n_pairs + pp] for pp in range(n_pairs)]
        an_ref[jb * qb:(jb + 1) * qb, :] = _rmsnorm(jnp.concatenate(outs, axis=1), ga_ref[...])
    kt_ref[0] = k_s[prv, tm:tm + qb, :].T
    vt_ref[0] = v_s[prv, tm:tm + qb, :].T

    o = D_ATTN + 2 * KV_W
    xr_s[cur] = project(o, o + D_RNN)

    pad = SUBLANES
    xr = xr_s[prv]
    ext_ref[0:pad, :] = jnp.where(first_tile, 0.0, ext_ref[0:pad, :])
    ext_ref[pad:pad + tm, :] = xr
    cw = cw_ref[...]
    xc = cb_ref[...] + xr * cw[CONV_WIDTH - 1:CONV_WIDTH, :]
    for w in range(CONV_WIDTH - 1):
        sh = CONV_WIDTH - 1 - w
        xc = xc + ext_ref[pad - sh:pad - sh + tm, :] * cw[w:w + 1, :]
    ext_ref[0:pad, :] = xr[tm - pad:tm, :]
    cx_ref[0] = xr[tm - pad:tm, :]
    a, u = _lru_gates(xc, wa_ref, ba_ref, wx_ref, bx_ref, lam_ref)
    yr_s[cur] = project(o + D_RNN, o + 2 * D_RNN)
    _to_lane_tiles(a_scr, a)
    _to_lane_tiles(u_scr, u)
    carry = _lru_scan_tiles(a_scr, u_scr, hs_scr, jnp.where(first_tile, 0.0, h_ref[...]))
    h_ref[...] = carry
    hl_ref[0] = carry
    hseq = jnp.concatenate([hs_scr[j] for j in range(hs_scr.shape[0])], axis=1)
    rn_ref[...] = _rmsnorm(jax.nn.gelu(yr_s[prv]) * hseq, gr_ref[...])


def _front(x2d, sinks, g_mix, w_bf, gq2, gk2, ctab, s1tab, s2tab, g_attn,
           cw, cb, wa, ba, wx, bx, lam, g_rnn, batch, seq, tm):
    n = x2d.shape[0]
    nt = n // tm
    tps = seq // tm
    qb = ATTN_BLOCK
    cur = lambda i: (jnp.minimum(i, nt - 1), 0)
    tab = lambda i: (jnp.minimum(i, nt - 1) % tps, 0)
    fix = lambda i: (0, 0)
    prev = lambda i: (jnp.maximum(i - 1, 0), 0)
    per_seq = lambda i: (jnp.maximum(i - 1, 0) // tps, 0, 0)
    vec = lambda w: pl.BlockSpec((1, w), fix)
    return pl.pallas_call(
        functools.partial(_front_kernel, tiles_per_seq=tps),
        grid=(nt + 1,),
        in_specs=[pl.BlockSpec(memory_space=pltpu.SMEM),
                  pl.BlockSpec((tm, D_MODEL), cur), vec(D_MODEL), pl.BlockSpec((D_MODEL, D_IN), fix),
                  vec(LANES), vec(LANES),
                  pl.BlockSpec((tm, LANES), tab), pl.BlockSpec((tm, LANES), tab), pl.BlockSpec((tm, LANES), tab),
                  pl.BlockSpec((2, qb, 4 * qb), lambda i: (0, 0, 0)), vec(D_ATTN),
                  pl.BlockSpec((CONV_WIDTH, D_RNN), fix), vec(D_RNN),
                  pl.BlockSpec((D_RNN, D_RNN), fix), vec(D_RNN),
                  pl.BlockSpec((D_RNN, D_RNN), fix), vec(D_RNN), vec(D_RNN), vec(D_RNN)],
        out_specs=(pl.BlockSpec((tm, D_ATTN), prev), pl.BlockSpec((tm, D_RNN), prev),
                   pl.BlockSpec((1, KV_W, qb), per_seq), pl.BlockSpec((1, KV_W, qb), per_seq),
                   pl.BlockSpec((1, 1, D_RNN), per_seq), pl.BlockSpec((1, SUBLANES, D_RNN), per_seq)),
        out_shape=(jax.ShapeDtypeStruct((n, D_ATTN), F32), jax.ShapeDtypeStruct((n, D_RNN), F32),
                   jax.ShapeDtypeStruct((batch, KV_W, qb), F32), jax.ShapeDtypeStruct((batch, KV_W, qb), F32),
                   jax.ShapeDtypeStruct((batch, 1, D_RNN), F32),
                   jax.ShapeDtypeStruct((batch, SUBLANES, D_RNN), F32)),
        scratch_shapes=[pltpu.VMEM((2, tm, D_ATTN), F32),
                        pltpu.VMEM((2, tm + qb, KV_W), F32), pltpu.VMEM((2, tm + qb, KV_W), F32),
                        pltpu.VMEM((2, tm, D_RNN), F32), pltpu.VMEM((2, tm, D_RNN), F32),
                        pltpu.VMEM((tm + SUBLANES, D_RNN), F32), pltpu.VMEM((1, D_RNN), F32),
                        pltpu.VMEM((tm // qb * (N_HEADS // 2), qb, 4 * qb), F32),
                        pltpu.VMEM((tm // qb * (N_HEADS // 2), qb, 4 * qb), BF16),
                        pltpu.VMEM((D_RNN // LANES, tm, LANES), F32), pltpu.VMEM((D_RNN // LANES, tm, LANES), F32),
                        pltpu.VMEM((D_RNN // LANES, tm, LANES), F32)],
        compiler_params=_cparams(1),
        name="front",
    )(sinks, x2d, g_mix, w_bf, gq2, gk2, ctab, s1tab, s2tab, _band_bias(qb), g_attn,
      cw, cb, wa, ba, wx, bx, lam, g_rnn)


def _mix_route_kernel(*refs):
    i = pl.program_id(0)
    hb_s, d_s = refs[16:18]

    @pl.when(i == 0)
    def _():
        hb_s[...] = jnp.zeros(hb_s.shape, BF16)
        d_s[...] = jnp.zeros(d_s.shape, I32)

    for cur in range(2):
        @pl.when(i % 2 == cur)
        def _():
            _mix_route_body(cur, 1 - cur, *refs)


def _mix_route_body(cur, prv, x_ref, an_ref, rn_ref, woa_ref, wor_ref, g_ref, wr2_ref, br_ref, tri_ref, low_ref,
                    x2_ref, ts_ref, dest_ref, gate_ref, n8_ref, off_ref, hb_s, d_s):
    tt = x_ref.shape[0]
    tile_rows = ts_ref.shape[0]
    n_chunks = tile_rows // DISPATCH_CHUNK

    hb_prev = hb_s[prv]
    d16 = [d_s[prv, k:k + 1, :].astype(I16) for k in range(TOP_K)]
    ri = lax.broadcasted_iota(I32, (DISPATCH_CHUNK, tt), 0).astype(I16)
    one = jnp.ones((DISPATCH_CHUNK, tt), BF16)

    def dispatch(chunks):
        for c in chunks:
            p = jnp.zeros((DISPATCH_CHUNK, tt), BF16)
            for d in d16:
                p = jnp.where(ri == d - jnp.int16(c * DISPATCH_CHUNK), one, p)
            ts_ref[c * DISPATCH_CHUNK:(c + 1) * DISPATCH_CHUNK, :] = _pack_bf16_pairs(
                jnp.dot(p, hb_prev, preferred_element_type=F32))

    third = -(-n_chunks // 3)

    x2 = x_ref[...] + jnp.dot(an_ref[...].astype(BF16), woa_ref[...], preferred_element_type=F32) \
        + jnp.dot(rn_ref[...].astype(BF16), wor_ref[...], preferred_element_type=F32)
    x2_ref[...] = x2
    dispatch(range(0, third))
    hn = _rmsnorm(x2, g_ref[...])

    nt = (((1,), (1,)), ((), ()))
    hb = hn.astype(BF16)
    hb_s[cur] = hb
    hmid = (hn - hb.astype(F32)).astype(BF16)
    wr2 = wr2_ref[...]
    both = lax.dot_general(wr2, hb, nt, preferred_element_type=F32)
    logits = (lax.dot_general(wr2[:N_EXPERTS], hmid, nt, preferred_element_type=F32)
              + both[N_EXPERTS:]) + both[:N_EXPERTS] + br_ref[...]
    dispatch(range(third, 2 * third))

    ie = lax.broadcasted_iota(I32, (N_EXPERTS, tt), 0).astype(F32)
    l = logits
    vals, sels = [], []
    for _ in range(TOP_K):
        m = jnp.max(l, axis=0, keepdims=True)
        idx = jnp.min(jnp.where(l == m, ie, float(N_EXPERTS)), axis=0, keepdims=True)
        sel = ie == idx
        vals.append(m)
        sels.append(sel)
        l = jnp.where(sel, NEG_BIG, l)
    es = [jnp.exp(v - vals[0]) for v in vals]
    den = es[0] + es[1] + es[2] + es[3]
    gate_ref[0] = jnp.concatenate([e / den for e in es], axis=0)
    dispatch(range(2 * third, n_chunks))

    oh = jnp.zeros((N_EXPERTS, tt), F32)
    for sel in sels:
        oh = oh + jnp.where(sel, 1.0, 0.0)
    before = jnp.dot(oh.astype(BF16), tri_ref[...], preferred_element_type=F32)
    cnt = jnp.sum(oh, axis=1, keepdims=True).astype(I32)
    n8 = ((cnt + (SUBLANES - 1)) >> 3) << 3
    n8b = jnp.broadcast_to(n8, (N_EXPERTS, LANES))
    off = jnp.dot(low_ref[...], n8b.astype(F32).astype(BF16), preferred_element_type=F32)
    n8_ref[0] = n8b
    off_ref[0] = off.astype(I32)
    base = off[:, 0:1] + before
    dests = jnp.concatenate(
        [jnp.sum(jnp.where(sel, base, 0.0), axis=0, keepdims=True).astype(I32) for sel in sels], axis=0)
    dest_ref[0] = dests
    d_s[cur, 0:TOP_K, :] = dests


def _tile_rows(tt):
    return -(-(TOP_K * tt + N_EXPERTS * (SUBLANES - 1)) // DISPATCH_CHUNK) * DISPATCH_CHUNK


def _mix_route(x2d, an, rn, woa, wor, g, wrt, br, low, tt):
    n = x2d.shape[0]
    nt = n // tt
    tile_rows = _tile_rows(tt)
    tri = jnp.triu(jnp.ones((tt, tt), BF16), k=1)
    row = lambda i: (jnp.minimum(i, nt - 1), 0)
    prev = lambda i: (jnp.maximum(i - 1, 0), 0)
    fix = lambda i: (0, 0)
    t3 = lambda i: (jnp.minimum(i, nt - 1), 0, 0)
    in_specs = [pl.BlockSpec((tt, D_MODEL), row), pl.BlockSpec((tt, D_ATTN), row),
                pl.BlockSpec((tt, D_RNN), row), pl.BlockSpec((D_ATTN, D_MODEL), fix),
                pl.BlockSpec((D_RNN, D_MODEL), fix), pl.BlockSpec((1, D_MODEL), fix),
                pl.BlockSpec((2 * N_EXPERTS, D_MODEL), fix), pl.BlockSpec((N_EXPERTS, 1), fix),
                pl.BlockSpec((tt, tt), fix), pl.BlockSpec((N_EXPERTS, N_EXPERTS), fix)]
    out_shape = (jax.ShapeDtypeStruct((n, D_MODEL), F32),
                 jax.ShapeDtypeStruct((nt * tile_rows, HALF), U32),
                 jax.ShapeDtypeStruct((nt, TOP_K, tt), I32),
                 jax.ShapeDtypeStruct((nt, TOP_K, tt), F32),
                 jax.ShapeDtypeStruct((nt, N_EXPERTS, LANES), I32),
                 jax.ShapeDtypeStruct((nt, N_EXPERTS, LANES), I32))
    out_specs = (pl.BlockSpec((tt, D_MODEL), row),
                 pl.BlockSpec((tile_rows, HALF), prev),
                 pl.BlockSpec((1, TOP_K, tt), t3), pl.BlockSpec((1, TOP_K, tt), t3),
                 pl.BlockSpec((1, N_EXPERTS, LANES), t3), pl.BlockSpec((1, N_EXPERTS, LANES), t3))
    return pl.pallas_call(
        _mix_route_kernel,
        grid=(nt + 1,),
        in_specs=in_specs,
        out_specs=out_specs,
        out_shape=out_shape,
        scratch_shapes=[pltpu.VMEM((2, tt, D_MODEL), BF16), pltpu.VMEM((2, SUBLANES, tt), I32)],
        compiler_params=_cparams(1),
        name="mix_route",
    )(x2d, an, rn, woa, wor, g, wrt, br, tri, low)


LOW_BITS = 4


def _start_piece(src_hbm, dst_buf, sem, s, d, l8, nbits):
    def bit_copy(c):
        size = SUBLANES << c
        low = (l8 & ((1 << c) - 1)) * SUBLANES

        @pl.when(((l8 >> c) & 1) == 1)
        def _():
            pltpu.make_async_copy(
                src_hbm.at[pl.ds(pl.multiple_of(s + low, SUBLANES), size)],
                dst_buf.at[pl.ds(pl.multiple_of(d + low, SUBLANES), size)], sem).start()

    for c in range(min(LOW_BITS, nbits)):
        bit_copy(c)
    if nbits > LOW_BITS:
        def long_copies(_, carry):
            for c in range(LOW_BITS, nbits):
                bit_copy(c)
            return carry

        lax.fori_loop(0, jnp.where(l8 >= (1 << LOW_BITS), 1, 0), long_copies, 0)


def _start_pieces(src_hbm, dst_buf, sem, p_lo, p_hi, psrc_ref, pdst_ref, plen_ref, nbits):
    def body(p, carry):
        _start_piece(src_hbm, dst_buf, sem, psrc_ref[p], pdst_ref[p], plen_ref[p], nbits)
        return carry

    lax.fori_loop(p_lo, p_hi, body, 0)


def _wait_rows(src_hbm, dst_buf, sem, rows8, nbits):
    for c in range(nbits):
        size = SUBLANES << c

        @pl.when(((rows8 >> c) & 1) == 1)
        def _():
            pltpu.make_async_copy(src_hbm.at[pl.ds(0, size)], dst_buf.at[pl.ds(0, size)], sem).wait()


def _moe_gmm_kernel(be_ref, rows_ref, wslot_ref, nxt_ref,
                    psa_ref, pea_ref, srca_ref, dsta_ref, lena_ref, hsrca_ref, hlena_ref,
                    psb_ref, peb_ref, srcb_ref, dstb_ref, lenb_ref, hsrcb_ref, hlenb_ref,
                    tsa_hbm, tsb_hbm, wgu_hbm, wdn_hbm, bg_ref, bu_ref, bd_ref, perm_ref,
                    ys_ref, lhs_ref, wgu_buf, wdn_buf, wg_ref, wu_ref, wd_ref, sem_ref, wsem_ref,
                    *, nbits_a, nbits_b):
    j = pl.program_id(0)
    nb = pl.num_programs(0)
    slot = j % 2
    bm = lhs_ref.shape[1]

    def gather(blk, sl):
        for ts_hbm, ps, pe, src, dst, ln, hsrc, hlen, nbits in (
                (tsa_hbm, psa_ref, pea_ref, srca_ref, dsta_ref, lena_ref, hsrca_ref, hlena_ref, nbits_a),
                (tsb_hbm, psb_ref, peb_ref, srcb_ref, dstb_ref, lenb_ref, hsrcb_ref, hlenb_ref, nbits_b)):
            _start_pieces(ts_hbm, lhs_ref.at[sl], sem_ref.at[sl], ps[blk], pe[blk], src, dst, ln, nbits)
            _start_piece(ts_hbm, lhs_ref.at[sl], sem_ref.at[sl], hsrc[blk], 0, hlen[blk], nbits)

    def weight_copies(e, ws):
        return (pltpu.make_async_copy(wgu_hbm.at[e], wgu_buf.at[ws], wsem_ref.at[ws]),
                pltpu.make_async_copy(wdn_hbm.at[e], wdn_buf.at[ws], wsem_ref.at[ws]))

    @pl.when(j == 0)
    def _():
        lhs_ref[...] = jnp.zeros(lhs_ref.shape, U32)
        gather(0, 0)
        for cp in weight_copies(be_ref[0], wslot_ref[0]):
            cp.start()

    @pl.when(j + 1 < nb)
    def _():
        gather(j + 1, 1 - slot)

    @pl.when(jnp.logical_or(j == 0, be_ref[j] != be_ref[jnp.maximum(j - 1, 0)]))
    def _():
        ws = wslot_ref[j]
        for cp in weight_copies(be_ref[j], ws):
            cp.wait()
        nxt = nxt_ref[j]

        @pl.when(nxt >= 0)
        def _():
            for cp in weight_copies(nxt, 1 - ws):
                cp.start()

        perm = perm_ref[...]
        half = PERM_COLS // 2
        for c in range(2 * D_FF // PERM_COLS):
            wb = wgu_buf[ws, :, c * PERM_COLS:(c + 1) * PERM_COLS].astype(BF16)
            wp = jnp.dot(wb, perm, preferred_element_type=F32).astype(BF16)
            wg_ref[:, c * half:(c + 1) * half] = wp[:, :half]
            wu_ref[:, c * half:(c + 1) * half] = wp[:, half:]
        wd_ref[...] = wdn_buf[ws].astype(BF16)

    rows8 = rows_ref[j]
    _wait_rows(tsa_hbm, lhs_ref.at[slot], sem_ref.at[slot], rows8, (bm // SUBLANES).bit_length())

    def expert_rows(rs):
        x = _unpack_bf16_pairs(lhs_ref[slot, rs, :])
        gate = jnp.dot(x, wg_ref[...], preferred_element_type=F32) + bg_ref[0]
        up = jnp.dot(x, wu_ref[...], preferred_element_type=F32) + bu_ref[0]
        gate = jnp.minimum(gate, SWIGLU_LIMIT)
        up = jnp.clip(up, -SWIGLU_LIMIT, SWIGLU_LIMIT)
        act = (up + 1.0) * (gate * jax.nn.sigmoid(SWIGLU_ALPHA * gate))
        y = jnp.dot(act.astype(BF16), wd_ref[...], preferred_element_type=F32) + bd_ref[0]
        ys_ref[rs, :] = _pack_bf16_pairs(y.astype(BF16).astype(F32))

    half8 = bm // 2 // SUBLANES

    @pl.when(rows8 > half8)
    def _():
        expert_rows(slice(0, bm))

    @pl.when(jnp.logical_and(rows8 > 0, rows8 <= half8))
    def _():
        expert_rows(slice(0, bm // 2))
        ys_ref[bm // 2:, :] = jnp.zeros((bm // 2, HALF), U32)

    @pl.when(rows8 == 0)
    def _():
        ys_ref[...] = jnp.zeros(ys_ref.shape, U32)


def _moe_gmm(blocks, tabs_a, tabs_b, ts_a, ts_b, w_gu, w_dn, bg, bu, bd, perm, nblocks, bm, nbits_a, nbits_b):
    we = lambda j, be, *_: (be[j], 0, 0)
    grid_spec = pltpu.PrefetchScalarGridSpec(
        num_scalar_prefetch=18,
        grid=(nblocks,),
        in_specs=[pl.BlockSpec(memory_space=pl.ANY), pl.BlockSpec(memory_space=pl.ANY),
                  pl.BlockSpec(memory_space=pl.ANY), pl.BlockSpec(memory_space=pl.ANY),
                  pl.BlockSpec((1, 1, D_FF), we), pl.BlockSpec((1, 1, D_FF), we),
                  pl.BlockSpec((1, 1, D_MODEL), we),
                  pl.BlockSpec((PERM_COLS, PERM_COLS), lambda j, *_: (0, 0))],
        out_specs=pl.BlockSpec((bm, HALF), lambda j, *_: (j, 0)),
        scratch_shapes=[pltpu.VMEM((2, bm, HALF), U32),
                        pltpu.VMEM((2, D_MODEL, 2 * D_FF), F32), pltpu.VMEM((2, D_FF, D_MODEL), F32),
                        pltpu.VMEM((D_MODEL, D_FF), BF16), pltpu.VMEM((D_MODEL, D_FF), BF16),
                        pltpu.VMEM((D_FF, D_MODEL), BF16),
                        pltpu.SemaphoreType.DMA((2,)), pltpu.SemaphoreType.DMA((2,))],
    )
    return pl.pallas_call(
        functools.partial(_moe_gmm_kernel, nbits_a=nbits_a, nbits_b=nbits_b),
        grid_spec=grid_spec,
        out_shape=jax.ShapeDtypeStruct((nblocks * bm, HALF), U32),
        compiler_params=_cparams(1),
        name="moe_gmm",
    )(*blocks, *tabs_a, *tabs_b, ts_a, ts_b, w_gu, w_dn, bg, bu, bd, perm)


def _combine_kernel(psrc_ref, pdst_ref, plen_ref, tlo_ref, thi_ref, tsrc_ref, tdst_ref, tlen_ref, rows_ref,
                    ys_hbm, x2_ref, dest_ref, gate_ref, o_ref,
                    buf_ref, db_ref, gb_ref, sem_ref, *, nbits):
    i = pl.program_id(0)
    n = pl.num_programs(0)
    slot = i % 2
    tt = x2_ref.shape[0]
    tile_rows = buf_ref.shape[1]

    def gather(tile, sl):
        _start_pieces(ys_hbm, buf_ref.at[sl], sem_ref.at[sl], tile * N_EXPERTS, (tile + 1) * N_EXPERTS,
                      psrc_ref, pdst_ref, plen_ref, nbits)
        _start_pieces(ys_hbm, buf_ref.at[sl], sem_ref.at[sl], tlo_ref[tile], thi_ref[tile],
                      tsrc_ref, tdst_ref, tlen_ref, nbits)

    @pl.when(i == 0)
    def _():
        buf_ref[...] = jnp.zeros(buf_ref.shape, U32)
        gather(0, 0)

    @pl.when(i + 1 < n)
    def _():
        gather(i + 1, 1 - slot)

    _wait_rows(ys_hbm, buf_ref.at[slot], sem_ref.at[slot], rows_ref[i], (tile_rows // SUBLANES).bit_length())

    dest = dest_ref[0]
    gate = gate_ref[0]
    for k in range(TOP_K):
        db_ref[k] = jnp.broadcast_to(dest[:, k:k + 1], (tt, DISPATCH_CHUNK)).astype(I16)
        gb_ref[k] = jnp.broadcast_to(gate[:, k:k + 1], (tt, DISPATCH_CHUNK)).astype(BF16)
    li = lax.broadcasted_iota(I32, (tt, DISPATCH_CHUNK), 1).astype(I16)
    gms = []
    for c in range(tile_rows // DISPATCH_CHUNK):
        lic = li + jnp.int16(c * DISPATCH_CHUNK)
        gm = jnp.zeros((tt, DISPATCH_CHUNK), BF16)
        for k in range(TOP_K):
            gm = jnp.where(lic == db_ref[k], gb_ref[k], gm)
        gms.append(gm)
    o_ref[...] = x2_ref[...] + jnp.dot(jnp.concatenate(gms, axis=1), _unpack_bf16_pairs(buf_ref[slot]),
                                       preferred_element_type=F32)


def _combine(tabs, ys, x2, dest, gate, tt, nbits):
    n = x2.shape[0]
    nt = n // tt
    tile_rows = _tile_rows(tt)
    grid_spec = pltpu.PrefetchScalarGridSpec(
        num_scalar_prefetch=len(tabs),
        grid=(nt,),
        in_specs=[pl.BlockSpec(memory_space=pl.ANY),
                  pl.BlockSpec((tt, D_MODEL), lambda i, *_: (i, 0)),
                  pl.BlockSpec((1, tt, TOP_K), lambda i, *_: (i, 0, 0)),
                  pl.BlockSpec((1, tt, TOP_K), lambda i, *_: (i, 0, 0))],
        out_specs=pl.BlockSpec((tt, D_MODEL), lambda i, *_: (i, 0)),
        scratch_shapes=[pltpu.VMEM((2, tile_rows, HALF), U32),
                        pltpu.VMEM((TOP_K, tt, DISPATCH_CHUNK), I16), pltpu.VMEM((TOP_K, tt, DISPATCH_CHUNK), BF16),
                        pltpu.SemaphoreType.DMA((2,))],
    )
    return pl.pallas_call(
        functools.partial(_combine_kernel, nbits=nbits),
        grid_spec=grid_spec,
        out_shape=jax.ShapeDtypeStruct((n, D_MODEL), F32),
        compiler_params=_cparams(1),
        name="combine",
    )(*tabs, ys, x2, dest, gate)


def _piece_tables(n8_a, off_a, rows_a, n8_b, off_b, bm, nblocks):
    nta = n8_a.shape[0]
    n8 = jnp.concatenate([n8_a, n8_b], axis=0)
    seg_off = jnp.concatenate([off_a, off_b], axis=0)
    n_tiles = n8.shape[0]
    tile_base = jnp.concatenate([jnp.arange(nta, dtype=I32) * rows_a, jnp.zeros((n_tiles - nta,), I32)])[:, None]
    tot = jnp.sum(n8, axis=0)
    pos0 = jnp.cumsum(n8, axis=0) - n8
    nblk = (tot + bm - 1) // bm
    cs = jnp.cumsum(nblk)
    bs = cs - nblk
    kblk = pos0 // bm
    len0 = jnp.minimum(n8, (kblk + 1) * bm - pos0)
    len1 = n8 - len0
    b0 = bs[None, :] + kblk
    src0 = tile_base + seg_off
    in_blk = pos0 - kblk * bm
    jj = jnp.arange(nblocks, dtype=I32)
    i32 = lambda v: v.astype(I32)

    def gmm_tabs(sl):
        em = lambda v: v[sl].T.reshape(-1)
        blk_em = em(b0)
        first = i32(jnp.sum(blk_em[None, :] < jj[:, None], axis=1))
        last = i32(jnp.sum(blk_em[None, :] <= jj[:, None], axis=1))
        hit = (blk_em[None, :] + 1 == jj[:, None]) & (em(len1)[None, :] > 0)
        tail_src = i32(jnp.sum(jnp.where(hit, em(src0 + len0)[None, :], 0), axis=1))
        tail_len = i32(jnp.sum(jnp.where(hit, em(len1)[None, :], 0), axis=1) // SUBLANES)
        return first, last, i32(em(src0)), i32(em(in_blk)), i32(em(len0) // SUBLANES), tail_src, tail_len

    def comb_tabs(sl):
        tm = lambda v: v[sl].reshape(-1)
        has_tail = len1[sl] > 0
        cnt = jnp.sum(has_tail, axis=1)
        lo = jnp.cumsum(cnt) - cnt
        slot_ = lo[:, None] + jnp.cumsum(has_tail, axis=1) - has_tail
        hit = (slot_.reshape(-1)[None, :] == jj[:, None]) & has_tail.reshape(-1)[None, :]
        pick = lambda v: i32(jnp.sum(jnp.where(hit, tm(v)[None, :], 0), axis=1))
        return (i32(tm(b0 * bm + in_blk)), i32(tm(seg_off)), i32(tm(len0) // SUBLANES),
                i32(lo), i32(lo + cnt), pick((b0 + 1) * bm), pick(seg_off + len0), pick(len1 // SUBLANES),
                i32(jnp.sum(n8[sl], axis=1) // SUBLANES))

    count_le = lambda v: jnp.sum(cs[None, :] <= v[:, None], axis=1)
    n_active = cs[-1]
    e_last = count_le(jnp.maximum(n_active - 1, 0).reshape(1))[0]
    block_e = jnp.minimum(count_le(jj), e_last).astype(I32)
    ee = jnp.arange(N_EXPERTS, dtype=I32)
    mine = (jj[:, None] >= bs[None, :]) & (jj[:, None] < cs[None, :])
    left = jnp.clip(tot[None, :] - (jj[:, None] - bs[None, :]) * bm, 0, bm)
    rows8 = (jnp.sum(jnp.where(mine, left, 0), axis=1) // SUBLANES).astype(I32)
    has = nblk > 0
    run = jnp.cumsum(has.astype(I32)) - 1
    later = (ee[None, :] > ee[:, None]) & has[None, :]
    nxt_e = jnp.min(jnp.where(later, ee[None, :], N_EXPERTS), axis=1)
    nxt_e = jnp.where(nxt_e == N_EXPERTS, -1, nxt_e)
    own = block_e[:, None] == ee[None, :]
    wslot = (jnp.sum(jnp.where(own, run[None, :], 0), axis=1) % 2).astype(I32)
    nxt = jnp.sum(jnp.where(own, nxt_e[None, :], 0), axis=1).astype(I32)
    a, b = slice(0, nta), slice(nta, n_tiles)
    return (block_e, rows8, wslot, nxt), gmm_tabs(a), gmm_tabs(b), comb_tabs(a), comb_tabs(b)


def _block_diag(w):
    nb, bi, bo = w.shape
    eye = jnp.eye(nb, dtype=w.dtype)
    return (eye[:, None, :, None] * w[:, :, None, :]).reshape(nb * bi, nb * bo)


def _step(x_prompt, x_sample, cache_k, cache_v, state_conv, state_h, g_mix_norm, w_in, g_q_norm, g_k_norm,
          attn_sinks, conv_w, conv_b, w_lru_a, b_lru_a, w_lru_x, b_lru_x, lru_lambda, g_attn_out, g_rnn_out,
          w_out, g_ffn_norm, w_router, b_router, w_gate_up, b_gate_up, w_down, b_down,
          *, tm, tt, bm, past_len):
    B, S, D = x_prompt.shape
    NS = x_sample.shape[0]
    assert x_sample.shape[1] == 1 and D == D_MODEL
    assert (B * S) % tt == 0 and S % tm == 0 and tm % ATTN_BLOCK == 0
    assert NS % SUBLANES == 0 and tt <= bm
    assert tt % SUBLANES == 0 and NS <= bm
    n_pt = (B * S) // tt
    total_rows = TOP_K * (B * S + NS) + (n_pt + 1) * N_EXPERTS * (SUBLANES - 1)
    nblocks = -(-total_rows // bm) + N_EXPERTS
    nbits_p = (tt // SUBLANES).bit_length()
    nbits_s = (NS // SUBLANES).bit_length()

    l = 0
    row = lambda v: v[l].reshape(1, -1)
    w_in_bf = w_in[l].astype(BF16)
    gq2 = jnp.tile(g_q_norm[l], 2).reshape(1, LANES)
    gk2 = jnp.tile(g_k_norm[l], 2).reshape(1, LANES)
    wa = _block_diag(w_lru_a[l]).astype(BF16)
    wx = _block_diag(w_lru_x[l]).astype(BF16)
    ba = b_lru_a[l].reshape(1, D_RNN)
    bx = b_lru_x[l].reshape(1, D_RNN)
    wo = w_out[l].astype(BF16)
    woa, wor = wo[:D_ATTN], wo[D_ATTN:]
    wr = w_router[l].T
    wr_hi = wr.astype(BF16)
    wrt = jnp.concatenate([wr_hi, (wr - wr_hi.astype(F32)).astype(BF16)], axis=0)
    br = b_router[l].reshape(N_EXPERTS, 1)
    low = jnp.tril(jnp.ones((N_EXPERTS, N_EXPERTS), BF16), k=-1)
    bgu = b_gate_up[l].reshape(N_EXPERTS, D_FF, 2)
    bg = bgu[:, :, 0].reshape(N_EXPERTS, 1, D_FF)
    bu = bgu[:, :, 1].reshape(N_EXPERTS, 1, D_FF)
    bd = b_down[l].reshape(N_EXPERTS, 1, D_MODEL)
    half = PERM_COLS // 2
    pr = jnp.arange(PERM_COLS)
    perm = (pr[None, :] == jnp.where(pr % 2 == 0, pr // 2, half + pr // 2)[:, None]).astype(BF16)
    sinks = attn_sinks[l]

    ctab, s1tab, s2tab = _rope_tables(jnp.arange(S))
    an, rn, kt_p, vt_p, h_last_p, xr_tail = _front(
        x_prompt.reshape(B * S, D), sinks, row(g_mix_norm), w_in_bf, gq2, gk2, ctab, s1tab, s2tab,
        row(g_attn_out), conv_w[l], row(conv_b), wa, ba, wx, bx, row(lru_lambda), row(g_rnn_out), B, S, tm)
    x2_p, ts_p, dest_p, gate_p, n8_p, off_p = _mix_route(
        x_prompt.reshape(B * S, D), an, rn, woa, wor, row(g_ffn_norm), wrt, br, low, tt)

    cs_tab = _rope_tables(jnp.full((NS,), past_len, I32))
    q_s, k_s, v_s, xr_s, yr_s = _in_proj(x_sample.reshape(NS, D), row(g_mix_norm), w_in_bf, gq2, gk2,
                                         *cs_tab, NS)
    to_rows = lambda c: jnp.transpose(c, (0, 2, 3, 1)).reshape(NS * KV_W, WINDOW)
    from_rows = lambda c, n: jnp.transpose(c.reshape(n, N_KV_HEADS, HEAD_DIM, WINDOW), (0, 3, 1, 2))[None]
    an_s, kt_s, vt_s = _attn_sample(q_s, k_s, v_s, to_rows(cache_k[l]), to_rows(cache_v[l]), sinks,
                                    row(g_attn_out), min(NS, SAMPLE_ATTN_SEQS))
    rn_s, h_last_s, hist_s = _rnn_sample(xr_s, yr_s, jnp.transpose(state_conv[l], (1, 0, 2)), state_h[l],
                                         conv_w[l], row(conv_b), wa, ba, wx, bx, row(lru_lambda),
                                         row(g_rnn_out))
    x2_s, ts_s, dest_s, gate_s, n8_s, off_s = _mix_route(
        x_sample.reshape(NS, D), an_s, rn_s, woa, wor, row(g_ffn_norm), wrt, br, low, NS)

    blocks, gmm_p, gmm_s, comb_p, comb_s = _piece_tables(
        n8_p[:, :, 0], off_p[:, :, 0], _tile_rows(tt), n8_s[:, :, 0], off_s[:, :, 0], bm, nblocks)
    ys = _moe_gmm(blocks, gmm_p, gmm_s, ts_p, ts_s, w_gate_up[l], w_down[l], bg, bu, bd, perm,
                  nblocks, bm, nbits_p, nbits_s)
    tr = lambda a: jnp.transpose(a, (0, 2, 1))
    y_p = _combine(comb_p, ys, x2_p, tr(dest_p), tr(gate_p), tt, nbits_p)
    y_s = _combine(comb_s, ys, x2_s, tr(dest_s), tr(gate_s), NS, nbits_s)

    cp = xr_tail[:, SUBLANES - (CONV_WIDTH - 1):]
    return (y_p.reshape(B, S, D), y_s.reshape(NS, 1, D),
            from_rows(kt_p, B), from_rows(vt_p, B), cp[None], h_last_p.reshape(1, B, D_RNN),
            from_rows(kt_s, NS), from_rows(vt_s, NS), jnp.transpose(hist_s, (1, 0, 2))[None], h_last_s[None])


def kernel(x_prompt, x_sample, cache_k, cache_v, state_conv, state_h, g_mix_norm, w_in, g_q_norm, g_k_norm, attn_sinks, conv_w, conv_b, w_lru_a, b_lru_a, w_lru_x, b_lru_x, lru_lambda, g_attn_out, g_rnn_out, w_out, g_ffn_norm, w_router, b_router, w_gate_up, b_gate_up, w_down, b_down):
    return _step(x_prompt, x_sample, cache_k, cache_v, state_conv, state_h, g_mix_norm, w_in, g_q_norm,
                 g_k_norm, attn_sinks, conv_w, conv_b, w_lru_a, b_lru_a, w_lru_x, b_lru_x, lru_lambda,
                 g_attn_out, g_rnn_out, w_out, g_ffn_norm, w_router, b_router, w_gate_up, b_gate_up,
                 w_down, b_down, tm=512, tt=512, bm=MOE_BLOCK_ROWS, past_len=PAST_LEN)
```

```python
import functools

import jax
import jax.numpy as jnp
from jax import lax
from jax.experimental import pallas as pl
from jax.experimental.pallas import tpu as pltpu

F32 = jnp.float32
BF16 = jnp.bfloat16
I32 = jnp.int32
I16 = jnp.int16

D_MODEL = 1024
HEAD_DIM = 64
N_HEADS = 8
N_KV_HEADS = 2
GROUP = 4
WINDOW = 128
ATTN_BLOCK = 128
ROT_DIM = 16
ROPE_THETA = 500000.0
D_ATTN = 512
D_RNN = 512
KV_W = 128
D_IN = 1792
CONV_WIDTH = 4
LRU_C = 8.0
N_EXPERTS = 32
TOP_K = 4
D_FF = 1024
SWIGLU_LIMIT = 7.0
SWIGLU_ALPHA = 1.702
EPS = 1e-6
PAST_LEN = 8192

LANES = 128
SUBLANES = 8
NEG_BIG = -1e30
VMEM_LIMIT = 56 * 1024 * 1024

MOE_BLOCK_ROWS = 512
PERM_COLS = 256
DISPATCH_CHUNK = 256
SAMPLE_ATTN_SEQS = 32


def _cparams(n_axes):
    return pltpu.CompilerParams(dimension_semantics=("arbitrary",) * n_axes,
                                vmem_limit_bytes=VMEM_LIMIT)


U32 = jnp.uint32
HALF = D_MODEL // 2
HI16 = 0xFFFF0000


def _pack_bf16_pairs(x):
    bits = pltpu.bitcast(x, U32)
    return (bits[:, HALF:] & U32(HI16)) | (bits[:, :HALF] >> 16)


def _unpack_bf16_pairs(u):
    lo = pltpu.bitcast(u << 16, F32)
    hi = pltpu.bitcast(u & U32(HI16), F32)
    return jnp.concatenate([lo, hi], axis=1).astype(BF16)


def _rmsnorm(x, g):
    ms = jnp.mean(x * x, axis=-1, keepdims=True)
    return (x * lax.rsqrt(ms + EPS)) * g


def _head_norm_rope(t, g, c, s1, s2, lo):
    sq = t * t
    s_lo = jnp.sum(jnp.where(lo, sq, 0.0), axis=-1, keepdims=True)
    s_hi = jnp.sum(jnp.where(lo, 0.0, sq), axis=-1, keepdims=True)
    ms = jnp.where(lo, s_lo, s_hi) * (1.0 / HEAD_DIM)
    n = (t * lax.rsqrt(ms + EPS)) * g
    up = pltpu.roll(n, LANES - ROT_DIM // 2, 1)
    dn = pltpu.roll(n, ROT_DIM // 2, 1)
    return n * c + up * s1 + dn * s2


def _in_proj_kernel(x_ref, g_ref, w_ref, gq_ref, gk_ref, c_ref, s1_ref, s2_ref,
                    q_ref, k_ref, v_ref, xr_ref, yr_ref):
    tm = x_ref.shape[0]
    h = _rmsnorm(x_ref[...], g_ref[...])
    proj = jnp.dot(h.astype(BF16), w_ref[...], preferred_element_type=F32)
    rope = (c_ref[...], s1_ref[...], s2_ref[...], lax.broadcasted_iota(I32, (tm, LANES), 1) < HEAD_DIM)
    gq = gq_ref[...]
    for j in range(D_ATTN // LANES):
        q_ref[:, j * LANES:(j + 1) * LANES] = _head_norm_rope(proj[:, j * LANES:(j + 1) * LANES], gq, *rope)
    k_ref[...] = _head_norm_rope(proj[:, D_ATTN:D_ATTN + KV_W], gk_ref[...], *rope)
    v_ref[...] = proj[:, D_ATTN + KV_W:D_ATTN + 2 * KV_W]
    o = D_ATTN + 2 * KV_W
    xr_ref[...] = proj[:, o:o + D_RNN]
    yr_ref[...] = proj[:, o + D_RNN:o + 2 * D_RNN]


def _in_proj(x2d, g, w_bf, gq2, gk2, ctab, s1tab, s2tab, tm):
    n = x2d.shape[0]
    ntab = ctab.shape[0] // tm
    row = lambda i: (i, 0)
    fix = lambda i: (0, 0)
    tab = lambda i: (i % ntab, 0)
    out_shapes = (jax.ShapeDtypeStruct((n, D_ATTN), F32), jax.ShapeDtypeStruct((n, KV_W), F32),
                  jax.ShapeDtypeStruct((n, KV_W), F32), jax.ShapeDtypeStruct((n, D_RNN), F32),
                  jax.ShapeDtypeStruct((n, D_RNN), F32))
    return pl.pallas_call(
        _in_proj_kernel,
        grid=(n // tm,),
        in_specs=[pl.BlockSpec((tm, D_MODEL), row), pl.BlockSpec((1, D_MODEL), fix),
                  pl.BlockSpec((D_MODEL, D_IN), fix), pl.BlockSpec((1, LANES), fix),
                  pl.BlockSpec((1, LANES), fix), pl.BlockSpec((tm, LANES), tab),
                  pl.BlockSpec((tm, LANES), tab), pl.BlockSpec((tm, LANES), tab)],
        out_specs=(pl.BlockSpec((tm, D_ATTN), row), pl.BlockSpec((tm, KV_W), row),
                   pl.BlockSpec((tm, KV_W), row), pl.BlockSpec((tm, D_RNN), row),
                   pl.BlockSpec((tm, D_RNN), row)),
        out_shape=out_shapes,
        compiler_params=_cparams(1),
        name="in_proj",
    )(x2d, g, w_bf, gq2, gk2, ctab, s1tab, s2tab)


def _rope_tables(pos):
    half = ROT_DIM // 2
    inv = ROPE_THETA ** (-jnp.arange(0, ROT_DIM, 2, dtype=F32) / ROT_DIM)
    ang = pos.astype(F32)[:, None] * inv[None, :]
    cos = jnp.cos(ang)
    sin = jnp.sin(ang)
    n = pos.shape[0]
    ones = jnp.ones((n, HEAD_DIM - ROT_DIM), F32)
    zeros = jnp.zeros((n, HEAD_DIM - ROT_DIM), F32)
    zh = jnp.zeros((n, half), F32)
    c = jnp.concatenate([cos, cos, ones], axis=1)
    s1 = jnp.concatenate([-sin, zh, zeros], axis=1)
    s2 = jnp.concatenate([zh, sin, zeros], axis=1)
    two = lambda t: jnp.concatenate([t, t], axis=1)
    return two(c), two(s1), two(s2)


def _band_bias(qb):
    qi = jnp.arange(qb, dtype=I32)[:, None]
    c = jnp.arange(2 * qb, dtype=I32)[None, :]
    band = (c >= qi) & (c <= qi + qb)
    first = band & (c >= qb)
    one = jnp.where(jnp.stack([first, band]), 0.0, NEG_BIG).astype(F32)
    return jnp.concatenate([one, one], axis=2)


def _attn_sample_kernel(sink_ref, q_ref, kn_ref, vn_ref, kt_ref, vt_ref, g_ref,
                        o_ref, nkt_ref, nvt_ref, acc_ref):
    bb = q_ref.shape[0]
    q = q_ref[...] * (HEAD_DIM ** -0.5)
    kn = kn_ref[...]
    vn = vn_ref[...]
    kt = kt_ref[...]
    vt = vt_ref[...]
    col = lax.broadcasted_iota(I32, (bb, bb * KV_W), 1)
    rowb = lax.broadcasted_iota(I32, (bb, bb * KV_W), 0)
    own_seq = (col >> (KV_W.bit_length() - 1)) == rowb
    half_hi = ((col >> (HEAD_DIM.bit_length() - 1)) & 1) == 1
    qbig = []
    for h in range(N_HEADS):
        kv = h // GROUP
        pair = q[:, (h // 2) * LANES:(h // 2 + 1) * LANES]
        if (h % 2) != kv:
            pair = pltpu.roll(pair, HEAD_DIM, 1)
        tiled = jnp.concatenate([pair] * bb, axis=1)
        keep = own_seq & (half_hi if kv == 1 else jnp.logical_not(half_hi))
        qbig.append(jnp.where(keep, tiled, 0.0))
    qbig = jnp.concatenate(qbig, axis=0)
    s = jnp.dot(qbig.astype(BF16), kt.astype(BF16), preferred_element_type=F32)
    qb16 = q.astype(BF16).astype(F32)
    kb16 = kn.astype(BF16).astype(F32)
    s_new, sink = [], []
    for h in range(N_HEADS):
        kv = h // GROUP
        s_new.append(jnp.sum(qb16[:, h * HEAD_DIM:(h + 1) * HEAD_DIM] * kb16[:, kv * HEAD_DIM:(kv + 1) * HEAD_DIM],
                             axis=-1, keepdims=True))
        sink.append(jnp.full((bb, 1), sink_ref[h], F32))
    s_new = jnp.concatenate(s_new, axis=0)
    sink = jnp.concatenate(sink, axis=0)
    m = jnp.maximum(jnp.maximum(jnp.max(s, axis=-1, keepdims=True), s_new), sink)
    e = jnp.exp(s - m)
    e_new = jnp.exp(s_new - m)
    inv = 1.0 / (jnp.sum(e, axis=-1, keepdims=True) + e_new + jnp.exp(sink - m))
    obig = lax.dot_general(e.astype(BF16), vt.astype(BF16), (((1,), (1,)), ((), ())),
                           preferred_element_type=F32)
    for h in range(N_HEADS):
        kv = h // GROUP
        blk = jnp.where(own_seq, obig[h * bb:(h + 1) * bb, :], 0.0)
        fold = blk[:, 0:KV_W]
        for t in range(1, bb):
            fold = fold + blk[:, t * KV_W:(t + 1) * KV_W]
        hs = slice(h * bb, (h + 1) * bb)
        ks = slice(kv * HEAD_DIM, (kv + 1) * HEAD_DIM)
        acc_ref[:, h * HEAD_DIM:(h + 1) * HEAD_DIM] = (fold[:, ks] + e_new[hs] * vn[:, ks]) * inv[hs]
    o_ref[...] = _rmsnorm(acc_ref[...], g_ref[...])

    last = lax.broadcasted_iota(I32, (KV_W, WINDOW), 1) == WINDOW - 1
    for b in range(bb):
        rs = slice(b * KV_W, (b + 1) * KV_W)
        kcol = jnp.broadcast_to(kn[b:b + 1, :], (KV_W, KV_W)).T
        vcol = jnp.broadcast_to(vn[b:b + 1, :], (KV_W, KV_W)).T
        nkt_ref[rs, :] = jnp.where(last, kcol, pltpu.roll(kt[rs, :], WINDOW - 1, 1))
        nvt_ref[rs, :] = jnp.where(last, vcol, pltpu.roll(vt[rs, :], WINDOW - 1, 1))


def _attn_sample(q, kn, vn, kt2d, vt2d, sinks, g_attn, bb):
    n = q.shape[0]
    row = lambda i: (i, 0)
    fix = lambda i: (0, 0)
    cache = pl.BlockSpec((bb * KV_W, WINDOW), row)
    return pl.pallas_call(
        _attn_sample_kernel,
        grid=(n // bb,),
        in_specs=[pl.BlockSpec(memory_space=pltpu.SMEM),
                  pl.BlockSpec((bb, D_ATTN), row), pl.BlockSpec((bb, KV_W), row),
                  pl.BlockSpec((bb, KV_W), row), cache, cache,
                  pl.BlockSpec((1, D_ATTN), fix)],
        out_specs=(pl.BlockSpec((bb, D_ATTN), row), cache, cache),
        out_shape=(jax.ShapeDtypeStruct((n, D_ATTN), F32),
                   jax.ShapeDtypeStruct(kt2d.shape, F32), jax.ShapeDtypeStruct(vt2d.shape, F32)),
        scratch_shapes=[pltpu.VMEM((bb, D_ATTN), F32)],
        compiler_params=_cparams(1),
        name="attn_sample",
    )(sinks, q, kn, vn, kt2d, vt2d, g_attn)


def _softplus(z):
    return jnp.maximum(z, 0.0) + jnp.log1p(jnp.exp(-jnp.abs(z)))


def _lru_gates(xc, wa_ref, ba_ref, wx_ref, bx_ref, lam_ref):
    xb = xc.astype(BF16)
    r = jax.nn.sigmoid(jnp.dot(xb, wa_ref[...], preferred_element_type=F32) + ba_ref[...])
    i = jax.nn.sigmoid(jnp.dot(xb, wx_ref[...], preferred_element_type=F32) + bx_ref[...])
    log_a = (-LRU_C * r) * _softplus(-lam_ref[...])
    a = jnp.exp(log_a)
    z = -jnp.tanh(log_a) * (a * a + 1.0)
    u = jnp.where(z > 0.0, z * lax.rsqrt(z), 0.0) * (i * xc)
    return a, u


def _lru_scan(a, u, h0):
    ng = a.shape[0] // SUBLANES
    a3 = a.reshape(ng, SUBLANES, D_RNN)
    u3 = u.reshape(ng, SUBLANES, D_RNN)
    t8 = lax.broadcasted_iota(I32, (ng, SUBLANES, D_RNN), 1)
    d = 1
    while d < SUBLANES:
        a_s = jnp.where(t8 >= d, pltpu.roll(a3, d, 1), 1.0)
        u_s = jnp.where(t8 >= d, pltpu.roll(u3, d, 1), 0.0)
        u3 = a3 * u_s + u3
        a3 = a3 * a_s
        d *= 2
    carry = h0
    groups = []
    for g in range(ng):
        hg = a3[g] * carry + u3[g]
        groups.append(hg)
        carry = hg[SUBLANES - 1:SUBLANES, :]
    return jnp.concatenate(groups, axis=0), carry


def _lru_scan_tiles(a_ref, u_ref, h_ref, h0):
    nl, rows, _ = a_ref.shape
    ng = rows // SUBLANES
    step = lambda ref, s: jnp.concatenate(
        [ref[j, pl.ds(s, ng, stride=SUBLANES), :] for j in range(nl)], axis=1)
    prods = [step(a_ref, 0)]
    locs = [step(u_ref, 0)]
    for s in range(1, SUBLANES):
        a_s = step(a_ref, s)
        locs.append(a_s * locs[-1] + step(u_ref, s))
        prods.append(a_s * prods[-1])
    after, h_last = _lru_scan(prods[-1], locs[-1], h0)
    row = lax.broadcasted_iota(I32, (ng, D_RNN), 0)
    before = jnp.where(row == 0, h0, pltpu.roll(after, 1, 0))
    for s in range(SUBLANES):
        h_s = locs[s] + prods[s] * before
        for j in range(nl):
            h_ref[j, pl.ds(s, ng, stride=SUBLANES), :] = h_s[:, j * LANES:(j + 1) * LANES]
    return h_last


def _to_lane_tiles(ref, x):
    for j in range(ref.shape[0]):
        ref[j] = x[:, j * LANES:(j + 1) * LANES]


def _rnn_sample_kernel(xr_ref, yr_ref, hist_ref, h0_ref, cw_ref, cb_ref, wa_ref, ba_ref, wx_ref, bx_ref,
                       lam_ref, g_ref, o_ref, hl_ref, nh_ref):
    cw = cw_ref[...]
    xr = xr_ref[...]
    xc = cb_ref[...] + xr * cw[CONV_WIDTH - 1:CONV_WIDTH, :]
    for w in range(CONV_WIDTH - 1):
        xc = xc + hist_ref[w] * cw[w:w + 1, :]
    a, u = _lru_gates(xc, wa_ref, ba_ref, wx_ref, bx_ref, lam_ref)
    h = a * h0_ref[...] + u
    hl_ref[...] = h
    o_ref[...] = _rmsnorm(jax.nn.gelu(yr_ref[...]) * h, g_ref[...])
    for w in range(CONV_WIDTH - 2):
        nh_ref[w] = hist_ref[w + 1]
    nh_ref[CONV_WIDTH - 2] = xr


def _rnn_sample(xr, yr, hist, h0, cw, cb, wa, ba, wx, bx, lam, g):
    n = xr.shape[0]
    full = lambda a: pl.BlockSpec(a.shape, lambda: (0,) * a.ndim)
    args = (xr, yr, hist, h0, cw, cb, wa, ba, wx, bx, lam, g)
    return pl.pallas_call(
        _rnn_sample_kernel,
        in_specs=[full(a) for a in args],
        out_specs=(pl.BlockSpec((n, D_RNN), lambda: (0, 0)), pl.BlockSpec((n, D_RNN), lambda: (0, 0)),
                   pl.BlockSpec(hist.shape, lambda: (0, 0, 0))),
        out_shape=(jax.ShapeDtypeStruct((n, D_RNN), F32), jax.ShapeDtypeStruct((n, D_RNN), F32),
                   jax.ShapeDtypeStruct(hist.shape, F32)),
        compiler_params=pltpu.CompilerParams(vmem_limit_bytes=VMEM_LIMIT),
        name="rnn_sample",
    )(*args)


def _front_kernel(*refs, tiles_per_seq):
    i = pl.program_id(0)
    q_s, k_s, v_s, xr_s, yr_s, ext_ref, h_ref = refs[25:32]

    @pl.when(i == 0)
    def _():
        for r in (q_s, k_s, v_s, xr_s, yr_s, ext_ref, h_ref):
            r[...] = jnp.zeros(r.shape, F32)

    for cur in range(2):
        @pl.when(i % 2 == cur)
        def _():
            _front_body(cur, 1 - cur, *refs, tiles_per_seq=tiles_per_seq)


def _front_body(cur, prv, sink_ref, x_ref, gm_ref, w_ref, gq_ref, gk_ref, c_ref, s1_ref, s2_ref, bias_ref,
                ga_ref, cw_ref, cb_ref, wa_ref, ba_ref, wx_ref, bx_ref, lam_ref, gr_ref,
                an_ref, rn_ref, kt_ref, vt_ref, hl_ref, cx_ref,
                q_s, k_s, v_s, xr_s, yr_s, ext_ref, h_ref, s_ref, e_ref, a_scr, u_scr, hs_scr,
                *, tiles_per_seq):
    i = pl.program_id(0)
    tm = x_ref.shape[0]
    qb = ATTN_BLOCK
    t = jnp.maximum(i - 1, 0)
    first_tile = (t % tiles_per_seq) == 0

    hx = _rmsnorm(x_ref[...], gm_ref[...]).astype(BF16)
    rope = (c_ref[...], s1_ref[...], s2_ref[...], lax.broadcasted_iota(I32, (tm, LANES), 1) < HEAD_DIM)
    project = lambda c0, c1: jnp.dot(hx, w_ref[:, c0:c1], preferred_element_type=F32)

    nqb = tm // qb
    n_pairs = N_HEADS // 2
    lo_k = lax.broadcasted_iota(I32, (2 * qb, LANES), 1) < HEAD_DIM
    lo_q = lax.broadcasted_iota(I32, (qb, LANES), 1) < HEAD_DIM
    nt_dims = (((1,), (1,)), ((), ()))
    vbds = []
    for jb in range(nqb):
        k2 = k_s[prv, jb * qb:(jb + 2) * qb, :]
        v2 = v_s[prv, jb * qb:(jb + 2) * qb, :]
        k2r = pltpu.roll(k2, HEAD_DIM, 1)
        v2r = pltpu.roll(v2, HEAD_DIM, 1)
        bias = bias_ref[jnp.where(first_tile, 0, 1)] if jb == 0 else bias_ref[1]
        for kv in range(N_KV_HEADS):
            ka, kb = (k2, k2r) if kv == 0 else (k2r, k2)
            va, vb = (v2, v2r) if kv == 0 else (v2r, v2)
            kbd = jnp.concatenate([jnp.where(lo_k, ka, 0.0), jnp.where(lo_k, 0.0, kb)], axis=0).astype(BF16)
            vbds.append(jnp.concatenate([jnp.where(lo_k, va, 0.0), jnp.where(lo_k, 0.0, vb)],
                                        axis=0).astype(BF16))
            for p in range(GROUP // 2):
                pp = kv * (GROUP // 2) + p
                qp = (q_s[prv, jb * qb:(jb + 1) * qb, pp * LANES:(pp + 1) * LANES]
                      * (HEAD_DIM ** -0.5)).astype(BF16)
                s_ref[jb * n_pairs + pp] = lax.dot_general(qp, kbd, nt_dims,
                                                           preferred_element_type=F32) + bias
    pq = project(0, D_ATTN)
    for j in range(D_ATTN // LANES):
        q_s[cur, :, j * LANES:(j + 1) * LANES] = _head_norm_rope(pq[:, j * LANES:(j + 1) * LANES], gq_ref[...],
                                                                 *rope)
    invs = []
    for c in range(nqb * n_pairs):
        pp = c % n_pairs
        inv = []
        for tpos in range(2):
            cols = slice(tpos * 2 * qb, (tpos + 1) * 2 * qb)
            st = s_ref[c, :, cols]
            sink = sink_ref[2 * pp + tpos]
            m = jnp.maximum(jnp.max(st, axis=-1, keepdims=True), sink)
            e = jnp.exp(st - m)
            e_ref[c, :, cols] = e.astype(BF16)
            inv.append(1.0 / (jnp.sum(e, axis=-1, keepdims=True) + jnp.exp(sink - m)))
        invs.append(jnp.where(lo_q, inv[0], inv[1]))
    pkv = project(D_ATTN, D_ATTN + 2 * KV_W)
    k_s[cur, qb:qb + tm, :] = _head_norm_rope(pkv[:, :KV_W], gk_ref[...], *rope)
    v_s[cur, qb:qb + tm, :] = pkv[:, KV_W:]
    k_s[cur, 0:qb, :] = k_s[prv, tm:tm + qb, :]
    v_s[cur, 0:qb, :] = v_s[prv, tm:tm + qb, :]
    for jb in range(nqb):
        outs = [jnp.dot(e_ref[jb * n_pairs + pp], vbds[jb * N_KV_HEADS + pp // (GROUP // 2)],
                        preferred_element_type=F32) * invs[jb * n_pairs + pp] for pp in range(n_pairs)]
        an_ref[jb * qb:(jb + 1) * qb, :] = _rmsnorm(jnp.concatenate(outs, axis=1), ga_ref[...])
    kt_ref[0] = k_s[prv, tm:tm + qb, :].T
    vt_ref[0] = v_s[prv, tm:tm + qb, :].T

    o = D_ATTN + 2 * KV_W
    xr_s[cur] = project(o, o + D_RNN)

    pad = SUBLANES
    xr = xr_s[prv]
    ext_ref[0:pad, :] = jnp.where(first_tile, 0.0, ext_ref[0:pad, :])
    ext_ref[pad:pad + tm, :] = xr
    cw = cw_ref[...]
    xc = cb_ref[...] + xr * cw[CONV_WIDTH - 1:CONV_WIDTH, :]
    for w in range(CONV_WIDTH - 1):
        sh = CONV_WIDTH - 1 - w
        xc = xc + ext_ref[pad - sh:pad - sh + tm, :] * cw[w:w + 1, :]
    ext_ref[0:pad, :] = xr[tm - pad:tm, :]
    cx_ref[0] = xr[tm - pad:tm, :]
    a, u = _lru_gates(xc, wa_ref, ba_ref, wx_ref, bx_ref, lam_ref)
    yr_s[cur] = project(o + D_RNN, o + 2 * D_RNN)
    _to_lane_tiles(a_scr, a)
    _to_lane_tiles(u_scr, u)
    carry = _lru_scan_tiles(a_scr, u_scr, hs_scr, jnp.where(first_tile, 0.0, h_ref[...]))
    h_ref[...] = carry
    hl_ref[0] = carry
    hseq = jnp.concatenate([hs_scr[j] for j in range(hs_scr.shape[0])], axis=1)
    rn_ref[...] = _rmsnorm(jax.nn.gelu(yr_s[prv]) * hseq, gr_ref[...])


def _front(x2d, sinks, g_mix, w_bf, gq2, gk2, ctab, s1tab, s2tab, g_attn,
           cw, cb, wa, ba, wx, bx, lam, g_rnn, batch, seq, tm):
    n = x2d.shape[0]
    nt = n // tm
    tps = seq // tm
    qb = ATTN_BLOCK
    cur = lambda i: (jnp.minimum(i, nt - 1), 0)
    tab = lambda i: (jnp.minimum(i, nt - 1) % tps, 0)
    fix = lambda i: (0, 0)
    prev = lambda i: (jnp.maximum(i - 1, 0), 0)
    per_seq = lambda i: (jnp.maximum(i - 1, 0) // tps, 0, 0)
    vec = lambda w: pl.BlockSpec((1, w), fix)
    return pl.pallas_call(
        functools.partial(_front_kernel, tiles_per_seq=tps),
        grid=(nt + 1,),
        in_specs=[pl.BlockSpec(memory_space=pltpu.SMEM),
                  pl.BlockSpec((tm, D_MODEL), cur), vec(D_MODEL), pl.BlockSpec((D_MODEL, D_IN), fix),
                  vec(LANES), vec(LANES),
                  pl.BlockSpec((tm, LANES), tab), pl.BlockSpec((tm, LANES), tab), pl.BlockSpec((tm, LANES), tab),
                  pl.BlockSpec((2, qb, 4 * qb), lambda i: (0, 0, 0)), vec(D_ATTN),
                  pl.BlockSpec((CONV_WIDTH, D_RNN), fix), vec(D_RNN),
                  pl.BlockSpec((D_RNN, D_RNN), fix), vec(D_RNN),
                  pl.BlockSpec((D_RNN, D_RNN), fix), vec(D_RNN), vec(D_RNN), vec(D_RNN)],
        out_specs=(pl.BlockSpec((tm, D_ATTN), prev), pl.BlockSpec((tm, D_RNN), prev),
                   pl.BlockSpec((1, KV_W, qb), per_seq), pl.BlockSpec((1, KV_W, qb), per_seq),
                   pl.BlockSpec((1, 1, D_RNN), per_seq), pl.BlockSpec((1, SUBLANES, D_RNN), per_seq)),
        out_shape=(jax.ShapeDtypeStruct((n, D_ATTN), F32), jax.ShapeDtypeStruct((n, D_RNN), F32),
                   jax.ShapeDtypeStruct((batch, KV_W, qb), F32), jax.ShapeDtypeStruct((batch, KV_W, qb), F32),
                   jax.ShapeDtypeStruct((batch, 1, D_RNN), F32),
                   jax.ShapeDtypeStruct((batch, SUBLANES, D_RNN), F32)),
        scratch_shapes=[pltpu.VMEM((2, tm, D_ATTN), F32),
                        pltpu.VMEM((2, tm + qb, KV_W), F32), pltpu.VMEM((2, tm + qb, KV_W), F32),
                        pltpu.VMEM((2, tm, D_RNN), F32), pltpu.VMEM((2, tm, D_RNN), F32),
                        pltpu.VMEM((tm + SUBLANES, D_RNN), F32), pltpu.VMEM((1, D_RNN), F32),
                        pltpu.VMEM((tm // qb * (N_HEADS // 2), qb, 4 * qb), F32),
                        pltpu.VMEM((tm // qb * (N_HEADS // 2), qb, 4 * qb), BF16),
                        pltpu.VMEM((D_RNN // LANES, tm, LANES), F32), pltpu.VMEM((D_RNN // LANES, tm, LANES), F32),
                        pltpu.VMEM((D_RNN // LANES, tm, LANES), F32)],
        compiler_params=_cparams(1),
        name="front",
    )(sinks, x2d, g_mix, w_bf, gq2, gk2, ctab, s1tab, s2tab, _band_bias(qb), g_attn,
      cw, cb, wa, ba, wx, bx, lam, g_rnn)


def _mix_route_kernel(*refs):
    i = pl.program_id(0)
    hb_s, d_s = refs[16:18]

    @pl.when(i == 0)
    def _():
        hb_s[...] = jnp.zeros(hb_s.shape, BF16)
        d_s[...] = jnp.zeros(d_s.shape, I32)

    for cur in range(2):
        @pl.when(i % 2 == cur)
        def _():
            _mix_route_body(cur, 1 - cur, *refs)


def _mix_route_body(cur, prv, x_ref, an_ref, rn_ref, woa_ref, wor_ref, g_ref, wr2_ref, br_ref, tri_ref, low_ref,
                    x2_ref, ts_ref, dest_ref, gate_ref, n8_ref, off_ref, hb_s, d_s):
    tt = x_ref.shape[0]
    tile_rows = ts_ref.shape[0]
    n_chunks = tile_rows // DISPATCH_CHUNK

    hb_prev = hb_s[prv]
    d16 = [d_s[prv, k:k + 1, :].astype(I16) for k in range(TOP_K)]
    ri = lax.broadcasted_iota(I32, (DISPATCH_CHUNK, tt), 0).astype(I16)
    one = jnp.ones((DISPATCH_CHUNK, tt), BF16)

    def dispatch(chunks):
        for c in chunks:
            p = jnp.zeros((DISPATCH_CHUNK, tt), BF16)
            for d in d16:
                p = jnp.where(ri == d - jnp.int16(c * DISPATCH_CHUNK), one, p)
            ts_ref[c * DISPATCH_CHUNK:(c + 1) * DISPATCH_CHUNK, :] = _pack_bf16_pairs(
                jnp.dot(p, hb_prev, preferred_element_type=F32))

    third = -(-n_chunks // 3)

    x2 = x_ref[...] + jnp.dot(an_ref[...].astype(BF16), woa_ref[...], preferred_element_type=F32) \
        + jnp.dot(rn_ref[...].astype(BF16), wor_ref[...], preferred_element_type=F32)
    x2_ref[...] = x2
    dispatch(range(0, third))
    hn = _rmsnorm(x2, g_ref[...])

    nt = (((1,), (1,)), ((), ()))
    hb = hn.astype(BF16)
    hb_s[cur] = hb
    hmid = (hn - hb.astype(F32)).astype(BF16)
    wr2 = wr2_ref[...]
    both = lax.dot_general(wr2, hb, nt, preferred_element_type=F32)
    logits = (lax.dot_general(wr2[:N_EXPERTS], hmid, nt, preferred_element_type=F32)
              + both[N_EXPERTS:]) + both[:N_EXPERTS] + br_ref[...]
    dispatch(range(third, 2 * third))

    ie = lax.broadcasted_iota(I32, (N_EXPERTS, tt), 0).astype(F32)
    l = logits
    vals, sels = [], []
    for _ in range(TOP_K):
        m = jnp.max(l, axis=0, keepdims=True)
        idx = jnp.min(jnp.where(l == m, ie, float(N_EXPERTS)), axis=0, keepdims=True)
        sel = ie == idx
        vals.append(m)
        sels.append(sel)
        l = jnp.where(sel, NEG_BIG, l)
    es = [jnp.exp(v - vals[0]) for v in vals]
    den = es[0] + es[1] + es[2] + es[3]
    gate_ref[0] = jnp.concatenate([e / den for e in es], axis=0)
    dispatch(range(2 * third, n_chunks))

    oh = jnp.zeros((N_EXPERTS, tt), F32)
    for sel in sels:
        oh = oh + jnp.where(sel, 1.0, 0.0)
    before = jnp.dot(oh.astype(BF16), tri_ref[...], preferred_element_type=F32)
    cnt = jnp.sum(oh, axis=1, keepdims=True).astype(I32)
    n8 = ((cnt + (SUBLANES - 1)) >> 3) << 3
    n8b = jnp.broadcast_to(n8, (N_EXPERTS, LANES))
    off = jnp.dot(low_ref[...], n8b.astype(F32).astype(BF16), preferred_element_type=F32)
    n8_ref[0] = n8b
    off_ref[0] = off.astype(I32)
    base = off[:, 0:1] + before
    dests = jnp.concatenate(
        [jnp.sum(jnp.where(sel, base, 0.0), axis=0, keepdims=True).astype(I32) for sel in sels], axis=0)
    dest_ref[0] = dests
    d_s[cur, 0:TOP_K, :] = dests


def _tile_rows(tt):
    return -(-(TOP_K * tt + N_EXPERTS * (SUBLANES - 1)) // DISPATCH_CHUNK) * DISPATCH_CHUNK


def _mix_route(x2d, an, rn, woa, wor, g, wrt, br, low, tt):
    n = x2d.shape[0]
    nt = n // tt
    tile_rows = _tile_rows(tt)
    tri = jnp.triu(jnp.ones((tt, tt), BF16), k=1)
    row = lambda i: (jnp.minimum(i, nt - 1), 0)
    prev = lambda i: (jnp.maximum(i - 1, 0), 0)
    fix = lambda i: (0, 0)
    t3 = lambda i: (jnp.minimum(i, nt - 1), 0, 0)
    in_specs = [pl.BlockSpec((tt, D_MODEL), row), pl.BlockSpec((tt, D_ATTN), row),
                pl.BlockSpec((tt, D_RNN), row), pl.BlockSpec((D_ATTN, D_MODEL), fix),
                pl.BlockSpec((D_RNN, D_MODEL), fix), pl.BlockSpec((1, D_MODEL), fix),
                pl.BlockSpec((2 * N_EXPERTS, D_MODEL), fix), pl.BlockSpec((N_EXPERTS, 1), fix),
                pl.BlockSpec((tt, tt), fix), pl.BlockSpec((N_EXPERTS, N_EXPERTS), fix)]
    out_shape = (jax.ShapeDtypeStruct((n, D_MODEL), F32),
                 jax.ShapeDtypeStruct((nt * tile_rows, HALF), U32),
                 jax.ShapeDtypeStruct((nt, TOP_K, tt), I32),
                 jax.ShapeDtypeStruct((nt, TOP_K, tt), F32),
                 jax.ShapeDtypeStruct((nt, N_EXPERTS, LANES), I32),
                 jax.ShapeDtypeStruct((nt, N_EXPERTS, LANES), I32))
    out_specs = (pl.BlockSpec((tt, D_MODEL), row),
                 pl.BlockSpec((tile_rows, HALF), prev),
                 pl.BlockSpec((1, TOP_K, tt), t3), pl.BlockSpec((1, TOP_K, tt), t3),
                 pl.BlockSpec((1, N_EXPERTS, LANES), t3), pl.BlockSpec((1, N_EXPERTS, LANES), t3))
    return pl.pallas_call(
        _mix_route_kernel,
        grid=(nt + 1,),
        in_specs=in_specs,
        out_specs=out_specs,
        out_shape=out_shape,
        scratch_shapes=[pltpu.VMEM((2, tt, D_MODEL), BF16), pltpu.VMEM((2, SUBLANES, tt), I32)],
        compiler_params=_cparams(1),
        name="mix_route",
    )(x2d, an, rn, woa, wor, g, wrt, br, tri, low)


LOW_BITS = 4


def _start_piece(src_hbm, dst_buf, sem, s, d, l8, nbits):
    def bit_copy(c):
        size = SUBLANES << c
        low = (l8 & ((1 << c) - 1)) * SUBLANES

        @pl.when(((l8 >> c) & 1) == 1)
        def _():
            pltpu.make_async_copy(
                src_hbm.at[pl.ds(pl.multiple_of(s + low, SUBLANES), size)],
                dst_buf.at[pl.ds(pl.multiple_of(d + low, SUBLANES), size)], sem).start()

    for c in range(min(LOW_BITS, nbits)):
        bit_copy(c)
    if nbits > LOW_BITS:
        def long_copies(_, carry):
            for c in range(LOW_BITS, nbits):
                bit_copy(c)
            return carry

        lax.fori_loop(0, jnp.where(l8 >= (1 << LOW_BITS), 1, 0), long_copies, 0)


def _start_pieces(src_hbm, dst_buf, sem, p_lo, p_hi, psrc_ref, pdst_ref, plen_ref, nbits):
    def body(p, carry):
        _start_piece(src_hbm, dst_buf, sem, psrc_ref[p], pdst_ref[p], plen_ref[p], nbits)
        return carry

    lax.fori_loop(p_lo, p_hi, body, 0)


def _wait_rows(src_hbm, dst_buf, sem, rows8, nbits):
    for c in range(nbits):
        size = SUBLANES << c

        @pl.when(((rows8 >> c) & 1) == 1)
        def _():
            pltpu.make_async_copy(src_hbm.at[pl.ds(0, size)], dst_buf.at[pl.ds(0, size)], sem).wait()


def _moe_gmm_kernel(be_ref, rows_ref, wslot_ref, nxt_ref,
                    psa_ref, pea_ref, srca_ref, dsta_ref, lena_ref, hsrca_ref, hlena_ref,
                    psb_ref, peb_ref, srcb_ref, dstb_ref, lenb_ref, hsrcb_ref, hlenb_ref,
                    tsa_hbm, tsb_hbm, wgu_hbm, wdn_hbm, bg_ref, bu_ref, bd_ref, perm_ref,
                    ys_ref, lhs_ref, wgu_buf, wdn_buf, wg_ref, wu_ref, wd_ref, sem_ref, wsem_ref,
                    *, nbits_a, nbits_b):
    j = pl.program_id(0)
    nb = pl.num_programs(0)
    slot = j % 2
    bm = lhs_ref.shape[1]

    def gather(blk, sl):
        for ts_hbm, ps, pe, src, dst, ln, hsrc, hlen, nbits in (
                (tsa_hbm, psa_ref, pea_ref, srca_ref, dsta_ref, lena_ref, hsrca_ref, hlena_ref, nbits_a),
                (tsb_hbm, psb_ref, peb_ref, srcb_ref, dstb_ref, lenb_ref, hsrcb_ref, hlenb_ref, nbits_b)):
            _start_pieces(ts_hbm, lhs_ref.at[sl], sem_ref.at[sl], ps[blk], pe[blk], src, dst, ln, nbits)
            _start_piece(ts_hbm, lhs_ref.at[sl], sem_ref.at[sl], hsrc[blk], 0, hlen[blk], nbits)

    def weight_copies(e, ws):
        return (pltpu.make_async_copy(wgu_hbm.at[e], wgu_buf.at[ws], wsem_ref.at[ws]),
                pltpu.make_async_copy(wdn_hbm.at[e], wdn_buf.at[ws], wsem_ref.at[ws]))

    @pl.when(j == 0)
    def _():
        lhs_ref[...] = jnp.zeros(lhs_ref.shape, U32)
        gather(0, 0)
        for cp in weight_copies(be_ref[0], wslot_ref[0]):
            cp.start()

    @pl.when(j + 1 < nb)
    def _():
        gather(j + 1, 1 - slot)

    @pl.when(jnp.logical_or(j == 0, be_ref[j] != be_ref[jnp.maximum(j - 1, 0)]))
    def _():
        ws = wslot_ref[j]
        for cp in weight_copies(be_ref[j], ws):
            cp.wait()
        nxt = nxt_ref[j]

        @pl.when(nxt >= 0)
        def _():
            for cp in weight_copies(nxt, 1 - ws):
                cp.start()

        perm = perm_ref[...]
        half = PERM_COLS // 2
        for c in range(2 * D_FF // PERM_COLS):
            wb = wgu_buf[ws, :, c * PERM_COLS:(c + 1) * PERM_COLS].astype(BF16)
            wp = jnp.dot(wb, perm, preferred_element_type=F32).astype(BF16)
            wg_ref[:, c * half:(c + 1) * half] = wp[:, :half]
            wu_ref[:, c * half:(c + 1) * half] = wp[:, half:]
        wd_ref[...] = wdn_buf[ws].astype(BF16)

    rows8 = rows_ref[j]
    _wait_rows(tsa_hbm, lhs_ref.at[slot], sem_ref.at[slot], rows8, (bm // SUBLANES).bit_length())

    def expert_rows(rs):
        x = _unpack_bf16_pairs(lhs_ref[slot, rs, :])
        gate = jnp.dot(x, wg_ref[...], preferred_element_type=F32) + bg_ref[0]
        up = jnp.dot(x, wu_ref[...], preferred_element_type=F32) + bu_ref[0]
        gate = jnp.minimum(gate, SWIGLU_LIMIT)
        up = jnp.clip(up, -SWIGLU_LIMIT, SWIGLU_LIMIT)
        act = (up + 1.0) * (gate * jax.nn.sigmoid(SWIGLU_ALPHA * gate))
        y = jnp.dot(act.astype(BF16), wd_ref[...], preferred_element_type=F32) + bd_ref[0]
        ys_ref[rs, :] = _pack_bf16_pairs(y.astype(BF16).astype(F32))

    quarter = bm // 4
    for nq in range(5):
        lo8, hi8 = (nq - 1) * quarter // SUBLANES, nq * quarter // SUBLANES

        @pl.when(jnp.logical_and(rows8 > lo8, rows8 <= hi8) if nq else rows8 == 0)
        def _():
            if nq:
                expert_rows(slice(0, nq * quarter))
            if nq < 4:
                ys_ref[nq * quarter:, :] = jnp.zeros((bm - nq * quarter, HALF), U32)


def _moe_gmm(blocks, tabs_a, tabs_b, ts_a, ts_b, w_gu, w_dn, bg, bu, bd, perm, nblocks, bm, nbits_a, nbits_b):
    we = lambda j, be, *_: (be[j], 0, 0)
    grid_spec = pltpu.PrefetchScalarGridSpec(
        num_scalar_prefetch=18,
        grid=(nblocks,),
        in_specs=[pl.BlockSpec(memory_space=pl.ANY), pl.BlockSpec(memory_space=pl.ANY),
                  pl.BlockSpec(memory_space=pl.ANY), pl.BlockSpec(memory_space=pl.ANY),
                  pl.BlockSpec((1, 1, D_FF), we), pl.BlockSpec((1, 1, D_FF), we),
                  pl.BlockSpec((1, 1, D_MODEL), we),
                  pl.BlockSpec((PERM_COLS, PERM_COLS), lambda j, *_: (0, 0))],
        out_specs=pl.BlockSpec((bm, HALF), lambda j, *_: (j, 0)),
        scratch_shapes=[pltpu.VMEM((2, bm, HALF), U32),
                        pltpu.VMEM((2, D_MODEL, 2 * D_FF), F32), pltpu.VMEM((2, D_FF, D_MODEL), F32),
                        pltpu.VMEM((D_MODEL, D_FF), BF16), pltpu.VMEM((D_MODEL, D_FF), BF16),
                        pltpu.VMEM((D_FF, D_MODEL), BF16),
                        pltpu.SemaphoreType.DMA((2,)), pltpu.SemaphoreType.DMA((2,))],
    )
    return pl.pallas_call(
        functools.partial(_moe_gmm_kernel, nbits_a=nbits_a, nbits_b=nbits_b),
        grid_spec=grid_spec,
        out_shape=jax.ShapeDtypeStruct((nblocks * bm, HALF), U32),
        compiler_params=_cparams(1),
        name="moe_gmm",
    )(*blocks, *tabs_a, *tabs_b, ts_a, ts_b, w_gu, w_dn, bg, bu, bd, perm)


def _combine_kernel(psrc_ref, pdst_ref, plen_ref, tlo_ref, thi_ref, tsrc_ref, tdst_ref, tlen_ref, rows_ref,
                    ys_hbm, x2_ref, dest_ref, gate_ref, o_ref,
                    buf_ref, db_ref, gb_ref, sem_ref, *, nbits):
    i = pl.program_id(0)
    n = pl.num_programs(0)
    slot = i % 2
    tt = x2_ref.shape[0]
    tile_rows = buf_ref.shape[1]

    def gather(tile, sl):
        _start_pieces(ys_hbm, buf_ref.at[sl], sem_ref.at[sl], tile * N_EXPERTS, (tile + 1) * N_EXPERTS,
                      psrc_ref, pdst_ref, plen_ref, nbits)
        _start_pieces(ys_hbm, buf_ref.at[sl], sem_ref.at[sl], tlo_ref[tile], thi_ref[tile],
                      tsrc_ref, tdst_ref, tlen_ref, nbits)

    @pl.when(i == 0)
    def _():
        buf_ref[...] = jnp.zeros(buf_ref.shape, U32)
        gather(0, 0)

    @pl.when(i + 1 < n)
    def _():
        gather(i + 1, 1 - slot)

    _wait_rows(ys_hbm, buf_ref.at[slot], sem_ref.at[slot], rows_ref[i], (tile_rows // SUBLANES).bit_length())

    dest = dest_ref[0]
    gate = gate_ref[0]
    for k in range(TOP_K):
        db_ref[k] = jnp.broadcast_to(dest[:, k:k + 1], (tt, DISPATCH_CHUNK)).astype(I16)
        gb_ref[k] = jnp.broadcast_to(gate[:, k:k + 1], (tt, DISPATCH_CHUNK)).astype(BF16)
    li = lax.broadcasted_iota(I32, (tt, DISPATCH_CHUNK), 1).astype(I16)
    gms = []
    for c in range(tile_rows // DISPATCH_CHUNK):
        lic = li + jnp.int16(c * DISPATCH_CHUNK)
        gm = jnp.zeros((tt, DISPATCH_CHUNK), BF16)
        for k in range(TOP_K):
            gm = jnp.where(lic == db_ref[k], gb_ref[k], gm)
        gms.append(gm)
    o_ref[...] = x2_ref[...] + jnp.dot(jnp.concatenate(gms, axis=1), _unpack_bf16_pairs(buf_ref[slot]),
                                       preferred_element_type=F32)


def _combine(tabs, ys, x2, dest, gate, tt, nbits):
    n = x2.shape[0]
    nt = n // tt
    tile_rows = _tile_rows(tt)
    grid_spec = pltpu.PrefetchScalarGridSpec(
        num_scalar_prefetch=len(tabs),
        grid=(nt,),
        in_specs=[pl.BlockSpec(memory_space=pl.ANY),
                  pl.BlockSpec((tt, D_MODEL), lambda i, *_: (i, 0)),
                  pl.BlockSpec((1, tt, TOP_K), lambda i, *_: (i, 0, 0)),
                  pl.BlockSpec((1, tt, TOP_K), lambda i, *_: (i, 0, 0))],
        out_specs=pl.BlockSpec((tt, D_MODEL), lambda i, *_: (i, 0)),
        scratch_shapes=[pltpu.VMEM((2, tile_rows, HALF), U32),
                        pltpu.VMEM((TOP_K, tt, DISPATCH_CHUNK), I16), pltpu.VMEM((TOP_K, tt, DISPATCH_CHUNK), BF16),
                        pltpu.SemaphoreType.DMA((2,))],
    )
    return pl.pallas_call(
        functools.partial(_combine_kernel, nbits=nbits),
        grid_spec=grid_spec,
        out_shape=jax.ShapeDtypeStruct((n, D_MODEL), F32),
        compiler_params=_cparams(1),
        name="combine",
    )(*tabs, ys, x2, dest, gate)


def _piece_tables(n8_a, off_a, rows_a, n8_b, off_b, bm, nblocks):
    nta = n8_a.shape[0]
    n8 = jnp.concatenate([n8_a, n8_b], axis=0)
    seg_off = jnp.concatenate([off_a, off_b], axis=0)
    n_tiles = n8.shape[0]
    tile_base = jnp.concatenate([jnp.arange(nta, dtype=I32) * rows_a, jnp.zeros((n_tiles - nta,), I32)])[:, None]
    tot = jnp.sum(n8, axis=0)
    pos0 = jnp.cumsum(n8, axis=0) - n8
    nblk = (tot + bm - 1) // bm
    cs = jnp.cumsum(nblk)
    bs = cs - nblk
    kblk = pos0 // bm
    len0 = jnp.minimum(n8, (kblk + 1) * bm - pos0)
    len1 = n8 - len0
    b0 = bs[None, :] + kblk
    src0 = tile_base + seg_off
    in_blk = pos0 - kblk * bm
    jj = jnp.arange(nblocks, dtype=I32)
    i32 = lambda v: v.astype(I32)

    def gmm_tabs(sl):
        em = lambda v: v[sl].T.reshape(-1)
        blk_em = em(b0)
        first = i32(jnp.sum(blk_em[None, :] < jj[:, None], axis=1))
        last = i32(jnp.sum(blk_em[None, :] <= jj[:, None], axis=1))
        hit = (blk_em[None, :] + 1 == jj[:, None]) & (em(len1)[None, :] > 0)
        tail_src = i32(jnp.sum(jnp.where(hit, em(src0 + len0)[None, :], 0), axis=1))
        tail_len = i32(jnp.sum(jnp.where(hit, em(len1)[None, :], 0), axis=1) // SUBLANES)
        return first, last, i32(em(src0)), i32(em(in_blk)), i32(em(len0) // SUBLANES), tail_src, tail_len

    def comb_tabs(sl):
        tm = lambda v: v[sl].reshape(-1)
        has_tail = len1[sl] > 0
        cnt = jnp.sum(has_tail, axis=1)
        lo = jnp.cumsum(cnt) - cnt
        slot_ = lo[:, None] + jnp.cumsum(has_tail, axis=1) - has_tail
        hit = (slot_.reshape(-1)[None, :] == jj[:, None]) & has_tail.reshape(-1)[None, :]
        pick = lambda v: i32(jnp.sum(jnp.where(hit, tm(v)[None, :], 0), axis=1))
        return (i32(tm(b0 * bm + in_blk)), i32(tm(seg_off)), i32(tm(len0) // SUBLANES),
                i32(lo), i32(lo + cnt), pick((b0 + 1) * bm), pick(seg_off + len0), pick(len1 // SUBLANES),
                i32(jnp.sum(n8[sl], axis=1) // SUBLANES))

    count_le = lambda v: jnp.sum(cs[None, :] <= v[:, None], axis=1)
    n_active = cs[-1]
    e_last = count_le(jnp.maximum(n_active - 1, 0).reshape(1))[0]
    block_e = jnp.minimum(count_le(jj), e_last).astype(I32)
    ee = jnp.arange(N_EXPERTS, dtype=I32)
    mine = (jj[:, None] >= bs[None, :]) & (jj[:, None] < cs[None, :])
    left = jnp.clip(tot[None, :] - (jj[:, None] - bs[None, :]) * bm, 0, bm)
    rows8 = (jnp.sum(jnp.where(mine, left, 0), axis=1) // SUBLANES).astype(I32)
    has = nblk > 0
    run = jnp.cumsum(has.astype(I32)) - 1
    later = (ee[None, :] > ee[:, None]) & has[None, :]
    nxt_e = jnp.min(jnp.where(later, ee[None, :], N_EXPERTS), axis=1)
    nxt_e = jnp.where(nxt_e == N_EXPERTS, -1, nxt_e)
    own = block_e[:, None] == ee[None, :]
    wslot = (jnp.sum(jnp.where(own, run[None, :], 0), axis=1) % 2).astype(I32)
    nxt = jnp.sum(jnp.where(own, nxt_e[None, :], 0), axis=1).astype(I32)
    a, b = slice(0, nta), slice(nta, n_tiles)
    return (block_e, rows8, wslot, nxt), gmm_tabs(a), gmm_tabs(b), comb_tabs(a), comb_tabs(b)


def _block_diag(w):
    nb, bi, bo = w.shape
    eye = jnp.eye(nb, dtype=w.dtype)
    return (eye[:, None, :, None] * w[:, :, None, :]).reshape(nb * bi, nb * bo)


def _step(x_prompt, x_sample, cache_k, cache_v, state_conv, state_h, g_mix_norm, w_in, g_q_norm, g_k_norm,
          attn_sinks, conv_w, conv_b, w_lru_a, b_lru_a, w_lru_x, b_lru_x, lru_lambda, g_attn_out, g_rnn_out,
          w_out, g_ffn_norm, w_router, b_router, w_gate_up, b_gate_up, w_down, b_down,
          *, tm, tt, bm, past_len):
    B, S, D = x_prompt.shape
    NS = x_sample.shape[0]
    assert x_sample.shape[1] == 1 and D == D_MODEL
    assert (B * S) % tt == 0 and S % tm == 0 and tm % ATTN_BLOCK == 0
    assert NS % SUBLANES == 0 and tt <= bm
    assert tt % SUBLANES == 0 and NS <= bm
    n_pt = (B * S) // tt
    total_rows = TOP_K * (B * S + NS) + (n_pt + 1) * N_EXPERTS * (SUBLANES - 1)
    nblocks = -(-total_rows // bm) + N_EXPERTS
    nbits_p = (tt // SUBLANES).bit_length()
    nbits_s = (NS // SUBLANES).bit_length()

    l = 0
    row = lambda v: v[l].reshape(1, -1)
    w_in_bf = w_in[l].astype(BF16)
    gq2 = jnp.tile(g_q_norm[l], 2).reshape(1, LANES)
    gk2 = jnp.tile(g_k_norm[l], 2).reshape(1, LANES)
    wa = _block_diag(w_lru_a[l]).astype(BF16)
    wx = _block_diag(w_lru_x[l]).astype(BF16)
    ba = b_lru_a[l].reshape(1, D_RNN)
    bx = b_lru_x[l].reshape(1, D_RNN)
    wo = w_out[l].astype(BF16)
    woa, wor = wo[:D_ATTN], wo[D_ATTN:]
    wr = w_router[l].T
    wr_hi = wr.astype(BF16)
    wrt = jnp.concatenate([wr_hi, (wr - wr_hi.astype(F32)).astype(BF16)], axis=0)
    br = b_router[l].reshape(N_EXPERTS, 1)
    low = jnp.tril(jnp.ones((N_EXPERTS, N_EXPERTS), BF16), k=-1)
    bgu = b_gate_up[l].reshape(N_EXPERTS, D_FF, 2)
    bg = bgu[:, :, 0].reshape(N_EXPERTS, 1, D_FF)
    bu = bgu[:, :, 1].reshape(N_EXPERTS, 1, D_FF)
    bd = b_down[l].reshape(N_EXPERTS, 1, D_MODEL)
    half = PERM_COLS // 2
    pr = jnp.arange(PERM_COLS)
    perm = (pr[None, :] == jnp.where(pr % 2 == 0, pr // 2, half + pr // 2)[:, None]).astype(BF16)
    sinks = attn_sinks[l]

    ctab, s1tab, s2tab = _rope_tables(jnp.arange(S))
    an, rn, kt_p, vt_p, h_last_p, xr_tail = _front(
        x_prompt.reshape(B * S, D), sinks, row(g_mix_norm), w_in_bf, gq2, gk2, ctab, s1tab, s2tab,
        row(g_attn_out), conv_w[l], row(conv_b), wa, ba, wx, bx, row(lru_lambda), row(g_rnn_out), B, S, tm)
    x2_p, ts_p, dest_p, gate_p, n8_p, off_p = _mix_route(
        x_prompt.reshape(B * S, D), an, rn, woa, wor, row(g_ffn_norm), wrt, br, low, tt)

    cs_tab = _rope_tables(jnp.full((NS,), past_len, I32))
    q_s, k_s, v_s, xr_s, yr_s = _in_proj(x_sample.reshape(NS, D), row(g_mix_norm), w_in_bf, gq2, gk2,
                                         *cs_tab, NS)
    to_rows = lambda c: jnp.transpose(c, (0, 2, 3, 1)).reshape(NS * KV_W, WINDOW)
    from_rows = lambda c, n: jnp.transpose(c.reshape(n, N_KV_HEADS, HEAD_DIM, WINDOW), (0, 3, 1, 2))[None]
    an_s, kt_s, vt_s = _attn_sample(q_s, k_s, v_s, to_rows(cache_k[l]), to_rows(cache_v[l]), sinks,
                                    row(g_attn_out), min(NS, SAMPLE_ATTN_SEQS))
    rn_s, h_last_s, hist_s = _rnn_sample(xr_s, yr_s, jnp.transpose(state_conv[l], (1, 0, 2)), state_h[l],
                                         conv_w[l], row(conv_b), wa, ba, wx, bx, row(lru_lambda),
                                         row(g_rnn_out))
    x2_s, ts_s, dest_s, gate_s, n8_s, off_s = _mix_route(
        x_sample.reshape(NS, D), an_s, rn_s, woa, wor, row(g_ffn_norm), wrt, br, low, NS)

    blocks, gmm_p, gmm_s, comb_p, comb_s = _piece_tables(
        n8_p[:, :, 0], off_p[:, :, 0], _tile_rows(tt), n8_s[:, :, 0], off_s[:, :, 0], bm, nblocks)
    ys = _moe_gmm(blocks, gmm_p, gmm_s, ts_p, ts_s, w_gate_up[l], w_down[l], bg, bu, bd, perm,
                  nblocks, bm, nbits_p, nbits_s)
    tr = lambda a: jnp.transpose(a, (0, 2, 1))
    y_p = _combine(comb_p, ys, x2_p, tr(dest_p), tr(gate_p), tt, nbits_p)
    y_s = _combine(comb_s, ys, x2_s, tr(dest_s), tr(gate_s), NS, nbits_s)

    cp = xr_tail[:, SUBLANES - (CONV_WIDTH - 1):]
    return (y_p.reshape(B, S, D), y_s.reshape(NS, 1, D),
            from_rows(kt_p, B), from_rows(vt_p, B), cp[None], h_last_p.reshape(1, B, D_RNN),
            from_rows(kt_s, NS), from_rows(vt_s, NS), jnp.transpose(hist_s, (1, 0, 2))[None], h_last_s[None])


def kernel(x_prompt, x_sample, cache_k, cache_v, state_conv, state_h, g_mix_norm, w_in, g_q_norm, g_k_norm, attn_sinks, conv_w, conv_b, w_lru_a, b_lru_a, w_lru_x, b_lru_x, lru_lambda, g_attn_out, g_rnn_out, w_out, g_ffn_norm, w_router, b_router, w_gate_up, b_gate_up, w_down, b_down):
    return _step(x_prompt, x_sample, cache_k, cache_v, state_conv, state_h, g_mix_norm, w_in, g_q_norm,
                 g_k_norm, attn_sinks, conv_w, conv_b, w_lru_a, b_lru_a, w_lru_x, b_lru_x, lru_lambda,
                 g_attn_out, g_rnn_out, w_out, g_ffn_norm, w_router, b_router, w_gate_up, b_gate_up,
                 w_down, b_down, tm=512, tt=512, bm=MOE_BLOCK_ROWS, past_len=PAST_LEN)
```

```python
import functools

import jax
import jax.numpy as jnp
from jax import lax
from jax.experimental import pallas as pl
from jax.experimental.pallas import tpu as pltpu

F32 = jnp.float32
BF16 = jnp.bfloat16
I32 = jnp.int32
I16 = jnp.int16

D_MODEL = 1024
HEAD_DIM = 64
N_HEADS = 8
N_KV_HEADS = 2
GROUP = 4
WINDOW = 128
ATTN_BLOCK = 128
ROT_DIM = 16
ROPE_THETA = 500000.0
D_ATTN = 512
D_RNN = 512
KV_W = 128
D_IN = 1792
CONV_WIDTH = 4
LRU_C = 8.0
N_EXPERTS = 32
TOP_K = 4
D_FF = 1024
SWIGLU_LIMIT = 7.0
SWIGLU_ALPHA = 1.702
EPS = 1e-6
PAST_LEN = 8192

LANES = 128
SUBLANES = 8
NEG_BIG = -1e30
VMEM_LIMIT = 56 * 1024 * 1024

MOE_BLOCK_ROWS = 512
PERM_COLS = 256
DISPATCH_CHUNK = 256
SAMPLE_ATTN_SEQS = 32


def _cparams(n_axes):
    return pltpu.CompilerParams(dimension_semantics=("arbitrary",) * n_axes,
                                vmem_limit_bytes=VMEM_LIMIT)


U32 = jnp.uint32
HALF = D_MODEL // 2
HI16 = 0xFFFF0000


def _pack_bf16_pairs(x):
    bits = pltpu.bitcast(x, U32)
    return (bits[:, HALF:] & U32(HI16)) | (bits[:, :HALF] >> 16)


def _unpack_bf16_pairs(u):
    lo = pltpu.bitcast(u << 16, F32)
    hi = pltpu.bitcast(u & U32(HI16), F32)
    return jnp.concatenate([lo, hi], axis=1).astype(BF16)


def _rmsnorm(x, g):
    ms = jnp.mean(x * x, axis=-1, keepdims=True)
    return (x * lax.rsqrt(ms + EPS)) * g


def _head_norm_rope(t, g, c, s1, s2, lo):
    sq = t * t
    s_lo = jnp.sum(jnp.where(lo, sq, 0.0), axis=-1, keepdims=True)
    s_hi = jnp.sum(jnp.where(lo, 0.0, sq), axis=-1, keepdims=True)
    ms = jnp.where(lo, s_lo, s_hi) * (1.0 / HEAD_DIM)
    n = (t * lax.rsqrt(ms + EPS)) * g
    up = pltpu.roll(n, LANES - ROT_DIM // 2, 1)
    dn = pltpu.roll(n, ROT_DIM // 2, 1)
    return n * c + up * s1 + dn * s2


def _in_proj_kernel(x_ref, g_ref, w_ref, gq_ref, gk_ref, c_ref, s1_ref, s2_ref,
                    q_ref, k_ref, v_ref, xr_ref, yr_ref):
    tm = x_ref.shape[0]
    h = _rmsnorm(x_ref[...], g_ref[...])
    proj = jnp.dot(h.astype(BF16), w_ref[...], preferred_element_type=F32)
    rope = (c_ref[...], s1_ref[...], s2_ref[...], lax.broadcasted_iota(I32, (tm, LANES), 1) < HEAD_DIM)
    gq = gq_ref[...]
    for j in range(D_ATTN // LANES):
        q_ref[:, j * LANES:(j + 1) * LANES] = _head_norm_rope(proj[:, j * LANES:(j + 1) * LANES], gq, *rope)
    k_ref[...] = _head_norm_rope(proj[:, D_ATTN:D_ATTN + KV_W], gk_ref[...], *rope)
    v_ref[...] = proj[:, D_ATTN + KV_W:D_ATTN + 2 * KV_W]
    o = D_ATTN + 2 * KV_W
    xr_ref[...] = proj[:, o:o + D_RNN]
    yr_ref[...] = proj[:, o + D_RNN:o + 2 * D_RNN]


def _in_proj(x2d, g, w_bf, gq2, gk2, ctab, s1tab, s2tab, tm):
    n = x2d.shape[0]
    ntab = ctab.shape[0] // tm
    row = lambda i: (i, 0)
    fix = lambda i: (0, 0)
    tab = lambda i: (i % ntab, 0)
    out_shapes = (jax.ShapeDtypeStruct((n, D_ATTN), F32), jax.ShapeDtypeStruct((n, KV_W), F32),
                  jax.ShapeDtypeStruct((n, KV_W), F32), jax.ShapeDtypeStruct((n, D_RNN), F32),
                  jax.ShapeDtypeStruct((n, D_RNN), F32))
    return pl.pallas_call(
        _in_proj_kernel,
        grid=(n // tm,),
        in_specs=[pl.BlockSpec((tm, D_MODEL), row), pl.BlockSpec((1, D_MODEL), fix),
                  pl.BlockSpec((D_MODEL, D_IN), fix), pl.BlockSpec((1, LANES), fix),
                  pl.BlockSpec((1, LANES), fix), pl.BlockSpec((tm, LANES), tab),
                  pl.BlockSpec((tm, LANES), tab), pl.BlockSpec((tm, LANES), tab)],
        out_specs=(pl.BlockSpec((tm, D_ATTN), row), pl.BlockSpec((tm, KV_W), row),
                   pl.BlockSpec((tm, KV_W), row), pl.BlockSpec((tm, D_RNN), row),
                   pl.BlockSpec((tm, D_RNN), row)),
        out_shape=out_shapes,
        compiler_params=_cparams(1),
        name="in_proj",
    )(x2d, g, w_bf, gq2, gk2, ctab, s1tab, s2tab)


def _rope_tables(pos):
    half = ROT_DIM // 2
    inv = ROPE_THETA ** (-jnp.arange(0, ROT_DIM, 2, dtype=F32) / ROT_DIM)
    d = jnp.arange(LANES) % HEAD_DIM
    ang = pos.astype(F32)[:, None] * inv[d % half][None, :]
    cos = jnp.cos(ang)
    sin = jnp.sin(ang)
    c = jnp.where(d < ROT_DIM, cos, 1.0)
    s1 = jnp.where(d < half, -sin, 0.0)
    s2 = jnp.where((d >= half) & (d < ROT_DIM), sin, 0.0)
    return c, s1, s2


def _band_bias(qb):
    qi = jnp.arange(qb, dtype=I32)[:, None]
    c = jnp.arange(2 * qb, dtype=I32)[None, :]
    band = (c >= qi) & (c <= qi + qb)
    first = band & (c >= qb)
    one = jnp.where(jnp.stack([first, band]), 0.0, NEG_BIG).astype(F32)
    return jnp.concatenate([one, one], axis=2)


def _attn_sample_kernel(sink_ref, q_ref, kn_ref, vn_ref, kt_ref, vt_ref, g_ref,
                        o_ref, nkt_ref, nvt_ref, acc_ref):
    bb = q_ref.shape[0]
    q = q_ref[...] * (HEAD_DIM ** -0.5)
    kn = kn_ref[...]
    vn = vn_ref[...]
    kt = kt_ref[...]
    vt = vt_ref[...]
    col = lax.broadcasted_iota(I32, (bb, bb * KV_W), 1)
    rowb = lax.broadcasted_iota(I32, (bb, bb * KV_W), 0)
    own_seq = (col >> (KV_W.bit_length() - 1)) == rowb
    half_hi = ((col >> (HEAD_DIM.bit_length() - 1)) & 1) == 1
    qbig = []
    for h in range(N_HEADS):
        kv = h // GROUP
        pair = q[:, (h // 2) * LANES:(h // 2 + 1) * LANES]
        if (h % 2) != kv:
            pair = pltpu.roll(pair, HEAD_DIM, 1)
        tiled = jnp.concatenate([pair] * bb, axis=1)
        keep = own_seq & (half_hi if kv == 1 else jnp.logical_not(half_hi))
        qbig.append(jnp.where(keep, tiled, 0.0))
    qbig = jnp.concatenate(qbig, axis=0)
    s = jnp.dot(qbig.astype(BF16), kt.astype(BF16), preferred_element_type=F32)
    qb16 = q.astype(BF16).astype(F32)
    kb16 = kn.astype(BF16).astype(F32)
    s_new, sink = [], []
    for h in range(N_HEADS):
        kv = h // GROUP
        s_new.append(jnp.sum(qb16[:, h * HEAD_DIM:(h + 1) * HEAD_DIM] * kb16[:, kv * HEAD_DIM:(kv + 1) * HEAD_DIM],
                             axis=-1, keepdims=True))
        sink.append(jnp.full((bb, 1), sink_ref[h], F32))
    s_new = jnp.concatenate(s_new, axis=0)
    sink = jnp.concatenate(sink, axis=0)
    m = jnp.maximum(jnp.maximum(jnp.max(s, axis=-1, keepdims=True), s_new), sink)
    e = jnp.exp(s - m)
    e_new = jnp.exp(s_new - m)
    inv = 1.0 / (jnp.sum(e, axis=-1, keepdims=True) + e_new + jnp.exp(sink - m))
    obig = lax.dot_general(e.astype(BF16), vt.astype(BF16), (((1,), (1,)), ((), ())),
                           preferred_element_type=F32)
    for h in range(N_HEADS):
        kv = h // GROUP
        blk = jnp.where(own_seq, obig[h * bb:(h + 1) * bb, :], 0.0)
        fold = blk[:, 0:KV_W]
        for t in range(1, bb):
            fold = fold + blk[:, t * KV_W:(t + 1) * KV_W]
        hs = slice(h * bb, (h + 1) * bb)
        ks = slice(kv * HEAD_DIM, (kv + 1) * HEAD_DIM)
        acc_ref[:, h * HEAD_DIM:(h + 1) * HEAD_DIM] = (fold[:, ks] + e_new[hs] * vn[:, ks]) * inv[hs]
    o_ref[...] = _rmsnorm(acc_ref[...], g_ref[...])

    last = lax.broadcasted_iota(I32, (KV_W, WINDOW), 1) == WINDOW - 1
    for b in range(bb):
        rs = slice(b * KV_W, (b + 1) * KV_W)
        kcol = jnp.broadcast_to(kn[b:b + 1, :], (KV_W, KV_W)).T
        vcol = jnp.broadcast_to(vn[b:b + 1, :], (KV_W, KV_W)).T
        nkt_ref[rs, :] = jnp.where(last, kcol, pltpu.roll(kt[rs, :], WINDOW - 1, 1))
        nvt_ref[rs, :] = jnp.where(last, vcol, pltpu.roll(vt[rs, :], WINDOW - 1, 1))


def _attn_sample(q, kn, vn, kt2d, vt2d, sinks, g_attn, bb):
    n = q.shape[0]
    row = lambda i: (i, 0)
    fix = lambda i: (0, 0)
    cache = pl.BlockSpec((bb * KV_W, WINDOW), row)
    return pl.pallas_call(
        _attn_sample_kernel,
        grid=(n // bb,),
        in_specs=[pl.BlockSpec(memory_space=pltpu.SMEM),
                  pl.BlockSpec((bb, D_ATTN), row), pl.BlockSpec((bb, KV_W), row),
                  pl.BlockSpec((bb, KV_W), row), cache, cache,
                  pl.BlockSpec((1, D_ATTN), fix)],
        out_specs=(pl.BlockSpec((bb, D_ATTN), row), cache, cache),
        out_shape=(jax.ShapeDtypeStruct((n, D_ATTN), F32),
                   jax.ShapeDtypeStruct(kt2d.shape, F32), jax.ShapeDtypeStruct(vt2d.shape, F32)),
        scratch_shapes=[pltpu.VMEM((bb, D_ATTN), F32)],
        compiler_params=_cparams(1),
        name="attn_sample",
    )(sinks, q, kn, vn, kt2d, vt2d, g_attn)


def _softplus(z):
    return jnp.maximum(z, 0.0) + jnp.log1p(jnp.exp(-jnp.abs(z)))


def _lru_gates(xc, wa_ref, ba_ref, wx_ref, bx_ref, lam_ref):
    xb = xc.astype(BF16)
    r = jax.nn.sigmoid(jnp.dot(xb, wa_ref[...], preferred_element_type=F32) + ba_ref[...])
    i = jax.nn.sigmoid(jnp.dot(xb, wx_ref[...], preferred_element_type=F32) + bx_ref[...])
    log_a = (-LRU_C * r) * _softplus(-lam_ref[...])
    a = jnp.exp(log_a)
    z = -jnp.tanh(log_a) * (a * a + 1.0)
    u = jnp.where(z > 0.0, z * lax.rsqrt(z), 0.0) * (i * xc)
    return a, u


def _lru_scan(a, u, h0):
    ng = a.shape[0] // SUBLANES
    a3 = a.reshape(ng, SUBLANES, D_RNN)
    u3 = u.reshape(ng, SUBLANES, D_RNN)
    t8 = lax.broadcasted_iota(I32, (ng, SUBLANES, D_RNN), 1)
    d = 1
    while d < SUBLANES:
        a_s = jnp.where(t8 >= d, pltpu.roll(a3, d, 1), 1.0)
        u_s = jnp.where(t8 >= d, pltpu.roll(u3, d, 1), 0.0)
        u3 = a3 * u_s + u3
        a3 = a3 * a_s
        d *= 2
    carry = h0
    groups = []
    for g in range(ng):
        hg = a3[g] * carry + u3[g]
        groups.append(hg)
        carry = hg[SUBLANES - 1:SUBLANES, :]
    return jnp.concatenate(groups, axis=0), carry


def _lru_scan_tiles(a_ref, u_ref, h_ref, h0):
    nl, rows, _ = a_ref.shape
    ng = rows // SUBLANES
    step = lambda ref, s: jnp.concatenate(
        [ref[j, pl.ds(s, ng, stride=SUBLANES), :] for j in range(nl)], axis=1)
    prods = [step(a_ref, 0)]
    locs = [step(u_ref, 0)]
    for s in range(1, SUBLANES):
        a_s = step(a_ref, s)
        locs.append(a_s * locs[-1] + step(u_ref, s))
        prods.append(a_s * prods[-1])
    after, h_last = _lru_scan(prods[-1], locs[-1], h0)
    row = lax.broadcasted_iota(I32, (ng, D_RNN), 0)
    before = jnp.where(row == 0, h0, pltpu.roll(after, 1, 0))
    for s in range(SUBLANES):
        h_s = locs[s] + prods[s] * before
        for j in range(nl):
            h_ref[j, pl.ds(s, ng, stride=SUBLANES), :] = h_s[:, j * LANES:(j + 1) * LANES]
    return h_last


def _to_lane_tiles(ref, x):
    for j in range(ref.shape[0]):
        ref[j] = x[:, j * LANES:(j + 1) * LANES]


def _rnn_sample_kernel(xr_ref, yr_ref, hist_ref, h0_ref, cw_ref, cb_ref, wa_ref, ba_ref, wx_ref, bx_ref,
                       lam_ref, g_ref, o_ref, hl_ref, nh_ref):
    cw = cw_ref[...]
    xr = xr_ref[...]
    xc = cb_ref[...] + xr * cw[CONV_WIDTH - 1:CONV_WIDTH, :]
    for w in range(CONV_WIDTH - 1):
        xc = xc + hist_ref[w] * cw[w:w + 1, :]
    a, u = _lru_gates(xc, wa_ref, ba_ref, wx_ref, bx_ref, lam_ref)
    h = a * h0_ref[...] + u
    hl_ref[...] = h
    o_ref[...] = _rmsnorm(jax.nn.gelu(yr_ref[...]) * h, g_ref[...])
    for w in range(CONV_WIDTH - 2):
        nh_ref[w] = hist_ref[w + 1]
    nh_ref[CONV_WIDTH - 2] = xr


def _rnn_sample(xr, yr, hist, h0, cw, cb, wa, ba, wx, bx, lam, g):
    n = xr.shape[0]
    full = lambda a: pl.BlockSpec(a.shape, lambda: (0,) * a.ndim)
    args = (xr, yr, hist, h0, cw, cb, wa, ba, wx, bx, lam, g)
    return pl.pallas_call(
        _rnn_sample_kernel,
        in_specs=[full(a) for a in args],
        out_specs=(pl.BlockSpec((n, D_RNN), lambda: (0, 0)), pl.BlockSpec((n, D_RNN), lambda: (0, 0)),
                   pl.BlockSpec(hist.shape, lambda: (0, 0, 0))),
        out_shape=(jax.ShapeDtypeStruct((n, D_RNN), F32), jax.ShapeDtypeStruct((n, D_RNN), F32),
                   jax.ShapeDtypeStruct(hist.shape, F32)),
        compiler_params=pltpu.CompilerParams(vmem_limit_bytes=VMEM_LIMIT),
        name="rnn_sample",
    )(*args)


def _front_kernel(*refs, tiles_per_seq):
    i = pl.program_id(0)
    q_s, k_s, v_s, xr_s, yr_s, ext_ref, h_ref = refs[25:32]

    @pl.when(i == 0)
    def _():
        for r in (q_s, k_s, v_s, xr_s, yr_s, ext_ref, h_ref):
            r[...] = jnp.zeros(r.shape, F32)

    for cur in range(2):
        @pl.when(i % 2 == cur)
        def _():
            _front_body(cur, 1 - cur, *refs, tiles_per_seq=tiles_per_seq)


def _front_body(cur, prv, sink_ref, x_ref, gm_ref, w_ref, gq_ref, gk_ref, c_ref, s1_ref, s2_ref, bias_ref,
                ga_ref, cw_ref, cb_ref, wa_ref, ba_ref, wx_ref, bx_ref, lam_ref, gr_ref,
                an_ref, rn_ref, kt_ref, vt_ref, hl_ref, cx_ref,
                q_s, k_s, v_s, xr_s, yr_s, ext_ref, h_ref, s_ref, e_ref, a_scr, u_scr, hs_scr,
                *, tiles_per_seq):
    i = pl.program_id(0)
    tm = x_ref.shape[0]
    qb = ATTN_BLOCK
    t = jnp.maximum(i - 1, 0)
    first_tile = (t % tiles_per_seq) == 0

    hx = _rmsnorm(x_ref[...], gm_ref[...]).astype(BF16)
    rope = (c_ref[...], s1_ref[...], s2_ref[...], lax.broadcasted_iota(I32, (tm, LANES), 1) < HEAD_DIM)
    project = lambda c0, c1: jnp.dot(hx, w_ref[:, c0:c1], preferred_element_type=F32)

    nqb = tm // qb
    n_pairs = N_HEADS // 2
    lo_k = lax.broadcasted_iota(I32, (2 * qb, LANES), 1) < HEAD_DIM
    lo_q = lax.broadcasted_iota(I32, (qb, LANES), 1) < HEAD_DIM
    nt_dims = (((1,), (1,)), ((), ()))
    vbds = []
    for jb in range(nqb):
        k2 = k_s[prv, jb * qb:(jb + 2) * qb, :]
        v2 = v_s[prv, jb * qb:(jb + 2) * qb, :]
        k2r = pltpu.roll(k2, HEAD_DIM, 1)
        v2r = pltpu.roll(v2, HEAD_DIM, 1)
        bias = bias_ref[jnp.where(first_tile, 0, 1)] if jb == 0 else bias_ref[1]
        for kv in range(N_KV_HEADS):
            ka, kb = (k2, k2r) if kv == 0 else (k2r, k2)
            va, vb = (v2, v2r) if kv == 0 else (v2r, v2)
            kbd = jnp.concatenate([jnp.where(lo_k, ka, 0.0), jnp.where(lo_k, 0.0, kb)], axis=0).astype(BF16)
            vbds.append(jnp.concatenate([jnp.where(lo_k, va, 0.0), jnp.where(lo_k, 0.0, vb)],
                                        axis=0).astype(BF16))
            for p in range(GROUP // 2):
                pp = kv * (GROUP // 2) + p
                qp = (q_s[prv, jb * qb:(jb + 1) * qb, pp * LANES:(pp + 1) * LANES]
                      * (HEAD_DIM ** -0.5)).astype(BF16)
                s_ref[jb * n_pairs + pp] = lax.dot_general(qp, kbd, nt_dims,
                                                           preferred_element_type=F32) + bias
    pq = project(0, D_ATTN)
    for j in range(D_ATTN // LANES):
        q_s[cur, :, j * LANES:(j + 1) * LANES] = _head_norm_rope(pq[:, j * LANES:(j + 1) * LANES], gq_ref[...],
                                                                 *rope)
    invs = []
    for c in range(nqb * n_pairs):
        pp = c % n_pairs
        inv = []
        for tpos in range(2):
            cols = slice(tpos * 2 * qb, (tpos + 1) * 2 * qb)
            st = s_ref[c, :, cols]
            sink = sink_ref[2 * pp + tpos]
            m = jnp.maximum(jnp.max(st, axis=-1, keepdims=True), sink)
            e = jnp.exp(st - m)
            e_ref[c, :, cols] = e.astype(BF16)
            inv.append(1.0 / (jnp.sum(e, axis=-1, keepdims=True) + jnp.exp(sink - m)))
        invs.append(jnp.where(lo_q, inv[0], inv[1]))
    pkv = project(D_ATTN, D_ATTN + 2 * KV_W)
    k_s[cur, qb:qb + tm, :] = _head_norm_rope(pkv[:, :KV_W], gk_ref[...], *rope)
    v_s[cur, qb:qb + tm, :] = pkv[:, KV_W:]
    k_s[cur, 0:qb, :] = k_s[prv, tm:tm + qb, :]
    v_s[cur, 0:qb, :] = v_s[prv, tm:tm + qb, :]
    for jb in range(nqb):
        outs = [jnp.dot(e_ref[jb * n_pairs + pp], vbds[jb * N_KV_HEADS + pp // (GROUP // 2)],
                        preferred_element_type=F32) * invs[jb * n_pairs + pp] for pp in range(n_pairs)]
        an_ref[jb * qb:(jb + 1) * qb, :] = _rmsnorm(jnp.concatenate(outs, axis=1), ga_ref[...])
    kt_ref[0] = k_s[prv, tm:tm + qb, :].T
    vt_ref[0] = v_s[prv, tm:tm + qb, :].T

    o = D_ATTN + 2 * KV_W
    xr_s[cur] = project(o, o + D_RNN)

    pad = SUBLANES
    xr = xr_s[prv]
    ext_ref[0:pad, :] = jnp.where(first_tile, 0.0, ext_ref[0:pad, :])
    ext_ref[pad:pad + tm, :] = xr
    cw = cw_ref[...]
    xc = cb_ref[...] + xr * cw[CONV_WIDTH - 1:CONV_WIDTH, :]
    for w in range(CONV_WIDTH - 1):
        sh = CONV_WIDTH - 1 - w
        xc = xc + ext_ref[pad - sh:pad - sh + tm, :] * cw[w:w + 1, :]
    ext_ref[0:pad, :] = xr[tm - pad:tm, :]
    cx_ref[0] = xr[tm - pad:tm, :]
    a, u = _lru_gates(xc, wa_ref, ba_ref, wx_ref, bx_ref, lam_ref)
    yr_s[cur] = project(o + D_RNN, o + 2 * D_RNN)
    _to_lane_tiles(a_scr, a)
    _to_lane_tiles(u_scr, u)
    carry = _lru_scan_tiles(a_scr, u_scr, hs_scr, jnp.where(first_tile, 0.0, h_ref[...]))
    h_ref[...] = carry
    hl_ref[0] = carry
    hseq = jnp.concatenate([hs_scr[j] for j in range(hs_scr.shape[0])], axis=1)
    rn_ref[...] = _rmsnorm(jax.nn.gelu(yr_s[prv]) * hseq, gr_ref[...])


def _front(x2d, sinks, g_mix, w_bf, gq2, gk2, ctab, s1tab, s2tab, g_attn,
           cw, cb, wa, ba, wx, bx, lam, g_rnn, batch, seq, tm):
    n = x2d.shape[0]
    nt = n // tm
    tps = seq // tm
    qb = ATTN_BLOCK
    cur = lambda i: (jnp.minimum(i, nt - 1), 0)
    tab = lambda i: (jnp.minimum(i, nt - 1) % tps, 0)
    fix = lambda i: (0, 0)
    prev = lambda i: (jnp.maximum(i - 1, 0), 0)
    per_seq = lambda i: (jnp.maximum(i - 1, 0) // tps, 0, 0)
    vec = lambda w: pl.BlockSpec((1, w), fix)
    return pl.pallas_call(
        functools.partial(_front_kernel, tiles_per_seq=tps),
        grid=(nt + 1,),
        in_specs=[pl.BlockSpec(memory_space=pltpu.SMEM),
                  pl.BlockSpec((tm, D_MODEL), cur), vec(D_MODEL), pl.BlockSpec((D_MODEL, D_IN), fix),
                  vec(LANES), vec(LANES),
                  pl.BlockSpec((tm, LANES), tab), pl.BlockSpec((tm, LANES), tab), pl.BlockSpec((tm, LANES), tab),
                  pl.BlockSpec((2, qb, 4 * qb), lambda i: (0, 0, 0)), vec(D_ATTN),
                  pl.BlockSpec((CONV_WIDTH, D_RNN), fix), vec(D_RNN),
                  pl.BlockSpec((D_RNN, D_RNN), fix), vec(D_RNN),
                  pl.BlockSpec((D_RNN, D_RNN), fix), vec(D_RNN), vec(D_RNN), vec(D_RNN)],
        out_specs=(pl.BlockSpec((tm, D_ATTN), prev), pl.BlockSpec((tm, D_RNN), prev),
                   pl.BlockSpec((1, KV_W, qb), per_seq), pl.BlockSpec((1, KV_W, qb), per_seq),
                   pl.BlockSpec((1, 1, D_RNN), per_seq), pl.BlockSpec((1, SUBLANES, D_RNN), per_seq)),
        out_shape=(jax.ShapeDtypeStruct((n, D_ATTN), F32), jax.ShapeDtypeStruct((n, D_RNN), F32),
                   jax.ShapeDtypeStruct((batch, KV_W, qb), F32), jax.ShapeDtypeStruct((batch, KV_W, qb), F32),
                   jax.ShapeDtypeStruct((batch, 1, D_RNN), F32),
                   jax.ShapeDtypeStruct((batch, SUBLANES, D_RNN), F32)),
        scratch_shapes=[pltpu.VMEM((2, tm, D_ATTN), F32),
                        pltpu.VMEM((2, tm + qb, KV_W), F32), pltpu.VMEM((2, tm + qb, KV_W), F32),
                        pltpu.VMEM((2, tm, D_RNN), F32), pltpu.VMEM((2, tm, D_RNN), F32),
                        pltpu.VMEM((tm + SUBLANES, D_RNN), F32), pltpu.VMEM((1, D_RNN), F32),
                        pltpu.VMEM((tm // qb * (N_HEADS // 2), qb, 4 * qb), F32),
                        pltpu.VMEM((tm // qb * (N_HEADS // 2), qb, 4 * qb), BF16),
                        pltpu.VMEM((D_RNN // LANES, tm, LANES), F32), pltpu.VMEM((D_RNN // LANES, tm, LANES), F32),
                        pltpu.VMEM((D_RNN // LANES, tm, LANES), F32)],
        compiler_params=_cparams(1),
        name="front",
    )(sinks, x2d, g_mix, w_bf, gq2, gk2, ctab, s1tab, s2tab, _band_bias(qb), g_attn,
      cw, cb, wa, ba, wx, bx, lam, g_rnn)


def _mix_route_kernel(*refs):
    i = pl.program_id(0)
    hb_s, d_s = refs[16:18]

    @pl.when(i == 0)
    def _():
        hb_s[...] = jnp.zeros(hb_s.shape, BF16)
        d_s[...] = jnp.zeros(d_s.shape, I32)

    for cur in range(2):
        @pl.when(i % 2 == cur)
        def _():
            _mix_route_body(cur, 1 - cur, *refs)


def _mix_route_body(cur, prv, x_ref, an_ref, rn_ref, woa_ref, wor_ref, g_ref, wr2_ref, br_ref, tri_ref, low_ref,
                    x2_ref, ts_ref, dest_ref, gate_ref, n8_ref, off_ref, hb_s, d_s):
    tt = x_ref.shape[0]
    tile_rows = ts_ref.shape[0]
    n_chunks = tile_rows // DISPATCH_CHUNK

    hb_prev = hb_s[prv]
    d16 = [d_s[prv, k:k + 1, :].astype(I16) for k in range(TOP_K)]
    ri = lax.broadcasted_iota(I32, (DISPATCH_CHUNK, tt), 0).astype(I16)
    one = jnp.ones((DISPATCH_CHUNK, tt), BF16)

    def dispatch(chunks):
        for c in chunks:
            p = jnp.zeros((DISPATCH_CHUNK, tt), BF16)
            for d in d16:
                p = jnp.where(ri == d - jnp.int16(c * DISPATCH_CHUNK), one, p)
            ts_ref[c * DISPATCH_CHUNK:(c + 1) * DISPATCH_CHUNK, :] = _pack_bf16_pairs(
                jnp.dot(p, hb_prev, preferred_element_type=F32))

    third = -(-n_chunks // 3)

    x2 = x_ref[...] + jnp.dot(an_ref[...].astype(BF16), woa_ref[...], preferred_element_type=F32) \
        + jnp.dot(rn_ref[...].astype(BF16), wor_ref[...], preferred_element_type=F32)
    x2_ref[...] = x2
    dispatch(range(0, third))
    hn = _rmsnorm(x2, g_ref[...])

    nt = (((1,), (1,)), ((), ()))
    hb = hn.astype(BF16)
    hb_s[cur] = hb
    hmid = (hn - hb.astype(F32)).astype(BF16)
    wr2 = wr2_ref[...]
    both = lax.dot_general(wr2, hb, nt, preferred_element_type=F32)
    logits = (lax.dot_general(wr2[:N_EXPERTS], hmid, nt, preferred_element_type=F32)
              + both[N_EXPERTS:]) + both[:N_EXPERTS] + br_ref[...]
    dispatch(range(third, 2 * third))

    ie = lax.broadcasted_iota(I32, (N_EXPERTS, tt), 0).astype(F32)
    l = logits
    vals, sels = [], []
    for _ in range(TOP_K):
        m = jnp.max(l, axis=0, keepdims=True)
        idx = jnp.min(jnp.where(l == m, ie, float(N_EXPERTS)), axis=0, keepdims=True)
        sel = ie == idx
        vals.append(m)
        sels.append(sel)
        l = jnp.where(sel, NEG_BIG, l)
    es = [jnp.exp(v - vals[0]) for v in vals]
    den = es[0] + es[1] + es[2] + es[3]
    gate_ref[0] = jnp.concatenate([e / den for e in es], axis=0)
    dispatch(range(2 * third, n_chunks))

    oh = jnp.zeros((N_EXPERTS, tt), F32)
    for sel in sels:
        oh = oh + jnp.where(sel, 1.0, 0.0)
    before = jnp.dot(oh.astype(BF16), tri_ref[...], preferred_element_type=F32)
    cnt = jnp.sum(oh, axis=1, keepdims=True).astype(I32)
    n8 = ((cnt + (SUBLANES - 1)) >> 3) << 3
    n8b = jnp.broadcast_to(n8, (N_EXPERTS, LANES))
    off = jnp.dot(low_ref[...], n8b.astype(F32).astype(BF16), preferred_element_type=F32)
    n8_ref[0] = n8b
    off_ref[0] = off.astype(I32)
    base = off[:, 0:1] + before
    dests = jnp.concatenate(
        [jnp.sum(jnp.where(sel, base, 0.0), axis=0, keepdims=True).astype(I32) for sel in sels], axis=0)
    dest_ref[0] = dests
    d_s[cur, 0:TOP_K, :] = dests


def _tile_rows(tt):
    return -(-(TOP_K * tt + N_EXPERTS * (SUBLANES - 1)) // DISPATCH_CHUNK) * DISPATCH_CHUNK


def _mix_route(x2d, an, rn, woa, wor, g, wrt, br, low, tt):
    n = x2d.shape[0]
    nt = n // tt
    tile_rows = _tile_rows(tt)
    tri = jnp.triu(jnp.ones((tt, tt), BF16), k=1)
    row = lambda i: (jnp.minimum(i, nt - 1), 0)
    prev = lambda i: (jnp.maximum(i - 1, 0), 0)
    fix = lambda i: (0, 0)
    t3 = lambda i: (jnp.minimum(i, nt - 1), 0, 0)
    in_specs = [pl.BlockSpec((tt, D_MODEL), row), pl.BlockSpec((tt, D_ATTN), row),
                pl.BlockSpec((tt, D_RNN), row), pl.BlockSpec((D_ATTN, D_MODEL), fix),
                pl.BlockSpec((D_RNN, D_MODEL), fix), pl.BlockSpec((1, D_MODEL), fix),
                pl.BlockSpec((2 * N_EXPERTS, D_MODEL), fix), pl.BlockSpec((N_EXPERTS, 1), fix),
                pl.BlockSpec((tt, tt), fix), pl.BlockSpec((N_EXPERTS, N_EXPERTS), fix)]
    out_shape = (jax.ShapeDtypeStruct((n, D_MODEL), F32),
                 jax.ShapeDtypeStruct((nt * tile_rows, HALF), U32),
                 jax.ShapeDtypeStruct((nt, TOP_K, tt), I32),
                 jax.ShapeDtypeStruct((nt, TOP_K, tt), F32),
                 jax.ShapeDtypeStruct((nt, N_EXPERTS, LANES), I32),
                 jax.ShapeDtypeStruct((nt, N_EXPERTS, LANES), I32))
    out_specs = (pl.BlockSpec((tt, D_MODEL), row),
                 pl.BlockSpec((tile_rows, HALF), prev),
                 pl.BlockSpec((1, TOP_K, tt), t3), pl.BlockSpec((1, TOP_K, tt), t3),
                 pl.BlockSpec((1, N_EXPERTS, LANES), t3), pl.BlockSpec((1, N_EXPERTS, LANES), t3))
    return pl.pallas_call(
        _mix_route_kernel,
        grid=(nt + 1,),
        in_specs=in_specs,
        out_specs=out_specs,
        out_shape=out_shape,
        scratch_shapes=[pltpu.VMEM((2, tt, D_MODEL), BF16), pltpu.VMEM((2, SUBLANES, tt), I32)],
        compiler_params=_cparams(1),
        name="mix_route",
    )(x2d, an, rn, woa, wor, g, wrt, br, tri, low)


LOW_BITS = 4


def _start_piece(src_hbm, dst_buf, sem, s, d, l8, nbits):
    def bit_copy(c):
        size = SUBLANES << c
        low = (l8 & ((1 << c) - 1)) * SUBLANES

        @pl.when(((l8 >> c) & 1) == 1)
        def _():
            pltpu.make_async_copy(
                src_hbm.at[pl.ds(pl.multiple_of(s + low, SUBLANES), size)],
                dst_buf.at[pl.ds(pl.multiple_of(d + low, SUBLANES), size)], sem).start()

    for c in range(min(LOW_BITS, nbits)):
        bit_copy(c)
    if nbits > LOW_BITS:
        def long_copies(_, carry):
            for c in range(LOW_BITS, nbits):
                bit_copy(c)
            return carry

        lax.fori_loop(0, jnp.where(l8 >= (1 << LOW_BITS), 1, 0), long_copies, 0)


def _start_pieces(src_hbm, dst_buf, sem, p_lo, p_hi, psrc_ref, pdst_ref, plen_ref, nbits):
    def body(p, carry):
        _start_piece(src_hbm, dst_buf, sem, psrc_ref[p], pdst_ref[p], plen_ref[p], nbits)
        return carry

    lax.fori_loop(p_lo, p_hi, body, 0)


def _wait_rows(src_hbm, dst_buf, sem, rows8, nbits):
    for c in range(nbits):
        size = SUBLANES << c

        @pl.when(((rows8 >> c) & 1) == 1)
        def _():
            pltpu.make_async_copy(src_hbm.at[pl.ds(0, size)], dst_buf.at[pl.ds(0, size)], sem).wait()


def _moe_gmm_kernel(be_ref, rows_ref, wslot_ref, nxt_ref,
                    psa_ref, pea_ref, srca_ref, dsta_ref, lena_ref, hsrca_ref, hlena_ref,
                    psb_ref, peb_ref, srcb_ref, dstb_ref, lenb_ref, hsrcb_ref, hlenb_ref,
                    tsa_hbm, tsb_hbm, wgu_hbm, wdn_hbm, bg_ref, bu_ref, bd_ref, perm_ref,
                    ys_ref, lhs_ref, wgu_buf, wdn_buf, wg_ref, wu_ref, wd_ref, sem_ref, wsem_ref,
                    *, nbits_a, nbits_b):
    j = pl.program_id(0)
    nb = pl.num_programs(0)
    slot = j % 2
    bm = lhs_ref.shape[1]

    def gather(blk, sl):
        for ts_hbm, ps, pe, src, dst, ln, hsrc, hlen, nbits in (
                (tsa_hbm, psa_ref, pea_ref, srca_ref, dsta_ref, lena_ref, hsrca_ref, hlena_ref, nbits_a),
                (tsb_hbm, psb_ref, peb_ref, srcb_ref, dstb_ref, lenb_ref, hsrcb_ref, hlenb_ref, nbits_b)):
            _start_pieces(ts_hbm, lhs_ref.at[sl], sem_ref.at[sl], ps[blk], pe[blk], src, dst, ln, nbits)
            _start_piece(ts_hbm, lhs_ref.at[sl], sem_ref.at[sl], hsrc[blk], 0, hlen[blk], nbits)

    def weight_copies(e, ws):
        return (pltpu.make_async_copy(wgu_hbm.at[e], wgu_buf.at[ws], wsem_ref.at[ws]),
                pltpu.make_async_copy(wdn_hbm.at[e], wdn_buf.at[ws], wsem_ref.at[ws]))

    @pl.when(j == 0)
    def _():
        lhs_ref[...] = jnp.zeros(lhs_ref.shape, U32)
        gather(0, 0)
        for cp in weight_copies(be_ref[0], wslot_ref[0]):
            cp.start()

    @pl.when(j + 1 < nb)
    def _():
        gather(j + 1, 1 - slot)

    @pl.when(jnp.logical_or(j == 0, be_ref[j] != be_ref[jnp.maximum(j - 1, 0)]))
    def _():
        ws = wslot_ref[j]
        for cp in weight_copies(be_ref[j], ws):
            cp.wait()
        nxt = nxt_ref[j]

        @pl.when(nxt >= 0)
        def _():
            for cp in weight_copies(nxt, 1 - ws):
                cp.start()

        perm = perm_ref[...]
        half = PERM_COLS // 2
        for c in range(2 * D_FF // PERM_COLS):
            wb = wgu_buf[ws, :, c * PERM_COLS:(c + 1) * PERM_COLS].astype(BF16)
            wp = jnp.dot(wb, perm, preferred_element_type=F32).astype(BF16)
            wg_ref[:, c * half:(c + 1) * half] = wp[:, :half]
            wu_ref[:, c * half:(c + 1) * half] = wp[:, half:]
        wd_ref[...] = wdn_buf[ws].astype(BF16)

    rows8 = rows_ref[j]
    _wait_rows(tsa_hbm, lhs_ref.at[slot], sem_ref.at[slot], rows8, (bm // SUBLANES).bit_length())

    def expert_rows(rs):
        x = _unpack_bf16_pairs(lhs_ref[slot, rs, :])
        gate = jnp.dot(x, wg_ref[...], preferred_element_type=F32) + bg_ref[0]
        up = jnp.dot(x, wu_ref[...], preferred_element_type=F32) + bu_ref[0]
        gate = jnp.minimum(gate, SWIGLU_LIMIT)
        up = jnp.clip(up, -SWIGLU_LIMIT, SWIGLU_LIMIT)
        act = (up + 1.0) * (gate * jax.nn.sigmoid(SWIGLU_ALPHA * gate))
        y = jnp.dot(act.astype(BF16), wd_ref[...], preferred_element_type=F32) + bd_ref[0]
        ys_ref[rs, :] = _pack_bf16_pairs(y.astype(BF16).astype(F32))

    quarter = bm // 4
    for nq in range(5):
        lo8, hi8 = (nq - 1) * quarter // SUBLANES, nq * quarter // SUBLANES

        @pl.when(jnp.logical_and(rows8 > lo8, rows8 <= hi8) if nq else rows8 == 0)
        def _():
            if nq:
                expert_rows(slice(0, nq * quarter))
            if nq < 4:
                ys_ref[nq * quarter:, :] = jnp.zeros((bm - nq * quarter, HALF), U32)


def _moe_gmm(blocks, tabs_a, tabs_b, ts_a, ts_b, w_gu, w_dn, bg, bu, bd, perm, nblocks, bm, nbits_a, nbits_b):
    we = lambda j, be, *_: (be[j], 0, 0)
    grid_spec = pltpu.PrefetchScalarGridSpec(
        num_scalar_prefetch=18,
        grid=(nblocks,),
        in_specs=[pl.BlockSpec(memory_space=pl.ANY), pl.BlockSpec(memory_space=pl.ANY),
                  pl.BlockSpec(memory_space=pl.ANY), pl.BlockSpec(memory_space=pl.ANY),
                  pl.BlockSpec((1, 1, D_FF), we), pl.BlockSpec((1, 1, D_FF), we),
                  pl.BlockSpec((1, 1, D_MODEL), we),
                  pl.BlockSpec((PERM_COLS, PERM_COLS), lambda j, *_: (0, 0))],
        out_specs=pl.BlockSpec((bm, HALF), lambda j, *_: (j, 0)),
        scratch_shapes=[pltpu.VMEM((2, bm, HALF), U32),
                        pltpu.VMEM((2, D_MODEL, 2 * D_FF), F32), pltpu.VMEM((2, D_FF, D_MODEL), F32),
                        pltpu.VMEM((D_MODEL, D_FF), BF16), pltpu.VMEM((D_MODEL, D_FF), BF16),
                        pltpu.VMEM((D_FF, D_MODEL), BF16),
                        pltpu.SemaphoreType.DMA((2,)), pltpu.SemaphoreType.DMA((2,))],
    )
    return pl.pallas_call(
        functools.partial(_moe_gmm_kernel, nbits_a=nbits_a, nbits_b=nbits_b),
        grid_spec=grid_spec,
        out_shape=jax.ShapeDtypeStruct((nblocks * bm, HALF), U32),
        compiler_params=_cparams(1),
        name="moe_gmm",
    )(*blocks, *tabs_a, *tabs_b, ts_a, ts_b, w_gu, w_dn, bg, bu, bd, perm)


def _combine_kernel(psrc_ref, pdst_ref, plen_ref, tlo_ref, thi_ref, tsrc_ref, tdst_ref, tlen_ref, rows_ref,
                    ys_hbm, x2_ref, dest_ref, gate_ref, o_ref,
                    buf_ref, db_ref, gb_ref, sem_ref, *, nbits):
    i = pl.program_id(0)
    n = pl.num_programs(0)
    slot = i % 2
    tt = x2_ref.shape[0]
    tile_rows = buf_ref.shape[1]

    def gather(tile, sl):
        _start_pieces(ys_hbm, buf_ref.at[sl], sem_ref.at[sl], tile * N_EXPERTS, (tile + 1) * N_EXPERTS,
                      psrc_ref, pdst_ref, plen_ref, nbits)
        _start_pieces(ys_hbm, buf_ref.at[sl], sem_ref.at[sl], tlo_ref[tile], thi_ref[tile],
                      tsrc_ref, tdst_ref, tlen_ref, nbits)

    @pl.when(i == 0)
    def _():
        buf_ref[...] = jnp.zeros(buf_ref.shape, U32)
        gather(0, 0)

    @pl.when(i + 1 < n)
    def _():
        gather(i + 1, 1 - slot)

    _wait_rows(ys_hbm, buf_ref.at[slot], sem_ref.at[slot], rows_ref[i], (tile_rows // SUBLANES).bit_length())

    rows = jnp.concatenate([dest_ref[0].astype(F32), gate_ref[0],
                            jnp.zeros((LANES - 2 * TOP_K, tt), F32)], axis=0)
    cols = rows.T
    for k in range(TOP_K):
        db_ref[k] = jnp.broadcast_to(cols[:, k:k + 1], (tt, DISPATCH_CHUNK)).astype(I32).astype(I16)
        gb_ref[k] = jnp.broadcast_to(cols[:, TOP_K + k:TOP_K + k + 1], (tt, DISPATCH_CHUNK)).astype(BF16)
    li = lax.broadcasted_iota(I32, (tt, DISPATCH_CHUNK), 1).astype(I16)
    gms = []
    for c in range(tile_rows // DISPATCH_CHUNK):
        lic = li + jnp.int16(c * DISPATCH_CHUNK)
        gm = jnp.zeros((tt, DISPATCH_CHUNK), BF16)
        for k in range(TOP_K):
            gm = jnp.where(lic == db_ref[k], gb_ref[k], gm)
        gms.append(gm)
    o_ref[...] = x2_ref[...] + jnp.dot(jnp.concatenate(gms, axis=1), _unpack_bf16_pairs(buf_ref[slot]),
                                       preferred_element_type=F32)


def _combine(tabs, ys, x2, dest, gate, tt, nbits):
    n = x2.shape[0]
    nt = n // tt
    tile_rows = _tile_rows(tt)
    grid_spec = pltpu.PrefetchScalarGridSpec(
        num_scalar_prefetch=len(tabs),
        grid=(nt,),
        in_specs=[pl.BlockSpec(memory_space=pl.ANY),
                  pl.BlockSpec((tt, D_MODEL), lambda i, *_: (i, 0)),
                  pl.BlockSpec((1, TOP_K, tt), lambda i, *_: (i, 0, 0)),
                  pl.BlockSpec((1, TOP_K, tt), lambda i, *_: (i, 0, 0))],
        out_specs=pl.BlockSpec((tt, D_MODEL), lambda i, *_: (i, 0)),
        scratch_shapes=[pltpu.VMEM((2, tile_rows, HALF), U32),
                        pltpu.VMEM((TOP_K, tt, DISPATCH_CHUNK), I16), pltpu.VMEM((TOP_K, tt, DISPATCH_CHUNK), BF16),
                        pltpu.SemaphoreType.DMA((2,))],
    )
    return pl.pallas_call(
        functools.partial(_combine_kernel, nbits=nbits),
        grid_spec=grid_spec,
        out_shape=jax.ShapeDtypeStruct((n, D_MODEL), F32),
        compiler_params=_cparams(1),
        name="combine",
    )(*tabs, ys, x2, dest, gate)


def _piece_tables(n8_a, off_a, rows_a, n8_b, off_b, bm, nblocks):
    nta = n8_a.shape[0]
    n8 = jnp.concatenate([n8_a, n8_b], axis=0)
    seg_off = jnp.concatenate([off_a, off_b], axis=0)
    n_tiles = n8.shape[0]
    tile_base = jnp.concatenate([jnp.arange(nta, dtype=I32) * rows_a, jnp.zeros((n_tiles - nta,), I32)])[:, None]
    tot = jnp.sum(n8, axis=0)
    pos0 = jnp.cumsum(n8, axis=0) - n8
    nblk = (tot + bm - 1) // bm
    cs = jnp.cumsum(nblk)
    bs = cs - nblk
    kblk = pos0 // bm
    len0 = jnp.minimum(n8, (kblk + 1) * bm - pos0)
    len1 = n8 - len0
    b0 = bs[None, :] + kblk
    src0 = tile_base + seg_off
    in_blk = pos0 - kblk * bm
    jj = jnp.arange(nblocks, dtype=I32)
    i32 = lambda v: v.astype(I32)

    def gmm_tabs(sl):
        em = lambda v: v[sl].T.reshape(-1)
        blk_em = em(b0)
        first = i32(jnp.sum(blk_em[None, :] < jj[:, None], axis=1))
        last = i32(jnp.sum(blk_em[None, :] <= jj[:, None], axis=1))
        hit = (blk_em[None, :] + 1 == jj[:, None]) & (em(len1)[None, :] > 0)
        tail_src = i32(jnp.sum(jnp.where(hit, em(src0 + len0)[None, :], 0), axis=1))
        tail_len = i32(jnp.sum(jnp.where(hit, em(len1)[None, :], 0), axis=1) // SUBLANES)
        return first, last, i32(em(src0)), i32(em(in_blk)), i32(em(len0) // SUBLANES), tail_src, tail_len

    def comb_tabs(sl):
        tm = lambda v: v[sl].reshape(-1)
        has_tail = len1[sl] > 0
        cnt = jnp.sum(has_tail, axis=1)
        lo = jnp.cumsum(cnt) - cnt
        slot_ = lo[:, None] + jnp.cumsum(has_tail, axis=1) - has_tail
        hit = (slot_.reshape(-1)[None, :] == jj[:, None]) & has_tail.reshape(-1)[None, :]
        pick = lambda v: i32(jnp.sum(jnp.where(hit, tm(v)[None, :], 0), axis=1))
        return (i32(tm(b0 * bm + in_blk)), i32(tm(seg_off)), i32(tm(len0) // SUBLANES),
                i32(lo), i32(lo + cnt), pick((b0 + 1) * bm), pick(seg_off + len0), pick(len1 // SUBLANES),
                i32(jnp.sum(n8[sl], axis=1) // SUBLANES))

    count_le = lambda v: jnp.sum(cs[None, :] <= v[:, None], axis=1)
    n_active = cs[-1]
    e_last = count_le(jnp.maximum(n_active - 1, 0).reshape(1))[0]
    block_e = jnp.minimum(count_le(jj), e_last).astype(I32)
    ee = jnp.arange(N_EXPERTS, dtype=I32)
    mine = (jj[:, None] >= bs[None, :]) & (jj[:, None] < cs[None, :])
    left = jnp.clip(tot[None, :] - (jj[:, None] - bs[None, :]) * bm, 0, bm)
    rows8 = (jnp.sum(jnp.where(mine, left, 0), axis=1) // SUBLANES).astype(I32)
    has = nblk > 0
    run = jnp.cumsum(has.astype(I32)) - 1
    later = (ee[None, :] > ee[:, None]) & has[None, :]
    nxt_e = jnp.min(jnp.where(later, ee[None, :], N_EXPERTS), axis=1)
    nxt_e = jnp.where(nxt_e == N_EXPERTS, -1, nxt_e)
    own = block_e[:, None] == ee[None, :]
    wslot = (jnp.sum(jnp.where(own, run[None, :], 0), axis=1) % 2).astype(I32)
    nxt = jnp.sum(jnp.where(own, nxt_e[None, :], 0), axis=1).astype(I32)
    a, b = slice(0, nta), slice(nta, n_tiles)
    return (block_e, rows8, wslot, nxt), gmm_tabs(a), gmm_tabs(b), comb_tabs(a), comb_tabs(b)


def _block_diag(w):
    nb, bi, bo = w.shape
    eye = jnp.eye(nb, dtype=w.dtype)
    return (eye[:, None, :, None] * w[:, :, None, :]).reshape(nb * bi, nb * bo)


def _step(x_prompt, x_sample, cache_k, cache_v, state_conv, state_h, g_mix_norm, w_in, g_q_norm, g_k_norm,
          attn_sinks, conv_w, conv_b, w_lru_a, b_lru_a, w_lru_x, b_lru_x, lru_lambda, g_attn_out, g_rnn_out,
          w_out, g_ffn_norm, w_router, b_router, w_gate_up, b_gate_up, w_down, b_down,
          *, tm, tt, bm, past_len):
    B, S, D = x_prompt.shape
    NS = x_sample.shape[0]
    assert x_sample.shape[1] == 1 and D == D_MODEL
    assert (B * S) % tt == 0 and S % tm == 0 and tm % ATTN_BLOCK == 0
    assert NS % SUBLANES == 0 and tt <= bm
    assert tt % SUBLANES == 0 and NS <= bm
    n_pt = (B * S) // tt
    total_rows = TOP_K * (B * S + NS) + (n_pt + 1) * N_EXPERTS * (SUBLANES - 1)
    nblocks = -(-total_rows // bm) + N_EXPERTS
    nbits_p = (tt // SUBLANES).bit_length()
    nbits_s = (NS // SUBLANES).bit_length()

    l = 0
    row = lambda v: v[l].reshape(1, -1)
    w_in_bf = w_in[l].astype(BF16)
    gq2 = jnp.tile(g_q_norm[l], 2).reshape(1, LANES)
    gk2 = jnp.tile(g_k_norm[l], 2).reshape(1, LANES)
    wa = _block_diag(w_lru_a[l]).astype(BF16)
    wx = _block_diag(w_lru_x[l]).astype(BF16)
    ba = b_lru_a[l].reshape(1, D_RNN)
    bx = b_lru_x[l].reshape(1, D_RNN)
    wo = w_out[l].astype(BF16)
    woa, wor = wo[:D_ATTN], wo[D_ATTN:]
    wr = w_router[l].T
    wr_hi = wr.astype(BF16)
    wrt = jnp.concatenate([wr_hi, (wr - wr_hi.astype(F32)).astype(BF16)], axis=0)
    br = b_router[l].reshape(N_EXPERTS, 1)
    low = jnp.tril(jnp.ones((N_EXPERTS, N_EXPERTS), BF16), k=-1)
    bgu = b_gate_up[l].reshape(N_EXPERTS, D_FF, 2)
    bg = bgu[:, :, 0].reshape(N_EXPERTS, 1, D_FF)
    bu = bgu[:, :, 1].reshape(N_EXPERTS, 1, D_FF)
    bd = b_down[l].reshape(N_EXPERTS, 1, D_MODEL)
    half = PERM_COLS // 2
    pr = jnp.arange(PERM_COLS)
    perm = (pr[None, :] == jnp.where(pr % 2 == 0, pr // 2, half + pr // 2)[:, None]).astype(BF16)
    sinks = attn_sinks[l]

    ctab, s1tab, s2tab = _rope_tables(jnp.arange(S))
    an, rn, kt_p, vt_p, h_last_p, xr_tail = _front(
        x_prompt.reshape(B * S, D), sinks, row(g_mix_norm), w_in_bf, gq2, gk2, ctab, s1tab, s2tab,
        row(g_attn_out), conv_w[l], row(conv_b), wa, ba, wx, bx, row(lru_lambda), row(g_rnn_out), B, S, tm)
    x2_p, ts_p, dest_p, gate_p, n8_p, off_p = _mix_route(
        x_prompt.reshape(B * S, D), an, rn, woa, wor, row(g_ffn_norm), wrt, br, low, tt)

    cs_tab = _rope_tables(jnp.full((NS,), past_len, I32))
    q_s, k_s, v_s, xr_s, yr_s = _in_proj(x_sample.reshape(NS, D), row(g_mix_norm), w_in_bf, gq2, gk2,
                                         *cs_tab, NS)
    to_rows = lambda c: jnp.transpose(c, (0, 2, 3, 1)).reshape(NS * KV_W, WINDOW)
    from_rows = lambda c, n: jnp.transpose(c.reshape(n, N_KV_HEADS, HEAD_DIM, WINDOW), (0, 3, 1, 2))[None]
    an_s, kt_s, vt_s = _attn_sample(q_s, k_s, v_s, to_rows(cache_k[l]), to_rows(cache_v[l]), sinks,
                                    row(g_attn_out), min(NS, SAMPLE_ATTN_SEQS))
    rn_s, h_last_s, hist_s = _rnn_sample(xr_s, yr_s, jnp.transpose(state_conv[l], (1, 0, 2)), state_h[l],
                                         conv_w[l], row(conv_b), wa, ba, wx, bx, row(lru_lambda),
                                         row(g_rnn_out))
    x2_s, ts_s, dest_s, gate_s, n8_s, off_s = _mix_route(
        x_sample.reshape(NS, D), an_s, rn_s, woa, wor, row(g_ffn_norm), wrt, br, low, NS)

    blocks, gmm_p, gmm_s, comb_p, comb_s = _piece_tables(
        n8_p[:, :, 0], off_p[:, :, 0], _tile_rows(tt), n8_s[:, :, 0], off_s[:, :, 0], bm, nblocks)
    ys = _moe_gmm(blocks, gmm_p, gmm_s, ts_p, ts_s, w_gate_up[l], w_down[l], bg, bu, bd, perm,
                  nblocks, bm, nbits_p, nbits_s)
    y_p = _combine(comb_p, ys, x2_p, dest_p, gate_p, tt, nbits_p)
    y_s = _combine(comb_s, ys, x2_s, dest_s, gate_s, NS, nbits_s)

    cp = xr_tail[:, SUBLANES - (CONV_WIDTH - 1):]
    return (y_p.reshape(B, S, D), y_s.reshape(NS, 1, D),
            from_rows(kt_p, B), from_rows(vt_p, B), cp[None], h_last_p.reshape(1, B, D_RNN),
            from_rows(kt_s, NS), from_rows(vt_s, NS), jnp.transpose(hist_s, (1, 0, 2))[None], h_last_s[None])


def kernel(x_prompt, x_sample, cache_k, cache_v, state_conv, state_h, g_mix_norm, w_in, g_q_norm, g_k_norm, attn_sinks, conv_w, conv_b, w_lru_a, b_lru_a, w_lru_x, b_lru_x, lru_lambda, g_attn_out, g_rnn_out, w_out, g_ffn_norm, w_router, b_router, w_gate_up, b_gate_up, w_down, b_down):
    return _step(x_prompt, x_sample, cache_k, cache_v, state_conv, state_h, g_mix_norm, w_in, g_q_norm,
                 g_k_norm, attn_sinks, conv_w, conv_b, w_lru_a, b_lru_a, w_lru_x, b_lru_x, lru_lambda,
                 g_attn_out, g_rnn_out, w_out, g_ffn_norm, w_router, b_router, w_gate_up, b_gate_up,
                 w_down, b_down, tm=512, tt=512, bm=MOE_BLOCK_ROWS, past_len=PAST_LEN)
```

```python
import functools

import jax
import jax.numpy as jnp
from jax import lax
from jax.experimental import pallas as pl
from jax.experimental.pallas import tpu as pltpu

F32 = jnp.float32
BF16 = jnp.bfloat16
I32 = jnp.int32
I16 = jnp.int16

D_MODEL = 1024
HEAD_DIM = 64
N_HEADS = 8
N_KV_HEADS = 2
GROUP = 4
WINDOW = 128
ATTN_BLOCK = 128
ROT_DIM = 16
ROPE_THETA = 500000.0
D_ATTN = 512
D_RNN = 512
KV_W = 128
D_IN = 1792
CONV_WIDTH = 4
LRU_C = 8.0
N_EXPERTS = 32
TOP_K = 4
D_FF = 1024
SWIGLU_LIMIT = 7.0
SWIGLU_ALPHA = 1.702
EPS = 1e-6
PAST_LEN = 8192

LANES = 128
SUBLANES = 8
NEG_BIG = -1e30
VMEM_LIMIT = 56 * 1024 * 1024

MOE_BLOCK_ROWS = 512
PERM_COLS = 256
DISPATCH_CHUNK = 256
SAMPLE_ATTN_SEQS = 32


def _cparams(n_axes):
    return pltpu.CompilerParams(dimension_semantics=("arbitrary",) * n_axes,
                                vmem_limit_bytes=VMEM_LIMIT)


U32 = jnp.uint32
HALF = D_MODEL // 2
HI16 = 0xFFFF0000


def _pack_bf16_pairs(x):
    bits = pltpu.bitcast(x, U32)
    return (bits[:, HALF:] & U32(HI16)) | (bits[:, :HALF] >> 16)


def _unpack_bf16_pairs(u):
    lo = pltpu.bitcast(u << 16, F32)
    hi = pltpu.bitcast(u & U32(HI16), F32)
    return jnp.concatenate([lo, hi], axis=1).astype(BF16)


def _rmsnorm(x, g):
    ms = jnp.mean(x * x, axis=-1, keepdims=True)
    return (x * lax.rsqrt(ms + EPS)) * g


def _head_norm_rope(t, g, c, s1, s2, lo):
    sq = t * t
    s_lo = jnp.sum(jnp.where(lo, sq, 0.0), axis=-1, keepdims=True)
    s_hi = jnp.sum(jnp.where(lo, 0.0, sq), axis=-1, keepdims=True)
    ms = jnp.where(lo, s_lo, s_hi) * (1.0 / HEAD_DIM)
    n = (t * lax.rsqrt(ms + EPS)) * g
    up = pltpu.roll(n, LANES - ROT_DIM // 2, 1)
    dn = pltpu.roll(n, ROT_DIM // 2, 1)
    return n * c + up * s1 + dn * s2


def _in_proj_kernel(x_ref, g_ref, w_ref, gq_ref, gk_ref, c_ref, s1_ref, s2_ref,
                    q_ref, k_ref, v_ref, xr_ref, yr_ref):
    tm = x_ref.shape[0]
    h = _rmsnorm(x_ref[...], g_ref[...])
    proj = jnp.dot(h.astype(BF16), w_ref[...], preferred_element_type=F32)
    rope = (c_ref[...], s1_ref[...], s2_ref[...], lax.broadcasted_iota(I32, (tm, LANES), 1) < HEAD_DIM)
    gq = gq_ref[...]
    for j in range(D_ATTN // LANES):
        q_ref[:, j * LANES:(j + 1) * LANES] = _head_norm_rope(proj[:, j * LANES:(j + 1) * LANES], gq, *rope)
    k_ref[...] = _head_norm_rope(proj[:, D_ATTN:D_ATTN + KV_W], gk_ref[...], *rope)
    v_ref[...] = proj[:, D_ATTN + KV_W:D_ATTN + 2 * KV_W]
    o = D_ATTN + 2 * KV_W
    xr_ref[...] = proj[:, o:o + D_RNN]
    yr_ref[...] = proj[:, o + D_RNN:o + 2 * D_RNN]


def _in_proj(x2d, g, w_bf, gq2, gk2, ctab, s1tab, s2tab, tm):
    n = x2d.shape[0]
    ntab = ctab.shape[0] // tm
    row = lambda i: (i, 0)
    fix = lambda i: (0, 0)
    tab = lambda i: (i % ntab, 0)
    out_shapes = (jax.ShapeDtypeStruct((n, D_ATTN), F32), jax.ShapeDtypeStruct((n, KV_W), F32),
                  jax.ShapeDtypeStruct((n, KV_W), F32), jax.ShapeDtypeStruct((n, D_RNN), F32),
                  jax.ShapeDtypeStruct((n, D_RNN), F32))
    return pl.pallas_call(
        _in_proj_kernel,
        grid=(n // tm,),
        in_specs=[pl.BlockSpec((tm, D_MODEL), row), pl.BlockSpec((1, D_MODEL), fix),
                  pl.BlockSpec((D_MODEL, D_IN), fix), pl.BlockSpec((1, LANES), fix),
                  pl.BlockSpec((1, LANES), fix), pl.BlockSpec((tm, LANES), tab),
                  pl.BlockSpec((tm, LANES), tab), pl.BlockSpec((tm, LANES), tab)],
        out_specs=(pl.BlockSpec((tm, D_ATTN), row), pl.BlockSpec((tm, KV_W), row),
                   pl.BlockSpec((tm, KV_W), row), pl.BlockSpec((tm, D_RNN), row),
                   pl.BlockSpec((tm, D_RNN), row)),
        out_shape=out_shapes,
        compiler_params=_cparams(1),
        name="in_proj",
    )(x2d, g, w_bf, gq2, gk2, ctab, s1tab, s2tab)


def _rope_tables(pos):
    half = ROT_DIM // 2
    inv = ROPE_THETA ** (-jnp.arange(0, ROT_DIM, 2, dtype=F32) / ROT_DIM)
    d = jnp.arange(LANES) % HEAD_DIM
    ang = pos.astype(F32)[:, None] * inv[d % half][None, :]
    cos = jnp.cos(ang)
    sin = jnp.sin(ang)
    c = jnp.where(d < ROT_DIM, cos, 1.0)
    s1 = jnp.where(d < half, -sin, 0.0)
    s2 = jnp.where((d >= half) & (d < ROT_DIM), sin, 0.0)
    return c, s1, s2


def _band_bias(qb):
    qi = jnp.arange(qb, dtype=I32)[:, None]
    c = jnp.arange(2 * qb, dtype=I32)[None, :]
    band = (c >= qi) & (c <= qi + qb)
    first = band & (c >= qb)
    one = jnp.where(jnp.stack([first, band]), 0.0, NEG_BIG).astype(F32)
    return jnp.concatenate([one, one], axis=2)


def _attn_sample_kernel(sink_ref, q_ref, kn_ref, vn_ref, kt_ref, vt_ref, g_ref,
                        o_ref, nkt_ref, nvt_ref, acc_ref):
    bb = q_ref.shape[0]
    q = q_ref[...] * (HEAD_DIM ** -0.5)
    kn = kn_ref[...]
    vn = vn_ref[...]
    kt = kt_ref[...]
    vt = vt_ref[...]
    col = lax.broadcasted_iota(I32, (bb, bb * KV_W), 1)
    rowb = lax.broadcasted_iota(I32, (bb, bb * KV_W), 0)
    own_seq = (col >> (KV_W.bit_length() - 1)) == rowb
    half_hi = ((col >> (HEAD_DIM.bit_length() - 1)) & 1) == 1
    qbig = []
    for h in range(N_HEADS):
        kv = h // GROUP
        pair = q[:, (h // 2) * LANES:(h // 2 + 1) * LANES]
        if (h % 2) != kv:
            pair = pltpu.roll(pair, HEAD_DIM, 1)
        tiled = jnp.concatenate([pair] * bb, axis=1)
        keep = own_seq & (half_hi if kv == 1 else jnp.logical_not(half_hi))
        qbig.append(jnp.where(keep, tiled, 0.0))
    qbig = jnp.concatenate(qbig, axis=0)
    s = jnp.dot(qbig.astype(BF16), kt.astype(BF16), preferred_element_type=F32)
    qb16 = q.astype(BF16).astype(F32)
    kb16 = kn.astype(BF16).astype(F32)
    s_new, sink = [], []
    for h in range(N_HEADS):
        kv = h // GROUP
        s_new.append(jnp.sum(qb16[:, h * HEAD_DIM:(h + 1) * HEAD_DIM] * kb16[:, kv * HEAD_DIM:(kv + 1) * HEAD_DIM],
                             axis=-1, keepdims=True))
        sink.append(jnp.full((bb, 1), sink_ref[h], F32))
    s_new = jnp.concatenate(s_new, axis=0)
    sink = jnp.concatenate(sink, axis=0)
    m = jnp.maximum(jnp.maximum(jnp.max(s, axis=-1, keepdims=True), s_new), sink)
    e = jnp.exp(s - m)
    e_new = jnp.exp(s_new - m)
    inv = 1.0 / (jnp.sum(e, axis=-1, keepdims=True) + e_new + jnp.exp(sink - m))
    obig = lax.dot_general(e.astype(BF16), vt.astype(BF16), (((1,), (1,)), ((), ())),
                           preferred_element_type=F32)
    for h in range(N_HEADS):
        kv = h // GROUP
        blk = jnp.where(own_seq, obig[h * bb:(h + 1) * bb, :], 0.0)
        fold = blk[:, 0:KV_W]
        for t in range(1, bb):
            fold = fold + blk[:, t * KV_W:(t + 1) * KV_W]
        hs = slice(h * bb, (h + 1) * bb)
        ks = slice(kv * HEAD_DIM, (kv + 1) * HEAD_DIM)
        acc_ref[:, h * HEAD_DIM:(h + 1) * HEAD_DIM] = (fold[:, ks] + e_new[hs] * vn[:, ks]) * inv[hs]
    o_ref[...] = _rmsnorm(acc_ref[...], g_ref[...])

    last = lax.broadcasted_iota(I32, (KV_W, WINDOW), 1) == WINDOW - 1
    for b in range(bb):
        rs = slice(b * KV_W, (b + 1) * KV_W)
        kcol = jnp.broadcast_to(kn[b:b + 1, :], (KV_W, KV_W)).T
        vcol = jnp.broadcast_to(vn[b:b + 1, :], (KV_W, KV_W)).T
        nkt_ref[rs, :] = jnp.where(last, kcol, pltpu.roll(kt[rs, :], WINDOW - 1, 1))
        nvt_ref[rs, :] = jnp.where(last, vcol, pltpu.roll(vt[rs, :], WINDOW - 1, 1))


def _attn_sample(q, kn, vn, kt2d, vt2d, sinks, g_attn, bb):
    n = q.shape[0]
    row = lambda i: (i, 0)
    fix = lambda i: (0, 0)
    cache = pl.BlockSpec((bb * KV_W, WINDOW), row)
    return pl.pallas_call(
        _attn_sample_kernel,
        grid=(n // bb,),
        in_specs=[pl.BlockSpec(memory_space=pltpu.SMEM),
                  pl.BlockSpec((bb, D_ATTN), row), pl.BlockSpec((bb, KV_W), row),
                  pl.BlockSpec((bb, KV_W), row), cache, cache,
                  pl.BlockSpec((1, D_ATTN), fix)],
        out_specs=(pl.BlockSpec((bb, D_ATTN), row), cache, cache),
        out_shape=(jax.ShapeDtypeStruct((n, D_ATTN), F32),
                   jax.ShapeDtypeStruct(kt2d.shape, F32), jax.ShapeDtypeStruct(vt2d.shape, F32)),
        scratch_shapes=[pltpu.VMEM((bb, D_ATTN), F32)],
        compiler_params=_cparams(1),
        name="attn_sample",
    )(sinks, q, kn, vn, kt2d, vt2d, g_attn)


def _softplus(z):
    return jnp.maximum(z, 0.0) + jnp.log1p(jnp.exp(-jnp.abs(z)))


def _lru_gates(xc, wa_ref, ba_ref, wx_ref, bx_ref, lam_ref):
    xb = xc.astype(BF16)
    r = jax.nn.sigmoid(jnp.dot(xb, wa_ref[...], preferred_element_type=F32) + ba_ref[...])
    i = jax.nn.sigmoid(jnp.dot(xb, wx_ref[...], preferred_element_type=F32) + bx_ref[...])
    log_a = (-LRU_C * r) * _softplus(-lam_ref[...])
    a = jnp.exp(log_a)
    z = -jnp.tanh(log_a) * (a * a + 1.0)
    u = jnp.where(z > 0.0, z * lax.rsqrt(z), 0.0) * (i * xc)
    return a, u


def _lru_scan(a, u, h0):
    ng = a.shape[0] // SUBLANES
    a3 = a.reshape(ng, SUBLANES, D_RNN)
    u3 = u.reshape(ng, SUBLANES, D_RNN)
    t8 = lax.broadcasted_iota(I32, (ng, SUBLANES, D_RNN), 1)
    d = 1
    while d < SUBLANES:
        a_s = jnp.where(t8 >= d, pltpu.roll(a3, d, 1), 1.0)
        u_s = jnp.where(t8 >= d, pltpu.roll(u3, d, 1), 0.0)
        u3 = a3 * u_s + u3
        a3 = a3 * a_s
        d *= 2
    carry = h0
    groups = []
    for g in range(ng):
        hg = a3[g] * carry + u3[g]
        groups.append(hg)
        carry = hg[SUBLANES - 1:SUBLANES, :]
    return jnp.concatenate(groups, axis=0), carry


def _lru_scan_tiles(a_ref, u_ref, h_ref, h0):
    nl, rows, _ = a_ref.shape
    ng = rows // SUBLANES
    step = lambda ref, s: jnp.concatenate(
        [ref[j, pl.ds(s, ng, stride=SUBLANES), :] for j in range(nl)], axis=1)
    prods = [step(a_ref, 0)]
    locs = [step(u_ref, 0)]
    for s in range(1, SUBLANES):
        a_s = step(a_ref, s)
        locs.append(a_s * locs[-1] + step(u_ref, s))
        prods.append(a_s * prods[-1])
    after, h_last = _lru_scan(prods[-1], locs[-1], h0)
    row = lax.broadcasted_iota(I32, (ng, D_RNN), 0)
    before = jnp.where(row == 0, h0, pltpu.roll(after, 1, 0))
    for s in range(SUBLANES):
        h_s = locs[s] + prods[s] * before
        for j in range(nl):
            h_ref[j, pl.ds(s, ng, stride=SUBLANES), :] = h_s[:, j * LANES:(j + 1) * LANES]
    return h_last


def _to_lane_tiles(ref, x):
    for j in range(ref.shape[0]):
        ref[j] = x[:, j * LANES:(j + 1) * LANES]


def _rnn_sample_kernel(xr_ref, yr_ref, hist_ref, h0_ref, cw_ref, cb_ref, wa_ref, ba_ref, wx_ref, bx_ref,
                       lam_ref, g_ref, o_ref, hl_ref, nh_ref):
    cw = cw_ref[...]
    xr = xr_ref[...]
    xc = cb_ref[...] + xr * cw[CONV_WIDTH - 1:CONV_WIDTH, :]
    for w in range(CONV_WIDTH - 1):
        xc = xc + hist_ref[w] * cw[w:w + 1, :]
    a, u = _lru_gates(xc, wa_ref, ba_ref, wx_ref, bx_ref, lam_ref)
    h = a * h0_ref[...] + u
    hl_ref[...] = h
    o_ref[...] = _rmsnorm(jax.nn.gelu(yr_ref[...]) * h, g_ref[...])
    for w in range(CONV_WIDTH - 2):
        nh_ref[w] = hist_ref[w + 1]
    nh_ref[CONV_WIDTH - 2] = xr


def _rnn_sample(xr, yr, hist, h0, cw, cb, wa, ba, wx, bx, lam, g):
    n = xr.shape[0]
    full = lambda a: pl.BlockSpec(a.shape, lambda: (0,) * a.ndim)
    args = (xr, yr, hist, h0, cw, cb, wa, ba, wx, bx, lam, g)
    return pl.pallas_call(
        _rnn_sample_kernel,
        in_specs=[full(a) for a in args],
        out_specs=(pl.BlockSpec((n, D_RNN), lambda: (0, 0)), pl.BlockSpec((n, D_RNN), lambda: (0, 0)),
                   pl.BlockSpec(hist.shape, lambda: (0, 0, 0))),
        out_shape=(jax.ShapeDtypeStruct((n, D_RNN), F32), jax.ShapeDtypeStruct((n, D_RNN), F32),
                   jax.ShapeDtypeStruct(hist.shape, F32)),
        compiler_params=pltpu.CompilerParams(vmem_limit_bytes=VMEM_LIMIT),
        name="rnn_sample",
    )(*args)


def _front_kernel(*refs, tiles_per_seq, nt):
    i = pl.program_id(0)
    q_s, k_s, v_s, xr_s, yr_s, ext_ref, h_ref = refs[25:32]

    @pl.when(i == 0)
    def _():
        for r in (q_s, k_s, v_s, xr_s, yr_s, ext_ref, h_ref):
            r[...] = jnp.zeros(r.shape, F32)
        _front_body(0, 1, *refs, tiles_per_seq=tiles_per_seq, tail=False)

    @pl.when(i == nt)
    def _():
        _front_body(nt % 2, 1 - nt % 2, *refs, tiles_per_seq=tiles_per_seq, proj=False)

    for cur in range(2):
        @pl.when((i % 2 == cur) & (i > 0) & (i < nt))
        def _():
            _front_body(cur, 1 - cur, *refs, tiles_per_seq=tiles_per_seq)


def _front_body(cur, prv, sink_ref, x_ref, gm_ref, w_ref, gq_ref, gk_ref, c_ref, s1_ref, s2_ref, bias_ref,
                ga_ref, cw_ref, cb_ref, wa_ref, ba_ref, wx_ref, bx_ref, lam_ref, gr_ref,
                an_ref, rn_ref, kt_ref, vt_ref, hl_ref, cx_ref,
                q_s, k_s, v_s, xr_s, yr_s, ext_ref, h_ref, s_ref, e_ref, a_scr, u_scr, hs_scr,
                *, tiles_per_seq, proj=True, tail=True):
    i = pl.program_id(0)
    tm = x_ref.shape[0]
    qb = ATTN_BLOCK
    t = jnp.maximum(i - 1, 0)
    first_tile = (t % tiles_per_seq) == 0

    if proj:
        hx = _rmsnorm(x_ref[...], gm_ref[...]).astype(BF16)
        rope = (c_ref[...], s1_ref[...], s2_ref[...], lax.broadcasted_iota(I32, (tm, LANES), 1) < HEAD_DIM)
        project = lambda c0, c1: jnp.dot(hx, w_ref[:, c0:c1], preferred_element_type=F32)

    nqb = tm // qb
    n_pairs = N_HEADS // 2
    lo_k = lax.broadcasted_iota(I32, (2 * qb, LANES), 1) < HEAD_DIM
    lo_q = lax.broadcasted_iota(I32, (qb, LANES), 1) < HEAD_DIM
    nt_dims = (((1,), (1,)), ((), ()))
    vbds = []
    for jb in range(nqb if tail else 0):
        k2 = k_s[prv, jb * qb:(jb + 2) * qb, :]
        v2 = v_s[prv, jb * qb:(jb + 2) * qb, :]
        k2r = pltpu.roll(k2, HEAD_DIM, 1)
        v2r = pltpu.roll(v2, HEAD_DIM, 1)
        bias = bias_ref[jnp.where(first_tile, 0, 1)] if jb == 0 else bias_ref[1]
        for kv in range(N_KV_HEADS):
            ka, kb = (k2, k2r) if kv == 0 else (k2r, k2)
            va, vb = (v2, v2r) if kv == 0 else (v2r, v2)
            kbd = jnp.concatenate([jnp.where(lo_k, ka, 0.0), jnp.where(lo_k, 0.0, kb)], axis=0).astype(BF16)
            vbds.append(jnp.concatenate([jnp.where(lo_k, va, 0.0), jnp.where(lo_k, 0.0, vb)],
                                        axis=0).astype(BF16))
            for p in range(GROUP // 2):
                pp = kv * (GROUP // 2) + p
                qp = (q_s[prv, jb * qb:(jb + 1) * qb, pp * LANES:(pp + 1) * LANES]
                      * (HEAD_DIM ** -0.5)).astype(BF16)
                s_ref[jb * n_pairs + pp] = lax.dot_general(qp, kbd, nt_dims,
                                                           preferred_element_type=F32) + bias
    if proj:
        pq = project(0, D_ATTN)
        for j in range(D_ATTN // LANES):
            q_s[cur, :, j * LANES:(j + 1) * LANES] = _head_norm_rope(pq[:, j * LANES:(j + 1) * LANES],
                                                                     gq_ref[...], *rope)
    invs = []
    for c in range(nqb * n_pairs if tail else 0):
        pp = c % n_pairs
        inv = []
        for tpos in range(2):
            cols = slice(tpos * 2 * qb, (tpos + 1) * 2 * qb)
            st = s_ref[c, :, cols]
            sink = sink_ref[2 * pp + tpos]
            m = jnp.maximum(jnp.max(st, axis=-1, keepdims=True), sink)
            e = jnp.exp(st - m)
            e_ref[c, :, cols] = e.astype(BF16)
            inv.append(1.0 / (jnp.sum(e, axis=-1, keepdims=True) + jnp.exp(sink - m)))
        invs.append(jnp.where(lo_q, inv[0], inv[1]))
    if proj:
        pkv = project(D_ATTN, D_ATTN + 2 * KV_W)
        k_s[cur, qb:qb + tm, :] = _head_norm_rope(pkv[:, :KV_W], gk_ref[...], *rope)
        v_s[cur, qb:qb + tm, :] = pkv[:, KV_W:]
        k_s[cur, 0:qb, :] = k_s[prv, tm:tm + qb, :]
        v_s[cur, 0:qb, :] = v_s[prv, tm:tm + qb, :]
    for jb in range(nqb if tail else 0):
        outs = [jnp.dot(e_ref[jb * n_pairs + pp], vbds[jb * N_KV_HEADS + pp // (GROUP // 2)],
                        preferred_element_type=F32) * invs[jb * n_pairs + pp] for pp in range(n_pairs)]
        an_ref[jb * qb:(jb + 1) * qb, :] = _rmsnorm(jnp.concatenate(outs, axis=1), ga_ref[...])
    if tail:
        kt_ref[0] = k_s[prv, tm:tm + qb, :].T
        vt_ref[0] = v_s[prv, tm:tm + qb, :].T

    o = D_ATTN + 2 * KV_W
    if proj:
        xr_s[cur] = project(o, o + D_RNN)
    if not tail:
        yr_s[cur] = project(o + D_RNN, o + 2 * D_RNN)
        return

    pad = SUBLANES
    xr = xr_s[prv]
    ext_ref[0:pad, :] = jnp.where(first_tile, 0.0, ext_ref[0:pad, :])
    ext_ref[pad:pad + tm, :] = xr
    cw = cw_ref[...]
    xc = cb_ref[...] + xr * cw[CONV_WIDTH - 1:CONV_WIDTH, :]
    for w in range(CONV_WIDTH - 1):
        sh = CONV_WIDTH - 1 - w
        xc = xc + ext_ref[pad - sh:pad - sh + tm, :] * cw[w:w + 1, :]
    ext_ref[0:pad, :] = xr[tm - pad:tm, :]
    cx_ref[0] = xr[tm - pad:tm, :]
    a, u = _lru_gates(xc, wa_ref, ba_ref, wx_ref, bx_ref, lam_ref)
    if proj:
        yr_s[cur] = project(o + D_RNN, o + 2 * D_RNN)
    _to_lane_tiles(a_scr, a)
    _to_lane_tiles(u_scr, u)
    carry = _lru_scan_tiles(a_scr, u_scr, hs_scr, jnp.where(first_tile, 0.0, h_ref[...]))
    h_ref[...] = carry
    hl_ref[0] = carry
    hseq = jnp.concatenate([hs_scr[j] for j in range(hs_scr.shape[0])], axis=1)
    rn_ref[...] = _rmsnorm(jax.nn.gelu(yr_s[prv]) * hseq, gr_ref[...])


def _front(x2d, sinks, g_mix, w_bf, gq2, gk2, ctab, s1tab, s2tab, g_attn,
           cw, cb, wa, ba, wx, bx, lam, g_rnn, batch, seq, tm):
    n = x2d.shape[0]
    nt = n // tm
    tps = seq // tm
    qb = ATTN_BLOCK
    cur = lambda i: (jnp.minimum(i, nt - 1), 0)
    tab = lambda i: (jnp.minimum(i, nt - 1) % tps, 0)
    fix = lambda i: (0, 0)
    prev = lambda i: (jnp.maximum(i - 1, 0), 0)
    per_seq = lambda i: (jnp.maximum(i - 1, 0) // tps, 0, 0)
    vec = lambda w: pl.BlockSpec((1, w), fix)
    return pl.pallas_call(
        functools.partial(_front_kernel, tiles_per_seq=tps, nt=nt),
        grid=(nt + 1,),
        in_specs=[pl.BlockSpec(memory_space=pltpu.SMEM),
                  pl.BlockSpec((tm, D_MODEL), cur), vec(D_MODEL), pl.BlockSpec((D_MODEL, D_IN), fix),
                  vec(LANES), vec(LANES),
                  pl.BlockSpec((tm, LANES), tab), pl.BlockSpec((tm, LANES), tab), pl.BlockSpec((tm, LANES), tab),
                  pl.BlockSpec((2, qb, 4 * qb), lambda i: (0, 0, 0)), vec(D_ATTN),
                  pl.BlockSpec((CONV_WIDTH, D_RNN), fix), vec(D_RNN),
                  pl.BlockSpec((D_RNN, D_RNN), fix), vec(D_RNN),
                  pl.BlockSpec((D_RNN, D_RNN), fix), vec(D_RNN), vec(D_RNN), vec(D_RNN)],
        out_specs=(pl.BlockSpec((tm, D_ATTN), prev), pl.BlockSpec((tm, D_RNN), prev),
                   pl.BlockSpec((1, KV_W, qb), per_seq), pl.BlockSpec((1, KV_W, qb), per_seq),
                   pl.BlockSpec((1, 1, D_RNN), per_seq), pl.BlockSpec((1, SUBLANES, D_RNN), per_seq)),
        out_shape=(jax.ShapeDtypeStruct((n, D_ATTN), F32), jax.ShapeDtypeStruct((n, D_RNN), F32),
                   jax.ShapeDtypeStruct((batch, KV_W, qb), F32), jax.ShapeDtypeStruct((batch, KV_W, qb), F32),
                   jax.ShapeDtypeStruct((batch, 1, D_RNN), F32),
                   jax.ShapeDtypeStruct((batch, SUBLANES, D_RNN), F32)),
        scratch_shapes=[pltpu.VMEM((2, tm, D_ATTN), F32),
                        pltpu.VMEM((2, tm + qb, KV_W), F32), pltpu.VMEM((2, tm + qb, KV_W), F32),
                        pltpu.VMEM((2, tm, D_RNN), F32), pltpu.VMEM((2, tm, D_RNN), F32),
                        pltpu.VMEM((tm + SUBLANES, D_RNN), F32), pltpu.VMEM((1, D_RNN), F32),
                        pltpu.VMEM((tm // qb * (N_HEADS // 2), qb, 4 * qb), F32),
                        pltpu.VMEM((tm // qb * (N_HEADS // 2), qb, 4 * qb), BF16),
                        pltpu.VMEM((D_RNN // LANES, tm, LANES), F32), pltpu.VMEM((D_RNN // LANES, tm, LANES), F32),
                        pltpu.VMEM((D_RNN // LANES, tm, LANES), F32)],
        compiler_params=_cparams(1),
        name="front",
    )(sinks, x2d, g_mix, w_bf, gq2, gk2, ctab, s1tab, s2tab, _band_bias(qb), g_attn,
      cw, cb, wa, ba, wx, bx, lam, g_rnn)


def _mix_route_kernel(*refs, nt):
    i = pl.program_id(0)
    hb_s, d_s = refs[16:18]

    @pl.when(i == 0)
    def _():
        hb_s[...] = jnp.zeros(hb_s.shape, BF16)
        d_s[...] = jnp.zeros(d_s.shape, I32)
        _mix_route_body(0, 1, *refs, disp=False)

    @pl.when(i == nt)
    def _():
        _mix_route_body(nt % 2, 1 - nt % 2, *refs, route=False)

    for cur in range(2):
        @pl.when((i % 2 == cur) & (i > 0) & (i < nt))
        def _():
            _mix_route_body(cur, 1 - cur, *refs)


def _mix_route_body(cur, prv, x_ref, an_ref, rn_ref, woa_ref, wor_ref, g_ref, wr2_ref, br_ref, tri_ref, low_ref,
                    x2_ref, ts_ref, dest_ref, gate_ref, n8_ref, off_ref, hb_s, d_s, *, route=True, disp=True):
    tt = x_ref.shape[0]
    tile_rows = ts_ref.shape[0]
    n_chunks = tile_rows // DISPATCH_CHUNK

    hb_prev = hb_s[prv]
    d16 = [d_s[prv, k:k + 1, :].astype(I16) for k in range(TOP_K)]
    ri = lax.broadcasted_iota(I32, (DISPATCH_CHUNK, tt), 0).astype(I16)
    one = jnp.ones((DISPATCH_CHUNK, tt), BF16)

    def dispatch(chunks):
        for c in (chunks if disp else ()):
            p = jnp.zeros((DISPATCH_CHUNK, tt), BF16)
            for d in d16:
                p = jnp.where(ri == d - jnp.int16(c * DISPATCH_CHUNK), one, p)
            ts_ref[c * DISPATCH_CHUNK:(c + 1) * DISPATCH_CHUNK, :] = _pack_bf16_pairs(
                jnp.dot(p, hb_prev, preferred_element_type=F32))

    third = -(-n_chunks // 3)
    if not route:
        dispatch(range(n_chunks))
        return

    x2 = x_ref[...] + jnp.dot(an_ref[...].astype(BF16), woa_ref[...], preferred_element_type=F32) \
        + jnp.dot(rn_ref[...].astype(BF16), wor_ref[...], preferred_element_type=F32)
    x2_ref[...] = x2
    dispatch(range(0, third))
    hn = _rmsnorm(x2, g_ref[...])

    nt = (((1,), (1,)), ((), ()))
    hb = hn.astype(BF16)
    hb_s[cur] = hb
    hmid = (hn - hb.astype(F32)).astype(BF16)
    wr2 = wr2_ref[...]
    both = lax.dot_general(wr2, hb, nt, preferred_element_type=F32)
    logits = (lax.dot_general(wr2[:N_EXPERTS], hmid, nt, preferred_element_type=F32)
              + both[N_EXPERTS:]) + both[:N_EXPERTS] + br_ref[...]
    dispatch(range(third, 2 * third))

    ie = lax.broadcasted_iota(I32, (N_EXPERTS, tt), 0).astype(F32)
    l = logits
    vals, sels = [], []
    for _ in range(TOP_K):
        m = jnp.max(l, axis=0, keepdims=True)
        idx = jnp.min(jnp.where(l == m, ie, float(N_EXPERTS)), axis=0, keepdims=True)
        sel = ie == idx
        vals.append(m)
        sels.append(sel)
        l = jnp.where(sel, NEG_BIG, l)
    es = [jnp.exp(v - vals[0]) for v in vals]
    den = es[0] + es[1] + es[2] + es[3]
    gate_ref[0] = jnp.concatenate([e / den for e in es], axis=0)
    dispatch(range(2 * third, n_chunks))

    oh = jnp.zeros((N_EXPERTS, tt), F32)
    for sel in sels:
        oh = oh + jnp.where(sel, 1.0, 0.0)
    before = jnp.dot(oh.astype(BF16), tri_ref[...], preferred_element_type=F32)
    cnt = jnp.sum(oh, axis=1, keepdims=True).astype(I32)
    n8 = ((cnt + (SUBLANES - 1)) >> 3) << 3
    n8b = jnp.broadcast_to(n8, (N_EXPERTS, LANES))
    off = jnp.dot(low_ref[...], n8b.astype(F32).astype(BF16), preferred_element_type=F32)
    n8_ref[0] = n8b
    off_ref[0] = off.astype(I32)
    base = off[:, 0:1] + before
    dests = jnp.concatenate(
        [jnp.sum(jnp.where(sel, base, 0.0), axis=0, keepdims=True).astype(I32) for sel in sels], axis=0)
    dest_ref[0] = dests
    d_s[cur, 0:TOP_K, :] = dests


def _tile_rows(tt):
    return -(-(TOP_K * tt + N_EXPERTS * (SUBLANES - 1)) // DISPATCH_CHUNK) * DISPATCH_CHUNK


def _mix_route(x2d, an, rn, woa, wor, g, wrt, br, low, tt):
    n = x2d.shape[0]
    nt = n // tt
    tile_rows = _tile_rows(tt)
    tri = jnp.triu(jnp.ones((tt, tt), BF16), k=1)
    row = lambda i: (jnp.minimum(i, nt - 1), 0)
    prev = lambda i: (jnp.maximum(i - 1, 0), 0)
    fix = lambda i: (0, 0)
    t3 = lambda i: (jnp.minimum(i, nt - 1), 0, 0)
    in_specs = [pl.BlockSpec((tt, D_MODEL), row), pl.BlockSpec((tt, D_ATTN), row),
                pl.BlockSpec((tt, D_RNN), row), pl.BlockSpec((D_ATTN, D_MODEL), fix),
                pl.BlockSpec((D_RNN, D_MODEL), fix), pl.BlockSpec((1, D_MODEL), fix),
                pl.BlockSpec((2 * N_EXPERTS, D_MODEL), fix), pl.BlockSpec((N_EXPERTS, 1), fix),
                pl.BlockSpec((tt, tt), fix), pl.BlockSpec((N_EXPERTS, N_EXPERTS), fix)]
    out_shape = (jax.ShapeDtypeStruct((n, D_MODEL), F32),
                 jax.ShapeDtypeStruct((nt * tile_rows, HALF), U32),
                 jax.ShapeDtypeStruct((nt, TOP_K, tt), I32),
                 jax.ShapeDtypeStruct((nt, TOP_K, tt), F32),
                 jax.ShapeDtypeStruct((nt, N_EXPERTS, LANES), I32),
                 jax.ShapeDtypeStruct((nt, N_EXPERTS, LANES), I32))
    out_specs = (pl.BlockSpec((tt, D_MODEL), row),
                 pl.BlockSpec((tile_rows, HALF), prev),
                 pl.BlockSpec((1, TOP_K, tt), t3), pl.BlockSpec((1, TOP_K, tt), t3),
                 pl.BlockSpec((1, N_EXPERTS, LANES), t3), pl.BlockSpec((1, N_EXPERTS, LANES), t3))
    return pl.pallas_call(
        functools.partial(_mix_route_kernel, nt=nt),
        grid=(nt + 1,),
        in_specs=in_specs,
        out_specs=out_specs,
        out_shape=out_shape,
        scratch_shapes=[pltpu.VMEM((2, tt, D_MODEL), BF16), pltpu.VMEM((2, SUBLANES, tt), I32)],
        compiler_params=_cparams(1),
        name="mix_route",
    )(x2d, an, rn, woa, wor, g, wrt, br, tri, low)


LOW_BITS = 4


def _start_piece(src_hbm, dst_buf, sem, s, d, l8, nbits):
    def bit_copy(c):
        size = SUBLANES << c
        low = (l8 & ((1 << c) - 1)) * SUBLANES

        @pl.when(((l8 >> c) & 1) == 1)
        def _():
            pltpu.make_async_copy(
                src_hbm.at[pl.ds(pl.multiple_of(s + low, SUBLANES), size)],
                dst_buf.at[pl.ds(pl.multiple_of(d + low, SUBLANES), size)], sem).start()

    for c in range(min(LOW_BITS, nbits)):
        bit_copy(c)
    if nbits > LOW_BITS:
        def long_copies(_, carry):
            for c in range(LOW_BITS, nbits):
                bit_copy(c)
            return carry

        lax.fori_loop(0, jnp.where(l8 >= (1 << LOW_BITS), 1, 0), long_copies, 0)


def _start_pieces(src_hbm, dst_buf, sem, p_lo, p_hi, psrc_ref, pdst_ref, plen_ref, nbits):
    def body(p, carry):
        _start_piece(src_hbm, dst_buf, sem, psrc_ref[p], pdst_ref[p], plen_ref[p], nbits)
        return carry

    lax.fori_loop(p_lo, p_hi, body, 0)


def _wait_rows(src_hbm, dst_buf, sem, rows8, nbits):
    for c in range(nbits):
        size = SUBLANES << c

        @pl.when(((rows8 >> c) & 1) == 1)
        def _():
            pltpu.make_async_copy(src_hbm.at[pl.ds(0, size)], dst_buf.at[pl.ds(0, size)], sem).wait()


def _moe_gmm_kernel(be_ref, rows_ref, wslot_ref, nxt_ref,
                    psa_ref, pea_ref, srca_ref, dsta_ref, lena_ref, hsrca_ref, hlena_ref,
                    psb_ref, peb_ref, srcb_ref, dstb_ref, lenb_ref, hsrcb_ref, hlenb_ref,
                    tsa_hbm, tsb_hbm, wgu_hbm, wdn_hbm, bg_ref, bu_ref, bd_ref, perm_ref,
                    ys_ref, lhs_ref, wgu_buf, wdn_buf, wg_ref, wu_ref, wd_ref, sem_ref, wsem_ref,
                    *, nbits_a, nbits_b):
    j = pl.program_id(0)
    nb = pl.num_programs(0)
    slot = j % 2
    bm = lhs_ref.shape[1]

    def gather(blk, sl):
        for ts_hbm, ps, pe, src, dst, ln, hsrc, hlen, nbits in (
                (tsa_hbm, psa_ref, pea_ref, srca_ref, dsta_ref, lena_ref, hsrca_ref, hlena_ref, nbits_a),
                (tsb_hbm, psb_ref, peb_ref, srcb_ref, dstb_ref, lenb_ref, hsrcb_ref, hlenb_ref, nbits_b)):
            _start_pieces(ts_hbm, lhs_ref.at[sl], sem_ref.at[sl], ps[blk], pe[blk], src, dst, ln, nbits)
            _start_piece(ts_hbm, lhs_ref.at[sl], sem_ref.at[sl], hsrc[blk], 0, hlen[blk], nbits)

    def weight_copies(e, ws):
        return (pltpu.make_async_copy(wgu_hbm.at[e], wgu_buf.at[ws], wsem_ref.at[ws]),
                pltpu.make_async_copy(wdn_hbm.at[e], wdn_buf.at[ws], wsem_ref.at[ws]))

    @pl.when(j == 0)
    def _():
        lhs_ref[...] = jnp.zeros(lhs_ref.shape, U32)
        gather(0, 0)
        for cp in weight_copies(be_ref[0], wslot_ref[0]):
            cp.start()

    @pl.when(j + 1 < nb)
    def _():
        gather(j + 1, 1 - slot)

    @pl.when(jnp.logical_or(j == 0, be_ref[j] != be_ref[jnp.maximum(j - 1, 0)]))
    def _():
        ws = wslot_ref[j]
        for cp in weight_copies(be_ref[j], ws):
            cp.wait()
        nxt = nxt_ref[j]

        @pl.when(nxt >= 0)
        def _():
            for cp in weight_copies(nxt, 1 - ws):
                cp.start()

        perm = perm_ref[...]
        half = PERM_COLS // 2
        for c in range(2 * D_FF // PERM_COLS):
            wb = wgu_buf[ws, :, c * PERM_COLS:(c + 1) * PERM_COLS].astype(BF16)
            wp = jnp.dot(wb, perm, preferred_element_type=F32).astype(BF16)
            wg_ref[:, c * half:(c + 1) * half] = wp[:, :half]
            wu_ref[:, c * half:(c + 1) * half] = wp[:, half:]
        wd_ref[...] = wdn_buf[ws].astype(BF16)

    rows8 = rows_ref[j]
    _wait_rows(tsa_hbm, lhs_ref.at[slot], sem_ref.at[slot], rows8, (bm // SUBLANES).bit_length())

    def expert_rows(rs):
        x = _unpack_bf16_pairs(lhs_ref[slot, rs, :])
        gate = jnp.dot(x, wg_ref[...], preferred_element_type=F32) + bg_ref[0]
        up = jnp.dot(x, wu_ref[...], preferred_element_type=F32) + bu_ref[0]
        gate = jnp.minimum(gate, SWIGLU_LIMIT)
        up = jnp.clip(up, -SWIGLU_LIMIT, SWIGLU_LIMIT)
        act = (up + 1.0) * (gate * jax.nn.sigmoid(SWIGLU_ALPHA * gate))
        y = jnp.dot(act.astype(BF16), wd_ref[...], preferred_element_type=F32) + bd_ref[0]
        ys_ref[rs, :] = _pack_bf16_pairs(y.astype(BF16).astype(F32))

    quarter = bm // 4
    for nq in range(5):
        lo8, hi8 = (nq - 1) * quarter // SUBLANES, nq * quarter // SUBLANES

        @pl.when(jnp.logical_and(rows8 > lo8, rows8 <= hi8) if nq else rows8 == 0)
        def _():
            if nq:
                expert_rows(slice(0, nq * quarter))
            if nq < 4:
                ys_ref[nq * quarter:, :] = jnp.zeros((bm - nq * quarter, HALF), U32)


def _moe_gmm(blocks, tabs_a, tabs_b, ts_a, ts_b, w_gu, w_dn, bg, bu, bd, perm, nblocks, bm, nbits_a, nbits_b):
    we = lambda j, be, *_: (be[j], 0, 0)
    grid_spec = pltpu.PrefetchScalarGridSpec(
        num_scalar_prefetch=18,
        grid=(nblocks,),
        in_specs=[pl.BlockSpec(memory_space=pl.ANY), pl.BlockSpec(memory_space=pl.ANY),
                  pl.BlockSpec(memory_space=pl.ANY), pl.BlockSpec(memory_space=pl.ANY),
                  pl.BlockSpec((1, 1, D_FF), we), pl.BlockSpec((1, 1, D_FF), we),
                  pl.BlockSpec((1, 1, D_MODEL), we),
                  pl.BlockSpec((PERM_COLS, PERM_COLS), lambda j, *_: (0, 0))],
        out_specs=pl.BlockSpec((bm, HALF), lambda j, *_: (j, 0)),
        scratch_shapes=[pltpu.VMEM((2, bm, HALF), U32),
                        pltpu.VMEM((2, D_MODEL, 2 * D_FF), F32), pltpu.VMEM((2, D_FF, D_MODEL), F32),
                        pltpu.VMEM((D_MODEL, D_FF), BF16), pltpu.VMEM((D_MODEL, D_FF), BF16),
                        pltpu.VMEM((D_FF, D_MODEL), BF16),
                        pltpu.SemaphoreType.DMA((2,)), pltpu.SemaphoreType.DMA((2,))],
    )
    return pl.pallas_call(
        functools.partial(_moe_gmm_kernel, nbits_a=nbits_a, nbits_b=nbits_b),
        grid_spec=grid_spec,
        out_shape=jax.ShapeDtypeStruct((nblocks * bm, HALF), U32),
        compiler_params=_cparams(1),
        name="moe_gmm",
    )(*blocks, *tabs_a, *tabs_b, ts_a, ts_b, w_gu, w_dn, bg, bu, bd, perm)


def _combine_kernel(psrc_ref, pdst_ref, plen_ref, tlo_ref, thi_ref, tsrc_ref, tdst_ref, tlen_ref, rows_ref,
                    ys_hbm, x2_ref, dest_ref, gate_ref, o_ref,
                    buf_ref, db_ref, gb_ref, sem_ref, *, nbits):
    i = pl.program_id(0)
    n = pl.num_programs(0)
    slot = i % 2
    tt = x2_ref.shape[0]
    tile_rows = buf_ref.shape[1]

    def gather(tile, sl):
        _start_pieces(ys_hbm, buf_ref.at[sl], sem_ref.at[sl], tile * N_EXPERTS, (tile + 1) * N_EXPERTS,
                      psrc_ref, pdst_ref, plen_ref, nbits)
        _start_pieces(ys_hbm, buf_ref.at[sl], sem_ref.at[sl], tlo_ref[tile], thi_ref[tile],
                      tsrc_ref, tdst_ref, tlen_ref, nbits)

    @pl.when(i == 0)
    def _():
        buf_ref[...] = jnp.zeros(buf_ref.shape, U32)
        gather(0, 0)

    @pl.when(i + 1 < n)
    def _():
        gather(i + 1, 1 - slot)

    _wait_rows(ys_hbm, buf_ref.at[slot], sem_ref.at[slot], rows_ref[i], (tile_rows // SUBLANES).bit_length())

    rows = jnp.concatenate([dest_ref[0].astype(F32), gate_ref[0],
                            jnp.zeros((LANES - 2 * TOP_K, tt), F32)], axis=0)
    cols = rows.T
    for k in range(TOP_K):
        db_ref[k] = jnp.broadcast_to(cols[:, k:k + 1], (tt, DISPATCH_CHUNK)).astype(I32).astype(I16)
        gb_ref[k] = jnp.broadcast_to(cols[:, TOP_K + k:TOP_K + k + 1], (tt, DISPATCH_CHUNK)).astype(BF16)
    li = lax.broadcasted_iota(I32, (tt, DISPATCH_CHUNK), 1).astype(I16)
    gms = []
    for c in range(tile_rows // DISPATCH_CHUNK):
        lic = li + jnp.int16(c * DISPATCH_CHUNK)
        gm = jnp.zeros((tt, DISPATCH_CHUNK), BF16)
        for k in range(TOP_K):
            gm = jnp.where(lic == db_ref[k], gb_ref[k], gm)
        gms.append(gm)
    o_ref[...] = x2_ref[...] + jnp.dot(jnp.concatenate(gms, axis=1), _unpack_bf16_pairs(buf_ref[slot]),
                                       preferred_element_type=F32)


def _combine(tabs, ys, x2, dest, gate, tt, nbits):
    n = x2.shape[0]
    nt = n // tt
    tile_rows = _tile_rows(tt)
    grid_spec = pltpu.PrefetchScalarGridSpec(
        num_scalar_prefetch=len(tabs),
        grid=(nt,),
        in_specs=[pl.BlockSpec(memory_space=pl.ANY),
                  pl.BlockSpec((tt, D_MODEL), lambda i, *_: (i, 0)),
                  pl.BlockSpec((1, TOP_K, tt), lambda i, *_: (i, 0, 0)),
                  pl.BlockSpec((1, TOP_K, tt), lambda i, *_: (i, 0, 0))],
        out_specs=pl.BlockSpec((tt, D_MODEL), lambda i, *_: (i, 0)),
        scratch_shapes=[pltpu.VMEM((2, tile_rows, HALF), U32),
                        pltpu.VMEM((TOP_K, tt, DISPATCH_CHUNK), I16), pltpu.VMEM((TOP_K, tt, DISPATCH_CHUNK), BF16),
                        pltpu.SemaphoreType.DMA((2,))],
    )
    return pl.pallas_call(
        functools.partial(_combine_kernel, nbits=nbits),
        grid_spec=grid_spec,
        out_shape=jax.ShapeDtypeStruct((n, D_MODEL), F32),
        compiler_params=_cparams(1),
        name="combine",
    )(*tabs, ys, x2, dest, gate)


def _piece_tables(n8_a, off_a, rows_a, n8_b, off_b, bm, nblocks):
    nta = n8_a.shape[0]
    n8 = jnp.concatenate([n8_a, n8_b], axis=0)
    seg_off = jnp.concatenate([off_a, off_b], axis=0)
    n_tiles = n8.shape[0]
    tile_base = jnp.concatenate([jnp.arange(nta, dtype=I32) * rows_a, jnp.zeros((n_tiles - nta,), I32)])[:, None]
    tot = jnp.sum(n8, axis=0)
    pos0 = jnp.cumsum(n8, axis=0) - n8
    nblk = (tot + bm - 1) // bm
    cs = jnp.cumsum(nblk)
    bs = cs - nblk
    kblk = pos0 // bm
    len0 = jnp.minimum(n8, (kblk + 1) * bm - pos0)
    len1 = n8 - len0
    b0 = bs[None, :] + kblk
    src0 = tile_base + seg_off
    in_blk = pos0 - kblk * bm
    jj = jnp.arange(nblocks, dtype=I32)
    i32 = lambda v: v.astype(I32)

    def gmm_tabs(sl):
        em = lambda v: v[sl].T.reshape(-1)
        blk_em = em(b0)
        first = i32(jnp.sum(blk_em[None, :] < jj[:, None], axis=1))
        last = i32(jnp.sum(blk_em[None, :] <= jj[:, None], axis=1))
        hit = (blk_em[None, :] + 1 == jj[:, None]) & (em(len1)[None, :] > 0)
        tail_src = i32(jnp.sum(jnp.where(hit, em(src0 + len0)[None, :], 0), axis=1))
        tail_len = i32(jnp.sum(jnp.where(hit, em(len1)[None, :], 0), axis=1) // SUBLANES)
        return first, last, i32(em(src0)), i32(em(in_blk)), i32(em(len0) // SUBLANES), tail_src, tail_len

    def comb_tabs(sl):
        tm = lambda v: v[sl].reshape(-1)
        has_tail = len1[sl] > 0
        cnt = jnp.sum(has_tail, axis=1)
        lo = jnp.cumsum(cnt) - cnt
        slot_ = lo[:, None] + jnp.cumsum(has_tail, axis=1) - has_tail
        hit = (slot_.reshape(-1)[None, :] == jj[:, None]) & has_tail.reshape(-1)[None, :]
        pick = lambda v: i32(jnp.sum(jnp.where(hit, tm(v)[None, :], 0), axis=1))
        return (i32(tm(b0 * bm + in_blk)), i32(tm(seg_off)), i32(tm(len0) // SUBLANES),
                i32(lo), i32(lo + cnt), pick((b0 + 1) * bm), pick(seg_off + len0), pick(len1 // SUBLANES),
                i32(jnp.sum(n8[sl], axis=1) // SUBLANES))

    count_le = lambda v: jnp.sum(cs[None, :] <= v[:, None], axis=1)
    n_active = cs[-1]
    e_last = count_le(jnp.maximum(n_active - 1, 0).reshape(1))[0]
    block_e = jnp.minimum(count_le(jj), e_last).astype(I32)
    ee = jnp.arange(N_EXPERTS, dtype=I32)
    mine = (jj[:, None] >= bs[None, :]) & (jj[:, None] < cs[None, :])
    left = jnp.clip(tot[None, :] - (jj[:, None] - bs[None, :]) * bm, 0, bm)
    rows8 = (jnp.sum(jnp.where(mine, left, 0), axis=1) // SUBLANES).astype(I32)
    has = nblk > 0
    run = jnp.cumsum(has.astype(I32)) - 1
    later = (ee[None, :] > ee[:, None]) & has[None, :]
    nxt_e = jnp.min(jnp.where(later, ee[None, :], N_EXPERTS), axis=1)
    nxt_e = jnp.where(nxt_e == N_EXPERTS, -1, nxt_e)
    own = block_e[:, None] == ee[None, :]
    wslot = (jnp.sum(jnp.where(own, run[None, :], 0), axis=1) % 2).astype(I32)
    nxt = jnp.sum(jnp.where(own, nxt_e[None, :], 0), axis=1).astype(I32)
    a, b = slice(0, nta), slice(nta, n_tiles)
    return (block_e, rows8, wslot, nxt), gmm_tabs(a), gmm_tabs(b), comb_tabs(a), comb_tabs(b)


def _block_diag(w):
    nb, bi, bo = w.shape
    eye = jnp.eye(nb, dtype=w.dtype)
    return (eye[:, None, :, None] * w[:, :, None, :]).reshape(nb * bi, nb * bo)


def _step(x_prompt, x_sample, cache_k, cache_v, state_conv, state_h, g_mix_norm, w_in, g_q_norm, g_k_norm,
          attn_sinks, conv_w, conv_b, w_lru_a, b_lru_a, w_lru_x, b_lru_x, lru_lambda, g_attn_out, g_rnn_out,
          w_out, g_ffn_norm, w_router, b_router, w_gate_up, b_gate_up, w_down, b_down,
          *, tm, tt, bm, past_len):
    B, S, D = x_prompt.shape
    NS = x_sample.shape[0]
    assert x_sample.shape[1] == 1 and D == D_MODEL
    assert (B * S) % tt == 0 and S % tm == 0 and tm % ATTN_BLOCK == 0
    assert NS % SUBLANES == 0 and tt <= bm
    assert tt % SUBLANES == 0 and NS <= bm
    n_pt = (B * S) // tt
    total_rows = TOP_K * (B * S + NS) + (n_pt + 1) * N_EXPERTS * (SUBLANES - 1)
    nblocks = -(-total_rows // bm) + N_EXPERTS
    nbits_p = (tt // SUBLANES).bit_length()
    nbits_s = (NS // SUBLANES).bit_length()

    l = 0
    row = lambda v: v[l].reshape(1, -1)
    w_in_bf = w_in[l].astype(BF16)
    gq2 = jnp.tile(g_q_norm[l], 2).reshape(1, LANES)
    gk2 = jnp.tile(g_k_norm[l], 2).reshape(1, LANES)
    wa = _block_diag(w_lru_a[l]).astype(BF16)
    wx = _block_diag(w_lru_x[l]).astype(BF16)
    ba = b_lru_a[l].reshape(1, D_RNN)
    bx = b_lru_x[l].reshape(1, D_RNN)
    wo = w_out[l].astype(BF16)
    woa, wor = wo[:D_ATTN], wo[D_ATTN:]
    wr = w_router[l].T
    wr_hi = wr.astype(BF16)
    wrt = jnp.concatenate([wr_hi, (wr - wr_hi.astype(F32)).astype(BF16)], axis=0)
    br = b_router[l].reshape(N_EXPERTS, 1)
    low = jnp.tril(jnp.ones((N_EXPERTS, N_EXPERTS), BF16), k=-1)
    bgu = b_gate_up[l].reshape(N_EXPERTS, D_FF, 2)
    bg = bgu[:, :, 0].reshape(N_EXPERTS, 1, D_FF)
    bu = bgu[:, :, 1].reshape(N_EXPERTS, 1, D_FF)
    bd = b_down[l].reshape(N_EXPERTS, 1, D_MODEL)
    half = PERM_COLS // 2
    pr = jnp.arange(PERM_COLS)
    perm = (pr[None, :] == jnp.where(pr % 2 == 0, pr // 2, half + pr // 2)[:, None]).astype(BF16)
    sinks = attn_sinks[l]

    ctab, s1tab, s2tab = _rope_tables(jnp.arange(S))
    an, rn, kt_p, vt_p, h_last_p, xr_tail = _front(
        x_prompt.reshape(B * S, D), sinks, row(g_mix_norm), w_in_bf, gq2, gk2, ctab, s1tab, s2tab,
        row(g_attn_out), conv_w[l], row(conv_b), wa, ba, wx, bx, row(lru_lambda), row(g_rnn_out), B, S, tm)
    x2_p, ts_p, dest_p, gate_p, n8_p, off_p = _mix_route(
        x_prompt.reshape(B * S, D), an, rn, woa, wor, row(g_ffn_norm), wrt, br, low, tt)

    cs_tab = _rope_tables(jnp.full((NS,), past_len, I32))
    q_s, k_s, v_s, xr_s, yr_s = _in_proj(x_sample.reshape(NS, D), row(g_mix_norm), w_in_bf, gq2, gk2,
                                         *cs_tab, NS)
    to_rows = lambda c: jnp.transpose(c, (0, 2, 3, 1)).reshape(NS * KV_W, WINDOW)
    from_rows = lambda c, n: jnp.transpose(c.reshape(n, N_KV_HEADS, HEAD_DIM, WINDOW), (0, 3, 1, 2))[None]
    an_s, kt_s, vt_s = _attn_sample(q_s, k_s, v_s, to_rows(cache_k[l]), to_rows(cache_v[l]), sinks,
                                    row(g_attn_out), min(NS, SAMPLE_ATTN_SEQS))
    rn_s, h_last_s, hist_s = _rnn_sample(xr_s, yr_s, jnp.transpose(state_conv[l], (1, 0, 2)), state_h[l],
                                         conv_w[l], row(conv_b), wa, ba, wx, bx, row(lru_lambda),
                                         row(g_rnn_out))
    x2_s, ts_s, dest_s, gate_s, n8_s, off_s = _mix_route(
        x_sample.reshape(NS, D), an_s, rn_s, woa, wor, row(g_ffn_norm), wrt, br, low, NS)

    blocks, gmm_p, gmm_s, comb_p, comb_s = _piece_tables(
        n8_p[:, :, 0], off_p[:, :, 0], _tile_rows(tt), n8_s[:, :, 0], off_s[:, :, 0], bm, nblocks)
    ys = _moe_gmm(blocks, gmm_p, gmm_s, ts_p, ts_s, w_gate_up[l], w_down[l], bg, bu, bd, perm,
                  nblocks, bm, nbits_p, nbits_s)
    y_p = _combine(comb_p, ys, x2_p, dest_p, gate_p, tt, nbits_p)
    y_s = _combine(comb_s, ys, x2_s, dest_s, gate_s, NS, nbits_s)

    cp = xr_tail[:, SUBLANES - (CONV_WIDTH - 1):]
    return (y_p.reshape(B, S, D), y_s.reshape(NS, 1, D),
            from_rows(kt_p, B), from_rows(vt_p, B), cp[None], h_last_p.reshape(1, B, D_RNN),
            from_rows(kt_s, NS), from_rows(vt_s, NS), jnp.transpose(hist_s, (1, 0, 2))[None], h_last_s[None])


def kernel(x_prompt, x_sample, cache_k, cache_v, state_conv, state_h, g_mix_norm, w_in, g_q_norm, g_k_norm, attn_sinks, conv_w, conv_b, w_lru_a, b_lru_a, w_lru_x, b_lru_x, lru_lambda, g_attn_out, g_rnn_out, w_out, g_ffn_norm, w_router, b_router, w_gate_up, b_gate_up, w_down, b_down):
    return _step(x_prompt, x_sample, cache_k, cache_v, state_conv, state_h, g_mix_norm, w_in, g_q_norm,
                 g_k_norm, attn_sinks, conv_w, conv_b, w_lru_a, b_lru_a, w_lru_x, b_lru_x, lru_lambda,
                 g_attn_out, g_rnn_out, w_out, g_ffn_norm, w_router, b_router, w_gate_up, b_gate_up,
                 w_down, b_down, tm=512, tt=512, bm=MOE_BLOCK_ROWS, past_len=PAST_LEN)
```

```python
import functools

import jax
import jax.numpy as jnp
from jax import lax
from jax.experimental import pallas as pl
from jax.experimental.pallas import tpu as pltpu

F32 = jnp.float32
BF16 = jnp.bfloat16
I32 = jnp.int32
I16 = jnp.int16

D_MODEL = 1024
HEAD_DIM = 64
N_HEADS = 8
N_KV_HEADS = 2
GROUP = 4
WINDOW = 128
ATTN_BLOCK = 128
ROT_DIM = 16
ROPE_THETA = 500000.0
D_ATTN = 512
D_RNN = 512
KV_W = 128
D_IN = 1792
CONV_WIDTH = 4
LRU_C = 8.0
N_EXPERTS = 32
TOP_K = 4
D_FF = 1024
SWIGLU_LIMIT = 7.0
SWIGLU_ALPHA = 1.702
EPS = 1e-6
PAST_LEN = 8192

LANES = 128
SUBLANES = 8
NEG_BIG = -1e30
VMEM_LIMIT = 56 * 1024 * 1024

MOE_BLOCK_ROWS = 512
PERM_COLS = 256
DISPATCH_CHUNK = 256
SAMPLE_ATTN_SEQS = 32


def _cparams(n_axes):
    return pltpu.CompilerParams(dimension_semantics=("arbitrary",) * n_axes,
                                vmem_limit_bytes=VMEM_LIMIT)


U32 = jnp.uint32
HALF = D_MODEL // 2
HI16 = 0xFFFF0000


def _pack_bf16_pairs(x):
    bits = pltpu.bitcast(x, U32)
    return (bits[:, HALF:] & U32(HI16)) | (bits[:, :HALF] >> 16)


def _unpack_bf16_pairs(u):
    lo = pltpu.bitcast(u << 16, F32)
    hi = pltpu.bitcast(u & U32(HI16), F32)
    return jnp.concatenate([lo, hi], axis=1).astype(BF16)


def _rmsnorm(x, g):
    ms = jnp.mean(x * x, axis=-1, keepdims=True)
    return (x * lax.rsqrt(ms + EPS)) * g


def _head_norm_rope(t, g, c, s1, s2, lo):
    sq = t * t
    s_lo = jnp.sum(jnp.where(lo, sq, 0.0), axis=-1, keepdims=True)
    s_hi = jnp.sum(jnp.where(lo, 0.0, sq), axis=-1, keepdims=True)
    ms = jnp.where(lo, s_lo, s_hi) * (1.0 / HEAD_DIM)
    n = (t * lax.rsqrt(ms + EPS)) * g
    up = pltpu.roll(n, LANES - ROT_DIM // 2, 1)
    dn = pltpu.roll(n, ROT_DIM // 2, 1)
    return n * c + up * s1 + dn * s2


def _in_proj_kernel(x_ref, g_ref, w_ref, gq_ref, gk_ref, c_ref, s1_ref, s2_ref,
                    q_ref, k_ref, v_ref, xr_ref, yr_ref):
    tm = x_ref.shape[0]
    h = _rmsnorm(x_ref[...], g_ref[...])
    proj = jnp.dot(h.astype(BF16), w_ref[...], preferred_element_type=F32)
    rope = (c_ref[...], s1_ref[...], s2_ref[...], lax.broadcasted_iota(I32, (tm, LANES), 1) < HEAD_DIM)
    gq = gq_ref[...]
    for j in range(D_ATTN // LANES):
        q_ref[:, j * LANES:(j + 1) * LANES] = _head_norm_rope(proj[:, j * LANES:(j + 1) * LANES], gq, *rope)
    k_ref[...] = _head_norm_rope(proj[:, D_ATTN:D_ATTN + KV_W], gk_ref[...], *rope)
    v_ref[...] = proj[:, D_ATTN + KV_W:D_ATTN + 2 * KV_W]
    o = D_ATTN + 2 * KV_W
    xr_ref[...] = proj[:, o:o + D_RNN]
    yr_ref[...] = proj[:, o + D_RNN:o + 2 * D_RNN]


def _in_proj(x2d, g, w_bf, gq2, gk2, ctab, s1tab, s2tab, tm):
    n = x2d.shape[0]
    ntab = ctab.shape[0] // tm
    row = lambda i: (i, 0)
    fix = lambda i: (0, 0)
    tab = lambda i: (i % ntab, 0)
    out_shapes = (jax.ShapeDtypeStruct((n, D_ATTN), F32), jax.ShapeDtypeStruct((n, KV_W), F32),
                  jax.ShapeDtypeStruct((n, KV_W), F32), jax.ShapeDtypeStruct((n, D_RNN), F32),
                  jax.ShapeDtypeStruct((n, D_RNN), F32))
    return pl.pallas_call(
        _in_proj_kernel,
        grid=(n // tm,),
        in_specs=[pl.BlockSpec((tm, D_MODEL), row), pl.BlockSpec((1, D_MODEL), fix),
                  pl.BlockSpec((D_MODEL, D_IN), fix), pl.BlockSpec((1, LANES), fix),
                  pl.BlockSpec((1, LANES), fix), pl.BlockSpec((tm, LANES), tab),
                  pl.BlockSpec((tm, LANES), tab), pl.BlockSpec((tm, LANES), tab)],
        out_specs=(pl.BlockSpec((tm, D_ATTN), row), pl.BlockSpec((tm, KV_W), row),
                   pl.BlockSpec((tm, KV_W), row), pl.BlockSpec((tm, D_RNN), row),
                   pl.BlockSpec((tm, D_RNN), row)),
        out_shape=out_shapes,
        compiler_params=_cparams(1),
        name="in_proj",
    )(x2d, g, w_bf, gq2, gk2, ctab, s1tab, s2tab)


def _rope_tables(pos):
    half = ROT_DIM // 2
    inv = ROPE_THETA ** (-jnp.arange(0, ROT_DIM, 2, dtype=F32) / ROT_DIM)
    d = jnp.arange(LANES) % HEAD_DIM
    ang = pos.astype(F32)[:, None] * inv[d % half][None, :]
    cos = jnp.cos(ang)
    sin = jnp.sin(ang)
    c = jnp.where(d < ROT_DIM, cos, 1.0)
    s1 = jnp.where(d < half, -sin, 0.0)
    s2 = jnp.where((d >= half) & (d < ROT_DIM), sin, 0.0)
    return c, s1, s2


def _band_bias(qb):
    qi = jnp.arange(qb, dtype=I32)[:, None]
    c = jnp.arange(2 * qb, dtype=I32)[None, :]
    band = (c >= qi) & (c <= qi + qb)
    first = band & (c >= qb)
    one = jnp.where(jnp.stack([first, band]), 0.0, NEG_BIG).astype(F32)
    return jnp.concatenate([one, one], axis=2)


def _attn_sample_kernel(sink_ref, q_ref, kn_ref, vn_ref, kt_ref, vt_ref, g_ref,
                        o_ref, nkt_ref, nvt_ref, acc_ref):
    bb = q_ref.shape[0]
    q = q_ref[...] * (HEAD_DIM ** -0.5)
    kn = kn_ref[...]
    vn = vn_ref[...]
    kt = kt_ref[...]
    vt = vt_ref[...]
    col = lax.broadcasted_iota(I32, (bb, bb * KV_W), 1)
    rowb = lax.broadcasted_iota(I32, (bb, bb * KV_W), 0)
    own_seq = (col >> (KV_W.bit_length() - 1)) == rowb
    half_hi = ((col >> (HEAD_DIM.bit_length() - 1)) & 1) == 1
    qbig = []
    for h in range(N_HEADS):
        kv = h // GROUP
        pair = q[:, (h // 2) * LANES:(h // 2 + 1) * LANES]
        if (h % 2) != kv:
            pair = pltpu.roll(pair, HEAD_DIM, 1)
        tiled = jnp.concatenate([pair] * bb, axis=1)
        keep = own_seq & (half_hi if kv == 1 else jnp.logical_not(half_hi))
        qbig.append(jnp.where(keep, tiled, 0.0))
    qbig = jnp.concatenate(qbig, axis=0)
    s = jnp.dot(qbig.astype(BF16), kt.astype(BF16), preferred_element_type=F32)
    qb16 = q.astype(BF16).astype(F32)
    kb16 = kn.astype(BF16).astype(F32)
    s_new, sink = [], []
    for h in range(N_HEADS):
        kv = h // GROUP
        s_new.append(jnp.sum(qb16[:, h * HEAD_DIM:(h + 1) * HEAD_DIM] * kb16[:, kv * HEAD_DIM:(kv + 1) * HEAD_DIM],
                             axis=-1, keepdims=True))
        sink.append(jnp.full((bb, 1), sink_ref[h], F32))
    s_new = jnp.concatenate(s_new, axis=0)
    sink = jnp.concatenate(sink, axis=0)
    m = jnp.maximum(jnp.maximum(jnp.max(s, axis=-1, keepdims=True), s_new), sink)
    e = jnp.exp(s - m)
    e_new = jnp.exp(s_new - m)
    inv = 1.0 / (jnp.sum(e, axis=-1, keepdims=True) + e_new + jnp.exp(sink - m))
    obig = lax.dot_general(e.astype(BF16), vt.astype(BF16), (((1,), (1,)), ((), ())),
                           preferred_element_type=F32)
    for h in range(N_HEADS):
        kv = h // GROUP
        blk = jnp.where(own_seq, obig[h * bb:(h + 1) * bb, :], 0.0)
        fold = blk[:, 0:KV_W]
        for t in range(1, bb):
            fold = fold + blk[:, t * KV_W:(t + 1) * KV_W]
        hs = slice(h * bb, (h + 1) * bb)
        ks = slice(kv * HEAD_DIM, (kv + 1) * HEAD_DIM)
        acc_ref[:, h * HEAD_DIM:(h + 1) * HEAD_DIM] = (fold[:, ks] + e_new[hs] * vn[:, ks]) * inv[hs]
    o_ref[...] = _rmsnorm(acc_ref[...], g_ref[...])

    last = lax.broadcasted_iota(I32, (KV_W, WINDOW), 1) == WINDOW - 1
    for b in range(bb):
        rs = slice(b * KV_W, (b + 1) * KV_W)
        kcol = jnp.broadcast_to(kn[b:b + 1, :], (KV_W, KV_W)).T
        vcol = jnp.broadcast_to(vn[b:b + 1, :], (KV_W, KV_W)).T
        nkt_ref[rs, :] = jnp.where(last, kcol, pltpu.roll(kt[rs, :], WINDOW - 1, 1))
        nvt_ref[rs, :] = jnp.where(last, vcol, pltpu.roll(vt[rs, :], WINDOW - 1, 1))


def _attn_sample(q, kn, vn, kt2d, vt2d, sinks, g_attn, bb):
    n = q.shape[0]
    row = lambda i: (i, 0)
    fix = lambda i: (0, 0)
    cache = pl.BlockSpec((bb * KV_W, WINDOW), row)
    return pl.pallas_call(
        _attn_sample_kernel,
        grid=(n // bb,),
        in_specs=[pl.BlockSpec(memory_space=pltpu.SMEM),
                  pl.BlockSpec((bb, D_ATTN), row), pl.BlockSpec((bb, KV_W), row),
                  pl.BlockSpec((bb, KV_W), row), cache, cache,
                  pl.BlockSpec((1, D_ATTN), fix)],
        out_specs=(pl.BlockSpec((bb, D_ATTN), row), cache, cache),
        out_shape=(jax.ShapeDtypeStruct((n, D_ATTN), F32),
                   jax.ShapeDtypeStruct(kt2d.shape, F32), jax.ShapeDtypeStruct(vt2d.shape, F32)),
        scratch_shapes=[pltpu.VMEM((bb, D_ATTN), F32)],
        compiler_params=_cparams(1),
        name="attn_sample",
    )(sinks, q, kn, vn, kt2d, vt2d, g_attn)


def _softplus(z):
    return jnp.maximum(z, 0.0) + jnp.log1p(jnp.exp(-jnp.abs(z)))


def _lru_gates(xc, wa_ref, ba_ref, wx_ref, bx_ref, lam_ref):
    xb = xc.astype(BF16)
    r = jax.nn.sigmoid(jnp.dot(xb, wa_ref[...], preferred_element_type=F32) + ba_ref[...])
    i = jax.nn.sigmoid(jnp.dot(xb, wx_ref[...], preferred_element_type=F32) + bx_ref[...])
    log_a = (-LRU_C * r) * _softplus(-lam_ref[...])
    a = jnp.exp(log_a)
    z = -jnp.tanh(log_a) * (a * a + 1.0)
    u = jnp.where(z > 0.0, z * lax.rsqrt(z), 0.0) * (i * xc)
    return a, u


def _lru_scan(a, u, h0):
    ng = a.shape[0] // SUBLANES
    a3 = a.reshape(ng, SUBLANES, D_RNN)
    u3 = u.reshape(ng, SUBLANES, D_RNN)
    t8 = lax.broadcasted_iota(I32, (ng, SUBLANES, D_RNN), 1)
    d = 1
    while d < SUBLANES:
        a_s = jnp.where(t8 >= d, pltpu.roll(a3, d, 1), 1.0)
        u_s = jnp.where(t8 >= d, pltpu.roll(u3, d, 1), 0.0)
        u3 = a3 * u_s + u3
        a3 = a3 * a_s
        d *= 2
    carry = h0
    groups = []
    for g in range(ng):
        hg = a3[g] * carry + u3[g]
        groups.append(hg)
        carry = hg[SUBLANES - 1:SUBLANES, :]
    return jnp.concatenate(groups, axis=0), carry


def _lru_scan_tiles(a_ref, u_ref, h_ref, h0):
    nl, rows, _ = a_ref.shape
    ng = rows // SUBLANES
    step = lambda ref, s: jnp.concatenate(
        [ref[j, pl.ds(s, ng, stride=SUBLANES), :] for j in range(nl)], axis=1)
    prods = [step(a_ref, 0)]
    locs = [step(u_ref, 0)]
    for s in range(1, SUBLANES):
        a_s = step(a_ref, s)
        locs.append(a_s * locs[-1] + step(u_ref, s))
        prods.append(a_s * prods[-1])
    after, h_last = _lru_scan(prods[-1], locs[-1], h0)
    row = lax.broadcasted_iota(I32, (ng, D_RNN), 0)
    before = jnp.where(row == 0, h0, pltpu.roll(after, 1, 0))
    for s in range(SUBLANES):
        h_s = locs[s] + prods[s] * before
        for j in range(nl):
            h_ref[j, pl.ds(s, ng, stride=SUBLANES), :] = h_s[:, j * LANES:(j + 1) * LANES]
    return h_last


def _to_lane_tiles(ref, x):
    for j in range(ref.shape[0]):
        ref[j] = x[:, j * LANES:(j + 1) * LANES]


def _rnn_sample_kernel(xr_ref, yr_ref, hist_ref, h0_ref, cw_ref, cb_ref, wa_ref, ba_ref, wx_ref, bx_ref,
                       lam_ref, g_ref, o_ref, hl_ref, nh_ref):
    cw = cw_ref[...]
    xr = xr_ref[...]
    xc = cb_ref[...] + xr * cw[CONV_WIDTH - 1:CONV_WIDTH, :]
    for w in range(CONV_WIDTH - 1):
        xc = xc + hist_ref[w] * cw[w:w + 1, :]
    a, u = _lru_gates(xc, wa_ref, ba_ref, wx_ref, bx_ref, lam_ref)
    h = a * h0_ref[...] + u
    hl_ref[...] = h
    o_ref[...] = _rmsnorm(jax.nn.gelu(yr_ref[...]) * h, g_ref[...])
    for w in range(CONV_WIDTH - 2):
        nh_ref[w] = hist_ref[w + 1]
    nh_ref[CONV_WIDTH - 2] = xr


def _rnn_sample(xr, yr, hist, h0, cw, cb, wa, ba, wx, bx, lam, g):
    n = xr.shape[0]
    full = lambda a: pl.BlockSpec(a.shape, lambda: (0,) * a.ndim)
    args = (xr, yr, hist, h0, cw, cb, wa, ba, wx, bx, lam, g)
    return pl.pallas_call(
        _rnn_sample_kernel,
        in_specs=[full(a) for a in args],
        out_specs=(pl.BlockSpec((n, D_RNN), lambda: (0, 0)), pl.BlockSpec((n, D_RNN), lambda: (0, 0)),
                   pl.BlockSpec(hist.shape, lambda: (0, 0, 0))),
        out_shape=(jax.ShapeDtypeStruct((n, D_RNN), F32), jax.ShapeDtypeStruct((n, D_RNN), F32),
                   jax.ShapeDtypeStruct(hist.shape, F32)),
        compiler_params=pltpu.CompilerParams(vmem_limit_bytes=VMEM_LIMIT),
        name="rnn_sample",
    )(*args)


def _front_kernel(*refs, tiles_per_seq, nt):
    i = pl.program_id(0)
    q_s, k_s, v_s, xr_s, yr_s, ext_ref, h_ref = refs[25:32]

    @pl.when(i == 0)
    def _():
        for r in (q_s, k_s, v_s, xr_s, yr_s, ext_ref, h_ref):
            r[...] = jnp.zeros(r.shape, F32)
        _front_body(0, 1, *refs, tiles_per_seq=tiles_per_seq, tail=False)

    @pl.when(i == nt)
    def _():
        _front_body(nt % 2, 1 - nt % 2, *refs, tiles_per_seq=tiles_per_seq, proj=False)

    for cur in range(2):
        @pl.when((i % 2 == cur) & (i > 0) & (i < nt))
        def _():
            _front_body(cur, 1 - cur, *refs, tiles_per_seq=tiles_per_seq)


def _front_body(cur, prv, sink_ref, x_ref, gm_ref, w_ref, gq_ref, gk_ref, c_ref, s1_ref, s2_ref, bias_ref,
                ga_ref, cw_ref, cb_ref, wa_ref, ba_ref, wx_ref, bx_ref, lam_ref, gr_ref,
                an_ref, rn_ref, kt_ref, vt_ref, hl_ref, cx_ref,
                q_s, k_s, v_s, xr_s, yr_s, ext_ref, h_ref, s_ref, e_ref, a_scr, u_scr, hs_scr,
                *, tiles_per_seq, proj=True, tail=True):
    i = pl.program_id(0)
    tm = x_ref.shape[0]
    qb = ATTN_BLOCK
    t = jnp.maximum(i - 1, 0)
    first_tile = (t % tiles_per_seq) == 0

    if proj:
        hx = _rmsnorm(x_ref[...], gm_ref[...]).astype(BF16)
        rope = (c_ref[...], s1_ref[...], s2_ref[...], lax.broadcasted_iota(I32, (tm, LANES), 1) < HEAD_DIM)
        project = lambda c0, c1: jnp.dot(hx, w_ref[:, c0:c1], preferred_element_type=F32)

    nqb = tm // qb
    n_pairs = N_HEADS // 2
    lo_k = lax.broadcasted_iota(I32, (2 * qb, LANES), 1) < HEAD_DIM
    lo_q = lax.broadcasted_iota(I32, (qb, LANES), 1) < HEAD_DIM
    nt_dims = (((1,), (1,)), ((), ()))
    vbds = []
    for jb in range(nqb if tail else 0):
        k2 = k_s[prv, jb * qb:(jb + 2) * qb, :]
        v2 = v_s[prv, jb * qb:(jb + 2) * qb, :]
        k2r = pltpu.roll(k2, HEAD_DIM, 1)
        v2r = pltpu.roll(v2, HEAD_DIM, 1)
        bias = bias_ref[jnp.where(first_tile, 0, 1)] if jb == 0 else bias_ref[1]
        for kv in range(N_KV_HEADS):
            ka, kb = (k2, k2r) if kv == 0 else (k2r, k2)
            va, vb = (v2, v2r) if kv == 0 else (v2r, v2)
            kbd = jnp.concatenate([jnp.where(lo_k, ka, 0.0), jnp.where(lo_k, 0.0, kb)], axis=0).astype(BF16)
            vbds.append(jnp.concatenate([jnp.where(lo_k, va, 0.0), jnp.where(lo_k, 0.0, vb)],
                                        axis=0).astype(BF16))
            for p in range(GROUP // 2):
                pp = kv * (GROUP // 2) + p
                qp = (q_s[prv, jb * qb:(jb + 1) * qb, pp * LANES:(pp + 1) * LANES]
                      * (HEAD_DIM ** -0.5)).astype(BF16)
                s_ref[jb * n_pairs + pp] = lax.dot_general(qp, kbd, nt_dims,
                                                           preferred_element_type=F32) + bias
    if proj:
        pq = project(0, D_ATTN)
        for j in range(D_ATTN // LANES):
            q_s[cur, :, j * LANES:(j + 1) * LANES] = _head_norm_rope(pq[:, j * LANES:(j + 1) * LANES],
                                                                     gq_ref[...], *rope)
    invs = []
    for c in range(nqb * n_pairs if tail else 0):
        pp = c % n_pairs
        inv = []
        for tpos in range(2):
            cols = slice(tpos * 2 * qb, (tpos + 1) * 2 * qb)
            st = s_ref[c, :, cols]
            sink = sink_ref[2 * pp + tpos]
            m = jnp.maximum(jnp.max(st, axis=-1, keepdims=True), sink)
            e = jnp.exp(st - m)
            e_ref[c, :, cols] = e.astype(BF16)
            inv.append(1.0 / (jnp.sum(e, axis=-1, keepdims=True) + jnp.exp(sink - m)))
        invs.append(jnp.where(lo_q, inv[0], inv[1]))
    if proj:
        pkv = project(D_ATTN, D_ATTN + 2 * KV_W)
        k_s[cur, qb:qb + tm, :] = _head_norm_rope(pkv[:, :KV_W], gk_ref[...], *rope)
        v_s[cur, qb:qb + tm, :] = pkv[:, KV_W:]
        k_s[cur, 0:qb, :] = k_s[prv, tm:tm + qb, :]
        v_s[cur, 0:qb, :] = v_s[prv, tm:tm + qb, :]
    for jb in range(nqb if tail else 0):
        outs = [jnp.dot(e_ref[jb * n_pairs + pp], vbds[jb * N_KV_HEADS + pp // (GROUP // 2)],
                        preferred_element_type=F32) * invs[jb * n_pairs + pp] for pp in range(n_pairs)]
        an_ref[jb * qb:(jb + 1) * qb, :] = _rmsnorm(jnp.concatenate(outs, axis=1), ga_ref[...])
    if tail:
        kt_ref[0] = k_s[prv, tm:tm + qb, :].T
        vt_ref[0] = v_s[prv, tm:tm + qb, :].T

    o = D_ATTN + 2 * KV_W
    if proj:
        xr_s[cur] = project(o, o + D_RNN)
    if not tail:
        yr_s[cur] = project(o + D_RNN, o + 2 * D_RNN)
        return

    pad = SUBLANES
    xr = xr_s[prv]
    ext_ref[0:pad, :] = jnp.where(first_tile, 0.0, ext_ref[0:pad, :])
    ext_ref[pad:pad + tm, :] = xr
    cw = cw_ref[...]
    xc = cb_ref[...] + xr * cw[CONV_WIDTH - 1:CONV_WIDTH, :]
    for w in range(CONV_WIDTH - 1):
        sh = CONV_WIDTH - 1 - w
        xc = xc + ext_ref[pad - sh:pad - sh + tm, :] * cw[w:w + 1, :]
    ext_ref[0:pad, :] = xr[tm - pad:tm, :]
    cx_ref[0] = xr[tm - pad:tm, :]
    a, u = _lru_gates(xc, wa_ref, ba_ref, wx_ref, bx_ref, lam_ref)
    if proj:
        yr_s[cur] = project(o + D_RNN, o + 2 * D_RNN)
    _to_lane_tiles(a_scr, a)
    _to_lane_tiles(u_scr, u)
    carry = _lru_scan_tiles(a_scr, u_scr, hs_scr, jnp.where(first_tile, 0.0, h_ref[...]))
    h_ref[...] = carry
    hl_ref[0] = carry
    hseq = jnp.concatenate([hs_scr[j] for j in range(hs_scr.shape[0])], axis=1)
    rn_ref[...] = _rmsnorm(jax.nn.gelu(yr_s[prv]) * hseq, gr_ref[...])


def _front(x2d, sinks, g_mix, w_bf, gq2, gk2, ctab, s1tab, s2tab, g_attn,
           cw, cb, wa, ba, wx, bx, lam, g_rnn, batch, seq, tm):
    n = x2d.shape[0]
    nt = n // tm
    tps = seq // tm
    qb = ATTN_BLOCK
    cur = lambda i: (jnp.minimum(i, nt - 1), 0)
    tab = lambda i: (jnp.minimum(i, nt - 1) % tps, 0)
    fix = lambda i: (0, 0)
    prev = lambda i: (jnp.maximum(i - 1, 0), 0)
    per_seq = lambda i: (jnp.maximum(i - 1, 0) // tps, 0, 0)
    vec = lambda w: pl.BlockSpec((1, w), fix)
    return pl.pallas_call(
        functools.partial(_front_kernel, tiles_per_seq=tps, nt=nt),
        grid=(nt + 1,),
        in_specs=[pl.BlockSpec(memory_space=pltpu.SMEM),
                  pl.BlockSpec((tm, D_MODEL), cur), vec(D_MODEL), pl.BlockSpec((D_MODEL, D_IN), fix),
                  vec(LANES), vec(LANES),
                  pl.BlockSpec((tm, LANES), tab), pl.BlockSpec((tm, LANES), tab), pl.BlockSpec((tm, LANES), tab),
                  pl.BlockSpec((2, qb, 4 * qb), lambda i: (0, 0, 0)), vec(D_ATTN),
                  pl.BlockSpec((CONV_WIDTH, D_RNN), fix), vec(D_RNN),
                  pl.BlockSpec((D_RNN, D_RNN), fix), vec(D_RNN),
                  pl.BlockSpec((D_RNN, D_RNN), fix), vec(D_RNN), vec(D_RNN), vec(D_RNN)],
        out_specs=(pl.BlockSpec((tm, D_ATTN), prev), pl.BlockSpec((tm, D_RNN), prev),
                   pl.BlockSpec((1, KV_W, qb), per_seq), pl.BlockSpec((1, KV_W, qb), per_seq),
                   pl.BlockSpec((1, 1, D_RNN), per_seq), pl.BlockSpec((1, SUBLANES, D_RNN), per_seq)),
        out_shape=(jax.ShapeDtypeStruct((n, D_ATTN), F32), jax.ShapeDtypeStruct((n, D_RNN), F32),
                   jax.ShapeDtypeStruct((batch, KV_W, qb), F32), jax.ShapeDtypeStruct((batch, KV_W, qb), F32),
                   jax.ShapeDtypeStruct((batch, 1, D_RNN), F32),
                   jax.ShapeDtypeStruct((batch, SUBLANES, D_RNN), F32)),
        scratch_shapes=[pltpu.VMEM((2, tm, D_ATTN), F32),
                        pltpu.VMEM((2, tm + qb, KV_W), F32), pltpu.VMEM((2, tm + qb, KV_W), F32),
                        pltpu.VMEM((2, tm, D_RNN), F32), pltpu.VMEM((2, tm, D_RNN), F32),
                        pltpu.VMEM((tm + SUBLANES, D_RNN), F32), pltpu.VMEM((1, D_RNN), F32),
                        pltpu.VMEM((tm // qb * (N_HEADS // 2), qb, 4 * qb), F32),
                        pltpu.VMEM((tm // qb * (N_HEADS // 2), qb, 4 * qb), BF16),
                        pltpu.VMEM((D_RNN // LANES, tm, LANES), F32), pltpu.VMEM((D_RNN // LANES, tm, LANES), F32),
                        pltpu.VMEM((D_RNN // LANES, tm, LANES), F32)],
        compiler_params=_cparams(1),
        name="front",
    )(sinks, x2d, g_mix, w_bf, gq2, gk2, ctab, s1tab, s2tab, _band_bias(qb), g_attn,
      cw, cb, wa, ba, wx, bx, lam, g_rnn)


def _mix_route_kernel(*refs, nt):
    i = pl.program_id(0)
    hb_s, d_s = refs[16:18]

    @pl.when(i == 0)
    def _():
        hb_s[...] = jnp.zeros(hb_s.shape, BF16)
        d_s[...] = jnp.zeros(d_s.shape, I32)
        _mix_route_body(0, 1, *refs, disp=False)

    @pl.when(i == nt)
    def _():
        _mix_route_body(nt % 2, 1 - nt % 2, *refs, route=False)

    for cur in range(2):
        @pl.when((i % 2 == cur) & (i > 0) & (i < nt))
        def _():
            _mix_route_body(cur, 1 - cur, *refs)


def _mix_route_body(cur, prv, x_ref, an_ref, rn_ref, woa_ref, wor_ref, g_ref, wr2_ref, br_ref, tri_ref, low_ref,
                    x2_ref, ts_ref, dest_ref, gate_ref, n8_ref, off_ref, hb_s, d_s, *, route=True, disp=True):
    tt = x_ref.shape[0]
    tile_rows = ts_ref.shape[0]
    n_chunks = tile_rows // DISPATCH_CHUNK

    hb_prev = hb_s[prv]
    d16 = [d_s[prv, k:k + 1, :].astype(I16) for k in range(TOP_K)]
    ri = lax.broadcasted_iota(I32, (DISPATCH_CHUNK, tt), 0).astype(I16)
    one = jnp.ones((DISPATCH_CHUNK, tt), BF16)

    def dispatch(chunks):
        for c in (chunks if disp else ()):
            p = jnp.zeros((DISPATCH_CHUNK, tt), BF16)
            for d in d16:
                p = jnp.where(ri == d - jnp.int16(c * DISPATCH_CHUNK), one, p)
            ts_ref[c * DISPATCH_CHUNK:(c + 1) * DISPATCH_CHUNK, :] = _pack_bf16_pairs(
                jnp.dot(p, hb_prev, preferred_element_type=F32))

    third = -(-n_chunks // 3)
    if not route:
        dispatch(range(n_chunks))
        return

    x2 = x_ref[...] + jnp.dot(an_ref[...].astype(BF16), woa_ref[...], preferred_element_type=F32) \
        + jnp.dot(rn_ref[...].astype(BF16), wor_ref[...], preferred_element_type=F32)
    x2_ref[...] = x2
    dispatch(range(0, third))
    hn = _rmsnorm(x2, g_ref[...])

    nt = (((1,), (1,)), ((), ()))
    hb = hn.astype(BF16)
    hb_s[cur] = hb
    hmid = (hn - hb.astype(F32)).astype(BF16)
    wr2 = wr2_ref[...]
    both = lax.dot_general(wr2, hb, nt, preferred_element_type=F32)
    logits = (lax.dot_general(wr2[:N_EXPERTS], hmid, nt, preferred_element_type=F32)
              + both[N_EXPERTS:]) + both[:N_EXPERTS] + br_ref[...]
    dispatch(range(third, 2 * third))

    ie = lax.broadcasted_iota(I32, (N_EXPERTS, tt), 0).astype(F32)
    l = logits
    vals, sels = [], []
    for _ in range(TOP_K):
        m = jnp.max(l, axis=0, keepdims=True)
        idx = jnp.min(jnp.where(l == m, ie, float(N_EXPERTS)), axis=0, keepdims=True)
        sel = ie == idx
        vals.append(m)
        sels.append(sel)
        l = jnp.where(sel, NEG_BIG, l)
    es = [jnp.exp(v - vals[0]) for v in vals]
    den = es[0] + es[1] + es[2] + es[3]
    gate_ref[0] = jnp.concatenate([e / den for e in es], axis=0)
    dispatch(range(2 * third, n_chunks))

    oh = jnp.zeros((N_EXPERTS, tt), F32)
    for sel in sels:
        oh = oh + jnp.where(sel, 1.0, 0.0)
    before = jnp.dot(oh.astype(BF16), tri_ref[...], preferred_element_type=F32)
    cnt = jnp.sum(oh, axis=1, keepdims=True).astype(I32)
    n8 = ((cnt + (SUBLANES - 1)) >> 3) << 3
    n8b = jnp.broadcast_to(n8, (N_EXPERTS, LANES))
    off = jnp.dot(low_ref[...], n8b.astype(F32).astype(BF16), preferred_element_type=F32)
    n8_ref[0] = n8b
    off_ref[0] = off.astype(I32)
    base = off[:, 0:1] + before
    dests = jnp.concatenate(
        [jnp.sum(jnp.where(sel, base, 0.0), axis=0, keepdims=True).astype(I32) for sel in sels], axis=0)
    dest_ref[0] = dests
    d_s[cur, 0:TOP_K, :] = dests


def _tile_rows(tt):
    return -(-(TOP_K * tt + N_EXPERTS * (SUBLANES - 1)) // DISPATCH_CHUNK) * DISPATCH_CHUNK


def _mix_route(x2d, an, rn, woa, wor, g, wrt, br, low, tt):
    n = x2d.shape[0]
    nt = n // tt
    tile_rows = _tile_rows(tt)
    tri = jnp.triu(jnp.ones((tt, tt), BF16), k=1)
    row = lambda i: (jnp.minimum(i, nt - 1), 0)
    prev = lambda i: (jnp.maximum(i - 1, 0), 0)
    fix = lambda i: (0, 0)
    t3 = lambda i: (jnp.minimum(i, nt - 1), 0, 0)
    in_specs = [pl.BlockSpec((tt, D_MODEL), row), pl.BlockSpec((tt, D_ATTN), row),
                pl.BlockSpec((tt, D_RNN), row), pl.BlockSpec((D_ATTN, D_MODEL), fix),
                pl.BlockSpec((D_RNN, D_MODEL), fix), pl.BlockSpec((1, D_MODEL), fix),
                pl.BlockSpec((2 * N_EXPERTS, D_MODEL), fix), pl.BlockSpec((N_EXPERTS, 1), fix),
                pl.BlockSpec((tt, tt), fix), pl.BlockSpec((N_EXPERTS, N_EXPERTS), fix)]
    out_shape = (jax.ShapeDtypeStruct((n, D_MODEL), F32),
                 jax.ShapeDtypeStruct((nt * tile_rows, HALF), U32),
                 jax.ShapeDtypeStruct((nt, TOP_K, tt), I32),
                 jax.ShapeDtypeStruct((nt, TOP_K, tt), F32),
                 jax.ShapeDtypeStruct((nt, N_EXPERTS, LANES), I32),
                 jax.ShapeDtypeStruct((nt, N_EXPERTS, LANES), I32))
    out_specs = (pl.BlockSpec((tt, D_MODEL), row),
                 pl.BlockSpec((tile_rows, HALF), prev),
                 pl.BlockSpec((1, TOP_K, tt), t3), pl.BlockSpec((1, TOP_K, tt), t3),
                 pl.BlockSpec((1, N_EXPERTS, LANES), t3), pl.BlockSpec((1, N_EXPERTS, LANES), t3))
    return pl.pallas_call(
        functools.partial(_mix_route_kernel, nt=nt),
        grid=(nt + 1,),
        in_specs=in_specs,
        out_specs=out_specs,
        out_shape=out_shape,
        scratch_shapes=[pltpu.VMEM((2, tt, D_MODEL), BF16), pltpu.VMEM((2, SUBLANES, tt), I32)],
        compiler_params=_cparams(1),
        name="mix_route",
    )(x2d, an, rn, woa, wor, g, wrt, br, tri, low)


LOW_BITS = 4


def _start_piece_group(src_hbm, dst_buf, sem, pieces, nbits):
    def bit_copy(s8, d8, l8, c):
        size = 1 << c
        low = l8 & (size - 1)

        @pl.when((l8 & size) != 0)
        def _():
            pltpu.make_async_copy(src_hbm.at[pl.ds(s8 + low, size)], dst_buf.at[pl.ds(d8 + low, size)],
                                  sem).start()

    for piece in pieces:
        for c in range(min(LOW_BITS, nbits)):
            bit_copy(*piece, c)
    if nbits > LOW_BITS:
        def long_copies(_, carry):
            for piece in pieces:
                for c in range(LOW_BITS, nbits):
                    bit_copy(*piece, c)
            return carry

        any_len = functools.reduce(lambda a, b: a | b, [l8 for _, _, l8 in pieces])
        lax.fori_loop(0, jnp.where(any_len >= (1 << LOW_BITS), 1, 0), long_copies, 0)


def _start_piece(src_hbm, dst_buf, sem, s8, d8, l8, nbits):
    _start_piece_group(src_hbm, dst_buf, sem, [(s8, d8, l8)], nbits)


def _start_pieces(src_hbm, dst_buf, sem, p_lo, p_hi, psrc_ref, pdst_ref, plen_ref, nbits, unroll=1):
    static = isinstance(p_hi, int)
    assert not static or p_hi % unroll == 0
    masked = unroll > 1 and not static
    last = plen_ref.shape[0] - 1

    def body(g, carry):
        pieces = []
        for u in range(unroll):
            p = p_lo + g * unroll + u
            q = jnp.minimum(p, last) if masked else p
            l8 = jnp.where(p < p_hi, plen_ref[q], 0) if masked else plen_ref[q]
            pieces.append((psrc_ref[q], pdst_ref[q], l8))
        _start_piece_group(src_hbm, dst_buf, sem, pieces, nbits)
        return carry

    lax.fori_loop(0, p_hi // unroll if static else (p_hi - p_lo + (unroll - 1)) // unroll, body, 0)


def _wait_rows(src_hbm, dst_buf, sem, rows8, nbits):
    for c in range(nbits):
        size = 1 << c

        @pl.when((rows8 & size) != 0)
        def _():
            pltpu.make_async_copy(src_hbm.at[pl.ds(0, size)], dst_buf.at[pl.ds(0, size)], sem).wait()


def _moe_gmm_kernel(be_ref, rows_ref, wslot_ref, nxt_ref,
                    psa_ref, pea_ref, srca_ref, dsta_ref, lena_ref, hsrca_ref, hlena_ref,
                    psb_ref, peb_ref, srcb_ref, dstb_ref, lenb_ref, hsrcb_ref, hlenb_ref,
                    tsa_hbm, tsb_hbm, wgu_hbm, wdn_hbm, bg_ref, bu_ref, bd_ref, perm_ref,
                    ys_ref, lhs_ref, wgu_buf, wdn_buf, wg_ref, wu_ref, wd_ref, sem_ref, wsem_ref,
                    *, nbits_a, nbits_b):
    j = pl.program_id(0)
    nb = pl.num_programs(0)
    slot = j % 2
    bm = lhs_ref.shape[1] * SUBLANES

    def gather(blk, sl):
        for ts_hbm, ps, pe, src, dst, ln, hsrc, hlen, nbits in (
                (tsa_hbm, psa_ref, pea_ref, srca_ref, dsta_ref, lena_ref, hsrca_ref, hlena_ref, nbits_a),
                (tsb_hbm, psb_ref, peb_ref, srcb_ref, dstb_ref, lenb_ref, hsrcb_ref, hlenb_ref, nbits_b)):
            _start_pieces(ts_hbm, lhs_ref.at[sl], sem_ref.at[sl], ps[blk], pe[blk], src, dst, ln, nbits)
            _start_piece(ts_hbm, lhs_ref.at[sl], sem_ref.at[sl], hsrc[blk], 0, hlen[blk], nbits)

    def weight_copies(e, ws):
        return (pltpu.make_async_copy(wgu_hbm.at[e], wgu_buf.at[ws], wsem_ref.at[ws]),
                pltpu.make_async_copy(wdn_hbm.at[e], wdn_buf.at[ws], wsem_ref.at[ws]))

    @pl.when(j == 0)
    def _():
        lhs_ref[...] = jnp.zeros(lhs_ref.shape, U32)
        gather(0, 0)
        for cp in weight_copies(be_ref[0], wslot_ref[0]):
            cp.start()

    @pl.when(j + 1 < nb)
    def _():
        gather(j + 1, 1 - slot)

    @pl.when(jnp.logical_or(j == 0, be_ref[j] != be_ref[jnp.maximum(j - 1, 0)]))
    def _():
        ws = wslot_ref[j]
        for cp in weight_copies(be_ref[j], ws):
            cp.wait()
        nxt = nxt_ref[j]

        @pl.when(nxt >= 0)
        def _():
            for cp in weight_copies(nxt, 1 - ws):
                cp.start()

        perm = perm_ref[...]
        half = PERM_COLS // 2
        for c in range(2 * D_FF // PERM_COLS):
            wb = wgu_buf[ws, :, c * PERM_COLS:(c + 1) * PERM_COLS].astype(BF16)
            wp = jnp.dot(wb, perm, preferred_element_type=F32).astype(BF16)
            wg_ref[:, c * half:(c + 1) * half] = wp[:, :half]
            wu_ref[:, c * half:(c + 1) * half] = wp[:, half:]
        wd_ref[...] = wdn_buf[ws].astype(BF16)

    rows8 = rows_ref[j]
    _wait_rows(tsa_hbm, lhs_ref.at[slot], sem_ref.at[slot], rows8, (bm // SUBLANES).bit_length())

    def expert_rows(rs):
        x = _unpack_bf16_pairs(lhs_ref[slot, 0:rs.stop // SUBLANES].reshape(rs.stop, HALF))
        gate = jnp.dot(x, wg_ref[...], preferred_element_type=F32) + bg_ref[0]
        up = jnp.dot(x, wu_ref[...], preferred_element_type=F32) + bu_ref[0]
        gate = jnp.minimum(gate, SWIGLU_LIMIT)
        up = jnp.clip(up, -SWIGLU_LIMIT, SWIGLU_LIMIT)
        act = (up + 1.0) * (gate * jax.nn.sigmoid(SWIGLU_ALPHA * gate))
        y = jnp.dot(act.astype(BF16), wd_ref[...], preferred_element_type=F32) + bd_ref[0]
        ys_ref[rs, :] = _pack_bf16_pairs(y.astype(BF16).astype(F32))

    quarter = bm // 4
    for nq in range(5):
        lo8, hi8 = (nq - 1) * quarter // SUBLANES, nq * quarter // SUBLANES

        @pl.when(jnp.logical_and(rows8 > lo8, rows8 <= hi8) if nq else rows8 == 0)
        def _():
            if nq:
                expert_rows(slice(0, nq * quarter))
            if nq < 4:
                ys_ref[nq * quarter:, :] = jnp.zeros((bm - nq * quarter, HALF), U32)


def _moe_gmm(blocks, tabs_a, tabs_b, ts_a, ts_b, w_gu, w_dn, bg, bu, bd, perm, nblocks, bm, nbits_a, nbits_b):
    we = lambda j, be, *_: (be[j], 0, 0)
    grid_spec = pltpu.PrefetchScalarGridSpec(
        num_scalar_prefetch=18,
        grid=(nblocks,),
        in_specs=[pl.BlockSpec(memory_space=pl.ANY), pl.BlockSpec(memory_space=pl.ANY),
                  pl.BlockSpec(memory_space=pl.ANY), pl.BlockSpec(memory_space=pl.ANY),
                  pl.BlockSpec((1, 1, D_FF), we), pl.BlockSpec((1, 1, D_FF), we),
                  pl.BlockSpec((1, 1, D_MODEL), we),
                  pl.BlockSpec((PERM_COLS, PERM_COLS), lambda j, *_: (0, 0))],
        out_specs=pl.BlockSpec((bm, HALF), lambda j, *_: (j, 0)),
        scratch_shapes=[pltpu.VMEM((2, bm // SUBLANES, SUBLANES, HALF), U32),
                        pltpu.VMEM((2, D_MODEL, 2 * D_FF), F32), pltpu.VMEM((2, D_FF, D_MODEL), F32),
                        pltpu.VMEM((D_MODEL, D_FF), BF16), pltpu.VMEM((D_MODEL, D_FF), BF16),
                        pltpu.VMEM((D_FF, D_MODEL), BF16),
                        pltpu.SemaphoreType.DMA((2,)), pltpu.SemaphoreType.DMA((2,))],
    )
    return pl.pallas_call(
        functools.partial(_moe_gmm_kernel, nbits_a=nbits_a, nbits_b=nbits_b),
        grid_spec=grid_spec,
        out_shape=jax.ShapeDtypeStruct((nblocks * bm, HALF), U32),
        compiler_params=_cparams(1),
        name="moe_gmm",
    )(*blocks, *tabs_a, *tabs_b, ts_a.reshape(-1, SUBLANES, HALF), ts_b.reshape(-1, SUBLANES, HALF),
      w_gu, w_dn, bg, bu, bd, perm)


def _combine_kernel(psrc_ref, pdst_ref, plen_ref, tlo_ref, thi_ref, tsrc_ref, tdst_ref, tlen_ref, rows_ref,
                    ys_hbm, x2_ref, dest_ref, gate_ref, o_ref,
                    buf_ref, db_ref, gb_ref, sem_ref, *, nbits):
    i = pl.program_id(0)
    n = pl.num_programs(0)
    slot = i % 2
    tt = x2_ref.shape[0]
    tile_rows = buf_ref.shape[1] * SUBLANES

    def gather(tile, sl):
        _start_pieces(ys_hbm, buf_ref.at[sl], sem_ref.at[sl], tile * N_EXPERTS, N_EXPERTS,
                      psrc_ref, pdst_ref, plen_ref, nbits, unroll=4)
        _start_pieces(ys_hbm, buf_ref.at[sl], sem_ref.at[sl], tlo_ref[tile], thi_ref[tile],
                      tsrc_ref, tdst_ref, tlen_ref, nbits)

    @pl.when(i == 0)
    def _():
        buf_ref[...] = jnp.zeros(buf_ref.shape, U32)
        gather(0, 0)

    @pl.when(i + 1 < n)
    def _():
        gather(i + 1, 1 - slot)

    _wait_rows(ys_hbm, buf_ref.at[slot], sem_ref.at[slot], rows_ref[i], (tile_rows // SUBLANES).bit_length())

    rows = jnp.concatenate([dest_ref[0].astype(F32), gate_ref[0],
                            jnp.zeros((LANES - 2 * TOP_K, tt), F32)], axis=0)
    cols = rows.T
    for k in range(TOP_K):
        db_ref[k] = jnp.broadcast_to(cols[:, k:k + 1], (tt, DISPATCH_CHUNK)).astype(I32).astype(I16)
        gb_ref[k] = jnp.broadcast_to(cols[:, TOP_K + k:TOP_K + k + 1], (tt, DISPATCH_CHUNK)).astype(BF16)
    li = lax.broadcasted_iota(I32, (tt, DISPATCH_CHUNK), 1).astype(I16)
    gms = []
    for c in range(tile_rows // DISPATCH_CHUNK):
        lic = li + jnp.int16(c * DISPATCH_CHUNK)
        gm = jnp.zeros((tt, DISPATCH_CHUNK), BF16)
        for k in range(TOP_K):
            gm = jnp.where(lic == db_ref[k], gb_ref[k], gm)
        gms.append(gm)
    o_ref[...] = x2_ref[...] + jnp.dot(jnp.concatenate(gms, axis=1),
                                       _unpack_bf16_pairs(buf_ref[slot].reshape(tile_rows, HALF)),
                                       preferred_element_type=F32)


def _combine(tabs, ys, x2, dest, gate, tt, nbits):
    n = x2.shape[0]
    nt = n // tt
    tile_rows = _tile_rows(tt)
    grid_spec = pltpu.PrefetchScalarGridSpec(
        num_scalar_prefetch=len(tabs),
        grid=(nt,),
        in_specs=[pl.BlockSpec(memory_space=pl.ANY),
                  pl.BlockSpec((tt, D_MODEL), lambda i, *_: (i, 0)),
                  pl.BlockSpec((1, TOP_K, tt), lambda i, *_: (i, 0, 0)),
                  pl.BlockSpec((1, TOP_K, tt), lambda i, *_: (i, 0, 0))],
        out_specs=pl.BlockSpec((tt, D_MODEL), lambda i, *_: (i, 0)),
        scratch_shapes=[pltpu.VMEM((2, tile_rows // SUBLANES, SUBLANES, HALF), U32),
                        pltpu.VMEM((TOP_K, tt, DISPATCH_CHUNK), I16), pltpu.VMEM((TOP_K, tt, DISPATCH_CHUNK), BF16),
                        pltpu.SemaphoreType.DMA((2,))],
    )
    return pl.pallas_call(
        functools.partial(_combine_kernel, nbits=nbits),
        grid_spec=grid_spec,
        out_shape=jax.ShapeDtypeStruct((n, D_MODEL), F32),
        compiler_params=_cparams(1),
        name="combine",
    )(*tabs, ys.reshape(-1, SUBLANES, HALF), x2, dest, gate)


def _piece_tables(n8_a, off_a, rows_a, n8_b, off_b, bm, nblocks):
    nta = n8_a.shape[0]
    n8 = jnp.concatenate([n8_a, n8_b], axis=0)
    seg_off = jnp.concatenate([off_a, off_b], axis=0)
    n_tiles = n8.shape[0]
    tile_base = jnp.concatenate([jnp.arange(nta, dtype=I32) * rows_a, jnp.zeros((n_tiles - nta,), I32)])[:, None]
    tot = jnp.sum(n8, axis=0)
    pos0 = jnp.cumsum(n8, axis=0) - n8
    nblk = (tot + bm - 1) // bm
    cs = jnp.cumsum(nblk)
    bs = cs - nblk
    kblk = pos0 // bm
    len0 = jnp.minimum(n8, (kblk + 1) * bm - pos0)
    len1 = n8 - len0
    b0 = bs[None, :] + kblk
    src0 = tile_base + seg_off
    in_blk = pos0 - kblk * bm
    jj = jnp.arange(nblocks, dtype=I32)
    i32 = lambda v: v.astype(I32)

    def gmm_tabs(sl):
        em = lambda v: v[sl].T.reshape(-1)
        blk_em = em(b0)
        first = i32(jnp.sum(blk_em[None, :] < jj[:, None], axis=1))
        last = i32(jnp.sum(blk_em[None, :] <= jj[:, None], axis=1))
        hit = (blk_em[None, :] + 1 == jj[:, None]) & (em(len1)[None, :] > 0)
        tail_src = i32(jnp.sum(jnp.where(hit, em(src0 + len0)[None, :], 0), axis=1) // SUBLANES)
        tail_len = i32(jnp.sum(jnp.where(hit, em(len1)[None, :], 0), axis=1) // SUBLANES)
        return (first, last, i32(em(src0) // SUBLANES), i32(em(in_blk) // SUBLANES), i32(em(len0) // SUBLANES),
                tail_src, tail_len)

    def comb_tabs(sl):
        tm = lambda v: v[sl].reshape(-1)
        has_tail = len1[sl] > 0
        cnt = jnp.sum(has_tail, axis=1)
        lo = jnp.cumsum(cnt) - cnt
        slot_ = lo[:, None] + jnp.cumsum(has_tail, axis=1) - has_tail
        hit = (slot_.reshape(-1)[None, :] == jj[:, None]) & has_tail.reshape(-1)[None, :]
        pick = lambda v: i32(jnp.sum(jnp.where(hit, tm(v)[None, :], 0), axis=1) // SUBLANES)
        return (i32(tm(b0 * bm + in_blk) // SUBLANES), i32(tm(seg_off) // SUBLANES), i32(tm(len0) // SUBLANES),
                i32(lo), i32(lo + cnt), pick((b0 + 1) * bm), pick(seg_off + len0), pick(len1),
                i32(jnp.sum(n8[sl], axis=1) // SUBLANES))

    count_le = lambda v: jnp.sum(cs[None, :] <= v[:, None], axis=1)
    n_active = cs[-1]
    e_last = count_le(jnp.maximum(n_active - 1, 0).reshape(1))[0]
    block_e = jnp.minimum(count_le(jj), e_last).astype(I32)
    ee = jnp.arange(N_EXPERTS, dtype=I32)
    mine = (jj[:, None] >= bs[None, :]) & (jj[:, None] < cs[None, :])
    left = jnp.clip(tot[None, :] - (jj[:, None] - bs[None, :]) * bm, 0, bm)
    rows8 = (jnp.sum(jnp.where(mine, left, 0), axis=1) // SUBLANES).astype(I32)
    has = nblk > 0
    run = jnp.cumsum(has.astype(I32)) - 1
    later = (ee[None, :] > ee[:, None]) & has[None, :]
    nxt_e = jnp.min(jnp.where(later, ee[None, :], N_EXPERTS), axis=1)
    nxt_e = jnp.where(nxt_e == N_EXPERTS, -1, nxt_e)
    own = block_e[:, None] == ee[None, :]
    wslot = (jnp.sum(jnp.where(own, run[None, :], 0), axis=1) % 2).astype(I32)
    nxt = jnp.sum(jnp.where(own, nxt_e[None, :], 0), axis=1).astype(I32)
    a, b = slice(0, nta), slice(nta, n_tiles)
    return (block_e, rows8, wslot, nxt), gmm_tabs(a), gmm_tabs(b), comb_tabs(a), comb_tabs(b)


def _block_diag(w):
    nb, bi, bo = w.shape
    eye = jnp.eye(nb, dtype=w.dtype)
    return (eye[:, None, :, None] * w[:, :, None, :]).reshape(nb * bi, nb * bo)


def _step(x_prompt, x_sample, cache_k, cache_v, state_conv, state_h, g_mix_norm, w_in, g_q_norm, g_k_norm,
          attn_sinks, conv_w, conv_b, w_lru_a, b_lru_a, w_lru_x, b_lru_x, lru_lambda, g_attn_out, g_rnn_out,
          w_out, g_ffn_norm, w_router, b_router, w_gate_up, b_gate_up, w_down, b_down,
          *, tm, tt, bm, past_len):
    B, S, D = x_prompt.shape
    NS = x_sample.shape[0]
    assert x_sample.shape[1] == 1 and D == D_MODEL
    assert (B * S) % tt == 0 and S % tm == 0 and tm % ATTN_BLOCK == 0
    assert NS % SUBLANES == 0 and tt <= bm
    assert tt % SUBLANES == 0 and NS <= bm
    n_pt = (B * S) // tt
    total_rows = TOP_K * (B * S + NS) + (n_pt + 1) * N_EXPERTS * (SUBLANES - 1)
    nblocks = -(-total_rows // bm) + N_EXPERTS
    nbits_p = (tt // SUBLANES).bit_length()
    nbits_s = (NS // SUBLANES).bit_length()

    l = 0
    row = lambda v: v[l].reshape(1, -1)
    w_in_bf = w_in[l].astype(BF16)
    gq2 = jnp.tile(g_q_norm[l], 2).reshape(1, LANES)
    gk2 = jnp.tile(g_k_norm[l], 2).reshape(1, LANES)
    wa = _block_diag(w_lru_a[l]).astype(BF16)
    wx = _block_diag(w_lru_x[l]).astype(BF16)
    ba = b_lru_a[l].reshape(1, D_RNN)
    bx = b_lru_x[l].reshape(1, D_RNN)
    wo = w_out[l].astype(BF16)
    woa, wor = wo[:D_ATTN], wo[D_ATTN:]
    wr = w_router[l].T
    wr_hi = wr.astype(BF16)
    wrt = jnp.concatenate([wr_hi, (wr - wr_hi.astype(F32)).astype(BF16)], axis=0)
    br = b_router[l].reshape(N_EXPERTS, 1)
    low = jnp.tril(jnp.ones((N_EXPERTS, N_EXPERTS), BF16), k=-1)
    bgu = b_gate_up[l].reshape(N_EXPERTS, D_FF, 2)
    bg = bgu[:, :, 0].reshape(N_EXPERTS, 1, D_FF)
    bu = bgu[:, :, 1].reshape(N_EXPERTS, 1, D_FF)
    bd = b_down[l].reshape(N_EXPERTS, 1, D_MODEL)
    half = PERM_COLS // 2
    pr = jnp.arange(PERM_COLS)
    perm = (pr[None, :] == jnp.where(pr % 2 == 0, pr // 2, half + pr // 2)[:, None]).astype(BF16)
    sinks = attn_sinks[l]

    ctab, s1tab, s2tab = _rope_tables(jnp.arange(S))
    an, rn, kt_p, vt_p, h_last_p, xr_tail = _front(
        x_prompt.reshape(B * S, D), sinks, row(g_mix_norm), w_in_bf, gq2, gk2, ctab, s1tab, s2tab,
        row(g_attn_out), conv_w[l], row(conv_b), wa, ba, wx, bx, row(lru_lambda), row(g_rnn_out), B, S, tm)
    x2_p, ts_p, dest_p, gate_p, n8_p, off_p = _mix_route(
        x_prompt.reshape(B * S, D), an, rn, woa, wor, row(g_ffn_norm), wrt, br, low, tt)

    cs_tab = _rope_tables(jnp.full((NS,), past_len, I32))
    q_s, k_s, v_s, xr_s, yr_s = _in_proj(x_sample.reshape(NS, D), row(g_mix_norm), w_in_bf, gq2, gk2,
                                         *cs_tab, NS)
    to_rows = lambda c: jnp.transpose(c, (0, 2, 3, 1)).reshape(NS * KV_W, WINDOW)
    from_rows = lambda c, n: jnp.transpose(c.reshape(n, N_KV_HEADS, HEAD_DIM, WINDOW), (0, 3, 1, 2))[None]
    an_s, kt_s, vt_s = _attn_sample(q_s, k_s, v_s, to_rows(cache_k[l]), to_rows(cache_v[l]), sinks,
                                    row(g_attn_out), min(NS, SAMPLE_ATTN_SEQS))
    rn_s, h_last_s, hist_s = _rnn_sample(xr_s, yr_s, jnp.transpose(state_conv[l], (1, 0, 2)), state_h[l],
                                         conv_w[l], row(conv_b), wa, ba, wx, bx, row(lru_lambda),
                                         row(g_rnn_out))
    x2_s, ts_s, dest_s, gate_s, n8_s, off_s = _mix_route(
        x_sample.reshape(NS, D), an_s, rn_s, woa, wor, row(g_ffn_norm), wrt, br, low, NS)

    blocks, gmm_p, gmm_s, comb_p, comb_s = _piece_tables(
        n8_p[:, :, 0], off_p[:, :, 0], _tile_rows(tt), n8_s[:, :, 0], off_s[:, :, 0], bm, nblocks)
    ys = _moe_gmm(blocks, gmm_p, gmm_s, ts_p, ts_s, w_gate_up[l], w_down[l], bg, bu, bd, perm,
                  nblocks, bm, nbits_p, nbits_s)
    y_p = _combine(comb_p, ys, x2_p, dest_p, gate_p, tt, nbits_p)
    y_s = _combine(comb_s, ys, x2_s, dest_s, gate_s, NS, nbits_s)

    cp = xr_tail[:, SUBLANES - (CONV_WIDTH - 1):]
    return (y_p.reshape(B, S, D), y_s.reshape(NS, 1, D),
            from_rows(kt_p, B), from_rows(vt_p, B), cp[None], h_last_p.reshape(1, B, D_RNN),
            from_rows(kt_s, NS), from_rows(vt_s, NS), jnp.transpose(hist_s, (1, 0, 2))[None], h_last_s[None])


def kernel(x_prompt, x_sample, cache_k, cache_v, state_conv, state_h, g_mix_norm, w_in, g_q_norm, g_k_norm, attn_sinks, conv_w, conv_b, w_lru_a, b_lru_a, w_lru_x, b_lru_x, lru_lambda, g_attn_out, g_rnn_out, w_out, g_ffn_norm, w_router, b_router, w_gate_up, b_gate_up, w_down, b_down):
    return _step(x_prompt, x_sample, cache_k, cache_v, state_conv, state_h, g_mix_norm, w_in, g_q_norm,
                 g_k_norm, attn_sinks, conv_w, conv_b, w_lru_a, b_lru_a, w_lru_x, b_lru_x, lru_lambda,
                 g_attn_out, g_rnn_out, w_out, g_ffn_norm, w_router, b_router, w_gate_up, b_gate_up,
                 w_down, b_down, tm=512, tt=512, bm=MOE_BLOCK_ROWS, past_len=PAST_LEN)
```

```python
import functools

import jax
import jax.numpy as jnp
from jax import lax
from jax.experimental import pallas as pl
from jax.experimental.pallas import tpu as pltpu

F32 = jnp.float32
BF16 = jnp.bfloat16
I32 = jnp.int32
I16 = jnp.int16

D_MODEL = 1024
HEAD_DIM = 64
N_HEADS = 8
N_KV_HEADS = 2
GROUP = 4
WINDOW = 128
ATTN_BLOCK = 128
ROT_DIM = 16
ROPE_THETA = 500000.0
D_ATTN = 512
D_RNN = 512
KV_W = 128
D_IN = 1792
CONV_WIDTH = 4
LRU_C = 8.0
N_EXPERTS = 32
TOP_K = 4
D_FF = 1024
SWIGLU_LIMIT = 7.0
SWIGLU_ALPHA = 1.702
EPS = 1e-6
PAST_LEN = 8192

LANES = 128
SUBLANES = 8
NEG_BIG = -1e30
VMEM_LIMIT = 56 * 1024 * 1024

MOE_BLOCK_ROWS = 512
PERM_COLS = 256
DISPATCH_CHUNK = 256
SAMPLE_ATTN_SEQS = 32


def _cparams(n_axes):
    return pltpu.CompilerParams(dimension_semantics=("arbitrary",) * n_axes,
                                vmem_limit_bytes=VMEM_LIMIT)


U32 = jnp.uint32
HALF = D_MODEL // 2
HI16 = 0xFFFF0000


def _pack_bf16_pairs(x):
    bits = pltpu.bitcast(x, U32)
    return (bits[:, HALF:] & U32(HI16)) | (bits[:, :HALF] >> 16)


def _unpack_bf16_pairs(u):
    lo = pltpu.bitcast(u << 16, F32)
    hi = pltpu.bitcast(u & U32(HI16), F32)
    return jnp.concatenate([lo, hi], axis=1).astype(BF16)


def _rmsnorm(x, g):
    ms = jnp.mean(x * x, axis=-1, keepdims=True)
    return (x * lax.rsqrt(ms + EPS)) * g


def _head_norm_rope(t, g, c, s1, s2, lo):
    sq = t * t
    s_lo = jnp.sum(jnp.where(lo, sq, 0.0), axis=-1, keepdims=True)
    s_hi = jnp.sum(jnp.where(lo, 0.0, sq), axis=-1, keepdims=True)
    ms = jnp.where(lo, s_lo, s_hi) * (1.0 / HEAD_DIM)
    n = (t * lax.rsqrt(ms + EPS)) * g
    up = pltpu.roll(n, LANES - ROT_DIM // 2, 1)
    dn = pltpu.roll(n, ROT_DIM // 2, 1)
    return n * c + up * s1 + dn * s2


def _in_proj_kernel(x_ref, g_ref, w_ref, gq_ref, gk_ref, c_ref, s1_ref, s2_ref,
                    q_ref, k_ref, v_ref, xr_ref, yr_ref):
    tm = x_ref.shape[0]
    h = _rmsnorm(x_ref[...], g_ref[...])
    proj = jnp.dot(h.astype(BF16), w_ref[...], preferred_element_type=F32)
    rope = (c_ref[...], s1_ref[...], s2_ref[...], lax.broadcasted_iota(I32, (tm, LANES), 1) < HEAD_DIM)
    gq = gq_ref[...]
    for j in range(D_ATTN // LANES):
        q_ref[:, j * LANES:(j + 1) * LANES] = _head_norm_rope(proj[:, j * LANES:(j + 1) * LANES], gq, *rope)
    k_ref[...] = _head_norm_rope(proj[:, D_ATTN:D_ATTN + KV_W], gk_ref[...], *rope)
    v_ref[...] = proj[:, D_ATTN + KV_W:D_ATTN + 2 * KV_W]
    o = D_ATTN + 2 * KV_W
    xr_ref[...] = proj[:, o:o + D_RNN]
    yr_ref[...] = proj[:, o + D_RNN:o + 2 * D_RNN]


def _in_proj(x2d, g, w_bf, gq2, gk2, ctab, s1tab, s2tab, tm):
    n = x2d.shape[0]
    ntab = ctab.shape[0] // tm
    row = lambda i: (i, 0)
    fix = lambda i: (0, 0)
    tab = lambda i: (i % ntab, 0)
    out_shapes = (jax.ShapeDtypeStruct((n, D_ATTN), F32), jax.ShapeDtypeStruct((n, KV_W), F32),
                  jax.ShapeDtypeStruct((n, KV_W), F32), jax.ShapeDtypeStruct((n, D_RNN), F32),
                  jax.ShapeDtypeStruct((n, D_RNN), F32))
    return pl.pallas_call(
        _in_proj_kernel,
        grid=(n // tm,),
        in_specs=[pl.BlockSpec((tm, D_MODEL), row), pl.BlockSpec((1, D_MODEL), fix),
                  pl.BlockSpec((D_MODEL, D_IN), fix), pl.BlockSpec((1, LANES), fix),
                  pl.BlockSpec((1, LANES), fix), pl.BlockSpec((tm, LANES), tab),
                  pl.BlockSpec((tm, LANES), tab), pl.BlockSpec((tm, LANES), tab)],
        out_specs=(pl.BlockSpec((tm, D_ATTN), row), pl.BlockSpec((tm, KV_W), row),
                   pl.BlockSpec((tm, KV_W), row), pl.BlockSpec((tm, D_RNN), row),
                   pl.BlockSpec((tm, D_RNN), row)),
        out_shape=out_shapes,
        compiler_params=_cparams(1),
        name="in_proj",
    )(x2d, g, w_bf, gq2, gk2, ctab, s1tab, s2tab)


def _rope_tables(pos):
    half = ROT_DIM // 2
    inv = ROPE_THETA ** (-jnp.arange(0, ROT_DIM, 2, dtype=F32) / ROT_DIM)
    d = jnp.arange(LANES) % HEAD_DIM
    ang = pos.astype(F32)[:, None] * inv[d % half][None, :]
    cos = jnp.cos(ang)
    sin = jnp.sin(ang)
    c = jnp.where(d < ROT_DIM, cos, 1.0)
    s1 = jnp.where(d < half, -sin, 0.0)
    s2 = jnp.where((d >= half) & (d < ROT_DIM), sin, 0.0)
    return c, s1, s2


def _band_bias(qb):
    qi = jnp.arange(qb, dtype=I32)[:, None]
    c = jnp.arange(2 * qb, dtype=I32)[None, :]
    band = (c >= qi) & (c <= qi + qb)
    first = band & (c >= qb)
    one = jnp.where(jnp.stack([first, band]), 0.0, NEG_BIG).astype(F32)
    return jnp.concatenate([one, one], axis=2)


def _attn_sample_kernel(sink_ref, q_ref, kn_ref, vn_ref, kt_ref, vt_ref, g_ref,
                        o_ref, nkt_ref, nvt_ref, acc_ref):
    bb = q_ref.shape[0]
    q = q_ref[...] * (HEAD_DIM ** -0.5)
    kn = kn_ref[...]
    vn = vn_ref[...]
    kt = kt_ref[...]
    vt = vt_ref[...]
    col = lax.broadcasted_iota(I32, (bb, bb * KV_W), 1)
    rowb = lax.broadcasted_iota(I32, (bb, bb * KV_W), 0)
    own_seq = (col >> (KV_W.bit_length() - 1)) == rowb
    half_hi = ((col >> (HEAD_DIM.bit_length() - 1)) & 1) == 1
    qbig = []
    for h in range(N_HEADS):
        kv = h // GROUP
        pair = q[:, (h // 2) * LANES:(h // 2 + 1) * LANES]
        if (h % 2) != kv:
            pair = pltpu.roll(pair, HEAD_DIM, 1)
        tiled = jnp.concatenate([pair] * bb, axis=1)
        keep = own_seq & (half_hi if kv == 1 else jnp.logical_not(half_hi))
        qbig.append(jnp.where(keep, tiled, 0.0))
    qbig = jnp.concatenate(qbig, axis=0)
    s = jnp.dot(qbig.astype(BF16), kt.astype(BF16), preferred_element_type=F32)
    qb16 = q.astype(BF16).astype(F32)
    kb16 = kn.astype(BF16).astype(F32)
    s_new, sink = [], []
    for h in range(N_HEADS):
        kv = h // GROUP
        s_new.append(jnp.sum(qb16[:, h * HEAD_DIM:(h + 1) * HEAD_DIM] * kb16[:, kv * HEAD_DIM:(kv + 1) * HEAD_DIM],
                             axis=-1, keepdims=True))
        sink.append(jnp.full((bb, 1), sink_ref[h], F32))
    s_new = jnp.concatenate(s_new, axis=0)
    sink = jnp.concatenate(sink, axis=0)
    m = jnp.maximum(jnp.maximum(jnp.max(s, axis=-1, keepdims=True), s_new), sink)
    e = jnp.exp(s - m)
    e_new = jnp.exp(s_new - m)
    inv = 1.0 / (jnp.sum(e, axis=-1, keepdims=True) + e_new + jnp.exp(sink - m))
    obig = lax.dot_general(e.astype(BF16), vt.astype(BF16), (((1,), (1,)), ((), ())),
                           preferred_element_type=F32)
    for h in range(N_HEADS):
        kv = h // GROUP
        blk = jnp.where(own_seq, obig[h * bb:(h + 1) * bb, :], 0.0)
        fold = blk[:, 0:KV_W]
        for t in range(1, bb):
            fold = fold + blk[:, t * KV_W:(t + 1) * KV_W]
        hs = slice(h * bb, (h + 1) * bb)
        ks = slice(kv * HEAD_DIM, (kv + 1) * HEAD_DIM)
        acc_ref[:, h * HEAD_DIM:(h + 1) * HEAD_DIM] = (fold[:, ks] + e_new[hs] * vn[:, ks]) * inv[hs]
    o_ref[...] = _rmsnorm(acc_ref[...], g_ref[...])

    last = lax.broadcasted_iota(I32, (KV_W, WINDOW), 1) == WINDOW - 1
    for b in range(bb):
        rs = slice(b * KV_W, (b + 1) * KV_W)
        kcol = jnp.broadcast_to(kn[b:b + 1, :], (KV_W, KV_W)).T
        vcol = jnp.broadcast_to(vn[b:b + 1, :], (KV_W, KV_W)).T
        nkt_ref[rs, :] = jnp.where(last, kcol, pltpu.roll(kt[rs, :], WINDOW - 1, 1))
        nvt_ref[rs, :] = jnp.where(last, vcol, pltpu.roll(vt[rs, :], WINDOW - 1, 1))


def _attn_sample(q, kn, vn, kt2d, vt2d, sinks, g_attn, bb):
    n = q.shape[0]
    row = lambda i: (i, 0)
    fix = lambda i: (0, 0)
    cache = pl.BlockSpec((bb * KV_W, WINDOW), row)
    return pl.pallas_call(
        _attn_sample_kernel,
        grid=(n // bb,),
        in_specs=[pl.BlockSpec(memory_space=pltpu.SMEM),
                  pl.BlockSpec((bb, D_ATTN), row), pl.BlockSpec((bb, KV_W), row),
                  pl.BlockSpec((bb, KV_W), row), cache, cache,
                  pl.BlockSpec((1, D_ATTN), fix)],
        out_specs=(pl.BlockSpec((bb, D_ATTN), row), cache, cache),
        out_shape=(jax.ShapeDtypeStruct((n, D_ATTN), F32),
                   jax.ShapeDtypeStruct(kt2d.shape, F32), jax.ShapeDtypeStruct(vt2d.shape, F32)),
        scratch_shapes=[pltpu.VMEM((bb, D_ATTN), F32)],
        compiler_params=_cparams(1),
        name="attn_sample",
    )(sinks, q, kn, vn, kt2d, vt2d, g_attn)


def _softplus(z):
    return jnp.maximum(z, 0.0) + jnp.log1p(jnp.exp(-jnp.abs(z)))


def _lru_gates(xc, wa_ref, ba_ref, wx_ref, bx_ref, lam_ref):
    xb = xc.astype(BF16)
    r = jax.nn.sigmoid(jnp.dot(xb, wa_ref[...], preferred_element_type=F32) + ba_ref[...])
    i = jax.nn.sigmoid(jnp.dot(xb, wx_ref[...], preferred_element_type=F32) + bx_ref[...])
    log_a = (-LRU_C * r) * _softplus(-lam_ref[...])
    a = jnp.exp(log_a)
    z = -jnp.tanh(log_a) * (a * a + 1.0)
    u = jnp.where(z > 0.0, z * lax.rsqrt(z), 0.0) * (i * xc)
    return a, u


def _lru_scan(a, u, h0):
    ng = a.shape[0] // SUBLANES
    a3 = a.reshape(ng, SUBLANES, D_RNN)
    u3 = u.reshape(ng, SUBLANES, D_RNN)
    t8 = lax.broadcasted_iota(I32, (ng, SUBLANES, D_RNN), 1)
    d = 1
    while d < SUBLANES:
        a_s = jnp.where(t8 >= d, pltpu.roll(a3, d, 1), 1.0)
        u_s = jnp.where(t8 >= d, pltpu.roll(u3, d, 1), 0.0)
        u3 = a3 * u_s + u3
        a3 = a3 * a_s
        d *= 2
    carry = h0
    groups = []
    for g in range(ng):
        hg = a3[g] * carry + u3[g]
        groups.append(hg)
        carry = hg[SUBLANES - 1:SUBLANES, :]
    return jnp.concatenate(groups, axis=0), carry


def _lru_scan_tiles(a_ref, u_ref, h_ref, h0):
    nl, rows, _ = a_ref.shape
    ng = rows // SUBLANES
    step = lambda ref, s: jnp.concatenate(
        [ref[j, pl.ds(s, ng, stride=SUBLANES), :] for j in range(nl)], axis=1)
    prods = [step(a_ref, 0)]
    locs = [step(u_ref, 0)]
    for s in range(1, SUBLANES):
        a_s = step(a_ref, s)
        locs.append(a_s * locs[-1] + step(u_ref, s))
        prods.append(a_s * prods[-1])
    after, h_last = _lru_scan(prods[-1], locs[-1], h0)
    row = lax.broadcasted_iota(I32, (ng, D_RNN), 0)
    before = jnp.where(row == 0, h0, pltpu.roll(after, 1, 0))
    for s in range(SUBLANES):
        h_s = locs[s] + prods[s] * before
        for j in range(nl):
            h_ref[j, pl.ds(s, ng, stride=SUBLANES), :] = h_s[:, j * LANES:(j + 1) * LANES]
    return h_last


def _to_lane_tiles(ref, x):
    for j in range(ref.shape[0]):
        ref[j] = x[:, j * LANES:(j + 1) * LANES]


def _rnn_sample_kernel(xr_ref, yr_ref, hist_ref, h0_ref, cw_ref, cb_ref, wa_ref, ba_ref, wx_ref, bx_ref,
                       lam_ref, g_ref, o_ref, hl_ref, nh_ref):
    cw = cw_ref[...]
    xr = xr_ref[...]
    xc = cb_ref[...] + xr * cw[CONV_WIDTH - 1:CONV_WIDTH, :]
    for w in range(CONV_WIDTH - 1):
        xc = xc + hist_ref[w] * cw[w:w + 1, :]
    a, u = _lru_gates(xc, wa_ref, ba_ref, wx_ref, bx_ref, lam_ref)
    h = a * h0_ref[...] + u
    hl_ref[...] = h
    o_ref[...] = _rmsnorm(jax.nn.gelu(yr_ref[...]) * h, g_ref[...])
    for w in range(CONV_WIDTH - 2):
        nh_ref[w] = hist_ref[w + 1]
    nh_ref[CONV_WIDTH - 2] = xr


def _rnn_sample(xr, yr, hist, h0, cw, cb, wa, ba, wx, bx, lam, g):
    n = xr.shape[0]
    full = lambda a: pl.BlockSpec(a.shape, lambda: (0,) * a.ndim)
    args = (xr, yr, hist, h0, cw, cb, wa, ba, wx, bx, lam, g)
    return pl.pallas_call(
        _rnn_sample_kernel,
        in_specs=[full(a) for a in args],
        out_specs=(pl.BlockSpec((n, D_RNN), lambda: (0, 0)), pl.BlockSpec((n, D_RNN), lambda: (0, 0)),
                   pl.BlockSpec(hist.shape, lambda: (0, 0, 0))),
        out_shape=(jax.ShapeDtypeStruct((n, D_RNN), F32), jax.ShapeDtypeStruct((n, D_RNN), F32),
                   jax.ShapeDtypeStruct(hist.shape, F32)),
        compiler_params=pltpu.CompilerParams(vmem_limit_bytes=VMEM_LIMIT),
        name="rnn_sample",
    )(*args)


def _front_kernel(*refs, tiles_per_seq, nt):
    i = pl.program_id(0)
    q_s, k_s, v_s, xr_s, yr_s, ext_ref, h_ref = refs[25:32]

    @pl.when(i == 0)
    def _():
        for r in (q_s, k_s, v_s, xr_s, yr_s, ext_ref, h_ref):
            r[...] = jnp.zeros(r.shape, F32)
        _front_body(0, 1, *refs, tiles_per_seq=tiles_per_seq, tail=False)

    @pl.when(i == nt)
    def _():
        _front_body(nt % 2, 1 - nt % 2, *refs, tiles_per_seq=tiles_per_seq, proj=False)

    for cur in range(2):
        @pl.when((i % 2 == cur) & (i > 0) & (i < nt))
        def _():
            _front_body(cur, 1 - cur, *refs, tiles_per_seq=tiles_per_seq)


def _front_body(cur, prv, sink_ref, x_ref, gm_ref, w_ref, gq_ref, gk_ref, c_ref, s1_ref, s2_ref, bias_ref,
                ga_ref, cw_ref, cb_ref, wa_ref, ba_ref, wx_ref, bx_ref, lam_ref, gr_ref,
                an_ref, rn_ref, kt_ref, vt_ref, hl_ref, cx_ref,
                q_s, k_s, v_s, xr_s, yr_s, ext_ref, h_ref, s_ref, e_ref, a_scr, u_scr, hs_scr,
                *, tiles_per_seq, proj=True, tail=True):
    i = pl.program_id(0)
    tm = x_ref.shape[0]
    qb = ATTN_BLOCK
    t = jnp.maximum(i - 1, 0)
    first_tile = (t % tiles_per_seq) == 0

    if proj:
        hx = _rmsnorm(x_ref[...], gm_ref[...]).astype(BF16)
        rope = (c_ref[...], s1_ref[...], s2_ref[...], lax.broadcasted_iota(I32, (tm, LANES), 1) < HEAD_DIM)
        project = lambda c0, c1: jnp.dot(hx, w_ref[:, c0:c1], preferred_element_type=F32)

    nqb = tm // qb
    n_pairs = N_HEADS // 2
    lo_k = lax.broadcasted_iota(I32, (2 * qb, LANES), 1) < HEAD_DIM
    lo_q = lax.broadcasted_iota(I32, (qb, LANES), 1) < HEAD_DIM
    nt_dims = (((1,), (1,)), ((), ()))
    vbds = []
    for jb in range(nqb if tail else 0):
        k2 = k_s[prv, jb * qb:(jb + 2) * qb, :]
        v2 = v_s[prv, jb * qb:(jb + 2) * qb, :]
        k2r = pltpu.roll(k2, HEAD_DIM, 1)
        v2r = pltpu.roll(v2, HEAD_DIM, 1)
        bias = bias_ref[jnp.where(first_tile, 0, 1)] if jb == 0 else bias_ref[1]
        for kv in range(N_KV_HEADS):
            ka, kb = (k2, k2r) if kv == 0 else (k2r, k2)
            va, vb = (v2, v2r) if kv == 0 else (v2r, v2)
            kbd = jnp.concatenate([jnp.where(lo_k, ka, 0.0), jnp.where(lo_k, 0.0, kb)], axis=0).astype(BF16)
            vbds.append(jnp.concatenate([jnp.where(lo_k, va, 0.0), jnp.where(lo_k, 0.0, vb)],
                                        axis=0).astype(BF16))
            for p in range(GROUP // 2):
                pp = kv * (GROUP // 2) + p
                qp = (q_s[prv, jb * qb:(jb + 1) * qb, pp * LANES:(pp + 1) * LANES]
                      * (HEAD_DIM ** -0.5)).astype(BF16)
                s_ref[jb * n_pairs + pp] = lax.dot_general(qp, kbd, nt_dims,
                                                           preferred_element_type=F32) + bias
    if proj:
        pq = project(0, D_ATTN)
        for j in range(D_ATTN // LANES):
            q_s[cur, :, j * LANES:(j + 1) * LANES] = _head_norm_rope(pq[:, j * LANES:(j + 1) * LANES],
                                                                     gq_ref[...], *rope)
    invs = []
    for c in range(nqb * n_pairs if tail else 0):
        pp = c % n_pairs
        inv = []
        for tpos in range(2):
            cols = slice(tpos * 2 * qb, (tpos + 1) * 2 * qb)
            st = s_ref[c, :, cols]
            sink = sink_ref[2 * pp + tpos]
            m = jnp.maximum(jnp.max(st, axis=-1, keepdims=True), sink)
            e = jnp.exp(st - m)
            e_ref[c, :, cols] = e.astype(BF16)
            inv.append(1.0 / (jnp.sum(e, axis=-1, keepdims=True) + jnp.exp(sink - m)))
        invs.append(jnp.where(lo_q, inv[0], inv[1]))
    if proj:
        pkv = project(D_ATTN, D_ATTN + 2 * KV_W)
        k_s[cur, qb:qb + tm, :] = _head_norm_rope(pkv[:, :KV_W], gk_ref[...], *rope)
        v_s[cur, qb:qb + tm, :] = pkv[:, KV_W:]
        k_s[cur, 0:qb, :] = k_s[prv, tm:tm + qb, :]
        v_s[cur, 0:qb, :] = v_s[prv, tm:tm + qb, :]
    for jb in range(nqb if tail else 0):
        outs = [jnp.dot(e_ref[jb * n_pairs + pp], vbds[jb * N_KV_HEADS + pp // (GROUP // 2)],
                        preferred_element_type=F32) * invs[jb * n_pairs + pp] for pp in range(n_pairs)]
        an_ref[jb * qb:(jb + 1) * qb, :] = _rmsnorm(jnp.concatenate(outs, axis=1), ga_ref[...])
    if tail:
        kt_ref[0] = k_s[prv, tm:tm + qb, :].T
        vt_ref[0] = v_s[prv, tm:tm + qb, :].T

    o = D_ATTN + 2 * KV_W
    if proj:
        xr_s[cur] = project(o, o + D_RNN)
    if not tail:
        yr_s[cur] = project(o + D_RNN, o + 2 * D_RNN)
        return

    pad = SUBLANES
    xr = xr_s[prv]
    ext_ref[0:pad, :] = jnp.where(first_tile, 0.0, ext_ref[0:pad, :])
    ext_ref[pad:pad + tm, :] = xr
    cw = cw_ref[...]
    xc = cb_ref[...] + xr * cw[CONV_WIDTH - 1:CONV_WIDTH, :]
    for w in range(CONV_WIDTH - 1):
        sh = CONV_WIDTH - 1 - w
        xc = xc + ext_ref[pad - sh:pad - sh + tm, :] * cw[w:w + 1, :]
    ext_ref[0:pad, :] = xr[tm - pad:tm, :]
    cx_ref[0] = xr[tm - pad:tm, :]
    a, u = _lru_gates(xc, wa_ref, ba_ref, wx_ref, bx_ref, lam_ref)
    if proj:
        yr_s[cur] = project(o + D_RNN, o + 2 * D_RNN)
    _to_lane_tiles(a_scr, a)
    _to_lane_tiles(u_scr, u)
    carry = _lru_scan_tiles(a_scr, u_scr, hs_scr, jnp.where(first_tile, 0.0, h_ref[...]))
    h_ref[...] = carry
    hl_ref[0] = carry
    hseq = jnp.concatenate([hs_scr[j] for j in range(hs_scr.shape[0])], axis=1)
    rn_ref[...] = _rmsnorm(jax.nn.gelu(yr_s[prv]) * hseq, gr_ref[...])


def _front(x2d, sinks, g_mix, w_bf, gq2, gk2, ctab, s1tab, s2tab, g_attn,
           cw, cb, wa, ba, wx, bx, lam, g_rnn, batch, seq, tm):
    n = x2d.shape[0]
    nt = n // tm
    tps = seq // tm
    qb = ATTN_BLOCK
    cur = lambda i: (jnp.minimum(i, nt - 1), 0)
    tab = lambda i: (jnp.minimum(i, nt - 1) % tps, 0)
    fix = lambda i: (0, 0)
    prev = lambda i: (jnp.maximum(i - 1, 0), 0)
    per_seq = lambda i: (jnp.maximum(i - 1, 0) // tps, 0, 0)
    vec = lambda w: pl.BlockSpec((1, w), fix)
    return pl.pallas_call(
        functools.partial(_front_kernel, tiles_per_seq=tps, nt=nt),
        grid=(nt + 1,),
        in_specs=[pl.BlockSpec(memory_space=pltpu.SMEM),
                  pl.BlockSpec((tm, D_MODEL), cur), vec(D_MODEL), pl.BlockSpec((D_MODEL, D_IN), fix),
                  vec(LANES), vec(LANES),
                  pl.BlockSpec((tm, LANES), tab), pl.BlockSpec((tm, LANES), tab), pl.BlockSpec((tm, LANES), tab),
                  pl.BlockSpec((2, qb, 4 * qb), lambda i: (0, 0, 0)), vec(D_ATTN),
                  pl.BlockSpec((CONV_WIDTH, D_RNN), fix), vec(D_RNN),
                  pl.BlockSpec((D_RNN, D_RNN), fix), vec(D_RNN),
                  pl.BlockSpec((D_RNN, D_RNN), fix), vec(D_RNN), vec(D_RNN), vec(D_RNN)],
        out_specs=(pl.BlockSpec((tm, D_ATTN), prev), pl.BlockSpec((tm, D_RNN), prev),
                   pl.BlockSpec((1, KV_W, qb), per_seq), pl.BlockSpec((1, KV_W, qb), per_seq),
                   pl.BlockSpec((1, 1, D_RNN), per_seq), pl.BlockSpec((1, SUBLANES, D_RNN), per_seq)),
        out_shape=(jax.ShapeDtypeStruct((n, D_ATTN), F32), jax.ShapeDtypeStruct((n, D_RNN), F32),
                   jax.ShapeDtypeStruct((batch, KV_W, qb), F32), jax.ShapeDtypeStruct((batch, KV_W, qb), F32),
                   jax.ShapeDtypeStruct((batch, 1, D_RNN), F32),
                   jax.ShapeDtypeStruct((batch, SUBLANES, D_RNN), F32)),
        scratch_shapes=[pltpu.VMEM((2, tm, D_ATTN), F32),
                        pltpu.VMEM((2, tm + qb, KV_W), F32), pltpu.VMEM((2, tm + qb, KV_W), F32),
                        pltpu.VMEM((2, tm, D_RNN), F32), pltpu.VMEM((2, tm, D_RNN), F32),
                        pltpu.VMEM((tm + SUBLANES, D_RNN), F32), pltpu.VMEM((1, D_RNN), F32),
                        pltpu.VMEM((tm // qb * (N_HEADS // 2), qb, 4 * qb), F32),
                        pltpu.VMEM((tm // qb * (N_HEADS // 2), qb, 4 * qb), BF16),
                        pltpu.VMEM((D_RNN // LANES, tm, LANES), F32), pltpu.VMEM((D_RNN // LANES, tm, LANES), F32),
                        pltpu.VMEM((D_RNN // LANES, tm, LANES), F32)],
        compiler_params=_cparams(1),
        name="front",
    )(sinks, x2d, g_mix, w_bf, gq2, gk2, ctab, s1tab, s2tab, _band_bias(qb), g_attn,
      cw, cb, wa, ba, wx, bx, lam, g_rnn)


def _mix_route_kernel(*refs, nt):
    i = pl.program_id(0)
    hb_s, d_s = refs[16:18]

    @pl.when(i == 0)
    def _():
        hb_s[...] = jnp.zeros(hb_s.shape, BF16)
        d_s[...] = jnp.zeros(d_s.shape, I32)
        _mix_route_body(0, 1, *refs, disp=False)

    @pl.when(i == nt)
    def _():
        _mix_route_body(nt % 2, 1 - nt % 2, *refs, route=False)

    for cur in range(2):
        @pl.when((i % 2 == cur) & (i > 0) & (i < nt))
        def _():
            _mix_route_body(cur, 1 - cur, *refs)


def _mix_route_body(cur, prv, x_ref, an_ref, rn_ref, woa_ref, wor_ref, g_ref, wr2_ref, br_ref, tri_ref, low_ref,
                    x2_ref, ts_ref, dest_ref, gate_ref, n8_ref, off_ref, hb_s, d_s, *, route=True, disp=True):
    tt = x_ref.shape[0]
    tile_rows = ts_ref.shape[0]
    n_chunks = tile_rows // DISPATCH_CHUNK

    hb_prev = hb_s[prv]
    d16 = [d_s[prv, k:k + 1, :].astype(I16) for k in range(TOP_K)]
    ri = lax.broadcasted_iota(I32, (DISPATCH_CHUNK, tt), 0).astype(I16)
    one = jnp.ones((DISPATCH_CHUNK, tt), BF16)

    def dispatch(chunks):
        for c in (chunks if disp else ()):
            p = jnp.zeros((DISPATCH_CHUNK, tt), BF16)
            for d in d16:
                p = jnp.where(ri == d - jnp.int16(c * DISPATCH_CHUNK), one, p)
            ts_ref[c * DISPATCH_CHUNK:(c + 1) * DISPATCH_CHUNK, :] = _pack_bf16_pairs(
                jnp.dot(p, hb_prev, preferred_element_type=F32))

    third = -(-n_chunks // 3)
    if not route:
        dispatch(range(n_chunks))
        return

    x2 = x_ref[...] + jnp.dot(an_ref[...].astype(BF16), woa_ref[...], preferred_element_type=F32) \
        + jnp.dot(rn_ref[...].astype(BF16), wor_ref[...], preferred_element_type=F32)
    x2_ref[...] = x2
    dispatch(range(0, third))
    hn = _rmsnorm(x2, g_ref[...])

    nt = (((1,), (1,)), ((), ()))
    hb = hn.astype(BF16)
    hb_s[cur] = hb
    hmid = (hn - hb.astype(F32)).astype(BF16)
    wr2 = wr2_ref[...]
    both = lax.dot_general(wr2, hb, nt, preferred_element_type=F32)
    logits = (lax.dot_general(wr2[:N_EXPERTS], hmid, nt, preferred_element_type=F32)
              + both[N_EXPERTS:]) + both[:N_EXPERTS] + br_ref[...]
    dispatch(range(third, 2 * third))

    ie = lax.broadcasted_iota(I32, (N_EXPERTS, tt), 0).astype(F32)
    l = logits
    vals, sels = [], []
    for _ in range(TOP_K):
        m = jnp.max(l, axis=0, keepdims=True)
        idx = jnp.min(jnp.where(l == m, ie, float(N_EXPERTS)), axis=0, keepdims=True)
        sel = ie == idx
        vals.append(m)
        sels.append(sel)
        l = jnp.where(sel, NEG_BIG, l)
    es = [jnp.exp(v - vals[0]) for v in vals]
    den = es[0] + es[1] + es[2] + es[3]
    gate_ref[0] = jnp.concatenate([e / den for e in es], axis=0)
    dispatch(range(2 * third, n_chunks))

    oh = jnp.zeros((N_EXPERTS, tt), F32)
    for sel in sels:
        oh = oh + jnp.where(sel, 1.0, 0.0)
    before = jnp.dot(oh.astype(BF16), tri_ref[...], preferred_element_type=F32)
    cnt = jnp.sum(oh, axis=1, keepdims=True).astype(I32)
    n8 = ((cnt + (SUBLANES - 1)) >> 3) << 3
    n8b = jnp.broadcast_to(n8, (N_EXPERTS, LANES))
    off = jnp.dot(low_ref[...], n8b.astype(F32).astype(BF16), preferred_element_type=F32)
    n8_ref[0] = n8b
    off_ref[0] = off.astype(I32)
    base = off[:, 0:1] + before
    dests = jnp.concatenate(
        [jnp.sum(jnp.where(sel, base, 0.0), axis=0, keepdims=True).astype(I32) for sel in sels], axis=0)
    dest_ref[0] = dests
    d_s[cur, 0:TOP_K, :] = dests


def _tile_rows(tt):
    return -(-(TOP_K * tt + N_EXPERTS * (SUBLANES - 1)) // DISPATCH_CHUNK) * DISPATCH_CHUNK


def _mix_route(x2d, an, rn, woa, wor, g, wrt, br, low, tt):
    n = x2d.shape[0]
    nt = n // tt
    tile_rows = _tile_rows(tt)
    tri = jnp.triu(jnp.ones((tt, tt), BF16), k=1)
    row = lambda i: (jnp.minimum(i, nt - 1), 0)
    prev = lambda i: (jnp.maximum(i - 1, 0), 0)
    fix = lambda i: (0, 0)
    t3 = lambda i: (jnp.minimum(i, nt - 1), 0, 0)
    in_specs = [pl.BlockSpec((tt, D_MODEL), row), pl.BlockSpec((tt, D_ATTN), row),
                pl.BlockSpec((tt, D_RNN), row), pl.BlockSpec((D_ATTN, D_MODEL), fix),
                pl.BlockSpec((D_RNN, D_MODEL), fix), pl.BlockSpec((1, D_MODEL), fix),
                pl.BlockSpec((2 * N_EXPERTS, D_MODEL), fix), pl.BlockSpec((N_EXPERTS, 1), fix),
                pl.BlockSpec((tt, tt), fix), pl.BlockSpec((N_EXPERTS, N_EXPERTS), fix)]
    out_shape = (jax.ShapeDtypeStruct((n, D_MODEL), F32),
                 jax.ShapeDtypeStruct((nt * tile_rows, HALF), U32),
                 jax.ShapeDtypeStruct((nt, TOP_K, tt), I32),
                 jax.ShapeDtypeStruct((nt, TOP_K, tt), F32),
                 jax.ShapeDtypeStruct((nt, N_EXPERTS, LANES), I32),
                 jax.ShapeDtypeStruct((nt, N_EXPERTS, LANES), I32))
    out_specs = (pl.BlockSpec((tt, D_MODEL), row),
                 pl.BlockSpec((tile_rows, HALF), prev),
                 pl.BlockSpec((1, TOP_K, tt), t3), pl.BlockSpec((1, TOP_K, tt), t3),
                 pl.BlockSpec((1, N_EXPERTS, LANES), t3), pl.BlockSpec((1, N_EXPERTS, LANES), t3))
    return pl.pallas_call(
        functools.partial(_mix_route_kernel, nt=nt),
        grid=(nt + 1,),
        in_specs=in_specs,
        out_specs=out_specs,
        out_shape=out_shape,
        scratch_shapes=[pltpu.VMEM((2, tt, D_MODEL), BF16), pltpu.VMEM((2, SUBLANES, tt), I32)],
        compiler_params=_cparams(1),
        name="mix_route",
    )(x2d, an, rn, woa, wor, g, wrt, br, tri, low)


LOW_BITS = 4


def _start_piece_group(src_hbm, dst_buf, sem, pieces, nbits):
    def bit_copy(s8, d8, l8, c):
        size = 1 << c
        low = l8 & (size - 1)

        @pl.when((l8 & size) != 0)
        def _():
            pltpu.make_async_copy(src_hbm.at[pl.ds(s8 + low, size)], dst_buf.at[pl.ds(d8 + low, size)],
                                  sem).start()

    for piece in pieces:
        for c in range(min(LOW_BITS, nbits)):
            bit_copy(*piece, c)
    if nbits > LOW_BITS:
        def long_copies(_, carry):
            for piece in pieces:
                for c in range(LOW_BITS, nbits):
                    bit_copy(*piece, c)
            return carry

        any_len = functools.reduce(lambda a, b: a | b, [l8 for _, _, l8 in pieces])
        lax.fori_loop(0, jnp.where(any_len >= (1 << LOW_BITS), 1, 0), long_copies, 0)


def _start_piece(src_hbm, dst_buf, sem, s8, d8, l8, nbits):
    _start_piece_group(src_hbm, dst_buf, sem, [(s8, d8, l8)], nbits)


def _start_pieces(src_hbm, dst_buf, sem, p_lo, p_hi, psrc_ref, pdst_ref, plen_ref, nbits, unroll=1):
    static = isinstance(p_hi, int)
    assert not static or p_hi % unroll == 0
    masked = unroll > 1 and not static
    last = plen_ref.shape[0] - 1

    def body(g, carry):
        pieces = []
        for u in range(unroll):
            p = p_lo + g * unroll + u
            q = jnp.minimum(p, last) if masked else p
            l8 = jnp.where(p < p_hi, plen_ref[q], 0) if masked else plen_ref[q]
            pieces.append((psrc_ref[q], pdst_ref[q], l8))
        _start_piece_group(src_hbm, dst_buf, sem, pieces, nbits)
        return carry

    if unroll == 1 and not static:
        entry = lambda p: (psrc_ref[p], pdst_ref[p], plen_ref[p])

        def one(p, piece):
            nxt = entry(jnp.minimum(p + 1, last))
            _start_piece_group(src_hbm, dst_buf, sem, [piece], nbits)
            return nxt

        lax.fori_loop(p_lo, p_hi, one, entry(jnp.minimum(p_lo, last)))
    else:
        lax.fori_loop(0, p_hi // unroll if static else (p_hi - p_lo + (unroll - 1)) // unroll, body, 0)


def _wait_rows(src_hbm, dst_buf, sem, rows8, nbits):
    for c in range(nbits):
        size = 1 << c

        @pl.when((rows8 & size) != 0)
        def _():
            pltpu.make_async_copy(src_hbm.at[pl.ds(0, size)], dst_buf.at[pl.ds(0, size)], sem).wait()


def _moe_gmm_kernel(be_ref, rows_ref, wslot_ref, nxt_ref,
                    psa_ref, pea_ref, srca_ref, dsta_ref, lena_ref, hsrca_ref, hlena_ref,
                    psb_ref, peb_ref, srcb_ref, dstb_ref, lenb_ref, hsrcb_ref, hlenb_ref,
                    tsa_hbm, tsb_hbm, wgu_hbm, wdn_hbm, bg_ref, bu_ref, bd_ref, perm_ref,
                    ys_ref, lhs_ref, wgu_buf, wdn_buf, wg_ref, wu_ref, wd_ref, sem_ref, wsem_ref,
                    *, nbits_a, nbits_b):
    j = pl.program_id(0)
    nb = pl.num_programs(0)
    slot = j % 2
    bm = lhs_ref.shape[1] * SUBLANES

    def gather(blk, sl):
        for ts_hbm, ps, pe, src, dst, ln, hsrc, hlen, nbits in (
                (tsa_hbm, psa_ref, pea_ref, srca_ref, dsta_ref, lena_ref, hsrca_ref, hlena_ref, nbits_a),
                (tsb_hbm, psb_ref, peb_ref, srcb_ref, dstb_ref, lenb_ref, hsrcb_ref, hlenb_ref, nbits_b)):
            _start_pieces(ts_hbm, lhs_ref.at[sl], sem_ref.at[sl], ps[blk], pe[blk], src, dst, ln, nbits)
            _start_piece(ts_hbm, lhs_ref.at[sl], sem_ref.at[sl], hsrc[blk], 0, hlen[blk], nbits)

    def weight_copies(e, ws):
        return (pltpu.make_async_copy(wgu_hbm.at[e], wgu_buf.at[ws], wsem_ref.at[ws]),
                pltpu.make_async_copy(wdn_hbm.at[e], wdn_buf.at[ws], wsem_ref.at[ws]))

    @pl.when(j == 0)
    def _():
        lhs_ref[...] = jnp.zeros(lhs_ref.shape, U32)
        gather(0, 0)
        for cp in weight_copies(be_ref[0], wslot_ref[0]):
            cp.start()

    @pl.when(j + 1 < nb)
    def _():
        gather(j + 1, 1 - slot)

    @pl.when(jnp.logical_or(j == 0, be_ref[j] != be_ref[jnp.maximum(j - 1, 0)]))
    def _():
        ws = wslot_ref[j]
        for cp in weight_copies(be_ref[j], ws):
            cp.wait()
        nxt = nxt_ref[j]

        @pl.when(nxt >= 0)
        def _():
            for cp in weight_copies(nxt, 1 - ws):
                cp.start()

        perm = perm_ref[...]
        half = PERM_COLS // 2
        for c in range(2 * D_FF // PERM_COLS):
            wb = wgu_buf[ws, :, c * PERM_COLS:(c + 1) * PERM_COLS].astype(BF16)
            wp = jnp.dot(wb, perm, preferred_element_type=F32).astype(BF16)
            wg_ref[:, c * half:(c + 1) * half] = wp[:, :half]
            wu_ref[:, c * half:(c + 1) * half] = wp[:, half:]
        wd_ref[...] = wdn_buf[ws].astype(BF16)

    rows8 = rows_ref[j]
    _wait_rows(tsa_hbm, lhs_ref.at[slot], sem_ref.at[slot], rows8, (bm // SUBLANES).bit_length())

    def expert_rows(rs):
        x = _unpack_bf16_pairs(lhs_ref[slot, 0:rs.stop // SUBLANES].reshape(rs.stop, HALF))
        gate = jnp.dot(x, wg_ref[...], preferred_element_type=F32) + bg_ref[0]
        up = jnp.dot(x, wu_ref[...], preferred_element_type=F32) + bu_ref[0]
        gate = jnp.minimum(gate, SWIGLU_LIMIT)
        up = jnp.clip(up, -SWIGLU_LIMIT, SWIGLU_LIMIT)
        act = (up + 1.0) * (gate * jax.nn.sigmoid(SWIGLU_ALPHA * gate))
        y = jnp.dot(act.astype(BF16), wd_ref[...], preferred_element_type=F32) + bd_ref[0]
        ys_ref[rs, :] = _pack_bf16_pairs(y.astype(BF16).astype(F32))

    quarter = bm // 4
    for nq in range(5):
        lo8, hi8 = (nq - 1) * quarter // SUBLANES, nq * quarter // SUBLANES

        @pl.when(jnp.logical_and(rows8 > lo8, rows8 <= hi8) if nq else rows8 == 0)
        def _():
            if nq:
                expert_rows(slice(0, nq * quarter))
            if nq < 4:
                ys_ref[nq * quarter:, :] = jnp.zeros((bm - nq * quarter, HALF), U32)


def _moe_gmm(blocks, tabs_a, tabs_b, ts_a, ts_b, w_gu, w_dn, bg, bu, bd, perm, nblocks, bm, nbits_a, nbits_b):
    we = lambda j, be, *_: (be[j], 0, 0)
    grid_spec = pltpu.PrefetchScalarGridSpec(
        num_scalar_prefetch=18,
        grid=(nblocks,),
        in_specs=[pl.BlockSpec(memory_space=pl.ANY), pl.BlockSpec(memory_space=pl.ANY),
                  pl.BlockSpec(memory_space=pl.ANY), pl.BlockSpec(memory_space=pl.ANY),
                  pl.BlockSpec((1, 1, D_FF), we), pl.BlockSpec((1, 1, D_FF), we),
                  pl.BlockSpec((1, 1, D_MODEL), we),
                  pl.BlockSpec((PERM_COLS, PERM_COLS), lambda j, *_: (0, 0))],
        out_specs=pl.BlockSpec((bm, HALF), lambda j, *_: (j, 0)),
        scratch_shapes=[pltpu.VMEM((2, bm // SUBLANES, SUBLANES, HALF), U32),
                        pltpu.VMEM((2, D_MODEL, 2 * D_FF), F32), pltpu.VMEM((2, D_FF, D_MODEL), F32),
                        pltpu.VMEM((D_MODEL, D_FF), BF16), pltpu.VMEM((D_MODEL, D_FF), BF16),
                        pltpu.VMEM((D_FF, D_MODEL), BF16),
                        pltpu.SemaphoreType.DMA((2,)), pltpu.SemaphoreType.DMA((2,))],
    )
    return pl.pallas_call(
        functools.partial(_moe_gmm_kernel, nbits_a=nbits_a, nbits_b=nbits_b),
        grid_spec=grid_spec,
        out_shape=jax.ShapeDtypeStruct((nblocks * bm, HALF), U32),
        compiler_params=_cparams(1),
        name="moe_gmm",
    )(*blocks, *tabs_a, *tabs_b, ts_a.reshape(-1, SUBLANES, HALF), ts_b.reshape(-1, SUBLANES, HALF),
      w_gu, w_dn, bg, bu, bd, perm)


def _combine_kernel(psrc_ref, pdst_ref, plen_ref, tlo_ref, thi_ref, tsrc_ref, tdst_ref, tlen_ref, rows_ref,
                    ys_hbm, x2_ref, dest_ref, gate_ref, o_ref,
                    buf_ref, db_ref, gb_ref, sem_ref, *, nbits):
    i = pl.program_id(0)
    n = pl.num_programs(0)
    slot = i % 2
    tt = x2_ref.shape[0]
    tile_rows = buf_ref.shape[1] * SUBLANES

    def gather(tile, sl):
        _start_pieces(ys_hbm, buf_ref.at[sl], sem_ref.at[sl], tile * N_EXPERTS, N_EXPERTS,
                      psrc_ref, pdst_ref, plen_ref, nbits, unroll=4)
        _start_pieces(ys_hbm, buf_ref.at[sl], sem_ref.at[sl], tlo_ref[tile], thi_ref[tile],
                      tsrc_ref, tdst_ref, tlen_ref, nbits)

    @pl.when(i == 0)
    def _():
        buf_ref[...] = jnp.zeros(buf_ref.shape, U32)
        gather(0, 0)

    @pl.when(i + 1 < n)
    def _():
        gather(i + 1, 1 - slot)

    _wait_rows(ys_hbm, buf_ref.at[slot], sem_ref.at[slot], rows_ref[i], (tile_rows // SUBLANES).bit_length())

    rows = jnp.concatenate([dest_ref[0].astype(F32), gate_ref[0],
                            jnp.zeros((LANES - 2 * TOP_K, tt), F32)], axis=0)
    cols = rows.T
    for k in range(TOP_K):
        db_ref[k] = jnp.broadcast_to(cols[:, k:k + 1], (tt, DISPATCH_CHUNK)).astype(I32).astype(I16)
        gb_ref[k] = jnp.broadcast_to(cols[:, TOP_K + k:TOP_K + k + 1], (tt, DISPATCH_CHUNK)).astype(BF16)
    li = lax.broadcasted_iota(I32, (tt, DISPATCH_CHUNK), 1).astype(I16)
    gms = []
    for c in range(tile_rows // DISPATCH_CHUNK):
        lic = li + jnp.int16(c * DISPATCH_CHUNK)
        gm = jnp.zeros((tt, DISPATCH_CHUNK), BF16)
        for k in range(TOP_K):
            gm = jnp.where(lic == db_ref[k], gb_ref[k], gm)
        gms.append(gm)
    o_ref[...] = x2_ref[...] + jnp.dot(jnp.concatenate(gms, axis=1),
                                       _unpack_bf16_pairs(buf_ref[slot].reshape(tile_rows, HALF)),
                                       preferred_element_type=F32)


def _combine(tabs, ys, x2, dest, gate, tt, nbits):
    n = x2.shape[0]
    nt = n // tt
    tile_rows = _tile_rows(tt)
    grid_spec = pltpu.PrefetchScalarGridSpec(
        num_scalar_prefetch=len(tabs),
        grid=(nt,),
        in_specs=[pl.BlockSpec(memory_space=pl.ANY),
                  pl.BlockSpec((tt, D_MODEL), lambda i, *_: (i, 0)),
                  pl.BlockSpec((1, TOP_K, tt), lambda i, *_: (i, 0, 0)),
                  pl.BlockSpec((1, TOP_K, tt), lambda i, *_: (i, 0, 0))],
        out_specs=pl.BlockSpec((tt, D_MODEL), lambda i, *_: (i, 0)),
        scratch_shapes=[pltpu.VMEM((2, tile_rows // SUBLANES, SUBLANES, HALF), U32),
                        pltpu.VMEM((TOP_K, tt, DISPATCH_CHUNK), I16), pltpu.VMEM((TOP_K, tt, DISPATCH_CHUNK), BF16),
                        pltpu.SemaphoreType.DMA((2,))],
    )
    return pl.pallas_call(
        functools.partial(_combine_kernel, nbits=nbits),
        grid_spec=grid_spec,
        out_shape=jax.ShapeDtypeStruct((n, D_MODEL), F32),
        compiler_params=_cparams(1),
        name="combine",
    )(*tabs, ys.reshape(-1, SUBLANES, HALF), x2, dest, gate)


def _piece_tables(n8_a, off_a, rows_a, n8_b, off_b, bm, nblocks):
    nta = n8_a.shape[0]
    n8 = jnp.concatenate([n8_a, n8_b], axis=0)
    seg_off = jnp.concatenate([off_a, off_b], axis=0)
    n_tiles = n8.shape[0]
    tile_base = jnp.concatenate([jnp.arange(nta, dtype=I32) * rows_a, jnp.zeros((n_tiles - nta,), I32)])[:, None]
    tot = jnp.sum(n8, axis=0)
    pos0 = jnp.cumsum(n8, axis=0) - n8
    nblk = (tot + bm - 1) // bm
    cs = jnp.cumsum(nblk)
    bs = cs - nblk
    kblk = pos0 // bm
    len0 = jnp.minimum(n8, (kblk + 1) * bm - pos0)
    len1 = n8 - len0
    b0 = bs[None, :] + kblk
    src0 = tile_base + seg_off
    in_blk = pos0 - kblk * bm
    jj = jnp.arange(nblocks, dtype=I32)
    i32 = lambda v: v.astype(I32)

    def gmm_tabs(sl):
        em = lambda v: v[sl].T.reshape(-1)
        blk_em = em(b0)
        first = i32(jnp.sum(blk_em[None, :] < jj[:, None], axis=1))
        last = i32(jnp.sum(blk_em[None, :] <= jj[:, None], axis=1))
        hit = (blk_em[None, :] + 1 == jj[:, None]) & (em(len1)[None, :] > 0)
        tail_src = i32(jnp.sum(jnp.where(hit, em(src0 + len0)[None, :], 0), axis=1) // SUBLANES)
        tail_len = i32(jnp.sum(jnp.where(hit, em(len1)[None, :], 0), axis=1) // SUBLANES)
        return (first, last, i32(em(src0) // SUBLANES), i32(em(in_blk) // SUBLANES), i32(em(len0) // SUBLANES),
                tail_src, tail_len)

    def comb_tabs(sl):
        tm = lambda v: v[sl].reshape(-1)
        has_tail = len1[sl] > 0
        cnt = jnp.sum(has_tail, axis=1)
        lo = jnp.cumsum(cnt) - cnt
        slot_ = lo[:, None] + jnp.cumsum(has_tail, axis=1) - has_tail
        hit = (slot_.reshape(-1)[None, :] == jj[:, None]) & has_tail.reshape(-1)[None, :]
        pick = lambda v: i32(jnp.sum(jnp.where(hit, tm(v)[None, :], 0), axis=1) // SUBLANES)
        return (i32(tm(b0 * bm + in_blk) // SUBLANES), i32(tm(seg_off) // SUBLANES), i32(tm(len0) // SUBLANES),
                i32(lo), i32(lo + cnt), pick((b0 + 1) * bm), pick(seg_off + len0), pick(len1),
                i32(jnp.sum(n8[sl], axis=1) // SUBLANES))

    count_le = lambda v: jnp.sum(cs[None, :] <= v[:, None], axis=1)
    n_active = cs[-1]
    e_last = count_le(jnp.maximum(n_active - 1, 0).reshape(1))[0]
    block_e = jnp.minimum(count_le(jj), e_last).astype(I32)
    ee = jnp.arange(N_EXPERTS, dtype=I32)
    mine = (jj[:, None] >= bs[None, :]) & (jj[:, None] < cs[None, :])
    left = jnp.clip(tot[None, :] - (jj[:, None] - bs[None, :]) * bm, 0, bm)
    rows8 = (jnp.sum(jnp.where(mine, left, 0), axis=1) // SUBLANES).astype(I32)
    has = nblk > 0
    run = jnp.cumsum(has.astype(I32)) - 1
    later = (ee[None, :] > ee[:, None]) & has[None, :]
    nxt_e = jnp.min(jnp.where(later, ee[None, :], N_EXPERTS), axis=1)
    nxt_e = jnp.where(nxt_e == N_EXPERTS, -1, nxt_e)
    own = block_e[:, None] == ee[None, :]
    wslot = (jnp.sum(jnp.where(own, run[None, :], 0), axis=1) % 2).astype(I32)
    nxt = jnp.sum(jnp.where(own, nxt_e[None, :], 0), axis=1).astype(I32)
    a, b = slice(0, nta), slice(nta, n_tiles)
    return (block_e, rows8, wslot, nxt), gmm_tabs(a), gmm_tabs(b), comb_tabs(a), comb_tabs(b)


def _block_diag(w):
    nb, bi, bo = w.shape
    eye = jnp.eye(nb, dtype=w.dtype)
    return (eye[:, None, :, None] * w[:, :, None, :]).reshape(nb * bi, nb * bo)


def _step(x_prompt, x_sample, cache_k, cache_v, state_conv, state_h, g_mix_norm, w_in, g_q_norm, g_k_norm,
          attn_sinks, conv_w, conv_b, w_lru_a, b_lru_a, w_lru_x, b_lru_x, lru_lambda, g_attn_out, g_rnn_out,
          w_out, g_ffn_norm, w_router, b_router, w_gate_up, b_gate_up, w_down, b_down,
          *, tm, tt, bm, past_len):
    B, S, D = x_prompt.shape
    NS = x_sample.shape[0]
    assert x_sample.shape[1] == 1 and D == D_MODEL
    assert (B * S) % tt == 0 and S % tm == 0 and tm % ATTN_BLOCK == 0
    assert NS % SUBLANES == 0 and tt <= bm
    assert tt % SUBLANES == 0 and NS <= bm
    n_pt = (B * S) // tt
    total_rows = TOP_K * (B * S + NS) + (n_pt + 1) * N_EXPERTS * (SUBLANES - 1)
    nblocks = -(-total_rows // bm) + N_EXPERTS
    nbits_p = (tt // SUBLANES).bit_length()
    nbits_s = (NS // SUBLANES).bit_length()

    l = 0
    row = lambda v: v[l].reshape(1, -1)
    w_in_bf = w_in[l].astype(BF16)
    gq2 = jnp.tile(g_q_norm[l], 2).reshape(1, LANES)
    gk2 = jnp.tile(g_k_norm[l], 2).reshape(1, LANES)
    wa = _block_diag(w_lru_a[l]).astype(BF16)
    wx = _block_diag(w_lru_x[l]).astype(BF16)
    ba = b_lru_a[l].reshape(1, D_RNN)
    bx = b_lru_x[l].reshape(1, D_RNN)
    wo = w_out[l].astype(BF16)
    woa, wor = wo[:D_ATTN], wo[D_ATTN:]
    wr = w_router[l].T
    wr_hi = wr.astype(BF16)
    wrt = jnp.concatenate([wr_hi, (wr - wr_hi.astype(F32)).astype(BF16)], axis=0)
    br = b_router[l].reshape(N_EXPERTS, 1)
    low = jnp.tril(jnp.ones((N_EXPERTS, N_EXPERTS), BF16), k=-1)
    bgu = b_gate_up[l].reshape(N_EXPERTS, D_FF, 2)
    bg = bgu[:, :, 0].reshape(N_EXPERTS, 1, D_FF)
    bu = bgu[:, :, 1].reshape(N_EXPERTS, 1, D_FF)
    bd = b_down[l].reshape(N_EXPERTS, 1, D_MODEL)
    half = PERM_COLS // 2
    pr = jnp.arange(PERM_COLS)
    perm = (pr[None, :] == jnp.where(pr % 2 == 0, pr // 2, half + pr // 2)[:, None]).astype(BF16)
    sinks = attn_sinks[l]

    ctab, s1tab, s2tab = _rope_tables(jnp.arange(S))
    an, rn, kt_p, vt_p, h_last_p, xr_tail = _front(
        x_prompt.reshape(B * S, D), sinks, row(g_mix_norm), w_in_bf, gq2, gk2, ctab, s1tab, s2tab,
        row(g_attn_out), conv_w[l], row(conv_b), wa, ba, wx, bx, row(lru_lambda), row(g_rnn_out), B, S, tm)
    x2_p, ts_p, dest_p, gate_p, n8_p, off_p = _mix_route(
        x_prompt.reshape(B * S, D), an, rn, woa, wor, row(g_ffn_norm), wrt, br, low, tt)

    cs_tab = _rope_tables(jnp.full((NS,), past_len, I32))
    q_s, k_s, v_s, xr_s, yr_s = _in_proj(x_sample.reshape(NS, D), row(g_mix_norm), w_in_bf, gq2, gk2,
                                         *cs_tab, NS)
    to_rows = lambda c: jnp.transpose(c, (0, 2, 3, 1)).reshape(NS * KV_W, WINDOW)
    from_rows = lambda c, n: jnp.transpose(c.reshape(n, N_KV_HEADS, HEAD_DIM, WINDOW), (0, 3, 1, 2))[None]
    an_s, kt_s, vt_s = _attn_sample(q_s, k_s, v_s, to_rows(cache_k[l]), to_rows(cache_v[l]), sinks,
                                    row(g_attn_out), min(NS, SAMPLE_ATTN_SEQS))
    rn_s, h_last_s, hist_s = _rnn_sample(xr_s, yr_s, jnp.transpose(state_conv[l], (1, 0, 2)), state_h[l],
                                         conv_w[l], row(conv_b), wa, ba, wx, bx, row(lru_lambda),
                                         row(g_rnn_out))
    x2_s, ts_s, dest_s, gate_s, n8_s, off_s = _mix_route(
        x_sample.reshape(NS, D), an_s, rn_s, woa, wor, row(g_ffn_norm), wrt, br, low, NS)

    blocks, gmm_p, gmm_s, comb_p, comb_s = _piece_tables(
        n8_p[:, :, 0], off_p[:, :, 0], _tile_rows(tt), n8_s[:, :, 0], off_s[:, :, 0], bm, nblocks)
    ys = _moe_gmm(blocks, gmm_p, gmm_s, ts_p, ts_s, w_gate_up[l], w_down[l], bg, bu, bd, perm,
                  nblocks, bm, nbits_p, nbits_s)
    y_p = _combine(comb_p, ys, x2_p, dest_p, gate_p, tt, nbits_p)
    y_s = _combine(comb_s, ys, x2_s, dest_s, gate_s, NS, nbits_s)

    cp = xr_tail[:, SUBLANES - (CONV_WIDTH - 1):]
    return (y_p.reshape(B, S, D), y_s.reshape(NS, 1, D),
            from_rows(kt_p, B), from_rows(vt_p, B), cp[None], h_last_p.reshape(1, B, D_RNN),
            from_rows(kt_s, NS), from_rows(vt_s, NS), jnp.transpose(hist_s, (1, 0, 2))[None], h_last_s[None])


def kernel(x_prompt, x_sample, cache_k, cache_v, state_conv, state_h, g_mix_norm, w_in, g_q_norm, g_k_norm, attn_sinks, conv_w, conv_b, w_lru_a, b_lru_a, w_lru_x, b_lru_x, lru_lambda, g_attn_out, g_rnn_out, w_out, g_ffn_norm, w_router, b_router, w_gate_up, b_gate_up, w_down, b_down):
    return _step(x_prompt, x_sample, cache_k, cache_v, state_conv, state_h, g_mix_norm, w_in, g_q_norm,
                 g_k_norm, attn_sinks, conv_w, conv_b, w_lru_a, b_lru_a, w_lru_x, b_lru_x, lru_lambda,
                 g_attn_out, g_rnn_out, w_out, g_ffn_norm, w_router, b_router, w_gate_up, b_gate_up,
                 w_down, b_down, tm=512, tt=512, bm=MOE_BLOCK_ROWS, past_len=PAST_LEN)
```
